```python
import math
import jax, jax.numpy as jnp
from jax import lax
import numpy as np

D_MODEL = 4096
BATCH = 8
SEQ = 4096
DEPTH = 2

ATTN_PATTERNS = ((128, 1), (512, 4), (2048, 16))
N_GROUPS_ATTN = 3
HEADS_PER_GROUP = 16
HEAD_DIM = 128
D_ATTN = HEADS_PER_GROUP * HEAD_DIM
ATTN_BLOCK = 128
QKV_COLS = N_GROUPS_ATTN * 3 * D_ATTN
IN_ATTN_COLS = QKV_COLS + D_ATTN

NUM_BUCKETS = 32
MAX_DISTANCE = 2048
N_BIAS_HEADS = N_GROUPS_ATTN * HEADS_PER_GROUP

EXPAND = 2
D_INNER = EXPAND * D_MODEL
SSM_HEAD_DIM = 64
SSM_HEADS = D_INNER // SSM_HEAD_DIM
SSM_GROUPS = 8
HEADS_PER_SSM_GROUP = SSM_HEADS // SSM_GROUPS
D_STATE = 128
CONV_WIDTH = 4
CONV_DIM = D_INNER + 2 * SSM_GROUPS * D_STATE
IN_SSM_COLS = D_INNER + CONV_DIM + SSM_HEADS
CHUNK = 128

N_MIXERS = 2
N_ATTN_LAYERS = (DEPTH + 1) // 2
N_SSM_LAYERS = DEPTH // 2
DEEPNORM_ALPHA = (2 * DEPTH) ** 0.25
DEEPNORM_BETA = (8 * DEPTH) ** -0.25
LN_EPS = 1e-5
RMS_EPS = 1e-5
NEG_INF = -1e30

kernel_name = 'hybrid_dilated_attn_mamba2_deepnorm'


def t5_causal_bucket(dist):
    max_exact = NUM_BUCKETS // 2
    d_f = jnp.maximum(dist, 1).astype(jnp.float32)
    large = max_exact + (jnp.log(d_f / max_exact) / math.log(MAX_DISTANCE / max_exact)
                         * (NUM_BUCKETS - max_exact)).astype(jnp.int32)
    large = jnp.minimum(large, NUM_BUCKETS - 1)
    return jnp.where(dist < max_exact, dist, large)


def dilated_group_attention(q, k, v, bias_table, window, dilation):
    b, s, h, dh = q.shape
    span = window // dilation
    L = s // dilation
    nb = -(-L // ATTN_BLOCK)
    lp = nb * ATTN_BLOCK

    def to_sub(t):
        t = t.reshape(b, L, dilation, h, dh).transpose(0, 2, 1, 3, 4)
        t = jnp.pad(t, ((0, 0), (0, 0), (0, lp - L), (0, 0), (0, 0)))
        return t.reshape(b, dilation, nb, ATTN_BLOCK, h, dh)

    def with_prev(t):
        prev = jnp.pad(t, ((0, 0), (0, 0), (1, 0), (0, 0), (0, 0), (0, 0)))[:, :, :-1]
        return jnp.concatenate([prev, t], axis=3)

    qb = to_sub(q)
    kk = with_prev(to_sub(k))
    vv = with_prev(to_sub(v))

    qi = jnp.arange(ATTN_BLOCK)[:, None]
    ki = jnp.arange(2 * ATTN_BLOCK)[None, :]
    delta = ATTN_BLOCK + qi - ki
    band = (delta >= 0) & (delta <= span)
    not_first = (jnp.arange(nb) > 0)[:, None, None]
    valid = band[None] & (not_first | (ki >= ATTN_BLOCK)[None])
    bucket = t5_causal_bucket(jnp.clip(delta, 0, None) * dilation)
    bias = bias_table.astype(jnp.float32)[bucket].transpose(2, 0, 1)

    logits = jnp.einsum('brnqhd,brnkhd->brnhqk', qb, kk).astype(jnp.float32)
    logits = logits * (dh ** -0.5) + bias[None, None, None]
    logits = jnp.where(valid[None, None, :, None], logits, NEG_INF)
    m = jnp.max(logits, axis=-1, keepdims=True)
    p = jnp.exp(logits - m)
    denom = jnp.sum(p, axis=-1, keepdims=True)
    o = jnp.einsum('brnhqk,brnkhd->brnqhd', p / denom, vv.astype(jnp.float32))
    lse = (m + jnp.log(denom))[..., 0]

    o = o.reshape(b, dilation, lp, h, dh)[:, :, :L].transpose(0, 2, 1, 3, 4).reshape(b, s, h, dh)
    lse = lse.transpose(0, 1, 2, 4, 3).reshape(b, dilation, lp, h)[:, :, :L]
    lse = lse.transpose(0, 2, 1, 3).reshape(b, s, h)
    return o, lse


def dilated_attention_mixer(x, w_in, w_out, rel_bias):
    b, s, _ = x.shape
    proj = jnp.einsum('bsd,de->bse', x, w_in)
    qkv = proj[..., :QKV_COLS].reshape(b, s, N_GROUPS_ATTN, 3, HEADS_PER_GROUP, HEAD_DIM)
    gate = proj[..., QKV_COLS:]
    outs, lses = [], []
    for g, (window, dilation) in enumerate(ATTN_PATTERNS):
        o, lse = dilated_group_attention(
            qkv[:, :, g, 0], qkv[:, :, g, 1], qkv[:, :, g, 2],
            rel_bias[:, g * HEADS_PER_GROUP:(g + 1) * HEADS_PER_GROUP], window, dilation)
        outs.append(o)
        lses.append(lse)
    w = jax.nn.softmax(jnp.stack(lses), axis=0)
    o = jnp.einsum('gbsh,gbshd->bshd', w, jnp.stack(outs)).reshape(b, s, D_ATTN)
    y = o.astype(x.dtype) * jax.nn.silu(gate)
    return jnp.einsum('bse,ed->bsd', y, w_out)


def ssd_chunked_scan(xs, dt, a, bm, cm):
    b, s, g, hpg, p = xs.shape
    n = bm.shape[-1]
    nc = s // CHUNK

    def chunks(t):
        return t.reshape((b, nc, CHUNK) + t.shape[2:]).swapaxes(0, 1)

    causal = jnp.tril(jnp.ones((CHUNK, CHUNK), dtype=bool))

    def step(state, inp):
        xc, dtc, bc, cc = inp
        bc = bc.astype(jnp.float32)
        cc = cc.astype(jnp.float32)
        a_cum = jnp.cumsum(dtc * a, axis=1)
        seg = a_cum[:, :, None] - a_cum[:, None, :]
        decay = jnp.exp(jnp.where(causal[None, :, :, None, None], seg, -jnp.inf))
        xdt = xc.astype(jnp.float32) * dtc[..., None]
        cb = jnp.einsum('blgn,bsgn->blsg', cc, bc)
        y_diag = jnp.einsum('blsg,blsgh,bsghp->blghp', cb, decay, xdt)
        y_off = jnp.einsum('blgn,bghpn,blgh->blghp', cc, state, jnp.exp(a_cum))
        to_end = jnp.exp(a_cum[:, -1:] - a_cum)
        new_state = (state * jnp.exp(a_cum[:, -1])[..., None, None]
                     + jnp.einsum('bsgn,bsgh,bsghp->bghpn', bc, to_end, xdt))
        return new_state, y_diag + y_off

    state0 = jnp.zeros((b, g, hpg, p, n), jnp.float32)
    _, y = lax.scan(step, state0, (chunks(xs), chunks(dt), chunks(bm), chunks(cm)))
    return y.swapaxes(0, 1).reshape(b, s, g, hpg, p)


def ssd_mixer(x, w_in, conv_w, conv_b, dt_bias, a_log, d_skip, norm_w, w_out):
    b, s, _ = x.shape
    proj = jnp.einsum('bsd,de->bse', x, w_in)
    z = proj[..., :D_INNER]
    xbc = proj[..., D_INNER:D_INNER + CONV_DIM]
    dt_raw = proj[..., D_INNER + CONV_DIM:]
    xbc = lax.conv_general_dilated(
        xbc, conv_w[:, None, :], window_strides=(1,), padding=[(CONV_WIDTH - 1, 0)],
        dimension_numbers=('NWC', 'WIO', 'NWC'), feature_group_count=CONV_DIM)
    xbc = jax.nn.silu(xbc + conv_b)
    gn = SSM_GROUPS * D_STATE
    xs = xbc[..., :D_INNER].reshape(b, s, SSM_GROUPS, HEADS_PER_SSM_GROUP, SSM_HEAD_DIM)
    bm = xbc[..., D_INNER:D_INNER + gn].reshape(b, s, SSM_GROUPS, D_STATE)
    cm = xbc[..., D_INNER + gn:].reshape(b, s, SSM_GROUPS, D_STATE)
    dt = jax.nn.softplus(dt_raw.astype(jnp.float32) + dt_bias.astype(jnp.float32))
    dt = dt.reshape(b, s, SSM_GROUPS, HEADS_PER_SSM_GROUP)
    a = -jnp.exp(a_log.astype(jnp.float32)).reshape(SSM_GROUPS, HEADS_PER_SSM_GROUP)
    y = ssd_chunked_scan(xs, dt, a, bm, cm)
    y = y + d_skip.astype(jnp.float32).reshape(SSM_GROUPS, HEADS_PER_SSM_GROUP)[:, :, None] * xs
    y = y.reshape(b, s, D_INNER) * jax.nn.silu(z.astype(jnp.float32))
    yg = y.reshape(b, s, SSM_GROUPS, D_INNER // SSM_GROUPS)
    yg = yg * lax.rsqrt(jnp.mean(yg * yg, axis=-1, keepdims=True) + RMS_EPS)
    y = yg.reshape(b, s, D_INNER) * norm_w.astype(jnp.float32)
    return jnp.einsum('bse,ed->bsd', y.astype(x.dtype), w_out)


def layer_norm(x, g, b):
    xf = x.astype(jnp.float32)
    mu = jnp.mean(xf, axis=-1, keepdims=True)
    var = jnp.mean(jnp.square(xf - mu), axis=-1, keepdims=True)
    return ((xf - mu) * lax.rsqrt(var + LN_EPS) * g.astype(jnp.float32)
            + b.astype(jnp.float32)).astype(x.dtype)


def _fwd_setup_inputs(seed: int = 0) -> dict:
    key = jax.random.key(seed)
    ks = jax.random.split(key, 15)
    f32 = jnp.float32
    nrm = jax.random.normal
    x = nrm(ks[0], (BATCH, SEQ, D_MODEL), f32)
    w_in_attn = nrm(ks[1], (N_ATTN_LAYERS, D_MODEL, IN_ATTN_COLS), f32) * D_MODEL ** -0.5
    w_out_attn = nrm(ks[2], (N_ATTN_LAYERS, D_ATTN, D_MODEL), f32) * (D_ATTN ** -0.5 * DEEPNORM_BETA)
    rel_bias = nrm(ks[3], (NUM_BUCKETS, N_BIAS_HEADS), f32) * 0.5
    w_in_ssm = nrm(ks[4], (N_SSM_LAYERS, D_MODEL, IN_SSM_COLS), f32) * D_MODEL ** -0.5
    conv_w = nrm(ks[5], (N_SSM_LAYERS, CONV_WIDTH, CONV_DIM), f32) * CONV_WIDTH ** -0.5
    conv_b = nrm(ks[6], (N_SSM_LAYERS, CONV_DIM), f32) * 0.02
    dt0 = jnp.exp(jax.random.uniform(ks[7], (N_SSM_LAYERS, SSM_HEADS), f32,
                                     minval=math.log(1e-3), maxval=math.log(1e-1)))
    dt_bias = dt0 + jnp.log(-jnp.expm1(-dt0))
    a_log = jnp.log(jax.random.uniform(ks[8], (N_SSM_LAYERS, SSM_HEADS), f32, minval=1.0, maxval=16.0))
    d_skip = 1.0 + 0.1 * nrm(ks[9], (N_SSM_LAYERS, SSM_HEADS), f32)
    ssm_norm_w = 1.0 + 0.02 * nrm(ks[10], (N_SSM_LAYERS, D_INNER), f32)
    w_out_ssm = nrm(ks[11], (N_SSM_LAYERS, D_INNER, D_MODEL), f32) * (D_INNER ** -0.5 * DEEPNORM_BETA)
    ln_g = 1.0 + 0.02 * nrm(ks[12], (DEPTH, D_MODEL), f32)
    ln_b = 0.02 * nrm(ks[13], (DEPTH, D_MODEL), f32)
    return {'x': x, 'w_in_attn': w_in_attn, 'w_out_attn': w_out_attn, 'rel_bias': rel_bias,
            'w_in_ssm': w_in_ssm, 'conv_w': conv_w, 'conv_b': conv_b, 'dt_bias': dt_bias,
            'a_log': a_log, 'd_skip': d_skip, 'ssm_norm_w': ssm_norm_w, 'w_out_ssm': w_out_ssm,
            'ln_g': ln_g, 'ln_b': ln_b}


def _fwd_reference(x, w_in_attn, w_out_attn, rel_bias, w_in_ssm, conv_w, conv_b, dt_bias,
              a_log, d_skip, ssm_norm_w, w_out_ssm, ln_g, ln_b):
    for i in range(DEPTH):
        j = i // N_MIXERS
        if i % N_MIXERS == 0:
            h = dilated_attention_mixer(x, w_in_attn[j], w_out_attn[j], rel_bias)
        else:
            h = ssd_mixer(x, w_in_ssm[j], conv_w[j], conv_b[j], dt_bias[j], a_log[j],
                          d_skip[j], ssm_norm_w[j], w_out_ssm[j])
        x = layer_norm(DEEPNORM_ALPHA * x + h, ln_g[i], ln_b[i])
    return x


import jax as _jax
import jax.numpy as _jnp

TWIN_FORMAT = 'train_step'
FWD_PARAMS = ['x', 'w_in_attn', 'w_out_attn', 'rel_bias', 'w_in_ssm', 'conv_w', 'conv_b', 'dt_bias', 'a_log', 'd_skip', 'ssm_norm_w', 'w_out_ssm', 'ln_g', 'ln_b']
TWIN_WEIGHTS = ['w_in_attn', 'w_out_attn', 'rel_bias', 'w_in_ssm', 'conv_w', 'conv_b', 'dt_bias', 'a_log', 'd_skip', 'ssm_norm_w', 'w_out_ssm', 'ln_g', 'ln_b']
TWIN_DIFF_INPUT = 'x'
TWIN_INPUTS = ['x', 'w_in_attn', 'w_out_attn', 'rel_bias', 'w_in_ssm', 'conv_w', 'conv_b', 'dt_bias', 'a_log', 'd_skip', 'ssm_norm_w', 'w_out_ssm', 'ln_g', 'ln_b', 'loss_target', 'm_w_in_attn', 'm_w_out_attn', 'm_rel_bias', 'm_w_in_ssm', 'm_conv_w', 'm_conv_b', 'm_dt_bias', 'm_a_log', 'm_d_skip', 'm_ssm_norm_w', 'm_w_out_ssm', 'm_ln_g', 'm_ln_b', 'v_w_in_attn', 'v_w_out_attn', 'v_rel_bias', 'v_w_in_ssm', 'v_conv_w', 'v_conv_b', 'v_dt_bias', 'v_a_log', 'v_d_skip', 'v_ssm_norm_w', 'v_w_out_ssm', 'v_ln_g', 'v_ln_b']
TWIN_OUTPUTS = ['loss', 'grad_x', 'grad_w_in_attn', 'grad_w_out_attn', 'grad_rel_bias', 'grad_w_in_ssm', 'grad_conv_w', 'grad_conv_b', 'grad_dt_bias', 'grad_a_log', 'grad_d_skip', 'grad_ssm_norm_w', 'grad_w_out_ssm', 'grad_ln_g', 'grad_ln_b', 'delta_w_in_attn', 'delta_w_out_attn', 'delta_rel_bias', 'delta_w_in_ssm', 'delta_conv_w', 'delta_conv_b', 'delta_dt_bias', 'delta_a_log', 'delta_d_skip', 'delta_ssm_norm_w', 'delta_w_out_ssm', 'delta_ln_g', 'delta_ln_b', 'new_m_w_in_attn', 'new_m_w_out_attn', 'new_m_rel_bias', 'new_m_w_in_ssm', 'new_m_conv_w', 'new_m_conv_b', 'new_m_dt_bias', 'new_m_a_log', 'new_m_d_skip', 'new_m_ssm_norm_w', 'new_m_w_out_ssm', 'new_m_ln_g', 'new_m_ln_b', 'new_v_w_in_attn', 'new_v_w_out_attn', 'new_v_rel_bias', 'new_v_w_in_ssm', 'new_v_conv_w', 'new_v_conv_b', 'new_v_dt_bias', 'new_v_a_log', 'new_v_d_skip', 'new_v_ssm_norm_w', 'new_v_w_out_ssm', 'new_v_ln_g', 'new_v_ln_b']
TWIN_LEAF_KINDS = {'loss': 'loss', 'grad_x': 'grad_x', 'grad_w_in_attn': 'grad_w', 'grad_w_out_attn': 'grad_w', 'grad_rel_bias': 'grad_w', 'grad_w_in_ssm': 'grad_w', 'grad_conv_w': 'grad_w', 'grad_conv_b': 'grad_w', 'grad_dt_bias': 'grad_w', 'grad_a_log': 'grad_w', 'grad_d_skip': 'grad_w', 'grad_ssm_norm_w': 'grad_w', 'grad_w_out_ssm': 'grad_w', 'grad_ln_g': 'grad_w', 'grad_ln_b': 'grad_w', 'delta_w_in_attn': 'delta_w', 'delta_w_out_attn': 'delta_w', 'delta_rel_bias': 'delta_w', 'delta_w_in_ssm': 'delta_w', 'delta_conv_w': 'delta_w', 'delta_conv_b': 'delta_w', 'delta_dt_bias': 'delta_w', 'delta_a_log': 'delta_w', 'delta_d_skip': 'delta_w', 'delta_ssm_norm_w': 'delta_w', 'delta_w_out_ssm': 'delta_w', 'delta_ln_g': 'delta_w', 'delta_ln_b': 'delta_w', 'new_m_w_in_attn': 'new_m', 'new_m_w_out_attn': 'new_m', 'new_m_rel_bias': 'new_m', 'new_m_w_in_ssm': 'new_m', 'new_m_conv_w': 'new_m', 'new_m_conv_b': 'new_m', 'new_m_dt_bias': 'new_m', 'new_m_a_log': 'new_m', 'new_m_d_skip': 'new_m', 'new_m_ssm_norm_w': 'new_m', 'new_m_w_out_ssm': 'new_m', 'new_m_ln_g': 'new_m', 'new_m_ln_b': 'new_m', 'new_v_w_in_attn': 'new_v', 'new_v_w_out_attn': 'new_v', 'new_v_rel_bias': 'new_v', 'new_v_w_in_ssm': 'new_v', 'new_v_conv_w': 'new_v', 'new_v_conv_b': 'new_v', 'new_v_dt_bias': 'new_v', 'new_v_a_log': 'new_v', 'new_v_d_skip': 'new_v', 'new_v_ssm_norm_w': 'new_v', 'new_v_w_out_ssm': 'new_v', 'new_v_ln_g': 'new_v', 'new_v_ln_b': 'new_v'}


def _forward(args):
    return _fwd_reference(*[args[k] for k in FWD_PARAMS])


def _output_shape():
    out = _jax.eval_shape(lambda: _forward(_fwd_setup_inputs(0)))
    return out.shape, out.dtype

N_MICROBATCH = 1
ADAM_LR = 0.001
ADAM_B1 = 0.9
ADAM_B2 = 0.999
ADAM_EPS = 1e-08
ADAM_WD = 0.01
ADAM_STEP = 10
PER_EXAMPLE_BATCH_AXIS = {'x': 0, 'loss_target': 0}
SHARED_INPUTS = []
_WEIGHT_DTYPES = {'w_in_attn': _jnp.float32, 'w_out_attn': _jnp.float32, 'rel_bias': _jnp.float32, 'w_in_ssm': _jnp.float32, 'conv_w': _jnp.float32, 'conv_b': _jnp.float32, 'dt_bias': _jnp.float32, 'a_log': _jnp.float32, 'd_skip': _jnp.float32, 'ssm_norm_w': _jnp.float32, 'w_out_ssm': _jnp.float32, 'ln_g': _jnp.float32, 'ln_b': _jnp.float32}
MOMENT_SCALE = {'w_in_attn': 1.694269e-03, 'w_out_attn': 4.035192e-03, 'rel_bias': 2.704425e-03, 'w_in_ssm': 1.032059e-02, 'conv_w': 9.921432e-03, 'conv_b': 1.563445e-02, 'dt_bias': 2.216004e-02, 'a_log': 5.128855e-02, 'd_skip': 6.159520e-02, 'ssm_norm_w': 1.098580e-02, 'w_out_ssm': 3.153780e-02, 'ln_g': 5.651735e+00, 'ln_b': 3.499356e-01}


def _to_microbatches(a, axis):
    t = _jnp.moveaxis(a, axis, 0)
    t = t.reshape((N_MICROBATCH, t.shape[0] // N_MICROBATCH) + t.shape[1:])
    return _jnp.moveaxis(t, 1, axis + 1)


def setup_inputs(seed: int = 0) -> dict:
    inp = _fwd_setup_inputs(seed)
    key = _jax.random.fold_in(_jax.random.key(seed), 7919)
    shape, _ = _output_shape()
    out = dict(inp)
    out["loss_target"] = _jax.random.normal(_jax.random.fold_in(key, 0), shape, _jnp.float32)
    for i, name in enumerate(TWIN_WEIGHTS):
        w = inp[name].astype(_jnp.float32)
        if MOMENT_SCALE is None:
            s = _jnp.sqrt(_jnp.mean(_jnp.square(w)) + 1e-30)
        else:
            s = MOMENT_SCALE[name]
        km, kv = _jax.random.split(_jax.random.fold_in(key, i + 1))
        out[name] = w
        out["m_" + name] = s * _jax.random.normal(km, w.shape, _jnp.float32)
        out["v_" + name] = (s * s) * _jax.random.uniform(kv, w.shape, _jnp.float32, 0.5, 1.5)
    if N_MICROBATCH > 1:
        for name, axis in PER_EXAMPLE_BATCH_AXIS.items():
            out[name] = _to_microbatches(out[name], axis)
    return {'x': out['x'], 'w_in_attn': out['w_in_attn'], 'w_out_attn': out['w_out_attn'], 'rel_bias': out['rel_bias'], 'w_in_ssm': out['w_in_ssm'], 'conv_w': out['conv_w'], 'conv_b': out['conv_b'], 'dt_bias': out['dt_bias'], 'a_log': out['a_log'], 'd_skip': out['d_skip'], 'ssm_norm_w': out['ssm_norm_w'], 'w_out_ssm': out['w_out_ssm'], 'ln_g': out['ln_g'], 'ln_b': out['ln_b'], 'loss_target': out['loss_target'], 'm_w_in_attn': out['m_w_in_attn'], 'm_w_out_attn': out['m_w_out_attn'], 'm_rel_bias': out['m_rel_bias'], 'm_w_in_ssm': out['m_w_in_ssm'], 'm_conv_w': out['m_conv_w'], 'm_conv_b': out['m_conv_b'], 'm_dt_bias': out['m_dt_bias'], 'm_a_log': out['m_a_log'], 'm_d_skip': out['m_d_skip'], 'm_ssm_norm_w': out['m_ssm_norm_w'], 'm_w_out_ssm': out['m_w_out_ssm'], 'm_ln_g': out['m_ln_g'], 'm_ln_b': out['m_ln_b'], 'v_w_in_attn': out['v_w_in_attn'], 'v_w_out_attn': out['v_w_out_attn'], 'v_rel_bias': out['v_rel_bias'], 'v_w_in_ssm': out['v_w_in_ssm'], 'v_conv_w': out['v_conv_w'], 'v_conv_b': out['v_conv_b'], 'v_dt_bias': out['v_dt_bias'], 'v_a_log': out['v_a_log'], 'v_d_skip': out['v_d_skip'], 'v_ssm_norm_w': out['v_ssm_norm_w'], 'v_w_out_ssm': out['v_w_out_ssm'], 'v_ln_g': out['v_ln_g'], 'v_ln_b': out['v_ln_b']}


def _loss(weights, diff, rest, loss_target):
    with _jax.named_scope("forward"):
        args = {**rest, TWIN_DIFF_INPUT: diff, **{k: w.astype(_WEIGHT_DTYPES[k]) for k, w in weights.items()}}
        y = _forward(args)
    with _jax.named_scope("loss_head"):
        err = _jnp.square(y.astype(_jnp.float32) - loss_target)
        return 0.5 * _jnp.sum(_jnp.mean(err, axis=-1)) if err.ndim else 0.5 * err


def _adamw(w, g, m, v):
    m = ADAM_B1 * m + (1.0 - ADAM_B1) * g
    v = ADAM_B2 * v + (1.0 - ADAM_B2) * _jnp.square(g)
    m_hat = m / (1.0 - ADAM_B1 ** ADAM_STEP)
    v_hat = v / (1.0 - ADAM_B2 ** ADAM_STEP)
    delta = -ADAM_LR * (m_hat / (_jnp.sqrt(v_hat) + ADAM_EPS) + ADAM_WD * w)
    return delta, m, v


def reference(x, w_in_attn, w_out_attn, rel_bias, w_in_ssm, conv_w, conv_b, dt_bias, a_log, d_skip, ssm_norm_w, w_out_ssm, ln_g, ln_b, loss_target, m_w_in_attn, m_w_out_attn, m_rel_bias, m_w_in_ssm, m_conv_w, m_conv_b, m_dt_bias, m_a_log, m_d_skip, m_ssm_norm_w, m_w_out_ssm, m_ln_g, m_ln_b, v_w_in_attn, v_w_out_attn, v_rel_bias, v_w_in_ssm, v_conv_w, v_conv_b, v_dt_bias, v_a_log, v_d_skip, v_ssm_norm_w, v_w_out_ssm, v_ln_g, v_ln_b):
    given = dict(x=x, w_in_attn=w_in_attn, w_out_attn=w_out_attn, rel_bias=rel_bias, w_in_ssm=w_in_ssm, conv_w=conv_w, conv_b=conv_b, dt_bias=dt_bias, a_log=a_log, d_skip=d_skip, ssm_norm_w=ssm_norm_w, w_out_ssm=w_out_ssm, ln_g=ln_g, ln_b=ln_b, loss_target=loss_target, m_w_in_attn=m_w_in_attn, m_w_out_attn=m_w_out_attn, m_rel_bias=m_rel_bias, m_w_in_ssm=m_w_in_ssm, m_conv_w=m_conv_w, m_conv_b=m_conv_b, m_dt_bias=m_dt_bias, m_a_log=m_a_log, m_d_skip=m_d_skip, m_ssm_norm_w=m_ssm_norm_w, m_w_out_ssm=m_w_out_ssm, m_ln_g=m_ln_g, m_ln_b=m_ln_b, v_w_in_attn=v_w_in_attn, v_w_out_attn=v_w_out_attn, v_rel_bias=v_rel_bias, v_w_in_ssm=v_w_in_ssm, v_conv_w=v_conv_w, v_conv_b=v_conv_b, v_dt_bias=v_dt_bias, v_a_log=v_a_log, v_d_skip=v_d_skip, v_ssm_norm_w=v_ssm_norm_w, v_w_out_ssm=v_w_out_ssm, v_ln_g=v_ln_g, v_ln_b=v_ln_b)
    weights = {n: given[n] for n in TWIN_WEIGHTS}
    shared = {n: given[n] for n in SHARED_INPUTS}
    per_example = {n: given[n] for n in ['x']}
    grad_fn = _jax.value_and_grad(_loss, argnums=(0, 1))

    def one_microbatch(ex, loss_target):
        ex = dict(ex)
        diff = ex.pop(TWIN_DIFF_INPUT)
        return grad_fn(weights, diff, {**shared, **ex}, loss_target)

    if N_MICROBATCH == 1:
        loss, (grad_w, grad_x) = one_microbatch(per_example, given["loss_target"])
    else:
        def body(carry, xs):
            loss_sum, grad_sum = carry
            l_k, (gw_k, gx_k) = one_microbatch(xs[0], xs[1])
            with _jax.named_scope("update"):
                return (loss_sum + l_k, _jax.tree.map(_jnp.add, grad_sum, gw_k)), gx_k

        init = (_jnp.zeros((), _jnp.float32), _jax.tree.map(_jnp.zeros_like, weights))
        (loss, grad_w), grad_x = _jax.lax.scan(body, init, (per_example, given["loss_target"]))
    with _jax.named_scope("update"):
        delta_w, new_m, new_v = {}, {}, {}
        for n in TWIN_WEIGHTS:
            delta_w[n], new_m[n], new_v[n] = _adamw(weights[n], grad_w[n], given["m_" + n], given["v_" + n])
    return (loss, grad_x, *[grad_w[n] for n in TWIN_WEIGHTS], *[delta_w[n] for n in TWIN_WEIGHTS],
            *[new_m[n] for n in TWIN_WEIGHTS], *[new_v[n] for n in TWIN_WEIGHTS])
```

```python
import functools
import math

import numpy as np
import jax
import jax.numpy as jnp
from jax import lax
from jax.experimental import pallas as pl
from jax.experimental.pallas import tpu as pltpu

F32 = jnp.float32
BF16 = jnp.bfloat16
MESH = pl.DeviceIdType.MESH

ATTN_PATTERNS = ((128, 1), (512, 4), (2048, 16))
N_GROUPS_ATTN = 3
HEAD_DIM = 128
ATTN_BLOCK = 128
NUM_BUCKETS = 32
MAX_DISTANCE = 2048
SSM_HEAD_DIM = 64
HEADS_PER_SSM_GROUP = 16
SSM_GROUP_WIDTH = HEADS_PER_SSM_GROUP * SSM_HEAD_DIM
D_STATE = 128
CONV_WIDTH = 4
CHUNK = 128
DEPTH = 2
DEEPNORM_ALPHA = (2 * DEPTH) ** 0.25
LN_EPS = 1e-5
RMS_EPS = 1e-5
NEG_INF = -1e30
ADAM_LR = 0.001
ADAM_B1 = 0.9
ADAM_B2 = 0.999
ADAM_EPS = 1e-08
ADAM_WD = 0.01
ADAM_STEP = 10

N_CHIPS = 4
N_DEV = 8

VMEM_LIMIT_V7X = 56 * 1024 * 1024
LANES = 128


def _cparams(sem=None):
    return pltpu.CompilerParams(dimension_semantics=sem, vmem_limit_bytes=VMEM_LIMIT_V7X)


def _sigmoid(x):
    return 1.0 / (1.0 + jnp.exp(-x))


def _dot(a, b):
    return jnp.dot(a, b, preferred_element_type=F32)


def _dot_nt(a, b):
    return lax.dot_general(a, b, (((1,), (1,)), ((), ())), preferred_element_type=F32)


def _dot_tn(a, b):
    return lax.dot_general(a, b, (((0,), (0,)), ((), ())), preferred_element_type=F32)


def _split2(x):
    hi = x.astype(BF16)
    lo = (x - hi.astype(F32)).astype(BF16)
    return hi, lo


def _split3(x):
    hi = x.astype(BF16)
    r = x - hi.astype(F32)
    mid = r.astype(BF16)
    lo = (r - mid.astype(F32)).astype(BF16)
    return hi, mid, lo


def _matmul(a, b, *, mode, grid, a_spec, b_spec, out_shape, out_spec, tile, name,
            add=None, add_spec=None, add_scale=1.0):
    nk = grid[2]
    tm, tn = tile
    dot = {"nn": _dot, "nt": _dot_nt, "tn": _dot_tn}[mode]
    has_add = add is not None

    def body(*refs):
        if has_add:
            a_ref, b_ref, add_ref, o_ref, acc_ref = refs
        else:
            a_ref, b_ref, o_ref, acc_ref = refs
        k = pl.program_id(2)

        @pl.when(k == 0)
        def _():
            acc_ref[...] = jnp.zeros_like(acc_ref)

        acc_ref[...] += dot(a_ref[...].astype(BF16), b_ref[...].astype(BF16))

        @pl.when(k == nk - 1)
        def _():
            r = acc_ref[...]
            if has_add:
                r = r + add_scale * add_ref[...].astype(F32)
            o_ref[...] = r.astype(o_ref.dtype)

    in_specs = [a_spec, b_spec] + ([add_spec] if has_add else [])
    args = (a, b) + ((add,) if has_add else ())
    return pl.pallas_call(
        body, name=name, grid=grid, in_specs=in_specs, out_specs=out_spec, out_shape=out_shape,
        scratch_shapes=[pltpu.VMEM((tm, tn), F32)],
        compiler_params=_cparams(("parallel", "parallel", "arbitrary")),
    )(*args)


def _pick(n, pref):
    for t in pref:
        if n % t == 0:
            return t
    return n


_TILE_PREF = (1024, 512, 256, 128)


def _mm_nn_sharded(a, w4, out_dtype, name):
    m, k = a.shape
    _, _, nn = w4.shape
    tm, tk, tn = _pick(m, _TILE_PREF), _pick(k, _TILE_PREF), _pick(nn, _TILE_PREF)
    npb = nn // tn
    return _matmul(
        a, w4, mode="nn", grid=(m // tm, N_CHIPS * npb, k // tk), tile=(tm, tn), name=name,
        a_spec=pl.BlockSpec((tm, tk), lambda i, j, kk: (i, kk)),
        b_spec=pl.BlockSpec((None, tk, tn), lambda i, j, kk: (j // npb, kk, j % npb)),
        out_shape=jax.ShapeDtypeStruct((m, N_CHIPS * nn), out_dtype),
        out_spec=pl.BlockSpec((tm, tn), lambda i, j, kk: (i, j)))


def _mm_nn(a, b, out_dtype, name):
    m, k = a.shape
    _, n = b.shape
    tm, tk, tn = _pick(m, _TILE_PREF), _pick(k, _TILE_PREF), _pick(n, _TILE_PREF)
    return _matmul(
        a, b, mode="nn", grid=(m // tm, n // tn, k // tk), tile=(tm, tn), name=name,
        a_spec=pl.BlockSpec((tm, tk), lambda i, j, kk: (i, kk)),
        b_spec=pl.BlockSpec((tk, tn), lambda i, j, kk: (kk, j)),
        out_shape=jax.ShapeDtypeStruct((m, n), out_dtype),
        out_spec=pl.BlockSpec((tm, tn), lambda i, j, kk: (i, j)))


def _mm_nt(a, b, out_dtype, name, add=None, add_scale=1.0):
    m, k = a.shape
    n, _ = b.shape
    tm, tk, tn = _pick(m, _TILE_PREF), _pick(k, _TILE_PREF), _pick(n, _TILE_PREF)
    return _matmul(
        a, b, mode="nt", grid=(m // tm, n // tn, k // tk), tile=(tm, tn), name=name,
        a_spec=pl.BlockSpec((tm, tk), lambda i, j, kk: (i, kk)),
        b_spec=pl.BlockSpec((tn, tk), lambda i, j, kk: (j, kk)),
        out_shape=jax.ShapeDtypeStruct((m, n), out_dtype),
        out_spec=pl.BlockSpec((tm, tn), lambda i, j, kk: (i, j)),
        add=add, add_spec=pl.BlockSpec((tm, tn), lambda i, j, kk: (i, j)), add_scale=add_scale)


def _mm_nt_sharded_k(a, w4, out_dtype, name, add=None, add_scale=1.0):
    m, _ = a.shape
    _, n, kn = w4.shape
    tm, tk, tn = _pick(m, _TILE_PREF), _pick(kn, _TILE_PREF), _pick(n, _TILE_PREF)
    kpb = kn // tk
    return _matmul(
        a, w4, mode="nt", grid=(m // tm, n // tn, N_CHIPS * kpb), tile=(tm, tn), name=name,
        a_spec=pl.BlockSpec((tm, tk), lambda i, j, kk: (i, kk)),
        b_spec=pl.BlockSpec((None, tn, tk), lambda i, j, kk: (kk // kpb, j, kk % kpb)),
        out_shape=jax.ShapeDtypeStruct((m, n), out_dtype),
        out_spec=pl.BlockSpec((tm, tn), lambda i, j, kk: (i, j)),
        add=add, add_spec=pl.BlockSpec((tm, tn), lambda i, j, kk: (i, j)), add_scale=add_scale)


def _mm_nt_sharded_n(a, w4, out_dtype, name):
    m, k = a.shape
    _, nn, _ = w4.shape
    tm, tk, tn = _pick(m, _TILE_PREF), _pick(k, _TILE_PREF), _pick(nn, _TILE_PREF)
    npb = nn // tn
    return _matmul(
        a, w4, mode="nt", grid=(m // tm, N_CHIPS * npb, k // tk), tile=(tm, tn), name=name,
        a_spec=pl.BlockSpec((tm, tk), lambda i, j, kk: (i, kk)),
        b_spec=pl.BlockSpec((None, tn, tk), lambda i, j, kk: (j // npb, j % npb, kk)),
        out_shape=jax.ShapeDtypeStruct((m, N_CHIPS * nn), out_dtype),
        out_spec=pl.BlockSpec((tm, tn), lambda i, j, kk: (i, j)))


def _mm_tn(a, b, out_dtype, name, shard_cols=None):
    k, m = a.shape
    _, n = b.shape
    nn = n if shard_cols is None else shard_cols
    tm, tk, tn = _pick(m, _TILE_PREF), _pick(k, _TILE_PREF), _pick(nn, _TILE_PREF)
    if shard_cols is None:
        out_shape = jax.ShapeDtypeStruct((m, n), out_dtype)
        out_spec = pl.BlockSpec((tm, tn), lambda i, j, kk: (i, j))
    else:
        npb = nn // tn
        out_shape = jax.ShapeDtypeStruct((n // nn, m, nn), out_dtype)
        out_spec = pl.BlockSpec((None, tm, tn), lambda i, j, kk: (j // npb, i, j % npb))
    return _matmul(
        a, b, mode="tn", grid=(m // tm, n // tn, k // tk), tile=(tm, tn), name=name,
        a_spec=pl.BlockSpec((tk, tm), lambda i, j, kk: (kk, i)),
        b_spec=pl.BlockSpec((tk, tn), lambda i, j, kk: (kk, j)),
        out_shape=out_shape, out_spec=out_spec)


def _cast_bf16(x, name):
    r, c = x.shape
    tr = _pick(r, (512, 256, 128, 8))

    def body(x_ref, o_ref):
        o_ref[...] = x_ref[...].astype(BF16)

    return pl.pallas_call(
        body, name=name, grid=(r // tr,),
        in_specs=[pl.BlockSpec((tr, c), lambda i: (i, 0))],
        out_specs=pl.BlockSpec((tr, c), lambda i: (i, 0)),
        out_shape=jax.ShapeDtypeStruct((r, c), BF16),
        compiler_params=_cparams(("parallel",)),
    )(x)


def _bucket_tiles():
    qi = np.arange(ATTN_BLOCK)[:, None]
    ki = np.arange(2 * ATTN_BLOCK)[None, :]
    delta = np.clip(ATTN_BLOCK + qi - ki, 0, None)
    tiles = []
    max_exact = NUM_BUCKETS // 2
    for _, dil in ATTN_PATTERNS:
        dist = (delta * dil).astype(np.int32)
        d_f = np.maximum(dist, 1).astype(np.float32)
        large = max_exact + (np.log(d_f / np.float32(max_exact)) / np.float32(math.log(MAX_DISTANCE / max_exact))
                             * np.float32(NUM_BUCKETS - max_exact)).astype(np.int32)
        large = np.minimum(large, NUM_BUCKETS - 1)
        tiles.append(np.where(dist < max_exact, dist, large).astype(np.int32))
    return jnp.asarray(np.stack(tiles))


def _bias_expand(rel_bias, buckets, hpg):
    def body(tab_ref, bk_ref, o_ref):
        g, h = pl.program_id(0), pl.program_id(1)
        bk = bk_ref[...]
        acc = jnp.zeros((ATTN_BLOCK, 2 * ATTN_BLOCK), F32)
        for b in range(NUM_BUCKETS):
            acc = jnp.where(bk == b, tab_ref[b, g * hpg + h], acc)
        o_ref[...] = acc

    return pl.pallas_call(
        body, name="bias_expand", grid=(N_GROUPS_ATTN, hpg),
        in_specs=[pl.BlockSpec(memory_space=pltpu.SMEM),
                  pl.BlockSpec((None, ATTN_BLOCK, 2 * ATTN_BLOCK), lambda g, h: (g, 0, 0))],
        out_specs=pl.BlockSpec((None, None, ATTN_BLOCK, 2 * ATTN_BLOCK), lambda g, h: (g, h, 0, 0)),
        out_shape=jax.ShapeDtypeStruct((N_GROUPS_ATTN, hpg, ATTN_BLOCK, 2 * ATTN_BLOCK), F32),
        compiler_params=_cparams(("parallel", "parallel")),
    )(rel_bias, buckets)


def _bias_reduce(dtiles, buckets, hpg):
    def body(t_ref, bk_ref, o_ref):
        bk = bk_ref[...]
        t = t_ref[...]
        rows = lax.broadcasted_iota(jnp.int32, (NUM_BUCKETS, LANES), 0)
        acc = jnp.zeros((NUM_BUCKETS, LANES), F32)
        for b in range(NUM_BUCKETS):
            s = jnp.sum(jnp.sum(jnp.where(bk == b, t, 0.0), axis=1, keepdims=True), axis=0, keepdims=True)
            acc = jnp.where(rows == b, s, acc)
        o_ref[...] = acc

    return pl.pallas_call(
        body, name="bias_reduce", grid=(N_GROUPS_ATTN, hpg),
        in_specs=[pl.BlockSpec((None, None, ATTN_BLOCK, 2 * ATTN_BLOCK), lambda g, h: (g, h, 0, 0)),
                  pl.BlockSpec((None, ATTN_BLOCK, 2 * ATTN_BLOCK), lambda g, h: (g, 0, 0))],
        out_specs=pl.BlockSpec((None, NUM_BUCKETS, LANES), lambda g, h: (g * hpg + h, 0, 0)),
        out_shape=jax.ShapeDtypeStruct((N_GROUPS_ATTN * hpg, NUM_BUCKETS, LANES), F32),
        compiler_params=_cparams(("parallel", "parallel")),
    )(dtiles, buckets)


def _attn_valid(n_is_first):
    qi = lax.broadcasted_iota(jnp.int32, (ATTN_BLOCK, 2 * ATTN_BLOCK), 0)
    ki = lax.broadcasted_iota(jnp.int32, (ATTN_BLOCK, 2 * ATTN_BLOCK), 1)
    delta = ATTN_BLOCK + qi - ki
    band = (delta >= 0) & (delta <= ATTN_BLOCK)
    return band & (jnp.logical_not(n_is_first) | (ki >= ATTN_BLOCK))


def _attn_fwd(pa, bias, g, dil, hpg):
    s, c = pa.shape
    w = hpg * HEAD_DIM
    cpb = c // w
    rows = s // dil
    nb = rows // ATTN_BLOCK
    pav = pa.reshape(rows, dil * c)
    scale = HEAD_DIM ** -0.5

    def body(q_ref, kc_ref, kp_ref, vc_ref, vp_ref, bias_ref, o_ref, lse_ref):
        valid = _attn_valid(pl.program_id(1) == 0)
        for h in range(hpg):
            sl = slice(h * HEAD_DIM, (h + 1) * HEAD_DIM)
            k2 = jnp.concatenate([kp_ref[:, sl], kc_ref[:, sl]], axis=0)
            v2 = jnp.concatenate([vp_ref[:, sl], vc_ref[:, sl]], axis=0)
            sc = _dot_nt(q_ref[:, sl], k2) * scale + bias_ref[h]
            sc = jnp.where(valid, sc, NEG_INF)
            m = jnp.max(sc, axis=1, keepdims=True)
            p = jnp.exp(sc - m)
            l = jnp.sum(p, axis=1, keepdims=True)
            o_ref[:, sl] = _dot(p.astype(BF16), v2) / l
            lse_ref[:, sl] = jnp.broadcast_to(m + jnp.log(l), (ATTN_BLOCK, HEAD_DIM))

    def col(off):
        return lambda r, n: (n, r * cpb + 3 * g + off)

    def colp(off):
        return lambda r, n: (jnp.maximum(n - 1, 0), r * cpb + 3 * g + off)

    blk = (ATTN_BLOCK, w)
    o, lse = pl.pallas_call(
        body, name=f"attn_fwd_g{g}", grid=(dil, nb),
        in_specs=[pl.BlockSpec(blk, col(0)), pl.BlockSpec(blk, col(1)), pl.BlockSpec(blk, colp(1)),
                  pl.BlockSpec(blk, col(2)), pl.BlockSpec(blk, colp(2)),
                  pl.BlockSpec((None, hpg, ATTN_BLOCK, 2 * ATTN_BLOCK), lambda r, n: (g, 0, 0, 0))],
        out_specs=[pl.BlockSpec(blk, lambda r, n: (n, r)), pl.BlockSpec(blk, lambda r, n: (n, r))],
        out_shape=[jax.ShapeDtypeStruct((rows, dil * w), F32), jax.ShapeDtypeStruct((rows, dil * w), F32)],
        compiler_params=_cparams(("parallel", "parallel")),
    )(pav, pav, pav, pav, pav, bias)
    return o.reshape(s, w), lse.reshape(s, w)


def _attn_combine(os_, lses, pa, hpg):
    s, w = os_[0].shape
    gate_blk = pa.shape[1] // w - 1
    tm = _pick(s, (256, 128))

    def body(o0, o1, o2, l0, l1, l2, gate_ref, o_ref, lse_ref, y_ref):
        a0, a1, a2 = l0[...], l1[...], l2[...]
        m = jnp.maximum(jnp.maximum(a0, a1), a2)
        e0, e1, e2 = jnp.exp(a0 - m), jnp.exp(a1 - m), jnp.exp(a2 - m)
        den = e0 + e1 + e2
        o = (e0 * o0[...] + e1 * o1[...] + e2 * o2[...]) / den
        gate = gate_ref[...].astype(F32)
        o_ref[...] = o.astype(BF16)
        lse_ref[...] = m + jnp.log(den)
        y_ref[...] = (o * (gate * _sigmoid(gate))).astype(BF16)

    spec = pl.BlockSpec((tm, w), lambda i: (i, 0))
    return pl.pallas_call(
        body, name="attn_combine", grid=(s // tm,),
        in_specs=[spec] * 6 + [pl.BlockSpec((tm, w), lambda i: (i, gate_blk))],
        out_specs=[spec, spec, spec],
        out_shape=[jax.ShapeDtypeStruct((s, w), BF16), jax.ShapeDtypeStruct((s, w), F32),
                   jax.ShapeDtypeStruct((s, w), BF16)],
        compiler_params=_cparams(("parallel",)),
    )(*os_, *lses, pa)


def _attn_pre_bwd(dy, o, pa, hpg):
    s, w = dy.shape
    gate_blk = pa.shape[1] // w - 1
    tm = _pick(s, (256, 128))

    def body(dy_ref, o_ref, gate_ref, do_ref, dl_ref, dg_ref):
        gate = gate_ref[...].astype(F32)
        sg = _sigmoid(gate)
        dyv = dy_ref[...].astype(F32)
        ov = o_ref[...].astype(F32)
        do = dyv * (gate * sg)
        do_ref[...] = do.astype(BF16)
        dg_ref[...] = (dyv * ov * (sg * (1.0 + gate * (1.0 - sg)))).astype(BF16)
        prod = do * ov
        for h in range(hpg):
            sl = slice(h * HEAD_DIM, (h + 1) * HEAD_DIM)
            dl_ref[:, sl] = jnp.broadcast_to(jnp.sum(prod[:, sl], axis=1, keepdims=True), (tm, HEAD_DIM))

    spec = pl.BlockSpec((tm, w), lambda i: (i, 0))
    return pl.pallas_call(
        body, name="attn_pre_bwd", grid=(s // tm,),
        in_specs=[spec, spec, pl.BlockSpec((tm, w), lambda i: (i, gate_blk))],
        out_specs=[spec, spec, spec],
        out_shape=[jax.ShapeDtypeStruct((s, w), BF16), jax.ShapeDtypeStruct((s, w), F32),
                   jax.ShapeDtypeStruct((s, w), BF16)],
        compiler_params=_cparams(("parallel",)),
    )(dy, o, pa)


def _attn_bwd(pa, bias, do, lse, delta, g, dil, hpg):
    s, c = pa.shape
    w = hpg * HEAD_DIM
    cpb = c // w
    rows = s // dil
    nb = rows // ATTN_BLOCK
    pav = pa.reshape(rows, dil * c)
    dov, lsev, dlv = (t.reshape(rows, dil * w) for t in (do, lse, delta))
    scale = HEAD_DIM ** -0.5

    def body(q_ref, kc_ref, kp_ref, vc_ref, vp_ref, bias_ref, do_ref, lse_ref, dl_ref,
             dq_ref, dk_ref, dv_ref, db_ref, dkc_ref, dvc_ref):
        r, i = pl.program_id(0), pl.program_id(1)
        n = nb - 1 - i
        valid = _attn_valid(n == 0)

        @pl.when((r == 0) & (i == 0))
        def _():
            db_ref[...] = jnp.zeros_like(db_ref)

        @pl.when(i == 0)
        def _():
            dkc_ref[...] = jnp.zeros_like(dkc_ref)
            dvc_ref[...] = jnp.zeros_like(dvc_ref)

        for h in range(hpg):
            sl = slice(h * HEAD_DIM, (h + 1) * HEAD_DIM)
            q = q_ref[:, sl]
            dov_ = do_ref[:, sl]
            k2 = jnp.concatenate([kp_ref[:, sl], kc_ref[:, sl]], axis=0)
            v2 = jnp.concatenate([vp_ref[:, sl], vc_ref[:, sl]], axis=0)
            sc = _dot_nt(q, k2) * scale + bias_ref[h]
            p = jnp.exp(jnp.where(valid, sc - lse_ref[:, sl][:, 0:1], NEG_INF))
            dp = _dot_nt(dov_, v2)
            ds = p * (dp - dl_ref[:, sl][:, 0:1])
            db_ref[h] += ds
            dsb = ds.astype(BF16)
            dq_ref[:, sl] = (_dot(dsb, k2) * scale).astype(BF16)
            dk2 = _dot_tn(dsb, q) * scale
            dv2 = _dot_tn(p.astype(BF16), dov_)
            dk_ref[:, sl] = (dk2[ATTN_BLOCK:] + dkc_ref[:, sl]).astype(BF16)
            dv_ref[:, sl] = (dv2[ATTN_BLOCK:] + dvc_ref[:, sl]).astype(BF16)
            dkc_ref[:, sl] = dk2[:ATTN_BLOCK]
            dvc_ref[:, sl] = dv2[:ATTN_BLOCK]

    def col(off):
        return lambda r, i: (nb - 1 - i, r * cpb + 3 * g + off)

    def colp(off):
        return lambda r, i: (jnp.maximum(nb - 2 - i, 0), r * cpb + 3 * g + off)

    blk = (ATTN_BLOCK, w)
    tok = pl.BlockSpec(blk, lambda r, i: (nb - 1 - i, r))
    dq, dk, dv, db = pl.pallas_call(
        body, name=f"attn_bwd_g{g}", grid=(dil, nb),
        in_specs=[pl.BlockSpec(blk, col(0)), pl.BlockSpec(blk, col(1)), pl.BlockSpec(blk, colp(1)),
                  pl.BlockSpec(blk, col(2)), pl.BlockSpec(blk, colp(2)),
                  pl.BlockSpec((None, hpg, ATTN_BLOCK, 2 * ATTN_BLOCK), lambda r, i: (g, 0, 0, 0)),
                  tok, tok, tok],
        out_specs=[tok, tok, tok,
                   pl.BlockSpec((hpg, ATTN_BLOCK, 2 * ATTN_BLOCK), lambda r, i: (0, 0, 0))],
        out_shape=[jax.ShapeDtypeStruct((rows, dil * w), BF16)] * 3
        + [jax.ShapeDtypeStruct((hpg, ATTN_BLOCK, 2 * ATTN_BLOCK), F32)],
        scratch_shapes=[pltpu.VMEM(blk, F32), pltpu.VMEM(blk, F32)],
        compiler_params=_cparams(("arbitrary", "arbitrary")),
    )(pav, pav, pav, pav, pav, bias, dov, lsev, dlv)
    return dq.reshape(s, w), dk.reshape(s, w), dv.reshape(s, w), db


def _ln_fwd(xin, h, gamma, beta, name, affine_in=None, target=None):
    s, d = xin.shape
    tm = _pick(s, (128,))
    has_aff = affine_in is not None
    has_tgt = target is not None

    def body(*refs):
        it = iter(refs)
        x_ref, h_ref, g_ref, b_ref = next(it), next(it), next(it), next(it)
        if has_aff:
            gi_ref, bi_ref = next(it), next(it)
        if has_tgt:
            t_ref = next(it)
        xh_ref, rs_ref = next(it), next(it)
        x = x_ref[...]
        if has_aff:
            x = x * gi_ref[...] + bi_ref[...]
        u = DEEPNORM_ALPHA * x + h_ref[...]
        mu = jnp.mean(u, axis=1, keepdims=True)
        uc = u - mu
        var = jnp.mean(uc * uc, axis=1, keepdims=True)
        rstd = lax.rsqrt(var + LN_EPS)
        xhat = uc * rstd
        xh_ref[...] = xhat
        rs_ref[...] = rstd
        y = xhat * g_ref[...] + b_ref[...]
        if has_tgt:
            dy_ref, l_ref = next(it), next(it)
            e = y - t_ref[...]
            dy_ref[...] = e * (1.0 / d)
            l_ref[...] = jnp.sum(e * e, axis=1, keepdims=True)
        else:
            y_ref = next(it)
            y_ref[...] = y.astype(BF16)

    row = pl.BlockSpec((tm, d), lambda i: (i, 0))
    vec = pl.BlockSpec((1, d), lambda i: (0, 0))
    one = pl.BlockSpec((tm, 1), lambda i: (i, 0))
    in_specs = [row, row, vec, vec] + ([vec, vec] if has_aff else []) + ([row] if has_tgt else [])
    args = [xin, h, gamma, beta] + (list(affine_in) if has_aff else []) + ([target] if has_tgt else [])
    out_specs = [row, one] + ([row, one] if has_tgt else [row])
    out_shape = [jax.ShapeDtypeStruct((s, d), F32), jax.ShapeDtypeStruct((s, 1), F32)]
    out_shape += ([jax.ShapeDtypeStruct((s, d), F32), jax.ShapeDtypeStruct((s, 1), F32)] if has_tgt
                  else [jax.ShapeDtypeStruct((s, d), BF16)])
    return pl.pallas_call(
        body, name=name, grid=(s // tm,), in_specs=in_specs, out_specs=out_specs, out_shape=out_shape,
        compiler_params=_cparams(("parallel",)),
    )(*args)


def _ln_bwd(dy, xhat, rstd, gamma, name):
    s, d = dy.shape
    tm = _pick(s, (128,))

    def body(dy_ref, xh_ref, rs_ref, g_ref, du_ref, dub_ref, dg_ref, db_ref):
        @pl.when(pl.program_id(0) == 0)
        def _():
            dg_ref[...] = jnp.zeros_like(dg_ref)
            db_ref[...] = jnp.zeros_like(db_ref)

        dyv = dy_ref[...]
        xh = xh_ref[...]
        dg_ref[...] += jnp.sum(dyv * xh, axis=0, keepdims=True)
        db_ref[...] += jnp.sum(dyv, axis=0, keepdims=True)
        dxh = dyv * g_ref[...]
        m1 = jnp.mean(dxh, axis=1, keepdims=True)
        m2 = jnp.mean(dxh * xh, axis=1, keepdims=True)
        du = rs_ref[...] * (dxh - m1 - xh * m2)
        du_ref[...] = du
        dub_ref[...] = du.astype(BF16)

    row = pl.BlockSpec((tm, d), lambda i: (i, 0))
    vec = pl.BlockSpec((1, d), lambda i: (0, 0))
    one = pl.BlockSpec((tm, 1), lambda i: (i, 0))
    return pl.pallas_call(
        body, name=name, grid=(s // tm,), in_specs=[row, row, one, vec],
        out_specs=[row, row, vec, vec],
        out_shape=[jax.ShapeDtypeStruct((s, d), F32), jax.ShapeDtypeStruct((s, d), BF16),
                   jax.ShapeDtypeStruct((1, d), F32), jax.ShapeDtypeStruct((1, d), F32)],
        compiler_params=_cparams(("arbitrary",)),
    )(dy, xhat, rstd, gamma)


_HALO = 16


def _conv_taps(ext, tm, w_ref):
    acc = None
    for k in range(CONV_WIDTH):
        lo = _HALO - (CONV_WIDTH - 1) + k
        term = w_ref[k:k + 1, :] * ext[lo:lo + tm, :]
        acc = term if acc is None else acc + term
    return acc


def _conv_fwd(pzx, conv_w, conv_b, d_inner):
    s, _ = pzx.shape
    cd = conv_w.shape[1]
    tm = _pick(s, (512, 256, 128))
    tc = _pick(cd, (1024, 512, 256, 128))
    off = d_inner // tc
    hb = tm // _HALO

    def body(x_ref, p_ref, w_ref, b_ref, o_ref):
        prev = jnp.where(pl.program_id(0) > 0, p_ref[...].astype(F32), 0.0)
        ext = jnp.concatenate([prev, x_ref[...].astype(F32)], axis=0)
        pre = _conv_taps(ext, tm, w_ref) + b_ref[...]
        o_ref[...] = (pre * _sigmoid(pre)).astype(BF16)

    return pl.pallas_call(
        body, name="conv_fwd", grid=(s // tm, cd // tc),
        in_specs=[pl.BlockSpec((tm, tc), lambda i, j: (i, off + j)),
                  pl.BlockSpec((_HALO, tc), lambda i, j: (jnp.maximum(i * hb - 1, 0), off + j)),
                  pl.BlockSpec((CONV_WIDTH, tc), lambda i, j: (0, j)),
                  pl.BlockSpec((1, tc), lambda i, j: (0, j))],
        out_specs=pl.BlockSpec((tm, tc), lambda i, j: (i, j)),
        out_shape=jax.ShapeDtypeStruct((s, cd), BF16),
        compiler_params=_cparams(("parallel", "parallel")),
    )(pzx, pzx, conv_w, conv_b)


def _conv_bwd_a(pzx, dxbc, conv_w, conv_b, d_inner):
    s, _ = pzx.shape
    cd = conv_w.shape[1]
    tm = _pick(s, (512, 256, 128))
    tc = _pick(cd, (1024, 512, 256, 128))
    off = d_inner // tc
    hb = tm // _HALO

    def body(x_ref, p_ref, d_ref, w_ref, b_ref, o_ref, dw_ref, db_ref):
        @pl.when(pl.program_id(1) == 0)
        def _():
            dw_ref[...] = jnp.zeros_like(dw_ref)
            db_ref[...] = jnp.zeros_like(db_ref)

        prev = jnp.where(pl.program_id(1) > 0, p_ref[...].astype(F32), 0.0)
        ext = jnp.concatenate([prev, x_ref[...].astype(F32)], axis=0)
        pre = _conv_taps(ext, tm, w_ref) + b_ref[...]
        sg = _sigmoid(pre)
        dpre = d_ref[...].astype(F32) * (sg * (1.0 + pre * (1.0 - sg)))
        o_ref[...] = dpre
        db_ref[...] += jnp.sum(dpre, axis=0, keepdims=True)
        for k in range(CONV_WIDTH):
            lo = _HALO - (CONV_WIDTH - 1) + k
            dw_ref[k:k + 1, :] += jnp.sum(dpre * ext[lo:lo + tm, :], axis=0, keepdims=True)

    return pl.pallas_call(
        body, name="conv_bwd_a", grid=(cd // tc, s // tm),
        in_specs=[pl.BlockSpec((tm, tc), lambda j, i: (i, off + j)),
                  pl.BlockSpec((_HALO, tc), lambda j, i: (jnp.maximum(i * hb - 1, 0), off + j)),
                  pl.BlockSpec((tm, tc), lambda j, i: (i, j)),
                  pl.BlockSpec((CONV_WIDTH, tc), lambda j, i: (0, j)),
                  pl.BlockSpec((1, tc), lambda j, i: (0, j))],
        out_specs=[pl.BlockSpec((tm, tc), lambda j, i: (i, j)),
                   pl.BlockSpec((CONV_WIDTH, tc), lambda j, i: (0, j)),
                   pl.BlockSpec((1, tc), lambda j, i: (0, j))],
        out_shape=[jax.ShapeDtypeStruct((s, cd), F32), jax.ShapeDtypeStruct((CONV_WIDTH, cd), F32),
                   jax.ShapeDtypeStruct((1, cd), F32)],
        compiler_params=_cparams(("parallel", "arbitrary")),
    )(pzx, pzx, dxbc, conv_w, conv_b)


def _conv_bwd_b(dpre, conv_w):
    s, cd = dpre.shape
    tm = _pick(s, (512, 256, 128))
    tc = _pick(cd, (1024, 512, 256, 128))
    hb = tm // 8
    nrb = s // tm

    def body(x_ref, nx_ref, w_ref, o_ref):
        nxt = jnp.where(pl.program_id(0) < nrb - 1, nx_ref[...], 0.0)
        ext = jnp.concatenate([x_ref[...], nxt], axis=0)
        acc = None
        for k in range(CONV_WIDTH):
            lo = CONV_WIDTH - 1 - k
            term = w_ref[k:k + 1, :] * ext[lo:lo + tm, :]
            acc = term if acc is None else acc + term
        o_ref[...] = acc.astype(BF16)

    return pl.pallas_call(
        body, name="conv_bwd_b", grid=(nrb, cd // tc),
        in_specs=[pl.BlockSpec((tm, tc), lambda i, j: (i, j)),
                  pl.BlockSpec((8, tc), lambda i, j: (jnp.minimum((i + 1) * hb, s // 8 - 1), j)),
                  pl.BlockSpec((CONV_WIDTH, tc), lambda i, j: (0, j))],
        out_specs=pl.BlockSpec((tm, tc), lambda i, j: (i, j)),
        out_shape=jax.ShapeDtypeStruct((s, cd), BF16),
        compiler_params=_cparams(("parallel", "parallel")),
    )(dpre, dpre, conv_w)


def _expand_matrix():
    e = np.zeros((LANES, SSM_GROUP_WIDTH), np.float32)
    for h in range(HEADS_PER_SSM_GROUP):
        e[h, h * SSM_HEAD_DIM:(h + 1) * SSM_HEAD_DIM] = 1.0
    return jnp.asarray(e, BF16)


def _expand(t, e):
    hi, lo = _split2(t)
    return _dot(hi, e) + _dot(lo, e)


def _segsum(v, e):
    hi, lo = _split2(v)
    return _dot_nt(hi, e) + _dot_nt(lo, e)


def _tri_dot(tri, x):
    hi, mid, lo = _split3(x)
    return _dot(tri, hi) + _dot(tri, mid) + _dot(tri, lo)


def _ssd_common(dtp_ref, a_ref, dtb_ref, x_ref, e):
    li = lax.broadcasted_iota(jnp.int32, (CHUNK, CHUNK), 0)
    si = lax.broadcasted_iota(jnp.int32, (CHUNK, CHUNK), 1)
    causal = li >= si
    tril = causal.astype(BF16)
    raw = dtp_ref[...] + dtb_ref[...]
    dt = jnp.maximum(raw, 0.0) + jnp.log(1.0 + jnp.exp(-jnp.abs(raw)))
    head_lane = lax.broadcasted_iota(jnp.int32, (1, LANES), 1) < HEADS_PER_SSM_GROUP
    a = jnp.where(head_lane, -jnp.exp(a_ref[...]), 0.0)
    a_cum = _tri_dot(tril, dt * a)
    a_cum_t = a_cum.T
    e_a = jnp.exp(a_cum)
    to_end = jnp.exp(a_cum[CHUNK - 1:CHUNK, :] - a_cum)
    x = x_ref[...].astype(F32)
    dt_e = _expand(dt, e)
    return dict(causal=causal, raw=raw, dt=dt, a=a, a_cum=a_cum, a_cum_t=a_cum_t, e_a=e_a,
                to_end=to_end, x=x, dt_e=dt_e, xdt=x * dt_e, e_a_e=_expand(e_a, e),
                to_end_e=_expand(to_end, e))


def _decay(q, h):
    seg = q["a_cum"][:, h:h + 1] - q["a_cum_t"][h:h + 1, :]
    return jnp.exp(jnp.where(q["causal"], seg, -jnp.inf))


def _ssd_specs(ng, d_inner, rev, nc):
    cidx = (lambda i: nc - 1 - i) if rev else (lambda i: i)
    boff = d_inner // D_STATE
    return dict(
        xs=pl.BlockSpec((CHUNK, SSM_GROUP_WIDTH), lambda g, i: (cidx(i), g)),
        b=pl.BlockSpec((CHUNK, D_STATE), lambda g, i: (cidx(i), boff + g)),
        c=pl.BlockSpec((CHUNK, D_STATE), lambda g, i: (cidx(i), boff + ng + g)),
        dtp=pl.BlockSpec((None, CHUNK, LANES), lambda g, i: (g, cidx(i), 0)),
        vec=pl.BlockSpec((None, 1, LANES), lambda g, i: (g, 0, 0)),
        wide=pl.BlockSpec((None, 1, SSM_GROUP_WIDTH), lambda g, i: (g, 0, 0)),
        e=pl.BlockSpec((LANES, SSM_GROUP_WIDTH), lambda g, i: (0, 0)),
        st=pl.BlockSpec((None, None, D_STATE, SSM_GROUP_WIDTH), lambda g, i: (g, cidx(i), 0, 0)),
        tok=pl.BlockSpec((CHUNK, SSM_GROUP_WIDTH), lambda g, i: (cidx(i), g)),
        bc_out=pl.BlockSpec((CHUNK, D_STATE), lambda g, i: (cidx(i), g)),
    )


def _ssd_fwd(xbc, dtp, a_pad, dtb_pad, dsk_e, e, d_inner):
    s = xbc.shape[0]
    ng = d_inner // SSM_GROUP_WIDTH
    nc = s // CHUNK

    def body(x_ref, b_ref, c_ref, dtp_ref, a_ref, dtb_ref, dsk_ref, e_ref, y_ref, st_ref, state):
        lane = lax.broadcasted_iota(jnp.int32, (CHUNK, LANES), 1)
        @pl.when(pl.program_id(1) == 0)
        def _():
            state[...] = jnp.zeros_like(state)

        ev = e_ref[...]
        q = _ssd_common(dtp_ref, a_ref, dtb_ref, x_ref, ev)
        bm, cm = b_ref[...], c_ref[...]
        cb = _dot_nt(cm, bm)
        s0 = state[...]
        st_ref[...] = s0
        y = _dot(cm, s0.astype(BF16)) * q["e_a_e"] + dsk_ref[...] * q["x"]
        xdt = q["xdt"]
        left = lane[:, :] < SSM_HEAD_DIM
        for j in range(HEADS_PER_SSM_GROUP // 2):
            sl = slice(j * LANES, (j + 1) * LANES)
            x2 = xdt[:, sl]
            m0 = (cb * _decay(q, 2 * j)).astype(BF16)
            m1 = (cb * _decay(q, 2 * j + 1)).astype(BF16)
            mcat = jnp.concatenate([m0, m1], axis=1)
            xbd = jnp.concatenate([jnp.where(left, x2, 0.0), jnp.where(left, 0.0, x2)], axis=0).astype(BF16)
            y_ref[:, sl] = (y[:, sl] + _dot(mcat, xbd)).astype(BF16)
        state[...] = s0 * q["e_a_e"][CHUNK - 1:CHUNK, :] + _dot_tn(bm, (q["to_end_e"] * xdt).astype(BF16))

    sp = _ssd_specs(ng, d_inner, False, nc)
    return pl.pallas_call(
        body, name="ssd_fwd", grid=(ng, nc),
        in_specs=[sp["xs"], sp["b"], sp["c"], sp["dtp"], sp["vec"], sp["vec"], sp["wide"], sp["e"]],
        out_specs=[sp["tok"], sp["st"]],
        out_shape=[jax.ShapeDtypeStruct((s, d_inner), BF16),
                   jax.ShapeDtypeStruct((ng, nc, D_STATE, SSM_GROUP_WIDTH), F32)],
        scratch_shapes=[pltpu.VMEM((D_STATE, SSM_GROUP_WIDTH), F32)],
        compiler_params=_cparams(("parallel", "arbitrary")),
    )(xbc, xbc, xbc, dtp, a_pad, dtb_pad, dsk_e, e)


def _ssd_bwd(xbc, dtp, a_pad, dtb_pad, dsk_e, e, states, dy, d_inner):
    s = xbc.shape[0]
    ng = d_inner // SSM_GROUP_WIDTH
    nc = s // CHUNK

    def body(x_ref, b_ref, c_ref, dtp_ref, a_ref, dtb_ref, dsk_ref, e_ref, st_ref, dy_ref,
             dx_ref, db_ref, dc_ref, ddt_ref, da_ref, ddtb_ref, dd_ref, dstate):
        lane = lax.broadcasted_iota(jnp.int32, (CHUNK, LANES), 1)
        sub = lax.broadcasted_iota(jnp.int32, (CHUNK, LANES), 0)
        @pl.when(pl.program_id(1) == 0)
        def _():
            dstate[...] = jnp.zeros_like(dstate)
            da_ref[...] = jnp.zeros_like(da_ref)
            ddtb_ref[...] = jnp.zeros_like(ddtb_ref)
            dd_ref[...] = jnp.zeros_like(dd_ref)

        ev = e_ref[...]
        q = _ssd_common(dtp_ref, a_ref, dtb_ref, x_ref, ev)
        bm, cm = b_ref[...], c_ref[...]
        cb = _dot_nt(cm, bm)
        x, xdt, e_a_e, to_end_e = q["x"], q["xdt"], q["e_a_e"], q["to_end_e"]
        s0 = st_ref[...]
        s0b = s0.astype(BF16)
        ds1 = dstate[...]
        ds1b = ds1.astype(BF16)
        dy = dy_ref[...].astype(F32)
        e_last_e = e_a_e[CHUNK - 1:CHUNK, :]

        dye = dy * e_a_e
        dyeb = dye.astype(BF16)
        cs0 = _dot(cm, s0b)
        dc = _dot_nt(dyeb, s0b)
        dstate[...] = e_last_e * ds1 + _dot_tn(cm, dyeb)
        da_col = _segsum(dye * cs0, ev)

        gmat = _dot(bm, ds1b)
        dxdt = to_end_e * gmat
        dte = _segsum(xdt * gmat, ev) * q["to_end"]
        db = _dot_nt((to_end_e * xdt).astype(BF16), ds1b)
        da_col = da_col - dte
        last_row = (jnp.sum(dte, axis=0, keepdims=True)
                    + q["e_a"][CHUNK - 1:CHUNK, :] * jnp.sum(_segsum(s0 * ds1, ev), axis=0, keepdims=True))

        left = lane < SSM_HEAD_DIM
        dcb = jnp.zeros((CHUNK, CHUNK), F32)
        row_acc = jnp.zeros((CHUNK, LANES), F32)
        for j in range(HEADS_PER_SSM_GROUP // 2):
            sl = slice(j * LANES, (j + 1) * LANES)
            x2 = xdt[:, sl].astype(BF16)
            dy2 = dy[:, sl]
            dyl = jnp.where(left, dy2, 0.0).astype(BF16)
            dyr = jnp.where(left, 0.0, dy2).astype(BF16)
            ms = []
            for hh, dyh in ((0, dyl), (1, dyr)):
                h = 2 * j + hh
                dec = _decay(q, h)
                m = cb * dec
                dm = _dot_nt(dyh, x2)
                dcb = dcb + dm * dec
                dseg = dm * m
                da_col = da_col + jnp.where(lane == h, jnp.sum(dseg, axis=1, keepdims=True), 0.0)
                row_acc = row_acc + jnp.where(sub == h, jnp.sum(dseg, axis=0, keepdims=True), 0.0)
                ms.append(m.astype(BF16))
            mst = jnp.concatenate(ms, axis=0)
            dyst = jnp.concatenate([dyl, dyr], axis=0)
            d2 = dxdt[:, sl] + _dot_tn(mst, dyst)
            dx_ref[:, sl] = (d2 * q["dt_e"][:, sl] + dsk_ref[:, sl] * dy2).astype(BF16)
            dxdt_x = d2 * x[:, sl]
            if j == 0:
                parts = [dxdt_x]
            else:
                parts.append(dxdt_x)
        dcbb = dcb.astype(BF16)
        dc_ref[...] = (dc + _dot(dcbb, bm)).astype(BF16)
        db_ref[...] = (db + _dot_tn(dcbb, cm)).astype(BF16)

        d_a = da_col - row_acc.T + jnp.where(sub == CHUNK - 1, last_row, 0.0)
        triu = (lax.broadcasted_iota(jnp.int32, (CHUNK, CHUNK), 1)
                >= lax.broadcasted_iota(jnp.int32, (CHUNK, CHUNK), 0)).astype(BF16)
        d_dta = _tri_dot(triu, d_a)
        ddt = d_dta * q["a"] + _segsum(jnp.concatenate(parts, axis=1), ev)
        ddt_raw = ddt * _sigmoid(q["raw"])
        ddt_ref[...] = ddt_raw
        da_ref[...] += jnp.sum(d_dta * q["dt"], axis=0, keepdims=True) * q["a"]
        ddtb_ref[...] += jnp.sum(ddt_raw, axis=0, keepdims=True)
        dd_ref[...] += jnp.sum(dy * x, axis=0, keepdims=True)

    sp = _ssd_specs(ng, d_inner, True, nc)
    return pl.pallas_call(
        body, name="ssd_bwd", grid=(ng, nc),
        in_specs=[sp["xs"], sp["b"], sp["c"], sp["dtp"], sp["vec"], sp["vec"], sp["wide"], sp["e"],
                  sp["st"], sp["tok"]],
        out_specs=[sp["tok"], sp["bc_out"], sp["bc_out"], sp["dtp"], sp["vec"], sp["vec"], sp["wide"]],
        out_shape=[jax.ShapeDtypeStruct((s, d_inner), BF16),
                   jax.ShapeDtypeStruct((s, ng * D_STATE), BF16),
                   jax.ShapeDtypeStruct((s, ng * D_STATE), BF16),
                   jax.ShapeDtypeStruct((ng, s, LANES), F32),
                   jax.ShapeDtypeStruct((ng, 1, LANES), F32),
                   jax.ShapeDtypeStruct((ng, 1, LANES), F32),
                   jax.ShapeDtypeStruct((ng, 1, SSM_GROUP_WIDTH), F32)],
        scratch_shapes=[pltpu.VMEM((D_STATE, SSM_GROUP_WIDTH), F32)],
        compiler_params=_cparams(("parallel", "arbitrary")),
    )(xbc, xbc, xbc, dtp, a_pad, dtb_pad, dsk_e, e, states, dy)


def _gate_norm_fwd(y, pzx, norm_w):
    s, di = y.shape
    ng = di // SSM_GROUP_WIDTH
    tm = _pick(s, (512, 256, 128))

    def body(y_ref, z_ref, w_ref, o_ref):
        z = z_ref[...].astype(F32)
        y2 = y_ref[...].astype(F32) * (z * _sigmoid(z))
        r = lax.rsqrt(jnp.mean(y2 * y2, axis=1, keepdims=True) + RMS_EPS)
        o_ref[...] = (y2 * r * w_ref[...]).astype(BF16)

    blk = pl.BlockSpec((tm, SSM_GROUP_WIDTH), lambda i, g: (i, g))
    return pl.pallas_call(
        body, name="gate_norm_fwd", grid=(s // tm, ng),
        in_specs=[blk, blk, pl.BlockSpec((1, SSM_GROUP_WIDTH), lambda i, g: (0, g))],
        out_specs=blk, out_shape=jax.ShapeDtypeStruct((s, di), BF16),
        compiler_params=_cparams(("parallel", "parallel")),
    )(y, pzx, norm_w)


def _gate_norm_bwd(dy3, y, pzx, norm_w):
    s, di = y.shape
    ng = di // SSM_GROUP_WIDTH
    tm = _pick(s, (512, 256, 128))

    def body(d_ref, y_ref, z_ref, w_ref, dy_ref, dz_ref, dw_ref):
        @pl.when(pl.program_id(1) == 0)
        def _():
            dw_ref[...] = jnp.zeros_like(dw_ref)

        z = z_ref[...].astype(F32)
        yv = y_ref[...].astype(F32)
        sg = _sigmoid(z)
        sz = z * sg
        y2 = yv * sz
        r = lax.rsqrt(jnp.mean(y2 * y2, axis=1, keepdims=True) + RMS_EPS)
        nrm = y2 * r
        d3 = d_ref[...].astype(F32)
        dw_ref[...] += jnp.sum(d3 * nrm, axis=0, keepdims=True)
        dn = d3 * w_ref[...]
        dy2 = r * (dn - nrm * jnp.mean(dn * nrm, axis=1, keepdims=True))
        dy_ref[...] = (dy2 * sz).astype(BF16)
        dz_ref[...] = (dy2 * yv * (sg * (1.0 + z * (1.0 - sg)))).astype(BF16)

    blk = pl.BlockSpec((tm, SSM_GROUP_WIDTH), lambda g, i: (i, g))
    vec = pl.BlockSpec((1, SSM_GROUP_WIDTH), lambda g, i: (0, g))
    return pl.pallas_call(
        body, name="gate_norm_bwd", grid=(ng, s // tm),
        in_specs=[blk, blk, blk, vec], out_specs=[blk, blk, vec],
        out_shape=[jax.ShapeDtypeStruct((s, di), BF16), jax.ShapeDtypeStruct((s, di), BF16),
                   jax.ShapeDtypeStruct((1, di), F32)],
        compiler_params=_cparams(("parallel", "arbitrary")),
    )(dy3, y, pzx, norm_w)


_ANY = pl.BlockSpec(memory_space=pl.ANY)


def _place():
    x, y, c = lax.axis_index("x"), lax.axis_index("y"), lax.axis_index("c")
    chips = [(1 - x, y), (x, 1 - y), (1 - x, 1 - y)]
    return x, y, c, chips


def _gather_weights(shards):
    n = len(shards)

    def body(*refs):
        ins, outs = refs[:n], refs[n:2 * n]
        send_sems, recv_sems, loc_sems = refs[2 * n:]
        x, y, c, chips = _place()
        k = 2 * x + y
        sib = (x, y, 1 - c)

        def half(w, slot, hc):
            hr = shards[w].shape[0] // 2
            return outs[w].at[slot, pl.ds(hc * hr, hr)]

        def copy(w, j, src, dst, to):
            return pltpu.make_async_remote_copy(
                src_ref=src, dst_ref=dst, send_sem=send_sems.at[w, j], recv_sem=recv_sems.at[w, j],
                device_id=to, device_id_type=MESH)

        local, first, passed = [], [], []
        for w in range(n):
            hr = shards[w].shape[0] // 2
            cp = pltpu.make_async_copy(ins[w], outs[w].at[k], loc_sems.at[w])
            cp.start()
            local.append(cp)
            for j, chip in enumerate(chips):
                cp = copy(w, j, ins[w].at[pl.ds(c * hr, hr)], half(w, k, c), (*chip, c))
                cp.start()
                first.append(cp)
        for w in range(n):
            for j, (cx, cy) in enumerate(chips):
                kj = 2 * cx + cy
                copy(w, j, half(w, kj, c), half(w, kj, c), sib).wait_recv()
                cp = copy(w, 3 + j, half(w, kj, c), half(w, kj, c), sib)
                cp.start()
                passed.append(cp)
        for w in range(n):
            for j, (cx, cy) in enumerate(chips):
                kj = 2 * cx + cy
                copy(w, 3 + j, half(w, kj, 1 - c), half(w, kj, 1 - c), sib).wait_recv()
        for cp in first + passed:
            cp.wait_send()
        for cp in local:
            cp.wait()

    return pl.pallas_call(
        body, name="gather_weights",
        in_specs=[_ANY] * n, out_specs=[_ANY] * n,
        out_shape=[jax.ShapeDtypeStruct((N_CHIPS,) + s.shape, s.dtype) for s in shards],
        scratch_shapes=[pltpu.SemaphoreType.DMA((n, 6)), pltpu.SemaphoreType.DMA((n, 6)),
                        pltpu.SemaphoreType.DMA((n,))],
    )(*shards)


def _swap_halves(gs):
    n = len(gs)

    def body(*refs):
        ins, outs = refs[:n], refs[n:2 * n]
        send_sems, recv_sems = refs[2 * n:]
        x, y, c, _ = _place()
        cps = []
        for w in range(n):
            hr = gs[w].shape[1] // 2
            cp = pltpu.make_async_remote_copy(
                src_ref=ins[w].at[:, pl.ds((1 - c) * hr, hr)], dst_ref=outs[w],
                send_sem=send_sems.at[w], recv_sem=recv_sems.at[w],
                device_id=(x, y, 1 - c), device_id_type=MESH)
            cp.start()
            cps.append(cp)
        for cp in cps:
            cp.wait()

    return pl.pallas_call(
        body, name="grads_to_sibling",
        in_specs=[_ANY] * n, out_specs=[_ANY] * n,
        out_shape=[jax.ShapeDtypeStruct((g.shape[0], g.shape[1] // 2, g.shape[2]), g.dtype) for g in gs],
        scratch_shapes=[pltpu.SemaphoreType.DMA((n,)), pltpu.SemaphoreType.DMA((n,))],
    )(*gs)


def _scatter_to_chips(ts):
    n = len(ts)

    def body(*refs):
        ins, outs = refs[:n], refs[n:2 * n]
        send_sems, recv_sems, loc_sems = refs[2 * n:]
        x, y, c, chips = _place()
        k = 2 * x + y
        cps, local = [], []
        for w in range(n):
            cp = pltpu.make_async_copy(ins[w].at[k], outs[w].at[k], loc_sems.at[w])
            cp.start()
            local.append(cp)
            for j, (cx, cy) in enumerate(chips):
                cp = pltpu.make_async_remote_copy(
                    src_ref=ins[w].at[2 * cx + cy], dst_ref=outs[w].at[k],
                    send_sem=send_sems.at[w, j], recv_sem=recv_sems.at[w, j],
                    device_id=(cx, cy, c), device_id_type=MESH)
                cp.start()
                cps.append(cp)
        for w in range(n):
            for j, (cx, cy) in enumerate(chips):
                kj = 2 * cx + cy
                pltpu.make_async_remote_copy(
                    src_ref=ins[w].at[kj], dst_ref=outs[w].at[kj],
                    send_sem=send_sems.at[w, j], recv_sem=recv_sems.at[w, j],
                    device_id=(cx, cy, c), device_id_type=MESH).wait_recv()
        for cp in cps:
            cp.wait_send()
        for cp in local:
            cp.wait()

    return pl.pallas_call(
        body, name="grads_to_chips",
        in_specs=[_ANY] * n, out_specs=[_ANY] * n,
        out_shape=[jax.ShapeDtypeStruct(t.shape, t.dtype) for t in ts],
        scratch_shapes=[pltpu.SemaphoreType.DMA((n, 3)), pltpu.SemaphoreType.DMA((n, 3)),
                        pltpu.SemaphoreType.DMA((n,))],
    )(*ts)


def _join_halves(fs):
    n = len(fs)

    def body(*refs):
        ins, outs = refs[:n], refs[n:2 * n]
        send_sems, recv_sems, loc_sems = refs[2 * n:]
        x, y, c, _ = _place()
        cps, local = [], []
        for w in range(n):
            hr = fs[w].shape[0]
            cp = pltpu.make_async_copy(ins[w], outs[w].at[pl.ds(c * hr, hr)], loc_sems.at[w])
            cp.start()
            local.append(cp)
            cp = pltpu.make_async_remote_copy(
                src_ref=ins[w], dst_ref=outs[w].at[pl.ds(c * hr, hr)],
                send_sem=send_sems.at[w], recv_sem=recv_sems.at[w],
                device_id=(x, y, 1 - c), device_id_type=MESH)
            cp.start()
            cps.append(cp)
        for w in range(n):
            hr = fs[w].shape[0]
            pltpu.make_async_remote_copy(
                src_ref=ins[w], dst_ref=outs[w].at[pl.ds((1 - c) * hr, hr)],
                send_sem=send_sems.at[w], recv_sem=recv_sems.at[w],
                device_id=(x, y, 1 - c), device_id_type=MESH).wait_recv()
        for cp in cps:
            cp.wait_send()
        for cp in local:
            cp.wait()

    return pl.pallas_call(
        body, name="grads_join_halves",
        in_specs=[_ANY] * n, out_specs=[_ANY] * n,
        out_shape=[jax.ShapeDtypeStruct((2 * f.shape[0], f.shape[1]), f.dtype) for f in fs],
        scratch_shapes=[pltpu.SemaphoreType.DMA((n,)), pltpu.SemaphoreType.DMA((n,)),
                        pltpu.SemaphoreType.DMA((n,))],
    )(*fs)


def _all_gather_small(v, reduce, name):
    r, l = v.shape

    def body(v_ref, o_ref, *rest):
        if reduce:
            buf, send_sems, recv_sems = rest
        else:
            buf = o_ref
            send_sems, recv_sems = rest
        x, y, c, _ = _place()
        me = 4 * x + 2 * y + c
        buf[me] = v_ref[...]
        cps = []
        for d in range(1, N_DEV):
            peer = (x if d & 4 == 0 else 1 - x, y if d & 2 == 0 else 1 - y, c if d & 1 == 0 else 1 - c)
            cp = pltpu.make_async_remote_copy(
                src_ref=v_ref, dst_ref=buf.at[me], send_sem=send_sems.at[d - 1], recv_sem=recv_sems.at[d - 1],
                device_id=peer, device_id_type=MESH)
            cp.start()
            cps.append((cp, peer))
        for d, (cp, (px, py, pc)) in enumerate(cps, start=1):
            pltpu.make_async_remote_copy(
                src_ref=v_ref, dst_ref=buf.at[4 * px + 2 * py + pc], send_sem=send_sems.at[d - 1],
                recv_sem=recv_sems.at[d - 1], device_id=(px, py, pc), device_id_type=MESH).wait_recv()
        for cp, _ in cps:
            cp.wait_send()
        if reduce:
            acc = buf[0]
            for i in range(1, N_DEV):
                acc = acc + buf[i]
            o_ref[...] = acc

    vm = pl.BlockSpec(memory_space=pltpu.VMEM)
    out_shape = jax.ShapeDtypeStruct((r, l) if reduce else (N_DEV, r, l), F32)
    scratch = ([pltpu.VMEM((N_DEV, r, l), F32)] if reduce else []) + [
        pltpu.SemaphoreType.DMA((N_DEV - 1,)), pltpu.SemaphoreType.DMA((N_DEV - 1,))]
    return pl.pallas_call(
        body, name=name, in_specs=[vm], out_specs=vm, out_shape=out_shape, scratch_shapes=scratch,
    )(v)


_BLOCK_BYTES = 3 * 512 * 1024


def _rows_per_block(r, cn, itemsize=4):
    best = 8
    for t in range(8, r + 1, 8):
        if r % t == 0 and t * cn * itemsize <= _BLOCK_BYTES:
            best = t
    return best


def _add_sibling_half(g4, recv, cvec, name):
    ns, r, cn = g4.shape
    hr = r // 2
    tr = _rows_per_block(hr, cn)
    nrb = hr // tr

    def body(c_ref, a_ref, b_ref, o_ref):
        o_ref[...] = (a_ref[...].astype(F32) + b_ref[...].astype(F32)).astype(o_ref.dtype)

    grid_spec = pltpu.PrefetchScalarGridSpec(
        num_scalar_prefetch=1, grid=(ns, nrb),
        in_specs=[pl.BlockSpec((None, tr, cn), lambda j, i, c: (j, c[0] * nrb + i, 0)),
                  pl.BlockSpec((None, tr, cn), lambda j, i, c: (j, i, 0))],
        out_specs=pl.BlockSpec((None, tr, cn), lambda j, i, c: (j, i, 0)))
    return pl.pallas_call(
        body, name=name, grid_spec=grid_spec, out_shape=jax.ShapeDtypeStruct((ns, hr, cn), BF16),
        compiler_params=_cparams(("parallel", "parallel")),
    )(cvec, g4, recv)


def _sum_chips(r4, name):
    ns, hr, cn = r4.shape
    tr = _rows_per_block(hr, cn)

    def body(a_ref, o_ref):
        acc = a_ref[0].astype(F32)
        for j in range(1, ns):
            acc = acc + a_ref[j].astype(F32)
        o_ref[...] = acc

    return pl.pallas_call(
        body, name=name, grid=(hr // tr,),
        in_specs=[pl.BlockSpec((ns, tr, cn), lambda i: (0, i, 0))],
        out_specs=pl.BlockSpec((tr, cn), lambda i: (i, 0)),
        out_shape=jax.ShapeDtypeStruct((hr, cn), F32),
        compiler_params=_cparams(("parallel",)),
    )(r4)


def _adamw(w, g, m, v, name):
    r, cn = w.shape
    tr = _rows_per_block(r, cn)
    c1 = 1.0 - ADAM_B1 ** ADAM_STEP
    c2 = 1.0 - ADAM_B2 ** ADAM_STEP

    def body(w_ref, g_ref, m_ref, v_ref, go_ref, d_ref, mo_ref, vo_ref):
        gv = g_ref[...]
        mn = ADAM_B1 * m_ref[...] + (1.0 - ADAM_B1) * gv
        vn = ADAM_B2 * v_ref[...] + (1.0 - ADAM_B2) * (gv * gv)
        go_ref[...] = gv
        mo_ref[...] = mn
        vo_ref[...] = vn
        d_ref[...] = -ADAM_LR * ((mn / c1) / (jnp.sqrt(vn / c2) + ADAM_EPS) + ADAM_WD * w_ref[...])

    spec = pl.BlockSpec((tr, cn), lambda i: (i, 0))
    return pl.pallas_call(
        body, name=name, grid=(r // tr,), in_specs=[spec] * 4, out_specs=[spec] * 4,
        out_shape=[jax.ShapeDtypeStruct((r, cn), F32)] * 4,
        compiler_params=_cparams(("parallel",)),
    )(w, g, m, v)


def _pack(arrs):
    flat = jnp.concatenate([a.reshape(-1).astype(F32) for a in arrs])
    n = flat.shape[0]
    tot = -(-n // (8 * LANES)) * (8 * LANES)
    return jnp.pad(flat, (0, tot - n)).reshape(tot // LANES, LANES)


def _unpack(packed, shapes):
    flat = packed.reshape(-1)
    out, off = [], 0
    for shp in shapes:
        sz = int(np.prod(shp))
        out.append(flat[off:off + sz].reshape(shp))
        off += sz
    return out


def _local_step(x2, tgt, wa4, woa4, ws4, wos4, conv_w_f, conv_b_f, norm_w_f, rel_bias, dt_bias, a_log, d_skip,
                ln_g, ln_b):
    s, d = x2.shape
    d_attn = woa4.shape[1]
    hpg = d_attn // HEAD_DIM
    d_inner = wos4.shape[1] * N_CHIPS
    ng = d_inner // SSM_GROUP_WIDTH
    n_heads = dt_bias.shape[1]
    conv_dim = conv_w_f.shape[1]
    assert n_heads == ng * HEADS_PER_SSM_GROUP and conv_dim == d_inner + 2 * ng * D_STATE
    assert wa4.shape[2] * N_CHIPS == 10 * d_attn
    ssm_cols = ws4.shape[2]
    ws = jnp.concatenate([ws4[j] for j in range(N_CHIPS)], axis=1)
    wzx, wdt = ws[:, :d_inner + conv_dim], ws[:, d_inner + conv_dim:]
    wos = wos4.reshape(d_inner, d)

    xb = _cast_bf16(x2, "cast_x")
    pa = _mm_nn_sharded(xb, wa4, BF16, "mm_in_attn")
    buckets = _bucket_tiles()
    bias = _bias_expand(rel_bias, buckets, hpg)
    og, lg = [], []
    for g, (_, dil) in enumerate(ATTN_PATTERNS):
        o_, l_ = _attn_fwd(pa, bias, g, dil, hpg)
        og.append(o_)
        lg.append(l_)
    o, lse, yat = _attn_combine(og, lg, pa, hpg)
    h0 = _mm_nn_sharded(yat, woa4, F32, "mm_out_attn")
    g0, b0, g1, b1 = ln_g[0:1], ln_b[0:1], ln_g[1:2], ln_b[1:2]
    xhat0, rstd0, x1b = _ln_fwd(x2, h0, g0, b0, "ln0_fwd")

    pzx = _mm_nn(x1b, wzx, BF16, "mm_in_ssm")
    dt_raw = _mm_nn(x1b, wdt, F32, "mm_in_dt")

    def pad_heads(t):
        t = t.reshape(t.shape[0], ng, HEADS_PER_SSM_GROUP).transpose(1, 0, 2)
        return jnp.pad(t, ((0, 0), (0, 0), (0, LANES - HEADS_PER_SSM_GROUP)))

    def unpad_heads(t):
        return t[:, :, :HEADS_PER_SSM_GROUP].transpose(1, 0, 2).reshape(t.shape[1], n_heads)

    dtp = pad_heads(dt_raw)
    alog_p, dtb_p = pad_heads(a_log), pad_heads(dt_bias)
    dsk_e = jnp.repeat(d_skip.reshape(ng, 1, HEADS_PER_SSM_GROUP), SSM_HEAD_DIM, axis=2)
    e = _expand_matrix()
    xbc = _conv_fwd(pzx, conv_w_f, conv_b_f, d_inner)
    y_ssd, states = _ssd_fwd(xbc, dtp, alog_p, dtb_p, dsk_e, e, d_inner)
    y3 = _gate_norm_fwd(y_ssd, pzx, norm_w_f)
    h1 = _mm_nn(y3, wos, F32, "mm_out_ssm")
    xhat1, rstd1, dy2, row_sq = _ln_fwd(xhat0, h1, g1, b1, "ln1_fwd_loss", affine_in=(g0, b0), target=tgt)
    loss_local = 0.5 * jnp.sum(row_sq) / d

    du1, du1b, dg1, db1 = _ln_bwd(dy2, xhat1, rstd1, g1, "ln1_bwd")
    dy3 = _mm_nt(du1b, wos, BF16, "mm_d_y3")
    g_wos = _mm_tn(y3, du1b, BF16, "mm_g_w_out_ssm").reshape(N_CHIPS, d_inner // N_CHIPS, d)
    dy_ssd, dz, d_nw = _gate_norm_bwd(dy3, y_ssd, pzx, norm_w_f)
    dxs, dbm, dcm, ddtp, d_alog, d_dtb, d_dsk = _ssd_bwd(xbc, dtp, alog_p, dtb_p, dsk_e, e, states, dy_ssd, d_inner)
    dpre, d_cw, d_cb = _conv_bwd_a(pzx, jnp.concatenate([dxs, dbm, dcm], axis=1), conv_w_f, conv_b_f, d_inner)
    dpzx = jnp.concatenate([dz, _conv_bwd_b(dpre, conv_w_f)], axis=1)
    ddt_raw = unpad_heads(ddtp)
    t1 = _mm_nt(ddt_raw, wdt, F32, "mm_d_x1_dt", add=du1, add_scale=DEEPNORM_ALPHA)
    dx1 = _mm_nt(dpzx, wzx, F32, "mm_d_x1", add=t1)
    g_ws = jnp.concatenate([_mm_tn(x1b, dpzx, BF16, "mm_g_w_in_ssm"), _mm_tn(x1b, ddt_raw, BF16, "mm_g_w_dt")], axis=1)
    g_ws = jnp.stack([g_ws[:, j * ssm_cols:(j + 1) * ssm_cols] for j in range(N_CHIPS)])

    du0, du0b, dg0, db0 = _ln_bwd(dx1, xhat0, rstd0, g0, "ln0_bwd")
    dyat = _mm_nt_sharded_k(du0b, woa4, BF16, "mm_d_yat")
    g_woa = _mm_tn(yat, du0b, BF16, "mm_g_w_out_attn", shard_cols=d // N_CHIPS)
    do, delta, dgate = _attn_pre_bwd(dyat, o, pa, hpg)
    pieces, dbt = [], []
    for g, (_, dil) in enumerate(ATTN_PATTERNS):
        dq, dk, dv, db_ = _attn_bwd(pa, bias, do, lse, delta, g, dil, hpg)
        pieces += [dq, dk, dv]
        dbt.append(db_)
    dpa = jnp.concatenate(pieces + [dgate], axis=1)
    grad_x = _mm_nt_sharded_k(dpa, wa4, F32, "mm_d_x0", add=du0, add_scale=DEEPNORM_ALPHA)[None]
    g_wa = _mm_tn(xb, dpa, BF16, "mm_g_w_in_attn", shard_cols=wa4.shape[2])
    d_rel = _bias_reduce(jnp.stack(dbt), buckets, hpg)[:, :, 0].T

    d_dsk_h = d_dsk.reshape(n_heads, SSM_HEAD_DIM).sum(axis=1)
    small_full = [d_rel, d_cw, d_cb, unpad_heads(d_dtb), unpad_heads(d_alog), d_dsk_h[None], d_nw,
                  jnp.concatenate([dg0, dg1], axis=0), jnp.concatenate([db0, db1], axis=0)]
    return loss_local, grad_x, [g_wa, g_woa, g_ws, g_wos], small_full


def kernel(x, w_in_attn, w_out_attn, rel_bias, w_in_ssm, conv_w, conv_b, dt_bias, a_log, d_skip, ssm_norm_w, w_out_ssm, ln_g, ln_b, loss_target, m_w_in_attn, m_w_out_attn, m_rel_bias, m_w_in_ssm, m_conv_w, m_conv_b, m_dt_bias, m_a_log, m_d_skip, m_ssm_norm_w, m_w_out_ssm, m_ln_g, m_ln_b, v_w_in_attn, v_w_out_attn, v_rel_bias, v_w_in_ssm, v_conv_w, v_conv_b, v_dt_bias, v_a_log, v_d_skip, v_ssm_norm_w, v_w_out_ssm, v_ln_g, v_ln_b):
    xi, yi, ci = lax.axis_index("x"), lax.axis_index("y"), lax.axis_index("c")
    chip = 2 * xi + yi
    cvec = jnp.reshape(ci, (1,)).astype(jnp.int32)

    wa4, woa4, ws4, wos4 = _gather_weights([
        _cast_bf16(w_in_attn[0], "cast_w_in_attn"), _cast_bf16(w_out_attn[0], "cast_w_out_attn"),
        _cast_bf16(w_in_ssm[0], "cast_w_in_ssm"), _cast_bf16(w_out_ssm[0], "cast_w_out_ssm")])
    cw_l, cb_l, nw_l = conv_w[0], conv_b[0], ssm_norm_w[0]
    vec_shapes = [cw_l.shape, cb_l.shape, nw_l.shape]
    vec_all = _all_gather_small(_pack([cw_l, cb_l, nw_l]), False, "gather_vectors")
    parts = [_unpack(vec_all[2 * j], vec_shapes) for j in range(N_CHIPS)]
    conv_w_f = jnp.concatenate([p[0] for p in parts], axis=1)
    conv_b_f = jnp.concatenate([p[1] for p in parts], axis=0)[None]
    norm_w_f = jnp.concatenate([p[2] for p in parts], axis=0)[None]

    loss_local, grad_x, gs, small_full = _local_step(
        x[0], loss_target[0], wa4, woa4, ws4, wos4, conv_w_f, conv_b_f, norm_w_f, rel_bias, dt_bias, a_log,
        d_skip, ln_g, ln_b)
    loss = lax.psum(loss_local, ("x", "y", "c"))

    names = ["w_in_attn", "w_out_attn", "w_in_ssm", "w_out_ssm"]
    rs = _swap_halves(gs)
    ts = [_add_sibling_half(g_, r_, cvec, "add_sibling_" + nm) for g_, r_, nm in zip(gs, rs, names)]
    fs = [_sum_chips(r_, "sum_chips_" + nm) for r_, nm in zip(_scatter_to_chips(ts), names)]
    full = _join_halves(fs)
    big = {}
    for nm, gf, w_, m_, v_ in zip(names, full, (w_in_attn, w_out_attn, w_in_ssm, w_out_ssm),
                                  (m_w_in_attn, m_w_out_attn, m_w_in_ssm, m_w_out_ssm),
                                  (v_w_in_attn, v_w_out_attn, v_w_in_ssm, v_w_out_ssm)):
        big[nm] = [t[None] for t in _adamw(w_[0], gf, m_[0], v_[0], "adamw_" + nm)]

    summed = _unpack(_all_gather_small(_pack(small_full), True, "reduce_small_grads"),
                     [t.shape for t in small_full])
    s_rel, s_cw, s_cb, s_dtb, s_alog, s_dsk, s_nw, s_lng, s_lnb = summed
    cwc, nwc = conv_w.shape[2], ssm_norm_w.shape[1]
    s_cw = lax.dynamic_slice_in_dim(s_cw, chip * cwc, cwc, axis=1)[None]
    s_cb = lax.dynamic_slice_in_dim(s_cb, chip * cwc, cwc, axis=1)
    s_nw = lax.dynamic_slice_in_dim(s_nw, chip * nwc, nwc, axis=1)
    small_names = ["rel_bias", "conv_w", "conv_b", "dt_bias", "a_log", "d_skip", "ssm_norm_w", "ln_g", "ln_b"]
    small_g = [s_rel, s_cw, s_cb, s_dtb, s_alog, s_dsk, s_nw, s_lng, s_lnb]
    small_w = [rel_bias, conv_w, conv_b, dt_bias, a_log, d_skip, ssm_norm_w, ln_g, ln_b]
    small_m = [m_rel_bias, m_conv_w, m_conv_b, m_dt_bias, m_a_log, m_d_skip, m_ssm_norm_w, m_ln_g, m_ln_b]
    small_v = [v_rel_bias, v_conv_w, v_conv_b, v_dt_bias, v_a_log, v_d_skip, v_ssm_norm_w, v_ln_g, v_ln_b]
    shapes = [t.shape for t in small_w]
    res = _adamw(_pack(small_w), _pack(small_g), _pack(small_m), _pack(small_v), "adamw_small")
    small = {nm: [] for nm in small_names}
    for packed in res:
        for nm, t in zip(small_names, _unpack(packed, shapes)):
            small[nm].append(t)

    order = ["w_in_attn", "w_out_attn", "rel_bias", "w_in_ssm", "conv_w", "conv_b", "dt_bias", "a_log",
             "d_skip", "ssm_norm_w", "w_out_ssm", "ln_g", "ln_b"]
    table = {**big, **small}
    outs = [loss, grad_x]
    for kind in range(4):
        outs += [table[nm][kind] for nm in order]
    return tuple(outs)
```

```python
import functools
import math

import numpy as np
import jax
import jax.numpy as jnp
from jax import lax
from jax.experimental import pallas as pl
from jax.experimental.pallas import tpu as pltpu

F32 = jnp.float32
BF16 = jnp.bfloat16
MESH = pl.DeviceIdType.MESH

ATTN_PATTERNS = ((128, 1), (512, 4), (2048, 16))
N_GROUPS_ATTN = 3
HEAD_DIM = 128
ATTN_BLOCK = 128
NUM_BUCKETS = 32
MAX_DISTANCE = 2048
SSM_HEAD_DIM = 64
HEADS_PER_SSM_GROUP = 16
SSM_GROUP_WIDTH = HEADS_PER_SSM_GROUP * SSM_HEAD_DIM
D_STATE = 128
CONV_WIDTH = 4
CHUNK = 128
DEPTH = 2
DEEPNORM_ALPHA = (2 * DEPTH) ** 0.25
LN_EPS = 1e-5
RMS_EPS = 1e-5
NEG_INF = -1e30
ADAM_LR = 0.001
ADAM_B1 = 0.9
ADAM_B2 = 0.999
ADAM_EPS = 1e-08
ADAM_WD = 0.01
ADAM_STEP = 10

N_CHIPS = 4
N_DEV = 8

VMEM_LIMIT_V7X = 56 * 1024 * 1024
LANES = 128


def _cparams(sem=None):
    return pltpu.CompilerParams(dimension_semantics=sem, vmem_limit_bytes=VMEM_LIMIT_V7X)


def _sigmoid(x):
    return 1.0 / (1.0 + jnp.exp(-x))


def _dot(a, b):
    return jnp.dot(a, b, preferred_element_type=F32)


def _dot_nt(a, b):
    return lax.dot_general(a, b, (((1,), (1,)), ((), ())), preferred_element_type=F32)


def _dot_tn(a, b):
    return lax.dot_general(a, b, (((0,), (0,)), ((), ())), preferred_element_type=F32)


def _split2(x):
    hi = x.astype(BF16)
    lo = (x - hi.astype(F32)).astype(BF16)
    return hi, lo


def _split3(x):
    hi = x.astype(BF16)
    r = x - hi.astype(F32)
    mid = r.astype(BF16)
    lo = (r - mid.astype(F32)).astype(BF16)
    return hi, mid, lo


def _matmul(a, b, *, mode, grid, a_spec, b_spec, out_shape, out_spec, tile, name,
            add=None, add_spec=None, add_scale=1.0):
    nk = grid[2]
    tm, tn = tile
    dot = {"nn": _dot, "nt": _dot_nt, "tn": _dot_tn}[mode]
    has_add = add is not None

    def body(*refs):
        if has_add:
            a_ref, b_ref, add_ref, o_ref, acc_ref = refs
        else:
            a_ref, b_ref, o_ref, acc_ref = refs
        k = pl.program_id(2)

        @pl.when(k == 0)
        def _():
            acc_ref[...] = jnp.zeros_like(acc_ref)

        acc_ref[...] += dot(a_ref[...].astype(BF16), b_ref[...].astype(BF16))

        @pl.when(k == nk - 1)
        def _():
            r = acc_ref[...]
            if has_add:
                r = r + add_scale * add_ref[...].astype(F32)
            o_ref[...] = r.astype(o_ref.dtype)

    in_specs = [a_spec, b_spec] + ([add_spec] if has_add else [])
    args = (a, b) + ((add,) if has_add else ())
    return pl.pallas_call(
        body, name=name, grid=grid, in_specs=in_specs, out_specs=out_spec, out_shape=out_shape,
        scratch_shapes=[pltpu.VMEM((tm, tn), F32)],
        compiler_params=_cparams(("parallel", "parallel", "arbitrary")),
    )(*args)


def _pick(n, pref):
    for t in pref:
        if n % t == 0:
            return t
    return n


_TILE_PREF = (1024, 512, 256, 128)


def _mm_nn_sharded(a, w4, out_dtype, name):
    m, k = a.shape
    _, _, nn = w4.shape
    tm, tk, tn = _pick(m, _TILE_PREF), _pick(k, _TILE_PREF), _pick(nn, _TILE_PREF)
    npb = nn // tn
    return _matmul(
        a, w4, mode="nn", grid=(m // tm, N_CHIPS * npb, k // tk), tile=(tm, tn), name=name,
        a_spec=pl.BlockSpec((tm, tk), lambda i, j, kk: (i, kk)),
        b_spec=pl.BlockSpec((None, tk, tn), lambda i, j, kk: (j // npb, kk, j % npb)),
        out_shape=jax.ShapeDtypeStruct((m, N_CHIPS * nn), out_dtype),
        out_spec=pl.BlockSpec((tm, tn), lambda i, j, kk: (i, j)))


def _mm_nn(a, b, out_dtype, name):
    m, k = a.shape
    _, n = b.shape
    tm, tk, tn = _pick(m, _TILE_PREF), _pick(k, _TILE_PREF), _pick(n, _TILE_PREF)
    return _matmul(
        a, b, mode="nn", grid=(m // tm, n // tn, k // tk), tile=(tm, tn), name=name,
        a_spec=pl.BlockSpec((tm, tk), lambda i, j, kk: (i, kk)),
        b_spec=pl.BlockSpec((tk, tn), lambda i, j, kk: (kk, j)),
        out_shape=jax.ShapeDtypeStruct((m, n), out_dtype),
        out_spec=pl.BlockSpec((tm, tn), lambda i, j, kk: (i, j)))


def _mm_nt(a, b, out_dtype, name, add=None, add_scale=1.0):
    m, k = a.shape
    n, _ = b.shape
    tm, tk, tn = _pick(m, _TILE_PREF), _pick(k, _TILE_PREF), _pick(n, _TILE_PREF)
    return _matmul(
        a, b, mode="nt", grid=(m // tm, n // tn, k // tk), tile=(tm, tn), name=name,
        a_spec=pl.BlockSpec((tm, tk), lambda i, j, kk: (i, kk)),
        b_spec=pl.BlockSpec((tn, tk), lambda i, j, kk: (j, kk)),
        out_shape=jax.ShapeDtypeStruct((m, n), out_dtype),
        out_spec=pl.BlockSpec((tm, tn), lambda i, j, kk: (i, j)),
        add=add, add_spec=pl.BlockSpec((tm, tn), lambda i, j, kk: (i, j)), add_scale=add_scale)


def _mm_nt_sharded_k(a, w4, out_dtype, name, add=None, add_scale=1.0):
    m, _ = a.shape
    _, n, kn = w4.shape
    tm, tk, tn = _pick(m, _TILE_PREF), _pick(kn, _TILE_PREF), _pick(n, _TILE_PREF)
    kpb = kn // tk
    return _matmul(
        a, w4, mode="nt", grid=(m // tm, n // tn, N_CHIPS * kpb), tile=(tm, tn), name=name,
        a_spec=pl.BlockSpec((tm, tk), lambda i, j, kk: (i, kk)),
        b_spec=pl.BlockSpec((None, tn, tk), lambda i, j, kk: (kk // kpb, j, kk % kpb)),
        out_shape=jax.ShapeDtypeStruct((m, n), out_dtype),
        out_spec=pl.BlockSpec((tm, tn), lambda i, j, kk: (i, j)),
        add=add, add_spec=pl.BlockSpec((tm, tn), lambda i, j, kk: (i, j)), add_scale=add_scale)


def _mm_nt_sharded_n(a, w4, out_dtype, name):
    m, k = a.shape
    _, nn, _ = w4.shape
    tm, tk, tn = _pick(m, _TILE_PREF), _pick(k, _TILE_PREF), _pick(nn, _TILE_PREF)
    npb = nn // tn
    return _matmul(
        a, w4, mode="nt", grid=(m // tm, N_CHIPS * npb, k // tk), tile=(tm, tn), name=name,
        a_spec=pl.BlockSpec((tm, tk), lambda i, j, kk: (i, kk)),
        b_spec=pl.BlockSpec((None, tn, tk), lambda i, j, kk: (j // npb, j % npb, kk)),
        out_shape=jax.ShapeDtypeStruct((m, N_CHIPS * nn), out_dtype),
        out_spec=pl.BlockSpec((tm, tn), lambda i, j, kk: (i, j)))


def _mm_tn(a, b, out_dtype, name, shard_cols=None):
    k, m = a.shape
    _, n = b.shape
    nn = n if shard_cols is None else shard_cols
    tm, tk, tn = _pick(m, _TILE_PREF), _pick(k, _TILE_PREF), _pick(nn, _TILE_PREF)
    if shard_cols is None:
        out_shape = jax.ShapeDtypeStruct((m, n), out_dtype)
        out_spec = pl.BlockSpec((tm, tn), lambda i, j, kk: (i, j))
    else:
        npb = nn // tn
        out_shape = jax.ShapeDtypeStruct((n // nn, m, nn), out_dtype)
        out_spec = pl.BlockSpec((None, tm, tn), lambda i, j, kk: (j // npb, i, j % npb))
    return _matmul(
        a, b, mode="tn", grid=(m // tm, n // tn, k // tk), tile=(tm, tn), name=name,
        a_spec=pl.BlockSpec((tk, tm), lambda i, j, kk: (kk, i)),
        b_spec=pl.BlockSpec((tk, tn), lambda i, j, kk: (kk, j)),
        out_shape=out_shape, out_spec=out_spec)


def _cast_bf16(x, name):
    r, c = x.shape
    tr = _pick(r, (512, 256, 128, 8))

    def body(x_ref, o_ref):
        o_ref[...] = x_ref[...].astype(BF16)

    return pl.pallas_call(
        body, name=name, grid=(r // tr,),
        in_specs=[pl.BlockSpec((tr, c), lambda i: (i, 0))],
        out_specs=pl.BlockSpec((tr, c), lambda i: (i, 0)),
        out_shape=jax.ShapeDtypeStruct((r, c), BF16),
        compiler_params=_cparams(("parallel",)),
    )(x)


def _bucket_tiles():
    qi = np.arange(ATTN_BLOCK)[:, None]
    ki = np.arange(2 * ATTN_BLOCK)[None, :]
    delta = np.clip(ATTN_BLOCK + qi - ki, 0, None)
    tiles = []
    max_exact = NUM_BUCKETS // 2
    for _, dil in ATTN_PATTERNS:
        dist = (delta * dil).astype(np.int32)
        d_f = np.maximum(dist, 1).astype(np.float32)
        large = max_exact + (np.log(d_f / np.float32(max_exact)) / np.float32(math.log(MAX_DISTANCE / max_exact))
                             * np.float32(NUM_BUCKETS - max_exact)).astype(np.int32)
        large = np.minimum(large, NUM_BUCKETS - 1)
        tiles.append(np.where(dist < max_exact, dist, large).astype(np.int32))
    return jnp.asarray(np.stack(tiles))


def _bias_expand(rel_bias, buckets, hpg):
    def body(tab_ref, bk_ref, o_ref):
        g, h = pl.program_id(0), pl.program_id(1)
        bk = bk_ref[...]
        acc = jnp.zeros((ATTN_BLOCK, 2 * ATTN_BLOCK), F32)
        for b in range(NUM_BUCKETS):
            acc = jnp.where(bk == b, tab_ref[b, g * hpg + h], acc)
        o_ref[...] = acc

    return pl.pallas_call(
        body, name="bias_expand", grid=(N_GROUPS_ATTN, hpg),
        in_specs=[pl.BlockSpec(memory_space=pltpu.SMEM),
                  pl.BlockSpec((None, ATTN_BLOCK, 2 * ATTN_BLOCK), lambda g, h: (g, 0, 0))],
        out_specs=pl.BlockSpec((None, None, ATTN_BLOCK, 2 * ATTN_BLOCK), lambda g, h: (g, h, 0, 0)),
        out_shape=jax.ShapeDtypeStruct((N_GROUPS_ATTN, hpg, ATTN_BLOCK, 2 * ATTN_BLOCK), F32),
        compiler_params=_cparams(("parallel", "parallel")),
    )(rel_bias, buckets)


def _bias_reduce(dtiles, buckets, hpg):
    def body(t_ref, bk_ref, o_ref):
        bk = bk_ref[...]
        t = t_ref[...]
        rows = lax.broadcasted_iota(jnp.int32, (NUM_BUCKETS, LANES), 0)
        acc = jnp.zeros((NUM_BUCKETS, LANES), F32)
        for b in range(NUM_BUCKETS):
            s = jnp.sum(jnp.sum(jnp.where(bk == b, t, 0.0), axis=1, keepdims=True), axis=0, keepdims=True)
            acc = jnp.where(rows == b, s, acc)
        o_ref[...] = acc

    return pl.pallas_call(
        body, name="bias_reduce", grid=(N_GROUPS_ATTN, hpg),
        in_specs=[pl.BlockSpec((None, None, ATTN_BLOCK, 2 * ATTN_BLOCK), lambda g, h: (g, h, 0, 0)),
                  pl.BlockSpec((None, ATTN_BLOCK, 2 * ATTN_BLOCK), lambda g, h: (g, 0, 0))],
        out_specs=pl.BlockSpec((None, NUM_BUCKETS, LANES), lambda g, h: (g * hpg + h, 0, 0)),
        out_shape=jax.ShapeDtypeStruct((N_GROUPS_ATTN * hpg, NUM_BUCKETS, LANES), F32),
        compiler_params=_cparams(("parallel", "parallel")),
    )(dtiles, buckets)


def _attn_valid(n_is_first):
    qi = lax.broadcasted_iota(jnp.int32, (ATTN_BLOCK, 2 * ATTN_BLOCK), 0)
    ki = lax.broadcasted_iota(jnp.int32, (ATTN_BLOCK, 2 * ATTN_BLOCK), 1)
    delta = ATTN_BLOCK + qi - ki
    band = (delta >= 0) & (delta <= ATTN_BLOCK)
    return band & (jnp.logical_not(n_is_first) | (ki >= ATTN_BLOCK))


def _attn_fwd(pa, bias, g, dil, hpg):
    s, c = pa.shape
    w = hpg * HEAD_DIM
    cpb = c // w
    rows = s // dil
    nb = rows // ATTN_BLOCK
    pav = pa.reshape(rows, dil * c)
    scale = HEAD_DIM ** -0.5

    def body(q_ref, kc_ref, kp_ref, vc_ref, vp_ref, bias_ref, o_ref, lse_ref):
        valid = _attn_valid(pl.program_id(1) == 0)
        for h in range(hpg):
            sl = slice(h * HEAD_DIM, (h + 1) * HEAD_DIM)
            k2 = jnp.concatenate([kp_ref[:, sl], kc_ref[:, sl]], axis=0)
            v2 = jnp.concatenate([vp_ref[:, sl], vc_ref[:, sl]], axis=0)
            sc = _dot_nt(q_ref[:, sl], k2) * scale + bias_ref[h]
            sc = jnp.where(valid, sc, NEG_INF)
            m = jnp.max(sc, axis=1, keepdims=True)
            p = jnp.exp(sc - m)
            l = jnp.sum(p, axis=1, keepdims=True)
            o_ref[:, sl] = _dot(p.astype(BF16), v2) / l
            lse_ref[:, sl] = jnp.broadcast_to(m + jnp.log(l), (ATTN_BLOCK, HEAD_DIM))

    def col(off):
        return lambda r, n: (n, r * cpb + 3 * g + off)

    def colp(off):
        return lambda r, n: (jnp.maximum(n - 1, 0), r * cpb + 3 * g + off)

    blk = (ATTN_BLOCK, w)
    o, lse = pl.pallas_call(
        body, name=f"attn_fwd_g{g}", grid=(dil, nb),
        in_specs=[pl.BlockSpec(blk, col(0)), pl.BlockSpec(blk, col(1)), pl.BlockSpec(blk, colp(1)),
                  pl.BlockSpec(blk, col(2)), pl.BlockSpec(blk, colp(2)),
                  pl.BlockSpec((None, hpg, ATTN_BLOCK, 2 * ATTN_BLOCK), lambda r, n: (g, 0, 0, 0))],
        out_specs=[pl.BlockSpec(blk, lambda r, n: (n, r)), pl.BlockSpec(blk, lambda r, n: (n, r))],
        out_shape=[jax.ShapeDtypeStruct((rows, dil * w), F32), jax.ShapeDtypeStruct((rows, dil * w), F32)],
        compiler_params=_cparams(("parallel", "parallel")),
    )(pav, pav, pav, pav, pav, bias)
    return o.reshape(s, w), lse.reshape(s, w)


def _attn_combine(os_, lses, pa, hpg):
    s, w = os_[0].shape
    gate_blk = pa.shape[1] // w - 1
    tm = _pick(s, (256, 128))

    def body(o0, o1, o2, l0, l1, l2, gate_ref, o_ref, lse_ref, y_ref):
        a0, a1, a2 = l0[...], l1[...], l2[...]
        m = jnp.maximum(jnp.maximum(a0, a1), a2)
        e0, e1, e2 = jnp.exp(a0 - m), jnp.exp(a1 - m), jnp.exp(a2 - m)
        den = e0 + e1 + e2
        o = (e0 * o0[...] + e1 * o1[...] + e2 * o2[...]) / den
        gate = gate_ref[...].astype(F32)
        o_ref[...] = o.astype(BF16)
        lse_ref[...] = m + jnp.log(den)
        y_ref[...] = (o * (gate * _sigmoid(gate))).astype(BF16)

    spec = pl.BlockSpec((tm, w), lambda i: (i, 0))
    return pl.pallas_call(
        body, name="attn_combine", grid=(s // tm,),
        in_specs=[spec] * 6 + [pl.BlockSpec((tm, w), lambda i: (i, gate_blk))],
        out_specs=[spec, spec, spec],
        out_shape=[jax.ShapeDtypeStruct((s, w), BF16), jax.ShapeDtypeStruct((s, w), F32),
                   jax.ShapeDtypeStruct((s, w), BF16)],
        compiler_params=_cparams(("parallel",)),
    )(*os_, *lses, pa)


def _attn_pre_bwd(dy, o, pa, hpg):
    s, w = dy.shape
    gate_blk = pa.shape[1] // w - 1
    tm = _pick(s, (256, 128))

    def body(dy_ref, o_ref, gate_ref, do_ref, dl_ref, dg_ref):
        gate = gate_ref[...].astype(F32)
        sg = _sigmoid(gate)
        dyv = dy_ref[...].astype(F32)
        ov = o_ref[...].astype(F32)
        do = dyv * (gate * sg)
        do_ref[...] = do.astype(BF16)
        dg_ref[...] = (dyv * ov * (sg * (1.0 + gate * (1.0 - sg)))).astype(BF16)
        prod = do * ov
        for h in range(hpg):
            sl = slice(h * HEAD_DIM, (h + 1) * HEAD_DIM)
            dl_ref[:, sl] = jnp.broadcast_to(jnp.sum(prod[:, sl], axis=1, keepdims=True), (tm, HEAD_DIM))

    spec = pl.BlockSpec((tm, w), lambda i: (i, 0))
    return pl.pallas_call(
        body, name="attn_pre_bwd", grid=(s // tm,),
        in_specs=[spec, spec, pl.BlockSpec((tm, w), lambda i: (i, gate_blk))],
        out_specs=[spec, spec, spec],
        out_shape=[jax.ShapeDtypeStruct((s, w), BF16), jax.ShapeDtypeStruct((s, w), F32),
                   jax.ShapeDtypeStruct((s, w), BF16)],
        compiler_params=_cparams(("parallel",)),
    )(dy, o, pa)


def _attn_bwd(pa, bias, do, lse, delta, g, dil, hpg):
    s, c = pa.shape
    w = hpg * HEAD_DIM
    cpb = c // w
    rows = s // dil
    nb = rows // ATTN_BLOCK
    pav = pa.reshape(rows, dil * c)
    dov, lsev, dlv = (t.reshape(rows, dil * w) for t in (do, lse, delta))
    scale = HEAD_DIM ** -0.5

    def body(q_ref, kc_ref, kp_ref, vc_ref, vp_ref, bias_ref, do_ref, lse_ref, dl_ref,
             dq_ref, dk_ref, dv_ref, db_ref, dkc_ref, dvc_ref):
        r, i = pl.program_id(0), pl.program_id(1)
        n = nb - 1 - i
        valid = _attn_valid(n == 0)

        @pl.when((r == 0) & (i == 0))
        def _():
            db_ref[...] = jnp.zeros_like(db_ref)

        @pl.when(i == 0)
        def _():
            dkc_ref[...] = jnp.zeros_like(dkc_ref)
            dvc_ref[...] = jnp.zeros_like(dvc_ref)

        for h in range(hpg):
            sl = slice(h * HEAD_DIM, (h + 1) * HEAD_DIM)
            q = q_ref[:, sl]
            dov_ = do_ref[:, sl]
            k2 = jnp.concatenate([kp_ref[:, sl], kc_ref[:, sl]], axis=0)
            v2 = jnp.concatenate([vp_ref[:, sl], vc_ref[:, sl]], axis=0)
            sc = _dot_nt(q, k2) * scale + bias_ref[h]
            p = jnp.exp(jnp.where(valid, sc - lse_ref[:, sl][:, 0:1], NEG_INF))
            dp = _dot_nt(dov_, v2)
            ds = p * (dp - dl_ref[:, sl][:, 0:1])
            db_ref[h] += ds
            dsb = ds.astype(BF16)
            dq_ref[:, sl] = (_dot(dsb, k2) * scale).astype(BF16)
            dk2 = _dot_tn(dsb, q) * scale
            dv2 = _dot_tn(p.astype(BF16), dov_)
            dk_ref[:, sl] = (dk2[ATTN_BLOCK:] + dkc_ref[:, sl]).astype(BF16)
            dv_ref[:, sl] = (dv2[ATTN_BLOCK:] + dvc_ref[:, sl]).astype(BF16)
            dkc_ref[:, sl] = dk2[:ATTN_BLOCK]
            dvc_ref[:, sl] = dv2[:ATTN_BLOCK]

    def col(off):
        return lambda r, i: (nb - 1 - i, r * cpb + 3 * g + off)

    def colp(off):
        return lambda r, i: (jnp.maximum(nb - 2 - i, 0), r * cpb + 3 * g + off)

    blk = (ATTN_BLOCK, w)
    tok = pl.BlockSpec(blk, lambda r, i: (nb - 1 - i, r))
    dq, dk, dv, db = pl.pallas_call(
        body, name=f"attn_bwd_g{g}", grid=(dil, nb),
        in_specs=[pl.BlockSpec(blk, col(0)), pl.BlockSpec(blk, col(1)), pl.BlockSpec(blk, colp(1)),
                  pl.BlockSpec(blk, col(2)), pl.BlockSpec(blk, colp(2)),
                  pl.BlockSpec((None, hpg, ATTN_BLOCK, 2 * ATTN_BLOCK), lambda r, i: (g, 0, 0, 0)),
                  tok, tok, tok],
        out_specs=[tok, tok, tok,
                   pl.BlockSpec((hpg, ATTN_BLOCK, 2 * ATTN_BLOCK), lambda r, i: (0, 0, 0))],
        out_shape=[jax.ShapeDtypeStruct((rows, dil * w), BF16)] * 3
        + [jax.ShapeDtypeStruct((hpg, ATTN_BLOCK, 2 * ATTN_BLOCK), F32)],
        scratch_shapes=[pltpu.VMEM(blk, F32), pltpu.VMEM(blk, F32)],
        compiler_params=_cparams(("arbitrary", "arbitrary")),
    )(pav, pav, pav, pav, pav, bias, dov, lsev, dlv)
    return dq.reshape(s, w), dk.reshape(s, w), dv.reshape(s, w), db


def _ln_fwd(xin, h, gamma, beta, name, affine_in=None, target=None):
    s, d = xin.shape
    tm = _pick(s, (128,))
    has_aff = affine_in is not None
    has_tgt = target is not None

    def body(*refs):
        it = iter(refs)
        x_ref, h_ref, g_ref, b_ref = next(it), next(it), next(it), next(it)
        if has_aff:
            gi_ref, bi_ref = next(it), next(it)
        if has_tgt:
            t_ref = next(it)
        xh_ref, rs_ref = next(it), next(it)
        x = x_ref[...]
        if has_aff:
            x = x * gi_ref[...] + bi_ref[...]
        u = DEEPNORM_ALPHA * x + h_ref[...]
        mu = jnp.mean(u, axis=1, keepdims=True)
        uc = u - mu
        var = jnp.mean(uc * uc, axis=1, keepdims=True)
        rstd = lax.rsqrt(var + LN_EPS)
        xhat = uc * rstd
        xh_ref[...] = xhat
        rs_ref[...] = rstd
        y = xhat * g_ref[...] + b_ref[...]
        if has_tgt:
            dy_ref, l_ref = next(it), next(it)
            e = y - t_ref[...]
            dy_ref[...] = e * (1.0 / d)
            l_ref[...] = jnp.sum(e * e, axis=1, keepdims=True)
        else:
            y_ref = next(it)
            y_ref[...] = y.astype(BF16)

    row = pl.BlockSpec((tm, d), lambda i: (i, 0))
    vec = pl.BlockSpec((1, d), lambda i: (0, 0))
    one = pl.BlockSpec((tm, 1), lambda i: (i, 0))
    in_specs = [row, row, vec, vec] + ([vec, vec] if has_aff else []) + ([row] if has_tgt else [])
    args = [xin, h, gamma, beta] + (list(affine_in) if has_aff else []) + ([target] if has_tgt else [])
    out_specs = [row, one] + ([row, one] if has_tgt else [row])
    out_shape = [jax.ShapeDtypeStruct((s, d), F32), jax.ShapeDtypeStruct((s, 1), F32)]
    out_shape += ([jax.ShapeDtypeStruct((s, d), F32), jax.ShapeDtypeStruct((s, 1), F32)] if has_tgt
                  else [jax.ShapeDtypeStruct((s, d), BF16)])
    return pl.pallas_call(
        body, name=name, grid=(s // tm,), in_specs=in_specs, out_specs=out_specs, out_shape=out_shape,
        compiler_params=_cparams(("parallel",)),
    )(*args)


def _ln_bwd(dy, xhat, rstd, gamma, name):
    s, d = dy.shape
    tm = _pick(s, (128,))

    def body(dy_ref, xh_ref, rs_ref, g_ref, du_ref, dub_ref, dg_ref, db_ref):
        @pl.when(pl.program_id(0) == 0)
        def _():
            dg_ref[...] = jnp.zeros_like(dg_ref)
            db_ref[...] = jnp.zeros_like(db_ref)

        dyv = dy_ref[...]
        xh = xh_ref[...]
        dg_ref[...] += jnp.sum(dyv * xh, axis=0, keepdims=True)
        db_ref[...] += jnp.sum(dyv, axis=0, keepdims=True)
        dxh = dyv * g_ref[...]
        m1 = jnp.mean(dxh, axis=1, keepdims=True)
        m2 = jnp.mean(dxh * xh, axis=1, keepdims=True)
        du = rs_ref[...] * (dxh - m1 - xh * m2)
        du_ref[...] = du
        dub_ref[...] = du.astype(BF16)

    row = pl.BlockSpec((tm, d), lambda i: (i, 0))
    vec = pl.BlockSpec((1, d), lambda i: (0, 0))
    one = pl.BlockSpec((tm, 1), lambda i: (i, 0))
    return pl.pallas_call(
        body, name=name, grid=(s // tm,), in_specs=[row, row, one, vec],
        out_specs=[row, row, vec, vec],
        out_shape=[jax.ShapeDtypeStruct((s, d), F32), jax.ShapeDtypeStruct((s, d), BF16),
                   jax.ShapeDtypeStruct((1, d), F32), jax.ShapeDtypeStruct((1, d), F32)],
        compiler_params=_cparams(("arbitrary",)),
    )(dy, xhat, rstd, gamma)


_HALO = 16


def _conv_taps(ext, tm, w_ref):
    acc = None
    for k in range(CONV_WIDTH):
        lo = _HALO - (CONV_WIDTH - 1) + k
        term = w_ref[k:k + 1, :] * ext[lo:lo + tm, :]
        acc = term if acc is None else acc + term
    return acc


def _conv_fwd(pzx, conv_w, conv_b, d_inner):
    s, _ = pzx.shape
    cd = conv_w.shape[1]
    tm = _pick(s, (512, 256, 128))
    tc = _pick(cd, (1024, 512, 256, 128))
    off = d_inner // tc
    hb = tm // _HALO

    def body(x_ref, p_ref, w_ref, b_ref, o_ref):
        prev = jnp.where(pl.program_id(0) > 0, p_ref[...].astype(F32), 0.0)
        ext = jnp.concatenate([prev, x_ref[...].astype(F32)], axis=0)
        pre = _conv_taps(ext, tm, w_ref) + b_ref[...]
        o_ref[...] = (pre * _sigmoid(pre)).astype(BF16)

    return pl.pallas_call(
        body, name="conv_fwd", grid=(s // tm, cd // tc),
        in_specs=[pl.BlockSpec((tm, tc), lambda i, j: (i, off + j)),
                  pl.BlockSpec((_HALO, tc), lambda i, j: (jnp.maximum(i * hb - 1, 0), off + j)),
                  pl.BlockSpec((CONV_WIDTH, tc), lambda i, j: (0, j)),
                  pl.BlockSpec((1, tc), lambda i, j: (0, j))],
        out_specs=pl.BlockSpec((tm, tc), lambda i, j: (i, j)),
        out_shape=jax.ShapeDtypeStruct((s, cd), BF16),
        compiler_params=_cparams(("parallel", "parallel")),
    )(pzx, pzx, conv_w, conv_b)


def _conv_bwd_a(pzx, dxbc, conv_w, conv_b, d_inner):
    s, _ = pzx.shape
    cd = conv_w.shape[1]
    tm = _pick(s, (512, 256, 128))
    tc = _pick(cd, (1024, 512, 256, 128))
    off = d_inner // tc
    hb = tm // _HALO

    def body(x_ref, p_ref, d_ref, w_ref, b_ref, o_ref, dw_ref, db_ref):
        @pl.when(pl.program_id(1) == 0)
        def _():
            dw_ref[...] = jnp.zeros_like(dw_ref)
            db_ref[...] = jnp.zeros_like(db_ref)

        prev = jnp.where(pl.program_id(1) > 0, p_ref[...].astype(F32), 0.0)
        ext = jnp.concatenate([prev, x_ref[...].astype(F32)], axis=0)
        pre = _conv_taps(ext, tm, w_ref) + b_ref[...]
        sg = _sigmoid(pre)
        dpre = d_ref[...].astype(F32) * (sg * (1.0 + pre * (1.0 - sg)))
        o_ref[...] = dpre
        db_ref[...] += jnp.sum(dpre, axis=0, keepdims=True)
        for k in range(CONV_WIDTH):
            lo = _HALO - (CONV_WIDTH - 1) + k
            dw_ref[k:k + 1, :] += jnp.sum(dpre * ext[lo:lo + tm, :], axis=0, keepdims=True)

    return pl.pallas_call(
        body, name="conv_bwd_a", grid=(cd // tc, s // tm),
        in_specs=[pl.BlockSpec((tm, tc), lambda j, i: (i, off + j)),
                  pl.BlockSpec((_HALO, tc), lambda j, i: (jnp.maximum(i * hb - 1, 0), off + j)),
                  pl.BlockSpec((tm, tc), lambda j, i: (i, j)),
                  pl.BlockSpec((CONV_WIDTH, tc), lambda j, i: (0, j)),
                  pl.BlockSpec((1, tc), lambda j, i: (0, j))],
        out_specs=[pl.BlockSpec((tm, tc), lambda j, i: (i, j)),
                   pl.BlockSpec((CONV_WIDTH, tc), lambda j, i: (0, j)),
                   pl.BlockSpec((1, tc), lambda j, i: (0, j))],
        out_shape=[jax.ShapeDtypeStruct((s, cd), F32), jax.ShapeDtypeStruct((CONV_WIDTH, cd), F32),
                   jax.ShapeDtypeStruct((1, cd), F32)],
        compiler_params=_cparams(("parallel", "arbitrary")),
    )(pzx, pzx, dxbc, conv_w, conv_b)


def _conv_bwd_b(dpre, conv_w):
    s, cd = dpre.shape
    tm = _pick(s, (512, 256, 128))
    tc = _pick(cd, (1024, 512, 256, 128))
    hb = tm // 8
    nrb = s // tm

    def body(x_ref, nx_ref, w_ref, o_ref):
        nxt = jnp.where(pl.program_id(0) < nrb - 1, nx_ref[...], 0.0)
        ext = jnp.concatenate([x_ref[...], nxt], axis=0)
        acc = None
        for k in range(CONV_WIDTH):
            lo = CONV_WIDTH - 1 - k
            term = w_ref[k:k + 1, :] * ext[lo:lo + tm, :]
            acc = term if acc is None else acc + term
        o_ref[...] = acc.astype(BF16)

    return pl.pallas_call(
        body, name="conv_bwd_b", grid=(nrb, cd // tc),
        in_specs=[pl.BlockSpec((tm, tc), lambda i, j: (i, j)),
                  pl.BlockSpec((8, tc), lambda i, j: (jnp.minimum((i + 1) * hb, s // 8 - 1), j)),
                  pl.BlockSpec((CONV_WIDTH, tc), lambda i, j: (0, j))],
        out_specs=pl.BlockSpec((tm, tc), lambda i, j: (i, j)),
        out_shape=jax.ShapeDtypeStruct((s, cd), BF16),
        compiler_params=_cparams(("parallel", "parallel")),
    )(dpre, dpre, conv_w)


def _expand_matrix():
    e = np.zeros((LANES, SSM_GROUP_WIDTH), np.float32)
    for h in range(HEADS_PER_SSM_GROUP):
        e[h, h * SSM_HEAD_DIM:(h + 1) * SSM_HEAD_DIM] = 1.0
    return jnp.asarray(e, BF16)


def _expand(t, e):
    hi, lo = _split2(t)
    return _dot(hi, e) + _dot(lo, e)


def _segsum(v, e):
    hi, lo = _split2(v)
    return _dot_nt(hi, e) + _dot_nt(lo, e)


def _tri_dot(tri, x):
    hi, mid, lo = _split3(x)
    return _dot(tri, hi) + _dot(tri, mid) + _dot(tri, lo)


def _ssd_common(dtp_ref, a_ref, dtb_ref, x_ref, e):
    li = lax.broadcasted_iota(jnp.int32, (CHUNK, CHUNK), 0)
    si = lax.broadcasted_iota(jnp.int32, (CHUNK, CHUNK), 1)
    causal = li >= si
    tril = causal.astype(BF16)
    raw = dtp_ref[...] + dtb_ref[...]
    dt = jnp.maximum(raw, 0.0) + jnp.log(1.0 + jnp.exp(-jnp.abs(raw)))
    head_lane = lax.broadcasted_iota(jnp.int32, (1, LANES), 1) < HEADS_PER_SSM_GROUP
    a = jnp.where(head_lane, -jnp.exp(a_ref[...]), 0.0)
    a_cum = _tri_dot(tril, dt * a)
    a_cum_t = a_cum.T
    e_a = jnp.exp(a_cum)
    to_end = jnp.exp(a_cum[CHUNK - 1:CHUNK, :] - a_cum)
    x = x_ref[...].astype(F32)
    dt_e = _expand(dt, e)
    return dict(causal=causal, raw=raw, dt=dt, a=a, a_cum=a_cum, a_cum_t=a_cum_t, e_a=e_a,
                to_end=to_end, x=x, dt_e=dt_e, xdt=x * dt_e, e_a_e=_expand(e_a, e),
                to_end_e=_expand(to_end, e))


def _decay(q, h):
    seg = q["a_cum"][:, h:h + 1] - q["a_cum_t"][h:h + 1, :]
    return jnp.exp(jnp.where(q["causal"], seg, -jnp.inf))


def _ssd_specs(ng, d_inner, rev, nc):
    cidx = (lambda i: nc - 1 - i) if rev else (lambda i: i)
    boff = d_inner // D_STATE
    return dict(
        xs=pl.BlockSpec((CHUNK, SSM_GROUP_WIDTH), lambda g, i: (cidx(i), g)),
        b=pl.BlockSpec((CHUNK, D_STATE), lambda g, i: (cidx(i), boff + g)),
        c=pl.BlockSpec((CHUNK, D_STATE), lambda g, i: (cidx(i), boff + ng + g)),
        dtp=pl.BlockSpec((None, CHUNK, LANES), lambda g, i: (g, cidx(i), 0)),
        vec=pl.BlockSpec((None, 1, LANES), lambda g, i: (g, 0, 0)),
        wide=pl.BlockSpec((None, 1, SSM_GROUP_WIDTH), lambda g, i: (g, 0, 0)),
        e=pl.BlockSpec((LANES, SSM_GROUP_WIDTH), lambda g, i: (0, 0)),
        st=pl.BlockSpec((None, None, D_STATE, SSM_GROUP_WIDTH), lambda g, i: (g, cidx(i), 0, 0)),
        tok=pl.BlockSpec((CHUNK, SSM_GROUP_WIDTH), lambda g, i: (cidx(i), g)),
        bc_out=pl.BlockSpec((CHUNK, D_STATE), lambda g, i: (cidx(i), g)),
    )


def _ssd_fwd(xbc, dtp, a_pad, dtb_pad, dsk_e, e, d_inner):
    s = xbc.shape[0]
    ng = d_inner // SSM_GROUP_WIDTH
    nc = s // CHUNK

    def body(x_ref, b_ref, c_ref, dtp_ref, a_ref, dtb_ref, dsk_ref, e_ref, y_ref, st_ref, state):
        lane = lax.broadcasted_iota(jnp.int32, (CHUNK, LANES), 1)
        @pl.when(pl.program_id(1) == 0)
        def _():
            state[...] = jnp.zeros_like(state)

        ev = e_ref[...]
        q = _ssd_common(dtp_ref, a_ref, dtb_ref, x_ref, ev)
        bm, cm = b_ref[...], c_ref[...]
        cb = _dot_nt(cm, bm)
        s0 = state[...]
        st_ref[...] = s0
        y = _dot(cm, s0.astype(BF16)) * q["e_a_e"] + dsk_ref[...] * q["x"]
        xdt = q["xdt"]
        left = lane[:, :] < SSM_HEAD_DIM
        for j in range(HEADS_PER_SSM_GROUP // 2):
            sl = slice(j * LANES, (j + 1) * LANES)
            x2 = xdt[:, sl]
            m0 = (cb * _decay(q, 2 * j)).astype(BF16)
            m1 = (cb * _decay(q, 2 * j + 1)).astype(BF16)
            mcat = jnp.concatenate([m0, m1], axis=1)
            xbd = jnp.concatenate([jnp.where(left, x2, 0.0), jnp.where(left, 0.0, x2)], axis=0).astype(BF16)
            y_ref[:, sl] = (y[:, sl] + _dot(mcat, xbd)).astype(BF16)
        state[...] = s0 * q["e_a_e"][CHUNK - 1:CHUNK, :] + _dot_tn(bm, (q["to_end_e"] * xdt).astype(BF16))

    sp = _ssd_specs(ng, d_inner, False, nc)
    return pl.pallas_call(
        body, name="ssd_fwd", grid=(ng, nc),
        in_specs=[sp["xs"], sp["b"], sp["c"], sp["dtp"], sp["vec"], sp["vec"], sp["wide"], sp["e"]],
        out_specs=[sp["tok"], sp["st"]],
        out_shape=[jax.ShapeDtypeStruct((s, d_inner), BF16),
                   jax.ShapeDtypeStruct((ng, nc, D_STATE, SSM_GROUP_WIDTH), F32)],
        scratch_shapes=[pltpu.VMEM((D_STATE, SSM_GROUP_WIDTH), F32)],
        compiler_params=_cparams(("parallel", "arbitrary")),
    )(xbc, xbc, xbc, dtp, a_pad, dtb_pad, dsk_e, e)


def _ssd_bwd(xbc, dtp, a_pad, dtb_pad, dsk_e, e, states, dy, d_inner):
    s = xbc.shape[0]
    ng = d_inner // SSM_GROUP_WIDTH
    nc = s // CHUNK

    def body(x_ref, b_ref, c_ref, dtp_ref, a_ref, dtb_ref, dsk_ref, e_ref, st_ref, dy_ref,
             dx_ref, db_ref, dc_ref, ddt_ref, da_ref, ddtb_ref, dd_ref, dstate):
        lane = lax.broadcasted_iota(jnp.int32, (CHUNK, LANES), 1)
        sub = lax.broadcasted_iota(jnp.int32, (CHUNK, LANES), 0)
        @pl.when(pl.program_id(1) == 0)
        def _():
            dstate[...] = jnp.zeros_like(dstate)
            da_ref[...] = jnp.zeros_like(da_ref)
            ddtb_ref[...] = jnp.zeros_like(ddtb_ref)
            dd_ref[...] = jnp.zeros_like(dd_ref)

        ev = e_ref[...]
        q = _ssd_common(dtp_ref, a_ref, dtb_ref, x_ref, ev)
        bm, cm = b_ref[...], c_ref[...]
        cb = _dot_nt(cm, bm)
        x, xdt, e_a_e, to_end_e = q["x"], q["xdt"], q["e_a_e"], q["to_end_e"]
        s0 = st_ref[...]
        s0b = s0.astype(BF16)
        ds1 = dstate[...]
        ds1b = ds1.astype(BF16)
        dy = dy_ref[...].astype(F32)
        e_last_e = e_a_e[CHUNK - 1:CHUNK, :]

        dye = dy * e_a_e
        dyeb = dye.astype(BF16)
        cs0 = _dot(cm, s0b)
        dc = _dot_nt(dyeb, s0b)
        dstate[...] = e_last_e * ds1 + _dot_tn(cm, dyeb)
        da_col = _segsum(dye * cs0, ev)

        gmat = _dot(bm, ds1b)
        dxdt = to_end_e * gmat
        dte = _segsum(xdt * gmat, ev) * q["to_end"]
        db = _dot_nt((to_end_e * xdt).astype(BF16), ds1b)
        da_col = da_col - dte
        last_row = (jnp.sum(dte, axis=0, keepdims=True)
                    + q["e_a"][CHUNK - 1:CHUNK, :] * jnp.sum(_segsum(s0 * ds1, ev), axis=0, keepdims=True))

        left = lane < SSM_HEAD_DIM
        dcb = jnp.zeros((CHUNK, CHUNK), F32)
        row_acc = jnp.zeros((CHUNK, LANES), F32)
        for j in range(HEADS_PER_SSM_GROUP // 2):
            sl = slice(j * LANES, (j + 1) * LANES)
            x2 = xdt[:, sl].astype(BF16)
            dy2 = dy[:, sl]
            dyl = jnp.where(left, dy2, 0.0).astype(BF16)
            dyr = jnp.where(left, 0.0, dy2).astype(BF16)
            ms = []
            for hh, dyh in ((0, dyl), (1, dyr)):
                h = 2 * j + hh
                dec = _decay(q, h)
                m = cb * dec
                dm = _dot_nt(dyh, x2)
                dcb = dcb + dm * dec
                dseg = dm * m
                da_col = da_col + jnp.where(lane == h, jnp.sum(dseg, axis=1, keepdims=True), 0.0)
                row_acc = row_acc + jnp.where(sub == h, jnp.sum(dseg, axis=0, keepdims=True), 0.0)
                ms.append(m.astype(BF16))
            mst = jnp.concatenate(ms, axis=0)
            dyst = jnp.concatenate([dyl, dyr], axis=0)
            d2 = dxdt[:, sl] + _dot_tn(mst, dyst)
            dx_ref[:, sl] = (d2 * q["dt_e"][:, sl] + dsk_ref[:, sl] * dy2).astype(BF16)
            dxdt_x = d2 * x[:, sl]
            if j == 0:
                parts = [dxdt_x]
            else:
                parts.append(dxdt_x)
        dcbb = dcb.astype(BF16)
        dc_ref[...] = (dc + _dot(dcbb, bm)).astype(BF16)
        db_ref[...] = (db + _dot_tn(dcbb, cm)).astype(BF16)

        d_a = da_col - row_acc.T + jnp.where(sub == CHUNK - 1, last_row, 0.0)
        triu = (lax.broadcasted_iota(jnp.int32, (CHUNK, CHUNK), 1)
                >= lax.broadcasted_iota(jnp.int32, (CHUNK, CHUNK), 0)).astype(BF16)
        d_dta = _tri_dot(triu, d_a)
        ddt = d_dta * q["a"] + _segsum(jnp.concatenate(parts, axis=1), ev)
        ddt_raw = ddt * _sigmoid(q["raw"])
        ddt_ref[...] = ddt_raw
        da_ref[...] += jnp.sum(d_dta * q["dt"], axis=0, keepdims=True) * q["a"]
        ddtb_ref[...] += jnp.sum(ddt_raw, axis=0, keepdims=True)
        dd_ref[...] += jnp.sum(dy * x, axis=0, keepdims=True)

    sp = _ssd_specs(ng, d_inner, True, nc)
    return pl.pallas_call(
        body, name="ssd_bwd", grid=(ng, nc),
        in_specs=[sp["xs"], sp["b"], sp["c"], sp["dtp"], sp["vec"], sp["vec"], sp["wide"], sp["e"],
                  sp["st"], sp["tok"]],
        out_specs=[sp["tok"], sp["bc_out"], sp["bc_out"], sp["dtp"], sp["vec"], sp["vec"], sp["wide"]],
        out_shape=[jax.ShapeDtypeStruct((s, d_inner), BF16),
                   jax.ShapeDtypeStruct((s, ng * D_STATE), BF16),
                   jax.ShapeDtypeStruct((s, ng * D_STATE), BF16),
                   jax.ShapeDtypeStruct((ng, s, LANES), F32),
                   jax.ShapeDtypeStruct((ng, 1, LANES), F32),
                   jax.ShapeDtypeStruct((ng, 1, LANES), F32),
                   jax.ShapeDtypeStruct((ng, 1, SSM_GROUP_WIDTH), F32)],
        scratch_shapes=[pltpu.VMEM((D_STATE, SSM_GROUP_WIDTH), F32)],
        compiler_params=_cparams(("parallel", "arbitrary")),
    )(xbc, xbc, xbc, dtp, a_pad, dtb_pad, dsk_e, e, states, dy)


def _gate_norm_fwd(y, pzx, norm_w):
    s, di = y.shape
    ng = di // SSM_GROUP_WIDTH
    tm = _pick(s, (512, 256, 128))

    def body(y_ref, z_ref, w_ref, o_ref):
        z = z_ref[...].astype(F32)
        y2 = y_ref[...].astype(F32) * (z * _sigmoid(z))
        r = lax.rsqrt(jnp.mean(y2 * y2, axis=1, keepdims=True) + RMS_EPS)
        o_ref[...] = (y2 * r * w_ref[...]).astype(BF16)

    blk = pl.BlockSpec((tm, SSM_GROUP_WIDTH), lambda i, g: (i, g))
    return pl.pallas_call(
        body, name="gate_norm_fwd", grid=(s // tm, ng),
        in_specs=[blk, blk, pl.BlockSpec((1, SSM_GROUP_WIDTH), lambda i, g: (0, g))],
        out_specs=blk, out_shape=jax.ShapeDtypeStruct((s, di), BF16),
        compiler_params=_cparams(("parallel", "parallel")),
    )(y, pzx, norm_w)


def _gate_norm_bwd(dy3, y, pzx, norm_w):
    s, di = y.shape
    ng = di // SSM_GROUP_WIDTH
    tm = _pick(s, (512, 256, 128))

    def body(d_ref, y_ref, z_ref, w_ref, dy_ref, dz_ref, dw_ref):
        @pl.when(pl.program_id(1) == 0)
        def _():
            dw_ref[...] = jnp.zeros_like(dw_ref)

        z = z_ref[...].astype(F32)
        yv = y_ref[...].astype(F32)
        sg = _sigmoid(z)
        sz = z * sg
        y2 = yv * sz
        r = lax.rsqrt(jnp.mean(y2 * y2, axis=1, keepdims=True) + RMS_EPS)
        nrm = y2 * r
        d3 = d_ref[...].astype(F32)
        dw_ref[...] += jnp.sum(d3 * nrm, axis=0, keepdims=True)
        dn = d3 * w_ref[...]
        dy2 = r * (dn - nrm * jnp.mean(dn * nrm, axis=1, keepdims=True))
        dy_ref[...] = (dy2 * sz).astype(BF16)
        dz_ref[...] = (dy2 * yv * (sg * (1.0 + z * (1.0 - sg)))).astype(BF16)

    blk = pl.BlockSpec((tm, SSM_GROUP_WIDTH), lambda g, i: (i, g))
    vec = pl.BlockSpec((1, SSM_GROUP_WIDTH), lambda g, i: (0, g))
    return pl.pallas_call(
        body, name="gate_norm_bwd", grid=(ng, s // tm),
        in_specs=[blk, blk, blk, vec], out_specs=[blk, blk, vec],
        out_shape=[jax.ShapeDtypeStruct((s, di), BF16), jax.ShapeDtypeStruct((s, di), BF16),
                   jax.ShapeDtypeStruct((1, di), F32)],
        compiler_params=_cparams(("parallel", "arbitrary")),
    )(dy3, y, pzx, norm_w)


_ANY = pl.BlockSpec(memory_space=pl.ANY)


def _place():
    x, y, c = lax.axis_index("x"), lax.axis_index("y"), lax.axis_index("c")
    chips = [(1 - x, y), (x, 1 - y), (1 - x, 1 - y)]
    return x, y, c, chips


def _cast_to_slot(x, kvec, name):
    r, cn = x.shape
    tr = _rows_per_block(r, cn)

    def body(k_ref, x_ref, o_ref):
        o_ref[...] = x_ref[...].astype(BF16)

    grid_spec = pltpu.PrefetchScalarGridSpec(
        num_scalar_prefetch=1, grid=(r // tr,),
        in_specs=[pl.BlockSpec((tr, cn), lambda i, k: (i, 0))],
        out_specs=pl.BlockSpec((None, tr, cn), lambda i, k: (k[0], i, 0)))
    return pl.pallas_call(
        body, name=name, grid_spec=grid_spec, out_shape=jax.ShapeDtypeStruct((N_CHIPS, r, cn), BF16),
        compiler_params=_cparams(("parallel",)),
    )(kvec, x)


def _gather_weights(bufs):
    n = len(bufs)

    def body(*refs):
        outs = refs[n:2 * n]
        send_sems, recv_sems = refs[2 * n:]
        x, y, c, chips = _place()
        k = 2 * x + y
        sib = (x, y, 1 - c)

        def half(w, slot, hc):
            hr = bufs[w].shape[1] // 2
            return outs[w].at[slot, pl.ds(hc * hr, hr)]

        def copy(w, j, src, dst, to):
            return pltpu.make_async_remote_copy(
                src_ref=src, dst_ref=dst, send_sem=send_sems.at[w, j], recv_sem=recv_sems.at[w, j],
                device_id=to, device_id_type=MESH)

        first, passed = [], []
        for w in range(n):
            for j, chip in enumerate(chips):
                cp = copy(w, j, half(w, k, c), half(w, k, c), (*chip, c))
                cp.start()
                first.append(cp)
        for w in range(n):
            for j, (cx, cy) in enumerate(chips):
                kj = 2 * cx + cy
                copy(w, j, half(w, kj, c), half(w, kj, c), sib).wait_recv()
                cp = copy(w, 3 + j, half(w, kj, c), half(w, kj, c), sib)
                cp.start()
                passed.append(cp)
        for w in range(n):
            for j, (cx, cy) in enumerate(chips):
                kj = 2 * cx + cy
                copy(w, 3 + j, half(w, kj, 1 - c), half(w, kj, 1 - c), sib).wait_recv()
        for cp in first + passed:
            cp.wait_send()

    return pl.pallas_call(
        body, name="gather_weights",
        in_specs=[_ANY] * n, out_specs=[_ANY] * n,
        out_shape=[jax.ShapeDtypeStruct(b.shape, b.dtype) for b in bufs],
        input_output_aliases={w: w for w in range(n)},
        scratch_shapes=[pltpu.SemaphoreType.DMA((n, 6)), pltpu.SemaphoreType.DMA((n, 6))],
    )(*bufs)


def _swap_halves(gs):
    n = len(gs)

    def body(*refs):
        ins, outs = refs[:n], refs[n:2 * n]
        send_sems, recv_sems = refs[2 * n:]
        x, y, c, _ = _place()
        cps = []
        for w in range(n):
            hr = gs[w].shape[1] // 2
            cp = pltpu.make_async_remote_copy(
                src_ref=ins[w].at[:, pl.ds((1 - c) * hr, hr)], dst_ref=outs[w],
                send_sem=send_sems.at[w], recv_sem=recv_sems.at[w],
                device_id=(x, y, 1 - c), device_id_type=MESH)
            cp.start()
            cps.append(cp)
        for cp in cps:
            cp.wait()

    return pl.pallas_call(
        body, name="grads_to_sibling",
        in_specs=[_ANY] * n, out_specs=[_ANY] * n,
        out_shape=[jax.ShapeDtypeStruct((g.shape[0], g.shape[1] // 2, g.shape[2]), g.dtype) for g in gs],
        scratch_shapes=[pltpu.SemaphoreType.DMA((n,)), pltpu.SemaphoreType.DMA((n,))],
    )(*gs)


def _scatter_to_chips(ts):
    n = len(ts)

    def body(*refs):
        ins, outs = refs[:n], refs[n:2 * n]
        send_sems, recv_sems = refs[2 * n:]
        x, y, c, chips = _place()
        k = 2 * x + y
        cps = []
        for w in range(n):
            for j, (cx, cy) in enumerate(chips):
                cp = pltpu.make_async_remote_copy(
                    src_ref=ins[w].at[2 * cx + cy], dst_ref=outs[w].at[k],
                    send_sem=send_sems.at[w, j], recv_sem=recv_sems.at[w, j],
                    device_id=(cx, cy, c), device_id_type=MESH)
                cp.start()
                cps.append(cp)
        for w in range(n):
            for j, (cx, cy) in enumerate(chips):
                kj = 2 * cx + cy
                pltpu.make_async_remote_copy(
                    src_ref=ins[w].at[kj], dst_ref=outs[w].at[kj],
                    send_sem=send_sems.at[w, j], recv_sem=recv_sems.at[w, j],
                    device_id=(cx, cy, c), device_id_type=MESH).wait_recv()
        for cp in cps:
            cp.wait_send()

    return pl.pallas_call(
        body, name="grads_to_chips",
        in_specs=[_ANY] * n, out_specs=[_ANY] * n,
        out_shape=[jax.ShapeDtypeStruct(t.shape, t.dtype) for t in ts],
        scratch_shapes=[pltpu.SemaphoreType.DMA((n, 3)), pltpu.SemaphoreType.DMA((n, 3))],
    )(*ts)


def _join_halves(fs):
    n = len(fs)

    def body(*refs):
        outs = refs[n:2 * n]
        send_sems, recv_sems = refs[2 * n:]
        x, y, c, _ = _place()

        def copy(w, hc):
            hr = fs[w].shape[0] // 2
            rows = outs[w].at[pl.ds(hc * hr, hr)]
            return pltpu.make_async_remote_copy(
                src_ref=rows, dst_ref=rows, send_sem=send_sems.at[w], recv_sem=recv_sems.at[w],
                device_id=(x, y, 1 - c), device_id_type=MESH)

        cps = [copy(w, c) for w in range(n)]
        for cp in cps:
            cp.start()
        for w in range(n):
            copy(w, 1 - c).wait_recv()
        for cp in cps:
            cp.wait_send()

    return pl.pallas_call(
        body, name="grads_join_halves",
        in_specs=[_ANY] * n, out_specs=[_ANY] * n,
        out_shape=[jax.ShapeDtypeStruct(f.shape, f.dtype) for f in fs],
        input_output_aliases={w: w for w in range(n)},
        scratch_shapes=[pltpu.SemaphoreType.DMA((n,)), pltpu.SemaphoreType.DMA((n,))],
    )(*fs)


def _all_gather_small(v, reduce, name):
    r, l = v.shape

    def body(v_ref, o_ref, *rest):
        if reduce:
            buf, send_sems, recv_sems = rest
        else:
            buf = o_ref
            send_sems, recv_sems = rest
        x, y, c, _ = _place()
        me = 4 * x + 2 * y + c
        buf[me] = v_ref[...]
        cps = []
        for d in range(1, N_DEV):
            peer = (x if d & 4 == 0 else 1 - x, y if d & 2 == 0 else 1 - y, c if d & 1 == 0 else 1 - c)
            cp = pltpu.make_async_remote_copy(
                src_ref=v_ref, dst_ref=buf.at[me], send_sem=send_sems.at[d - 1], recv_sem=recv_sems.at[d - 1],
                device_id=peer, device_id_type=MESH)
            cp.start()
            cps.append((cp, peer))
        for d, (cp, (px, py, pc)) in enumerate(cps, start=1):
            pltpu.make_async_remote_copy(
                src_ref=v_ref, dst_ref=buf.at[4 * px + 2 * py + pc], send_sem=send_sems.at[d - 1],
                recv_sem=recv_sems.at[d - 1], device_id=(px, py, pc), device_id_type=MESH).wait_recv()
        for cp, _ in cps:
            cp.wait_send()
        if reduce:
            acc = buf[0]
            for i in range(1, N_DEV):
                acc = acc + buf[i]
            o_ref[...] = acc

    vm = pl.BlockSpec(memory_space=pltpu.VMEM)
    out_shape = jax.ShapeDtypeStruct((r, l) if reduce else (N_DEV, r, l), F32)
    scratch = ([pltpu.VMEM((N_DEV, r, l), F32)] if reduce else []) + [
        pltpu.SemaphoreType.DMA((N_DEV - 1,)), pltpu.SemaphoreType.DMA((N_DEV - 1,))]
    return pl.pallas_call(
        body, name=name, in_specs=[vm], out_specs=vm, out_shape=out_shape, scratch_shapes=scratch,
    )(v)


_BLOCK_BYTES = 3 * 512 * 1024


def _rows_per_block(r, cn, itemsize=4):
    best = 8
    for t in range(8, r + 1, 8):
        if r % t == 0 and t * cn * itemsize <= _BLOCK_BYTES:
            best = t
    return best


def _add_sibling_half(g4, recv, cvec, name):
    ns, r, cn = g4.shape
    hr = r // 2
    tr = _rows_per_block(hr, cn)
    nrb = hr // tr

    def body(c_ref, a_ref, b_ref, o_ref):
        o_ref[...] = (a_ref[...].astype(F32) + b_ref[...].astype(F32)).astype(o_ref.dtype)

    grid_spec = pltpu.PrefetchScalarGridSpec(
        num_scalar_prefetch=1, grid=(ns, nrb),
        in_specs=[pl.BlockSpec((None, tr, cn), lambda j, i, c: (j, c[0] * nrb + i, 0)),
                  pl.BlockSpec((None, tr, cn), lambda j, i, c: (j, i, 0))],
        out_specs=pl.BlockSpec((None, tr, cn), lambda j, i, c: (j, i, 0)))
    return pl.pallas_call(
        body, name=name, grid_spec=grid_spec, out_shape=jax.ShapeDtypeStruct((ns, hr, cn), BF16),
        compiler_params=_cparams(("parallel", "parallel")),
    )(cvec, g4, recv)


def _sum_chips(r4, t4, kvec, cvec, name):
    ns, hr, cn = r4.shape
    tr = _rows_per_block(hr, cn)
    nrb = hr // tr

    def body(k_ref, c_ref, r_ref, t_ref, o_ref):
        acc = t_ref[...].astype(F32)
        for dlt in range(1, ns):
            acc = acc + r_ref[(k_ref[0] + dlt) % ns].astype(F32)
        o_ref[...] = acc

    grid_spec = pltpu.PrefetchScalarGridSpec(
        num_scalar_prefetch=2, grid=(nrb,),
        in_specs=[pl.BlockSpec((ns, tr, cn), lambda i, k, c: (0, i, 0)),
                  pl.BlockSpec((None, tr, cn), lambda i, k, c: (k[0], i, 0))],
        out_specs=pl.BlockSpec((tr, cn), lambda i, k, c: (c[0] * nrb + i, 0)))
    return pl.pallas_call(
        body, name=name, grid_spec=grid_spec, out_shape=jax.ShapeDtypeStruct((2 * hr, cn), F32),
        compiler_params=_cparams(("parallel",)),
    )(kvec, cvec, r4, t4)


def _adamw(w, g, m, v, name):
    r, cn = w.shape
    tr = _rows_per_block(r, cn)
    c1 = 1.0 - ADAM_B1 ** ADAM_STEP
    c2 = 1.0 - ADAM_B2 ** ADAM_STEP

    def body(w_ref, g_ref, m_ref, v_ref, go_ref, d_ref, mo_ref, vo_ref):
        gv = g_ref[...]
        mn = ADAM_B1 * m_ref[...] + (1.0 - ADAM_B1) * gv
        vn = ADAM_B2 * v_ref[...] + (1.0 - ADAM_B2) * (gv * gv)
        go_ref[...] = gv
        mo_ref[...] = mn
        vo_ref[...] = vn
        d_ref[...] = -ADAM_LR * ((mn / c1) / (jnp.sqrt(vn / c2) + ADAM_EPS) + ADAM_WD * w_ref[...])

    spec = pl.BlockSpec((tr, cn), lambda i: (i, 0))
    return pl.pallas_call(
        body, name=name, grid=(r // tr,), in_specs=[spec] * 4, out_specs=[spec] * 4,
        out_shape=[jax.ShapeDtypeStruct((r, cn), F32)] * 4,
        compiler_params=_cparams(("parallel",)),
    )(w, g, m, v)


def _pack(arrs):
    flat = jnp.concatenate([a.reshape(-1).astype(F32) for a in arrs])
    n = flat.shape[0]
    tot = -(-n // (8 * LANES)) * (8 * LANES)
    return jnp.pad(flat, (0, tot - n)).reshape(tot // LANES, LANES)


def _unpack(packed, shapes):
    flat = packed.reshape(-1)
    out, off = [], 0
    for shp in shapes:
        sz = int(np.prod(shp))
        out.append(flat[off:off + sz].reshape(shp))
        off += sz
    return out


def _local_step(x2, tgt, wa4, woa4, ws4, wos4, conv_w_f, conv_b_f, norm_w_f, rel_bias, dt_bias, a_log, d_skip,
                ln_g, ln_b):
    s, d = x2.shape
    d_attn = woa4.shape[1]
    hpg = d_attn // HEAD_DIM
    d_inner = wos4.shape[1] * N_CHIPS
    ng = d_inner // SSM_GROUP_WIDTH
    n_heads = dt_bias.shape[1]
    conv_dim = conv_w_f.shape[1]
    assert n_heads == ng * HEADS_PER_SSM_GROUP and conv_dim == d_inner + 2 * ng * D_STATE
    assert wa4.shape[2] * N_CHIPS == 10 * d_attn
    ssm_cols = ws4.shape[2]
    ws = jnp.concatenate([ws4[j] for j in range(N_CHIPS)], axis=1)
    wzx, wdt = ws[:, :d_inner + conv_dim], ws[:, d_inner + conv_dim:]
    wos = wos4.reshape(d_inner, d)

    xb = _cast_bf16(x2, "cast_x")
    pa = _mm_nn_sharded(xb, wa4, BF16, "mm_in_attn")
    buckets = _bucket_tiles()
    bias = _bias_expand(rel_bias, buckets, hpg)
    og, lg = [], []
    for g, (_, dil) in enumerate(ATTN_PATTERNS):
        o_, l_ = _attn_fwd(pa, bias, g, dil, hpg)
        og.append(o_)
        lg.append(l_)
    o, lse, yat = _attn_combine(og, lg, pa, hpg)
    h0 = _mm_nn_sharded(yat, woa4, F32, "mm_out_attn")
    g0, b0, g1, b1 = ln_g[0:1], ln_b[0:1], ln_g[1:2], ln_b[1:2]
    xhat0, rstd0, x1b = _ln_fwd(x2, h0, g0, b0, "ln0_fwd")

    pzx = _mm_nn(x1b, wzx, BF16, "mm_in_ssm")
    dt_raw = _mm_nn(x1b, wdt, F32, "mm_in_dt")

    def pad_heads(t):
        t = t.reshape(t.shape[0], ng, HEADS_PER_SSM_GROUP).transpose(1, 0, 2)
        return jnp.pad(t, ((0, 0), (0, 0), (0, LANES - HEADS_PER_SSM_GROUP)))

    def unpad_heads(t):
        return t[:, :, :HEADS_PER_SSM_GROUP].transpose(1, 0, 2).reshape(t.shape[1], n_heads)

    dtp = pad_heads(dt_raw)
    alog_p, dtb_p = pad_heads(a_log), pad_heads(dt_bias)
    dsk_e = jnp.repeat(d_skip.reshape(ng, 1, HEADS_PER_SSM_GROUP), SSM_HEAD_DIM, axis=2)
    e = _expand_matrix()
    xbc = _conv_fwd(pzx, conv_w_f, conv_b_f, d_inner)
    y_ssd, states = _ssd_fwd(xbc, dtp, alog_p, dtb_p, dsk_e, e, d_inner)
    y3 = _gate_norm_fwd(y_ssd, pzx, norm_w_f)
    h1 = _mm_nn(y3, wos, F32, "mm_out_ssm")
    xhat1, rstd1, dy2, row_sq = _ln_fwd(xhat0, h1, g1, b1, "ln1_fwd_loss", affine_in=(g0, b0), target=tgt)
    loss_local = 0.5 * jnp.sum(row_sq) / d

    du1, du1b, dg1, db1 = _ln_bwd(dy2, xhat1, rstd1, g1, "ln1_bwd")
    dy3 = _mm_nt(du1b, wos, BF16, "mm_d_y3")
    g_wos = _mm_tn(y3, du1b, BF16, "mm_g_w_out_ssm").reshape(N_CHIPS, d_inner // N_CHIPS, d)
    dy_ssd, dz, d_nw = _gate_norm_bwd(dy3, y_ssd, pzx, norm_w_f)
    dxs, dbm, dcm, ddtp, d_alog, d_dtb, d_dsk = _ssd_bwd(xbc, dtp, alog_p, dtb_p, dsk_e, e, states, dy_ssd, d_inner)
    dpre, d_cw, d_cb = _conv_bwd_a(pzx, jnp.concatenate([dxs, dbm, dcm], axis=1), conv_w_f, conv_b_f, d_inner)
    dpzx = jnp.concatenate([dz, _conv_bwd_b(dpre, conv_w_f)], axis=1)
    ddt_raw = unpad_heads(ddtp)
    t1 = _mm_nt(ddt_raw, wdt, F32, "mm_d_x1_dt", add=du1, add_scale=DEEPNORM_ALPHA)
    dx1 = _mm_nt(dpzx, wzx, F32, "mm_d_x1", add=t1)
    g_ws = jnp.concatenate([_mm_tn(x1b, dpzx, BF16, "mm_g_w_in_ssm"), _mm_tn(x1b, ddt_raw, BF16, "mm_g_w_dt")], axis=1)
    g_ws = jnp.stack([g_ws[:, j * ssm_cols:(j + 1) * ssm_cols] for j in range(N_CHIPS)])

    du0, du0b, dg0, db0 = _ln_bwd(dx1, xhat0, rstd0, g0, "ln0_bwd")
    dyat = _mm_nt_sharded_k(du0b, woa4, BF16, "mm_d_yat")
    g_woa = _mm_tn(yat, du0b, BF16, "mm_g_w_out_attn", shard_cols=d // N_CHIPS)
    do, delta, dgate = _attn_pre_bwd(dyat, o, pa, hpg)
    pieces, dbt = [], []
    for g, (_, dil) in enumerate(ATTN_PATTERNS):
        dq, dk, dv, db_ = _attn_bwd(pa, bias, do, lse, delta, g, dil, hpg)
        pieces += [dq, dk, dv]
        dbt.append(db_)
    dpa = jnp.concatenate(pieces + [dgate], axis=1)
    grad_x = _mm_nt_sharded_k(dpa, wa4, F32, "mm_d_x0", add=du0, add_scale=DEEPNORM_ALPHA)[None]
    g_wa = _mm_tn(xb, dpa, BF16, "mm_g_w_in_attn", shard_cols=wa4.shape[2])
    d_rel = _bias_reduce(jnp.stack(dbt), buckets, hpg)[:, :, 0].T

    d_dsk_h = d_dsk.reshape(n_heads, SSM_HEAD_DIM).sum(axis=1)
    small_full = [d_rel, d_cw, d_cb, unpad_heads(d_dtb), unpad_heads(d_alog), d_dsk_h[None], d_nw,
                  jnp.concatenate([dg0, dg1], axis=0), jnp.concatenate([db0, db1], axis=0)]
    return loss_local, grad_x, [g_wa, g_woa, g_ws, g_wos], small_full


def kernel(x, w_in_attn, w_out_attn, rel_bias, w_in_ssm, conv_w, conv_b, dt_bias, a_log, d_skip, ssm_norm_w, w_out_ssm, ln_g, ln_b, loss_target, m_w_in_attn, m_w_out_attn, m_rel_bias, m_w_in_ssm, m_conv_w, m_conv_b, m_dt_bias, m_a_log, m_d_skip, m_ssm_norm_w, m_w_out_ssm, m_ln_g, m_ln_b, v_w_in_attn, v_w_out_attn, v_rel_bias, v_w_in_ssm, v_conv_w, v_conv_b, v_dt_bias, v_a_log, v_d_skip, v_ssm_norm_w, v_w_out_ssm, v_ln_g, v_ln_b):
    xi, yi, ci = lax.axis_index("x"), lax.axis_index("y"), lax.axis_index("c")
    chip = 2 * xi + yi
    cvec = jnp.reshape(ci, (1,)).astype(jnp.int32)
    kvec = jnp.reshape(chip, (1,)).astype(jnp.int32)

    wa4, woa4, ws4, wos4 = _gather_weights([
        _cast_to_slot(w_in_attn[0], kvec, "cast_w_in_attn"), _cast_to_slot(w_out_attn[0], kvec, "cast_w_out_attn"),
        _cast_to_slot(w_in_ssm[0], kvec, "cast_w_in_ssm"), _cast_to_slot(w_out_ssm[0], kvec, "cast_w_out_ssm")])
    cw_l, cb_l, nw_l = conv_w[0], conv_b[0], ssm_norm_w[0]
    vec_shapes = [cw_l.shape, cb_l.shape, nw_l.shape]
    vec_all = _all_gather_small(_pack([cw_l, cb_l, nw_l]), False, "gather_vectors")
    parts = [_unpack(vec_all[2 * j], vec_shapes) for j in range(N_CHIPS)]
    conv_w_f = jnp.concatenate([p[0] for p in parts], axis=1)
    conv_b_f = jnp.concatenate([p[1] for p in parts], axis=0)[None]
    norm_w_f = jnp.concatenate([p[2] for p in parts], axis=0)[None]

    loss_local, grad_x, gs, small_full = _local_step(
        x[0], loss_target[0], wa4, woa4, ws4, wos4, conv_w_f, conv_b_f, norm_w_f, rel_bias, dt_bias, a_log,
        d_skip, ln_g, ln_b)
    loss = lax.psum(loss_local, ("x", "y", "c"))

    names = ["w_in_attn", "w_out_attn", "w_in_ssm", "w_out_ssm"]
    rs = _swap_halves(gs)
    ts = [_add_sibling_half(g_, r_, cvec, "add_sibling_" + nm) for g_, r_, nm in zip(gs, rs, names)]
    fs = [_sum_chips(r_, t_, kvec, cvec, "sum_chips_" + nm) for r_, t_, nm in zip(_scatter_to_chips(ts), ts, names)]
    full = _join_halves(fs)
    big = {}
    for nm, gf, w_, m_, v_ in zip(names, full, (w_in_attn, w_out_attn, w_in_ssm, w_out_ssm),
                                  (m_w_in_attn, m_w_out_attn, m_w_in_ssm, m_w_out_ssm),
                                  (v_w_in_attn, v_w_out_attn, v_w_in_ssm, v_w_out_ssm)):
        big[nm] = [t[None] for t in _adamw(w_[0], gf, m_[0], v_[0], "adamw_" + nm)]

    summed = _unpack(_all_gather_small(_pack(small_full), True, "reduce_small_grads"),
                     [t.shape for t in small_full])
    s_rel, s_cw, s_cb, s_dtb, s_alog, s_dsk, s_nw, s_lng, s_lnb = summed
    cwc, nwc = conv_w.shape[2], ssm_norm_w.shape[1]
    s_cw = lax.dynamic_slice_in_dim(s_cw, chip * cwc, cwc, axis=1)[None]
    s_cb = lax.dynamic_slice_in_dim(s_cb, chip * cwc, cwc, axis=1)
    s_nw = lax.dynamic_slice_in_dim(s_nw, chip * nwc, nwc, axis=1)
    small_names = ["rel_bias", "conv_w", "conv_b", "dt_bias", "a_log", "d_skip", "ssm_norm_w", "ln_g", "ln_b"]
    small_g = [s_rel, s_cw, s_cb, s_dtb, s_alog, s_dsk, s_nw, s_lng, s_lnb]
    small_w = [rel_bias, conv_w, conv_b, dt_bias, a_log, d_skip, ssm_norm_w, ln_g, ln_b]
    small_m = [m_rel_bias, m_conv_w, m_conv_b, m_dt_bias, m_a_log, m_d_skip, m_ssm_norm_w, m_ln_g, m_ln_b]
    small_v = [v_rel_bias, v_conv_w, v_conv_b, v_dt_bias, v_a_log, v_d_skip, v_ssm_norm_w, v_ln_g, v_ln_b]
    shapes = [t.shape for t in small_w]
    res = _adamw(_pack(small_w), _pack(small_g), _pack(small_m), _pack(small_v), "adamw_small")
    small = {nm: [] for nm in small_names}
    for packed in res:
        for nm, t in zip(small_names, _unpack(packed, shapes)):
            small[nm].append(t)

    order = ["w_in_attn", "w_out_attn", "rel_bias", "w_in_ssm", "conv_w", "conv_b", "dt_bias", "a_log",
             "d_skip", "ssm_norm_w", "w_out_ssm", "ln_g", "ln_b"]
    table = {**big, **small}
    outs = [loss, grad_x]
    for kind in range(4):
        outs += [table[nm][kind] for nm in order]
    return tuple(outs)
```

```python
import functools
import math

import numpy as np
import jax
import jax.numpy as jnp
from jax import lax
from jax.experimental import pallas as pl
from jax.experimental.pallas import tpu as pltpu

F32 = jnp.float32
BF16 = jnp.bfloat16
MESH = pl.DeviceIdType.MESH

ATTN_PATTERNS = ((128, 1), (512, 4), (2048, 16))
N_GROUPS_ATTN = 3
HEAD_DIM = 128
ATTN_BLOCK = 128
NUM_BUCKETS = 32
MAX_DISTANCE = 2048
SSM_HEAD_DIM = 64
HEADS_PER_SSM_GROUP = 16
SSM_GROUP_WIDTH = HEADS_PER_SSM_GROUP * SSM_HEAD_DIM
D_STATE = 128
CONV_WIDTH = 4
CHUNK = 128
DEPTH = 2
DEEPNORM_ALPHA = (2 * DEPTH) ** 0.25
LN_EPS = 1e-5
RMS_EPS = 1e-5
NEG_INF = -1e30
ADAM_LR = 0.001
ADAM_B1 = 0.9
ADAM_B2 = 0.999
ADAM_EPS = 1e-08
ADAM_WD = 0.01
ADAM_STEP = 10

N_CHIPS = 4
N_DEV = 8

VMEM_LIMIT_V7X = 56 * 1024 * 1024
LANES = 128


def _cparams(sem=None):
    return pltpu.CompilerParams(dimension_semantics=sem, vmem_limit_bytes=VMEM_LIMIT_V7X)


def _sigmoid(x):
    return 1.0 / (1.0 + jnp.exp(-x))


def _dot(a, b):
    return jnp.dot(a, b, preferred_element_type=F32)


def _dot_nt(a, b):
    return lax.dot_general(a, b, (((1,), (1,)), ((), ())), preferred_element_type=F32)


def _dot_tn(a, b):
    return lax.dot_general(a, b, (((0,), (0,)), ((), ())), preferred_element_type=F32)


def _split2(x):
    hi = x.astype(BF16)
    lo = (x - hi.astype(F32)).astype(BF16)
    return hi, lo


def _split3(x):
    hi = x.astype(BF16)
    r = x - hi.astype(F32)
    mid = r.astype(BF16)
    lo = (r - mid.astype(F32)).astype(BF16)
    return hi, mid, lo


def _matmul(a, b, *, mode, grid, a_spec, b_spec, out_shape, out_spec, tile, name,
            add=None, add_spec=None, add_scale=1.0, after=None):
    nk = grid[2]
    tm, tn = tile
    dot = {"nn": _dot, "nt": _dot_nt, "tn": _dot_tn}[mode]
    has_add = add is not None
    has_after = after is not None

    def body(*refs):
        a_ref, b_ref = refs[:2]
        add_ref = refs[2] if has_add else None
        o_ref, acc_ref = refs[-2:]
        k = pl.program_id(2)

        @pl.when(k == 0)
        def _():
            acc_ref[...] = jnp.zeros_like(acc_ref)

        acc_ref[...] += dot(a_ref[...].astype(BF16), b_ref[...].astype(BF16))

        @pl.when(k == nk - 1)
        def _():
            r = acc_ref[...]
            if has_add:
                r = r + add_scale * add_ref[...].astype(F32)
            o_ref[...] = r.astype(o_ref.dtype)

    in_specs = [a_spec, b_spec] + ([add_spec] if has_add else []) + ([_ANY] if has_after else [])
    args = (a, b) + ((add,) if has_add else ()) + ((after,) if has_after else ())
    return pl.pallas_call(
        body, name=name, grid=grid, in_specs=in_specs, out_specs=out_spec, out_shape=out_shape,
        scratch_shapes=[pltpu.VMEM((tm, tn), F32)],
        compiler_params=_cparams(("parallel", "parallel", "arbitrary")),
    )(*args)


def _pick(n, pref):
    for t in pref:
        if n % t == 0:
            return t
    return n


_TILE_PREF = (1024, 512, 256, 128)


def _mm_nn_sharded(a, w4, out_dtype, name, after=None):
    m, k = a.shape
    _, _, nn = w4.shape
    tm, tk, tn = _pick(m, _TILE_PREF), _pick(k, _TILE_PREF), _pick(nn, _TILE_PREF)
    npb = nn // tn
    return _matmul(
        a, w4, mode="nn", grid=(m // tm, N_CHIPS * npb, k // tk), tile=(tm, tn), name=name,
        a_spec=pl.BlockSpec((tm, tk), lambda i, j, kk: (i, kk)),
        b_spec=pl.BlockSpec((None, tk, tn), lambda i, j, kk: (j // npb, kk, j % npb)),
        out_shape=jax.ShapeDtypeStruct((m, N_CHIPS * nn), out_dtype),
        out_spec=pl.BlockSpec((tm, tn), lambda i, j, kk: (i, j)), after=after)


def _mm_nn(a, b, out_dtype, name):
    m, k = a.shape
    _, n = b.shape
    tm, tk, tn = _pick(m, _TILE_PREF), _pick(k, _TILE_PREF), _pick(n, _TILE_PREF)
    return _matmul(
        a, b, mode="nn", grid=(m // tm, n // tn, k // tk), tile=(tm, tn), name=name,
        a_spec=pl.BlockSpec((tm, tk), lambda i, j, kk: (i, kk)),
        b_spec=pl.BlockSpec((tk, tn), lambda i, j, kk: (kk, j)),
        out_shape=jax.ShapeDtypeStruct((m, n), out_dtype),
        out_spec=pl.BlockSpec((tm, tn), lambda i, j, kk: (i, j)))


def _mm_nt(a, b, out_dtype, name, add=None, add_scale=1.0):
    m, k = a.shape
    n, _ = b.shape
    tm, tk, tn = _pick(m, _TILE_PREF), _pick(k, _TILE_PREF), _pick(n, _TILE_PREF)
    return _matmul(
        a, b, mode="nt", grid=(m // tm, n // tn, k // tk), tile=(tm, tn), name=name,
        a_spec=pl.BlockSpec((tm, tk), lambda i, j, kk: (i, kk)),
        b_spec=pl.BlockSpec((tn, tk), lambda i, j, kk: (j, kk)),
        out_shape=jax.ShapeDtypeStruct((m, n), out_dtype),
        out_spec=pl.BlockSpec((tm, tn), lambda i, j, kk: (i, j)),
        add=add, add_spec=pl.BlockSpec((tm, tn), lambda i, j, kk: (i, j)), add_scale=add_scale)


def _mm_nt_sharded_k(a, w4, out_dtype, name, add=None, add_scale=1.0, after=None):
    m, _ = a.shape
    _, n, kn = w4.shape
    tm, tk, tn = _pick(m, _TILE_PREF), _pick(kn, _TILE_PREF), _pick(n, _TILE_PREF)
    kpb = kn // tk
    return _matmul(
        a, w4, mode="nt", grid=(m // tm, n // tn, N_CHIPS * kpb), tile=(tm, tn), name=name,
        a_spec=pl.BlockSpec((tm, tk), lambda i, j, kk: (i, kk)),
        b_spec=pl.BlockSpec((None, tn, tk), lambda i, j, kk: (kk // kpb, j, kk % kpb)),
        out_shape=jax.ShapeDtypeStruct((m, n), out_dtype),
        out_spec=pl.BlockSpec((tm, tn), lambda i, j, kk: (i, j)),
        add=add, add_spec=pl.BlockSpec((tm, tn), lambda i, j, kk: (i, j)), add_scale=add_scale, after=after)


def _mm_tn(a, b, out_dtype, name, shard_cols=None):
    k, m = a.shape
    _, n = b.shape
    nn = n if shard_cols is None else shard_cols
    tm, tk, tn = _pick(m, _TILE_PREF), _pick(k, _TILE_PREF), _pick(nn, _TILE_PREF)
    if shard_cols is None:
        out_shape = jax.ShapeDtypeStruct((m, n), out_dtype)
        out_spec = pl.BlockSpec((tm, tn), lambda i, j, kk: (i, j))
    else:
        npb = nn // tn
        out_shape = jax.ShapeDtypeStruct((n // nn, m, nn), out_dtype)
        out_spec = pl.BlockSpec((None, tm, tn), lambda i, j, kk: (j // npb, i, j % npb))
    return _matmul(
        a, b, mode="tn", grid=(m // tm, n // tn, k // tk), tile=(tm, tn), name=name,
        a_spec=pl.BlockSpec((tk, tm), lambda i, j, kk: (kk, i)),
        b_spec=pl.BlockSpec((tk, tn), lambda i, j, kk: (kk, j)),
        out_shape=out_shape, out_spec=out_spec)


def _cast_bf16(x, name):
    r, c = x.shape
    tr = _pick(r, (512, 256, 128, 8))

    def body(x_ref, o_ref):
        o_ref[...] = x_ref[...].astype(BF16)

    return pl.pallas_call(
        body, name=name, grid=(r // tr,),
        in_specs=[pl.BlockSpec((tr, c), lambda i: (i, 0))],
        out_specs=pl.BlockSpec((tr, c), lambda i: (i, 0)),
        out_shape=jax.ShapeDtypeStruct((r, c), BF16),
        compiler_params=_cparams(("parallel",)),
    )(x)


def _bucket_tiles():
    qi = np.arange(ATTN_BLOCK)[:, None]
    ki = np.arange(2 * ATTN_BLOCK)[None, :]
    delta = np.clip(ATTN_BLOCK + qi - ki, 0, None)
    tiles = []
    max_exact = NUM_BUCKETS // 2
    for _, dil in ATTN_PATTERNS:
        dist = (delta * dil).astype(np.int32)
        d_f = np.maximum(dist, 1).astype(np.float32)
        large = max_exact + (np.log(d_f / np.float32(max_exact)) / np.float32(math.log(MAX_DISTANCE / max_exact))
                             * np.float32(NUM_BUCKETS - max_exact)).astype(np.int32)
        large = np.minimum(large, NUM_BUCKETS - 1)
        tiles.append(np.where(dist < max_exact, dist, large).astype(np.int32))
    return jnp.asarray(np.stack(tiles))


def _bias_expand(rel_bias, buckets, hpg):
    def body(tab_ref, bk_ref, o_ref):
        g, h = pl.program_id(0), pl.program_id(1)
        bk = bk_ref[...]
        acc = jnp.zeros((ATTN_BLOCK, 2 * ATTN_BLOCK), F32)
        for b in range(NUM_BUCKETS):
            acc = jnp.where(bk == b, tab_ref[b, g * hpg + h], acc)
        o_ref[...] = acc

    return pl.pallas_call(
        body, name="bias_expand", grid=(N_GROUPS_ATTN, hpg),
        in_specs=[pl.BlockSpec(memory_space=pltpu.SMEM),
                  pl.BlockSpec((None, ATTN_BLOCK, 2 * ATTN_BLOCK), lambda g, h: (g, 0, 0))],
        out_specs=pl.BlockSpec((None, None, ATTN_BLOCK, 2 * ATTN_BLOCK), lambda g, h: (g, h, 0, 0)),
        out_shape=jax.ShapeDtypeStruct((N_GROUPS_ATTN, hpg, ATTN_BLOCK, 2 * ATTN_BLOCK), F32),
        compiler_params=_cparams(("parallel", "parallel")),
    )(rel_bias, buckets)


def _bias_reduce(dtiles, buckets, hpg):
    def body(t_ref, bk_ref, o_ref):
        bk = bk_ref[...]
        t = t_ref[...]
        rows = lax.broadcasted_iota(jnp.int32, (NUM_BUCKETS, LANES), 0)
        acc = jnp.zeros((NUM_BUCKETS, LANES), F32)
        for b in range(NUM_BUCKETS):
            s = jnp.sum(jnp.sum(jnp.where(bk == b, t, 0.0), axis=1, keepdims=True), axis=0, keepdims=True)
            acc = jnp.where(rows == b, s, acc)
        o_ref[...] = acc

    return pl.pallas_call(
        body, name="bias_reduce", grid=(N_GROUPS_ATTN, hpg),
        in_specs=[pl.BlockSpec((None, None, ATTN_BLOCK, 2 * ATTN_BLOCK), lambda g, h: (g, h, 0, 0)),
                  pl.BlockSpec((None, ATTN_BLOCK, 2 * ATTN_BLOCK), lambda g, h: (g, 0, 0))],
        out_specs=pl.BlockSpec((None, NUM_BUCKETS, LANES), lambda g, h: (g * hpg + h, 0, 0)),
        out_shape=jax.ShapeDtypeStruct((N_GROUPS_ATTN * hpg, NUM_BUCKETS, LANES), F32),
        compiler_params=_cparams(("parallel", "parallel")),
    )(dtiles, buckets)


def _attn_valid(n_is_first):
    qi = lax.broadcasted_iota(jnp.int32, (ATTN_BLOCK, 2 * ATTN_BLOCK), 0)
    ki = lax.broadcasted_iota(jnp.int32, (ATTN_BLOCK, 2 * ATTN_BLOCK), 1)
    delta = ATTN_BLOCK + qi - ki
    band = (delta >= 0) & (delta <= ATTN_BLOCK)
    return band & (jnp.logical_not(n_is_first) | (ki >= ATTN_BLOCK))


def _attn_fwd(pa, bias, g, dil, hpg):
    s, c = pa.shape
    w = hpg * HEAD_DIM
    cpb = c // w
    rows = s // dil
    nb = rows // ATTN_BLOCK
    pav = pa.reshape(rows, dil * c)
    scale = HEAD_DIM ** -0.5

    def body(q_ref, kc_ref, kp_ref, vc_ref, vp_ref, bias_ref, o_ref, lse_ref):
        valid = _attn_valid(pl.program_id(1) == 0)
        for h in range(hpg):
            sl = slice(h * HEAD_DIM, (h + 1) * HEAD_DIM)
            k2 = jnp.concatenate([kp_ref[:, sl], kc_ref[:, sl]], axis=0)
            v2 = jnp.concatenate([vp_ref[:, sl], vc_ref[:, sl]], axis=0)
            sc = _dot_nt(q_ref[:, sl], k2) * scale + bias_ref[h]
            sc = jnp.where(valid, sc, NEG_INF)
            m = jnp.max(sc, axis=1, keepdims=True)
            p = jnp.exp(sc - m)
            l = jnp.sum(p, axis=1, keepdims=True)
            o_ref[:, sl] = _dot(p.astype(BF16), v2) / l
            lse_ref[:, sl] = jnp.broadcast_to(m + jnp.log(l), (ATTN_BLOCK, HEAD_DIM))

    def col(off):
        return lambda r, n: (n, r * cpb + 3 * g + off)

    def colp(off):
        return lambda r, n: (jnp.maximum(n - 1, 0), r * cpb + 3 * g + off)

    blk = (ATTN_BLOCK, w)
    o, lse = pl.pallas_call(
        body, name=f"attn_fwd_g{g}", grid=(dil, nb),
        in_specs=[pl.BlockSpec(blk, col(0)), pl.BlockSpec(blk, col(1)), pl.BlockSpec(blk, colp(1)),
                  pl.BlockSpec(blk, col(2)), pl.BlockSpec(blk, colp(2)),
                  pl.BlockSpec((None, hpg, ATTN_BLOCK, 2 * ATTN_BLOCK), lambda r, n: (g, 0, 0, 0))],
        out_specs=[pl.BlockSpec(blk, lambda r, n: (n, r)), pl.BlockSpec(blk, lambda r, n: (n, r))],
        out_shape=[jax.ShapeDtypeStruct((rows, dil * w), F32), jax.ShapeDtypeStruct((rows, dil * w), F32)],
        compiler_params=_cparams(("parallel", "parallel")),
    )(pav, pav, pav, pav, pav, bias)
    return o.reshape(s, w), lse.reshape(s, w)


def _attn_combine(os_, lses, pa, hpg):
    s, w = os_[0].shape
    gate_blk = pa.shape[1] // w - 1
    tm = _pick(s, (256, 128))

    def body(o0, o1, o2, l0, l1, l2, gate_ref, o_ref, lse_ref, y_ref):
        a0, a1, a2 = l0[...], l1[...], l2[...]
        m = jnp.maximum(jnp.maximum(a0, a1), a2)
        e0, e1, e2 = jnp.exp(a0 - m), jnp.exp(a1 - m), jnp.exp(a2 - m)
        den = e0 + e1 + e2
        o = (e0 * o0[...] + e1 * o1[...] + e2 * o2[...]) / den
        gate = gate_ref[...].astype(F32)
        o_ref[...] = o.astype(BF16)
        lse_ref[...] = m + jnp.log(den)
        y_ref[...] = (o * (gate * _sigmoid(gate))).astype(BF16)

    spec = pl.BlockSpec((tm, w), lambda i: (i, 0))
    return pl.pallas_call(
        body, name="attn_combine", grid=(s // tm,),
        in_specs=[spec] * 6 + [pl.BlockSpec((tm, w), lambda i: (i, gate_blk))],
        out_specs=[spec, spec, spec],
        out_shape=[jax.ShapeDtypeStruct((s, w), BF16), jax.ShapeDtypeStruct((s, w), F32),
                   jax.ShapeDtypeStruct((s, w), BF16)],
        compiler_params=_cparams(("parallel",)),
    )(*os_, *lses, pa)


def _attn_pre_bwd(dy, o, pa, hpg):
    s, w = dy.shape
    gate_blk = pa.shape[1] // w - 1
    tm = _pick(s, (256, 128))

    def body(dy_ref, o_ref, gate_ref, do_ref, dl_ref, dg_ref):
        gate = gate_ref[...].astype(F32)
        sg = _sigmoid(gate)
        dyv = dy_ref[...].astype(F32)
        ov = o_ref[...].astype(F32)
        do = dyv * (gate * sg)
        do_ref[...] = do.astype(BF16)
        dg_ref[...] = (dyv * ov * (sg * (1.0 + gate * (1.0 - sg)))).astype(BF16)
        prod = do * ov
        for h in range(hpg):
            sl = slice(h * HEAD_DIM, (h + 1) * HEAD_DIM)
            dl_ref[:, sl] = jnp.broadcast_to(jnp.sum(prod[:, sl], axis=1, keepdims=True), (tm, HEAD_DIM))

    spec = pl.BlockSpec((tm, w), lambda i: (i, 0))
    return pl.pallas_call(
        body, name="attn_pre_bwd", grid=(s // tm,),
        in_specs=[spec, spec, pl.BlockSpec((tm, w), lambda i: (i, gate_blk))],
        out_specs=[spec, spec, spec],
        out_shape=[jax.ShapeDtypeStruct((s, w), BF16), jax.ShapeDtypeStruct((s, w), F32),
                   jax.ShapeDtypeStruct((s, w), BF16)],
        compiler_params=_cparams(("parallel",)),
    )(dy, o, pa)


def _attn_bwd(pa, bias, do, lse, delta, g, dil, hpg):
    s, c = pa.shape
    w = hpg * HEAD_DIM
    cpb = c // w
    rows = s // dil
    nb = rows // ATTN_BLOCK
    pav = pa.reshape(rows, dil * c)
    dov, lsev, dlv = (t.reshape(rows, dil * w) for t in (do, lse, delta))
    scale = HEAD_DIM ** -0.5

    def body(q_ref, kc_ref, kp_ref, vc_ref, vp_ref, bias_ref, do_ref, lse_ref, dl_ref,
             dq_ref, dk_ref, dv_ref, db_ref, dkc_ref, dvc_ref):
        r, i = pl.program_id(0), pl.program_id(1)
        n = nb - 1 - i
        valid = _attn_valid(n == 0)

        @pl.when((r == 0) & (i == 0))
        def _():
            db_ref[...] = jnp.zeros_like(db_ref)

        @pl.when(i == 0)
        def _():
            dkc_ref[...] = jnp.zeros_like(dkc_ref)
            dvc_ref[...] = jnp.zeros_like(dvc_ref)

        for h in range(hpg):
            sl = slice(h * HEAD_DIM, (h + 1) * HEAD_DIM)
            q = q_ref[:, sl]
            dov_ = do_ref[:, sl]
            k2 = jnp.concatenate([kp_ref[:, sl], kc_ref[:, sl]], axis=0)
            v2 = jnp.concatenate([vp_ref[:, sl], vc_ref[:, sl]], axis=0)
            sc = _dot_nt(q, k2) * scale + bias_ref[h]
            p = jnp.exp(jnp.where(valid, sc - lse_ref[:, sl][:, 0:1], NEG_INF))
            dp = _dot_nt(dov_, v2)
            ds = p * (dp - dl_ref[:, sl][:, 0:1])
            db_ref[h] += ds
            dsb = ds.astype(BF16)
            dq_ref[:, sl] = (_dot(dsb, k2) * scale).astype(BF16)
            dk2 = _dot_tn(dsb, q) * scale
            dv2 = _dot_tn(p.astype(BF16), dov_)
            dk_ref[:, sl] = (dk2[ATTN_BLOCK:] + dkc_ref[:, sl]).astype(BF16)
            dv_ref[:, sl] = (dv2[ATTN_BLOCK:] + dvc_ref[:, sl]).astype(BF16)
            dkc_ref[:, sl] = dk2[:ATTN_BLOCK]
            dvc_ref[:, sl] = dv2[:ATTN_BLOCK]

    def col(off):
        return lambda r, i: (nb - 1 - i, r * cpb + 3 * g + off)

    def colp(off):
        return lambda r, i: (jnp.maximum(nb - 2 - i, 0), r * cpb + 3 * g + off)

    blk = (ATTN_BLOCK, w)
    tok = pl.BlockSpec(blk, lambda r, i: (nb - 1 - i, r))
    dq, dk, dv, db = pl.pallas_call(
        body, name=f"attn_bwd_g{g}", grid=(dil, nb),
        in_specs=[pl.BlockSpec(blk, col(0)), pl.BlockSpec(blk, col(1)), pl.BlockSpec(blk, colp(1)),
                  pl.BlockSpec(blk, col(2)), pl.BlockSpec(blk, colp(2)),
                  pl.BlockSpec((None, hpg, ATTN_BLOCK, 2 * ATTN_BLOCK), lambda r, i: (g, 0, 0, 0)),
                  tok, tok, tok],
        out_specs=[tok, tok, tok,
                   pl.BlockSpec((hpg, ATTN_BLOCK, 2 * ATTN_BLOCK), lambda r, i: (0, 0, 0))],
        out_shape=[jax.ShapeDtypeStruct((rows, dil * w), BF16)] * 3
        + [jax.ShapeDtypeStruct((hpg, ATTN_BLOCK, 2 * ATTN_BLOCK), F32)],
        scratch_shapes=[pltpu.VMEM(blk, F32), pltpu.VMEM(blk, F32)],
        compiler_params=_cparams(("arbitrary", "arbitrary")),
    )(pav, pav, pav, pav, pav, bias, dov, lsev, dlv)
    return dq.reshape(s, w), dk.reshape(s, w), dv.reshape(s, w), db


def _ln_fwd(xin, h, gamma, beta, name, affine_in=None, target=None):
    s, d = xin.shape
    tm = _pick(s, (128,))
    has_aff = affine_in is not None
    has_tgt = target is not None

    def body(*refs):
        it = iter(refs)
        x_ref, h_ref, g_ref, b_ref = next(it), next(it), next(it), next(it)
        if has_aff:
            gi_ref, bi_ref = next(it), next(it)
        if has_tgt:
            t_ref = next(it)
        xh_ref, rs_ref = next(it), next(it)
        x = x_ref[...]
        if has_aff:
            x = x * gi_ref[...] + bi_ref[...]
        u = DEEPNORM_ALPHA * x + h_ref[...]
        mu = jnp.mean(u, axis=1, keepdims=True)
        uc = u - mu
        var = jnp.mean(uc * uc, axis=1, keepdims=True)
        rstd = lax.rsqrt(var + LN_EPS)
        xhat = uc * rstd
        xh_ref[...] = xhat
        rs_ref[...] = rstd
        y = xhat * g_ref[...] + b_ref[...]
        if has_tgt:
            dy_ref, l_ref = next(it), next(it)
            e = y - t_ref[...]
            dy_ref[...] = e * (1.0 / d)
            l_ref[...] = jnp.sum(e * e, axis=1, keepdims=True)
        else:
            y_ref = next(it)
            y_ref[...] = y.astype(BF16)

    row = pl.BlockSpec((tm, d), lambda i: (i, 0))
    vec = pl.BlockSpec((1, d), lambda i: (0, 0))
    one = pl.BlockSpec((tm, 1), lambda i: (i, 0))
    in_specs = [row, row, vec, vec] + ([vec, vec] if has_aff else []) + ([row] if has_tgt else [])
    args = [xin, h, gamma, beta] + (list(affine_in) if has_aff else []) + ([target] if has_tgt else [])
    out_specs = [row, one] + ([row, one] if has_tgt else [row])
    out_shape = [jax.ShapeDtypeStruct((s, d), F32), jax.ShapeDtypeStruct((s, 1), F32)]
    out_shape += ([jax.ShapeDtypeStruct((s, d), F32), jax.ShapeDtypeStruct((s, 1), F32)] if has_tgt
                  else [jax.ShapeDtypeStruct((s, d), BF16)])
    return pl.pallas_call(
        body, name=name, grid=(s // tm,), in_specs=in_specs, out_specs=out_specs, out_shape=out_shape,
        compiler_params=_cparams(("parallel",)),
    )(*args)


def _ln_bwd(dy, xhat, rstd, gamma, name):
    s, d = dy.shape
    tm = _pick(s, (128,))

    def body(dy_ref, xh_ref, rs_ref, g_ref, du_ref, dub_ref, dg_ref, db_ref):
        @pl.when(pl.program_id(0) == 0)
        def _():
            dg_ref[...] = jnp.zeros_like(dg_ref)
            db_ref[...] = jnp.zeros_like(db_ref)

        dyv = dy_ref[...]
        xh = xh_ref[...]
        dg_ref[...] += jnp.sum(dyv * xh, axis=0, keepdims=True)
        db_ref[...] += jnp.sum(dyv, axis=0, keepdims=True)
        dxh = dyv * g_ref[...]
        m1 = jnp.mean(dxh, axis=1, keepdims=True)
        m2 = jnp.mean(dxh * xh, axis=1, keepdims=True)
        du = rs_ref[...] * (dxh - m1 - xh * m2)
        du_ref[...] = du
        dub_ref[...] = du.astype(BF16)

    row = pl.BlockSpec((tm, d), lambda i: (i, 0))
    vec = pl.BlockSpec((1, d), lambda i: (0, 0))
    one = pl.BlockSpec((tm, 1), lambda i: (i, 0))
    return pl.pallas_call(
        body, name=name, grid=(s // tm,), in_specs=[row, row, one, vec],
        out_specs=[row, row, vec, vec],
        out_shape=[jax.ShapeDtypeStruct((s, d), F32), jax.ShapeDtypeStruct((s, d), BF16),
                   jax.ShapeDtypeStruct((1, d), F32), jax.ShapeDtypeStruct((1, d), F32)],
        compiler_params=_cparams(("arbitrary",)),
    )(dy, xhat, rstd, gamma)


_HALO = 16


def _conv_taps(ext, tm, w_ref):
    acc = None
    for k in range(CONV_WIDTH):
        lo = _HALO - (CONV_WIDTH - 1) + k
        term = w_ref[k:k + 1, :] * ext[lo:lo + tm, :]
        acc = term if acc is None else acc + term
    return acc


def _conv_fwd(pzx, conv_w, conv_b, d_inner):
    s, _ = pzx.shape
    cd = conv_w.shape[1]
    tm = _pick(s, (512, 256, 128))
    tc = _pick(cd, (1024, 512, 256, 128))
    off = d_inner // tc
    hb = tm // _HALO

    def body(x_ref, p_ref, w_ref, b_ref, o_ref):
        prev = jnp.where(pl.program_id(0) > 0, p_ref[...].astype(F32), 0.0)
        ext = jnp.concatenate([prev, x_ref[...].astype(F32)], axis=0)
        pre = _conv_taps(ext, tm, w_ref) + b_ref[...]
        o_ref[...] = (pre * _sigmoid(pre)).astype(BF16)

    return pl.pallas_call(
        body, name="conv_fwd", grid=(s // tm, cd // tc),
        in_specs=[pl.BlockSpec((tm, tc), lambda i, j: (i, off + j)),
                  pl.BlockSpec((_HALO, tc), lambda i, j: (jnp.maximum(i * hb - 1, 0), off + j)),
                  pl.BlockSpec((CONV_WIDTH, tc), lambda i, j: (0, j)),
                  pl.BlockSpec((1, tc), lambda i, j: (0, j))],
        out_specs=pl.BlockSpec((tm, tc), lambda i, j: (i, j)),
        out_shape=jax.ShapeDtypeStruct((s, cd), BF16),
        compiler_params=_cparams(("parallel", "parallel")),
    )(pzx, pzx, conv_w, conv_b)


def _conv_bwd_a(pzx, dxbc, conv_w, conv_b, d_inner):
    s, _ = pzx.shape
    cd = conv_w.shape[1]
    tm = _pick(s, (512, 256, 128))
    tc = _pick(cd, (1024, 512, 256, 128))
    off = d_inner // tc
    hb = tm // _HALO

    def body(x_ref, p_ref, d_ref, w_ref, b_ref, o_ref, dw_ref, db_ref):
        @pl.when(pl.program_id(1) == 0)
        def _():
            dw_ref[...] = jnp.zeros_like(dw_ref)
            db_ref[...] = jnp.zeros_like(db_ref)

        prev = jnp.where(pl.program_id(1) > 0, p_ref[...].astype(F32), 0.0)
        ext = jnp.concatenate([prev, x_ref[...].astype(F32)], axis=0)
        pre = _conv_taps(ext, tm, w_ref) + b_ref[...]
        sg = _sigmoid(pre)
        dpre = d_ref[...].astype(F32) * (sg * (1.0 + pre * (1.0 - sg)))
        o_ref[...] = dpre
        db_ref[...] += jnp.sum(dpre, axis=0, keepdims=True)
        for k in range(CONV_WIDTH):
            lo = _HALO - (CONV_WIDTH - 1) + k
            dw_ref[k:k + 1, :] += jnp.sum(dpre * ext[lo:lo + tm, :], axis=0, keepdims=True)

    return pl.pallas_call(
        body, name="conv_bwd_a", grid=(cd // tc, s // tm),
        in_specs=[pl.BlockSpec((tm, tc), lambda j, i: (i, off + j)),
                  pl.BlockSpec((_HALO, tc), lambda j, i: (jnp.maximum(i * hb - 1, 0), off + j)),
                  pl.BlockSpec((tm, tc), lambda j, i: (i, j)),
                  pl.BlockSpec((CONV_WIDTH, tc), lambda j, i: (0, j)),
                  pl.BlockSpec((1, tc), lambda j, i: (0, j))],
        out_specs=[pl.BlockSpec((tm, tc), lambda j, i: (i, j)),
                   pl.BlockSpec((CONV_WIDTH, tc), lambda j, i: (0, j)),
                   pl.BlockSpec((1, tc), lambda j, i: (0, j))],
        out_shape=[jax.ShapeDtypeStruct((s, cd), F32), jax.ShapeDtypeStruct((CONV_WIDTH, cd), F32),
                   jax.ShapeDtypeStruct((1, cd), F32)],
        compiler_params=_cparams(("parallel", "arbitrary")),
    )(pzx, pzx, dxbc, conv_w, conv_b)


def _conv_bwd_b(dpre, conv_w):
    s, cd = dpre.shape
    tm = _pick(s, (512, 256, 128))
    tc = _pick(cd, (1024, 512, 256, 128))
    hb = tm // 8
    nrb = s // tm

    def body(x_ref, nx_ref, w_ref, o_ref):
        nxt = jnp.where(pl.program_id(0) < nrb - 1, nx_ref[...], 0.0)
        ext = jnp.concatenate([x_ref[...], nxt], axis=0)
        acc = None
        for k in range(CONV_WIDTH):
            lo = CONV_WIDTH - 1 - k
            term = w_ref[k:k + 1, :] * ext[lo:lo + tm, :]
            acc = term if acc is None else acc + term
        o_ref[...] = acc.astype(BF16)

    return pl.pallas_call(
        body, name="conv_bwd_b", grid=(nrb, cd // tc),
        in_specs=[pl.BlockSpec((tm, tc), lambda i, j: (i, j)),
                  pl.BlockSpec((8, tc), lambda i, j: (jnp.minimum((i + 1) * hb, s // 8 - 1), j)),
                  pl.BlockSpec((CONV_WIDTH, tc), lambda i, j: (0, j))],
        out_specs=pl.BlockSpec((tm, tc), lambda i, j: (i, j)),
        out_shape=jax.ShapeDtypeStruct((s, cd), BF16),
        compiler_params=_cparams(("parallel", "parallel")),
    )(dpre, dpre, conv_w)


def _expand_matrix():
    e = np.zeros((LANES, SSM_GROUP_WIDTH), np.float32)
    for h in range(HEADS_PER_SSM_GROUP):
        e[h, h * SSM_HEAD_DIM:(h + 1) * SSM_HEAD_DIM] = 1.0
    return jnp.asarray(e, BF16)


def _expand(t, e):
    hi, lo = _split2(t)
    return _dot(hi, e) + _dot(lo, e)


def _segsum(v, e):
    hi, lo = _split2(v)
    return _dot_nt(hi, e) + _dot_nt(lo, e)


def _tri_dot(tri, x):
    hi, mid, lo = _split3(x)
    return _dot(tri, hi) + _dot(tri, mid) + _dot(tri, lo)


def _ssd_common(dtp_ref, a_ref, dtb_ref, x_ref, e):
    li = lax.broadcasted_iota(jnp.int32, (CHUNK, CHUNK), 0)
    si = lax.broadcasted_iota(jnp.int32, (CHUNK, CHUNK), 1)
    causal = li >= si
    tril = causal.astype(BF16)
    raw = dtp_ref[...] + dtb_ref[...]
    dt = jnp.maximum(raw, 0.0) + jnp.log(1.0 + jnp.exp(-jnp.abs(raw)))
    head_lane = lax.broadcasted_iota(jnp.int32, (1, LANES), 1) < HEADS_PER_SSM_GROUP
    a = jnp.where(head_lane, -jnp.exp(a_ref[...]), 0.0)
    a_cum = _tri_dot(tril, dt * a)
    a_cum_t = a_cum.T
    e_a = jnp.exp(a_cum)
    to_end = jnp.exp(a_cum[CHUNK - 1:CHUNK, :] - a_cum)
    x = x_ref[...].astype(F32)
    dt_e = _expand(dt, e)
    return dict(causal=causal, raw=raw, dt=dt, a=a, a_cum=a_cum, a_cum_t=a_cum_t, e_a=e_a,
                to_end=to_end, x=x, dt_e=dt_e, xdt=x * dt_e, e_a_e=_expand(e_a, e),
                to_end_e=_expand(to_end, e))


def _decay(q, h):
    seg = q["a_cum"][:, h:h + 1] - q["a_cum_t"][h:h + 1, :]
    return jnp.exp(jnp.where(q["causal"], seg, -jnp.inf))


def _ssd_specs(ng, d_inner, rev, nc):
    cidx = (lambda i: nc - 1 - i) if rev else (lambda i: i)
    boff = d_inner // D_STATE
    return dict(
        xs=pl.BlockSpec((CHUNK, SSM_GROUP_WIDTH), lambda g, i: (cidx(i), g)),
        b=pl.BlockSpec((CHUNK, D_STATE), lambda g, i: (cidx(i), boff + g)),
        c=pl.BlockSpec((CHUNK, D_STATE), lambda g, i: (cidx(i), boff + ng + g)),
        dtp=pl.BlockSpec((None, CHUNK, LANES), lambda g, i: (g, cidx(i), 0)),
        vec=pl.BlockSpec((None, 1, LANES), lambda g, i: (g, 0, 0)),
        wide=pl.BlockSpec((None, 1, SSM_GROUP_WIDTH), lambda g, i: (g, 0, 0)),
        e=pl.BlockSpec((LANES, SSM_GROUP_WIDTH), lambda g, i: (0, 0)),
        st=pl.BlockSpec((None, None, D_STATE, SSM_GROUP_WIDTH), lambda g, i: (g, cidx(i), 0, 0)),
        tok=pl.BlockSpec((CHUNK, SSM_GROUP_WIDTH), lambda g, i: (cidx(i), g)),
        bc_out=pl.BlockSpec((CHUNK, D_STATE), lambda g, i: (cidx(i), g)),
    )


def _ssd_fwd(xbc, dtp, a_pad, dtb_pad, dsk_e, e, d_inner):
    s = xbc.shape[0]
    ng = d_inner // SSM_GROUP_WIDTH
    nc = s // CHUNK

    def body(x_ref, b_ref, c_ref, dtp_ref, a_ref, dtb_ref, dsk_ref, e_ref, y_ref, st_ref, state):
        lane = lax.broadcasted_iota(jnp.int32, (CHUNK, LANES), 1)
        @pl.when(pl.program_id(1) == 0)
        def _():
            state[...] = jnp.zeros_like(state)

        ev = e_ref[...]
        q = _ssd_common(dtp_ref, a_ref, dtb_ref, x_ref, ev)
        bm, cm = b_ref[...], c_ref[...]
        cb = _dot_nt(cm, bm)
        s0 = state[...]
        st_ref[...] = s0
        y = _dot(cm, s0.astype(BF16)) * q["e_a_e"] + dsk_ref[...] * q["x"]
        xdt = q["xdt"]
        left = lane[:, :] < SSM_HEAD_DIM
        for j in range(HEADS_PER_SSM_GROUP // 2):
            sl = slice(j * LANES, (j + 1) * LANES)
            x2 = xdt[:, sl]
            m0 = (cb * _decay(q, 2 * j)).astype(BF16)
            m1 = (cb * _decay(q, 2 * j + 1)).astype(BF16)
            mcat = jnp.concatenate([m0, m1], axis=1)
            xbd = jnp.concatenate([jnp.where(left, x2, 0.0), jnp.where(left, 0.0, x2)], axis=0).astype(BF16)
            y_ref[:, sl] = (y[:, sl] + _dot(mcat, xbd)).astype(BF16)
        state[...] = s0 * q["e_a_e"][CHUNK - 1:CHUNK, :] + _dot_tn(bm, (q["to_end_e"] * xdt).astype(BF16))

    sp = _ssd_specs(ng, d_inner, False, nc)
    return pl.pallas_call(
        body, name="ssd_fwd", grid=(ng, nc),
        in_specs=[sp["xs"], sp["b"], sp["c"], sp["dtp"], sp["vec"], sp["vec"], sp["wide"], sp["e"]],
        out_specs=[sp["tok"], sp["st"]],
        out_shape=[jax.ShapeDtypeStruct((s, d_inner), BF16),
                   jax.ShapeDtypeStruct((ng, nc, D_STATE, SSM_GROUP_WIDTH), F32)],
        scratch_shapes=[pltpu.VMEM((D_STATE, SSM_GROUP_WIDTH), F32)],
        compiler_params=_cparams(("parallel", "arbitrary")),
    )(xbc, xbc, xbc, dtp, a_pad, dtb_pad, dsk_e, e)


def _ssd_bwd(xbc, dtp, a_pad, dtb_pad, dsk_e, e, states, dy, d_inner):
    s = xbc.shape[0]
    ng = d_inner // SSM_GROUP_WIDTH
    nc = s // CHUNK

    def body(x_ref, b_ref, c_ref, dtp_ref, a_ref, dtb_ref, dsk_ref, e_ref, st_ref, dy_ref,
             dx_ref, db_ref, dc_ref, ddt_ref, da_ref, ddtb_ref, dd_ref, dstate):
        lane = lax.broadcasted_iota(jnp.int32, (CHUNK, LANES), 1)
        sub = lax.broadcasted_iota(jnp.int32, (CHUNK, LANES), 0)
        @pl.when(pl.program_id(1) == 0)
        def _():
            dstate[...] = jnp.zeros_like(dstate)
            da_ref[...] = jnp.zeros_like(da_ref)
            ddtb_ref[...] = jnp.zeros_like(ddtb_ref)
            dd_ref[...] = jnp.zeros_like(dd_ref)

        ev = e_ref[...]
        q = _ssd_common(dtp_ref, a_ref, dtb_ref, x_ref, ev)
        bm, cm = b_ref[...], c_ref[...]
        cb = _dot_nt(cm, bm)
        x, xdt, e_a_e, to_end_e = q["x"], q["xdt"], q["e_a_e"], q["to_end_e"]
        s0 = st_ref[...]
        s0b = s0.astype(BF16)
        ds1 = dstate[...]
        ds1b = ds1.astype(BF16)
        dy = dy_ref[...].astype(F32)
        e_last_e = e_a_e[CHUNK - 1:CHUNK, :]

        dye = dy * e_a_e
        dyeb = dye.astype(BF16)
        cs0 = _dot(cm, s0b)
        dc = _dot_nt(dyeb, s0b)
        dstate[...] = e_last_e * ds1 + _dot_tn(cm, dyeb)
        da_col = _segsum(dye * cs0, ev)

        gmat = _dot(bm, ds1b)
        dxdt = to_end_e * gmat
        dte = _segsum(xdt * gmat, ev) * q["to_end"]
        db = _dot_nt((to_end_e * xdt).astype(BF16), ds1b)
        da_col = da_col - dte
        last_row = (jnp.sum(dte, axis=0, keepdims=True)
                    + q["e_a"][CHUNK - 1:CHUNK, :] * jnp.sum(_segsum(s0 * ds1, ev), axis=0, keepdims=True))

        left = lane < SSM_HEAD_DIM
        dcb = jnp.zeros((CHUNK, CHUNK), F32)
        row_acc = jnp.zeros((CHUNK, LANES), F32)
        for j in range(HEADS_PER_SSM_GROUP // 2):
            sl = slice(j * LANES, (j + 1) * LANES)
            x2 = xdt[:, sl].astype(BF16)
            dy2 = dy[:, sl]
            dyl = jnp.where(left, dy2, 0.0).astype(BF16)
            dyr = jnp.where(left, 0.0, dy2).astype(BF16)
            ms = []
            for hh, dyh in ((0, dyl), (1, dyr)):
                h = 2 * j + hh
                dec = _decay(q, h)
                m = cb * dec
                dm = _dot_nt(dyh, x2)
                dcb = dcb + dm * dec
                dseg = dm * m
                da_col = da_col + jnp.where(lane == h, jnp.sum(dseg, axis=1, keepdims=True), 0.0)
                row_acc = row_acc + jnp.where(sub == h, jnp.sum(dseg, axis=0, keepdims=True), 0.0)
                ms.append(m.astype(BF16))
            mst = jnp.concatenate(ms, axis=0)
            dyst = jnp.concatenate([dyl, dyr], axis=0)
            d2 = dxdt[:, sl] + _dot_tn(mst, dyst)
            dx_ref[:, sl] = (d2 * q["dt_e"][:, sl] + dsk_ref[:, sl] * dy2).astype(BF16)
            dxdt_x = d2 * x[:, sl]
            if j == 0:
                parts = [dxdt_x]
            else:
                parts.append(dxdt_x)
        dcbb = dcb.astype(BF16)
        dc_ref[...] = (dc + _dot(dcbb, bm)).astype(BF16)
        db_ref[...] = (db + _dot_tn(dcbb, cm)).astype(BF16)

        d_a = da_col - row_acc.T + jnp.where(sub == CHUNK - 1, last_row, 0.0)
        triu = (lax.broadcasted_iota(jnp.int32, (CHUNK, CHUNK), 1)
                >= lax.broadcasted_iota(jnp.int32, (CHUNK, CHUNK), 0)).astype(BF16)
        d_dta = _tri_dot(triu, d_a)
        ddt = d_dta * q["a"] + _segsum(jnp.concatenate(parts, axis=1), ev)
        ddt_raw = ddt * _sigmoid(q["raw"])
        ddt_ref[...] = ddt_raw
        da_ref[...] += jnp.sum(d_dta * q["dt"], axis=0, keepdims=True) * q["a"]
        ddtb_ref[...] += jnp.sum(ddt_raw, axis=0, keepdims=True)
        dd_ref[...] += jnp.sum(dy * x, axis=0, keepdims=True)

    sp = _ssd_specs(ng, d_inner, True, nc)
    return pl.pallas_call(
        body, name="ssd_bwd", grid=(ng, nc),
        in_specs=[sp["xs"], sp["b"], sp["c"], sp["dtp"], sp["vec"], sp["vec"], sp["wide"], sp["e"],
                  sp["st"], sp["tok"]],
        out_specs=[sp["tok"], sp["bc_out"], sp["bc_out"], sp["dtp"], sp["vec"], sp["vec"], sp["wide"]],
        out_shape=[jax.ShapeDtypeStruct((s, d_inner), BF16),
                   jax.ShapeDtypeStruct((s, ng * D_STATE), BF16),
                   jax.ShapeDtypeStruct((s, ng * D_STATE), BF16),
                   jax.ShapeDtypeStruct((ng, s, LANES), F32),
                   jax.ShapeDtypeStruct((ng, 1, LANES), F32),
                   jax.ShapeDtypeStruct((ng, 1, LANES), F32),
                   jax.ShapeDtypeStruct((ng, 1, SSM_GROUP_WIDTH), F32)],
        scratch_shapes=[pltpu.VMEM((D_STATE, SSM_GROUP_WIDTH), F32)],
        compiler_params=_cparams(("parallel", "arbitrary")),
    )(xbc, xbc, xbc, dtp, a_pad, dtb_pad, dsk_e, e, states, dy)


def _gate_norm_fwd(y, pzx, norm_w):
    s, di = y.shape
    ng = di // SSM_GROUP_WIDTH
    tm = _pick(s, (512, 256, 128))

    def body(y_ref, z_ref, w_ref, o_ref):
        z = z_ref[...].astype(F32)
        y2 = y_ref[...].astype(F32) * (z * _sigmoid(z))
        r = lax.rsqrt(jnp.mean(y2 * y2, axis=1, keepdims=True) + RMS_EPS)
        o_ref[...] = (y2 * r * w_ref[...]).astype(BF16)

    blk = pl.BlockSpec((tm, SSM_GROUP_WIDTH), lambda i, g: (i, g))
    return pl.pallas_call(
        body, name="gate_norm_fwd", grid=(s // tm, ng),
        in_specs=[blk, blk, pl.BlockSpec((1, SSM_GROUP_WIDTH), lambda i, g: (0, g))],
        out_specs=blk, out_shape=jax.ShapeDtypeStruct((s, di), BF16),
        compiler_params=_cparams(("parallel", "parallel")),
    )(y, pzx, norm_w)


def _gate_norm_bwd(dy3, y, pzx, norm_w):
    s, di = y.shape
    ng = di // SSM_GROUP_WIDTH
    tm = _pick(s, (512, 256, 128))

    def body(d_ref, y_ref, z_ref, w_ref, dy_ref, dz_ref, dw_ref):
        @pl.when(pl.program_id(1) == 0)
        def _():
            dw_ref[...] = jnp.zeros_like(dw_ref)

        z = z_ref[...].astype(F32)
        yv = y_ref[...].astype(F32)
        sg = _sigmoid(z)
        sz = z * sg
        y2 = yv * sz
        r = lax.rsqrt(jnp.mean(y2 * y2, axis=1, keepdims=True) + RMS_EPS)
        nrm = y2 * r
        d3 = d_ref[...].astype(F32)
        dw_ref[...] += jnp.sum(d3 * nrm, axis=0, keepdims=True)
        dn = d3 * w_ref[...]
        dy2 = r * (dn - nrm * jnp.mean(dn * nrm, axis=1, keepdims=True))
        dy_ref[...] = (dy2 * sz).astype(BF16)
        dz_ref[...] = (dy2 * yv * (sg * (1.0 + z * (1.0 - sg)))).astype(BF16)

    blk = pl.BlockSpec((tm, SSM_GROUP_WIDTH), lambda g, i: (i, g))
    vec = pl.BlockSpec((1, SSM_GROUP_WIDTH), lambda g, i: (0, g))
    return pl.pallas_call(
        body, name="gate_norm_bwd", grid=(ng, s // tm),
        in_specs=[blk, blk, blk, vec], out_specs=[blk, blk, vec],
        out_shape=[jax.ShapeDtypeStruct((s, di), BF16), jax.ShapeDtypeStruct((s, di), BF16),
                   jax.ShapeDtypeStruct((1, di), F32)],
        compiler_params=_cparams(("parallel", "arbitrary")),
    )(dy3, y, pzx, norm_w)


_ANY = pl.BlockSpec(memory_space=pl.ANY)


def _place():
    x, y, c = lax.axis_index("x"), lax.axis_index("y"), lax.axis_index("c")
    chips = [(1 - x, y), (x, 1 - y), (1 - x, 1 - y)]
    return x, y, c, chips


def _cast_to_slot(x, kvec, name):
    r, cn = x.shape
    tr = _rows_per_block(r, cn)

    def body(k_ref, x_ref, o_ref):
        o_ref[...] = x_ref[...].astype(BF16)

    grid_spec = pltpu.PrefetchScalarGridSpec(
        num_scalar_prefetch=1, grid=(r // tr,),
        in_specs=[pl.BlockSpec((tr, cn), lambda i, k: (i, 0))],
        out_specs=pl.BlockSpec((None, tr, cn), lambda i, k: (k[0], i, 0)))
    return pl.pallas_call(
        body, name=name, grid_spec=grid_spec, out_shape=jax.ShapeDtypeStruct((N_CHIPS, r, cn), BF16),
        compiler_params=_cparams(("parallel",)),
    )(kvec, x)


def _gather_weights(bufs):
    n = len(bufs)

    def body(*refs):
        outs = refs[n:2 * n]
        send_sems, recv_sems = refs[2 * n:]
        x, y, c, chips = _place()
        k = 2 * x + y
        sib = (x, y, 1 - c)

        def half(w, slot, hc):
            hr = bufs[w].shape[1] // 2
            return outs[w].at[slot, pl.ds(hc * hr, hr)]

        def copy(w, j, src, dst, to):
            return pltpu.make_async_remote_copy(
                src_ref=src, dst_ref=dst, send_sem=send_sems.at[w, j], recv_sem=recv_sems.at[w, j],
                device_id=to, device_id_type=MESH)

        first, passed = [], []
        for w in range(n):
            for j, chip in enumerate(chips):
                cp = copy(w, j, half(w, k, c), half(w, k, c), (*chip, c))
                cp.start()
                first.append(cp)
        for w in range(n):
            for j, (cx, cy) in enumerate(chips):
                kj = 2 * cx + cy
                copy(w, j, half(w, kj, c), half(w, kj, c), sib).wait_recv()
                cp = copy(w, 3 + j, half(w, kj, c), half(w, kj, c), sib)
                cp.start()
                passed.append(cp)
        for w in range(n):
            for j, (cx, cy) in enumerate(chips):
                kj = 2 * cx + cy
                copy(w, 3 + j, half(w, kj, 1 - c), half(w, kj, 1 - c), sib).wait_recv()
        for cp in first + passed:
            cp.wait_send()

    return pl.pallas_call(
        body, name="gather_weights",
        in_specs=[_ANY] * n, out_specs=[_ANY] * n,
        out_shape=[jax.ShapeDtypeStruct(b.shape, b.dtype) for b in bufs],
        input_output_aliases={w: w for w in range(n)},
        scratch_shapes=[pltpu.SemaphoreType.DMA((n, 6)), pltpu.SemaphoreType.DMA((n, 6))],
    )(*bufs)


def _swap_halves(gs, name):
    n = len(gs)

    def body(*refs):
        ins, outs = refs[:n], refs[n:2 * n]
        send_sems, recv_sems = refs[2 * n:]
        x, y, c, _ = _place()
        cps = []
        for w in range(n):
            hr = gs[w].shape[1] // 2
            cp = pltpu.make_async_remote_copy(
                src_ref=ins[w].at[:, pl.ds((1 - c) * hr, hr)], dst_ref=outs[w],
                send_sem=send_sems.at[w], recv_sem=recv_sems.at[w],
                device_id=(x, y, 1 - c), device_id_type=MESH)
            cp.start()
            cps.append(cp)
        for cp in cps:
            cp.wait()

    return pl.pallas_call(
        body, name=name,
        in_specs=[_ANY] * n, out_specs=[_ANY] * n,
        out_shape=[jax.ShapeDtypeStruct((g.shape[0], g.shape[1] // 2, g.shape[2]), g.dtype) for g in gs],
        scratch_shapes=[pltpu.SemaphoreType.DMA((n,)), pltpu.SemaphoreType.DMA((n,))],
    )(*gs)


def _scatter_to_chips(ts):
    n = len(ts)

    def body(*refs):
        ins, outs = refs[:n], refs[n:2 * n]
        send_sems, recv_sems = refs[2 * n:]
        x, y, c, chips = _place()
        k = 2 * x + y
        cps = []
        for w in range(n):
            for j, (cx, cy) in enumerate(chips):
                cp = pltpu.make_async_remote_copy(
                    src_ref=ins[w].at[2 * cx + cy], dst_ref=outs[w].at[k],
                    send_sem=send_sems.at[w, j], recv_sem=recv_sems.at[w, j],
                    device_id=(cx, cy, c), device_id_type=MESH)
                cp.start()
                cps.append(cp)
        for w in range(n):
            for j, (cx, cy) in enumerate(chips):
                kj = 2 * cx + cy
                pltpu.make_async_remote_copy(
                    src_ref=ins[w].at[kj], dst_ref=outs[w].at[kj],
                    send_sem=send_sems.at[w, j], recv_sem=recv_sems.at[w, j],
                    device_id=(cx, cy, c), device_id_type=MESH).wait_recv()
        for cp in cps:
            cp.wait_send()

    return pl.pallas_call(
        body, name="grads_to_chips",
        in_specs=[_ANY] * n, out_specs=[_ANY] * n,
        out_shape=[jax.ShapeDtypeStruct(t.shape, t.dtype) for t in ts],
        scratch_shapes=[pltpu.SemaphoreType.DMA((n, 3)), pltpu.SemaphoreType.DMA((n, 3))],
    )(*ts)


def _join_halves(fs, name, after=None):
    n = len(fs)
    extra = [] if after is None else [after]

    def body(*refs):
        outs = refs[n + len(extra):2 * n + len(extra)]
        send_sems, recv_sems = refs[2 * n + len(extra):]
        x, y, c, _ = _place()

        def copy(w, hc):
            hr = fs[w].shape[0] // 2
            rows = outs[w].at[pl.ds(hc * hr, hr)]
            return pltpu.make_async_remote_copy(
                src_ref=rows, dst_ref=rows, send_sem=send_sems.at[w], recv_sem=recv_sems.at[w],
                device_id=(x, y, 1 - c), device_id_type=MESH)

        cps = [copy(w, c) for w in range(n)]
        for cp in cps:
            cp.start()
        for w in range(n):
            copy(w, 1 - c).wait_recv()
        for cp in cps:
            cp.wait_send()

    return pl.pallas_call(
        body, name=name,
        in_specs=[_ANY] * (n + len(extra)), out_specs=[_ANY] * n,
        out_shape=[jax.ShapeDtypeStruct(f.shape, f.dtype) for f in fs],
        input_output_aliases={w: w for w in range(n)},
        scratch_shapes=[pltpu.SemaphoreType.DMA((n,)), pltpu.SemaphoreType.DMA((n,))],
    )(*fs, *extra)


_HBM_SPEC = pl.BlockSpec(memory_space=pltpu.HBM)
_SEM_SPEC = pl.BlockSpec(memory_space=pltpu.SEMAPHORE)
_VMEM_SPEC = pl.BlockSpec(memory_space=pltpu.VMEM)
_EFFECT = pltpu.SideEffectType.DATAFLOW_SIDE_EFFECTING
_TOKEN = jax.ShapeDtypeStruct((8, LANES), F32)


def _hbm(a):
    return pltpu.with_memory_space_constraint(a, pltpu.HBM)


def _gather_copies(bufs, refs, send_sems, recv_sems, forward):
    x, y, c, chips = _place()
    k = 2 * x + y
    out, arrive = [], []
    for w, ref in enumerate(refs):
        hr = bufs[w].shape[1] // 2
        for j, (cx, cy) in enumerate(chips):
            kj = 2 * cx + cy
            slot_out, slot_in, half_in = (kj, kj, 1 - c) if forward else (k, kj, c)
            to = (x, y, 1 - c) if forward else (cx, cy, c)
            src = ref.at[slot_out, pl.ds(c * hr, hr)]
            land = ref.at[slot_in, pl.ds(half_in * hr, hr)]
            out.append(pltpu.make_async_remote_copy(
                src_ref=src, dst_ref=src, send_sem=send_sems.at[3 * w + j], recv_sem=recv_sems.at[3 * w + j],
                device_id=to, device_id_type=MESH))
            arrive.append(pltpu.make_async_remote_copy(
                src_ref=land, dst_ref=land, send_sem=send_sems.at[3 * w + j], recv_sem=recv_sems.at[3 * w + j],
                device_id=to, device_id_type=MESH))
    return out, arrive


def _gather_start(bufs, forward, name, after=None):
    n = len(bufs)
    extra = [] if after is None else [after]

    def body(*refs):
        ins = refs[:n]
        send_sems, recv_sems = refs[n + len(extra)], refs[n + len(extra) + 1]
        token = refs[-1]
        out, _ = _gather_copies(bufs, ins, send_sems, recv_sems, forward)
        for cp in out:
            cp.start()
        token[...] = jnp.zeros_like(token)

    res = pl.pallas_call(
        body, name=name,
        out_shape=(pltpu.SemaphoreType.DMA((3 * n,)), pltpu.SemaphoreType.DMA((3 * n,)))
        + tuple(pltpu.HBM(b.shape, b.dtype) for b in bufs) + (_TOKEN,),
        in_specs=(_HBM_SPEC,) * n + (_ANY,) * len(extra),
        out_specs=(_SEM_SPEC, _SEM_SPEC) + (_HBM_SPEC,) * n + (_VMEM_SPEC,),
        input_output_aliases={w: 2 + w for w in range(n)},
        compiler_params=pltpu.CompilerParams(has_side_effects=_EFFECT),
    )(*[_hbm(b) for b in bufs], *extra)
    return res[0], res[1], list(res[2:2 + n]), res[-1]


def _gather_wait(bufs, send_sems, recv_sems, after, forward, name):
    n = len(bufs)

    def body(*refs):
        ins = refs[:n]
        send_sems, recv_sems = refs[n], refs[n + 1]
        out, arrive = _gather_copies(bufs, ins, send_sems, recv_sems, forward)
        for cp in out:
            cp.wait_send()
        for cp in arrive:
            cp.wait_recv()

    res = pl.pallas_call(
        body, name=name,
        out_shape=tuple(pltpu.HBM(b.shape, b.dtype) for b in bufs),
        in_specs=(_HBM_SPEC,) * n + (_SEM_SPEC, _SEM_SPEC, _ANY), out_specs=(_HBM_SPEC,) * n,
        input_output_aliases={w: w for w in range(n)},
        compiler_params=pltpu.CompilerParams(has_side_effects=_EFFECT),
    )(*bufs, send_sems, recv_sems, after)
    return list(res)


def _scatter_copies(t_ref, land_ref, send_sems, recv_sems):
    x, y, c, chips = _place()
    k = 2 * x + y
    out, arrive = [], []
    for j, (cx, cy) in enumerate(chips):
        kj = 2 * cx + cy
        out.append(pltpu.make_async_remote_copy(
            src_ref=t_ref.at[kj], dst_ref=land_ref.at[k], send_sem=send_sems.at[j], recv_sem=recv_sems.at[j],
            device_id=(cx, cy, c), device_id_type=MESH))
        arrive.append(pltpu.make_async_remote_copy(
            src_ref=t_ref.at[kj], dst_ref=land_ref.at[kj], send_sem=send_sems.at[j], recv_sem=recv_sems.at[j],
            device_id=(cx, cy, c), device_id_type=MESH))
    return out, arrive


def _scatter_start(t, name):
    def body(t_ref, land_ref, send_sems, recv_sems, t_thru, land_thru, token):
        out, _ = _scatter_copies(t_ref, land_ref, send_sems, recv_sems)
        for cp in out:
            cp.start()
        token[...] = jnp.zeros_like(token)

    return pl.pallas_call(
        body, name=name,
        out_shape=(pltpu.SemaphoreType.DMA((3,)), pltpu.SemaphoreType.DMA((3,)),
                   pltpu.HBM(t.shape, t.dtype), pltpu.HBM(t.shape, t.dtype), _TOKEN),
        in_specs=(_HBM_SPEC, _HBM_SPEC), out_specs=(_SEM_SPEC, _SEM_SPEC, _HBM_SPEC, _HBM_SPEC, _VMEM_SPEC),
        input_output_aliases={0: 2, 1: 3},
        compiler_params=pltpu.CompilerParams(has_side_effects=_EFFECT),
    )(_hbm(t), _hbm(lax.empty(t.shape, t.dtype)))


def _scatter_wait(send_sems, recv_sems, t_thru, land_thru, after, name):
    def body(t_ref, land_ref, send_sems, recv_sems, after_ref, t_out, land_out):
        out, arrive = _scatter_copies(t_ref, land_ref, send_sems, recv_sems)
        for cp in out:
            cp.wait_send()
        for cp in arrive:
            cp.wait_recv()

    return pl.pallas_call(
        body, name=name,
        out_shape=(pltpu.HBM(t_thru.shape, t_thru.dtype), pltpu.HBM(land_thru.shape, land_thru.dtype)),
        in_specs=(_HBM_SPEC, _HBM_SPEC, _SEM_SPEC, _SEM_SPEC, _ANY), out_specs=(_HBM_SPEC, _HBM_SPEC),
        input_output_aliases={0: 0, 1: 1},
        compiler_params=pltpu.CompilerParams(has_side_effects=_EFFECT),
    )(t_thru, land_thru, send_sems, recv_sems, after)


def _all_gather_small(v, reduce, name):
    r, l = v.shape

    def body(v_ref, o_ref, *rest):
        if reduce:
            buf, send_sems, recv_sems = rest
        else:
            buf = o_ref
            send_sems, recv_sems = rest
        x, y, c, _ = _place()
        me = 4 * x + 2 * y + c
        buf[me] = v_ref[...]
        cps = []
        for d in range(1, N_DEV):
            peer = (x if d & 4 == 0 else 1 - x, y if d & 2 == 0 else 1 - y, c if d & 1 == 0 else 1 - c)
            cp = pltpu.make_async_remote_copy(
                src_ref=v_ref, dst_ref=buf.at[me], send_sem=send_sems.at[d - 1], recv_sem=recv_sems.at[d - 1],
                device_id=peer, device_id_type=MESH)
            cp.start()
            cps.append((cp, peer))
        for d, (cp, (px, py, pc)) in enumerate(cps, start=1):
            pltpu.make_async_remote_copy(
                src_ref=v_ref, dst_ref=buf.at[4 * px + 2 * py + pc], send_sem=send_sems.at[d - 1],
                recv_sem=recv_sems.at[d - 1], device_id=(px, py, pc), device_id_type=MESH).wait_recv()
        for cp, _ in cps:
            cp.wait_send()
        if reduce:
            acc = buf[0]
            for i in range(1, N_DEV):
                acc = acc + buf[i]
            o_ref[...] = acc

    vm = pl.BlockSpec(memory_space=pltpu.VMEM)
    out_shape = jax.ShapeDtypeStruct((r, l) if reduce else (N_DEV, r, l), F32)
    scratch = ([pltpu.VMEM((N_DEV, r, l), F32)] if reduce else []) + [
        pltpu.SemaphoreType.DMA((N_DEV - 1,)), pltpu.SemaphoreType.DMA((N_DEV - 1,))]
    return pl.pallas_call(
        body, name=name, in_specs=[vm], out_specs=vm, out_shape=out_shape, scratch_shapes=scratch,
    )(v)


_BLOCK_BYTES = 3 * 512 * 1024


def _rows_per_block(r, cn, itemsize=4):
    best = 8
    for t in range(8, r + 1, 8):
        if r % t == 0 and t * cn * itemsize <= _BLOCK_BYTES:
            best = t
    return best


def _add_sibling_half(g4, recv, cvec, name):
    ns, r, cn = g4.shape
    hr = r // 2
    tr = _rows_per_block(hr, cn)
    nrb = hr // tr

    def body(c_ref, a_ref, b_ref, o_ref):
        o_ref[...] = (a_ref[...].astype(F32) + b_ref[...].astype(F32)).astype(o_ref.dtype)

    grid_spec = pltpu.PrefetchScalarGridSpec(
        num_scalar_prefetch=1, grid=(ns, nrb),
        in_specs=[pl.BlockSpec((None, tr, cn), lambda j, i, c: (j, c[0] * nrb + i, 0)),
                  pl.BlockSpec((None, tr, cn), lambda j, i, c: (j, i, 0))],
        out_specs=pl.BlockSpec((None, tr, cn), lambda j, i, c: (j, i, 0)))
    return pl.pallas_call(
        body, name=name, grid_spec=grid_spec, out_shape=jax.ShapeDtypeStruct((ns, hr, cn), BF16),
        compiler_params=_cparams(("parallel", "parallel")),
    )(cvec, g4, recv)


def _sum_chips(r4, t4, kvec, cvec, name):
    ns, hr, cn = r4.shape
    tr = _rows_per_block(hr, cn)
    nrb = hr // tr

    def body(k_ref, c_ref, r_ref, t_ref, o_ref):
        acc = t_ref[...].astype(F32)
        for dlt in range(1, ns):
            acc = acc + r_ref[(k_ref[0] + dlt) % ns].astype(F32)
        o_ref[...] = acc

    grid_spec = pltpu.PrefetchScalarGridSpec(
        num_scalar_prefetch=2, grid=(nrb,),
        in_specs=[pl.BlockSpec((ns, tr, cn), lambda i, k, c: (0, i, 0)),
                  pl.BlockSpec((None, tr, cn), lambda i, k, c: (k[0], i, 0))],
        out_specs=pl.BlockSpec((tr, cn), lambda i, k, c: (c[0] * nrb + i, 0)))
    return pl.pallas_call(
        body, name=name, grid_spec=grid_spec, out_shape=jax.ShapeDtypeStruct((2 * hr, cn), F32),
        compiler_params=_cparams(("parallel",)),
    )(kvec, cvec, r4, t4)


def _adamw(w, g, m, v, name):
    r, cn = w.shape
    tr = _rows_per_block(r, cn)
    c1 = 1.0 - ADAM_B1 ** ADAM_STEP
    c2 = 1.0 - ADAM_B2 ** ADAM_STEP

    def body(w_ref, g_ref, m_ref, v_ref, go_ref, d_ref, mo_ref, vo_ref):
        gv = g_ref[...]
        mn = ADAM_B1 * m_ref[...] + (1.0 - ADAM_B1) * gv
        vn = ADAM_B2 * v_ref[...] + (1.0 - ADAM_B2) * (gv * gv)
        go_ref[...] = gv
        mo_ref[...] = mn
        vo_ref[...] = vn
        d_ref[...] = -ADAM_LR * ((mn / c1) / (jnp.sqrt(vn / c2) + ADAM_EPS) + ADAM_WD * w_ref[...])

    spec = pl.BlockSpec((tr, cn), lambda i: (i, 0))
    return pl.pallas_call(
        body, name=name, grid=(r // tr,), in_specs=[spec] * 4, out_specs=[spec] * 4,
        out_shape=[jax.ShapeDtypeStruct((r, cn), F32)] * 4,
        compiler_params=_cparams(("parallel",)),
    )(w, g, m, v)


def _pack(arrs):
    flat = jnp.concatenate([a.reshape(-1).astype(F32) for a in arrs])
    n = flat.shape[0]
    tot = -(-n // (8 * LANES)) * (8 * LANES)
    return jnp.pad(flat, (0, tot - n)).reshape(tot // LANES, LANES)


def _unpack(packed, shapes):
    flat = packed.reshape(-1)
    out, off = [], 0
    for shp in shapes:
        sz = int(np.prod(shp))
        out.append(flat[off:off + sz].reshape(shp))
        off += sz
    return out


class _LocalExchange:
    def __init__(self, ws4, wos4):
        self.ssm = [ws4, wos4]
        self.grads = {}

    def ssm_gather_start(self):
        return None

    def ssm_gather_mid(self, after):
        return None

    def ssm_gather_end(self, after):
        return self.ssm

    def grad_ready(self, name, g4):
        self.grads[name] = g4
        return None

    def grad_sync(self, name, after):
        pass


class _Exchange:
    def __init__(self, kvec, cvec, ssm_bufs, after):
        self.kvec, self.cvec, self.bufs, self.after = kvec, cvec, ssm_bufs, after
        self.pending, self.summed, self.last_token = {}, {}, None

    def ssm_gather_start(self):
        self.sems = _gather_start(self.bufs, False, "ssm_gather_ici_start", self.after)
        self.bufs = self.sems[2]
        return self.sems[3]

    def ssm_gather_mid(self, after):
        bufs = _gather_wait(self.bufs, self.sems[0], self.sems[1], after, False, "ssm_gather_ici_wait")
        self.sems = _gather_start(bufs, True, "ssm_gather_fwd_start")
        self.bufs = self.sems[2]
        return self.sems[3]

    def ssm_gather_end(self, after):
        return _gather_wait(self.bufs, self.sems[0], self.sems[1], after, True, "ssm_gather_fwd_wait")

    def grad_ready(self, name, g4):
        recv = _swap_halves([g4], "grads_to_sibling_" + name)[0]
        t = _add_sibling_half(g4, recv, self.cvec, "add_sibling_" + name)
        send_sems, recv_sems, t_thru, land, token = _scatter_start(t, "scatter_start_" + name)
        self.pending[name] = (send_sems, recv_sems, t_thru, land)
        self.last_token = token
        return token

    def grad_sync(self, name, after):
        t, land = _scatter_wait(*self.pending.pop(name), after, "scatter_wait_" + name)
        self.summed[name] = _sum_chips(land, t, self.kvec, self.cvec, "sum_chips_" + name)


def _tie(vec, token):
    return vec if token is None else vec + token[0:1, 0:1].reshape((1,) * vec.ndim).astype(vec.dtype)


def _local_step(x2, tgt, wa4, woa4, ex, conv_w_f, conv_b_f, norm_w_f, rel_bias, dt_bias, a_log, d_skip,
                ln_g, ln_b):
    s, d = x2.shape
    d_attn = woa4.shape[1]
    hpg = d_attn // HEAD_DIM
    d_inner = norm_w_f.shape[1]
    ng = d_inner // SSM_GROUP_WIDTH
    n_heads = dt_bias.shape[1]
    conv_dim = conv_w_f.shape[1]
    assert n_heads == ng * HEADS_PER_SSM_GROUP and conv_dim == d_inner + 2 * ng * D_STATE
    assert wa4.shape[2] * N_CHIPS == 10 * d_attn

    xb = _cast_bf16(x2, "cast_x")
    pa = _mm_nn_sharded(xb, wa4, BF16, "mm_in_attn", after=ex.ssm_gather_start())
    buckets = _bucket_tiles()
    bias = _bias_expand(rel_bias, buckets, hpg)
    og, lg = [], []
    for g, (_, dil) in enumerate(ATTN_PATTERNS):
        o_, l_ = _attn_fwd(pa, bias, g, dil, hpg)
        og.append(o_)
        lg.append(l_)
    o, lse, yat = _attn_combine(og, lg, pa, hpg)
    h0 = _mm_nn_sharded(yat, woa4, F32, "mm_out_attn", after=ex.ssm_gather_mid(yat))
    g0, b0, g1, b1 = ln_g[0:1], ln_b[0:1], ln_g[1:2], ln_b[1:2]
    xhat0, rstd0, x1b = _ln_fwd(x2, h0, g0, b0, "ln0_fwd")

    ws4, wos4 = ex.ssm_gather_end(x1b)
    ssm_cols = ws4.shape[2]
    ws = jnp.concatenate([ws4[j] for j in range(N_CHIPS)], axis=1)
    wzx, wdt = ws[:, :d_inner + conv_dim], ws[:, d_inner + conv_dim:]
    wos = wos4.reshape(d_inner, d)
    pzx = _mm_nn(x1b, wzx, BF16, "mm_in_ssm")
    dt_raw = _mm_nn(x1b, wdt, F32, "mm_in_dt")

    def pad_heads(t):
        t = t.reshape(t.shape[0], ng, HEADS_PER_SSM_GROUP).transpose(1, 0, 2)
        return jnp.pad(t, ((0, 0), (0, 0), (0, LANES - HEADS_PER_SSM_GROUP)))

    def unpad_heads(t):
        return t[:, :, :HEADS_PER_SSM_GROUP].transpose(1, 0, 2).reshape(t.shape[1], n_heads)

    dtp = pad_heads(dt_raw)
    alog_p, dtb_p = pad_heads(a_log), pad_heads(dt_bias)
    dsk_e = jnp.repeat(d_skip.reshape(ng, 1, HEADS_PER_SSM_GROUP), SSM_HEAD_DIM, axis=2)
    e = _expand_matrix()
    xbc = _conv_fwd(pzx, conv_w_f, conv_b_f, d_inner)
    y_ssd, states = _ssd_fwd(xbc, dtp, alog_p, dtb_p, dsk_e, e, d_inner)
    y3 = _gate_norm_fwd(y_ssd, pzx, norm_w_f)
    h1 = _mm_nn(y3, wos, F32, "mm_out_ssm")
    xhat1, rstd1, dy2, row_sq = _ln_fwd(xhat0, h1, g1, b1, "ln1_fwd_loss", affine_in=(g0, b0), target=tgt)
    loss_local = 0.5 * jnp.sum(row_sq) / d

    du1, du1b, dg1, db1 = _ln_bwd(dy2, xhat1, rstd1, g1, "ln1_bwd")
    dy3 = _mm_nt(du1b, wos, BF16, "mm_d_y3")
    g_wos = _mm_tn(y3, du1b, BF16, "mm_g_w_out_ssm").reshape(N_CHIPS, d_inner // N_CHIPS, d)
    norm_w_t = _tie(norm_w_f, ex.grad_ready("w_out_ssm", g_wos))
    dy_ssd, dz, d_nw = _gate_norm_bwd(dy3, y_ssd, pzx, norm_w_t)
    dxs, dbm, dcm, ddtp, d_alog, d_dtb, d_dsk = _ssd_bwd(xbc, dtp, alog_p, dtb_p, dsk_e, e, states, dy_ssd, d_inner)
    dpre, d_cw, d_cb = _conv_bwd_a(pzx, jnp.concatenate([dxs, dbm, dcm], axis=1), conv_w_f, conv_b_f, d_inner)
    dpzx = jnp.concatenate([dz, _conv_bwd_b(dpre, conv_w_f)], axis=1)
    ddt_raw = unpad_heads(ddtp)
    t1 = _mm_nt(ddt_raw, wdt, F32, "mm_d_x1_dt", add=du1, add_scale=DEEPNORM_ALPHA)
    dx1 = _mm_nt(dpzx, wzx, F32, "mm_d_x1", add=t1)
    ex.grad_sync("w_out_ssm", dx1)
    g_ws = jnp.concatenate([_mm_tn(x1b, dpzx, BF16, "mm_g_w_in_ssm"), _mm_tn(x1b, ddt_raw, BF16, "mm_g_w_dt")], axis=1)
    g_ws = jnp.stack([g_ws[:, j * ssm_cols:(j + 1) * ssm_cols] for j in range(N_CHIPS)])
    g0_t = _tie(g0, ex.grad_ready("w_in_ssm", g_ws))

    du0, du0b, dg0, db0 = _ln_bwd(dx1, xhat0, rstd0, g0_t, "ln0_bwd")
    dyat = _mm_nt_sharded_k(du0b, woa4, BF16, "mm_d_yat")
    g_woa = _mm_tn(yat, du0b, BF16, "mm_g_w_out_attn", shard_cols=d // N_CHIPS)
    tok_woa = ex.grad_ready("w_out_attn", g_woa)
    do, delta, dgate = _attn_pre_bwd(dyat, o, pa, hpg)
    pieces, dbt = [], []
    for g, (_, dil) in enumerate(ATTN_PATTERNS):
        dq, dk, dv, db_ = _attn_bwd(pa, bias, do, lse, delta, g, dil, hpg)
        pieces += [dq, dk, dv]
        dbt.append(db_)
    dpa = jnp.concatenate(pieces + [dgate], axis=1)
    grad_x = _mm_nt_sharded_k(dpa, wa4, F32, "mm_d_x0", add=du0, add_scale=DEEPNORM_ALPHA, after=tok_woa)
    ex.grad_sync("w_in_ssm", grad_x)
    grad_x = grad_x[None]
    g_wa = _mm_tn(xb, dpa, BF16, "mm_g_w_in_attn", shard_cols=wa4.shape[2])
    ex.grad_sync("w_out_attn", g_wa)
    ex.grad_ready("w_in_attn", g_wa)
    d_rel = _bias_reduce(jnp.stack(dbt), buckets, hpg)[:, :, 0].T

    d_dsk_h = d_dsk.reshape(n_heads, SSM_HEAD_DIM).sum(axis=1)
    small_full = [d_rel, d_cw, d_cb, unpad_heads(d_dtb), unpad_heads(d_alog), d_dsk_h[None], d_nw,
                  jnp.concatenate([dg0, dg1], axis=0), jnp.concatenate([db0, db1], axis=0)]
    return loss_local, grad_x, small_full


def kernel(x, w_in_attn, w_out_attn, rel_bias, w_in_ssm, conv_w, conv_b, dt_bias, a_log, d_skip, ssm_norm_w, w_out_ssm, ln_g, ln_b, loss_target, m_w_in_attn, m_w_out_attn, m_rel_bias, m_w_in_ssm, m_conv_w, m_conv_b, m_dt_bias, m_a_log, m_d_skip, m_ssm_norm_w, m_w_out_ssm, m_ln_g, m_ln_b, v_w_in_attn, v_w_out_attn, v_rel_bias, v_w_in_ssm, v_conv_w, v_conv_b, v_dt_bias, v_a_log, v_d_skip, v_ssm_norm_w, v_w_out_ssm, v_ln_g, v_ln_b):
    xi, yi, ci = lax.axis_index("x"), lax.axis_index("y"), lax.axis_index("c")
    chip = 2 * xi + yi
    cvec = jnp.reshape(ci, (1,)).astype(jnp.int32)
    kvec = jnp.reshape(chip, (1,)).astype(jnp.int32)

    wa4, woa4 = _gather_weights([
        _cast_to_slot(w_in_attn[0], kvec, "cast_w_in_attn"), _cast_to_slot(w_out_attn[0], kvec, "cast_w_out_attn")])
    ex = _Exchange(kvec, cvec, [_cast_to_slot(w_in_ssm[0], kvec, "cast_w_in_ssm"),
                                _cast_to_slot(w_out_ssm[0], kvec, "cast_w_out_ssm")], after=woa4)
    cw_l, cb_l, nw_l = conv_w[0], conv_b[0], ssm_norm_w[0]
    vec_shapes = [cw_l.shape, cb_l.shape, nw_l.shape]
    vec_all = _all_gather_small(_pack([cw_l, cb_l, nw_l]), False, "gather_vectors")
    parts = [_unpack(vec_all[2 * j], vec_shapes) for j in range(N_CHIPS)]
    conv_w_f = jnp.concatenate([p[0] for p in parts], axis=1)
    conv_b_f = jnp.concatenate([p[1] for p in parts], axis=0)[None]
    norm_w_f = jnp.concatenate([p[2] for p in parts], axis=0)[None]

    loss_local, grad_x, small_full = _local_step(
        x[0], loss_target[0], wa4, woa4, ex, conv_w_f, conv_b_f, norm_w_f, rel_bias, dt_bias, a_log,
        d_skip, ln_g, ln_b)
    loss = lax.psum(loss_local, ("x", "y", "c"))

    big_w = dict(w_in_attn=(w_in_attn, m_w_in_attn, v_w_in_attn), w_out_attn=(w_out_attn, m_w_out_attn, v_w_out_attn),
                 w_in_ssm=(w_in_ssm, m_w_in_ssm, v_w_in_ssm), w_out_ssm=(w_out_ssm, m_w_out_ssm, v_w_out_ssm))
    big = {}

    def finish(names, join_name, after):
        last = None
        for nm, gf in zip(names, _join_halves([ex.summed[nm] for nm in names], join_name, after)):
            w_, m_, v_ = big_w[nm]
            res = _adamw(w_[0], gf, m_[0], v_[0], "adamw_" + nm)
            big[nm] = [t[None] for t in res]
            last = res[3]
        return last

    last = finish(["w_out_ssm", "w_in_ssm", "w_out_attn"], "grads_join_halves_a", ex.last_token)
    ex.grad_sync("w_in_attn", last)
    finish(["w_in_attn"], "grads_join_halves_b", None)

    summed = _unpack(_all_gather_small(_pack(small_full), True, "reduce_small_grads"),
                     [t.shape for t in small_full])
    s_rel, s_cw, s_cb, s_dtb, s_alog, s_dsk, s_nw, s_lng, s_lnb = summed
    cwc, nwc = conv_w.shape[2], ssm_norm_w.shape[1]
    s_cw = lax.dynamic_slice_in_dim(s_cw, chip * cwc, cwc, axis=1)[None]
    s_cb = lax.dynamic_slice_in_dim(s_cb, chip * cwc, cwc, axis=1)
    s_nw = lax.dynamic_slice_in_dim(s_nw, chip * nwc, nwc, axis=1)
    small_names = ["rel_bias", "conv_w", "conv_b", "dt_bias", "a_log", "d_skip", "ssm_norm_w", "ln_g", "ln_b"]
    small_g = [s_rel, s_cw, s_cb, s_dtb, s_alog, s_dsk, s_nw, s_lng, s_lnb]
    small_w = [rel_bias, conv_w, conv_b, dt_bias, a_log, d_skip, ssm_norm_w, ln_g, ln_b]
    small_m = [m_rel_bias, m_conv_w, m_conv_b, m_dt_bias, m_a_log, m_d_skip, m_ssm_norm_w, m_ln_g, m_ln_b]
    small_v = [v_rel_bias, v_conv_w, v_conv_b, v_dt_bias, v_a_log, v_d_skip, v_ssm_norm_w, v_ln_g, v_ln_b]
    shapes = [t.shape for t in small_w]
    res = _adamw(_pack(small_w), _pack(small_g), _pack(small_m), _pack(small_v), "adamw_small")
    small = {nm: [] for nm in small_names}
    for packed in res:
        for nm, t in zip(small_names, _unpack(packed, shapes)):
            small[nm].append(t)

    order = ["w_in_attn", "w_out_attn", "rel_bias", "w_in_ssm", "conv_w", "conv_b", "dt_bias", "a_log",
             "d_skip", "ssm_norm_w", "w_out_ssm", "ln_g", "ln_b"]
    table = {**big, **small}
    outs = [loss, grad_x]
    for kind in range(4):
        outs += [table[nm][kind] for nm in order]
    return tuple(outs)
```

```python
import functools
import math

import numpy as np
import jax
import jax.numpy as jnp
from jax import lax
from jax.experimental import pallas as pl
from jax.experimental.pallas import tpu as pltpu

F32 = jnp.float32
BF16 = jnp.bfloat16
MESH = pl.DeviceIdType.MESH

ATTN_PATTERNS = ((128, 1), (512, 4), (2048, 16))
N_GROUPS_ATTN = 3
HEAD_DIM = 128
ATTN_BLOCK = 128
NUM_BUCKETS = 32
MAX_DISTANCE = 2048
SSM_HEAD_DIM = 64
HEADS_PER_SSM_GROUP = 16
SSM_GROUP_WIDTH = HEADS_PER_SSM_GROUP * SSM_HEAD_DIM
D_STATE = 128
CONV_WIDTH = 4
CHUNK = 128
DEPTH = 2
DEEPNORM_ALPHA = (2 * DEPTH) ** 0.25
LN_EPS = 1e-5
RMS_EPS = 1e-5
NEG_INF = -1e30
ADAM_LR = 0.001
ADAM_B1 = 0.9
ADAM_B2 = 0.999
ADAM_EPS = 1e-08
ADAM_WD = 0.01
ADAM_STEP = 10

N_CHIPS = 4
N_DEV = 8

VMEM_LIMIT_V7X = 56 * 1024 * 1024
LANES = 128


def _cparams(sem=None):
    return pltpu.CompilerParams(dimension_semantics=sem, vmem_limit_bytes=VMEM_LIMIT_V7X)


def _sigmoid(x):
    return 1.0 / (1.0 + jnp.exp(-x))


def _dot(a, b):
    return jnp.dot(a, b, preferred_element_type=F32)


def _dot_nt(a, b):
    return lax.dot_general(a, b, (((1,), (1,)), ((), ())), preferred_element_type=F32)


def _dot_tn(a, b):
    return lax.dot_general(a, b, (((0,), (0,)), ((), ())), preferred_element_type=F32)


def _split2(x):
    hi = x.astype(BF16)
    lo = (x - hi.astype(F32)).astype(BF16)
    return hi, lo


def _split3(x):
    hi = x.astype(BF16)
    r = x - hi.astype(F32)
    mid = r.astype(BF16)
    lo = (r - mid.astype(F32)).astype(BF16)
    return hi, mid, lo


def _matmul(a, b, *, mode, grid, a_spec, b_spec, out_shape, out_spec, tile, name,
            add=None, add_spec=None, add_scale=1.0, after=None, into=None):
    nk = grid[2]
    tm, tn = tile
    dot = {"nn": _dot, "nt": _dot_nt, "tn": _dot_tn}[mode]
    has_add = add is not None
    has_after = after is not None
    has_into = into is not None

    def body(*refs):
        a_ref, b_ref = refs[:2]
        add_ref = refs[2] if has_add else None
        o_ref, acc_ref = refs[-2:]
        k = pl.program_id(2)

        @pl.when(k == 0)
        def _():
            acc_ref[...] = jnp.zeros_like(acc_ref)

        acc_ref[...] += dot(a_ref[...].astype(BF16), b_ref[...].astype(BF16))

        @pl.when(k == nk - 1)
        def _():
            r = acc_ref[...]
            if has_add:
                r = r + add_scale * add_ref[...].astype(F32)
            o_ref[...] = r.astype(o_ref.dtype)

    in_specs = ([a_spec, b_spec] + ([add_spec] if has_add else []) + ([_ANY] if has_after else [])
                + ([_ANY] if has_into else []))
    args = (a, b) + ((add,) if has_add else ()) + ((after,) if has_after else ()) + ((into,) if has_into else ())
    return pl.pallas_call(
        body, name=name, grid=grid, in_specs=in_specs, out_specs=out_spec, out_shape=out_shape,
        input_output_aliases={len(args) - 1: 0} if has_into else {},
        scratch_shapes=[pltpu.VMEM((tm, tn), F32)],
        compiler_params=_cparams(("parallel", "parallel", "arbitrary")),
    )(*args)


def _pick(n, pref):
    for t in pref:
        if n % t == 0:
            return t
    return n


_TILE_PREF = (1024, 512, 256, 128)


def _mm_nn_sharded(a, w4, out_dtype, name, after=None, col_off=0, n=None, classes=1):
    m, k = a.shape
    _, _, nn = w4.shape
    n = N_CHIPS * nn if n is None else n
    tm, tk = _pick(m // classes, _TILE_PREF), _pick(k, _TILE_PREF)
    tn = _pick(math.gcd(math.gcd(nn, n), col_off) if col_off else math.gcd(nn, n), _TILE_PREF)
    npb = nn // tn
    co = col_off // tn
    bpc, kb = m // classes // tm, k // tk
    return _matmul(
        a.reshape(m // classes, classes * k), w4, mode="nn", grid=(m // tm, n // tn, kb), tile=(tm, tn), name=name,
        a_spec=pl.BlockSpec((tm, tk), lambda i, j, kk: (i % bpc, (i // bpc) * kb + kk)),
        b_spec=pl.BlockSpec((None, tk, tn), lambda i, j, kk: ((j + co) // npb, kk, (j + co) % npb)),
        out_shape=jax.ShapeDtypeStruct((m, n), out_dtype),
        out_spec=pl.BlockSpec((tm, tn), lambda i, j, kk: (i, j)), after=after)


def _mm_nn(a, b, out_dtype, name, b_row_off=0, add=None, add_scale=1.0):
    m, k = a.shape
    _, n = b.shape
    tm, tk, tn = _pick(m, _TILE_PREF), _pick(k, _TILE_PREF), _pick(n, _TILE_PREF)
    assert b_row_off % tk == 0
    ko = b_row_off // tk
    return _matmul(
        a, b, mode="nn", grid=(m // tm, n // tn, k // tk), tile=(tm, tn), name=name,
        a_spec=pl.BlockSpec((tm, tk), lambda i, j, kk: (i, kk)),
        b_spec=pl.BlockSpec((tk, tn), lambda i, j, kk: (kk + ko, j)),
        out_shape=jax.ShapeDtypeStruct((m, n), out_dtype),
        out_spec=pl.BlockSpec((tm, tn), lambda i, j, kk: (i, j)),
        add=add, add_spec=pl.BlockSpec((tm, tn), lambda i, j, kk: (i, j)), add_scale=add_scale)


def _mm_nt(a, b, out_dtype, name, add=None, add_scale=1.0, n=None, b_row_off=0):
    m, k = a.shape
    n = b.shape[0] if n is None else n
    tm, tk, tn = _pick(m, _TILE_PREF), _pick(k, _TILE_PREF), _pick(n, _TILE_PREF)
    assert b_row_off % tn == 0
    no = b_row_off // tn
    return _matmul(
        a, b, mode="nt", grid=(m // tm, n // tn, k // tk), tile=(tm, tn), name=name,
        a_spec=pl.BlockSpec((tm, tk), lambda i, j, kk: (i, kk)),
        b_spec=pl.BlockSpec((tn, tk), lambda i, j, kk: (j + no, kk)),
        out_shape=jax.ShapeDtypeStruct((m, n), out_dtype),
        out_spec=pl.BlockSpec((tm, tn), lambda i, j, kk: (i, j)),
        add=add, add_spec=pl.BlockSpec((tm, tn), lambda i, j, kk: (i, j)), add_scale=add_scale)


def _mm_nt_sharded_k(a, w4, out_dtype, name, add=None, add_scale=1.0, after=None):
    m, _ = a.shape
    _, n, kn = w4.shape
    tm, tk, tn = _pick(m, _TILE_PREF), _pick(kn, _TILE_PREF), _pick(n, _TILE_PREF)
    kpb = kn // tk
    return _matmul(
        a, w4, mode="nt", grid=(m // tm, n // tn, N_CHIPS * kpb), tile=(tm, tn), name=name,
        a_spec=pl.BlockSpec((tm, tk), lambda i, j, kk: (i, kk)),
        b_spec=pl.BlockSpec((None, tn, tk), lambda i, j, kk: (kk // kpb, j, kk % kpb)),
        out_shape=jax.ShapeDtypeStruct((m, n), out_dtype),
        out_spec=pl.BlockSpec((tm, tn), lambda i, j, kk: (i, j)),
        add=add, add_spec=pl.BlockSpec((tm, tn), lambda i, j, kk: (i, j)), add_scale=add_scale, after=after)


def _mm_tn(a, b, out_dtype, name, shard_cols=None, out_rows=None, out_row_off=0, into=None):
    k, m = a.shape
    _, n = b.shape
    nn = n if shard_cols is None else shard_cols
    tm, tk, tn = _pick(m, _TILE_PREF), _pick(k, _TILE_PREF), _pick(nn, _TILE_PREF)
    if shard_cols is None:
        assert out_row_off % tm == 0
        ro = out_row_off // tm
        out_shape = jax.ShapeDtypeStruct((m if out_rows is None else out_rows, n), out_dtype)
        out_spec = pl.BlockSpec((tm, tn), lambda i, j, kk: (i + ro, j))
    else:
        npb = nn // tn
        out_shape = jax.ShapeDtypeStruct((n // nn, m, nn), out_dtype)
        out_spec = pl.BlockSpec((None, tm, tn), lambda i, j, kk: (j // npb, i, j % npb))
    return _matmul(
        a, b, mode="tn", grid=(m // tm, n // tn, k // tk), tile=(tm, tn), name=name,
        a_spec=pl.BlockSpec((tk, tm), lambda i, j, kk: (kk, i)),
        b_spec=pl.BlockSpec((tk, tn), lambda i, j, kk: (kk, j)),
        out_shape=out_shape, out_spec=out_spec, into=into)


def _cast_bf16(x, name):
    r, c = x.shape
    tr = _pick(r, (512, 256, 128, 8))

    def body(x_ref, o_ref):
        o_ref[...] = x_ref[...].astype(BF16)

    return pl.pallas_call(
        body, name=name, grid=(r // tr,),
        in_specs=[pl.BlockSpec((tr, c), lambda i: (i, 0))],
        out_specs=pl.BlockSpec((tr, c), lambda i: (i, 0)),
        out_shape=jax.ShapeDtypeStruct((r, c), BF16),
        compiler_params=_cparams(("parallel",)),
    )(x)


def _bucket_tiles():
    qi = np.arange(ATTN_BLOCK)[:, None]
    ki = np.arange(2 * ATTN_BLOCK)[None, :]
    delta = np.clip(ATTN_BLOCK + qi - ki, 0, None)
    tiles = []
    max_exact = NUM_BUCKETS // 2
    for _, dil in ATTN_PATTERNS:
        dist = (delta * dil).astype(np.int32)
        d_f = np.maximum(dist, 1).astype(np.float32)
        large = max_exact + (np.log(d_f / np.float32(max_exact)) / np.float32(math.log(MAX_DISTANCE / max_exact))
                             * np.float32(NUM_BUCKETS - max_exact)).astype(np.int32)
        large = np.minimum(large, NUM_BUCKETS - 1)
        tiles.append(np.where(dist < max_exact, dist, large).astype(np.int32))
    return jnp.asarray(np.stack(tiles))


def _bias_expand(rel_bias, buckets, hpg):
    def body(tab_ref, bk_ref, o_ref):
        g, h = pl.program_id(0), pl.program_id(1)
        bk = bk_ref[...]
        acc = jnp.zeros((ATTN_BLOCK, 2 * ATTN_BLOCK), F32)
        for b in range(NUM_BUCKETS):
            acc = jnp.where(bk == b, tab_ref[b, g * hpg + h], acc)
        o_ref[...] = acc

    return pl.pallas_call(
        body, name="bias_expand", grid=(N_GROUPS_ATTN, hpg),
        in_specs=[pl.BlockSpec(memory_space=pltpu.SMEM),
                  pl.BlockSpec((None, ATTN_BLOCK, 2 * ATTN_BLOCK), lambda g, h: (g, 0, 0))],
        out_specs=pl.BlockSpec((None, None, ATTN_BLOCK, 2 * ATTN_BLOCK), lambda g, h: (g, h, 0, 0)),
        out_shape=jax.ShapeDtypeStruct((N_GROUPS_ATTN, hpg, ATTN_BLOCK, 2 * ATTN_BLOCK), F32),
        compiler_params=_cparams(("parallel", "parallel")),
    )(rel_bias, buckets)


def _bias_reduce(dtiles, buckets, hpg):
    def body(t_ref, bk_ref, o_ref):
        bk = bk_ref[...]
        t = t_ref[...]
        rows = lax.broadcasted_iota(jnp.int32, (NUM_BUCKETS, LANES), 0)
        acc = jnp.zeros((NUM_BUCKETS, LANES), F32)
        for b in range(NUM_BUCKETS):
            s = jnp.sum(jnp.sum(jnp.where(bk == b, t, 0.0), axis=1, keepdims=True), axis=0, keepdims=True)
            acc = jnp.where(rows == b, s, acc)
        o_ref[...] = acc

    return pl.pallas_call(
        body, name="bias_reduce", grid=(N_GROUPS_ATTN, hpg),
        in_specs=[pl.BlockSpec((None, None, ATTN_BLOCK, 2 * ATTN_BLOCK), lambda g, h: (g, h, 0, 0)),
                  pl.BlockSpec((None, ATTN_BLOCK, 2 * ATTN_BLOCK), lambda g, h: (g, 0, 0))],
        out_specs=pl.BlockSpec((None, NUM_BUCKETS, LANES), lambda g, h: (g * hpg + h, 0, 0)),
        out_shape=jax.ShapeDtypeStruct((N_GROUPS_ATTN * hpg, NUM_BUCKETS, LANES), F32),
        compiler_params=_cparams(("parallel", "parallel")),
    )(dtiles, buckets)


def _attn_valid(n_is_first):
    qi = lax.broadcasted_iota(jnp.int32, (ATTN_BLOCK, 2 * ATTN_BLOCK), 0)
    ki = lax.broadcasted_iota(jnp.int32, (ATTN_BLOCK, 2 * ATTN_BLOCK), 1)
    delta = ATTN_BLOCK + qi - ki
    band = (delta >= 0) & (delta <= ATTN_BLOCK)
    return band & (jnp.logical_not(n_is_first) | (ki >= ATTN_BLOCK))


def _attn_fwd(pg, bias, g, dil, hpg):
    s = pg.shape[0]
    w = hpg * HEAD_DIM
    rows = s // dil
    nb = rows // ATTN_BLOCK
    scale = HEAD_DIM ** -0.5

    def body(q_ref, kc_ref, kp_ref, vc_ref, vp_ref, bias_ref, o_ref, lse_ref):
        valid = _attn_valid(pl.program_id(1) == 0)
        for h in range(hpg):
            sl = slice(h * HEAD_DIM, (h + 1) * HEAD_DIM)
            k2 = jnp.concatenate([kp_ref[:, sl], kc_ref[:, sl]], axis=0)
            v2 = jnp.concatenate([vp_ref[:, sl], vc_ref[:, sl]], axis=0)
            sc = _dot_nt(q_ref[:, sl], k2) * scale + bias_ref[h]
            sc = jnp.where(valid, sc, NEG_INF)
            m = jnp.max(sc, axis=1, keepdims=True)
            p = jnp.exp(sc - m)
            l = jnp.sum(p, axis=1, keepdims=True)
            o_ref[:, sl] = _dot(p.astype(BF16), v2) / l
            lse_ref[:, sl] = jnp.broadcast_to(m + jnp.log(l), (ATTN_BLOCK, HEAD_DIM))

    def col(off):
        return lambda r, n: (r * nb + n, off)

    def colp(off):
        return lambda r, n: (r * nb + jnp.maximum(n - 1, 0), off)

    blk = (ATTN_BLOCK, w)
    tok = pl.BlockSpec(blk, lambda r, n: (n, r))
    o, lse = pl.pallas_call(
        body, name=f"attn_fwd_g{g}", grid=(dil, nb),
        in_specs=[pl.BlockSpec(blk, col(0)), pl.BlockSpec(blk, col(1)), pl.BlockSpec(blk, colp(1)),
                  pl.BlockSpec(blk, col(2)), pl.BlockSpec(blk, colp(2)),
                  pl.BlockSpec((None, hpg, ATTN_BLOCK, 2 * ATTN_BLOCK), lambda r, n: (g, 0, 0, 0))],
        out_specs=[tok, tok],
        out_shape=[jax.ShapeDtypeStruct((rows, dil * w), F32), jax.ShapeDtypeStruct((rows, dil * w), F32)],
        compiler_params=_cparams(("parallel", "parallel")),
    )(pg, pg, pg, pg, pg, bias)
    return o.reshape(s, w), lse.reshape(s, w)


def _attn_combine(os_, lses, gate, hpg):
    s, w = os_[0].shape
    tm = _pick(s, (256, 128))

    def body(o0, o1, o2, l0, l1, l2, gate_ref, o_ref, lse_ref, y_ref):
        a0, a1, a2 = l0[...], l1[...], l2[...]
        m = jnp.maximum(jnp.maximum(a0, a1), a2)
        e0, e1, e2 = jnp.exp(a0 - m), jnp.exp(a1 - m), jnp.exp(a2 - m)
        den = e0 + e1 + e2
        o = (e0 * o0[...] + e1 * o1[...] + e2 * o2[...]) / den
        gate = gate_ref[...].astype(F32)
        o_ref[...] = o.astype(BF16)
        lse_ref[...] = m + jnp.log(den)
        y_ref[...] = (o * (gate * _sigmoid(gate))).astype(BF16)

    spec = pl.BlockSpec((tm, w), lambda i: (i, 0))
    return pl.pallas_call(
        body, name="attn_combine", grid=(s // tm,),
        in_specs=[spec] * 7,
        out_specs=[spec, spec, spec],
        out_shape=[jax.ShapeDtypeStruct((s, w), BF16), jax.ShapeDtypeStruct((s, w), F32),
                   jax.ShapeDtypeStruct((s, w), BF16)],
        compiler_params=_cparams(("parallel",)),
    )(*os_, *lses, gate)


def _attn_pre_bwd(dy, o, gate, hpg):
    s, w = dy.shape
    tm = _pick(s, (256, 128))

    def body(dy_ref, o_ref, gate_ref, do_ref, dl_ref, dg_ref):
        gate = gate_ref[...].astype(F32)
        sg = _sigmoid(gate)
        dyv = dy_ref[...].astype(F32)
        ov = o_ref[...].astype(F32)
        do = dyv * (gate * sg)
        do_ref[...] = do.astype(BF16)
        dg_ref[...] = (dyv * ov * (sg * (1.0 + gate * (1.0 - sg)))).astype(BF16)
        prod = do * ov
        for h in range(hpg):
            sl = slice(h * HEAD_DIM, (h + 1) * HEAD_DIM)
            dl_ref[:, sl] = jnp.broadcast_to(jnp.sum(prod[:, sl], axis=1, keepdims=True), (tm, HEAD_DIM))

    spec = pl.BlockSpec((tm, w), lambda i: (i, 0))
    return pl.pallas_call(
        body, name="attn_pre_bwd", grid=(s // tm,),
        in_specs=[spec, spec, spec],
        out_specs=[spec, spec, spec],
        out_shape=[jax.ShapeDtypeStruct((s, w), BF16), jax.ShapeDtypeStruct((s, w), F32),
                   jax.ShapeDtypeStruct((s, w), BF16)],
        compiler_params=_cparams(("parallel",)),
    )(dy, o, gate)


def _attn_bwd(pg, bias, do, lse, delta, g, dil, hpg):
    s = pg.shape[0]
    w = hpg * HEAD_DIM
    rows = s // dil
    nb = rows // ATTN_BLOCK
    dov, lsev, dlv = (t.reshape(rows, dil * w) for t in (do, lse, delta))
    scale = HEAD_DIM ** -0.5

    def body(q_ref, kc_ref, kp_ref, vc_ref, vp_ref, bias_ref, do_ref, lse_ref, dl_ref,
             dq_ref, dk_ref, dv_ref, db_ref, dkc_ref, dvc_ref):
        r, i = pl.program_id(0), pl.program_id(1)
        n = nb - 1 - i
        valid = _attn_valid(n == 0)

        @pl.when((r == 0) & (i == 0))
        def _():
            db_ref[...] = jnp.zeros_like(db_ref)

        @pl.when(i == 0)
        def _():
            dkc_ref[...] = jnp.zeros_like(dkc_ref)
            dvc_ref[...] = jnp.zeros_like(dvc_ref)

        for h in range(hpg):
            sl = slice(h * HEAD_DIM, (h + 1) * HEAD_DIM)
            q = q_ref[:, sl]
            dov_ = do_ref[:, sl]
            k2 = jnp.concatenate([kp_ref[:, sl], kc_ref[:, sl]], axis=0)
            v2 = jnp.concatenate([vp_ref[:, sl], vc_ref[:, sl]], axis=0)
            sc = _dot_nt(q, k2) * scale + bias_ref[h]
            p = jnp.exp(jnp.where(valid, sc - lse_ref[:, sl][:, 0:1], NEG_INF))
            dp = _dot_nt(dov_, v2)
            ds = p * (dp - dl_ref[:, sl][:, 0:1])
            db_ref[h] += ds
            dsb = ds.astype(BF16)
            dq_ref[:, sl] = (_dot(dsb, k2) * scale).astype(BF16)
            dk2 = _dot_tn(dsb, q) * scale
            dv2 = _dot_tn(p.astype(BF16), dov_)
            dk_ref[:, sl] = (dk2[ATTN_BLOCK:] + dkc_ref[:, sl]).astype(BF16)
            dv_ref[:, sl] = (dv2[ATTN_BLOCK:] + dvc_ref[:, sl]).astype(BF16)
            dkc_ref[:, sl] = dk2[:ATTN_BLOCK]
            dvc_ref[:, sl] = dv2[:ATTN_BLOCK]

    def col(off):
        return lambda r, i: (r * nb + nb - 1 - i, off)

    def colp(off):
        return lambda r, i: (r * nb + jnp.maximum(nb - 2 - i, 0), off)

    blk = (ATTN_BLOCK, w)
    tok = pl.BlockSpec(blk, lambda r, i: (nb - 1 - i, r))
    dq, dk, dv, db = pl.pallas_call(
        body, name=f"attn_bwd_g{g}", grid=(dil, nb),
        in_specs=[pl.BlockSpec(blk, col(0)), pl.BlockSpec(blk, col(1)), pl.BlockSpec(blk, colp(1)),
                  pl.BlockSpec(blk, col(2)), pl.BlockSpec(blk, colp(2)),
                  pl.BlockSpec((None, hpg, ATTN_BLOCK, 2 * ATTN_BLOCK), lambda r, i: (g, 0, 0, 0)),
                  tok, tok, tok],
        out_specs=[tok, tok, tok,
                   pl.BlockSpec((hpg, ATTN_BLOCK, 2 * ATTN_BLOCK), lambda r, i: (0, 0, 0))],
        out_shape=[jax.ShapeDtypeStruct((rows, dil * w), BF16)] * 3
        + [jax.ShapeDtypeStruct((hpg, ATTN_BLOCK, 2 * ATTN_BLOCK), F32)],
        scratch_shapes=[pltpu.VMEM(blk, F32), pltpu.VMEM(blk, F32)],
        compiler_params=_cparams(("arbitrary", "arbitrary")),
    )(pg, pg, pg, pg, pg, bias, dov, lsev, dlv)
    return dq.reshape(s, w), dk.reshape(s, w), dv.reshape(s, w), db


def _ln_fwd(xin, h, gamma, beta, name, affine_in=None, target=None):
    s, d = xin.shape
    tm = _pick(s, (128,))
    has_aff = affine_in is not None
    has_tgt = target is not None

    def body(*refs):
        it = iter(refs)
        x_ref, h_ref, g_ref, b_ref = next(it), next(it), next(it), next(it)
        if has_aff:
            gi_ref, bi_ref = next(it), next(it)
        if has_tgt:
            t_ref = next(it)
        xh_ref, rs_ref = next(it), next(it)
        x = x_ref[...]
        if has_aff:
            x = x * gi_ref[...] + bi_ref[...]
        u = DEEPNORM_ALPHA * x + h_ref[...]
        mu = jnp.mean(u, axis=1, keepdims=True)
        uc = u - mu
        var = jnp.mean(uc * uc, axis=1, keepdims=True)
        rstd = lax.rsqrt(var + LN_EPS)
        xhat = uc * rstd
        xh_ref[...] = xhat
        rs_ref[...] = rstd
        y = xhat * g_ref[...] + b_ref[...]
        if has_tgt:
            dy_ref, l_ref = next(it), next(it)
            e = y - t_ref[...]
            dy_ref[...] = e * (1.0 / d)
            l_ref[...] = jnp.sum(e * e, axis=1, keepdims=True)
        else:
            y_ref = next(it)
            y_ref[...] = y.astype(BF16)

    row = pl.BlockSpec((tm, d), lambda i: (i, 0))
    vec = pl.BlockSpec((1, d), lambda i: (0, 0))
    one = pl.BlockSpec((tm, 1), lambda i: (i, 0))
    in_specs = [row, row, vec, vec] + ([vec, vec] if has_aff else []) + ([row] if has_tgt else [])
    args = [xin, h, gamma, beta] + (list(affine_in) if has_aff else []) + ([target] if has_tgt else [])
    out_specs = [row, one] + ([row, one] if has_tgt else [row])
    out_shape = [jax.ShapeDtypeStruct((s, d), F32), jax.ShapeDtypeStruct((s, 1), F32)]
    out_shape += ([jax.ShapeDtypeStruct((s, d), F32), jax.ShapeDtypeStruct((s, 1), F32)] if has_tgt
                  else [jax.ShapeDtypeStruct((s, d), BF16)])
    return pl.pallas_call(
        body, name=name, grid=(s // tm,), in_specs=in_specs, out_specs=out_specs, out_shape=out_shape,
        compiler_params=_cparams(("parallel",)),
    )(*args)


def _ln_bwd(dy, xhat, rstd, gamma, name):
    s, d = dy.shape
    tm = _pick(s, (128,))

    def body(dy_ref, xh_ref, rs_ref, g_ref, du_ref, dub_ref, dg_ref, db_ref):
        @pl.when(pl.program_id(0) == 0)
        def _():
            dg_ref[...] = jnp.zeros_like(dg_ref)
            db_ref[...] = jnp.zeros_like(db_ref)

        dyv = dy_ref[...]
        xh = xh_ref[...]
        dg_ref[...] += jnp.sum(dyv * xh, axis=0, keepdims=True)
        db_ref[...] += jnp.sum(dyv, axis=0, keepdims=True)
        dxh = dyv * g_ref[...]
        m1 = jnp.mean(dxh, axis=1, keepdims=True)
        m2 = jnp.mean(dxh * xh, axis=1, keepdims=True)
        du = rs_ref[...] * (dxh - m1 - xh * m2)
        du_ref[...] = du
        dub_ref[...] = du.astype(BF16)

    row = pl.BlockSpec((tm, d), lambda i: (i, 0))
    vec = pl.BlockSpec((1, d), lambda i: (0, 0))
    one = pl.BlockSpec((tm, 1), lambda i: (i, 0))
    return pl.pallas_call(
        body, name=name, grid=(s // tm,), in_specs=[row, row, one, vec],
        out_specs=[row, row, vec, vec],
        out_shape=[jax.ShapeDtypeStruct((s, d), F32), jax.ShapeDtypeStruct((s, d), BF16),
                   jax.ShapeDtypeStruct((1, d), F32), jax.ShapeDtypeStruct((1, d), F32)],
        compiler_params=_cparams(("arbitrary",)),
    )(dy, xhat, rstd, gamma)


_HALO = 16


def _conv_taps(ext, tm, w_ref):
    acc = None
    for k in range(CONV_WIDTH):
        lo = _HALO - (CONV_WIDTH - 1) + k
        term = w_ref[k:k + 1, :] * ext[lo:lo + tm, :]
        acc = term if acc is None else acc + term
    return acc


def _conv_fwd(pzx, conv_w, conv_b, d_inner):
    s, _ = pzx.shape
    cd = conv_w.shape[1]
    tm = _pick(s, (512, 256, 128))
    tc = _pick(cd, (1024, 512, 256, 128))
    off = d_inner // tc
    hb = tm // _HALO

    def body(x_ref, p_ref, w_ref, b_ref, o_ref):
        prev = jnp.where(pl.program_id(0) > 0, p_ref[...].astype(F32), 0.0)
        ext = jnp.concatenate([prev, x_ref[...].astype(F32)], axis=0)
        pre = _conv_taps(ext, tm, w_ref) + b_ref[...]
        o_ref[...] = (pre * _sigmoid(pre)).astype(BF16)

    return pl.pallas_call(
        body, name="conv_fwd", grid=(s // tm, cd // tc),
        in_specs=[pl.BlockSpec((tm, tc), lambda i, j: (i, off + j)),
                  pl.BlockSpec((_HALO, tc), lambda i, j: (jnp.maximum(i * hb - 1, 0), off + j)),
                  pl.BlockSpec((CONV_WIDTH, tc), lambda i, j: (0, j)),
                  pl.BlockSpec((1, tc), lambda i, j: (0, j))],
        out_specs=pl.BlockSpec((tm, tc), lambda i, j: (i, j)),
        out_shape=jax.ShapeDtypeStruct((s, cd), BF16),
        compiler_params=_cparams(("parallel", "parallel")),
    )(pzx, pzx, conv_w, conv_b)


def _conv_bwd_a(pzx, dxbc, conv_w, conv_b, d_inner):
    s, _ = pzx.shape
    cd = conv_w.shape[1]
    tm = _pick(s, (512, 256, 128))
    tc = _pick(cd, (1024, 512, 256, 128))
    off = d_inner // tc
    hb = tm // _HALO

    def body(x_ref, p_ref, d_ref, w_ref, b_ref, o_ref, dw_ref, db_ref):
        @pl.when(pl.program_id(1) == 0)
        def _():
            dw_ref[...] = jnp.zeros_like(dw_ref)
            db_ref[...] = jnp.zeros_like(db_ref)

        prev = jnp.where(pl.program_id(1) > 0, p_ref[...].astype(F32), 0.0)
        ext = jnp.concatenate([prev, x_ref[...].astype(F32)], axis=0)
        pre = _conv_taps(ext, tm, w_ref) + b_ref[...]
        sg = _sigmoid(pre)
        dpre = d_ref[...].astype(F32) * (sg * (1.0 + pre * (1.0 - sg)))
        o_ref[...] = dpre
        db_ref[...] += jnp.sum(dpre, axis=0, keepdims=True)
        for k in range(CONV_WIDTH):
            lo = _HALO - (CONV_WIDTH - 1) + k
            dw_ref[k:k + 1, :] += jnp.sum(dpre * ext[lo:lo + tm, :], axis=0, keepdims=True)

    return pl.pallas_call(
        body, name="conv_bwd_a", grid=(cd // tc, s // tm),
        in_specs=[pl.BlockSpec((tm, tc), lambda j, i: (i, off + j)),
                  pl.BlockSpec((_HALO, tc), lambda j, i: (jnp.maximum(i * hb - 1, 0), off + j)),
                  pl.BlockSpec((tm, tc), lambda j, i: (i, j)),
                  pl.BlockSpec((CONV_WIDTH, tc), lambda j, i: (0, j)),
                  pl.BlockSpec((1, tc), lambda j, i: (0, j))],
        out_specs=[pl.BlockSpec((tm, tc), lambda j, i: (i, j)),
                   pl.BlockSpec((CONV_WIDTH, tc), lambda j, i: (0, j)),
                   pl.BlockSpec((1, tc), lambda j, i: (0, j))],
        out_shape=[jax.ShapeDtypeStruct((s, cd), F32), jax.ShapeDtypeStruct((CONV_WIDTH, cd), F32),
                   jax.ShapeDtypeStruct((1, cd), F32)],
        compiler_params=_cparams(("parallel", "arbitrary")),
    )(pzx, pzx, dxbc, conv_w, conv_b)


def _conv_bwd_b(dpre, conv_w, into, col_off):
    s, cd = dpre.shape
    tm = _pick(s, (512, 256, 128))
    tc = _pick(cd, (1024, 512, 256, 128))
    hb = tm // 8
    nrb = s // tm
    assert col_off % tc == 0
    co = col_off // tc

    def body(x_ref, nx_ref, w_ref, into_ref, o_ref):
        nxt = jnp.where(pl.program_id(0) < nrb - 1, nx_ref[...], 0.0)
        ext = jnp.concatenate([x_ref[...], nxt], axis=0)
        acc = None
        for k in range(CONV_WIDTH):
            lo = CONV_WIDTH - 1 - k
            term = w_ref[k:k + 1, :] * ext[lo:lo + tm, :]
            acc = term if acc is None else acc + term
        o_ref[...] = acc.astype(BF16)

    return pl.pallas_call(
        body, name="conv_bwd_b", grid=(nrb, cd // tc),
        in_specs=[pl.BlockSpec((tm, tc), lambda i, j: (i, j)),
                  pl.BlockSpec((8, tc), lambda i, j: (jnp.minimum((i + 1) * hb, s // 8 - 1), j)),
                  pl.BlockSpec((CONV_WIDTH, tc), lambda i, j: (0, j)), _ANY],
        out_specs=pl.BlockSpec((tm, tc), lambda i, j: (i, j + co)),
        out_shape=jax.ShapeDtypeStruct(into.shape, BF16),
        input_output_aliases={3: 0},
        compiler_params=_cparams(("parallel", "parallel")),
    )(dpre, dpre, conv_w, into)


def _expand_matrix():
    e = np.zeros((LANES, SSM_GROUP_WIDTH), np.float32)
    for h in range(HEADS_PER_SSM_GROUP):
        e[h, h * SSM_HEAD_DIM:(h + 1) * SSM_HEAD_DIM] = 1.0
    return jnp.asarray(e, BF16)


def _expand(t, e):
    hi, lo = _split2(t)
    return _dot(hi, e) + _dot(lo, e)


def _segsum(v, e):
    hi, lo = _split2(v)
    return _dot_nt(hi, e) + _dot_nt(lo, e)


def _tri_dot(tri, x):
    hi, mid, lo = _split3(x)
    return _dot(tri, hi) + _dot(tri, mid) + _dot(tri, lo)


def _ssd_common(dtp_ref, a_ref, dtb_ref, x_ref, e):
    li = lax.broadcasted_iota(jnp.int32, (CHUNK, CHUNK), 0)
    si = lax.broadcasted_iota(jnp.int32, (CHUNK, CHUNK), 1)
    causal = li >= si
    tril = causal.astype(BF16)
    raw = dtp_ref[...] + dtb_ref[...]
    dt = jnp.maximum(raw, 0.0) + jnp.log(1.0 + jnp.exp(-jnp.abs(raw)))
    head_lane = lax.broadcasted_iota(jnp.int32, (1, LANES), 1) < HEADS_PER_SSM_GROUP
    a = jnp.where(head_lane, -jnp.exp(a_ref[...]), 0.0)
    a_cum = _tri_dot(tril, dt * a)
    a_cum_t = a_cum.T
    e_a = jnp.exp(a_cum)
    to_end = jnp.exp(a_cum[CHUNK - 1:CHUNK, :] - a_cum)
    x = x_ref[...].astype(F32)
    dt_e = _expand(dt, e)
    return dict(causal=causal, raw=raw, dt=dt, a=a, a_cum=a_cum, a_cum_t=a_cum_t, e_a=e_a,
                to_end=to_end, x=x, dt_e=dt_e, xdt=x * dt_e, e_a_e=_expand(e_a, e),
                to_end_e=_expand(to_end, e))


def _decay(q, h):
    seg = q["a_cum"][:, h:h + 1] - q["a_cum_t"][h:h + 1, :]
    return jnp.exp(jnp.where(q["causal"], seg, -jnp.inf))


def _ssd_specs(ng, d_inner, rev, nc):
    cidx = (lambda i: nc - 1 - i) if rev else (lambda i: i)
    boff = d_inner // D_STATE
    return dict(
        xs=pl.BlockSpec((CHUNK, SSM_GROUP_WIDTH), lambda g, i: (cidx(i), g)),
        b=pl.BlockSpec((CHUNK, D_STATE), lambda g, i: (cidx(i), boff + g)),
        c=pl.BlockSpec((CHUNK, D_STATE), lambda g, i: (cidx(i), boff + ng + g)),
        dtp=pl.BlockSpec((None, CHUNK, LANES), lambda g, i: (g, cidx(i), 0)),
        vec=pl.BlockSpec((None, 1, LANES), lambda g, i: (g, 0, 0)),
        wide=pl.BlockSpec((None, 1, SSM_GROUP_WIDTH), lambda g, i: (g, 0, 0)),
        e=pl.BlockSpec((LANES, SSM_GROUP_WIDTH), lambda g, i: (0, 0)),
        st=pl.BlockSpec((None, None, D_STATE, SSM_GROUP_WIDTH), lambda g, i: (g, cidx(i), 0, 0)),
        tok=pl.BlockSpec((CHUNK, SSM_GROUP_WIDTH), lambda g, i: (cidx(i), g)),
        bc_out=pl.BlockSpec((CHUNK, D_STATE), lambda g, i: (cidx(i), g)),
    )


def _ssd_fwd(xbc, dtp, a_pad, dtb_pad, dsk_e, e, d_inner):
    s = xbc.shape[0]
    ng = d_inner // SSM_GROUP_WIDTH
    nc = s // CHUNK

    def body(x_ref, b_ref, c_ref, dtp_ref, a_ref, dtb_ref, dsk_ref, e_ref, y_ref, st_ref, state):
        lane = lax.broadcasted_iota(jnp.int32, (CHUNK, LANES), 1)
        @pl.when(pl.program_id(1) == 0)
        def _():
            state[...] = jnp.zeros_like(state)

        ev = e_ref[...]
        q = _ssd_common(dtp_ref, a_ref, dtb_ref, x_ref, ev)
        bm, cm = b_ref[...], c_ref[...]
        cb = _dot_nt(cm, bm)
        s0 = state[...]
        st_ref[...] = s0
        y = _dot(cm, s0.astype(BF16)) * q["e_a_e"] + dsk_ref[...] * q["x"]
        xdt = q["xdt"]
        left = lane[:, :] < SSM_HEAD_DIM
        for j in range(HEADS_PER_SSM_GROUP // 2):
            sl = slice(j * LANES, (j + 1) * LANES)
            x2 = xdt[:, sl]
            m0 = (cb * _decay(q, 2 * j)).astype(BF16)
            m1 = (cb * _decay(q, 2 * j + 1)).astype(BF16)
            mcat = jnp.concatenate([m0, m1], axis=1)
            xbd = jnp.concatenate([jnp.where(left, x2, 0.0), jnp.where(left, 0.0, x2)], axis=0).astype(BF16)
            y_ref[:, sl] = (y[:, sl] + _dot(mcat, xbd)).astype(BF16)
        state[...] = s0 * q["e_a_e"][CHUNK - 1:CHUNK, :] + _dot_tn(bm, (q["to_end_e"] * xdt).astype(BF16))

    sp = _ssd_specs(ng, d_inner, False, nc)
    return pl.pallas_call(
        body, name="ssd_fwd", grid=(ng, nc),
        in_specs=[sp["xs"], sp["b"], sp["c"], sp["dtp"], sp["vec"], sp["vec"], sp["wide"], sp["e"]],
        out_specs=[sp["tok"], sp["st"]],
        out_shape=[jax.ShapeDtypeStruct((s, d_inner), BF16),
                   jax.ShapeDtypeStruct((ng, nc, D_STATE, SSM_GROUP_WIDTH), F32)],
        scratch_shapes=[pltpu.VMEM((D_STATE, SSM_GROUP_WIDTH), F32)],
        compiler_params=_cparams(("parallel", "arbitrary")),
    )(xbc, xbc, xbc, dtp, a_pad, dtb_pad, dsk_e, e)


def _ssd_bwd(xbc, dtp, a_pad, dtb_pad, dsk_e, e, states, dy, d_inner):
    s = xbc.shape[0]
    ng = d_inner // SSM_GROUP_WIDTH
    nc = s // CHUNK

    def body(x_ref, b_ref, c_ref, dtp_ref, a_ref, dtb_ref, dsk_ref, e_ref, st_ref, dy_ref,
             dx_ref, db_ref, dc_ref, ddt_ref, da_ref, ddtb_ref, dd_ref, dstate):
        lane = lax.broadcasted_iota(jnp.int32, (CHUNK, LANES), 1)
        sub = lax.broadcasted_iota(jnp.int32, (CHUNK, LANES), 0)
        @pl.when(pl.program_id(1) == 0)
        def _():
            dstate[...] = jnp.zeros_like(dstate)
            da_ref[...] = jnp.zeros_like(da_ref)
            ddtb_ref[...] = jnp.zeros_like(ddtb_ref)
            dd_ref[...] = jnp.zeros_like(dd_ref)

        ev = e_ref[...]
        q = _ssd_common(dtp_ref, a_ref, dtb_ref, x_ref, ev)
        bm, cm = b_ref[...], c_ref[...]
        cb = _dot_nt(cm, bm)
        x, xdt, e_a_e, to_end_e = q["x"], q["xdt"], q["e_a_e"], q["to_end_e"]
        s0 = st_ref[...]
        s0b = s0.astype(BF16)
        ds1 = dstate[...]
        ds1b = ds1.astype(BF16)
        dy = dy_ref[...].astype(F32)
        e_last_e = e_a_e[CHUNK - 1:CHUNK, :]

        dye = dy * e_a_e
        dyeb = dye.astype(BF16)
        cs0 = _dot(cm, s0b)
        dc = _dot_nt(dyeb, s0b)
        dstate[...] = e_last_e * ds1 + _dot_tn(cm, dyeb)
        da_col = _segsum(dye * cs0, ev)

        gmat = _dot(bm, ds1b)
        dxdt = to_end_e * gmat
        dte = _segsum(xdt * gmat, ev) * q["to_end"]
        db = _dot_nt((to_end_e * xdt).astype(BF16), ds1b)
        da_col = da_col - dte
        last_row = (jnp.sum(dte, axis=0, keepdims=True)
                    + q["e_a"][CHUNK - 1:CHUNK, :] * jnp.sum(_segsum(s0 * ds1, ev), axis=0, keepdims=True))

        left = lane < SSM_HEAD_DIM
        dcb = jnp.zeros((CHUNK, CHUNK), F32)
        row_acc = jnp.zeros((CHUNK, LANES), F32)
        for j in range(HEADS_PER_SSM_GROUP // 2):
            sl = slice(j * LANES, (j + 1) * LANES)
            x2 = xdt[:, sl].astype(BF16)
            dy2 = dy[:, sl]
            dyl = jnp.where(left, dy2, 0.0).astype(BF16)
            dyr = jnp.where(left, 0.0, dy2).astype(BF16)
            ms = []
            for hh, dyh in ((0, dyl), (1, dyr)):
                h = 2 * j + hh
                dec = _decay(q, h)
                m = cb * dec
                dm = _dot_nt(dyh, x2)
                dcb = dcb + dm * dec
                dseg = dm * m
                da_col = da_col + jnp.where(lane == h, jnp.sum(dseg, axis=1, keepdims=True), 0.0)
                row_acc = row_acc + jnp.where(sub == h, jnp.sum(dseg, axis=0, keepdims=True), 0.0)
                ms.append(m.astype(BF16))
            mst = jnp.concatenate(ms, axis=0)
            dyst = jnp.concatenate([dyl, dyr], axis=0)
            d2 = dxdt[:, sl] + _dot_tn(mst, dyst)
            dx_ref[:, sl] = (d2 * q["dt_e"][:, sl] + dsk_ref[:, sl] * dy2).astype(BF16)
            dxdt_x = d2 * x[:, sl]
            if j == 0:
                parts = [dxdt_x]
            else:
                parts.append(dxdt_x)
        dcbb = dcb.astype(BF16)
        dc_ref[...] = (dc + _dot(dcbb, bm)).astype(BF16)
        db_ref[...] = (db + _dot_tn(dcbb, cm)).astype(BF16)

        d_a = da_col - row_acc.T + jnp.where(sub == CHUNK - 1, last_row, 0.0)
        triu = (lax.broadcasted_iota(jnp.int32, (CHUNK, CHUNK), 1)
                >= lax.broadcasted_iota(jnp.int32, (CHUNK, CHUNK), 0)).astype(BF16)
        d_dta = _tri_dot(triu, d_a)
        ddt = d_dta * q["a"] + _segsum(jnp.concatenate(parts, axis=1), ev)
        ddt_raw = ddt * _sigmoid(q["raw"])
        ddt_ref[...] = ddt_raw
        da_ref[...] += jnp.sum(d_dta * q["dt"], axis=0, keepdims=True) * q["a"]
        ddtb_ref[...] += jnp.sum(ddt_raw, axis=0, keepdims=True)
        dd_ref[...] += jnp.sum(dy * x, axis=0, keepdims=True)

    sp = _ssd_specs(ng, d_inner, True, nc)
    return pl.pallas_call(
        body, name="ssd_bwd", grid=(ng, nc),
        in_specs=[sp["xs"], sp["b"], sp["c"], sp["dtp"], sp["vec"], sp["vec"], sp["wide"], sp["e"],
                  sp["st"], sp["tok"]],
        out_specs=[sp["tok"], sp["bc_out"], sp["bc_out"], sp["dtp"], sp["vec"], sp["vec"], sp["wide"]],
        out_shape=[jax.ShapeDtypeStruct((s, d_inner), BF16),
                   jax.ShapeDtypeStruct((s, ng * D_STATE), BF16),
                   jax.ShapeDtypeStruct((s, ng * D_STATE), BF16),
                   jax.ShapeDtypeStruct((ng, s, LANES), F32),
                   jax.ShapeDtypeStruct((ng, 1, LANES), F32),
                   jax.ShapeDtypeStruct((ng, 1, LANES), F32),
                   jax.ShapeDtypeStruct((ng, 1, SSM_GROUP_WIDTH), F32)],
        scratch_shapes=[pltpu.VMEM((D_STATE, SSM_GROUP_WIDTH), F32)],
        compiler_params=_cparams(("parallel", "arbitrary")),
    )(xbc, xbc, xbc, dtp, a_pad, dtb_pad, dsk_e, e, states, dy)


def _gate_norm_fwd(y, pzx, norm_w):
    s, di = y.shape
    ng = di // SSM_GROUP_WIDTH
    tm = _pick(s, (512, 256, 128))

    def body(y_ref, z_ref, w_ref, o_ref):
        z = z_ref[...].astype(F32)
        y2 = y_ref[...].astype(F32) * (z * _sigmoid(z))
        r = lax.rsqrt(jnp.mean(y2 * y2, axis=1, keepdims=True) + RMS_EPS)
        o_ref[...] = (y2 * r * w_ref[...]).astype(BF16)

    blk = pl.BlockSpec((tm, SSM_GROUP_WIDTH), lambda i, g: (i, g))
    return pl.pallas_call(
        body, name="gate_norm_fwd", grid=(s // tm, ng),
        in_specs=[blk, blk, pl.BlockSpec((1, SSM_GROUP_WIDTH), lambda i, g: (0, g))],
        out_specs=blk, out_shape=jax.ShapeDtypeStruct((s, di), BF16),
        compiler_params=_cparams(("parallel", "parallel")),
    )(y, pzx, norm_w)


def _gate_norm_bwd(dy3, y, pzx, norm_w):
    s, di = y.shape
    ng = di // SSM_GROUP_WIDTH
    tm = _pick(s, (512, 256, 128))

    def body(d_ref, y_ref, z_ref, w_ref, dy_ref, dz_ref, dw_ref):
        @pl.when(pl.program_id(1) == 0)
        def _():
            dw_ref[...] = jnp.zeros_like(dw_ref)

        z = z_ref[...].astype(F32)
        yv = y_ref[...].astype(F32)
        sg = _sigmoid(z)
        sz = z * sg
        y2 = yv * sz
        r = lax.rsqrt(jnp.mean(y2 * y2, axis=1, keepdims=True) + RMS_EPS)
        nrm = y2 * r
        d3 = d_ref[...].astype(F32)
        dw_ref[...] += jnp.sum(d3 * nrm, axis=0, keepdims=True)
        dn = d3 * w_ref[...]
        dy2 = r * (dn - nrm * jnp.mean(dn * nrm, axis=1, keepdims=True))
        dy_ref[...] = (dy2 * sz).astype(BF16)
        dz_ref[...] = (dy2 * yv * (sg * (1.0 + z * (1.0 - sg)))).astype(BF16)

    blk = pl.BlockSpec((tm, SSM_GROUP_WIDTH), lambda g, i: (i, g))
    vec = pl.BlockSpec((1, SSM_GROUP_WIDTH), lambda g, i: (0, g))
    return pl.pallas_call(
        body, name="gate_norm_bwd", grid=(ng, s // tm),
        in_specs=[blk, blk, blk, vec], out_specs=[blk, blk, vec],
        out_shape=[jax.ShapeDtypeStruct((s, di), BF16), jax.ShapeDtypeStruct(pzx.shape, BF16),
                   jax.ShapeDtypeStruct((1, di), F32)],
        compiler_params=_cparams(("parallel", "arbitrary")),
    )(dy3, y, pzx, norm_w)


_ANY = pl.BlockSpec(memory_space=pl.ANY)


def _place():
    x, y, c = lax.axis_index("x"), lax.axis_index("y"), lax.axis_index("c")
    chips = [(1 - x, y), (x, 1 - y), (1 - x, 1 - y)]
    return x, y, c, chips


def _cast_to_slot(x, kvec, name):
    r, cn = x.shape
    tr = _rows_per_block(r, cn)

    def body(k_ref, x_ref, o_ref):
        o_ref[...] = x_ref[...].astype(BF16)

    grid_spec = pltpu.PrefetchScalarGridSpec(
        num_scalar_prefetch=1, grid=(r // tr,),
        in_specs=[pl.BlockSpec((tr, cn), lambda i, k: (i, 0))],
        out_specs=pl.BlockSpec((None, tr, cn), lambda i, k: (k[0], i, 0)))
    return pl.pallas_call(
        body, name=name, grid_spec=grid_spec, out_shape=jax.ShapeDtypeStruct((N_CHIPS, r, cn), BF16),
        compiler_params=_cparams(("parallel",)),
    )(kvec, x)


def _gather_weights(bufs):
    n = len(bufs)

    def body(*refs):
        outs = refs[n:2 * n]
        send_sems, recv_sems = refs[2 * n:]
        x, y, c, chips = _place()
        k = 2 * x + y
        sib = (x, y, 1 - c)

        def half(w, slot, hc):
            hr = bufs[w].shape[1] // 2
            return outs[w].at[slot, pl.ds(hc * hr, hr)]

        def copy(w, j, src, dst, to):
            return pltpu.make_async_remote_copy(
                src_ref=src, dst_ref=dst, send_sem=send_sems.at[w, j], recv_sem=recv_sems.at[w, j],
                device_id=to, device_id_type=MESH)

        first, passed = [], []
        for w in range(n):
            for j, chip in enumerate(chips):
                cp = copy(w, j, half(w, k, c), half(w, k, c), (*chip, c))
                cp.start()
                first.append(cp)
        for w in range(n):
            for j, (cx, cy) in enumerate(chips):
                kj = 2 * cx + cy
                copy(w, j, half(w, kj, c), half(w, kj, c), sib).wait_recv()
                cp = copy(w, 3 + j, half(w, kj, c), half(w, kj, c), sib)
                cp.start()
                passed.append(cp)
        for w in range(n):
            for j, (cx, cy) in enumerate(chips):
                kj = 2 * cx + cy
                copy(w, 3 + j, half(w, kj, 1 - c), half(w, kj, 1 - c), sib).wait_recv()
        for cp in first + passed:
            cp.wait_send()

    return pl.pallas_call(
        body, name="gather_weights",
        in_specs=[_ANY] * n, out_specs=[_ANY] * n,
        out_shape=[jax.ShapeDtypeStruct(b.shape, b.dtype) for b in bufs],
        input_output_aliases={w: w for w in range(n)},
        scratch_shapes=[pltpu.SemaphoreType.DMA((n, 6)), pltpu.SemaphoreType.DMA((n, 6))],
    )(*bufs)


def _swap_halves(gs, name, after=None):
    n = len(gs)
    extra = [] if after is None else [after]

    def body(*refs):
        ins, outs = refs[:n], refs[n + len(extra):2 * n + len(extra)]
        send_sems, recv_sems = refs[2 * n + len(extra):]
        x, y, c, _ = _place()
        cps = []
        for w in range(n):
            hr = gs[w].shape[1] // 2
            cp = pltpu.make_async_remote_copy(
                src_ref=ins[w].at[:, pl.ds((1 - c) * hr, hr)], dst_ref=outs[w],
                send_sem=send_sems.at[w], recv_sem=recv_sems.at[w],
                device_id=(x, y, 1 - c), device_id_type=MESH)
            cp.start()
            cps.append(cp)
        for cp in cps:
            cp.wait()

    return pl.pallas_call(
        body, name=name,
        in_specs=[_ANY] * (n + len(extra)), out_specs=[_ANY] * n,
        out_shape=[jax.ShapeDtypeStruct((g.shape[0], g.shape[1] // 2, g.shape[2]), g.dtype) for g in gs],
        scratch_shapes=[pltpu.SemaphoreType.DMA((n,)), pltpu.SemaphoreType.DMA((n,))],
    )(*gs, *extra)


def _join_halves(fs, name, after=None):
    n = len(fs)
    extra = [] if after is None else [after]

    def body(*refs):
        outs = refs[n + len(extra):2 * n + len(extra)]
        send_sems, recv_sems = refs[2 * n + len(extra):]
        x, y, c, _ = _place()

        def copy(w, hc):
            hr = fs[w].shape[0] // 2
            rows = outs[w].at[pl.ds(hc * hr, hr)]
            return pltpu.make_async_remote_copy(
                src_ref=rows, dst_ref=rows, send_sem=send_sems.at[w], recv_sem=recv_sems.at[w],
                device_id=(x, y, 1 - c), device_id_type=MESH)

        cps = [copy(w, c) for w in range(n)]
        for cp in cps:
            cp.start()
        for w in range(n):
            copy(w, 1 - c).wait_recv()
        for cp in cps:
            cp.wait_send()

    return pl.pallas_call(
        body, name=name,
        in_specs=[_ANY] * (n + len(extra)), out_specs=[_ANY] * n,
        out_shape=[jax.ShapeDtypeStruct(f.shape, f.dtype) for f in fs],
        input_output_aliases={w: w for w in range(n)},
        scratch_shapes=[pltpu.SemaphoreType.DMA((n,)), pltpu.SemaphoreType.DMA((n,))],
    )(*fs, *extra)


_HBM_SPEC = pl.BlockSpec(memory_space=pltpu.HBM)
_SEM_SPEC = pl.BlockSpec(memory_space=pltpu.SEMAPHORE)
_VMEM_SPEC = pl.BlockSpec(memory_space=pltpu.VMEM)
_EFFECT = pltpu.SideEffectType.DATAFLOW_SIDE_EFFECTING
_TOKEN = jax.ShapeDtypeStruct((8, LANES), F32)


def _hbm(a):
    return pltpu.with_memory_space_constraint(a, pltpu.HBM)


def _gather_copies(bufs, refs, send_sems, recv_sems, forward):
    x, y, c, chips = _place()
    k = 2 * x + y
    out, arrive = [], []
    for w, ref in enumerate(refs):
        hr = bufs[w].shape[1] // 2
        for j, (cx, cy) in enumerate(chips):
            kj = 2 * cx + cy
            slot_out, slot_in, half_in = (kj, kj, 1 - c) if forward else (k, kj, c)
            to = (x, y, 1 - c) if forward else (cx, cy, c)
            src = ref.at[slot_out, pl.ds(c * hr, hr)]
            land = ref.at[slot_in, pl.ds(half_in * hr, hr)]
            out.append(pltpu.make_async_remote_copy(
                src_ref=src, dst_ref=src, send_sem=send_sems.at[3 * w + j], recv_sem=recv_sems.at[3 * w + j],
                device_id=to, device_id_type=MESH))
            arrive.append(pltpu.make_async_remote_copy(
                src_ref=land, dst_ref=land, send_sem=send_sems.at[3 * w + j], recv_sem=recv_sems.at[3 * w + j],
                device_id=to, device_id_type=MESH))
    return out, arrive


def _gather_start(bufs, forward, name, after=None):
    n = len(bufs)
    extra = [] if after is None else [after]

    def body(*refs):
        ins = refs[:n]
        send_sems, recv_sems = refs[n + len(extra)], refs[n + len(extra) + 1]
        token = refs[-1]
        out, _ = _gather_copies(bufs, ins, send_sems, recv_sems, forward)
        for cp in out:
            cp.start()
        token[...] = jnp.zeros_like(token)

    res = pl.pallas_call(
        body, name=name,
        out_shape=(pltpu.SemaphoreType.DMA((3 * n,)), pltpu.SemaphoreType.DMA((3 * n,)))
        + tuple(pltpu.HBM(b.shape, b.dtype) for b in bufs) + (_TOKEN,),
        in_specs=(_HBM_SPEC,) * n + (_ANY,) * len(extra),
        out_specs=(_SEM_SPEC, _SEM_SPEC) + (_HBM_SPEC,) * n + (_VMEM_SPEC,),
        input_output_aliases={w: 2 + w for w in range(n)},
        compiler_params=pltpu.CompilerParams(has_side_effects=_EFFECT),
    )(*[_hbm(b) for b in bufs], *extra)
    return res[0], res[1], list(res[2:2 + n]), res[-1]


def _gather_wait(bufs, send_sems, recv_sems, after, forward, name):
    n = len(bufs)

    def body(*refs):
        ins = refs[:n]
        send_sems, recv_sems = refs[n], refs[n + 1]
        out, arrive = _gather_copies(bufs, ins, send_sems, recv_sems, forward)
        for cp in out:
            cp.wait_send()
        for cp in arrive:
            cp.wait_recv()

    res = pl.pallas_call(
        body, name=name,
        out_shape=tuple(pltpu.HBM(b.shape, b.dtype) for b in bufs),
        in_specs=(_HBM_SPEC,) * n + (_SEM_SPEC, _SEM_SPEC, _ANY), out_specs=(_HBM_SPEC,) * n,
        input_output_aliases={w: w for w in range(n)},
        compiler_params=pltpu.CompilerParams(has_side_effects=_EFFECT),
    )(*bufs, send_sems, recv_sems, after)
    return list(res)


def _scatter_copies(t_ref, land_ref, send_sems, recv_sems):
    x, y, c, chips = _place()
    k = 2 * x + y
    out, arrive = [], []
    for j, (cx, cy) in enumerate(chips):
        kj = 2 * cx + cy
        out.append(pltpu.make_async_remote_copy(
            src_ref=t_ref.at[kj], dst_ref=land_ref.at[k], send_sem=send_sems.at[j], recv_sem=recv_sems.at[j],
            device_id=(cx, cy, c), device_id_type=MESH))
        arrive.append(pltpu.make_async_remote_copy(
            src_ref=t_ref.at[kj], dst_ref=land_ref.at[kj], send_sem=send_sems.at[j], recv_sem=recv_sems.at[j],
            device_id=(cx, cy, c), device_id_type=MESH))
    return out, arrive


def _scatter_start(t, name):
    def body(t_ref, land_ref, send_sems, recv_sems, t_thru, land_thru, token):
        out, _ = _scatter_copies(t_ref, land_ref, send_sems, recv_sems)
        for cp in out:
            cp.start()
        token[...] = jnp.zeros_like(token)

    return pl.pallas_call(
        body, name=name,
        out_shape=(pltpu.SemaphoreType.DMA((3,)), pltpu.SemaphoreType.DMA((3,)),
                   pltpu.HBM(t.shape, t.dtype), pltpu.HBM(t.shape, t.dtype), _TOKEN),
        in_specs=(_HBM_SPEC, _HBM_SPEC), out_specs=(_SEM_SPEC, _SEM_SPEC, _HBM_SPEC, _HBM_SPEC, _VMEM_SPEC),
        input_output_aliases={0: 2, 1: 3},
        compiler_params=pltpu.CompilerParams(has_side_effects=_EFFECT),
    )(_hbm(t), _hbm(lax.empty(t.shape, t.dtype)))


def _scatter_wait(send_sems, recv_sems, t_thru, land_thru, after, name):
    def body(t_ref, land_ref, send_sems, recv_sems, after_ref, t_out, land_out):
        out, arrive = _scatter_copies(t_ref, land_ref, send_sems, recv_sems)
        for cp in out:
            cp.wait_send()
        for cp in arrive:
            cp.wait_recv()

    return pl.pallas_call(
        body, name=name,
        out_shape=(pltpu.HBM(t_thru.shape, t_thru.dtype), pltpu.HBM(land_thru.shape, land_thru.dtype)),
        in_specs=(_HBM_SPEC, _HBM_SPEC, _SEM_SPEC, _SEM_SPEC, _ANY), out_specs=(_HBM_SPEC, _HBM_SPEC),
        input_output_aliases={0: 0, 1: 1},
        compiler_params=pltpu.CompilerParams(has_side_effects=_EFFECT),
    )(t_thru, land_thru, send_sems, recv_sems, after)


def _all_gather_small(v, reduce, name):
    r, l = v.shape

    def body(v_ref, o_ref, *rest):
        if reduce:
            buf, send_sems, recv_sems = rest
        else:
            buf = o_ref
            send_sems, recv_sems = rest
        x, y, c, _ = _place()
        me = 4 * x + 2 * y + c
        buf[me] = v_ref[...]
        cps = []
        for d in range(1, N_DEV):
            peer = (x if d & 4 == 0 else 1 - x, y if d & 2 == 0 else 1 - y, c if d & 1 == 0 else 1 - c)
            cp = pltpu.make_async_remote_copy(
                src_ref=v_ref, dst_ref=buf.at[me], send_sem=send_sems.at[d - 1], recv_sem=recv_sems.at[d - 1],
                device_id=peer, device_id_type=MESH)
            cp.start()
            cps.append((cp, peer))
        for d, (cp, (px, py, pc)) in enumerate(cps, start=1):
            pltpu.make_async_remote_copy(
                src_ref=v_ref, dst_ref=buf.at[4 * px + 2 * py + pc], send_sem=send_sems.at[d - 1],
                recv_sem=recv_sems.at[d - 1], device_id=(px, py, pc), device_id_type=MESH).wait_recv()
        for cp, _ in cps:
            cp.wait_send()
        if reduce:
            acc = buf[0]
            for i in range(1, N_DEV):
                acc = acc + buf[i]
            o_ref[...] = acc

    vm = pl.BlockSpec(memory_space=pltpu.VMEM)
    out_shape = jax.ShapeDtypeStruct((r, l) if reduce else (N_DEV, r, l), F32)
    scratch = ([pltpu.VMEM((N_DEV, r, l), F32)] if reduce else []) + [
        pltpu.SemaphoreType.DMA((N_DEV - 1,)), pltpu.SemaphoreType.DMA((N_DEV - 1,))]
    return pl.pallas_call(
        body, name=name, in_specs=[vm], out_specs=vm, out_shape=out_shape, scratch_shapes=scratch,
    )(v)


_BLOCK_BYTES = 3 * 512 * 1024


def _rows_per_block(r, cn, itemsize=4):
    best = 8
    for t in range(8, r + 1, 8):
        if r % t == 0 and t * cn * itemsize <= _BLOCK_BYTES:
            best = t
    return best


def _add_sibling_half(g4, recv, cvec, name):
    ns, r, cn = g4.shape
    hr = r // 2
    tr = _rows_per_block(hr, cn)
    nrb = hr // tr

    def body(c_ref, a_ref, b_ref, o_ref):
        o_ref[...] = (a_ref[...].astype(F32) + b_ref[...].astype(F32)).astype(o_ref.dtype)

    grid_spec = pltpu.PrefetchScalarGridSpec(
        num_scalar_prefetch=1, grid=(ns, nrb),
        in_specs=[pl.BlockSpec((None, tr, cn), lambda j, i, c: (j, c[0] * nrb + i, 0)),
                  pl.BlockSpec((None, tr, cn), lambda j, i, c: (j, i, 0))],
        out_specs=pl.BlockSpec((None, tr, cn), lambda j, i, c: (j, i, 0)))
    return pl.pallas_call(
        body, name=name, grid_spec=grid_spec, out_shape=jax.ShapeDtypeStruct((ns, hr, cn), BF16),
        compiler_params=_cparams(("parallel", "parallel")),
    )(cvec, g4, recv)


def _sum_chips(r4, t4, kvec, cvec, name):
    ns, hr, cn = r4.shape
    tr = _rows_per_block(hr, cn)
    nrb = hr // tr

    def body(k_ref, c_ref, r_ref, t_ref, o_ref):
        acc = t_ref[...].astype(F32)
        for dlt in range(1, ns):
            acc = acc + r_ref[(k_ref[0] + dlt) % ns].astype(F32)
        o_ref[...] = acc

    grid_spec = pltpu.PrefetchScalarGridSpec(
        num_scalar_prefetch=2, grid=(nrb,),
        in_specs=[pl.BlockSpec((ns, tr, cn), lambda i, k, c: (0, i, 0)),
                  pl.BlockSpec((None, tr, cn), lambda i, k, c: (k[0], i, 0))],
        out_specs=pl.BlockSpec((tr, cn), lambda i, k, c: (c[0] * nrb + i, 0)))
    return pl.pallas_call(
        body, name=name, grid_spec=grid_spec, out_shape=jax.ShapeDtypeStruct((2 * hr, cn), F32),
        compiler_params=_cparams(("parallel",)),
    )(kvec, cvec, r4, t4)


def _adamw(w, g, m, v, name):
    r, cn = w.shape
    tr = _rows_per_block(r, cn)
    c1 = 1.0 - ADAM_B1 ** ADAM_STEP
    c2 = 1.0 - ADAM_B2 ** ADAM_STEP

    def body(w_ref, g_ref, m_ref, v_ref, go_ref, d_ref, mo_ref, vo_ref):
        gv = g_ref[...]
        mn = ADAM_B1 * m_ref[...] + (1.0 - ADAM_B1) * gv
        vn = ADAM_B2 * v_ref[...] + (1.0 - ADAM_B2) * (gv * gv)
        go_ref[...] = gv
        mo_ref[...] = mn
        vo_ref[...] = vn
        d_ref[...] = -ADAM_LR * ((mn / c1) / (jnp.sqrt(vn / c2) + ADAM_EPS) + ADAM_WD * w_ref[...])

    spec = pl.BlockSpec((tr, cn), lambda i: (i, 0))
    return pl.pallas_call(
        body, name=name, grid=(r // tr,), in_specs=[spec] * 4, out_specs=[spec] * 4,
        out_shape=[jax.ShapeDtypeStruct((r, cn), F32)] * 4,
        compiler_params=_cparams(("parallel",)),
    )(w, g, m, v)


def _pack(arrs):
    flat = jnp.concatenate([a.reshape(-1).astype(F32) for a in arrs])
    n = flat.shape[0]
    tot = -(-n // (8 * LANES)) * (8 * LANES)
    return jnp.pad(flat, (0, tot - n)).reshape(tot // LANES, LANES)


def _unpack(packed, shapes):
    flat = packed.reshape(-1)
    out, off = [], 0
    for shp in shapes:
        sz = int(np.prod(shp))
        out.append(flat[off:off + sz].reshape(shp))
        off += sz
    return out


class _LocalExchange:
    def __init__(self, ws4, wos4):
        self.ssm = [ws4, wos4]
        self.grads = {}

    def ssm_gather_start(self):
        return None

    def ssm_gather_mid(self, after):
        return None

    def ssm_gather_end(self, after):
        return self.ssm

    def grad_ready(self, name, g4, after=None):
        self.grads[name] = g4
        return None

    def grad_sync(self, name, after):
        pass

    def small_grads(self, small_full):
        self.small = small_full
        return None


class _Exchange:
    def __init__(self, kvec, cvec, ssm_bufs, after):
        self.kvec, self.cvec, self.bufs, self.after = kvec, cvec, ssm_bufs, after
        self.pending, self.summed, self.last_token = {}, {}, None

    def ssm_gather_start(self):
        self.sems = _gather_start(self.bufs, False, "ssm_gather_ici_start", self.after)
        self.bufs = self.sems[2]
        return self.sems[3]

    def ssm_gather_mid(self, after):
        bufs = _gather_wait(self.bufs, self.sems[0], self.sems[1], after, False, "ssm_gather_ici_wait")
        self.sems = _gather_start(bufs, True, "ssm_gather_fwd_start")
        self.bufs = self.sems[2]
        return self.sems[3]

    def ssm_gather_end(self, after):
        return _gather_wait(self.bufs, self.sems[0], self.sems[1], after, True, "ssm_gather_fwd_wait")

    def small_grads(self, small_full):
        packed = _all_gather_small(_pack(small_full), True, "reduce_small_grads")
        self.small = _unpack(packed, [t.shape for t in small_full])
        return packed

    def grad_ready(self, name, g4, after=None):
        recv = _swap_halves([g4], "grads_to_sibling_" + name, after)[0]
        t = _add_sibling_half(g4, recv, self.cvec, "add_sibling_" + name)
        send_sems, recv_sems, t_thru, land, token = _scatter_start(t, "scatter_start_" + name)
        self.pending[name] = (send_sems, recv_sems, t_thru, land)
        self.last_token = token
        return token

    def grad_sync(self, name, after):
        t, land = _scatter_wait(*self.pending.pop(name), after, "scatter_wait_" + name)
        self.summed[name] = _sum_chips(land, t, self.kvec, self.cvec, "sum_chips_" + name)


def _tie(vec, token):
    return vec if token is None else vec + token[0:1, 0:1].reshape((1,) * vec.ndim).astype(vec.dtype)


def _local_step(x2, tgt, wa4, woa4, ex, conv_w_f, conv_b_f, norm_w_f, rel_bias, dt_bias, a_log, d_skip,
                ln_g, ln_b):
    s, d = x2.shape
    d_attn = woa4.shape[1]
    hpg = d_attn // HEAD_DIM
    d_inner = norm_w_f.shape[1]
    ng = d_inner // SSM_GROUP_WIDTH
    n_heads = dt_bias.shape[1]
    conv_dim = conv_w_f.shape[1]
    assert n_heads == ng * HEADS_PER_SSM_GROUP and conv_dim == d_inner + 2 * ng * D_STATE
    assert wa4.shape[2] * N_CHIPS == 10 * d_attn

    xb = _cast_bf16(x2, "cast_x")
    tok = ex.ssm_gather_start()
    buckets = _bucket_tiles()
    bias = _bias_expand(rel_bias, buckets, hpg)
    pgs, og, lg = [], [], []
    for g, (_, dil) in enumerate(ATTN_PATTERNS):
        pg = _mm_nn_sharded(xb, wa4, BF16, f"mm_in_attn_g{g}", after=tok,
                            col_off=3 * g * d_attn, n=3 * d_attn, classes=dil)
        o_, l_ = _attn_fwd(pg, bias, g, dil, hpg)
        pgs.append(pg)
        og.append(o_)
        lg.append(l_)
    gate = _mm_nn_sharded(xb, wa4, BF16, "mm_in_attn_gate", col_off=9 * d_attn, n=d_attn)
    o, lse, yat = _attn_combine(og, lg, gate, hpg)
    h0 = _mm_nn_sharded(yat, woa4, F32, "mm_out_attn", after=ex.ssm_gather_mid(yat))
    g0, b0, g1, b1 = ln_g[0:1], ln_b[0:1], ln_g[1:2], ln_b[1:2]
    xhat0, rstd0, x1b = _ln_fwd(x2, h0, g0, b0, "ln0_fwd")

    wst4, wos4 = ex.ssm_gather_end(x1b)
    wst = wst4.reshape(N_CHIPS * wst4.shape[1], d)
    nzx = d_inner + conv_dim
    wos = wos4.reshape(d_inner, d)
    pzx = _mm_nt(x1b, wst, BF16, "mm_in_ssm", n=nzx)
    dt_raw = _mm_nt(x1b, wst, F32, "mm_in_dt", n=n_heads, b_row_off=nzx)

    def pad_heads(t):
        t = t.reshape(t.shape[0], ng, HEADS_PER_SSM_GROUP).transpose(1, 0, 2)
        return jnp.pad(t, ((0, 0), (0, 0), (0, LANES - HEADS_PER_SSM_GROUP)))

    def unpad_heads(t):
        return t[:, :, :HEADS_PER_SSM_GROUP].transpose(1, 0, 2).reshape(t.shape[1], n_heads)

    dtp = pad_heads(dt_raw)
    alog_p, dtb_p = pad_heads(a_log), pad_heads(dt_bias)
    dsk_e = jnp.repeat(d_skip.reshape(ng, 1, HEADS_PER_SSM_GROUP), SSM_HEAD_DIM, axis=2)
    e = _expand_matrix()
    xbc = _conv_fwd(pzx, conv_w_f, conv_b_f, d_inner)
    y_ssd, states = _ssd_fwd(xbc, dtp, alog_p, dtb_p, dsk_e, e, d_inner)
    y3 = _gate_norm_fwd(y_ssd, pzx, norm_w_f)
    h1 = _mm_nn(y3, wos, F32, "mm_out_ssm")
    xhat1, rstd1, dy2, row_sq = _ln_fwd(xhat0, h1, g1, b1, "ln1_fwd_loss", affine_in=(g0, b0), target=tgt)
    loss_local = 0.5 * jnp.sum(row_sq) / d

    du1, du1b, dg1, db1 = _ln_bwd(dy2, xhat1, rstd1, g1, "ln1_bwd")
    dy3 = _mm_nt(du1b, wos, BF16, "mm_d_y3")
    g_wos = _mm_tn(y3, du1b, BF16, "mm_g_w_out_ssm").reshape(N_CHIPS, d_inner // N_CHIPS, d)
    norm_w_t = _tie(norm_w_f, ex.grad_ready("w_out_ssm", g_wos))
    dy_ssd, dz, d_nw = _gate_norm_bwd(dy3, y_ssd, pzx, norm_w_t)
    dxs, dbm, dcm, ddtp, d_alog, d_dtb, d_dsk = _ssd_bwd(xbc, dtp, alog_p, dtb_p, dsk_e, e, states, dy_ssd, d_inner)
    dpre, d_cw, d_cb = _conv_bwd_a(pzx, jnp.concatenate([dxs, dbm, dcm], axis=1), conv_w_f, conv_b_f, d_inner)
    dpzx = _conv_bwd_b(dpre, conv_w_f, dz, d_inner)
    ddt_raw = unpad_heads(ddtp)
    t1 = _mm_nn(ddt_raw, wst, F32, "mm_d_x1_dt", b_row_off=nzx, add=du1, add_scale=DEEPNORM_ALPHA)
    dx1 = _mm_nn(dpzx, wst, F32, "mm_d_x1", add=t1)
    ex.grad_sync("w_out_ssm", dx1)
    g_wst = _mm_tn(dpzx, x1b, BF16, "mm_g_w_in_ssm", out_rows=wst.shape[0])
    g_wst = _mm_tn(ddt_raw, x1b, BF16, "mm_g_w_dt", out_rows=wst.shape[0], out_row_off=nzx, into=g_wst)
    g0_t = _tie(g0, ex.grad_ready("w_in_ssm", g_wst.reshape(wst4.shape)))

    du0, du0b, dg0, db0 = _ln_bwd(dx1, xhat0, rstd0, g0_t, "ln0_bwd")
    dyat = _mm_nt_sharded_k(du0b, woa4, BF16, "mm_d_yat")
    g_woa = _mm_tn(yat, du0b, BF16, "mm_g_w_out_attn", shard_cols=d // N_CHIPS)
    tok_woa = ex.grad_ready("w_out_attn", g_woa)
    do, delta, dgate = _attn_pre_bwd(dyat, o, gate, hpg)
    pieces, dbt = [], []
    for g, (_, dil) in enumerate(ATTN_PATTERNS):
        dq, dk, dv, db_ = _attn_bwd(pgs[g], bias, do, lse, delta, g, dil, hpg)
        pieces += [dq, dk, dv]
        dbt.append(db_)
    dpa = jnp.concatenate(pieces + [dgate], axis=1)
    grad_x = _mm_nt_sharded_k(dpa, wa4, F32, "mm_d_x0", add=du0, add_scale=DEEPNORM_ALPHA, after=tok_woa)
    ex.grad_sync("w_in_ssm", grad_x)
    grad_x = grad_x[None]
    g_wa = _mm_tn(xb, dpa, BF16, "mm_g_w_in_attn", shard_cols=wa4.shape[2])
    ex.grad_sync("w_out_attn", g_wa)
    d_rel = _bias_reduce(jnp.stack(dbt), buckets, hpg)[:, :, 0].T
    d_dsk_h = d_dsk.reshape(n_heads, SSM_HEAD_DIM).sum(axis=1)
    small_full = [d_rel, d_cw, d_cb, unpad_heads(d_dtb), unpad_heads(d_alog), d_dsk_h[None], d_nw,
                  jnp.concatenate([dg0, dg1], axis=0), jnp.concatenate([db0, db1], axis=0)]
    ex.grad_ready("w_in_attn", g_wa, after=ex.small_grads(small_full))
    return loss_local, grad_x


def kernel(x, w_in_attn, w_out_attn, rel_bias, w_in_ssm, conv_w, conv_b, dt_bias, a_log, d_skip, ssm_norm_w, w_out_ssm, ln_g, ln_b, loss_target, m_w_in_attn, m_w_out_attn, m_rel_bias, m_w_in_ssm, m_conv_w, m_conv_b, m_dt_bias, m_a_log, m_d_skip, m_ssm_norm_w, m_w_out_ssm, m_ln_g, m_ln_b, v_w_in_attn, v_w_out_attn, v_rel_bias, v_w_in_ssm, v_conv_w, v_conv_b, v_dt_bias, v_a_log, v_d_skip, v_ssm_norm_w, v_w_out_ssm, v_ln_g, v_ln_b):
    xi, yi, ci = lax.axis_index("x"), lax.axis_index("y"), lax.axis_index("c")
    chip = 2 * xi + yi
    cvec = jnp.reshape(ci, (1,)).astype(jnp.int32)
    kvec = jnp.reshape(chip, (1,)).astype(jnp.int32)

    wa4, woa4 = _gather_weights([
        _cast_to_slot(w_in_attn[0], kvec, "cast_w_in_attn"), _cast_to_slot(w_out_attn[0], kvec, "cast_w_out_attn")])
    ex = _Exchange(kvec, cvec, [_cast_to_slot(w_in_ssm[0].T, kvec, "cast_w_in_ssm"),
                                _cast_to_slot(w_out_ssm[0], kvec, "cast_w_out_ssm")], after=woa4)
    cw_l, cb_l, nw_l = conv_w[0], conv_b[0], ssm_norm_w[0]
    vec_shapes = [cw_l.shape, cb_l.shape, nw_l.shape]
    vec_all = _all_gather_small(_pack([cw_l, cb_l, nw_l]), False, "gather_vectors")
    parts = [_unpack(vec_all[2 * j], vec_shapes) for j in range(N_CHIPS)]
    conv_w_f = jnp.concatenate([p[0] for p in parts], axis=1)
    conv_b_f = jnp.concatenate([p[1] for p in parts], axis=0)[None]
    norm_w_f = jnp.concatenate([p[2] for p in parts], axis=0)[None]

    loss_local, grad_x = _local_step(
        x[0], loss_target[0], wa4, woa4, ex, conv_w_f, conv_b_f, norm_w_f, rel_bias, dt_bias, a_log,
        d_skip, ln_g, ln_b)
    loss = lax.psum(loss_local, ("x", "y", "c"))

    big_w = dict(w_in_attn=(w_in_attn, m_w_in_attn, v_w_in_attn), w_out_attn=(w_out_attn, m_w_out_attn, v_w_out_attn),
                 w_in_ssm=(w_in_ssm, m_w_in_ssm, v_w_in_ssm), w_out_ssm=(w_out_ssm, m_w_out_ssm, v_w_out_ssm))
    big = {}

    def finish(names, join_name, after):
        last = None
        for nm, gf in zip(names, _join_halves([ex.summed[nm] for nm in names], join_name, after)):
            flip = (lambda t: t.T) if nm == "w_in_ssm" else (lambda t: t)
            w_, m_, v_ = (flip(t[0]) for t in big_w[nm])
            res = _adamw(w_, gf, m_, v_, "adamw_" + nm)
            big[nm] = [flip(t)[None] for t in res]
            last = res[3]
        return last

    last = finish(["w_out_ssm", "w_in_ssm", "w_out_attn"], "grads_join_halves_a", ex.last_token)
    ex.grad_sync("w_in_attn", last)
    finish(["w_in_attn"], "grads_join_halves_b", None)

    s_rel, s_cw, s_cb, s_dtb, s_alog, s_dsk, s_nw, s_lng, s_lnb = ex.small
    cwc, nwc = conv_w.shape[2], ssm_norm_w.shape[1]
    s_cw = lax.dynamic_slice_in_dim(s_cw, chip * cwc, cwc, axis=1)[None]
    s_cb = lax.dynamic_slice_in_dim(s_cb, chip * cwc, cwc, axis=1)
    s_nw = lax.dynamic_slice_in_dim(s_nw, chip * nwc, nwc, axis=1)
    small_names = ["rel_bias", "conv_w", "conv_b", "dt_bias", "a_log", "d_skip", "ssm_norm_w", "ln_g", "ln_b"]
    small_g = [s_rel, s_cw, s_cb, s_dtb, s_alog, s_dsk, s_nw, s_lng, s_lnb]
    small_w = [rel_bias, conv_w, conv_b, dt_bias, a_log, d_skip, ssm_norm_w, ln_g, ln_b]
    small_m = [m_rel_bias, m_conv_w, m_conv_b, m_dt_bias, m_a_log, m_d_skip, m_ssm_norm_w, m_ln_g, m_ln_b]
    small_v = [v_rel_bias, v_conv_w, v_conv_b, v_dt_bias, v_a_log, v_d_skip, v_ssm_norm_w, v_ln_g, v_ln_b]
    shapes = [t.shape for t in small_w]
    res = _adamw(_pack(small_w), _pack(small_g), _pack(small_m), _pack(small_v), "adamw_small")
    small = {nm: [] for nm in small_names}
    for packed in res:
        for nm, t in zip(small_names, _unpack(packed, shapes)):
            small[nm].append(t)

    order = ["w_in_attn", "w_out_attn", "rel_bias", "w_in_ssm", "conv_w", "conv_b", "dt_bias", "a_log",
             "d_skip", "ssm_norm_w", "w_out_ssm", "ln_g", "ln_b"]
    table = {**big, **small}
    outs = [loss, grad_x]
    for kind in range(4):
        outs += [table[nm][kind] for nm in order]
    return tuple(outs)
```

```python
import functools
import math

import numpy as np
import jax
import jax.numpy as jnp
from jax import lax
from jax.experimental import pallas as pl
from jax.experimental.pallas import tpu as pltpu

F32 = jnp.float32
BF16 = jnp.bfloat16
MESH = pl.DeviceIdType.MESH

ATTN_PATTERNS = ((128, 1), (512, 4), (2048, 16))
N_GROUPS_ATTN = 3
HEAD_DIM = 128
ATTN_BLOCK = 128
NUM_BUCKETS = 32
MAX_DISTANCE = 2048
SSM_HEAD_DIM = 64
HEADS_PER_SSM_GROUP = 16
SSM_GROUP_WIDTH = HEADS_PER_SSM_GROUP * SSM_HEAD_DIM
D_STATE = 128
CONV_WIDTH = 4
CHUNK = 128
DEPTH = 2
DEEPNORM_ALPHA = (2 * DEPTH) ** 0.25
LN_EPS = 1e-5
RMS_EPS = 1e-5
NEG_INF = -1e30
ADAM_LR = 0.001
ADAM_B1 = 0.9
ADAM_B2 = 0.999
ADAM_EPS = 1e-08
ADAM_WD = 0.01
ADAM_STEP = 10

N_CHIPS = 4
N_DEV = 8

VMEM_LIMIT_V7X = 56 * 1024 * 1024
LANES = 128


def _cparams(sem=None):
    return pltpu.CompilerParams(dimension_semantics=sem, vmem_limit_bytes=VMEM_LIMIT_V7X)


def _sigmoid(x):
    return 1.0 / (1.0 + jnp.exp(-x))


def _dot(a, b):
    return jnp.dot(a, b, preferred_element_type=F32)


def _dot_nt(a, b):
    return lax.dot_general(a, b, (((1,), (1,)), ((), ())), preferred_element_type=F32)


def _dot_tn(a, b):
    return lax.dot_general(a, b, (((0,), (0,)), ((), ())), preferred_element_type=F32)


def _split2(x):
    hi = x.astype(BF16)
    lo = (x - hi.astype(F32)).astype(BF16)
    return hi, lo


def _split3(x):
    hi = x.astype(BF16)
    r = x - hi.astype(F32)
    mid = r.astype(BF16)
    lo = (r - mid.astype(F32)).astype(BF16)
    return hi, mid, lo


def _matmul(a, b, *, mode, grid, a_spec, b_spec, out_shape, out_spec, tile, name,
            add=None, add_spec=None, add_scale=1.0, after=None, into=None):
    nk = grid[2]
    tm, tn = tile
    dot = {"nn": _dot, "nt": _dot_nt, "tn": _dot_tn}[mode]
    has_add = add is not None
    has_after = after is not None
    has_into = into is not None

    def body(*refs):
        a_ref, b_ref = refs[:2]
        add_ref = refs[2] if has_add else None
        o_ref, acc_ref = refs[-2:]
        k = pl.program_id(2)

        @pl.when(k == 0)
        def _():
            acc_ref[...] = jnp.zeros_like(acc_ref)

        acc_ref[...] += dot(a_ref[...].astype(BF16), b_ref[...].astype(BF16))

        @pl.when(k == nk - 1)
        def _():
            r = acc_ref[...]
            if has_add:
                r = r + add_scale * add_ref[...].astype(F32)
            o_ref[...] = r.astype(o_ref.dtype)

    in_specs = ([a_spec, b_spec] + ([add_spec] if has_add else []) + ([_ANY] if has_after else [])
                + ([_ANY] if has_into else []))
    args = (a, b) + ((add,) if has_add else ()) + ((after,) if has_after else ()) + ((into,) if has_into else ())
    return pl.pallas_call(
        body, name=name, grid=grid, in_specs=in_specs, out_specs=out_spec, out_shape=out_shape,
        input_output_aliases={len(args) - 1: 0} if has_into else {},
        scratch_shapes=[pltpu.VMEM((tm, tn), F32)],
        compiler_params=_cparams(("parallel", "parallel", "arbitrary")),
    )(*args)


def _pick(n, pref):
    for t in pref:
        if n % t == 0:
            return t
    return n


_TILE_PREF = (1024, 512, 256, 128)


def _mm_nn_sharded(a, w4, out_dtype, name, after=None, col_off=0, n=None, classes=1):
    m, k = a.shape
    _, _, nn = w4.shape
    n = N_CHIPS * nn if n is None else n
    tm, tk = _pick(m // classes, _TILE_PREF), _pick(k, _TILE_PREF)
    tn = _pick(math.gcd(math.gcd(nn, n), col_off) if col_off else math.gcd(nn, n), _TILE_PREF)
    npb = nn // tn
    co = col_off // tn
    bpc, kb = m // classes // tm, k // tk
    av = a.reshape(m // classes, classes * k)
    out_shape = jax.ShapeDtypeStruct((m, n), out_dtype)
    if tm < _TILE_PREF[0]:
        return _matmul(
            av, w4, mode="nn", grid=(n // tn, m // tm, 1), tile=(tm, tn), name=name,
            a_spec=pl.BlockSpec((tm, k), lambda j, i, kk: (i % bpc, i // bpc)),
            b_spec=pl.BlockSpec((None, k, tn), lambda j, i, kk: ((j + co) // npb, 0, (j + co) % npb)),
            out_shape=out_shape, out_spec=pl.BlockSpec((tm, tn), lambda j, i, kk: (i, j)), after=after)
    return _matmul(
        av, w4, mode="nn", grid=(m // tm, n // tn, kb), tile=(tm, tn), name=name,
        a_spec=pl.BlockSpec((tm, tk), lambda i, j, kk: (i % bpc, (i // bpc) * kb + kk)),
        b_spec=pl.BlockSpec((None, tk, tn), lambda i, j, kk: ((j + co) // npb, kk, (j + co) % npb)),
        out_shape=out_shape, out_spec=pl.BlockSpec((tm, tn), lambda i, j, kk: (i, j)), after=after)


def _mm_nn(a, b, out_dtype, name, b_row_off=0, add=None, add_scale=1.0):
    m, k = a.shape
    _, n = b.shape
    tm, tk, tn = _pick(m, _TILE_PREF), _pick(k, _TILE_PREF), _pick(n, _TILE_PREF)
    assert b_row_off % tk == 0
    ko = b_row_off // tk
    return _matmul(
        a, b, mode="nn", grid=(m // tm, n // tn, k // tk), tile=(tm, tn), name=name,
        a_spec=pl.BlockSpec((tm, tk), lambda i, j, kk: (i, kk)),
        b_spec=pl.BlockSpec((tk, tn), lambda i, j, kk: (kk + ko, j)),
        out_shape=jax.ShapeDtypeStruct((m, n), out_dtype),
        out_spec=pl.BlockSpec((tm, tn), lambda i, j, kk: (i, j)),
        add=add, add_spec=pl.BlockSpec((tm, tn), lambda i, j, kk: (i, j)), add_scale=add_scale)


def _mm_nt(a, b, out_dtype, name, add=None, add_scale=1.0, n=None, b_row_off=0):
    m, k = a.shape
    n = b.shape[0] if n is None else n
    tm, tk, tn = _pick(m, _TILE_PREF), _pick(k, _TILE_PREF), _pick(n, _TILE_PREF)
    assert b_row_off % tn == 0
    no = b_row_off // tn
    return _matmul(
        a, b, mode="nt", grid=(m // tm, n // tn, k // tk), tile=(tm, tn), name=name,
        a_spec=pl.BlockSpec((tm, tk), lambda i, j, kk: (i, kk)),
        b_spec=pl.BlockSpec((tn, tk), lambda i, j, kk: (j + no, kk)),
        out_shape=jax.ShapeDtypeStruct((m, n), out_dtype),
        out_spec=pl.BlockSpec((tm, tn), lambda i, j, kk: (i, j)),
        add=add, add_spec=pl.BlockSpec((tm, tn), lambda i, j, kk: (i, j)), add_scale=add_scale)


def _mm_nt_sharded_k(a, w4, out_dtype, name, add=None, add_scale=1.0, after=None):
    m, _ = a.shape
    _, n, kn = w4.shape
    tm, tk, tn = _pick(m, _TILE_PREF), _pick(kn, _TILE_PREF), _pick(n, _TILE_PREF)
    kpb = kn // tk
    return _matmul(
        a, w4, mode="nt", grid=(m // tm, n // tn, N_CHIPS * kpb), tile=(tm, tn), name=name,
        a_spec=pl.BlockSpec((tm, tk), lambda i, j, kk: (i, kk)),
        b_spec=pl.BlockSpec((None, tn, tk), lambda i, j, kk: (kk // kpb, j, kk % kpb)),
        out_shape=jax.ShapeDtypeStruct((m, n), out_dtype),
        out_spec=pl.BlockSpec((tm, tn), lambda i, j, kk: (i, j)),
        add=add, add_spec=pl.BlockSpec((tm, tn), lambda i, j, kk: (i, j)), add_scale=add_scale, after=after)


def _mm_tn(a, b, out_dtype, name, shard_cols=None, out_rows=None, out_row_off=0, into=None, after=None):
    k, m = a.shape
    _, n = b.shape
    nn = n if shard_cols is None else shard_cols
    tm, tk, tn = _pick(m, _TILE_PREF), _pick(k, _TILE_PREF), _pick(nn, _TILE_PREF)
    if shard_cols is None:
        assert out_row_off % tm == 0
        ro = out_row_off // tm
        out_shape = jax.ShapeDtypeStruct((m if out_rows is None else out_rows, n), out_dtype)
        out_spec = pl.BlockSpec((tm, tn), lambda i, j, kk: (i + ro, j))
    else:
        npb = nn // tn
        out_shape = jax.ShapeDtypeStruct((n // nn, m, nn), out_dtype)
        out_spec = pl.BlockSpec((None, tm, tn), lambda i, j, kk: (j // npb, i, j % npb))
    return _matmul(
        a, b, mode="tn", grid=(m // tm, n // tn, k // tk), tile=(tm, tn), name=name,
        a_spec=pl.BlockSpec((tk, tm), lambda i, j, kk: (kk, i)),
        b_spec=pl.BlockSpec((tk, tn), lambda i, j, kk: (kk, j)),
        out_shape=out_shape, out_spec=out_spec, into=into, after=after)


def _cast_bf16(x, name, after=None):
    r, c = x.shape
    tr = _pick(r, (512, 256, 128, 8))
    extra = [] if after is None else [after]

    def body(x_ref, *rest):
        rest[-1][...] = x_ref[...].astype(BF16)

    return pl.pallas_call(
        body, name=name, grid=(r // tr,),
        in_specs=[pl.BlockSpec((tr, c), lambda i: (i, 0))] + [_ANY] * len(extra),
        out_specs=pl.BlockSpec((tr, c), lambda i: (i, 0)),
        out_shape=jax.ShapeDtypeStruct((r, c), BF16),
        compiler_params=_cparams(("parallel",)),
    )(x, *extra)


def _bucket_tiles():
    qi = np.arange(ATTN_BLOCK)[:, None]
    ki = np.arange(2 * ATTN_BLOCK)[None, :]
    delta = np.clip(ATTN_BLOCK + qi - ki, 0, None)
    tiles = []
    max_exact = NUM_BUCKETS // 2
    for _, dil in ATTN_PATTERNS:
        dist = (delta * dil).astype(np.int32)
        d_f = np.maximum(dist, 1).astype(np.float32)
        large = max_exact + (np.log(d_f / np.float32(max_exact)) / np.float32(math.log(MAX_DISTANCE / max_exact))
                             * np.float32(NUM_BUCKETS - max_exact)).astype(np.int32)
        large = np.minimum(large, NUM_BUCKETS - 1)
        tiles.append(np.where(dist < max_exact, dist, large).astype(np.int32))
    return jnp.asarray(np.stack(tiles))


def _bias_expand(rel_bias, buckets, hpg):
    def body(tab_ref, bk_ref, o_ref):
        g, h = pl.program_id(0), pl.program_id(1)
        bk = bk_ref[...]
        acc = jnp.zeros((ATTN_BLOCK, 2 * ATTN_BLOCK), F32)
        for b in range(NUM_BUCKETS):
            acc = jnp.where(bk == b, tab_ref[b, g * hpg + h], acc)
        o_ref[...] = acc

    return pl.pallas_call(
        body, name="bias_expand", grid=(N_GROUPS_ATTN, hpg),
        in_specs=[pl.BlockSpec(memory_space=pltpu.SMEM),
                  pl.BlockSpec((None, ATTN_BLOCK, 2 * ATTN_BLOCK), lambda g, h: (g, 0, 0))],
        out_specs=pl.BlockSpec((None, None, ATTN_BLOCK, 2 * ATTN_BLOCK), lambda g, h: (g, h, 0, 0)),
        out_shape=jax.ShapeDtypeStruct((N_GROUPS_ATTN, hpg, ATTN_BLOCK, 2 * ATTN_BLOCK), F32),
        compiler_params=_cparams(("parallel", "parallel")),
    )(rel_bias, buckets)


def _bias_reduce(dtiles, buckets, hpg):
    def body(t_ref, bk_ref, o_ref):
        bk = bk_ref[...]
        t = t_ref[...]
        rows = lax.broadcasted_iota(jnp.int32, (NUM_BUCKETS, LANES), 0)
        acc = jnp.zeros((NUM_BUCKETS, LANES), F32)
        for b in range(NUM_BUCKETS):
            s = jnp.sum(jnp.sum(jnp.where(bk == b, t, 0.0), axis=1, keepdims=True), axis=0, keepdims=True)
            acc = jnp.where(rows == b, s, acc)
        o_ref[...] = acc

    return pl.pallas_call(
        body, name="bias_reduce", grid=(N_GROUPS_ATTN, hpg),
        in_specs=[pl.BlockSpec((None, None, ATTN_BLOCK, 2 * ATTN_BLOCK), lambda g, h: (g, h, 0, 0)),
                  pl.BlockSpec((None, ATTN_BLOCK, 2 * ATTN_BLOCK), lambda g, h: (g, 0, 0))],
        out_specs=pl.BlockSpec((None, NUM_BUCKETS, LANES), lambda g, h: (g * hpg + h, 0, 0)),
        out_shape=jax.ShapeDtypeStruct((N_GROUPS_ATTN * hpg, NUM_BUCKETS, LANES), F32),
        compiler_params=_cparams(("parallel", "parallel")),
    )(dtiles, buckets)


def _attn_valid(n_is_first):
    qi = lax.broadcasted_iota(jnp.int32, (ATTN_BLOCK, 2 * ATTN_BLOCK), 0)
    ki = lax.broadcasted_iota(jnp.int32, (ATTN_BLOCK, 2 * ATTN_BLOCK), 1)
    delta = ATTN_BLOCK + qi - ki
    band = (delta >= 0) & (delta <= ATTN_BLOCK)
    return band & (jnp.logical_not(n_is_first) | (ki >= ATTN_BLOCK))


def _attn_fwd(pg, bias, g, dil, hpg):
    s = pg.shape[0]
    w = hpg * HEAD_DIM
    rows = s // dil
    nb = rows // ATTN_BLOCK
    scale = HEAD_DIM ** -0.5

    def body(q_ref, kc_ref, kp_ref, vc_ref, vp_ref, bias_ref, o_ref, lse_ref):
        valid = _attn_valid(pl.program_id(1) == 0)
        for h in range(hpg):
            sl = slice(h * HEAD_DIM, (h + 1) * HEAD_DIM)
            k2 = jnp.concatenate([kp_ref[:, sl], kc_ref[:, sl]], axis=0)
            v2 = jnp.concatenate([vp_ref[:, sl], vc_ref[:, sl]], axis=0)
            sc = _dot_nt(q_ref[:, sl], k2) * scale + bias_ref[h]
            sc = jnp.where(valid, sc, NEG_INF)
            m = jnp.max(sc, axis=1, keepdims=True)
            p = jnp.exp(sc - m)
            l = jnp.sum(p, axis=1, keepdims=True)
            o_ref[:, sl] = _dot(p.astype(BF16), v2) / l
            lse_ref[:, sl] = jnp.broadcast_to(m + jnp.log(l), (ATTN_BLOCK, HEAD_DIM))

    def col(off):
        return lambda r, n: (r * nb + n, off)

    def colp(off):
        return lambda r, n: (r * nb + jnp.maximum(n - 1, 0), off)

    blk = (ATTN_BLOCK, w)
    tok = pl.BlockSpec(blk, lambda r, n: (n, r))
    o, lse = pl.pallas_call(
        body, name=f"attn_fwd_g{g}", grid=(dil, nb),
        in_specs=[pl.BlockSpec(blk, col(0)), pl.BlockSpec(blk, col(1)), pl.BlockSpec(blk, colp(1)),
                  pl.BlockSpec(blk, col(2)), pl.BlockSpec(blk, colp(2)),
                  pl.BlockSpec((None, hpg, ATTN_BLOCK, 2 * ATTN_BLOCK), lambda r, n: (g, 0, 0, 0))],
        out_specs=[tok, tok],
        out_shape=[jax.ShapeDtypeStruct((rows, dil * w), F32), jax.ShapeDtypeStruct((rows, dil * w), F32)],
        compiler_params=_cparams(("parallel", "parallel")),
    )(pg, pg, pg, pg, pg, bias)
    return o.reshape(s, w), lse.reshape(s, w)


def _attn_combine(os_, lses, gate, hpg):
    s, w = os_[0].shape
    tm = _pick(s, (256, 128))

    def body(o0, o1, o2, l0, l1, l2, gate_ref, o_ref, lse_ref, y_ref):
        a0, a1, a2 = l0[...], l1[...], l2[...]
        m = jnp.maximum(jnp.maximum(a0, a1), a2)
        e0, e1, e2 = jnp.exp(a0 - m), jnp.exp(a1 - m), jnp.exp(a2 - m)
        den = e0 + e1 + e2
        o = (e0 * o0[...] + e1 * o1[...] + e2 * o2[...]) / den
        gate = gate_ref[...].astype(F32)
        o_ref[...] = o.astype(BF16)
        lse_ref[...] = m + jnp.log(den)
        y_ref[...] = (o * (gate * _sigmoid(gate))).astype(BF16)

    spec = pl.BlockSpec((tm, w), lambda i: (i, 0))
    return pl.pallas_call(
        body, name="attn_combine", grid=(s // tm,),
        in_specs=[spec] * 7,
        out_specs=[spec, spec, spec],
        out_shape=[jax.ShapeDtypeStruct((s, w), BF16), jax.ShapeDtypeStruct((s, w), F32),
                   jax.ShapeDtypeStruct((s, w), BF16)],
        compiler_params=_cparams(("parallel",)),
    )(*os_, *lses, gate)


def _attn_pre_bwd(dy, o, gate, hpg):
    s, w = dy.shape
    tm = _pick(s, (256, 128))

    def body(dy_ref, o_ref, gate_ref, do_ref, dl_ref, dg_ref):
        gate = gate_ref[...].astype(F32)
        sg = _sigmoid(gate)
        dyv = dy_ref[...].astype(F32)
        ov = o_ref[...].astype(F32)
        do = dyv * (gate * sg)
        do_ref[...] = do.astype(BF16)
        dg_ref[...] = (dyv * ov * (sg * (1.0 + gate * (1.0 - sg)))).astype(BF16)
        prod = do * ov
        for h in range(hpg):
            sl = slice(h * HEAD_DIM, (h + 1) * HEAD_DIM)
            dl_ref[:, sl] = jnp.broadcast_to(jnp.sum(prod[:, sl], axis=1, keepdims=True), (tm, HEAD_DIM))

    spec = pl.BlockSpec((tm, w), lambda i: (i, 0))
    return pl.pallas_call(
        body, name="attn_pre_bwd", grid=(s // tm,),
        in_specs=[spec, spec, spec],
        out_specs=[spec, spec, spec],
        out_shape=[jax.ShapeDtypeStruct((s, w), BF16), jax.ShapeDtypeStruct((s, w), F32),
                   jax.ShapeDtypeStruct((s, w), BF16)],
        compiler_params=_cparams(("parallel",)),
    )(dy, o, gate)


def _attn_bwd(pg, bias, do, lse, delta, g, dil, hpg):
    s = pg.shape[0]
    w = hpg * HEAD_DIM
    rows = s // dil
    nb = rows // ATTN_BLOCK
    dov, lsev, dlv = (t.reshape(rows, dil * w) for t in (do, lse, delta))
    scale = HEAD_DIM ** -0.5

    def body(q_ref, kc_ref, kp_ref, vc_ref, vp_ref, bias_ref, do_ref, lse_ref, dl_ref,
             dq_ref, dk_ref, dv_ref, db_ref, dkc_ref, dvc_ref):
        r, i = pl.program_id(0), pl.program_id(1)
        n = nb - 1 - i
        valid = _attn_valid(n == 0)

        @pl.when((r == 0) & (i == 0))
        def _():
            db_ref[...] = jnp.zeros_like(db_ref)

        @pl.when(i == 0)
        def _():
            dkc_ref[...] = jnp.zeros_like(dkc_ref)
            dvc_ref[...] = jnp.zeros_like(dvc_ref)

        for h in range(hpg):
            sl = slice(h * HEAD_DIM, (h + 1) * HEAD_DIM)
            q = q_ref[:, sl]
            dov_ = do_ref[:, sl]
            k2 = jnp.concatenate([kp_ref[:, sl], kc_ref[:, sl]], axis=0)
            v2 = jnp.concatenate([vp_ref[:, sl], vc_ref[:, sl]], axis=0)
            sc = _dot_nt(q, k2) * scale + bias_ref[h]
            p = jnp.exp(jnp.where(valid, sc - lse_ref[:, sl][:, 0:1], NEG_INF))
            dp = _dot_nt(dov_, v2)
            ds = p * (dp - dl_ref[:, sl][:, 0:1])
            db_ref[h] += ds
            dsb = ds.astype(BF16)
            dq_ref[:, sl] = (_dot(dsb, k2) * scale).astype(BF16)
            dk2 = _dot_tn(dsb, q) * scale
            dv2 = _dot_tn(p.astype(BF16), dov_)
            dk_ref[:, sl] = (dk2[ATTN_BLOCK:] + dkc_ref[:, sl]).astype(BF16)
            dv_ref[:, sl] = (dv2[ATTN_BLOCK:] + dvc_ref[:, sl]).astype(BF16)
            dkc_ref[:, sl] = dk2[:ATTN_BLOCK]
            dvc_ref[:, sl] = dv2[:ATTN_BLOCK]

    def col(off):
        return lambda r, i: (r * nb + nb - 1 - i, off)

    def colp(off):
        return lambda r, i: (r * nb + jnp.maximum(nb - 2 - i, 0), off)

    blk = (ATTN_BLOCK, w)
    tok = pl.BlockSpec(blk, lambda r, i: (nb - 1 - i, r))
    dq, dk, dv, db = pl.pallas_call(
        body, name=f"attn_bwd_g{g}", grid=(dil, nb),
        in_specs=[pl.BlockSpec(blk, col(0)), pl.BlockSpec(blk, col(1)), pl.BlockSpec(blk, colp(1)),
                  pl.BlockSpec(blk, col(2)), pl.BlockSpec(blk, colp(2)),
                  pl.BlockSpec((None, hpg, ATTN_BLOCK, 2 * ATTN_BLOCK), lambda r, i: (g, 0, 0, 0)),
                  tok, tok, tok],
        out_specs=[tok, tok, tok,
                   pl.BlockSpec((hpg, ATTN_BLOCK, 2 * ATTN_BLOCK), lambda r, i: (0, 0, 0))],
        out_shape=[jax.ShapeDtypeStruct((rows, dil * w), BF16)] * 3
        + [jax.ShapeDtypeStruct((hpg, ATTN_BLOCK, 2 * ATTN_BLOCK), F32)],
        scratch_shapes=[pltpu.VMEM(blk, F32), pltpu.VMEM(blk, F32)],
        compiler_params=_cparams(("arbitrary", "arbitrary")),
    )(pg, pg, pg, pg, pg, bias, dov, lsev, dlv)
    return dq.reshape(s, w), dk.reshape(s, w), dv.reshape(s, w), db


def _ln_fwd(xin, h, gamma, beta, name, affine_in=None, target=None):
    s, d = xin.shape
    tm = _pick(s, (128,))
    has_aff = affine_in is not None
    has_tgt = target is not None

    def body(*refs):
        it = iter(refs)
        x_ref, h_ref, g_ref, b_ref = next(it), next(it), next(it), next(it)
        if has_aff:
            gi_ref, bi_ref = next(it), next(it)
        if has_tgt:
            t_ref = next(it)
        xh_ref, rs_ref = next(it), next(it)
        x = x_ref[...]
        if has_aff:
            x = x * gi_ref[...] + bi_ref[...]
        u = DEEPNORM_ALPHA * x + h_ref[...]
        mu = jnp.mean(u, axis=1, keepdims=True)
        uc = u - mu
        var = jnp.mean(uc * uc, axis=1, keepdims=True)
        rstd = lax.rsqrt(var + LN_EPS)
        xhat = uc * rstd
        xh_ref[...] = xhat
        rs_ref[...] = rstd
        y = xhat * g_ref[...] + b_ref[...]
        if has_tgt:
            dy_ref, l_ref = next(it), next(it)
            e = y - t_ref[...]
            dy_ref[...] = e * (1.0 / d)
            l_ref[...] = jnp.sum(e * e, axis=1, keepdims=True)
        else:
            y_ref = next(it)
            y_ref[...] = y.astype(BF16)

    row = pl.BlockSpec((tm, d), lambda i: (i, 0))
    vec = pl.BlockSpec((1, d), lambda i: (0, 0))
    one = pl.BlockSpec((tm, 1), lambda i: (i, 0))
    in_specs = [row, row, vec, vec] + ([vec, vec] if has_aff else []) + ([row] if has_tgt else [])
    args = [xin, h, gamma, beta] + (list(affine_in) if has_aff else []) + ([target] if has_tgt else [])
    out_specs = [row, one] + ([row, one] if has_tgt else [row])
    out_shape = [jax.ShapeDtypeStruct((s, d), F32), jax.ShapeDtypeStruct((s, 1), F32)]
    out_shape += ([jax.ShapeDtypeStruct((s, d), F32), jax.ShapeDtypeStruct((s, 1), F32)] if has_tgt
                  else [jax.ShapeDtypeStruct((s, d), BF16)])
    return pl.pallas_call(
        body, name=name, grid=(s // tm,), in_specs=in_specs, out_specs=out_specs, out_shape=out_shape,
        compiler_params=_cparams(("parallel",)),
    )(*args)


def _ln_bwd(dy, xhat, rstd, gamma, name):
    s, d = dy.shape
    tm = _pick(s, (128,))

    def body(dy_ref, xh_ref, rs_ref, g_ref, du_ref, dub_ref, dg_ref, db_ref):
        @pl.when(pl.program_id(0) == 0)
        def _():
            dg_ref[...] = jnp.zeros_like(dg_ref)
            db_ref[...] = jnp.zeros_like(db_ref)

        dyv = dy_ref[...]
        xh = xh_ref[...]
        dg_ref[...] += jnp.sum(dyv * xh, axis=0, keepdims=True)
        db_ref[...] += jnp.sum(dyv, axis=0, keepdims=True)
        dxh = dyv * g_ref[...]
        m1 = jnp.mean(dxh, axis=1, keepdims=True)
        m2 = jnp.mean(dxh * xh, axis=1, keepdims=True)
        du = rs_ref[...] * (dxh - m1 - xh * m2)
        du_ref[...] = du
        dub_ref[...] = du.astype(BF16)

    row = pl.BlockSpec((tm, d), lambda i: (i, 0))
    vec = pl.BlockSpec((1, d), lambda i: (0, 0))
    one = pl.BlockSpec((tm, 1), lambda i: (i, 0))
    return pl.pallas_call(
        body, name=name, grid=(s // tm,), in_specs=[row, row, one, vec],
        out_specs=[row, row, vec, vec],
        out_shape=[jax.ShapeDtypeStruct((s, d), F32), jax.ShapeDtypeStruct((s, d), BF16),
                   jax.ShapeDtypeStruct((1, d), F32), jax.ShapeDtypeStruct((1, d), F32)],
        compiler_params=_cparams(("arbitrary",)),
    )(dy, xhat, rstd, gamma)


_HALO = 16


def _conv_taps(ext, tm, w_ref):
    acc = None
    for k in range(CONV_WIDTH):
        lo = _HALO - (CONV_WIDTH - 1) + k
        term = w_ref[k:k + 1, :] * ext[lo:lo + tm, :]
        acc = term if acc is None else acc + term
    return acc


def _conv_fwd(pzx, conv_w, conv_b, d_inner):
    s, _ = pzx.shape
    cd = conv_w.shape[1]
    tm = _pick(s, (512, 256, 128))
    tc = _pick(cd, (1024, 512, 256, 128))
    off = d_inner // tc
    hb = tm // _HALO

    def body(x_ref, p_ref, w_ref, b_ref, o_ref):
        prev = jnp.where(pl.program_id(0) > 0, p_ref[...].astype(F32), 0.0)
        ext = jnp.concatenate([prev, x_ref[...].astype(F32)], axis=0)
        pre = _conv_taps(ext, tm, w_ref) + b_ref[...]
        o_ref[...] = (pre * _sigmoid(pre)).astype(BF16)

    return pl.pallas_call(
        body, name="conv_fwd", grid=(s // tm, cd // tc),
        in_specs=[pl.BlockSpec((tm, tc), lambda i, j: (i, off + j)),
                  pl.BlockSpec((_HALO, tc), lambda i, j: (jnp.maximum(i * hb - 1, 0), off + j)),
                  pl.BlockSpec((CONV_WIDTH, tc), lambda i, j: (0, j)),
                  pl.BlockSpec((1, tc), lambda i, j: (0, j))],
        out_specs=pl.BlockSpec((tm, tc), lambda i, j: (i, j)),
        out_shape=jax.ShapeDtypeStruct((s, cd), BF16),
        compiler_params=_cparams(("parallel", "parallel")),
    )(pzx, pzx, conv_w, conv_b)


def _conv_bwd_a(pzx, dxbc, conv_w, conv_b, d_inner):
    s, _ = pzx.shape
    cd = conv_w.shape[1]
    tm = _pick(s, (512, 256, 128))
    tc = _pick(cd, (1024, 512, 256, 128))
    off = d_inner // tc
    hb = tm // _HALO

    def body(x_ref, p_ref, d_ref, w_ref, b_ref, o_ref, dw_ref, db_ref):
        @pl.when(pl.program_id(1) == 0)
        def _():
            dw_ref[...] = jnp.zeros_like(dw_ref)
            db_ref[...] = jnp.zeros_like(db_ref)

        prev = jnp.where(pl.program_id(1) > 0, p_ref[...].astype(F32), 0.0)
        ext = jnp.concatenate([prev, x_ref[...].astype(F32)], axis=0)
        pre = _conv_taps(ext, tm, w_ref) + b_ref[...]
        sg = _sigmoid(pre)
        dpre = d_ref[...].astype(F32) * (sg * (1.0 + pre * (1.0 - sg)))
        o_ref[...] = dpre
        db_ref[...] += jnp.sum(dpre, axis=0, keepdims=True)
        for k in range(CONV_WIDTH):
            lo = _HALO - (CONV_WIDTH - 1) + k
            dw_ref[k:k + 1, :] += jnp.sum(dpre * ext[lo:lo + tm, :], axis=0, keepdims=True)

    return pl.pallas_call(
        body, name="conv_bwd_a", grid=(cd // tc, s // tm),
        in_specs=[pl.BlockSpec((tm, tc), lambda j, i: (i, off + j)),
                  pl.BlockSpec((_HALO, tc), lambda j, i: (jnp.maximum(i * hb - 1, 0), off + j)),
                  pl.BlockSpec((tm, tc), lambda j, i: (i, j)),
                  pl.BlockSpec((CONV_WIDTH, tc), lambda j, i: (0, j)),
                  pl.BlockSpec((1, tc), lambda j, i: (0, j))],
        out_specs=[pl.BlockSpec((tm, tc), lambda j, i: (i, j)),
                   pl.BlockSpec((CONV_WIDTH, tc), lambda j, i: (0, j)),
                   pl.BlockSpec((1, tc), lambda j, i: (0, j))],
        out_shape=[jax.ShapeDtypeStruct((s, cd), F32), jax.ShapeDtypeStruct((CONV_WIDTH, cd), F32),
                   jax.ShapeDtypeStruct((1, cd), F32)],
        compiler_params=_cparams(("parallel", "arbitrary")),
    )(pzx, pzx, dxbc, conv_w, conv_b)


def _conv_bwd_b(dpre, conv_w, into, col_off):
    s, cd = dpre.shape
    tm = _pick(s, (512, 256, 128))
    tc = _pick(cd, (1024, 512, 256, 128))
    hb = tm // 8
    nrb = s // tm
    assert col_off % tc == 0
    co = col_off // tc

    def body(x_ref, nx_ref, w_ref, into_ref, o_ref):
        nxt = jnp.where(pl.program_id(0) < nrb - 1, nx_ref[...], 0.0)
        ext = jnp.concatenate([x_ref[...], nxt], axis=0)
        acc = None
        for k in range(CONV_WIDTH):
            lo = CONV_WIDTH - 1 - k
            term = w_ref[k:k + 1, :] * ext[lo:lo + tm, :]
            acc = term if acc is None else acc + term
        o_ref[...] = acc.astype(BF16)

    return pl.pallas_call(
        body, name="conv_bwd_b", grid=(nrb, cd // tc),
        in_specs=[pl.BlockSpec((tm, tc), lambda i, j: (i, j)),
                  pl.BlockSpec((8, tc), lambda i, j: (jnp.minimum((i + 1) * hb, s // 8 - 1), j)),
                  pl.BlockSpec((CONV_WIDTH, tc), lambda i, j: (0, j)), _ANY],
        out_specs=pl.BlockSpec((tm, tc), lambda i, j: (i, j + co)),
        out_shape=jax.ShapeDtypeStruct(into.shape, BF16),
        input_output_aliases={3: 0},
        compiler_params=_cparams(("parallel", "parallel")),
    )(dpre, dpre, conv_w, into)


def _expand_matrix():
    e = np.zeros((LANES, SSM_GROUP_WIDTH), np.float32)
    for h in range(HEADS_PER_SSM_GROUP):
        e[h, h * SSM_HEAD_DIM:(h + 1) * SSM_HEAD_DIM] = 1.0
    return jnp.asarray(e, BF16)


def _expand(t, e):
    hi, lo = _split2(t)
    return _dot(hi, e) + _dot(lo, e)


def _segsum(v, e):
    hi, lo = _split2(v)
    return _dot_nt(hi, e) + _dot_nt(lo, e)


def _tri_dot(tri, x):
    hi, mid, lo = _split3(x)
    return _dot(tri, hi) + _dot(tri, mid) + _dot(tri, lo)


def _ssd_common(dtp_ref, a_ref, dtb_ref, x_ref, e):
    li = lax.broadcasted_iota(jnp.int32, (CHUNK, CHUNK), 0)
    si = lax.broadcasted_iota(jnp.int32, (CHUNK, CHUNK), 1)
    causal = li >= si
    tril = causal.astype(BF16)
    raw = dtp_ref[...] + dtb_ref[...]
    dt = jnp.maximum(raw, 0.0) + jnp.log(1.0 + jnp.exp(-jnp.abs(raw)))
    head_lane = lax.broadcasted_iota(jnp.int32, (1, LANES), 1) < HEADS_PER_SSM_GROUP
    a = jnp.where(head_lane, -jnp.exp(a_ref[...]), 0.0)
    a_cum = _tri_dot(tril, dt * a)
    a_cum_t = a_cum.T
    e_a = jnp.exp(a_cum)
    to_end = jnp.exp(a_cum[CHUNK - 1:CHUNK, :] - a_cum)
    x = x_ref[...].astype(F32)
    dt_e = _expand(dt, e)
    return dict(causal=causal, raw=raw, dt=dt, a=a, a_cum=a_cum, a_cum_t=a_cum_t, e_a=e_a,
                to_end=to_end, x=x, dt_e=dt_e, xdt=x * dt_e, e_a_e=_expand(e_a, e),
                to_end_e=_expand(to_end, e))


def _decay(q, h):
    seg = q["a_cum"][:, h:h + 1] - q["a_cum_t"][h:h + 1, :]
    return jnp.exp(jnp.where(q["causal"], seg, -jnp.inf))


def _ssd_specs(ng, d_inner, rev, nc):
    cidx = (lambda i: nc - 1 - i) if rev else (lambda i: i)
    boff = d_inner // D_STATE
    return dict(
        xs=pl.BlockSpec((CHUNK, SSM_GROUP_WIDTH), lambda g, i: (cidx(i), g)),
        b=pl.BlockSpec((CHUNK, D_STATE), lambda g, i: (cidx(i), boff + g)),
        c=pl.BlockSpec((CHUNK, D_STATE), lambda g, i: (cidx(i), boff + ng + g)),
        dtp=pl.BlockSpec((None, CHUNK, LANES), lambda g, i: (g, cidx(i), 0)),
        vec=pl.BlockSpec((None, 1, LANES), lambda g, i: (g, 0, 0)),
        wide=pl.BlockSpec((None, 1, SSM_GROUP_WIDTH), lambda g, i: (g, 0, 0)),
        e=pl.BlockSpec((LANES, SSM_GROUP_WIDTH), lambda g, i: (0, 0)),
        st=pl.BlockSpec((None, None, D_STATE, SSM_GROUP_WIDTH), lambda g, i: (g, cidx(i), 0, 0)),
        tok=pl.BlockSpec((CHUNK, SSM_GROUP_WIDTH), lambda g, i: (cidx(i), g)),
        bc_out=pl.BlockSpec((CHUNK, D_STATE), lambda g, i: (cidx(i), g)),
    )


def _ssd_fwd(xbc, dtp, a_pad, dtb_pad, dsk_e, e, d_inner):
    s = xbc.shape[0]
    ng = d_inner // SSM_GROUP_WIDTH
    nc = s // CHUNK

    def body(x_ref, b_ref, c_ref, dtp_ref, a_ref, dtb_ref, dsk_ref, e_ref, y_ref, st_ref, state):
        lane = lax.broadcasted_iota(jnp.int32, (CHUNK, LANES), 1)
        @pl.when(pl.program_id(1) == 0)
        def _():
            state[...] = jnp.zeros_like(state)

        ev = e_ref[...]
        q = _ssd_common(dtp_ref, a_ref, dtb_ref, x_ref, ev)
        bm, cm = b_ref[...], c_ref[...]
        cb = _dot_nt(cm, bm)
        s0 = state[...]
        st_ref[...] = s0
        y = _dot(cm, s0.astype(BF16)) * q["e_a_e"] + dsk_ref[...] * q["x"]
        xdt = q["xdt"]
        left = lane[:, :] < SSM_HEAD_DIM
        for j in range(HEADS_PER_SSM_GROUP // 2):
            sl = slice(j * LANES, (j + 1) * LANES)
            x2 = xdt[:, sl]
            m0 = (cb * _decay(q, 2 * j)).astype(BF16)
            m1 = (cb * _decay(q, 2 * j + 1)).astype(BF16)
            mcat = jnp.concatenate([m0, m1], axis=1)
            xbd = jnp.concatenate([jnp.where(left, x2, 0.0), jnp.where(left, 0.0, x2)], axis=0).astype(BF16)
            y_ref[:, sl] = (y[:, sl] + _dot(mcat, xbd)).astype(BF16)
        state[...] = s0 * q["e_a_e"][CHUNK - 1:CHUNK, :] + _dot_tn(bm, (q["to_end_e"] * xdt).astype(BF16))

    sp = _ssd_specs(ng, d_inner, False, nc)
    return pl.pallas_call(
        body, name="ssd_fwd", grid=(ng, nc),
        in_specs=[sp["xs"], sp["b"], sp["c"], sp["dtp"], sp["vec"], sp["vec"], sp["wide"], sp["e"]],
        out_specs=[sp["tok"], sp["st"]],
        out_shape=[jax.ShapeDtypeStruct((s, d_inner), BF16),
                   jax.ShapeDtypeStruct((ng, nc, D_STATE, SSM_GROUP_WIDTH), F32)],
        scratch_shapes=[pltpu.VMEM((D_STATE, SSM_GROUP_WIDTH), F32)],
        compiler_params=_cparams(("parallel", "arbitrary")),
    )(xbc, xbc, xbc, dtp, a_pad, dtb_pad, dsk_e, e)


def _ssd_bwd(xbc, dtp, a_pad, dtb_pad, dsk_e, e, states, dy, d_inner):
    s = xbc.shape[0]
    ng = d_inner // SSM_GROUP_WIDTH
    nc = s // CHUNK

    def body(x_ref, b_ref, c_ref, dtp_ref, a_ref, dtb_ref, dsk_ref, e_ref, st_ref, dy_ref,
             dx_ref, db_ref, dc_ref, ddt_ref, da_ref, ddtb_ref, dd_ref, dstate):
        lane = lax.broadcasted_iota(jnp.int32, (CHUNK, LANES), 1)
        sub = lax.broadcasted_iota(jnp.int32, (CHUNK, LANES), 0)
        @pl.when(pl.program_id(1) == 0)
        def _():
            dstate[...] = jnp.zeros_like(dstate)
            da_ref[...] = jnp.zeros_like(da_ref)
            ddtb_ref[...] = jnp.zeros_like(ddtb_ref)
            dd_ref[...] = jnp.zeros_like(dd_ref)

        ev = e_ref[...]
        q = _ssd_common(dtp_ref, a_ref, dtb_ref, x_ref, ev)
        bm, cm = b_ref[...], c_ref[...]
        cb = _dot_nt(cm, bm)
        x, xdt, e_a_e, to_end_e = q["x"], q["xdt"], q["e_a_e"], q["to_end_e"]
        s0 = st_ref[...]
        s0b = s0.astype(BF16)
        ds1 = dstate[...]
        ds1b = ds1.astype(BF16)
        dy = dy_ref[...].astype(F32)
        e_last_e = e_a_e[CHUNK - 1:CHUNK, :]

        dye = dy * e_a_e
        dyeb = dye.astype(BF16)
        cs0 = _dot(cm, s0b)
        dc = _dot_nt(dyeb, s0b)
        dstate[...] = e_last_e * ds1 + _dot_tn(cm, dyeb)
        da_col = _segsum(dye * cs0, ev)

        gmat = _dot(bm, ds1b)
        dxdt = to_end_e * gmat
        dte = _segsum(xdt * gmat, ev) * q["to_end"]
        db = _dot_nt((to_end_e * xdt).astype(BF16), ds1b)
        da_col = da_col - dte
        last_row = (jnp.sum(dte, axis=0, keepdims=True)
                    + q["e_a"][CHUNK - 1:CHUNK, :] * jnp.sum(_segsum(s0 * ds1, ev), axis=0, keepdims=True))

        left = lane < SSM_HEAD_DIM
        dcb = jnp.zeros((CHUNK, CHUNK), F32)
        row_acc = jnp.zeros((CHUNK, LANES), F32)
        for j in range(HEADS_PER_SSM_GROUP // 2):
            sl = slice(j * LANES, (j + 1) * LANES)
            x2 = xdt[:, sl].astype(BF16)
            dy2 = dy[:, sl]
            dyl = jnp.where(left, dy2, 0.0).astype(BF16)
            dyr = jnp.where(left, 0.0, dy2).astype(BF16)
            ms = []
            for hh, dyh in ((0, dyl), (1, dyr)):
                h = 2 * j + hh
                dec = _decay(q, h)
                m = cb * dec
                dm = _dot_nt(dyh, x2)
                dcb = dcb + dm * dec
                dseg = dm * m
                da_col = da_col + jnp.where(lane == h, jnp.sum(dseg, axis=1, keepdims=True), 0.0)
                row_acc = row_acc + jnp.where(sub == h, jnp.sum(dseg, axis=0, keepdims=True), 0.0)
                ms.append(m.astype(BF16))
            mst = jnp.concatenate(ms, axis=0)
            dyst = jnp.concatenate([dyl, dyr], axis=0)
            d2 = dxdt[:, sl] + _dot_tn(mst, dyst)
            dx_ref[:, sl] = (d2 * q["dt_e"][:, sl] + dsk_ref[:, sl] * dy2).astype(BF16)
            dxdt_x = d2 * x[:, sl]
            if j == 0:
                parts = [dxdt_x]
            else:
                parts.append(dxdt_x)
        dcbb = dcb.astype(BF16)
        dc_ref[...] = (dc + _dot(dcbb, bm)).astype(BF16)
        db_ref[...] = (db + _dot_tn(dcbb, cm)).astype(BF16)

        d_a = da_col - row_acc.T + jnp.where(sub == CHUNK - 1, last_row, 0.0)
        triu = (lax.broadcasted_iota(jnp.int32, (CHUNK, CHUNK), 1)
                >= lax.broadcasted_iota(jnp.int32, (CHUNK, CHUNK), 0)).astype(BF16)
        d_dta = _tri_dot(triu, d_a)
        ddt = d_dta * q["a"] + _segsum(jnp.concatenate(parts, axis=1), ev)
        ddt_raw = ddt * _sigmoid(q["raw"])
        ddt_ref[...] = ddt_raw
        da_ref[...] += jnp.sum(d_dta * q["dt"], axis=0, keepdims=True) * q["a"]
        ddtb_ref[...] += jnp.sum(ddt_raw, axis=0, keepdims=True)
        dd_ref[...] += jnp.sum(dy * x, axis=0, keepdims=True)

    sp = _ssd_specs(ng, d_inner, True, nc)
    return pl.pallas_call(
        body, name="ssd_bwd", grid=(ng, nc),
        in_specs=[sp["xs"], sp["b"], sp["c"], sp["dtp"], sp["vec"], sp["vec"], sp["wide"], sp["e"],
                  sp["st"], sp["tok"]],
        out_specs=[sp["tok"], sp["bc_out"], sp["bc_out"], sp["dtp"], sp["vec"], sp["vec"], sp["wide"]],
        out_shape=[jax.ShapeDtypeStruct((s, d_inner), BF16),
                   jax.ShapeDtypeStruct((s, ng * D_STATE), BF16),
                   jax.ShapeDtypeStruct((s, ng * D_STATE), BF16),
                   jax.ShapeDtypeStruct((ng, s, LANES), F32),
                   jax.ShapeDtypeStruct((ng, 1, LANES), F32),
                   jax.ShapeDtypeStruct((ng, 1, LANES), F32),
                   jax.ShapeDtypeStruct((ng, 1, SSM_GROUP_WIDTH), F32)],
        scratch_shapes=[pltpu.VMEM((D_STATE, SSM_GROUP_WIDTH), F32)],
        compiler_params=_cparams(("parallel", "arbitrary")),
    )(xbc, xbc, xbc, dtp, a_pad, dtb_pad, dsk_e, e, states, dy)


def _gate_norm_fwd(y, pzx, norm_w):
    s, di = y.shape
    ng = di // SSM_GROUP_WIDTH
    tm = _pick(s, (512, 256, 128))

    def body(y_ref, z_ref, w_ref, o_ref):
        z = z_ref[...].astype(F32)
        y2 = y_ref[...].astype(F32) * (z * _sigmoid(z))
        r = lax.rsqrt(jnp.mean(y2 * y2, axis=1, keepdims=True) + RMS_EPS)
        o_ref[...] = (y2 * r * w_ref[...]).astype(BF16)

    blk = pl.BlockSpec((tm, SSM_GROUP_WIDTH), lambda i, g: (i, g))
    return pl.pallas_call(
        body, name="gate_norm_fwd", grid=(s // tm, ng),
        in_specs=[blk, blk, pl.BlockSpec((1, SSM_GROUP_WIDTH), lambda i, g: (0, g))],
        out_specs=blk, out_shape=jax.ShapeDtypeStruct((s, di), BF16),
        compiler_params=_cparams(("parallel", "parallel")),
    )(y, pzx, norm_w)


def _gate_norm_bwd(dy3, y, pzx, norm_w):
    s, di = y.shape
    ng = di // SSM_GROUP_WIDTH
    tm = _pick(s, (512, 256, 128))

    def body(d_ref, y_ref, z_ref, w_ref, dy_ref, dz_ref, dw_ref):
        @pl.when(pl.program_id(1) == 0)
        def _():
            dw_ref[...] = jnp.zeros_like(dw_ref)

        z = z_ref[...].astype(F32)
        yv = y_ref[...].astype(F32)
        sg = _sigmoid(z)
        sz = z * sg
        y2 = yv * sz
        r = lax.rsqrt(jnp.mean(y2 * y2, axis=1, keepdims=True) + RMS_EPS)
        nrm = y2 * r
        d3 = d_ref[...].astype(F32)
        dw_ref[...] += jnp.sum(d3 * nrm, axis=0, keepdims=True)
        dn = d3 * w_ref[...]
        dy2 = r * (dn - nrm * jnp.mean(dn * nrm, axis=1, keepdims=True))
        dy_ref[...] = (dy2 * sz).astype(BF16)
        dz_ref[...] = (dy2 * yv * (sg * (1.0 + z * (1.0 - sg)))).astype(BF16)

    blk = pl.BlockSpec((tm, SSM_GROUP_WIDTH), lambda g, i: (i, g))
    vec = pl.BlockSpec((1, SSM_GROUP_WIDTH), lambda g, i: (0, g))
    return pl.pallas_call(
        body, name="gate_norm_bwd", grid=(ng, s // tm),
        in_specs=[blk, blk, blk, vec], out_specs=[blk, blk, vec],
        out_shape=[jax.ShapeDtypeStruct((s, di), BF16), jax.ShapeDtypeStruct(pzx.shape, BF16),
                   jax.ShapeDtypeStruct((1, di), F32)],
        compiler_params=_cparams(("parallel", "arbitrary")),
    )(dy3, y, pzx, norm_w)


_ANY = pl.BlockSpec(memory_space=pl.ANY)


def _place():
    x, y, c = lax.axis_index("x"), lax.axis_index("y"), lax.axis_index("c")
    chips = [(1 - x, y), (x, 1 - y), (1 - x, 1 - y)]
    return x, y, c, chips


def _cast_to_slot(x, kvec, name, after=None):
    r, cn = x.shape
    tr = _rows_per_block(r, cn)
    extra = [] if after is None else [after]

    def body(k_ref, x_ref, *rest):
        rest[-1][...] = x_ref[...].astype(BF16)

    grid_spec = pltpu.PrefetchScalarGridSpec(
        num_scalar_prefetch=1, grid=(r // tr,),
        in_specs=[pl.BlockSpec((tr, cn), lambda i, k: (i, 0))] + [_ANY] * len(extra),
        out_specs=pl.BlockSpec((None, tr, cn), lambda i, k: (k[0], i, 0)))
    return pl.pallas_call(
        body, name=name, grid_spec=grid_spec, out_shape=jax.ShapeDtypeStruct((N_CHIPS, r, cn), BF16),
        compiler_params=_cparams(("parallel",)),
    )(kvec, x, *extra)


def _swap_halves(gs, name, after=None):
    n = len(gs)
    extra = [] if after is None else [after]

    def body(*refs):
        ins, outs = refs[:n], refs[n + len(extra):2 * n + len(extra)]
        send_sems, recv_sems = refs[2 * n + len(extra):]
        x, y, c, _ = _place()
        cps = []
        for w in range(n):
            hr = gs[w].shape[1] // 2
            cp = pltpu.make_async_remote_copy(
                src_ref=ins[w].at[:, pl.ds((1 - c) * hr, hr)], dst_ref=outs[w],
                send_sem=send_sems.at[w], recv_sem=recv_sems.at[w],
                device_id=(x, y, 1 - c), device_id_type=MESH)
            cp.start()
            cps.append(cp)
        for cp in cps:
            cp.wait()

    return pl.pallas_call(
        body, name=name,
        in_specs=[_ANY] * (n + len(extra)), out_specs=[_ANY] * n,
        out_shape=[jax.ShapeDtypeStruct((g.shape[0], g.shape[1] // 2, g.shape[2]), g.dtype) for g in gs],
        scratch_shapes=[pltpu.SemaphoreType.DMA((n,)), pltpu.SemaphoreType.DMA((n,))],
    )(*gs, *extra)


def _join_halves(fs, name, after=None):
    n = len(fs)
    extra = [] if after is None else [after]

    def body(*refs):
        outs = refs[n + len(extra):2 * n + len(extra)]
        send_sems, recv_sems = refs[2 * n + len(extra):]
        x, y, c, _ = _place()

        def copy(w, hc):
            hr = fs[w].shape[0] // 2
            rows = outs[w].at[pl.ds(hc * hr, hr)]
            return pltpu.make_async_remote_copy(
                src_ref=rows, dst_ref=rows, send_sem=send_sems.at[w], recv_sem=recv_sems.at[w],
                device_id=(x, y, 1 - c), device_id_type=MESH)

        cps = [copy(w, c) for w in range(n)]
        for cp in cps:
            cp.start()
        for w in range(n):
            copy(w, 1 - c).wait_recv()
        for cp in cps:
            cp.wait_send()

    return pl.pallas_call(
        body, name=name,
        in_specs=[_ANY] * (n + len(extra)), out_specs=[_ANY] * n,
        out_shape=[jax.ShapeDtypeStruct(f.shape, f.dtype) for f in fs],
        input_output_aliases={w: w for w in range(n)},
        scratch_shapes=[pltpu.SemaphoreType.DMA((n,)), pltpu.SemaphoreType.DMA((n,))],
    )(*fs, *extra)


_HBM_SPEC = pl.BlockSpec(memory_space=pltpu.HBM)
_SEM_SPEC = pl.BlockSpec(memory_space=pltpu.SEMAPHORE)
_VMEM_SPEC = pl.BlockSpec(memory_space=pltpu.VMEM)
_EFFECT = pltpu.SideEffectType.DATAFLOW_SIDE_EFFECTING
_TOKEN = jax.ShapeDtypeStruct((8, LANES), F32)


def _hbm(a):
    return pltpu.with_memory_space_constraint(a, pltpu.HBM)


def _gather_copies(bufs, refs, send_sems, recv_sems, forward, arrivals=True):
    x, y, c, chips = _place()
    k = 2 * x + y
    out, arrive = [], []
    for w, ref in enumerate(refs):
        hr = bufs[w].shape[1] // 2
        for j, (cx, cy) in enumerate(chips):
            kj = 2 * cx + cy
            slot_out, slot_in, half_in = (kj, kj, 1 - c) if forward else (k, kj, c)
            to = (x, y, 1 - c) if forward else (cx, cy, c)
            src = ref.at[slot_out, pl.ds(c * hr, hr)]
            land = ref.at[slot_in, pl.ds(half_in * hr, hr)]
            out.append(pltpu.make_async_remote_copy(
                src_ref=src, dst_ref=src, send_sem=send_sems.at[3 * w + j], recv_sem=recv_sems.at[3 * w + j],
                device_id=to, device_id_type=MESH))
            if arrivals:
                arrive.append(pltpu.make_async_remote_copy(
                    src_ref=land, dst_ref=land, send_sem=send_sems.at[3 * w + j], recv_sem=recv_sems.at[3 * w + j],
                    device_id=to, device_id_type=MESH))
    return out, arrive


def _gather_start(bufs, forward, name, after=None):
    n = len(bufs)
    extra = [] if after is None else [after]

    def body(*refs):
        ins = refs[:n]
        send_sems, recv_sems = refs[n + len(extra)], refs[n + len(extra) + 1]
        token = refs[-1]
        out, _ = _gather_copies(bufs, ins, send_sems, recv_sems, forward, arrivals=False)
        for cp in out:
            cp.start()
        token[...] = jnp.zeros_like(token)

    res = pl.pallas_call(
        body, name=name,
        out_shape=(pltpu.SemaphoreType.DMA((3 * n,)), pltpu.SemaphoreType.DMA((3 * n,)))
        + tuple(pltpu.HBM(b.shape, b.dtype) for b in bufs) + (_TOKEN,),
        in_specs=(_HBM_SPEC,) * n + (_ANY,) * len(extra),
        out_specs=(_SEM_SPEC, _SEM_SPEC) + (_HBM_SPEC,) * n + (_VMEM_SPEC,),
        input_output_aliases={w: 2 + w for w in range(n)},
        compiler_params=pltpu.CompilerParams(has_side_effects=_EFFECT),
    )(*[_hbm(b) for b in bufs], *extra)
    return res[0], res[1], list(res[2:2 + n]), res[-1]


def _gather_wait(bufs, send_sems, recv_sems, after, forward, name):
    n = len(bufs)

    def body(*refs):
        ins = refs[:n]
        send_sems, recv_sems = refs[n], refs[n + 1]
        out, arrive = _gather_copies(bufs, ins, send_sems, recv_sems, forward)
        for cp in out:
            cp.wait_send()
        for cp in arrive:
            cp.wait_recv()

    res = pl.pallas_call(
        body, name=name,
        out_shape=tuple(pltpu.HBM(b.shape, b.dtype) for b in bufs),
        in_specs=(_HBM_SPEC,) * n + (_SEM_SPEC, _SEM_SPEC, _ANY), out_specs=(_HBM_SPEC,) * n,
        input_output_aliases={w: w for w in range(n)},
        compiler_params=pltpu.CompilerParams(has_side_effects=_EFFECT),
    )(*bufs, send_sems, recv_sems, after)
    return list(res)


def _scatter_copies(t_ref, land_ref, send_sems, recv_sems, arrivals=True):
    x, y, c, chips = _place()
    k = 2 * x + y
    out, arrive = [], []
    for j, (cx, cy) in enumerate(chips):
        kj = 2 * cx + cy
        out.append(pltpu.make_async_remote_copy(
            src_ref=t_ref.at[kj], dst_ref=land_ref.at[k], send_sem=send_sems.at[j], recv_sem=recv_sems.at[j],
            device_id=(cx, cy, c), device_id_type=MESH))
        if arrivals:
            arrive.append(pltpu.make_async_remote_copy(
                src_ref=t_ref.at[kj], dst_ref=land_ref.at[kj], send_sem=send_sems.at[j], recv_sem=recv_sems.at[j],
                device_id=(cx, cy, c), device_id_type=MESH))
    return out, arrive


def _scatter_start(t, name):
    def body(t_ref, land_ref, send_sems, recv_sems, t_thru, land_thru, token):
        out, _ = _scatter_copies(t_ref, land_ref, send_sems, recv_sems, arrivals=False)
        for cp in out:
            cp.start()
        token[...] = jnp.zeros_like(token)

    return pl.pallas_call(
        body, name=name,
        out_shape=(pltpu.SemaphoreType.DMA((3,)), pltpu.SemaphoreType.DMA((3,)),
                   pltpu.HBM(t.shape, t.dtype), pltpu.HBM(t.shape, t.dtype), _TOKEN),
        in_specs=(_HBM_SPEC, _HBM_SPEC), out_specs=(_SEM_SPEC, _SEM_SPEC, _HBM_SPEC, _HBM_SPEC, _VMEM_SPEC),
        input_output_aliases={0: 2, 1: 3},
        compiler_params=pltpu.CompilerParams(has_side_effects=_EFFECT),
    )(_hbm(t), _hbm(lax.empty(t.shape, t.dtype)))


def _scatter_wait(send_sems, recv_sems, t_thru, land_thru, after, name):
    def body(t_ref, land_ref, send_sems, recv_sems, after_ref, t_out, land_out):
        out, arrive = _scatter_copies(t_ref, land_ref, send_sems, recv_sems)
        for cp in out:
            cp.wait_send()
        for cp in arrive:
            cp.wait_recv()

    return pl.pallas_call(
        body, name=name,
        out_shape=(pltpu.HBM(t_thru.shape, t_thru.dtype), pltpu.HBM(land_thru.shape, land_thru.dtype)),
        in_specs=(_HBM_SPEC, _HBM_SPEC, _SEM_SPEC, _SEM_SPEC, _ANY), out_specs=(_HBM_SPEC, _HBM_SPEC),
        input_output_aliases={0: 0, 1: 1},
        compiler_params=pltpu.CompilerParams(has_side_effects=_EFFECT),
    )(t_thru, land_thru, send_sems, recv_sems, after)


def _all_gather_small(v, reduce, name):
    r, l = v.shape

    def body(v_ref, o_ref, *rest):
        if reduce:
            buf, send_sems, recv_sems = rest
        else:
            buf = o_ref
            send_sems, recv_sems = rest
        x, y, c, _ = _place()
        me = 4 * x + 2 * y + c
        buf[me] = v_ref[...]
        cps = []
        for d in range(1, N_DEV):
            peer = (x if d & 4 == 0 else 1 - x, y if d & 2 == 0 else 1 - y, c if d & 1 == 0 else 1 - c)
            cp = pltpu.make_async_remote_copy(
                src_ref=v_ref, dst_ref=buf.at[me], send_sem=send_sems.at[d - 1], recv_sem=recv_sems.at[d - 1],
                device_id=peer, device_id_type=MESH)
            cp.start()
            cps.append((cp, peer))
        for d, (cp, (px, py, pc)) in enumerate(cps, start=1):
            pltpu.make_async_remote_copy(
                src_ref=v_ref, dst_ref=buf.at[4 * px + 2 * py + pc], send_sem=send_sems.at[d - 1],
                recv_sem=recv_sems.at[d - 1], device_id=(px, py, pc), device_id_type=MESH).wait_recv()
        for cp, _ in cps:
            cp.wait_send()
        if reduce:
            acc = buf[0]
            for i in range(1, N_DEV):
                acc = acc + buf[i]
            o_ref[...] = acc

    vm = pl.BlockSpec(memory_space=pltpu.VMEM)
    out_shape = jax.ShapeDtypeStruct((r, l) if reduce else (N_DEV, r, l), F32)
    scratch = ([pltpu.VMEM((N_DEV, r, l), F32)] if reduce else []) + [
        pltpu.SemaphoreType.DMA((N_DEV - 1,)), pltpu.SemaphoreType.DMA((N_DEV - 1,))]
    return pl.pallas_call(
        body, name=name, in_specs=[vm], out_specs=vm, out_shape=out_shape, scratch_shapes=scratch,
    )(v)


_BLOCK_BYTES = 3 * 512 * 1024


def _rows_per_block(r, cn, itemsize=4):
    best = 8
    for t in range(8, r + 1, 8):
        if r % t == 0 and t * cn * itemsize <= _BLOCK_BYTES:
            best = t
    return best


def _add_sibling_half(g4, recv, cvec, name):
    ns, r, cn = g4.shape
    hr = r // 2
    tr = _rows_per_block(hr, cn)
    nrb = hr // tr

    def body(c_ref, a_ref, b_ref, o_ref):
        o_ref[...] = (a_ref[...].astype(F32) + b_ref[...].astype(F32)).astype(o_ref.dtype)

    grid_spec = pltpu.PrefetchScalarGridSpec(
        num_scalar_prefetch=1, grid=(ns, nrb),
        in_specs=[pl.BlockSpec((None, tr, cn), lambda j, i, c: (j, c[0] * nrb + i, 0)),
                  pl.BlockSpec((None, tr, cn), lambda j, i, c: (j, i, 0))],
        out_specs=pl.BlockSpec((None, tr, cn), lambda j, i, c: (j, i, 0)))
    return pl.pallas_call(
        body, name=name, grid_spec=grid_spec, out_shape=jax.ShapeDtypeStruct((ns, hr, cn), BF16),
        compiler_params=_cparams(("parallel", "parallel")),
    )(cvec, g4, recv)


def _sum_chips(r4, t4, kvec, cvec, name):
    ns, hr, cn = r4.shape
    tr = _rows_per_block(hr, cn)
    nrb = hr // tr

    def body(k_ref, c_ref, r_ref, t_ref, o_ref):
        acc = t_ref[...].astype(F32)
        for dlt in range(1, ns):
            acc = acc + r_ref[(k_ref[0] + dlt) % ns].astype(F32)
        o_ref[...] = acc

    grid_spec = pltpu.PrefetchScalarGridSpec(
        num_scalar_prefetch=2, grid=(nrb,),
        in_specs=[pl.BlockSpec((ns, tr, cn), lambda i, k, c: (0, i, 0)),
                  pl.BlockSpec((None, tr, cn), lambda i, k, c: (k[0], i, 0))],
        out_specs=pl.BlockSpec((tr, cn), lambda i, k, c: (c[0] * nrb + i, 0)))
    return pl.pallas_call(
        body, name=name, grid_spec=grid_spec, out_shape=jax.ShapeDtypeStruct((2 * hr, cn), F32),
        compiler_params=_cparams(("parallel",)),
    )(kvec, cvec, r4, t4)


def _adamw(w, g, m, v, name):
    r, cn = w.shape
    tr = _rows_per_block(r, cn)
    c1 = 1.0 - ADAM_B1 ** ADAM_STEP
    c2 = 1.0 - ADAM_B2 ** ADAM_STEP

    def body(w_ref, g_ref, m_ref, v_ref, go_ref, d_ref, mo_ref, vo_ref):
        gv = g_ref[...]
        mn = ADAM_B1 * m_ref[...] + (1.0 - ADAM_B1) * gv
        vn = ADAM_B2 * v_ref[...] + (1.0 - ADAM_B2) * (gv * gv)
        go_ref[...] = gv
        mo_ref[...] = mn
        vo_ref[...] = vn
        d_ref[...] = -ADAM_LR * ((mn / c1) / (jnp.sqrt(vn / c2) + ADAM_EPS) + ADAM_WD * w_ref[...])

    spec = pl.BlockSpec((tr, cn), lambda i: (i, 0))
    return pl.pallas_call(
        body, name=name, grid=(r // tr,), in_specs=[spec] * 4, out_specs=[spec] * 4,
        out_shape=[jax.ShapeDtypeStruct((r, cn), F32)] * 4,
        compiler_params=_cparams(("parallel",)),
    )(w, g, m, v)


def _pack(arrs):
    flat = jnp.concatenate([a.reshape(-1).astype(F32) for a in arrs])
    n = flat.shape[0]
    tot = -(-n // (8 * LANES)) * (8 * LANES)
    return jnp.pad(flat, (0, tot - n)).reshape(tot // LANES, LANES)


def _unpack(packed, shapes):
    flat = packed.reshape(-1)
    out, off = [], 0
    for shp in shapes:
        sz = int(np.prod(shp))
        out.append(flat[off:off + sz].reshape(shp))
        off += sz
    return out


class _LocalExchange:
    def __init__(self, ws4, wos4):
        self.ssm = [ws4, wos4]
        self.grads = {}

    def ssm_gather_start(self):
        return None

    def ssm_gather_mid(self, after):
        return None

    def ssm_gather_end(self, after):
        return self.ssm

    def grad_ready(self, name, g4, after=None):
        self.grads[name] = g4
        return None

    def grad_sync(self, name, after):
        pass

    def small_grads(self, small_full):
        self.small = small_full
        return None


class _Exchange:
    def __init__(self, kvec, cvec, ssm_bufs, after):
        self.kvec, self.cvec, self.bufs, self.after = kvec, cvec, ssm_bufs, after
        self.pending, self.summed, self.last_token = {}, {}, None

    def ssm_gather_start(self):
        self.sems = _gather_start(self.bufs, False, "ssm_gather_ici_start", self.after)
        self.bufs = self.sems[2]
        return self.sems[3]

    def ssm_gather_mid(self, after):
        bufs = _gather_wait(self.bufs, self.sems[0], self.sems[1], after, False, "ssm_gather_ici_wait")
        self.sems = _gather_start(bufs, True, "ssm_gather_fwd_start")
        self.bufs = self.sems[2]
        return self.sems[3]

    def ssm_gather_end(self, after):
        return _gather_wait(self.bufs, self.sems[0], self.sems[1], after, True, "ssm_gather_fwd_wait")

    def small_grads(self, small_full):
        packed = _all_gather_small(_pack(small_full), True, "reduce_small_grads")
        self.small = _unpack(packed, [t.shape for t in small_full])
        return packed

    def grad_ready(self, name, g4, after=None):
        recv = _swap_halves([g4], "grads_to_sibling_" + name, after)[0]
        t = _add_sibling_half(g4, recv, self.cvec, "add_sibling_" + name)
        send_sems, recv_sems, t_thru, land, token = _scatter_start(t, "scatter_start_" + name)
        self.pending[name] = (send_sems, recv_sems, t_thru, land)
        self.last_token = token
        return token

    def grad_sync(self, name, after):
        t, land = _scatter_wait(*self.pending.pop(name), after, "scatter_wait_" + name)
        self.summed[name] = _sum_chips(land, t, self.kvec, self.cvec, "sum_chips_" + name)


def _tie(vec, token):
    return vec if token is None else vec + token[0:1, 0:1].reshape((1,) * vec.ndim).astype(vec.dtype)


def _local_step(x2, xb, tgt, wa4, woa4, ex, conv_w_f, conv_b_f, norm_w_f, rel_bias, dt_bias, a_log, d_skip,
                ln_g, ln_b):
    s, d = x2.shape
    d_attn = woa4.shape[1]
    hpg = d_attn // HEAD_DIM
    d_inner = norm_w_f.shape[1]
    ng = d_inner // SSM_GROUP_WIDTH
    n_heads = dt_bias.shape[1]
    conv_dim = conv_w_f.shape[1]
    assert n_heads == ng * HEADS_PER_SSM_GROUP and conv_dim == d_inner + 2 * ng * D_STATE
    assert wa4.shape[2] * N_CHIPS == 10 * d_attn

    tok = ex.ssm_gather_start()
    buckets = _bucket_tiles()
    bias = _bias_expand(rel_bias, buckets, hpg)
    pgs, og, lg = [], [], []
    for g, (_, dil) in enumerate(ATTN_PATTERNS):
        pg = _mm_nn_sharded(xb, wa4, BF16, f"mm_in_attn_g{g}", after=tok,
                            col_off=3 * g * d_attn, n=3 * d_attn, classes=dil)
        o_, l_ = _attn_fwd(pg, bias, g, dil, hpg)
        pgs.append(pg)
        og.append(o_)
        lg.append(l_)
    gate = _mm_nn_sharded(xb, wa4, BF16, "mm_in_attn_gate", col_off=9 * d_attn, n=d_attn)
    o, lse, yat = _attn_combine(og, lg, gate, hpg)
    h0 = _mm_nn_sharded(yat, woa4, F32, "mm_out_attn", after=ex.ssm_gather_mid(yat))
    g0, b0, g1, b1 = ln_g[0:1], ln_b[0:1], ln_g[1:2], ln_b[1:2]
    xhat0, rstd0, x1b = _ln_fwd(x2, h0, g0, b0, "ln0_fwd")

    wst4, wos4 = ex.ssm_gather_end(x1b)
    wst = wst4.reshape(N_CHIPS * wst4.shape[1], d)
    nzx = d_inner + conv_dim
    wos = wos4.reshape(d_inner, d)
    pzx = _mm_nt(x1b, wst, BF16, "mm_in_ssm", n=nzx)
    dt_raw = _mm_nt(x1b, wst, F32, "mm_in_dt", n=n_heads, b_row_off=nzx)

    def pad_heads(t):
        t = t.reshape(t.shape[0], ng, HEADS_PER_SSM_GROUP).transpose(1, 0, 2)
        return jnp.pad(t, ((0, 0), (0, 0), (0, LANES - HEADS_PER_SSM_GROUP)))

    def unpad_heads(t):
        return t[:, :, :HEADS_PER_SSM_GROUP].transpose(1, 0, 2).reshape(t.shape[1], n_heads)

    dtp = pad_heads(dt_raw)
    alog_p, dtb_p = pad_heads(a_log), pad_heads(dt_bias)
    dsk_e = jnp.repeat(d_skip.reshape(ng, 1, HEADS_PER_SSM_GROUP), SSM_HEAD_DIM, axis=2)
    e = _expand_matrix()
    xbc = _conv_fwd(pzx, conv_w_f, conv_b_f, d_inner)
    y_ssd, states = _ssd_fwd(xbc, dtp, alog_p, dtb_p, dsk_e, e, d_inner)
    y3 = _gate_norm_fwd(y_ssd, pzx, norm_w_f)
    h1 = _mm_nn(y3, wos, F32, "mm_out_ssm")
    xhat1, rstd1, dy2, row_sq = _ln_fwd(xhat0, h1, g1, b1, "ln1_fwd_loss", affine_in=(g0, b0), target=tgt)
    loss_local = 0.5 * jnp.sum(row_sq) / d

    du1, du1b, dg1, db1 = _ln_bwd(dy2, xhat1, rstd1, g1, "ln1_bwd")
    dy3 = _mm_nt(du1b, wos, BF16, "mm_d_y3")
    g_wos = _mm_tn(y3, du1b, BF16, "mm_g_w_out_ssm").reshape(N_CHIPS, d_inner // N_CHIPS, d)
    norm_w_t = _tie(norm_w_f, ex.grad_ready("w_out_ssm", g_wos))
    dy_ssd, dz, d_nw = _gate_norm_bwd(dy3, y_ssd, pzx, norm_w_t)
    dxs, dbm, dcm, ddtp, d_alog, d_dtb, d_dsk = _ssd_bwd(xbc, dtp, alog_p, dtb_p, dsk_e, e, states, dy_ssd, d_inner)
    dpre, d_cw, d_cb = _conv_bwd_a(pzx, jnp.concatenate([dxs, dbm, dcm], axis=1), conv_w_f, conv_b_f, d_inner)
    dpzx = _conv_bwd_b(dpre, conv_w_f, dz, d_inner)
    ddt_raw = unpad_heads(ddtp)
    t1 = _mm_nn(ddt_raw, wst, F32, "mm_d_x1_dt", b_row_off=nzx, add=du1, add_scale=DEEPNORM_ALPHA)
    dx1 = _mm_nn(dpzx, wst, F32, "mm_d_x1", add=t1)
    ex.grad_sync("w_out_ssm", dx1)
    g_wst = _mm_tn(dpzx, x1b, BF16, "mm_g_w_in_ssm", out_rows=wst.shape[0])
    g_wst = _mm_tn(ddt_raw, x1b, BF16, "mm_g_w_dt", out_rows=wst.shape[0], out_row_off=nzx, into=g_wst)
    g0_t = _tie(g0, ex.grad_ready("w_in_ssm", g_wst.reshape(wst4.shape)))

    du0, du0b, dg0, db0 = _ln_bwd(dx1, xhat0, rstd0, g0_t, "ln0_bwd")
    dyat = _mm_nt_sharded_k(du0b, woa4, BF16, "mm_d_yat")
    g_woa = _mm_tn(yat, du0b, BF16, "mm_g_w_out_attn", shard_cols=d // N_CHIPS)
    tok_woa = ex.grad_ready("w_out_attn", g_woa)
    do, delta, dgate = _attn_pre_bwd(dyat, o, gate, hpg)
    pieces, dbt = [], []
    for g, (_, dil) in enumerate(ATTN_PATTERNS):
        dq, dk, dv, db_ = _attn_bwd(pgs[g], bias, do, lse, delta, g, dil, hpg)
        pieces += [dq, dk, dv]
        dbt.append(db_)
    dpa = jnp.concatenate(pieces + [dgate], axis=1)
    g_wa = _mm_tn(xb, dpa, BF16, "mm_g_w_in_attn", shard_cols=wa4.shape[2], after=tok_woa)
    ex.grad_sync("w_in_ssm", g_wa)
    ex.grad_sync("w_out_attn", g_wa)
    d_rel = _bias_reduce(jnp.stack(dbt), buckets, hpg)[:, :, 0].T
    d_dsk_h = d_dsk.reshape(n_heads, SSM_HEAD_DIM).sum(axis=1)
    small_full = [d_rel, d_cw, d_cb, unpad_heads(d_dtb), unpad_heads(d_alog), d_dsk_h[None], d_nw,
                  jnp.concatenate([dg0, dg1], axis=0), jnp.concatenate([db0, db1], axis=0)]
    tok_wa = ex.grad_ready("w_in_attn", g_wa, after=ex.small_grads(small_full))
    grad_x = _mm_nt_sharded_k(dpa, wa4, F32, "mm_d_x0", add=du0, add_scale=DEEPNORM_ALPHA, after=tok_wa)
    return loss_local, grad_x[None]


def kernel(x, w_in_attn, w_out_attn, rel_bias, w_in_ssm, conv_w, conv_b, dt_bias, a_log, d_skip, ssm_norm_w, w_out_ssm, ln_g, ln_b, loss_target, m_w_in_attn, m_w_out_attn, m_rel_bias, m_w_in_ssm, m_conv_w, m_conv_b, m_dt_bias, m_a_log, m_d_skip, m_ssm_norm_w, m_w_out_ssm, m_ln_g, m_ln_b, v_w_in_attn, v_w_out_attn, v_rel_bias, v_w_in_ssm, v_conv_w, v_conv_b, v_dt_bias, v_a_log, v_d_skip, v_ssm_norm_w, v_w_out_ssm, v_ln_g, v_ln_b):
    xi, yi, ci = lax.axis_index("x"), lax.axis_index("y"), lax.axis_index("c")
    chip = 2 * xi + yi
    cvec = jnp.reshape(ci, (1,)).astype(jnp.int32)
    kvec = jnp.reshape(chip, (1,)).astype(jnp.int32)

    cw_l, cb_l, nw_l = conv_w[0], conv_b[0], ssm_norm_w[0]
    vec_shapes = [cw_l.shape, cb_l.shape, nw_l.shape]
    vec_all = _all_gather_small(_pack([cw_l, cb_l, nw_l]), False, "gather_vectors")
    l0 = [_cast_to_slot(w_in_attn[0], kvec, "cast_w_in_attn"), _cast_to_slot(w_out_attn[0], kvec, "cast_w_out_attn")]
    send0, recv0, l0, tok0 = _gather_start(l0, False, "l0_gather_ici_start", vec_all)
    xb = _cast_bf16(x[0], "cast_x", tok0)
    ssm_bufs = [_cast_to_slot(w_in_ssm[0].T, kvec, "cast_w_in_ssm", tok0),
                _cast_to_slot(w_out_ssm[0], kvec, "cast_w_out_ssm", tok0)]
    l0 = _gather_wait(l0, send0, recv0, xb, False, "l0_gather_ici_wait")
    send0, recv0, l0, _ = _gather_start(l0, True, "l0_gather_fwd_start")
    wa4, woa4 = _gather_wait(l0, send0, recv0, ssm_bufs[0], True, "l0_gather_fwd_wait")
    ex = _Exchange(kvec, cvec, ssm_bufs, after=woa4)
    parts = [_unpack(vec_all[2 * j], vec_shapes) for j in range(N_CHIPS)]
    conv_w_f = jnp.concatenate([p[0] for p in parts], axis=1)
    conv_b_f = jnp.concatenate([p[1] for p in parts], axis=0)[None]
    norm_w_f = jnp.concatenate([p[2] for p in parts], axis=0)[None]

    loss_local, grad_x = _local_step(
        x[0], xb, loss_target[0], wa4, woa4, ex, conv_w_f, conv_b_f, norm_w_f, rel_bias, dt_bias, a_log,
        d_skip, ln_g, ln_b)
    loss = lax.psum(loss_local, ("x", "y", "c"))

    big_w = dict(w_in_attn=(w_in_attn, m_w_in_attn, v_w_in_attn), w_out_attn=(w_out_attn, m_w_out_attn, v_w_out_attn),
                 w_in_ssm=(w_in_ssm, m_w_in_ssm, v_w_in_ssm), w_out_ssm=(w_out_ssm, m_w_out_ssm, v_w_out_ssm))
    big = {}

    def finish(names, join_name, after):
        last = None
        for nm, gf in zip(names, _join_halves([ex.summed[nm] for nm in names], join_name, after)):
            flip = (lambda t: t.T) if nm == "w_in_ssm" else (lambda t: t)
            w_, m_, v_ = (flip(t[0]) for t in big_w[nm])
            res = _adamw(w_, gf, m_, v_, "adamw_" + nm)
            big[nm] = [flip(t)[None] for t in res]
            last = res[3]
        return last

    last = finish(["w_out_ssm", "w_in_ssm", "w_out_attn"], "grads_join_halves_a", ex.last_token)
    ex.grad_sync("w_in_attn", last)
    finish(["w_in_attn"], "grads_join_halves_b", None)

    s_rel, s_cw, s_cb, s_dtb, s_alog, s_dsk, s_nw, s_lng, s_lnb = ex.small
    cwc, nwc = conv_w.shape[2], ssm_norm_w.shape[1]
    s_cw = lax.dynamic_slice_in_dim(s_cw, chip * cwc, cwc, axis=1)[None]
    s_cb = lax.dynamic_slice_in_dim(s_cb, chip * cwc, cwc, axis=1)
    s_nw = lax.dynamic_slice_in_dim(s_nw, chip * nwc, nwc, axis=1)
    small_names = ["rel_bias", "conv_w", "conv_b", "dt_bias", "a_log", "d_skip", "ssm_norm_w", "ln_g", "ln_b"]
    small_g = [s_rel, s_cw, s_cb, s_dtb, s_alog, s_dsk, s_nw, s_lng, s_lnb]
    small_w = [rel_bias, conv_w, conv_b, dt_bias, a_log, d_skip, ssm_norm_w, ln_g, ln_b]
    small_m = [m_rel_bias, m_conv_w, m_conv_b, m_dt_bias, m_a_log, m_d_skip, m_ssm_norm_w, m_ln_g, m_ln_b]
    small_v = [v_rel_bias, v_conv_w, v_conv_b, v_dt_bias, v_a_log, v_d_skip, v_ssm_norm_w, v_ln_g, v_ln_b]
    shapes = [t.shape for t in small_w]
    res = _adamw(_pack(small_w), _pack(small_g), _pack(small_m), _pack(small_v), "adamw_small")
    small = {nm: [] for nm in small_names}
    for packed in res:
        for nm, t in zip(small_names, _unpack(packed, shapes)):
            small[nm].append(t)

    order = ["w_in_attn", "w_out_attn", "rel_bias", "w_in_ssm", "conv_w", "conv_b", "dt_bias", "a_log",
             "d_skip", "ssm_norm_w", "w_out_ssm", "ln_g", "ln_b"]
    table = {**big, **small}
    outs = [loss, grad_x]
    for kind in range(4):
        outs += [table[nm][kind] for nm in order]
    return tuple(outs)
```

```python
import functools
import math

import numpy as np
import jax
import jax.numpy as jnp
from jax import lax
from jax.experimental import pallas as pl
from jax.experimental.pallas import tpu as pltpu

F32 = jnp.float32
BF16 = jnp.bfloat16
MESH = pl.DeviceIdType.MESH

ATTN_PATTERNS = ((128, 1), (512, 4), (2048, 16))
N_GROUPS_ATTN = 3
HEAD_DIM = 128
ATTN_BLOCK = 128
NUM_BUCKETS = 32
MAX_DISTANCE = 2048
SSM_HEAD_DIM = 64
HEADS_PER_SSM_GROUP = 16
SSM_GROUP_WIDTH = HEADS_PER_SSM_GROUP * SSM_HEAD_DIM
D_STATE = 128
CONV_WIDTH = 4
CHUNK = 128
DEPTH = 2
DEEPNORM_ALPHA = (2 * DEPTH) ** 0.25
LN_EPS = 1e-5
RMS_EPS = 1e-5
NEG_INF = -1e30
ADAM_LR = 0.001
ADAM_B1 = 0.9
ADAM_B2 = 0.999
ADAM_EPS = 1e-08
ADAM_WD = 0.01
ADAM_STEP = 10

N_CHIPS = 4
N_DEV = 8

VMEM_LIMIT_V7X = 56 * 1024 * 1024
LANES = 128


def _cparams(sem=None):
    return pltpu.CompilerParams(dimension_semantics=sem, vmem_limit_bytes=VMEM_LIMIT_V7X)


def _sigmoid(x):
    return 0.5 * jnp.tanh(0.5 * x) + 0.5


def _dot(a, b):
    return jnp.dot(a, b, preferred_element_type=F32)


def _dot_nt(a, b):
    return lax.dot_general(a, b, (((1,), (1,)), ((), ())), preferred_element_type=F32)


def _dot_tn(a, b):
    return lax.dot_general(a, b, (((0,), (0,)), ((), ())), preferred_element_type=F32)


def _split2(x):
    hi = x.astype(BF16)
    lo = (x - hi.astype(F32)).astype(BF16)
    return hi, lo


def _split3(x):
    hi = x.astype(BF16)
    r = x - hi.astype(F32)
    mid = r.astype(BF16)
    lo = (r - mid.astype(F32)).astype(BF16)
    return hi, mid, lo


def _matmul(a, b, *, mode, grid, a_spec, b_spec, out_shape, out_spec, tile, name,
            add=None, add_spec=None, add_scale=1.0, after=None, into=None):
    nk = grid[2]
    tm, tn = tile
    dot = {"nn": _dot, "nt": _dot_nt, "tn": _dot_tn}[mode]
    has_add = add is not None
    has_after = after is not None
    has_into = into is not None

    def body(*refs):
        a_ref, b_ref = refs[:2]
        add_ref = refs[2] if has_add else None
        o_ref, acc_ref = refs[-2:]
        k = pl.program_id(2)

        @pl.when(k == 0)
        def _():
            acc_ref[...] = jnp.zeros_like(acc_ref)

        acc_ref[...] += dot(a_ref[...].astype(BF16), b_ref[...].astype(BF16))

        @pl.when(k == nk - 1)
        def _():
            r = acc_ref[...]
            if has_add:
                r = r + add_scale * add_ref[...].astype(F32)
            o_ref[...] = r.astype(o_ref.dtype)

    in_specs = ([a_spec, b_spec] + ([add_spec] if has_add else []) + ([_ANY] if has_after else [])
                + ([_ANY] if has_into else []))
    args = (a, b) + ((add,) if has_add else ()) + ((after,) if has_after else ()) + ((into,) if has_into else ())
    return pl.pallas_call(
        body, name=name, grid=grid, in_specs=in_specs, out_specs=out_spec, out_shape=out_shape,
        input_output_aliases={len(args) - 1: 0} if has_into else {},
        scratch_shapes=[pltpu.VMEM((tm, tn), F32)],
        compiler_params=_cparams(("parallel", "parallel", "arbitrary")),
    )(*args)


def _pick(n, pref):
    for t in pref:
        if n % t == 0:
            return t
    return n


_TILE_PREF = (1024, 512, 256, 128)
_K_TILE_PREF = (2048,) + _TILE_PREF


def _mm_nn_sharded(a, w4, out_dtype, name, after=None, col_off=0, n=None, classes=1):
    m, k = a.shape
    _, _, nn = w4.shape
    n = N_CHIPS * nn if n is None else n
    tm, tk = _pick(m // classes, _TILE_PREF), _pick(k, _K_TILE_PREF)
    tn = _pick(math.gcd(math.gcd(nn, n), col_off) if col_off else math.gcd(nn, n), _TILE_PREF)
    npb = nn // tn
    co = col_off // tn
    bpc, kb = m // classes // tm, k // tk
    av = a.reshape(m // classes, classes * k)
    out_shape = jax.ShapeDtypeStruct((m, n), out_dtype)
    if tm < _TILE_PREF[0]:
        return _matmul(
            av, w4, mode="nn", grid=(n // tn, m // tm, 1), tile=(tm, tn), name=name,
            a_spec=pl.BlockSpec((tm, k), lambda j, i, kk: (i % bpc, i // bpc)),
            b_spec=pl.BlockSpec((None, k, tn), lambda j, i, kk: ((j + co) // npb, 0, (j + co) % npb)),
            out_shape=out_shape, out_spec=pl.BlockSpec((tm, tn), lambda j, i, kk: (i, j)), after=after)
    return _matmul(
        av, w4, mode="nn", grid=(m // tm, n // tn, kb), tile=(tm, tn), name=name,
        a_spec=pl.BlockSpec((tm, tk), lambda i, j, kk: (i % bpc, (i // bpc) * kb + kk)),
        b_spec=pl.BlockSpec((None, tk, tn), lambda i, j, kk: ((j + co) // npb, kk, (j + co) % npb)),
        out_shape=out_shape, out_spec=pl.BlockSpec((tm, tn), lambda i, j, kk: (i, j)), after=after)


def _mm_nn(a, b, out_dtype, name, b_row_off=0, add=None, add_scale=1.0):
    m, k = a.shape
    _, n = b.shape
    tm, tk, tn = _pick(m, _TILE_PREF), _pick(k, _K_TILE_PREF), _pick(n, _TILE_PREF)
    assert b_row_off % tk == 0
    ko = b_row_off // tk
    return _matmul(
        a, b, mode="nn", grid=(m // tm, n // tn, k // tk), tile=(tm, tn), name=name,
        a_spec=pl.BlockSpec((tm, tk), lambda i, j, kk: (i, kk)),
        b_spec=pl.BlockSpec((tk, tn), lambda i, j, kk: (kk + ko, j)),
        out_shape=jax.ShapeDtypeStruct((m, n), out_dtype),
        out_spec=pl.BlockSpec((tm, tn), lambda i, j, kk: (i, j)),
        add=add, add_spec=pl.BlockSpec((tm, tn), lambda i, j, kk: (i, j)), add_scale=add_scale)


def _mm_nt(a, b, out_dtype, name, add=None, add_scale=1.0, n=None, b_row_off=0):
    m, k = a.shape
    n = b.shape[0] if n is None else n
    tm, tk, tn = _pick(m, _TILE_PREF), _pick(k, _K_TILE_PREF), _pick(n, _TILE_PREF)
    assert b_row_off % tn == 0
    no = b_row_off // tn
    return _matmul(
        a, b, mode="nt", grid=(m // tm, n // tn, k // tk), tile=(tm, tn), name=name,
        a_spec=pl.BlockSpec((tm, tk), lambda i, j, kk: (i, kk)),
        b_spec=pl.BlockSpec((tn, tk), lambda i, j, kk: (j + no, kk)),
        out_shape=jax.ShapeDtypeStruct((m, n), out_dtype),
        out_spec=pl.BlockSpec((tm, tn), lambda i, j, kk: (i, j)),
        add=add, add_spec=pl.BlockSpec((tm, tn), lambda i, j, kk: (i, j)), add_scale=add_scale)


def _mm_nt_sharded_k(a, w4, out_dtype, name, add=None, add_scale=1.0, after=None):
    m, _ = a.shape
    _, n, kn = w4.shape
    tm, tk, tn = _pick(m, _TILE_PREF), _pick(kn, _TILE_PREF), _pick(n, _TILE_PREF)
    kpb = kn // tk
    return _matmul(
        a, w4, mode="nt", grid=(m // tm, n // tn, N_CHIPS * kpb), tile=(tm, tn), name=name,
        a_spec=pl.BlockSpec((tm, tk), lambda i, j, kk: (i, kk)),
        b_spec=pl.BlockSpec((None, tn, tk), lambda i, j, kk: (kk // kpb, j, kk % kpb)),
        out_shape=jax.ShapeDtypeStruct((m, n), out_dtype),
        out_spec=pl.BlockSpec((tm, tn), lambda i, j, kk: (i, j)),
        add=add, add_spec=pl.BlockSpec((tm, tn), lambda i, j, kk: (i, j)), add_scale=add_scale, after=after)


def _mm_tn(a, b, out_dtype, name, shard_cols=None, out_rows=None, out_row_off=0, into=None, after=None):
    k, m = a.shape
    _, n = b.shape
    nn = n if shard_cols is None else shard_cols
    tm, tk, tn = _pick(m, _TILE_PREF), _pick(k, _K_TILE_PREF), _pick(nn, _TILE_PREF)
    if shard_cols is None:
        assert out_row_off % tm == 0
        ro = out_row_off // tm
        out_shape = jax.ShapeDtypeStruct((m if out_rows is None else out_rows, n), out_dtype)
        out_spec = pl.BlockSpec((tm, tn), lambda i, j, kk: (i + ro, j))
    else:
        npb = nn // tn
        out_shape = jax.ShapeDtypeStruct((n // nn, m, nn), out_dtype)
        out_spec = pl.BlockSpec((None, tm, tn), lambda i, j, kk: (j // npb, i, j % npb))
    return _matmul(
        a, b, mode="tn", grid=(m // tm, n // tn, k // tk), tile=(tm, tn), name=name,
        a_spec=pl.BlockSpec((tk, tm), lambda i, j, kk: (kk, i)),
        b_spec=pl.BlockSpec((tk, tn), lambda i, j, kk: (kk, j)),
        out_shape=out_shape, out_spec=out_spec, into=into, after=after)


def _cast_bf16(x, name, after=None):
    r, c = x.shape
    tr = _pick(r, (512, 256, 128, 8))
    extra = [] if after is None else [after]

    def body(x_ref, *rest):
        rest[-1][...] = x_ref[...].astype(BF16)

    return pl.pallas_call(
        body, name=name, grid=(r // tr,),
        in_specs=[pl.BlockSpec((tr, c), lambda i: (i, 0))] + [_ANY] * len(extra),
        out_specs=pl.BlockSpec((tr, c), lambda i: (i, 0)),
        out_shape=jax.ShapeDtypeStruct((r, c), BF16),
        compiler_params=_cparams(("parallel",)),
    )(x, *extra)


def _bucket_tiles():
    qi = np.arange(ATTN_BLOCK)[:, None]
    ki = np.arange(2 * ATTN_BLOCK)[None, :]
    delta = np.clip(ATTN_BLOCK + qi - ki, 0, None)
    tiles = []
    max_exact = NUM_BUCKETS // 2
    for _, dil in ATTN_PATTERNS:
        dist = (delta * dil).astype(np.int32)
        d_f = np.maximum(dist, 1).astype(np.float32)
        large = max_exact + (np.log(d_f / np.float32(max_exact)) / np.float32(math.log(MAX_DISTANCE / max_exact))
                             * np.float32(NUM_BUCKETS - max_exact)).astype(np.int32)
        large = np.minimum(large, NUM_BUCKETS - 1)
        tiles.append(np.where(dist < max_exact, dist, large).astype(np.int32))
    return jnp.asarray(np.stack(tiles))


def _bias_expand(rel_bias, buckets, hpg):
    def body(tab_ref, bk_ref, o_ref):
        g, h = pl.program_id(0), pl.program_id(1)
        bk = bk_ref[...]
        acc = jnp.zeros((ATTN_BLOCK, 2 * ATTN_BLOCK), F32)
        for b in range(NUM_BUCKETS):
            acc = jnp.where(bk == b, tab_ref[b, g * hpg + h], acc)
        o_ref[...] = acc

    return pl.pallas_call(
        body, name="bias_expand", grid=(N_GROUPS_ATTN, hpg),
        in_specs=[pl.BlockSpec(memory_space=pltpu.SMEM),
                  pl.BlockSpec((None, ATTN_BLOCK, 2 * ATTN_BLOCK), lambda g, h: (g, 0, 0))],
        out_specs=pl.BlockSpec((None, None, ATTN_BLOCK, 2 * ATTN_BLOCK), lambda g, h: (g, h, 0, 0)),
        out_shape=jax.ShapeDtypeStruct((N_GROUPS_ATTN, hpg, ATTN_BLOCK, 2 * ATTN_BLOCK), F32),
        compiler_params=_cparams(("parallel", "parallel")),
    )(rel_bias, buckets)


def _bias_reduce(dtiles, buckets, hpg):
    def body(t_ref, bk_ref, o_ref):
        bk = bk_ref[...]
        t = t_ref[...]
        rows = lax.broadcasted_iota(jnp.int32, (NUM_BUCKETS, LANES), 0)
        acc = jnp.zeros((NUM_BUCKETS, LANES), F32)
        for b in range(NUM_BUCKETS):
            s = jnp.sum(jnp.sum(jnp.where(bk == b, t, 0.0), axis=1, keepdims=True), axis=0, keepdims=True)
            acc = jnp.where(rows == b, s, acc)
        o_ref[...] = acc

    return pl.pallas_call(
        body, name="bias_reduce", grid=(N_GROUPS_ATTN, hpg),
        in_specs=[pl.BlockSpec((None, None, ATTN_BLOCK, 2 * ATTN_BLOCK), lambda g, h: (g, h, 0, 0)),
                  pl.BlockSpec((None, ATTN_BLOCK, 2 * ATTN_BLOCK), lambda g, h: (g, 0, 0))],
        out_specs=pl.BlockSpec((None, NUM_BUCKETS, LANES), lambda g, h: (g * hpg + h, 0, 0)),
        out_shape=jax.ShapeDtypeStruct((N_GROUPS_ATTN * hpg, NUM_BUCKETS, LANES), F32),
        compiler_params=_cparams(("parallel", "parallel")),
    )(dtiles, buckets)


def _attn_valid(n_is_first):
    qi = lax.broadcasted_iota(jnp.int32, (ATTN_BLOCK, 2 * ATTN_BLOCK), 0)
    ki = lax.broadcasted_iota(jnp.int32, (ATTN_BLOCK, 2 * ATTN_BLOCK), 1)
    delta = ATTN_BLOCK + qi - ki
    band = (delta >= 0) & (delta <= ATTN_BLOCK)
    return band & (jnp.logical_not(n_is_first) | (ki >= ATTN_BLOCK))


def _attn_fwd(pg, bias, g, dil, hpg):
    s = pg.shape[0]
    w = hpg * HEAD_DIM
    rows = s // dil
    nb = rows // ATTN_BLOCK
    scale = HEAD_DIM ** -0.5

    def body(q_ref, kc_ref, kp_ref, vc_ref, vp_ref, bias_ref, o_ref, lse_ref):
        valid = _attn_valid(pl.program_id(1) == 0)
        for h in range(hpg):
            sl = slice(h * HEAD_DIM, (h + 1) * HEAD_DIM)
            k2 = jnp.concatenate([kp_ref[:, sl], kc_ref[:, sl]], axis=0)
            v2 = jnp.concatenate([vp_ref[:, sl], vc_ref[:, sl]], axis=0)
            sc = _dot_nt(q_ref[:, sl], k2) * scale + bias_ref[h]
            sc = jnp.where(valid, sc, NEG_INF)
            m = jnp.max(sc, axis=1, keepdims=True)
            p = jnp.exp(sc - m)
            l = jnp.sum(p, axis=1, keepdims=True)
            o_ref[:, sl] = _dot(p.astype(BF16), v2) * (1.0 / l)
            lse_ref[:, sl] = jnp.broadcast_to(m + jnp.log(l), (ATTN_BLOCK, HEAD_DIM))

    def col(off):
        return lambda r, n: (r * nb + n, off)

    def colp(off):
        return lambda r, n: (r * nb + jnp.maximum(n - 1, 0), off)

    blk = (ATTN_BLOCK, w)
    tok = pl.BlockSpec(blk, lambda r, n: (n, r))
    o, lse = pl.pallas_call(
        body, name=f"attn_fwd_g{g}", grid=(dil, nb),
        in_specs=[pl.BlockSpec(blk, col(0)), pl.BlockSpec(blk, col(1)), pl.BlockSpec(blk, colp(1)),
                  pl.BlockSpec(blk, col(2)), pl.BlockSpec(blk, colp(2)),
                  pl.BlockSpec((None, hpg, ATTN_BLOCK, 2 * ATTN_BLOCK), lambda r, n: (g, 0, 0, 0))],
        out_specs=[tok, tok],
        out_shape=[jax.ShapeDtypeStruct((rows, dil * w), F32), jax.ShapeDtypeStruct((rows, dil * w), F32)],
        compiler_params=_cparams(("parallel", "parallel")),
    )(pg, pg, pg, pg, pg, bias)
    return o.reshape(s, w), lse.reshape(s, w)


def _attn_combine(os_, lses, gate, hpg):
    s, w = os_[0].shape
    tm = _pick(s, (256, 128))

    def body(o0, o1, o2, l0, l1, l2, gate_ref, o_ref, lse_ref, y_ref):
        a0, a1, a2 = l0[...], l1[...], l2[...]
        m = jnp.maximum(jnp.maximum(a0, a1), a2)
        e0, e1, e2 = jnp.exp(a0 - m), jnp.exp(a1 - m), jnp.exp(a2 - m)
        den = e0 + e1 + e2
        o = (e0 * o0[...] + e1 * o1[...] + e2 * o2[...]) / den
        gate = gate_ref[...].astype(F32)
        o_ref[...] = o.astype(BF16)
        lse_ref[...] = m + jnp.log(den)
        y_ref[...] = (o * (gate * _sigmoid(gate))).astype(BF16)

    spec = pl.BlockSpec((tm, w), lambda i: (i, 0))
    return pl.pallas_call(
        body, name="attn_combine", grid=(s // tm,),
        in_specs=[spec] * 7,
        out_specs=[spec, spec, spec],
        out_shape=[jax.ShapeDtypeStruct((s, w), BF16), jax.ShapeDtypeStruct((s, w), F32),
                   jax.ShapeDtypeStruct((s, w), BF16)],
        compiler_params=_cparams(("parallel",)),
    )(*os_, *lses, gate)


def _attn_pre_bwd(dy, o, gate, hpg):
    s, w = dy.shape
    tm = _pick(s, (256, 128))

    def body(dy_ref, o_ref, gate_ref, do_ref, dl_ref, dg_ref):
        gate = gate_ref[...].astype(F32)
        sg = _sigmoid(gate)
        dyv = dy_ref[...].astype(F32)
        ov = o_ref[...].astype(F32)
        do = dyv * (gate * sg)
        do_ref[...] = do.astype(BF16)
        dg_ref[...] = (dyv * ov * (sg * (1.0 + gate * (1.0 - sg)))).astype(BF16)
        prod = do * ov
        for h in range(hpg):
            sl = slice(h * HEAD_DIM, (h + 1) * HEAD_DIM)
            dl_ref[:, sl] = jnp.broadcast_to(jnp.sum(prod[:, sl], axis=1, keepdims=True), (tm, HEAD_DIM))

    spec = pl.BlockSpec((tm, w), lambda i: (i, 0))
    return pl.pallas_call(
        body, name="attn_pre_bwd", grid=(s // tm,),
        in_specs=[spec, spec, spec],
        out_specs=[spec, spec, spec],
        out_shape=[jax.ShapeDtypeStruct((s, w), BF16), jax.ShapeDtypeStruct((s, w), F32),
                   jax.ShapeDtypeStruct((s, w), BF16)],
        compiler_params=_cparams(("parallel",)),
    )(dy, o, gate)


def _attn_bwd(pg, bias, do, lse, delta, g, dil, hpg):
    s = pg.shape[0]
    w = hpg * HEAD_DIM
    rows = s // dil
    nb = rows // ATTN_BLOCK
    dov, lsev, dlv = (t.reshape(rows, dil * w) for t in (do, lse, delta))
    scale = HEAD_DIM ** -0.5

    def body(q_ref, kc_ref, kp_ref, vc_ref, vp_ref, bias_ref, do_ref, lse_ref, dl_ref,
             dq_ref, dk_ref, dv_ref, db_ref, dkc_ref, dvc_ref):
        r, i = pl.program_id(0), pl.program_id(1)
        n = nb - 1 - i
        valid = _attn_valid(n == 0)

        @pl.when((r == 0) & (i == 0))
        def _():
            db_ref[...] = jnp.zeros_like(db_ref)

        @pl.when(i == 0)
        def _():
            dkc_ref[...] = jnp.zeros_like(dkc_ref)
            dvc_ref[...] = jnp.zeros_like(dvc_ref)

        for h in range(hpg):
            sl = slice(h * HEAD_DIM, (h + 1) * HEAD_DIM)
            q = q_ref[:, sl]
            dov_ = do_ref[:, sl]
            k2 = jnp.concatenate([kp_ref[:, sl], kc_ref[:, sl]], axis=0)
            v2 = jnp.concatenate([vp_ref[:, sl], vc_ref[:, sl]], axis=0)
            sc = _dot_nt(q, k2) * scale + bias_ref[h]
            p = jnp.exp(jnp.where(valid, sc - lse_ref[:, sl][:, 0:1], NEG_INF))
            dp = _dot_nt(dov_, v2)
            ds = p * (dp - dl_ref[:, sl][:, 0:1])
            db_ref[h] += ds
            dsb = ds.astype(BF16)
            dq_ref[:, sl] = (_dot(dsb, k2) * scale).astype(BF16)
            dk2 = _dot_tn(dsb, q) * scale
            dv2 = _dot_tn(p.astype(BF16), dov_)
            dk_ref[:, sl] = (dk2[ATTN_BLOCK:] + dkc_ref[:, sl]).astype(BF16)
            dv_ref[:, sl] = (dv2[ATTN_BLOCK:] + dvc_ref[:, sl]).astype(BF16)
            dkc_ref[:, sl] = dk2[:ATTN_BLOCK]
            dvc_ref[:, sl] = dv2[:ATTN_BLOCK]

    def col(off):
        return lambda r, i: (r * nb + nb - 1 - i, off)

    def colp(off):
        return lambda r, i: (r * nb + jnp.maximum(nb - 2 - i, 0), off)

    blk = (ATTN_BLOCK, w)
    tok = pl.BlockSpec(blk, lambda r, i: (nb - 1 - i, r))
    dq, dk, dv, db = pl.pallas_call(
        body, name=f"attn_bwd_g{g}", grid=(dil, nb),
        in_specs=[pl.BlockSpec(blk, col(0)), pl.BlockSpec(blk, col(1)), pl.BlockSpec(blk, colp(1)),
                  pl.BlockSpec(blk, col(2)), pl.BlockSpec(blk, colp(2)),
                  pl.BlockSpec((None, hpg, ATTN_BLOCK, 2 * ATTN_BLOCK), lambda r, i: (g, 0, 0, 0)),
                  tok, tok, tok],
        out_specs=[tok, tok, tok,
                   pl.BlockSpec((hpg, ATTN_BLOCK, 2 * ATTN_BLOCK), lambda r, i: (0, 0, 0))],
        out_shape=[jax.ShapeDtypeStruct((rows, dil * w), BF16)] * 3
        + [jax.ShapeDtypeStruct((hpg, ATTN_BLOCK, 2 * ATTN_BLOCK), F32)],
        scratch_shapes=[pltpu.VMEM(blk, F32), pltpu.VMEM(blk, F32)],
        compiler_params=_cparams(("arbitrary", "arbitrary")),
    )(pg, pg, pg, pg, pg, bias, dov, lsev, dlv)
    return dq.reshape(s, w), dk.reshape(s, w), dv.reshape(s, w), db


def _ln_fwd(xin, h, gamma, beta, name, affine_in=None, target=None):
    s, d = xin.shape
    tm = _pick(s, (128,))
    has_aff = affine_in is not None
    has_tgt = target is not None

    def body(*refs):
        it = iter(refs)
        x_ref, h_ref, g_ref, b_ref = next(it), next(it), next(it), next(it)
        if has_aff:
            gi_ref, bi_ref = next(it), next(it)
        if has_tgt:
            t_ref = next(it)
        xh_ref, rs_ref = next(it), next(it)
        x = x_ref[...]
        if has_aff:
            x = x * gi_ref[...] + bi_ref[...]
        u = DEEPNORM_ALPHA * x + h_ref[...]
        mu = jnp.mean(u, axis=1, keepdims=True)
        uc = u - mu
        var = jnp.mean(uc * uc, axis=1, keepdims=True)
        rstd = lax.rsqrt(var + LN_EPS)
        xhat = uc * rstd
        xh_ref[...] = xhat
        rs_ref[...] = rstd
        y = xhat * g_ref[...] + b_ref[...]
        if has_tgt:
            dy_ref, l_ref = next(it), next(it)
            e = y - t_ref[...]
            dy_ref[...] = e * (1.0 / d)
            l_ref[...] = jnp.sum(e * e, axis=1, keepdims=True)
        else:
            y_ref = next(it)
            y_ref[...] = y.astype(BF16)

    row = pl.BlockSpec((tm, d), lambda i: (i, 0))
    vec = pl.BlockSpec((1, d), lambda i: (0, 0))
    one = pl.BlockSpec((tm, 1), lambda i: (i, 0))
    in_specs = [row, row, vec, vec] + ([vec, vec] if has_aff else []) + ([row] if has_tgt else [])
    args = [xin, h, gamma, beta] + (list(affine_in) if has_aff else []) + ([target] if has_tgt else [])
    out_specs = [row, one] + ([row, one] if has_tgt else [row])
    out_shape = [jax.ShapeDtypeStruct((s, d), F32), jax.ShapeDtypeStruct((s, 1), F32)]
    out_shape += ([jax.ShapeDtypeStruct((s, d), F32), jax.ShapeDtypeStruct((s, 1), F32)] if has_tgt
                  else [jax.ShapeDtypeStruct((s, d), BF16)])
    return pl.pallas_call(
        body, name=name, grid=(s // tm,), in_specs=in_specs, out_specs=out_specs, out_shape=out_shape,
        compiler_params=_cparams(("parallel",)),
    )(*args)


def _ln_bwd(dy, xhat, rstd, gamma, name):
    s, d = dy.shape
    tm = _pick(s, (128,))

    def body(dy_ref, xh_ref, rs_ref, g_ref, du_ref, dub_ref, dg_ref, db_ref):
        @pl.when(pl.program_id(0) == 0)
        def _():
            dg_ref[...] = jnp.zeros_like(dg_ref)
            db_ref[...] = jnp.zeros_like(db_ref)

        dyv = dy_ref[...]
        xh = xh_ref[...]
        dg_ref[...] += jnp.sum(dyv * xh, axis=0, keepdims=True)
        db_ref[...] += jnp.sum(dyv, axis=0, keepdims=True)
        dxh = dyv * g_ref[...]
        m1 = jnp.mean(dxh, axis=1, keepdims=True)
        m2 = jnp.mean(dxh * xh, axis=1, keepdims=True)
        du = rs_ref[...] * (dxh - m1 - xh * m2)
        du_ref[...] = du
        dub_ref[...] = du.astype(BF16)

    row = pl.BlockSpec((tm, d), lambda i: (i, 0))
    vec = pl.BlockSpec((1, d), lambda i: (0, 0))
    one = pl.BlockSpec((tm, 1), lambda i: (i, 0))
    return pl.pallas_call(
        body, name=name, grid=(s // tm,), in_specs=[row, row, one, vec],
        out_specs=[row, row, vec, vec],
        out_shape=[jax.ShapeDtypeStruct((s, d), F32), jax.ShapeDtypeStruct((s, d), BF16),
                   jax.ShapeDtypeStruct((1, d), F32), jax.ShapeDtypeStruct((1, d), F32)],
        compiler_params=_cparams(("arbitrary",)),
    )(dy, xhat, rstd, gamma)


_HALO = 16
_STRIP = 16


def _strips(tm, fn, init, reverse=False):
    n = tm // _STRIP

    def step(i, carry):
        s_ = n - 1 - i if reverse else i
        return fn(pl.ds(pl.multiple_of(s_ * _STRIP, _STRIP), _STRIP), carry)

    return lax.fori_loop(0, n, step, init)


def _fold8(t):
    return t[0:8] + t[8:16]


def _conv_taps(ext, tm, w_ref):
    acc = None
    for k in range(CONV_WIDTH):
        lo = _HALO - (CONV_WIDTH - 1) + k
        term = w_ref[k:k + 1, :] * ext[lo:lo + tm, :]
        acc = term if acc is None else acc + term
    return acc


def _conv_strip(prev, cur, w_ref):
    ext = jnp.concatenate([prev, cur], axis=0)
    acc, taps = None, []
    for k in range(CONV_WIDTH):
        lo = _STRIP - (CONV_WIDTH - 1) + k
        taps.append(ext[lo:lo + _STRIP, :])
        term = w_ref[k:k + 1, :] * taps[k]
        acc = term if acc is None else acc + term
    return acc, taps


def _conv_fwd(pzx, conv_w, conv_b, d_inner):
    s, _ = pzx.shape
    cd = conv_w.shape[1]
    tm = _pick(s, (512, 256, 128))
    tc = _pick(cd, (1024, 512, 256, 128))
    off = d_inner // tc
    hb = tm // _HALO

    def body(x_ref, p_ref, w_ref, b_ref, o_ref):
        prev = jnp.where(pl.program_id(0) > 0, p_ref[...].astype(F32), 0.0)
        ext = jnp.concatenate([prev, x_ref[...].astype(F32)], axis=0)
        pre = _conv_taps(ext, tm, w_ref) + b_ref[...]
        o_ref[...] = (pre * _sigmoid(pre)).astype(BF16)

    return pl.pallas_call(
        body, name="conv_fwd", grid=(s // tm, cd // tc),
        in_specs=[pl.BlockSpec((tm, tc), lambda i, j: (i, off + j)),
                  pl.BlockSpec((_HALO, tc), lambda i, j: (jnp.maximum(i * hb - 1, 0), off + j)),
                  pl.BlockSpec((CONV_WIDTH, tc), lambda i, j: (0, j)),
                  pl.BlockSpec((1, tc), lambda i, j: (0, j))],
        out_specs=pl.BlockSpec((tm, tc), lambda i, j: (i, j)),
        out_shape=jax.ShapeDtypeStruct((s, cd), BF16),
        compiler_params=_cparams(("parallel", "parallel")),
    )(pzx, pzx, conv_w, conv_b)


def _conv_bwd_a(pzx, dxbc, conv_w, conv_b, d_inner):
    s, _ = pzx.shape
    cd = conv_w.shape[1]
    tm = _pick(s, (512, 256, 128))
    tc = _pick(cd, (1024, 512, 256, 128))
    off = d_inner // tc
    hb = tm // _HALO

    def body(x_ref, p_ref, d_ref, w_ref, b_ref, o_ref, dw_ref, db_ref, acc_ref):
        @pl.when(pl.program_id(1) == 0)
        def _():
            dw_ref[...] = jnp.zeros_like(dw_ref)
            db_ref[...] = jnp.zeros_like(db_ref)

        acc_ref[...] = jnp.zeros_like(acc_ref)

        def strip(rows, prev):
            cur = x_ref[rows, :].astype(F32)
            pre, taps = _conv_strip(prev, cur, w_ref)
            pre = pre + b_ref[...]
            sg = _sigmoid(pre)
            dpre = d_ref[rows, :].astype(F32) * (sg * (1.0 + pre * (1.0 - sg)))
            o_ref[rows, :] = dpre
            for k in range(CONV_WIDTH):
                acc_ref[k] += _fold8(dpre * taps[k])
            acc_ref[CONV_WIDTH] += _fold8(dpre)
            return cur

        _strips(tm, strip, jnp.where(pl.program_id(1) > 0, p_ref[...].astype(F32), 0.0))
        for k in range(CONV_WIDTH):
            dw_ref[k:k + 1, :] += jnp.sum(acc_ref[k], axis=0, keepdims=True)
        db_ref[...] += jnp.sum(acc_ref[CONV_WIDTH], axis=0, keepdims=True)

    return pl.pallas_call(
        body, name="conv_bwd_a", grid=(cd // tc, s // tm),
        in_specs=[pl.BlockSpec((tm, tc), lambda j, i: (i, off + j)),
                  pl.BlockSpec((_HALO, tc), lambda j, i: (jnp.maximum(i * hb - 1, 0), off + j)),
                  pl.BlockSpec((tm, tc), lambda j, i: (i, j)),
                  pl.BlockSpec((CONV_WIDTH, tc), lambda j, i: (0, j)),
                  pl.BlockSpec((1, tc), lambda j, i: (0, j))],
        out_specs=[pl.BlockSpec((tm, tc), lambda j, i: (i, j)),
                   pl.BlockSpec((CONV_WIDTH, tc), lambda j, i: (0, j)),
                   pl.BlockSpec((1, tc), lambda j, i: (0, j))],
        out_shape=[jax.ShapeDtypeStruct((s, cd), F32), jax.ShapeDtypeStruct((CONV_WIDTH, cd), F32),
                   jax.ShapeDtypeStruct((1, cd), F32)],
        scratch_shapes=[pltpu.VMEM((CONV_WIDTH + 1, 8, tc), F32)],
        compiler_params=_cparams(("parallel", "arbitrary")),
    )(pzx, pzx, dxbc, conv_w, conv_b)


def _conv_bwd_b(dpre, conv_w, into, col_off):
    s, cd = dpre.shape
    tm = _pick(s, (512, 256, 128))
    tc = _pick(cd, (1024, 512, 256, 128))
    hb = tm // 8
    nrb = s // tm
    assert col_off % tc == 0
    co = col_off // tc

    def body(x_ref, nx_ref, w_ref, into_ref, o_ref):
        nxt = jnp.where(pl.program_id(0) < nrb - 1, nx_ref[...], 0.0)
        ext = jnp.concatenate([x_ref[...], nxt], axis=0)
        acc = None
        for k in range(CONV_WIDTH):
            lo = CONV_WIDTH - 1 - k
            term = w_ref[k:k + 1, :] * ext[lo:lo + tm, :]
            acc = term if acc is None else acc + term
        o_ref[...] = acc.astype(BF16)

    return pl.pallas_call(
        body, name="conv_bwd_b", grid=(nrb, cd // tc),
        in_specs=[pl.BlockSpec((tm, tc), lambda i, j: (i, j)),
                  pl.BlockSpec((8, tc), lambda i, j: (jnp.minimum((i + 1) * hb, s // 8 - 1), j)),
                  pl.BlockSpec((CONV_WIDTH, tc), lambda i, j: (0, j)), _ANY],
        out_specs=pl.BlockSpec((tm, tc), lambda i, j: (i, j + co)),
        out_shape=jax.ShapeDtypeStruct(into.shape, BF16),
        input_output_aliases={3: 0},
        compiler_params=_cparams(("parallel", "parallel")),
    )(dpre, dpre, conv_w, into)


def _expand_matrix():
    e = np.zeros((LANES, SSM_GROUP_WIDTH), np.float32)
    for h in range(HEADS_PER_SSM_GROUP):
        e[h, h * SSM_HEAD_DIM:(h + 1) * SSM_HEAD_DIM] = 1.0
    return jnp.asarray(e, BF16)


def _expand(t, e):
    hi, lo = _split2(t)
    return _dot(hi, e) + _dot(lo, e)


def _segsum(v, e):
    hi, lo = _split2(v)
    return _dot_nt(hi, e) + _dot_nt(lo, e)


def _tri_dot(tri, x):
    hi, mid, lo = _split3(x)
    return _dot(tri, hi) + _dot(tri, mid) + _dot(tri, lo)


def _ssd_common(dtp_ref, a_ref, dtb_ref, x_ref, e):
    li = lax.broadcasted_iota(jnp.int32, (CHUNK, CHUNK), 0)
    si = lax.broadcasted_iota(jnp.int32, (CHUNK, CHUNK), 1)
    causal = li >= si
    tril = causal.astype(BF16)
    raw = dtp_ref[...] + dtb_ref[...]
    dt = jnp.maximum(raw, 0.0) + jnp.log(1.0 + jnp.exp(-jnp.abs(raw)))
    head_lane = lax.broadcasted_iota(jnp.int32, (1, LANES), 1) < HEADS_PER_SSM_GROUP
    a = jnp.where(head_lane, -jnp.exp(a_ref[...]), 0.0)
    a_cum = _tri_dot(tril, dt * a)
    a_cum_t = a_cum.T
    e_a = jnp.exp(a_cum)
    to_end = jnp.exp(a_cum[CHUNK - 1:CHUNK, :] - a_cum)
    x = x_ref[...].astype(F32)
    dt_e = _expand(dt, e)
    return dict(causal=causal, raw=raw, dt=dt, a=a, a_cum=a_cum, a_cum_t=a_cum_t, e_a=e_a,
                to_end=to_end, x=x, dt_e=dt_e, xdt=x * dt_e, e_a_e=_expand(e_a, e),
                to_end_e=_expand(to_end, e))


def _decay(q, h):
    seg = q["a_cum"][:, h:h + 1] - q["a_cum_t"][h:h + 1, :]
    return jnp.exp(jnp.where(q["causal"], seg, -jnp.inf))


def _ssd_specs(ng, d_inner, rev, nc):
    cidx = (lambda i: nc - 1 - i) if rev else (lambda i: i)
    boff = d_inner // D_STATE
    return dict(
        xs=pl.BlockSpec((CHUNK, SSM_GROUP_WIDTH), lambda g, i: (cidx(i), g)),
        b=pl.BlockSpec((CHUNK, D_STATE), lambda g, i: (cidx(i), boff + g)),
        c=pl.BlockSpec((CHUNK, D_STATE), lambda g, i: (cidx(i), boff + ng + g)),
        dtp=pl.BlockSpec((None, CHUNK, LANES), lambda g, i: (g, cidx(i), 0)),
        vec=pl.BlockSpec((None, 1, LANES), lambda g, i: (g, 0, 0)),
        wide=pl.BlockSpec((None, 1, SSM_GROUP_WIDTH), lambda g, i: (g, 0, 0)),
        e=pl.BlockSpec((LANES, SSM_GROUP_WIDTH), lambda g, i: (0, 0)),
        st=pl.BlockSpec((None, None, D_STATE, SSM_GROUP_WIDTH), lambda g, i: (g, cidx(i), 0, 0)),
        tok=pl.BlockSpec((CHUNK, SSM_GROUP_WIDTH), lambda g, i: (cidx(i), g)),
        bc_out=pl.BlockSpec((CHUNK, D_STATE), lambda g, i: (cidx(i), g)),
    )


def _ssd_fwd(xbc, dtp, a_pad, dtb_pad, dsk_e, e, d_inner):
    s = xbc.shape[0]
    ng = d_inner // SSM_GROUP_WIDTH
    nc = s // CHUNK

    def body(x_ref, b_ref, c_ref, dtp_ref, a_ref, dtb_ref, dsk_ref, e_ref, y_ref, st_ref, state):
        lane = lax.broadcasted_iota(jnp.int32, (CHUNK, LANES), 1)
        @pl.when(pl.program_id(1) == 0)
        def _():
            state[...] = jnp.zeros_like(state)

        ev = e_ref[...]
        q = _ssd_common(dtp_ref, a_ref, dtb_ref, x_ref, ev)
        bm, cm = b_ref[...], c_ref[...]
        cb = _dot_nt(cm, bm)
        s0 = state[...]
        st_ref[...] = s0
        y = _dot(cm, s0.astype(BF16)) * q["e_a_e"] + dsk_ref[...] * q["x"]
        xdt = q["xdt"]
        left = lane[:, :] < SSM_HEAD_DIM
        for j in range(HEADS_PER_SSM_GROUP // 2):
            sl = slice(j * LANES, (j + 1) * LANES)
            x2 = xdt[:, sl]
            m0 = (cb * _decay(q, 2 * j)).astype(BF16)
            m1 = (cb * _decay(q, 2 * j + 1)).astype(BF16)
            mcat = jnp.concatenate([m0, m1], axis=1)
            xbd = jnp.concatenate([jnp.where(left, x2, 0.0), jnp.where(left, 0.0, x2)], axis=0).astype(BF16)
            y_ref[:, sl] = (y[:, sl] + _dot(mcat, xbd)).astype(BF16)
        state[...] = s0 * q["e_a_e"][CHUNK - 1:CHUNK, :] + _dot_tn(bm, (q["to_end_e"] * xdt).astype(BF16))

    sp = _ssd_specs(ng, d_inner, False, nc)
    return pl.pallas_call(
        body, name="ssd_fwd", grid=(ng, nc),
        in_specs=[sp["xs"], sp["b"], sp["c"], sp["dtp"], sp["vec"], sp["vec"], sp["wide"], sp["e"]],
        out_specs=[sp["tok"], sp["st"]],
        out_shape=[jax.ShapeDtypeStruct((s, d_inner), BF16),
                   jax.ShapeDtypeStruct((ng, nc, D_STATE, SSM_GROUP_WIDTH), F32)],
        scratch_shapes=[pltpu.VMEM((D_STATE, SSM_GROUP_WIDTH), F32)],
        compiler_params=_cparams(("parallel", "arbitrary")),
    )(xbc, xbc, xbc, dtp, a_pad, dtb_pad, dsk_e, e)


def _ssd_bwd(xbc, dtp, a_pad, dtb_pad, dsk_e, e, states, dy, d_inner):
    s = xbc.shape[0]
    ng = d_inner // SSM_GROUP_WIDTH
    nc = s // CHUNK

    def body(x_ref, b_ref, c_ref, dtp_ref, a_ref, dtb_ref, dsk_ref, e_ref, st_ref, dy_ref,
             dx_ref, db_ref, dc_ref, ddt_ref, da_ref, ddtb_ref, dd_ref, dstate):
        lane = lax.broadcasted_iota(jnp.int32, (CHUNK, LANES), 1)
        sub = lax.broadcasted_iota(jnp.int32, (CHUNK, LANES), 0)
        @pl.when(pl.program_id(1) == 0)
        def _():
            dstate[...] = jnp.zeros_like(dstate)
            da_ref[...] = jnp.zeros_like(da_ref)
            ddtb_ref[...] = jnp.zeros_like(ddtb_ref)
            dd_ref[...] = jnp.zeros_like(dd_ref)

        ev = e_ref[...]
        q = _ssd_common(dtp_ref, a_ref, dtb_ref, x_ref, ev)
        bm, cm = b_ref[...], c_ref[...]
        cb = _dot_nt(cm, bm)
        x, xdt, e_a_e, to_end_e = q["x"], q["xdt"], q["e_a_e"], q["to_end_e"]
        s0 = st_ref[...]
        s0b = s0.astype(BF16)
        ds1 = dstate[...]
        ds1b = ds1.astype(BF16)
        dy = dy_ref[...].astype(F32)
        e_last_e = e_a_e[CHUNK - 1:CHUNK, :]

        dye = dy * e_a_e
        dyeb = dye.astype(BF16)
        cs0 = _dot(cm, s0b)
        dc = _dot_nt(dyeb, s0b)
        dstate[...] = e_last_e * ds1 + _dot_tn(cm, dyeb)
        da_col = _segsum(dye * cs0, ev)

        gmat = _dot(bm, ds1b)
        dxdt = to_end_e * gmat
        dte = _segsum(xdt * gmat, ev) * q["to_end"]
        db = _dot_nt((to_end_e * xdt).astype(BF16), ds1b)
        da_col = da_col - dte
        last_row = (jnp.sum(dte, axis=0, keepdims=True)
                    + q["e_a"][CHUNK - 1:CHUNK, :] * jnp.sum(_segsum(s0 * ds1, ev), axis=0, keepdims=True))

        left = lane < SSM_HEAD_DIM
        dcb = jnp.zeros((CHUNK, CHUNK), F32)
        row_acc = jnp.zeros((CHUNK, LANES), F32)
        for j in range(HEADS_PER_SSM_GROUP // 2):
            sl = slice(j * LANES, (j + 1) * LANES)
            x2 = xdt[:, sl].astype(BF16)
            dy2 = dy[:, sl]
            dyl = jnp.where(left, dy2, 0.0).astype(BF16)
            dyr = jnp.where(left, 0.0, dy2).astype(BF16)
            ms = []
            for hh, dyh in ((0, dyl), (1, dyr)):
                h = 2 * j + hh
                dec = _decay(q, h)
                m = cb * dec
                dm = _dot_nt(dyh, x2)
                dcb = dcb + dm * dec
                dseg = dm * m
                da_col = da_col + jnp.where(lane == h, jnp.sum(dseg, axis=1, keepdims=True), 0.0)
                row_acc = row_acc + jnp.where(sub == h, jnp.sum(dseg, axis=0, keepdims=True), 0.0)
                ms.append(m.astype(BF16))
            mst = jnp.concatenate(ms, axis=0)
            dyst = jnp.concatenate([dyl, dyr], axis=0)
            d2 = dxdt[:, sl] + _dot_tn(mst, dyst)
            dx_ref[:, sl] = (d2 * q["dt_e"][:, sl] + dsk_ref[:, sl] * dy2).astype(BF16)
            dxdt_x = d2 * x[:, sl]
            if j == 0:
                parts = [dxdt_x]
            else:
                parts.append(dxdt_x)
        dcbb = dcb.astype(BF16)
        dc_ref[...] = (dc + _dot(dcbb, bm)).astype(BF16)
        db_ref[...] = (db + _dot_tn(dcbb, cm)).astype(BF16)

        d_a = da_col - row_acc.T + jnp.where(sub == CHUNK - 1, last_row, 0.0)
        triu = (lax.broadcasted_iota(jnp.int32, (CHUNK, CHUNK), 1)
                >= lax.broadcasted_iota(jnp.int32, (CHUNK, CHUNK), 0)).astype(BF16)
        d_dta = _tri_dot(triu, d_a)
        ddt = d_dta * q["a"] + _segsum(jnp.concatenate(parts, axis=1), ev)
        ddt_raw = ddt * _sigmoid(q["raw"])
        ddt_ref[...] = ddt_raw
        da_ref[...] += jnp.sum(d_dta * q["dt"], axis=0, keepdims=True) * q["a"]
        ddtb_ref[...] += jnp.sum(ddt_raw, axis=0, keepdims=True)
        dd_ref[...] += jnp.sum(dy * x, axis=0, keepdims=True)

    sp = _ssd_specs(ng, d_inner, True, nc)
    return pl.pallas_call(
        body, name="ssd_bwd", grid=(ng, nc),
        in_specs=[sp["xs"], sp["b"], sp["c"], sp["dtp"], sp["vec"], sp["vec"], sp["wide"], sp["e"],
                  sp["st"], sp["tok"]],
        out_specs=[sp["tok"], sp["bc_out"], sp["bc_out"], sp["dtp"], sp["vec"], sp["vec"], sp["wide"]],
        out_shape=[jax.ShapeDtypeStruct((s, d_inner), BF16),
                   jax.ShapeDtypeStruct((s, ng * D_STATE), BF16),
                   jax.ShapeDtypeStruct((s, ng * D_STATE), BF16),
                   jax.ShapeDtypeStruct((ng, s, LANES), F32),
                   jax.ShapeDtypeStruct((ng, 1, LANES), F32),
                   jax.ShapeDtypeStruct((ng, 1, LANES), F32),
                   jax.ShapeDtypeStruct((ng, 1, SSM_GROUP_WIDTH), F32)],
        scratch_shapes=[pltpu.VMEM((D_STATE, SSM_GROUP_WIDTH), F32)],
        compiler_params=_cparams(("parallel", "arbitrary")),
    )(xbc, xbc, xbc, dtp, a_pad, dtb_pad, dsk_e, e, states, dy)


def _gate_norm_fwd(y, pzx, norm_w):
    s, di = y.shape
    ng = di // SSM_GROUP_WIDTH
    tm = _pick(s, (512, 256, 128))

    def body(y_ref, z_ref, w_ref, o_ref):
        z = z_ref[...].astype(F32)
        y2 = y_ref[...].astype(F32) * (z * _sigmoid(z))
        r = lax.rsqrt(jnp.mean(y2 * y2, axis=1, keepdims=True) + RMS_EPS)
        o_ref[...] = (y2 * r * w_ref[...]).astype(BF16)

    blk = pl.BlockSpec((tm, SSM_GROUP_WIDTH), lambda i, g: (i, g))
    return pl.pallas_call(
        body, name="gate_norm_fwd", grid=(s // tm, ng),
        in_specs=[blk, blk, pl.BlockSpec((1, SSM_GROUP_WIDTH), lambda i, g: (0, g))],
        out_specs=blk, out_shape=jax.ShapeDtypeStruct((s, di), BF16),
        compiler_params=_cparams(("parallel", "parallel")),
    )(y, pzx, norm_w)


def _gate_norm_bwd(dy3, y, pzx, norm_w):
    s, di = y.shape
    ng = di // SSM_GROUP_WIDTH
    tm = _pick(s, (512, 256, 128))

    def body(d_ref, y_ref, z_ref, w_ref, dy_ref, dz_ref, dw_ref):
        @pl.when(pl.program_id(1) == 0)
        def _():
            dw_ref[...] = jnp.zeros_like(dw_ref)

        z = z_ref[...].astype(F32)
        yv = y_ref[...].astype(F32)
        sg = _sigmoid(z)
        sz = z * sg
        y2 = yv * sz
        r = lax.rsqrt(jnp.mean(y2 * y2, axis=1, keepdims=True) + RMS_EPS)
        nrm = y2 * r
        d3 = d_ref[...].astype(F32)
        dw_ref[...] += jnp.sum(d3 * nrm, axis=0, keepdims=True)
        dn = d3 * w_ref[...]
        dy2 = r * (dn - nrm * jnp.mean(dn * nrm, axis=1, keepdims=True))
        dy_ref[...] = (dy2 * sz).astype(BF16)
        dz_ref[...] = (dy2 * yv * (sg * (1.0 + z * (1.0 - sg)))).astype(BF16)

    blk = pl.BlockSpec((tm, SSM_GROUP_WIDTH), lambda g, i: (i, g))
    vec = pl.BlockSpec((1, SSM_GROUP_WIDTH), lambda g, i: (0, g))
    return pl.pallas_call(
        body, name="gate_norm_bwd", grid=(ng, s // tm),
        in_specs=[blk, blk, blk, vec], out_specs=[blk, blk, vec],
        out_shape=[jax.ShapeDtypeStruct((s, di), BF16), jax.ShapeDtypeStruct(pzx.shape, BF16),
                   jax.ShapeDtypeStruct((1, di), F32)],
        compiler_params=_cparams(("parallel", "arbitrary")),
    )(dy3, y, pzx, norm_w)


_ANY = pl.BlockSpec(memory_space=pl.ANY)


def _place():
    x, y, c = lax.axis_index("x"), lax.axis_index("y"), lax.axis_index("c")
    chips = [(1 - x, y), (x, 1 - y), (1 - x, 1 - y)]
    return x, y, c, chips


def _cast_to_slot(x, kvec, name, after=None):
    r, cn = x.shape
    tr = _rows_per_block(r, cn)
    extra = [] if after is None else [after]

    def body(k_ref, x_ref, *rest):
        rest[-1][...] = x_ref[...].astype(BF16)

    grid_spec = pltpu.PrefetchScalarGridSpec(
        num_scalar_prefetch=1, grid=(r // tr,),
        in_specs=[pl.BlockSpec((tr, cn), lambda i, k: (i, 0))] + [_ANY] * len(extra),
        out_specs=pl.BlockSpec((None, tr, cn), lambda i, k: (k[0], i, 0)))
    return pl.pallas_call(
        body, name=name, grid_spec=grid_spec, out_shape=jax.ShapeDtypeStruct((N_CHIPS, r, cn), BF16),
        compiler_params=_cparams(("parallel",)),
    )(kvec, x, *extra)


def _swap_halves(gs, name, after=None):
    n = len(gs)
    extra = [] if after is None else [after]

    def body(*refs):
        ins, outs = refs[:n], refs[n + len(extra):2 * n + len(extra)]
        send_sems, recv_sems = refs[2 * n + len(extra):]
        x, y, c, _ = _place()
        cps = []
        for w in range(n):
            hr = gs[w].shape[1] // 2
            cp = pltpu.make_async_remote_copy(
                src_ref=ins[w].at[:, pl.ds((1 - c) * hr, hr)], dst_ref=outs[w],
                send_sem=send_sems.at[w], recv_sem=recv_sems.at[w],
                device_id=(x, y, 1 - c), device_id_type=MESH)
            cp.start()
            cps.append(cp)
        for cp in cps:
            cp.wait()

    return pl.pallas_call(
        body, name=name,
        in_specs=[_ANY] * (n + len(extra)), out_specs=[_ANY] * n,
        out_shape=[jax.ShapeDtypeStruct((g.shape[0], g.shape[1] // 2, g.shape[2]), g.dtype) for g in gs],
        scratch_shapes=[pltpu.SemaphoreType.DMA((n,)), pltpu.SemaphoreType.DMA((n,))],
    )(*gs, *extra)


def _join_halves(fs, name, after=None):
    n = len(fs)
    extra = [] if after is None else [after]

    def body(*refs):
        outs = refs[n + len(extra):2 * n + len(extra)]
        send_sems, recv_sems = refs[2 * n + len(extra):]
        x, y, c, _ = _place()

        def copy(w, hc):
            hr = fs[w].shape[0] // 2
            rows = outs[w].at[pl.ds(hc * hr, hr)]
            return pltpu.make_async_remote_copy(
                src_ref=rows, dst_ref=rows, send_sem=send_sems.at[w], recv_sem=recv_sems.at[w],
                device_id=(x, y, 1 - c), device_id_type=MESH)

        cps = [copy(w, c) for w in range(n)]
        for cp in cps:
            cp.start()
        for w in range(n):
            copy(w, 1 - c).wait_recv()
        for cp in cps:
            cp.wait_send()

    return pl.pallas_call(
        body, name=name,
        in_specs=[_ANY] * (n + len(extra)), out_specs=[_ANY] * n,
        out_shape=[jax.ShapeDtypeStruct(f.shape, f.dtype) for f in fs],
        input_output_aliases={w: w for w in range(n)},
        scratch_shapes=[pltpu.SemaphoreType.DMA((n,)), pltpu.SemaphoreType.DMA((n,))],
    )(*fs, *extra)


_HBM_SPEC = pl.BlockSpec(memory_space=pltpu.HBM)
_SEM_SPEC = pl.BlockSpec(memory_space=pltpu.SEMAPHORE)
_VMEM_SPEC = pl.BlockSpec(memory_space=pltpu.VMEM)
_EFFECT = pltpu.SideEffectType.DATAFLOW_SIDE_EFFECTING
_TOKEN = jax.ShapeDtypeStruct((8, LANES), F32)


def _hbm(a):
    return pltpu.with_memory_space_constraint(a, pltpu.HBM)


def _gather_copies(bufs, refs, send_sems, recv_sems, forward, arrivals=True):
    x, y, c, chips = _place()
    k = 2 * x + y
    out, arrive = [], []
    for w, ref in enumerate(refs):
        hr = bufs[w].shape[1] // 2
        for j, (cx, cy) in enumerate(chips):
            kj = 2 * cx + cy
            slot_out, slot_in, half_in = (kj, kj, 1 - c) if forward else (k, kj, c)
            to = (x, y, 1 - c) if forward else (cx, cy, c)
            src = ref.at[slot_out, pl.ds(c * hr, hr)]
            land = ref.at[slot_in, pl.ds(half_in * hr, hr)]
            out.append(pltpu.make_async_remote_copy(
                src_ref=src, dst_ref=src, send_sem=send_sems.at[3 * w + j], recv_sem=recv_sems.at[3 * w + j],
                device_id=to, device_id_type=MESH))
            if arrivals:
                arrive.append(pltpu.make_async_remote_copy(
                    src_ref=land, dst_ref=land, send_sem=send_sems.at[3 * w + j], recv_sem=recv_sems.at[3 * w + j],
                    device_id=to, device_id_type=MESH))
    return out, arrive


def _gather_start(bufs, forward, name, after=None):
    n = len(bufs)
    extra = [] if after is None else [after]

    def body(*refs):
        ins = refs[:n]
        send_sems, recv_sems = refs[n + len(extra)], refs[n + len(extra) + 1]
        token = refs[-1]
        out, _ = _gather_copies(bufs, ins, send_sems, recv_sems, forward, arrivals=False)
        for cp in out:
            cp.start()
        token[...] = jnp.zeros_like(token)

    res = pl.pallas_call(
        body, name=name,
        out_shape=(pltpu.SemaphoreType.DMA((3 * n,)), pltpu.SemaphoreType.DMA((3 * n,)))
        + tuple(pltpu.HBM(b.shape, b.dtype) for b in bufs) + (_TOKEN,),
        in_specs=(_HBM_SPEC,) * n + (_ANY,) * len(extra),
        out_specs=(_SEM_SPEC, _SEM_SPEC) + (_HBM_SPEC,) * n + (_VMEM_SPEC,),
        input_output_aliases={w: 2 + w for w in range(n)},
        compiler_params=pltpu.CompilerParams(has_side_effects=_EFFECT),
    )(*[_hbm(b) for b in bufs], *extra)
    return res[0], res[1], list(res[2:2 + n]), res[-1]


def _gather_wait(bufs, send_sems, recv_sems, after, forward, name):
    n = len(bufs)

    def body(*refs):
        ins = refs[:n]
        send_sems, recv_sems = refs[n], refs[n + 1]
        out, arrive = _gather_copies(bufs, ins, send_sems, recv_sems, forward)
        for cp in out:
            cp.wait_send()
        for cp in arrive:
            cp.wait_recv()

    res = pl.pallas_call(
        body, name=name,
        out_shape=tuple(pltpu.HBM(b.shape, b.dtype) for b in bufs),
        in_specs=(_HBM_SPEC,) * n + (_SEM_SPEC, _SEM_SPEC, _ANY), out_specs=(_HBM_SPEC,) * n,
        input_output_aliases={w: w for w in range(n)},
        compiler_params=pltpu.CompilerParams(has_side_effects=_EFFECT),
    )(*bufs, send_sems, recv_sems, after)
    return list(res)


def _scatter_copies(t_ref, land_ref, send_sems, recv_sems, arrivals=True):
    x, y, c, chips = _place()
    k = 2 * x + y
    out, arrive = [], []
    for j, (cx, cy) in enumerate(chips):
        kj = 2 * cx + cy
        out.append(pltpu.make_async_remote_copy(
            src_ref=t_ref.at[kj], dst_ref=land_ref.at[k], send_sem=send_sems.at[j], recv_sem=recv_sems.at[j],
            device_id=(cx, cy, c), device_id_type=MESH))
        if arrivals:
            arrive.append(pltpu.make_async_remote_copy(
                src_ref=t_ref.at[kj], dst_ref=land_ref.at[kj], send_sem=send_sems.at[j], recv_sem=recv_sems.at[j],
                device_id=(cx, cy, c), device_id_type=MESH))
    return out, arrive


def _scatter_start(t, name):
    def body(t_ref, land_ref, send_sems, recv_sems, t_thru, land_thru, token):
        out, _ = _scatter_copies(t_ref, land_ref, send_sems, recv_sems, arrivals=False)
        for cp in out:
            cp.start()
        token[...] = jnp.zeros_like(token)

    return pl.pallas_call(
        body, name=name,
        out_shape=(pltpu.SemaphoreType.DMA((3,)), pltpu.SemaphoreType.DMA((3,)),
                   pltpu.HBM(t.shape, t.dtype), pltpu.HBM(t.shape, t.dtype), _TOKEN),
        in_specs=(_HBM_SPEC, _HBM_SPEC), out_specs=(_SEM_SPEC, _SEM_SPEC, _HBM_SPEC, _HBM_SPEC, _VMEM_SPEC),
        input_output_aliases={0: 2, 1: 3},
        compiler_params=pltpu.CompilerParams(has_side_effects=_EFFECT),
    )(_hbm(t), _hbm(lax.empty(t.shape, t.dtype)))


def _scatter_wait(send_sems, recv_sems, t_thru, land_thru, after, name):
    def body(t_ref, land_ref, send_sems, recv_sems, after_ref, t_out, land_out):
        out, arrive = _scatter_copies(t_ref, land_ref, send_sems, recv_sems)
        for cp in out:
            cp.wait_send()
        for cp in arrive:
            cp.wait_recv()

    return pl.pallas_call(
        body, name=name,
        out_shape=(pltpu.HBM(t_thru.shape, t_thru.dtype), pltpu.HBM(land_thru.shape, land_thru.dtype)),
        in_specs=(_HBM_SPEC, _HBM_SPEC, _SEM_SPEC, _SEM_SPEC, _ANY), out_specs=(_HBM_SPEC, _HBM_SPEC),
        input_output_aliases={0: 0, 1: 1},
        compiler_params=pltpu.CompilerParams(has_side_effects=_EFFECT),
    )(t_thru, land_thru, send_sems, recv_sems, after)


def _all_gather_small(v, reduce, name):
    r, l = v.shape

    def body(v_ref, o_ref, *rest):
        if reduce:
            buf, send_sems, recv_sems = rest
        else:
            buf = o_ref
            send_sems, recv_sems = rest
        x, y, c, _ = _place()
        me = 4 * x + 2 * y + c
        buf[me] = v_ref[...]
        cps = []
        for d in range(1, N_DEV):
            peer = (x if d & 4 == 0 else 1 - x, y if d & 2 == 0 else 1 - y, c if d & 1 == 0 else 1 - c)
            cp = pltpu.make_async_remote_copy(
                src_ref=v_ref, dst_ref=buf.at[me], send_sem=send_sems.at[d - 1], recv_sem=recv_sems.at[d - 1],
                device_id=peer, device_id_type=MESH)
            cp.start()
            cps.append((cp, peer))
        for d, (cp, (px, py, pc)) in enumerate(cps, start=1):
            pltpu.make_async_remote_copy(
                src_ref=v_ref, dst_ref=buf.at[4 * px + 2 * py + pc], send_sem=send_sems.at[d - 1],
                recv_sem=recv_sems.at[d - 1], device_id=(px, py, pc), device_id_type=MESH).wait_recv()
        for cp, _ in cps:
            cp.wait_send()
        if reduce:
            acc = buf[0]
            for i in range(1, N_DEV):
                acc = acc + buf[i]
            o_ref[...] = acc

    vm = pl.BlockSpec(memory_space=pltpu.VMEM)
    out_shape = jax.ShapeDtypeStruct((r, l) if reduce else (N_DEV, r, l), F32)
    scratch = ([pltpu.VMEM((N_DEV, r, l), F32)] if reduce else []) + [
        pltpu.SemaphoreType.DMA((N_DEV - 1,)), pltpu.SemaphoreType.DMA((N_DEV - 1,))]
    return pl.pallas_call(
        body, name=name, in_specs=[vm], out_specs=vm, out_shape=out_shape, scratch_shapes=scratch,
    )(v)


_BLOCK_BYTES = 3 * 512 * 1024


def _rows_per_block(r, cn, itemsize=4):
    best = 8
    for t in range(8, r + 1, 8):
        if r % t == 0 and t * cn * itemsize <= _BLOCK_BYTES:
            best = t
    return best


def _add_sibling_half(g4, recv, cvec, name):
    ns, r, cn = g4.shape
    hr = r // 2
    tr = _rows_per_block(hr, cn)
    nrb = hr // tr

    def body(c_ref, a_ref, b_ref, o_ref):
        o_ref[...] = (a_ref[...].astype(F32) + b_ref[...].astype(F32)).astype(o_ref.dtype)

    grid_spec = pltpu.PrefetchScalarGridSpec(
        num_scalar_prefetch=1, grid=(ns, nrb),
        in_specs=[pl.BlockSpec((None, tr, cn), lambda j, i, c: (j, c[0] * nrb + i, 0)),
                  pl.BlockSpec((None, tr, cn), lambda j, i, c: (j, i, 0))],
        out_specs=pl.BlockSpec((None, tr, cn), lambda j, i, c: (j, i, 0)))
    return pl.pallas_call(
        body, name=name, grid_spec=grid_spec, out_shape=jax.ShapeDtypeStruct((ns, hr, cn), BF16),
        compiler_params=_cparams(("parallel", "parallel")),
    )(cvec, g4, recv)


def _sum_chips(r4, t4, kvec, cvec, name):
    ns, hr, cn = r4.shape
    tr = _rows_per_block(hr, cn)
    nrb = hr // tr

    def body(k_ref, c_ref, r_ref, t_ref, o_ref):
        acc = t_ref[...].astype(F32)
        for dlt in range(1, ns):
            acc = acc + r_ref[(k_ref[0] + dlt) % ns].astype(F32)
        o_ref[...] = acc

    grid_spec = pltpu.PrefetchScalarGridSpec(
        num_scalar_prefetch=2, grid=(nrb,),
        in_specs=[pl.BlockSpec((ns, tr, cn), lambda i, k, c: (0, i, 0)),
                  pl.BlockSpec((None, tr, cn), lambda i, k, c: (k[0], i, 0))],
        out_specs=pl.BlockSpec((tr, cn), lambda i, k, c: (c[0] * nrb + i, 0)))
    return pl.pallas_call(
        body, name=name, grid_spec=grid_spec, out_shape=jax.ShapeDtypeStruct((2 * hr, cn), F32),
        compiler_params=_cparams(("parallel",)),
    )(kvec, cvec, r4, t4)


def _adamw(w, g, m, v, name):
    r, cn = w.shape
    tr = _rows_per_block(r, cn)
    c1 = 1.0 - ADAM_B1 ** ADAM_STEP
    c2 = 1.0 - ADAM_B2 ** ADAM_STEP

    def body(w_ref, g_ref, m_ref, v_ref, go_ref, d_ref, mo_ref, vo_ref):
        gv = g_ref[...]
        mn = ADAM_B1 * m_ref[...] + (1.0 - ADAM_B1) * gv
        vn = ADAM_B2 * v_ref[...] + (1.0 - ADAM_B2) * (gv * gv)
        go_ref[...] = gv
        mo_ref[...] = mn
        vo_ref[...] = vn
        d_ref[...] = -ADAM_LR * ((mn / c1) / (jnp.sqrt(vn / c2) + ADAM_EPS) + ADAM_WD * w_ref[...])

    spec = pl.BlockSpec((tr, cn), lambda i: (i, 0))
    return pl.pallas_call(
        body, name=name, grid=(r // tr,), in_specs=[spec] * 4, out_specs=[spec] * 4,
        out_shape=[jax.ShapeDtypeStruct((r, cn), F32)] * 4,
        compiler_params=_cparams(("parallel",)),
    )(w, g, m, v)


def _pack(arrs):
    flat = jnp.concatenate([a.reshape(-1).astype(F32) for a in arrs])
    n = flat.shape[0]
    tot = -(-n // (8 * LANES)) * (8 * LANES)
    return jnp.pad(flat, (0, tot - n)).reshape(tot // LANES, LANES)


def _unpack(packed, shapes):
    flat = packed.reshape(-1)
    out, off = [], 0
    for shp in shapes:
        sz = int(np.prod(shp))
        out.append(flat[off:off + sz].reshape(shp))
        off += sz
    return out


class _LocalExchange:
    def __init__(self, ws4, wos4):
        self.ssm = [ws4, wos4]
        self.grads = {}

    def ssm_gather_start(self):
        return None

    def ssm_gather_mid(self, after):
        return None

    def ssm_gather_end(self, after):
        return self.ssm

    def grad_ready(self, name, g4, after=None):
        self.grads[name] = g4
        return None

    def grad_sync(self, name, after):
        pass

    def small_grads(self, small_full):
        self.small = small_full
        return None


class _Exchange:
    def __init__(self, kvec, cvec, ssm_bufs, after):
        self.kvec, self.cvec, self.bufs, self.after = kvec, cvec, ssm_bufs, after
        self.pending, self.summed, self.last_token = {}, {}, None

    def ssm_gather_start(self):
        self.sems = _gather_start(self.bufs, False, "ssm_gather_ici_start", self.after)
        self.bufs = self.sems[2]
        return self.sems[3]

    def ssm_gather_mid(self, after):
        bufs = _gather_wait(self.bufs, self.sems[0], self.sems[1], after, False, "ssm_gather_ici_wait")
        self.sems = _gather_start(bufs, True, "ssm_gather_fwd_start")
        self.bufs = self.sems[2]
        return self.sems[3]

    def ssm_gather_end(self, after):
        return _gather_wait(self.bufs, self.sems[0], self.sems[1], after, True, "ssm_gather_fwd_wait")

    def small_grads(self, small_full):
        packed = _all_gather_small(_pack(small_full), True, "reduce_small_grads")
        self.small = _unpack(packed, [t.shape for t in small_full])
        return packed

    def grad_ready(self, name, g4, after=None):
        recv = _swap_halves([g4], "grads_to_sibling_" + name, after)[0]
        t = _add_sibling_half(g4, recv, self.cvec, "add_sibling_" + name)
        send_sems, recv_sems, t_thru, land, token = _scatter_start(t, "scatter_start_" + name)
        self.pending[name] = (send_sems, recv_sems, t_thru, land)
        self.last_token = token
        return token

    def grad_sync(self, name, after):
        t, land = _scatter_wait(*self.pending.pop(name), after, "scatter_wait_" + name)
        self.summed[name] = _sum_chips(land, t, self.kvec, self.cvec, "sum_chips_" + name)


def _tie(vec, token):
    return vec if token is None else vec + token[0:1, 0:1].reshape((1,) * vec.ndim).astype(vec.dtype)


def _local_step(x2, xb, tgt, wa4, woa4, ex, conv_w_f, conv_b_f, norm_w_f, rel_bias, dt_bias, a_log, d_skip,
                ln_g, ln_b):
    s, d = x2.shape
    d_attn = woa4.shape[1]
    hpg = d_attn // HEAD_DIM
    d_inner = norm_w_f.shape[1]
    ng = d_inner // SSM_GROUP_WIDTH
    n_heads = dt_bias.shape[1]
    conv_dim = conv_w_f.shape[1]
    assert n_heads == ng * HEADS_PER_SSM_GROUP and conv_dim == d_inner + 2 * ng * D_STATE
    assert wa4.shape[2] * N_CHIPS == 10 * d_attn

    tok = ex.ssm_gather_start()
    buckets = _bucket_tiles()
    bias = _bias_expand(rel_bias, buckets, hpg)
    pgs, og, lg = [], [], []
    for g, (_, dil) in enumerate(ATTN_PATTERNS):
        pg = _mm_nn_sharded(xb, wa4, BF16, f"mm_in_attn_g{g}", after=tok,
                            col_off=3 * g * d_attn, n=3 * d_attn, classes=dil)
        o_, l_ = _attn_fwd(pg, bias, g, dil, hpg)
        pgs.append(pg)
        og.append(o_)
        lg.append(l_)
    gate = _mm_nn_sharded(xb, wa4, BF16, "mm_in_attn_gate", col_off=9 * d_attn, n=d_attn)
    o, lse, yat = _attn_combine(og, lg, gate, hpg)
    h0 = _mm_nn_sharded(yat, woa4, F32, "mm_out_attn", after=ex.ssm_gather_mid(yat))
    g0, b0, g1, b1 = ln_g[0:1], ln_b[0:1], ln_g[1:2], ln_b[1:2]
    xhat0, rstd0, x1b = _ln_fwd(x2, h0, g0, b0, "ln0_fwd")

    wst4, wos4 = ex.ssm_gather_end(x1b)
    wst = wst4.reshape(N_CHIPS * wst4.shape[1], d)
    nzx = d_inner + conv_dim
    wos = wos4.reshape(d_inner, d)
    pzx = _mm_nt(x1b, wst, BF16, "mm_in_ssm", n=nzx)
    dt_raw = _mm_nt(x1b, wst, F32, "mm_in_dt", n=n_heads, b_row_off=nzx)

    def pad_heads(t):
        t = t.reshape(t.shape[0], ng, HEADS_PER_SSM_GROUP).transpose(1, 0, 2)
        return jnp.pad(t, ((0, 0), (0, 0), (0, LANES - HEADS_PER_SSM_GROUP)))

    def unpad_heads(t):
        return t[:, :, :HEADS_PER_SSM_GROUP].transpose(1, 0, 2).reshape(t.shape[1], n_heads)

    dtp = pad_heads(dt_raw)
    alog_p, dtb_p = pad_heads(a_log), pad_heads(dt_bias)
    dsk_e = jnp.repeat(d_skip.reshape(ng, 1, HEADS_PER_SSM_GROUP), SSM_HEAD_DIM, axis=2)
    e = _expand_matrix()
    xbc = _conv_fwd(pzx, conv_w_f, conv_b_f, d_inner)
    y_ssd, states = _ssd_fwd(xbc, dtp, alog_p, dtb_p, dsk_e, e, d_inner)
    y3 = _gate_norm_fwd(y_ssd, pzx, norm_w_f)
    h1 = _mm_nn(y3, wos, F32, "mm_out_ssm")
    xhat1, rstd1, dy2, row_sq = _ln_fwd(xhat0, h1, g1, b1, "ln1_fwd_loss", affine_in=(g0, b0), target=tgt)
    loss_local = 0.5 * jnp.sum(row_sq) / d

    du1, du1b, dg1, db1 = _ln_bwd(dy2, xhat1, rstd1, g1, "ln1_bwd")
    dy3 = _mm_nt(du1b, wos, BF16, "mm_d_y3")
    g_wos = _mm_tn(y3, du1b, BF16, "mm_g_w_out_ssm").reshape(N_CHIPS, d_inner // N_CHIPS, d)
    norm_w_t = _tie(norm_w_f, ex.grad_ready("w_out_ssm", g_wos))
    dy_ssd, dz, d_nw = _gate_norm_bwd(dy3, y_ssd, pzx, norm_w_t)
    dxs, dbm, dcm, ddtp, d_alog, d_dtb, d_dsk = _ssd_bwd(xbc, dtp, alog_p, dtb_p, dsk_e, e, states, dy_ssd, d_inner)
    dpre, d_cw, d_cb = _conv_bwd_a(pzx, jnp.concatenate([dxs, dbm, dcm], axis=1), conv_w_f, conv_b_f, d_inner)
    dpzx = _conv_bwd_b(dpre, conv_w_f, dz, d_inner)
    ddt_raw = unpad_heads(ddtp)
    t1 = _mm_nn(ddt_raw, wst, F32, "mm_d_x1_dt", b_row_off=nzx, add=du1, add_scale=DEEPNORM_ALPHA)
    dx1 = _mm_nn(dpzx, wst, F32, "mm_d_x1", add=t1)
    ex.grad_sync("w_out_ssm", dx1)
    g_wst = _mm_tn(dpzx, x1b, BF16, "mm_g_w_in_ssm", out_rows=wst.shape[0])
    g_wst = _mm_tn(ddt_raw, x1b, BF16, "mm_g_w_dt", out_rows=wst.shape[0], out_row_off=nzx, into=g_wst)
    g0_t = _tie(g0, ex.grad_ready("w_in_ssm", g_wst.reshape(wst4.shape)))

    du0, du0b, dg0, db0 = _ln_bwd(dx1, xhat0, rstd0, g0_t, "ln0_bwd")
    dyat = _mm_nt_sharded_k(du0b, woa4, BF16, "mm_d_yat")
    g_woa = _mm_tn(yat, du0b, BF16, "mm_g_w_out_attn", shard_cols=d // N_CHIPS)
    tok_woa = ex.grad_ready("w_out_attn", g_woa)
    do, delta, dgate = _attn_pre_bwd(dyat, o, gate, hpg)
    pieces, dbt = [], []
    for g, (_, dil) in enumerate(ATTN_PATTERNS):
        dq, dk, dv, db_ = _attn_bwd(pgs[g], bias, do, lse, delta, g, dil, hpg)
        pieces += [dq, dk, dv]
        dbt.append(db_)
    dpa = jnp.concatenate(pieces + [dgate], axis=1)
    g_wa = _mm_tn(xb, dpa, BF16, "mm_g_w_in_attn", shard_cols=wa4.shape[2], after=tok_woa)
    ex.grad_sync("w_in_ssm", g_wa)
    ex.grad_sync("w_out_attn", g_wa)
    d_rel = _bias_reduce(jnp.stack(dbt), buckets, hpg)[:, :, 0].T
    d_dsk_h = d_dsk.reshape(n_heads, SSM_HEAD_DIM).sum(axis=1)
    small_full = [d_rel, d_cw, d_cb, unpad_heads(d_dtb), unpad_heads(d_alog), d_dsk_h[None], d_nw,
                  jnp.concatenate([dg0, dg1], axis=0), jnp.concatenate([db0, db1], axis=0)]
    tok_wa = ex.grad_ready("w_in_attn", g_wa, after=ex.small_grads(small_full))
    grad_x = _mm_nt_sharded_k(dpa, wa4, F32, "mm_d_x0", add=du0, add_scale=DEEPNORM_ALPHA, after=tok_wa)
    return loss_local, grad_x[None]


def kernel(x, w_in_attn, w_out_attn, rel_bias, w_in_ssm, conv_w, conv_b, dt_bias, a_log, d_skip, ssm_norm_w, w_out_ssm, ln_g, ln_b, loss_target, m_w_in_attn, m_w_out_attn, m_rel_bias, m_w_in_ssm, m_conv_w, m_conv_b, m_dt_bias, m_a_log, m_d_skip, m_ssm_norm_w, m_w_out_ssm, m_ln_g, m_ln_b, v_w_in_attn, v_w_out_attn, v_rel_bias, v_w_in_ssm, v_conv_w, v_conv_b, v_dt_bias, v_a_log, v_d_skip, v_ssm_norm_w, v_w_out_ssm, v_ln_g, v_ln_b):
    xi, yi, ci = lax.axis_index("x"), lax.axis_index("y"), lax.axis_index("c")
    chip = 2 * xi + yi
    cvec = jnp.reshape(ci, (1,)).astype(jnp.int32)
    kvec = jnp.reshape(chip, (1,)).astype(jnp.int32)

    cw_l, cb_l, nw_l = conv_w[0], conv_b[0], ssm_norm_w[0]
    vec_shapes = [cw_l.shape, cb_l.shape, nw_l.shape]
    vec_all = _all_gather_small(_pack([cw_l, cb_l, nw_l]), False, "gather_vectors")
    l0 = [_cast_to_slot(w_in_attn[0], kvec, "cast_w_in_attn"), _cast_to_slot(w_out_attn[0], kvec, "cast_w_out_attn")]
    send0, recv0, l0, tok0 = _gather_start(l0, False, "l0_gather_ici_start", vec_all)
    xb = _cast_bf16(x[0], "cast_x", tok0)
    ssm_bufs = [_cast_to_slot(w_in_ssm[0].T, kvec, "cast_w_in_ssm", tok0),
                _cast_to_slot(w_out_ssm[0], kvec, "cast_w_out_ssm", tok0)]
    l0 = _gather_wait(l0, send0, recv0, xb, False, "l0_gather_ici_wait")
    send0, recv0, l0, _ = _gather_start(l0, True, "l0_gather_fwd_start")
    wa4, woa4 = _gather_wait(l0, send0, recv0, ssm_bufs[0], True, "l0_gather_fwd_wait")
    ex = _Exchange(kvec, cvec, ssm_bufs, after=woa4)
    parts = [_unpack(vec_all[2 * j], vec_shapes) for j in range(N_CHIPS)]
    conv_w_f = jnp.concatenate([p[0] for p in parts], axis=1)
    conv_b_f = jnp.concatenate([p[1] for p in parts], axis=0)[None]
    norm_w_f = jnp.concatenate([p[2] for p in parts], axis=0)[None]

    loss_local, grad_x = _local_step(
        x[0], xb, loss_target[0], wa4, woa4, ex, conv_w_f, conv_b_f, norm_w_f, rel_bias, dt_bias, a_log,
        d_skip, ln_g, ln_b)
    loss = lax.psum(loss_local, ("x", "y", "c"))

    big_w = dict(w_in_attn=(w_in_attn, m_w_in_attn, v_w_in_attn), w_out_attn=(w_out_attn, m_w_out_attn, v_w_out_attn),
                 w_in_ssm=(w_in_ssm, m_w_in_ssm, v_w_in_ssm), w_out_ssm=(w_out_ssm, m_w_out_ssm, v_w_out_ssm))
    big = {}

    def finish(names, join_name, after):
        last = None
        for nm, gf in zip(names, _join_halves([ex.summed[nm] for nm in names], join_name, after)):
            flip = (lambda t: t.T) if nm == "w_in_ssm" else (lambda t: t)
            w_, m_, v_ = (flip(t[0]) for t in big_w[nm])
            res = _adamw(w_, gf, m_, v_, "adamw_" + nm)
            big[nm] = [flip(t)[None] for t in res]
            last = res[3]
        return last

    last = finish(["w_out_ssm", "w_in_ssm", "w_out_attn"], "grads_join_halves_a", ex.last_token)
    ex.grad_sync("w_in_attn", last)
    finish(["w_in_attn"], "grads_join_halves_b", None)

    s_rel, s_cw, s_cb, s_dtb, s_alog, s_dsk, s_nw, s_lng, s_lnb = ex.small
    cwc, nwc = conv_w.shape[2], ssm_norm_w.shape[1]
    s_cw = lax.dynamic_slice_in_dim(s_cw, chip * cwc, cwc, axis=1)[None]
    s_cb = lax.dynamic_slice_in_dim(s_cb, chip * cwc, cwc, axis=1)
    s_nw = lax.dynamic_slice_in_dim(s_nw, chip * nwc, nwc, axis=1)
    small_names = ["rel_bias", "conv_w", "conv_b", "dt_bias", "a_log", "d_skip", "ssm_norm_w", "ln_g", "ln_b"]
    small_g = [s_rel, s_cw, s_cb, s_dtb, s_alog, s_dsk, s_nw, s_lng, s_lnb]
    small_w = [rel_bias, conv_w, conv_b, dt_bias, a_log, d_skip, ssm_norm_w, ln_g, ln_b]
    small_m = [m_rel_bias, m_conv_w, m_conv_b, m_dt_bias, m_a_log, m_d_skip, m_ssm_norm_w, m_ln_g, m_ln_b]
    small_v = [v_rel_bias, v_conv_w, v_conv_b, v_dt_bias, v_a_log, v_d_skip, v_ssm_norm_w, v_ln_g, v_ln_b]
    shapes = [t.shape for t in small_w]
    res = _adamw(_pack(small_w), _pack(small_g), _pack(small_m), _pack(small_v), "adamw_small")
    small = {nm: [] for nm in small_names}
    for packed in res:
        for nm, t in zip(small_names, _unpack(packed, shapes)):
            small[nm].append(t)

    order = ["w_in_attn", "w_out_attn", "rel_bias", "w_in_ssm", "conv_w", "conv_b", "dt_bias", "a_log",
             "d_skip", "ssm_norm_w", "w_out_ssm", "ln_g", "ln_b"]
    table = {**big, **small}
    outs = [loss, grad_x]
    for kind in range(4):
        outs += [table[nm][kind] for nm in order]
    return tuple(outs)
```

```python
import functools
import math

import numpy as np
import jax
import jax.numpy as jnp
from jax import lax
from jax.experimental import pallas as pl
from jax.experimental.pallas import tpu as pltpu

F32 = jnp.float32
BF16 = jnp.bfloat16
MESH = pl.DeviceIdType.MESH

ATTN_PATTERNS = ((128, 1), (512, 4), (2048, 16))
N_GROUPS_ATTN = 3
HEAD_DIM = 128
ATTN_BLOCK = 128
NUM_BUCKETS = 32
MAX_DISTANCE = 2048
SSM_HEAD_DIM = 64
HEADS_PER_SSM_GROUP = 16
SSM_GROUP_WIDTH = HEADS_PER_SSM_GROUP * SSM_HEAD_DIM
D_STATE = 128
CONV_WIDTH = 4
CHUNK = 128
DEPTH = 2
DEEPNORM_ALPHA = (2 * DEPTH) ** 0.25
LN_EPS = 1e-5
RMS_EPS = 1e-5
NEG_INF = -1e30
ADAM_LR = 0.001
ADAM_B1 = 0.9
ADAM_B2 = 0.999
ADAM_EPS = 1e-08
ADAM_WD = 0.01
ADAM_STEP = 10

N_CHIPS = 4
N_DEV = 8

VMEM_LIMIT_V7X = 56 * 1024 * 1024
LANES = 128


def _cparams(sem=None):
    return pltpu.CompilerParams(dimension_semantics=sem, vmem_limit_bytes=VMEM_LIMIT_V7X)


def _sigmoid(x):
    return 0.5 * jnp.tanh(0.5 * x) + 0.5


def _dot(a, b):
    return jnp.dot(a, b, preferred_element_type=F32)


def _dot_nt(a, b):
    return lax.dot_general(a, b, (((1,), (1,)), ((), ())), preferred_element_type=F32)


def _dot_tn(a, b):
    return lax.dot_general(a, b, (((0,), (0,)), ((), ())), preferred_element_type=F32)


def _split2(x):
    hi = x.astype(BF16)
    lo = (x - hi.astype(F32)).astype(BF16)
    return hi, lo


def _split3(x):
    hi = x.astype(BF16)
    r = x - hi.astype(F32)
    mid = r.astype(BF16)
    lo = (r - mid.astype(F32)).astype(BF16)
    return hi, mid, lo


def _matmul(a, b, *, mode, grid, a_spec, b_spec, out_shape, out_spec, tile, name,
            add=None, add_spec=None, add_scale=1.0, after=None, into=None):
    nk = grid[2]
    tm, tn = tile
    dot = {"nn": _dot, "nt": _dot_nt, "tn": _dot_tn}[mode]
    has_add = add is not None
    has_after = after is not None
    has_into = into is not None

    def finish(r, add_ref, o_ref):
        if has_add:
            r = r + add_scale * add_ref[...].astype(F32)
        o_ref[...] = r.astype(o_ref.dtype)

    def body_one(*refs):
        a_ref, b_ref = refs[:2]
        finish(dot(a_ref[...].astype(BF16), b_ref[...].astype(BF16)), refs[2] if has_add else None, refs[-1])

    def body_acc(*refs):
        a_ref, b_ref = refs[:2]
        add_ref = refs[2] if has_add else None
        o_ref, acc_ref = refs[-2:]
        k = pl.program_id(2)

        @pl.when(k == 0)
        def _():
            acc_ref[...] = jnp.zeros_like(acc_ref)

        acc_ref[...] += dot(a_ref[...].astype(BF16), b_ref[...].astype(BF16))

        @pl.when(k == nk - 1)
        def _():
            finish(acc_ref[...], add_ref, o_ref)

    in_specs = ([a_spec, b_spec] + ([add_spec] if has_add else []) + ([_ANY] if has_after else [])
                + ([_ANY] if has_into else []))
    args = (a, b) + ((add,) if has_add else ()) + ((after,) if has_after else ()) + ((into,) if has_into else ())
    return pl.pallas_call(
        body_one if nk == 1 else body_acc, name=name, grid=grid, in_specs=in_specs, out_specs=out_spec,
        out_shape=out_shape,
        input_output_aliases={len(args) - 1: 0} if has_into else {},
        scratch_shapes=[] if nk == 1 else [pltpu.VMEM((tm, tn), F32)],
        compiler_params=_cparams(("parallel", "parallel", "arbitrary")),
    )(*args)


def _pick(n, pref):
    for t in pref:
        if n % t == 0:
            return t
    return n


_TILE_PREF = (1024, 512, 256, 128)
_K_TILE_PREF = (2048,) + _TILE_PREF


def _k_tile(k, out_dtype, has_add):
    return _pick(k, _K_TILE_PREF if (has_add or out_dtype != BF16) else (4096,) + _K_TILE_PREF)


def _mm_nn_sharded(a, w4, out_dtype, name, after=None, col_off=0, n=None, classes=1):
    m, k = a.shape
    _, _, nn = w4.shape
    n = N_CHIPS * nn if n is None else n
    tm, tk = _pick(m // classes, _TILE_PREF), _k_tile(k, out_dtype, False)
    tn = _pick(math.gcd(math.gcd(nn, n), col_off) if col_off else math.gcd(nn, n), _TILE_PREF)
    npb = nn // tn
    co = col_off // tn
    bpc, kb = m // classes // tm, k // tk
    av = a.reshape(m // classes, classes * k)
    out_shape = jax.ShapeDtypeStruct((m, n), out_dtype)
    if tm < _TILE_PREF[0]:
        return _matmul(
            av, w4, mode="nn", grid=(n // tn, m // tm, 1), tile=(tm, tn), name=name,
            a_spec=pl.BlockSpec((tm, k), lambda j, i, kk: (i % bpc, i // bpc)),
            b_spec=pl.BlockSpec((None, k, tn), lambda j, i, kk: ((j + co) // npb, 0, (j + co) % npb)),
            out_shape=out_shape, out_spec=pl.BlockSpec((tm, tn), lambda j, i, kk: (i, j)), after=after)
    return _matmul(
        av, w4, mode="nn", grid=(m // tm, n // tn, kb), tile=(tm, tn), name=name,
        a_spec=pl.BlockSpec((tm, tk), lambda i, j, kk: (i % bpc, (i // bpc) * kb + kk)),
        b_spec=pl.BlockSpec((None, tk, tn), lambda i, j, kk: ((j + co) // npb, kk, (j + co) % npb)),
        out_shape=out_shape, out_spec=pl.BlockSpec((tm, tn), lambda i, j, kk: (i, j)), after=after)


def _mm_nn(a, b, out_dtype, name, b_row_off=0, add=None, add_scale=1.0):
    m, k = a.shape
    _, n = b.shape
    tm, tk, tn = _pick(m, _TILE_PREF), _k_tile(k, out_dtype, add is not None), _pick(n, _TILE_PREF)
    assert b_row_off % tk == 0
    ko = b_row_off // tk
    return _matmul(
        a, b, mode="nn", grid=(m // tm, n // tn, k // tk), tile=(tm, tn), name=name,
        a_spec=pl.BlockSpec((tm, tk), lambda i, j, kk: (i, kk)),
        b_spec=pl.BlockSpec((tk, tn), lambda i, j, kk: (kk + ko, j)),
        out_shape=jax.ShapeDtypeStruct((m, n), out_dtype),
        out_spec=pl.BlockSpec((tm, tn), lambda i, j, kk: (i, j)),
        add=add, add_spec=pl.BlockSpec((tm, tn), lambda i, j, kk: (i, j)), add_scale=add_scale)


def _mm_nt(a, b, out_dtype, name, add=None, add_scale=1.0, n=None, b_row_off=0):
    m, k = a.shape
    n = b.shape[0] if n is None else n
    tm, tk, tn = _pick(m, _TILE_PREF), _k_tile(k, out_dtype, add is not None), _pick(n, _TILE_PREF)
    assert b_row_off % tn == 0
    no = b_row_off // tn
    return _matmul(
        a, b, mode="nt", grid=(m // tm, n // tn, k // tk), tile=(tm, tn), name=name,
        a_spec=pl.BlockSpec((tm, tk), lambda i, j, kk: (i, kk)),
        b_spec=pl.BlockSpec((tn, tk), lambda i, j, kk: (j + no, kk)),
        out_shape=jax.ShapeDtypeStruct((m, n), out_dtype),
        out_spec=pl.BlockSpec((tm, tn), lambda i, j, kk: (i, j)),
        add=add, add_spec=pl.BlockSpec((tm, tn), lambda i, j, kk: (i, j)), add_scale=add_scale)


def _mm_nt_sharded_k(a, w4, out_dtype, name, add=None, add_scale=1.0, after=None):
    m, _ = a.shape
    _, n, kn = w4.shape
    tm, tk, tn = _pick(m, _TILE_PREF), _pick(kn, (2560,) + _TILE_PREF), _pick(n, _TILE_PREF)
    kpb = kn // tk
    return _matmul(
        a, w4, mode="nt", grid=(m // tm, n // tn, N_CHIPS * kpb), tile=(tm, tn), name=name,
        a_spec=pl.BlockSpec((tm, tk), lambda i, j, kk: (i, kk)),
        b_spec=pl.BlockSpec((None, tn, tk), lambda i, j, kk: (kk // kpb, j, kk % kpb)),
        out_shape=jax.ShapeDtypeStruct((m, n), out_dtype),
        out_spec=pl.BlockSpec((tm, tn), lambda i, j, kk: (i, j)),
        add=add, add_spec=pl.BlockSpec((tm, tn), lambda i, j, kk: (i, j)), add_scale=add_scale, after=after)


def _mm_tn(a, b, out_dtype, name, shard_cols=None, out_rows=None, out_row_off=0, into=None, after=None):
    k, m = a.shape
    _, n = b.shape
    nn = n if shard_cols is None else shard_cols
    tm, tk, tn = _pick(m, _TILE_PREF), _k_tile(k, out_dtype, False), _pick(nn, _TILE_PREF)
    if shard_cols is None:
        assert out_row_off % tm == 0
        ro = out_row_off // tm
        out_shape = jax.ShapeDtypeStruct((m if out_rows is None else out_rows, n), out_dtype)
        out_spec = pl.BlockSpec((tm, tn), lambda i, j, kk: (i + ro, j))
    else:
        npb = nn // tn
        out_shape = jax.ShapeDtypeStruct((n // nn, m, nn), out_dtype)
        out_spec = pl.BlockSpec((None, tm, tn), lambda i, j, kk: (j // npb, i, j % npb))
    return _matmul(
        a, b, mode="tn", grid=(m // tm, n // tn, k // tk), tile=(tm, tn), name=name,
        a_spec=pl.BlockSpec((tk, tm), lambda i, j, kk: (kk, i)),
        b_spec=pl.BlockSpec((tk, tn), lambda i, j, kk: (kk, j)),
        out_shape=out_shape, out_spec=out_spec, into=into, after=after)


def _cast_bf16(x, name, after=None):
    r, c = x.shape
    tr = _pick(r, (512, 256, 128, 8))
    extra = [] if after is None else [after]

    def body(x_ref, *rest):
        rest[-1][...] = x_ref[...].astype(BF16)

    return pl.pallas_call(
        body, name=name, grid=(r // tr,),
        in_specs=[pl.BlockSpec((tr, c), lambda i: (i, 0))] + [_ANY] * len(extra),
        out_specs=pl.BlockSpec((tr, c), lambda i: (i, 0)),
        out_shape=jax.ShapeDtypeStruct((r, c), BF16),
        compiler_params=_cparams(("parallel",)),
    )(x, *extra)


def _bucket_tiles():
    qi = np.arange(ATTN_BLOCK)[:, None]
    ki = np.arange(2 * ATTN_BLOCK)[None, :]
    delta = np.clip(ATTN_BLOCK + qi - ki, 0, None)
    tiles = []
    max_exact = NUM_BUCKETS // 2
    for _, dil in ATTN_PATTERNS:
        dist = (delta * dil).astype(np.int32)
        d_f = np.maximum(dist, 1).astype(np.float32)
        large = max_exact + (np.log(d_f / np.float32(max_exact)) / np.float32(math.log(MAX_DISTANCE / max_exact))
                             * np.float32(NUM_BUCKETS - max_exact)).astype(np.int32)
        large = np.minimum(large, NUM_BUCKETS - 1)
        tiles.append(np.where(dist < max_exact, dist, large).astype(np.int32))
    return jnp.asarray(np.stack(tiles))


def _bias_expand(rel_bias, buckets, hpg):
    def body(tab_ref, bk_ref, o_ref):
        g, h = pl.program_id(0), pl.program_id(1)
        bk = bk_ref[...]
        acc = jnp.zeros((ATTN_BLOCK, 2 * ATTN_BLOCK), F32)
        for b in range(NUM_BUCKETS):
            acc = jnp.where(bk == b, tab_ref[b, g * hpg + h], acc)
        o_ref[...] = acc

    return pl.pallas_call(
        body, name="bias_expand", grid=(N_GROUPS_ATTN, hpg),
        in_specs=[pl.BlockSpec(memory_space=pltpu.SMEM),
                  pl.BlockSpec((None, ATTN_BLOCK, 2 * ATTN_BLOCK), lambda g, h: (g, 0, 0))],
        out_specs=pl.BlockSpec((None, None, ATTN_BLOCK, 2 * ATTN_BLOCK), lambda g, h: (g, h, 0, 0)),
        out_shape=jax.ShapeDtypeStruct((N_GROUPS_ATTN, hpg, ATTN_BLOCK, 2 * ATTN_BLOCK), F32),
        compiler_params=_cparams(("parallel", "parallel")),
    )(rel_bias, buckets)


def _bias_reduce(dtiles, buckets, hpg):
    def body(t_ref, bk_ref, o_ref):
        bk = bk_ref[...]
        t = t_ref[...]
        rows = lax.broadcasted_iota(jnp.int32, (NUM_BUCKETS, LANES), 0)
        acc = jnp.zeros((NUM_BUCKETS, LANES), F32)
        for b in range(NUM_BUCKETS):
            s = jnp.sum(jnp.sum(jnp.where(bk == b, t, 0.0), axis=1, keepdims=True), axis=0, keepdims=True)
            acc = jnp.where(rows == b, s, acc)
        o_ref[...] = acc

    return pl.pallas_call(
        body, name="bias_reduce", grid=(N_GROUPS_ATTN, hpg),
        in_specs=[pl.BlockSpec((None, None, ATTN_BLOCK, 2 * ATTN_BLOCK), lambda g, h: (g, h, 0, 0)),
                  pl.BlockSpec((None, ATTN_BLOCK, 2 * ATTN_BLOCK), lambda g, h: (g, 0, 0))],
        out_specs=pl.BlockSpec((None, NUM_BUCKETS, LANES), lambda g, h: (g * hpg + h, 0, 0)),
        out_shape=jax.ShapeDtypeStruct((N_GROUPS_ATTN * hpg, NUM_BUCKETS, LANES), F32),
        compiler_params=_cparams(("parallel", "parallel")),
    )(dtiles, buckets)


def _attn_valid(n_is_first):
    qi = lax.broadcasted_iota(jnp.int32, (ATTN_BLOCK, 2 * ATTN_BLOCK), 0)
    ki = lax.broadcasted_iota(jnp.int32, (ATTN_BLOCK, 2 * ATTN_BLOCK), 1)
    delta = ATTN_BLOCK + qi - ki
    band = (delta >= 0) & (delta <= ATTN_BLOCK)
    return band & (jnp.logical_not(n_is_first) | (ki >= ATTN_BLOCK))


def _attn_fwd(pg, bias, g, dil, hpg):
    s = pg.shape[0]
    w = hpg * HEAD_DIM
    rows = s // dil
    nb = rows // ATTN_BLOCK
    scale = HEAD_DIM ** -0.5

    def body(q_ref, kc_ref, kp_ref, vc_ref, vp_ref, bias_ref, o_ref, lse_ref):
        valid = _attn_valid(pl.program_id(1) == 0)
        lane = lax.broadcasted_iota(jnp.int32, (ATTN_BLOCK, LANES), 1)
        lse = jnp.zeros((ATTN_BLOCK, LANES), F32)
        for h in range(hpg):
            sl = slice(h * HEAD_DIM, (h + 1) * HEAD_DIM)
            k2 = jnp.concatenate([kp_ref[:, sl], kc_ref[:, sl]], axis=0)
            v2 = jnp.concatenate([vp_ref[:, sl], vc_ref[:, sl]], axis=0)
            sc = _dot_nt(q_ref[:, sl], k2) * scale + bias_ref[h]
            sc = jnp.where(valid, sc, NEG_INF)
            m = jnp.max(sc, axis=1, keepdims=True)
            p = jnp.exp(sc - m)
            l = jnp.sum(p, axis=1, keepdims=True)
            o_ref[:, sl] = _dot(p.astype(BF16), v2) * (1.0 / l)
            lse = jnp.where(lane == h, m + jnp.log(l), lse)
        lse_ref[...] = lse

    def col(off):
        return lambda r, n: (r * nb + n, off)

    def colp(off):
        return lambda r, n: (r * nb + jnp.maximum(n - 1, 0), off)

    blk = (ATTN_BLOCK, w)
    tok = pl.BlockSpec(blk, lambda r, n: (n, r))
    tok1 = pl.BlockSpec((ATTN_BLOCK, LANES), lambda r, n: (n, r))
    o, lse = pl.pallas_call(
        body, name=f"attn_fwd_g{g}", grid=(dil, nb),
        in_specs=[pl.BlockSpec(blk, col(0)), pl.BlockSpec(blk, col(1)), pl.BlockSpec(blk, colp(1)),
                  pl.BlockSpec(blk, col(2)), pl.BlockSpec(blk, colp(2)),
                  pl.BlockSpec((None, hpg, ATTN_BLOCK, 2 * ATTN_BLOCK), lambda r, n: (g, 0, 0, 0))],
        out_specs=[tok, tok1],
        out_shape=[jax.ShapeDtypeStruct((rows, dil * w), F32), jax.ShapeDtypeStruct((rows, dil * LANES), F32)],
        compiler_params=_cparams(("parallel", "parallel")),
    )(pg, pg, pg, pg, pg, bias)
    return o.reshape(s, w), lse.reshape(s, LANES)


def _attn_combine(os_, lses, gate, hpg):
    s, w = os_[0].shape
    tm = _pick(s, (256, 128))

    def body(o0, o1, o2, l0, l1, l2, gate_ref, o_ref, lse_ref, y_ref):
        a0, a1, a2 = l0[...], l1[...], l2[...]
        m = jnp.maximum(jnp.maximum(a0, a1), a2)
        e0, e1, e2 = jnp.exp(a0 - m), jnp.exp(a1 - m), jnp.exp(a2 - m)
        den = e0 + e1 + e2
        inv = 1.0 / den
        w0, w1, w2 = e0 * inv, e1 * inv, e2 * inv
        lse_ref[...] = m + jnp.log(den)
        for h in range(hpg):
            sl = slice(h * HEAD_DIM, (h + 1) * HEAD_DIM)
            o = w0[:, h:h + 1] * o0[:, sl] + w1[:, h:h + 1] * o1[:, sl] + w2[:, h:h + 1] * o2[:, sl]
            gate = gate_ref[:, sl].astype(F32)
            o_ref[:, sl] = o.astype(BF16)
            y_ref[:, sl] = (o * (gate * _sigmoid(gate))).astype(BF16)

    spec = pl.BlockSpec((tm, w), lambda i: (i, 0))
    spec1 = pl.BlockSpec((tm, LANES), lambda i: (i, 0))
    return pl.pallas_call(
        body, name="attn_combine", grid=(s // tm,),
        in_specs=[spec] * 3 + [spec1] * 3 + [spec],
        out_specs=[spec, spec1, spec],
        out_shape=[jax.ShapeDtypeStruct((s, w), BF16), jax.ShapeDtypeStruct((s, LANES), F32),
                   jax.ShapeDtypeStruct((s, w), BF16)],
        compiler_params=_cparams(("parallel",)),
    )(*os_, *lses, gate)


def _attn_pre_bwd(dy, o, gate, hpg):
    s, w = dy.shape
    tm = _pick(s, (256, 128))

    def body(dy_ref, o_ref, gate_ref, do_ref, dl_ref, dg_ref):
        gate = gate_ref[...].astype(F32)
        sg = _sigmoid(gate)
        dyv = dy_ref[...].astype(F32)
        ov = o_ref[...].astype(F32)
        do = dyv * (gate * sg)
        do_ref[...] = do.astype(BF16)
        dg_ref[...] = (dyv * ov * (sg * (1.0 + gate * (1.0 - sg)))).astype(BF16)
        prod = do * ov
        lane = lax.broadcasted_iota(jnp.int32, (tm, LANES), 1)
        dl = jnp.zeros((tm, LANES), F32)
        for h in range(hpg):
            sl = slice(h * HEAD_DIM, (h + 1) * HEAD_DIM)
            dl = jnp.where(lane == h, jnp.sum(prod[:, sl], axis=1, keepdims=True), dl)
        dl_ref[...] = dl

    spec = pl.BlockSpec((tm, w), lambda i: (i, 0))
    return pl.pallas_call(
        body, name="attn_pre_bwd", grid=(s // tm,),
        in_specs=[spec, spec, spec],
        out_specs=[spec, pl.BlockSpec((tm, LANES), lambda i: (i, 0)), spec],
        out_shape=[jax.ShapeDtypeStruct((s, w), BF16), jax.ShapeDtypeStruct((s, LANES), F32),
                   jax.ShapeDtypeStruct((s, w), BF16)],
        compiler_params=_cparams(("parallel",)),
    )(dy, o, gate)


def _attn_bwd(pg, bias, do, lse, delta, g, dil, hpg):
    s = pg.shape[0]
    w = hpg * HEAD_DIM
    rows = s // dil
    nb = rows // ATTN_BLOCK
    dov = do.reshape(rows, dil * w)
    lsev, dlv = (t.reshape(rows, dil * LANES) for t in (lse, delta))
    scale = HEAD_DIM ** -0.5

    def body(q_ref, kc_ref, kp_ref, vc_ref, vp_ref, bias_ref, do_ref, lse_ref, dl_ref,
             dq_ref, dk_ref, dv_ref, db_ref, dkc_ref, dvc_ref):
        r, i = pl.program_id(0), pl.program_id(1)
        n = nb - 1 - i
        valid = _attn_valid(n == 0)

        @pl.when((r == 0) & (i == 0))
        def _():
            db_ref[...] = jnp.zeros_like(db_ref)

        @pl.when(i == 0)
        def _():
            dkc_ref[...] = jnp.zeros_like(dkc_ref)
            dvc_ref[...] = jnp.zeros_like(dvc_ref)

        for h in range(hpg):
            sl = slice(h * HEAD_DIM, (h + 1) * HEAD_DIM)
            q = q_ref[:, sl]
            dov_ = do_ref[:, sl]
            k2 = jnp.concatenate([kp_ref[:, sl], kc_ref[:, sl]], axis=0)
            v2 = jnp.concatenate([vp_ref[:, sl], vc_ref[:, sl]], axis=0)
            sc = _dot_nt(q, k2) * scale + bias_ref[h]
            p = jnp.exp(jnp.where(valid, sc - lse_ref[:, h:h + 1], NEG_INF))
            dp = _dot_nt(dov_, v2)
            ds = p * (dp - dl_ref[:, h:h + 1])
            db_ref[h] += ds
            dsb = ds.astype(BF16)
            dq_ref[:, sl] = (_dot(dsb, k2) * scale).astype(BF16)
            dk2 = _dot_tn(dsb, q) * scale
            dv2 = _dot_tn(p.astype(BF16), dov_)
            dk_ref[:, sl] = (dk2[ATTN_BLOCK:] + dkc_ref[:, sl]).astype(BF16)
            dv_ref[:, sl] = (dv2[ATTN_BLOCK:] + dvc_ref[:, sl]).astype(BF16)
            dkc_ref[:, sl] = dk2[:ATTN_BLOCK]
            dvc_ref[:, sl] = dv2[:ATTN_BLOCK]

    def col(off):
        return lambda r, i: (r * nb + nb - 1 - i, off)

    def colp(off):
        return lambda r, i: (r * nb + jnp.maximum(nb - 2 - i, 0), off)

    blk = (ATTN_BLOCK, w)
    tok = pl.BlockSpec(blk, lambda r, i: (nb - 1 - i, r))
    tok1 = pl.BlockSpec((ATTN_BLOCK, LANES), lambda r, i: (nb - 1 - i, r))
    dq, dk, dv, db = pl.pallas_call(
        body, name=f"attn_bwd_g{g}", grid=(dil, nb),
        in_specs=[pl.BlockSpec(blk, col(0)), pl.BlockSpec(blk, col(1)), pl.BlockSpec(blk, colp(1)),
                  pl.BlockSpec(blk, col(2)), pl.BlockSpec(blk, colp(2)),
                  pl.BlockSpec((None, hpg, ATTN_BLOCK, 2 * ATTN_BLOCK), lambda r, i: (g, 0, 0, 0)),
                  tok, tok1, tok1],
        out_specs=[tok, tok, tok,
                   pl.BlockSpec((hpg, ATTN_BLOCK, 2 * ATTN_BLOCK), lambda r, i: (0, 0, 0))],
        out_shape=[jax.ShapeDtypeStruct((rows, dil * w), BF16)] * 3
        + [jax.ShapeDtypeStruct((hpg, ATTN_BLOCK, 2 * ATTN_BLOCK), F32)],
        scratch_shapes=[pltpu.VMEM(blk, F32), pltpu.VMEM(blk, F32)],
        compiler_params=_cparams(("arbitrary", "arbitrary")),
    )(pg, pg, pg, pg, pg, bias, dov, lsev, dlv)
    return dq.reshape(s, w), dk.reshape(s, w), dv.reshape(s, w), db


def _ln_fwd(xin, h, gamma, beta, name, affine_in=None, target=None):
    s, d = xin.shape
    tm = _pick(s, (128,))
    has_aff = affine_in is not None
    has_tgt = target is not None

    def body(*refs):
        it = iter(refs)
        x_ref, h_ref, g_ref, b_ref = next(it), next(it), next(it), next(it)
        if has_aff:
            gi_ref, bi_ref = next(it), next(it)
        if has_tgt:
            t_ref = next(it)
        xh_ref, rs_ref = next(it), next(it)
        x = x_ref[...]
        if has_aff:
            x = x * gi_ref[...] + bi_ref[...]
        u = DEEPNORM_ALPHA * x + h_ref[...]
        mu = jnp.mean(u, axis=1, keepdims=True)
        uc = u - mu
        var = jnp.mean(uc * uc, axis=1, keepdims=True)
        rstd = lax.rsqrt(var + LN_EPS)
        xhat = uc * rstd
        xh_ref[...] = xhat
        rs_ref[...] = rstd
        y = xhat * g_ref[...] + b_ref[...]
        if has_tgt:
            dy_ref, l_ref = next(it), next(it)
            e = y - t_ref[...]
            dy_ref[...] = e * (1.0 / d)
            l_ref[...] = jnp.sum(e * e, axis=1, keepdims=True)
        else:
            y_ref = next(it)
            y_ref[...] = y.astype(BF16)

    row = pl.BlockSpec((tm, d), lambda i: (i, 0))
    vec = pl.BlockSpec((1, d), lambda i: (0, 0))
    one = pl.BlockSpec((tm, 1), lambda i: (i, 0))
    in_specs = [row, row, vec, vec] + ([vec, vec] if has_aff else []) + ([row] if has_tgt else [])
    args = [xin, h, gamma, beta] + (list(affine_in) if has_aff else []) + ([target] if has_tgt else [])
    out_specs = [row, one] + ([row, one] if has_tgt else [row])
    out_shape = [jax.ShapeDtypeStruct((s, d), F32), jax.ShapeDtypeStruct((s, 1), F32)]
    out_shape += ([jax.ShapeDtypeStruct((s, d), F32), jax.ShapeDtypeStruct((s, 1), F32)] if has_tgt
                  else [jax.ShapeDtypeStruct((s, d), BF16)])
    return pl.pallas_call(
        body, name=name, grid=(s // tm,), in_specs=in_specs, out_specs=out_specs, out_shape=out_shape,
        compiler_params=_cparams(("parallel",)),
    )(*args)


def _ln_bwd(dy, xhat, rstd, gamma, name):
    s, d = dy.shape
    tm = _pick(s, (128,))

    def body(dy_ref, xh_ref, rs_ref, g_ref, du_ref, dub_ref, dg_ref, db_ref):
        @pl.when(pl.program_id(0) == 0)
        def _():
            dg_ref[...] = jnp.zeros_like(dg_ref)
            db_ref[...] = jnp.zeros_like(db_ref)

        dyv = dy_ref[...]
        xh = xh_ref[...]
        dg_ref[...] += jnp.sum(dyv * xh, axis=0, keepdims=True)
        db_ref[...] += jnp.sum(dyv, axis=0, keepdims=True)
        dxh = dyv * g_ref[...]
        m1 = jnp.mean(dxh, axis=1, keepdims=True)
        m2 = jnp.mean(dxh * xh, axis=1, keepdims=True)
        du = rs_ref[...] * (dxh - m1 - xh * m2)
        du_ref[...] = du
        dub_ref[...] = du.astype(BF16)

    row = pl.BlockSpec((tm, d), lambda i: (i, 0))
    vec = pl.BlockSpec((1, d), lambda i: (0, 0))
    one = pl.BlockSpec((tm, 1), lambda i: (i, 0))
    return pl.pallas_call(
        body, name=name, grid=(s // tm,), in_specs=[row, row, one, vec],
        out_specs=[row, row, vec, vec],
        out_shape=[jax.ShapeDtypeStruct((s, d), F32), jax.ShapeDtypeStruct((s, d), BF16),
                   jax.ShapeDtypeStruct((1, d), F32), jax.ShapeDtypeStruct((1, d), F32)],
        compiler_params=_cparams(("arbitrary",)),
    )(dy, xhat, rstd, gamma)


_HALO = 16
_STRIP = 16


def _strips(tm, fn, init, reverse=False):
    n = tm // _STRIP

    def step(i, carry):
        s_ = n - 1 - i if reverse else i
        return fn(pl.ds(pl.multiple_of(s_ * _STRIP, _STRIP), _STRIP), carry)

    return lax.fori_loop(0, n, step, init)


def _fold8(t):
    return t[0:8] + t[8:16]


def _conv_taps(ext, tm, w_ref):
    acc = None
    for k in range(CONV_WIDTH):
        lo = _HALO - (CONV_WIDTH - 1) + k
        term = w_ref[k:k + 1, :] * ext[lo:lo + tm, :]
        acc = term if acc is None else acc + term
    return acc


def _conv_strip(prev, cur, w_ref):
    ext = jnp.concatenate([prev, cur], axis=0)
    acc, taps = None, []
    for k in range(CONV_WIDTH):
        lo = _STRIP - (CONV_WIDTH - 1) + k
        taps.append(ext[lo:lo + _STRIP, :])
        term = w_ref[k:k + 1, :] * taps[k]
        acc = term if acc is None else acc + term
    return acc, taps


def _conv_fwd(pzx, conv_w, conv_b, d_inner):
    s, _ = pzx.shape
    cd = conv_w.shape[1]
    tm = _pick(s, (512, 256, 128))
    tc = _pick(cd, (1024, 512, 256, 128))
    off = d_inner // tc
    hb = tm // _HALO

    def body(x_ref, p_ref, w_ref, b_ref, o_ref):
        prev = jnp.where(pl.program_id(0) > 0, p_ref[...].astype(F32), 0.0)
        ext = jnp.concatenate([prev, x_ref[...].astype(F32)], axis=0)
        pre = _conv_taps(ext, tm, w_ref) + b_ref[...]
        o_ref[...] = (pre * _sigmoid(pre)).astype(BF16)

    return pl.pallas_call(
        body, name="conv_fwd", grid=(s // tm, cd // tc),
        in_specs=[pl.BlockSpec((tm, tc), lambda i, j: (i, off + j)),
                  pl.BlockSpec((_HALO, tc), lambda i, j: (jnp.maximum(i * hb - 1, 0), off + j)),
                  pl.BlockSpec((CONV_WIDTH, tc), lambda i, j: (0, j)),
                  pl.BlockSpec((1, tc), lambda i, j: (0, j))],
        out_specs=pl.BlockSpec((tm, tc), lambda i, j: (i, j)),
        out_shape=jax.ShapeDtypeStruct((s, cd), BF16),
        compiler_params=_cparams(("parallel", "parallel")),
    )(pzx, pzx, conv_w, conv_b)


def _conv_bwd_a(pzx, dxbc, conv_w, conv_b, d_inner):
    s, _ = pzx.shape
    cd = conv_w.shape[1]
    tm = _pick(s, (512, 256, 128))
    tc = _pick(cd, (1024, 512, 256, 128))
    off = d_inner // tc
    hb = tm // _HALO

    def body(x_ref, p_ref, d_ref, w_ref, b_ref, o_ref, dw_ref, db_ref, acc_ref):
        @pl.when(pl.program_id(1) == 0)
        def _():
            dw_ref[...] = jnp.zeros_like(dw_ref)
            db_ref[...] = jnp.zeros_like(db_ref)

        acc_ref[...] = jnp.zeros_like(acc_ref)

        def strip(rows, prev):
            cur = x_ref[rows, :].astype(F32)
            pre, taps = _conv_strip(prev, cur, w_ref)
            pre = pre + b_ref[...]
            sg = _sigmoid(pre)
            dpre = d_ref[rows, :].astype(F32) * (sg * (1.0 + pre * (1.0 - sg)))
            o_ref[rows, :] = dpre
            for k in range(CONV_WIDTH):
                acc_ref[k] += _fold8(dpre * taps[k])
            acc_ref[CONV_WIDTH] += _fold8(dpre)
            return cur

        _strips(tm, strip, jnp.where(pl.program_id(1) > 0, p_ref[...].astype(F32), 0.0))
        for k in range(CONV_WIDTH):
            dw_ref[k:k + 1, :] += jnp.sum(acc_ref[k], axis=0, keepdims=True)
        db_ref[...] += jnp.sum(acc_ref[CONV_WIDTH], axis=0, keepdims=True)

    return pl.pallas_call(
        body, name="conv_bwd_a", grid=(cd // tc, s // tm),
        in_specs=[pl.BlockSpec((tm, tc), lambda j, i: (i, off + j)),
                  pl.BlockSpec((_HALO, tc), lambda j, i: (jnp.maximum(i * hb - 1, 0), off + j)),
                  pl.BlockSpec((tm, tc), lambda j, i: (i, j)),
                  pl.BlockSpec((CONV_WIDTH, tc), lambda j, i: (0, j)),
                  pl.BlockSpec((1, tc), lambda j, i: (0, j))],
        out_specs=[pl.BlockSpec((tm, tc), lambda j, i: (i, j)),
                   pl.BlockSpec((CONV_WIDTH, tc), lambda j, i: (0, j)),
                   pl.BlockSpec((1, tc), lambda j, i: (0, j))],
        out_shape=[jax.ShapeDtypeStruct((s, cd), F32), jax.ShapeDtypeStruct((CONV_WIDTH, cd), F32),
                   jax.ShapeDtypeStruct((1, cd), F32)],
        scratch_shapes=[pltpu.VMEM((CONV_WIDTH + 1, 8, tc), F32)],
        compiler_params=_cparams(("parallel", "arbitrary")),
    )(pzx, pzx, dxbc, conv_w, conv_b)


def _conv_bwd_b(dpre, conv_w, into, col_off):
    s, cd = dpre.shape
    tm = _pick(s, (512, 256, 128))
    tc = _pick(cd, (1024, 512, 256, 128))
    hb = tm // 8
    nrb = s // tm
    assert col_off % tc == 0
    co = col_off // tc

    def body(x_ref, nx_ref, w_ref, into_ref, o_ref):
        nxt = jnp.where(pl.program_id(0) < nrb - 1, nx_ref[...], 0.0)
        ext = jnp.concatenate([x_ref[...], nxt], axis=0)
        acc = None
        for k in range(CONV_WIDTH):
            lo = CONV_WIDTH - 1 - k
            term = w_ref[k:k + 1, :] * ext[lo:lo + tm, :]
            acc = term if acc is None else acc + term
        o_ref[...] = acc.astype(BF16)

    return pl.pallas_call(
        body, name="conv_bwd_b", grid=(nrb, cd // tc),
        in_specs=[pl.BlockSpec((tm, tc), lambda i, j: (i, j)),
                  pl.BlockSpec((8, tc), lambda i, j: (jnp.minimum((i + 1) * hb, s // 8 - 1), j)),
                  pl.BlockSpec((CONV_WIDTH, tc), lambda i, j: (0, j)), _ANY],
        out_specs=pl.BlockSpec((tm, tc), lambda i, j: (i, j + co)),
        out_shape=jax.ShapeDtypeStruct(into.shape, BF16),
        input_output_aliases={3: 0},
        compiler_params=_cparams(("parallel", "parallel")),
    )(dpre, dpre, conv_w, into)


def _expand_matrix():
    e = np.zeros((LANES, SSM_GROUP_WIDTH), np.float32)
    for h in range(HEADS_PER_SSM_GROUP):
        e[h, h * SSM_HEAD_DIM:(h + 1) * SSM_HEAD_DIM] = 1.0
    return jnp.asarray(e, BF16)


def _expand(t, e):
    hi, lo = _split2(t)
    return _dot(hi, e) + _dot(lo, e)


def _segsum(v, e):
    hi, lo = _split2(v)
    return _dot_nt(hi, e) + _dot_nt(lo, e)


def _tri_dot(tri, x):
    hi, mid, lo = _split3(x)
    return _dot(tri, hi) + _dot(tri, mid) + _dot(tri, lo)


def _ssd_common(dtp_ref, a_ref, dtb_ref, x_ref, e):
    li = lax.broadcasted_iota(jnp.int32, (CHUNK, CHUNK), 0)
    si = lax.broadcasted_iota(jnp.int32, (CHUNK, CHUNK), 1)
    causal = li >= si
    tril = causal.astype(BF16)
    raw = dtp_ref[...] + dtb_ref[...]
    dt = jnp.maximum(raw, 0.0) + jnp.log(1.0 + jnp.exp(-jnp.abs(raw)))
    head_lane = lax.broadcasted_iota(jnp.int32, (1, LANES), 1) < HEADS_PER_SSM_GROUP
    a = jnp.where(head_lane, -jnp.exp(a_ref[...]), 0.0)
    a_cum = _tri_dot(tril, dt * a)
    a_cum_t = a_cum.T
    e_a = jnp.exp(a_cum)
    to_end = jnp.exp(a_cum[CHUNK - 1:CHUNK, :] - a_cum)
    x = x_ref[...].astype(F32)
    dt_e = _expand(dt, e)
    return dict(causal=causal, raw=raw, dt=dt, a=a, a_cum=a_cum, a_cum_t=a_cum_t, e_a=e_a,
                to_end=to_end, x=x, dt_e=dt_e, xdt=x * dt_e, e_a_e=_expand(e_a, e),
                to_end_e=_expand(to_end, e))


def _decay(q, h):
    seg = q["a_cum"][:, h:h + 1] - q["a_cum_t"][h:h + 1, :]
    return jnp.exp(jnp.where(q["causal"], seg, -jnp.inf))


def _ssd_specs(ng, d_inner, rev, nc):
    cidx = (lambda i: nc - 1 - i) if rev else (lambda i: i)
    boff = d_inner // D_STATE
    return dict(
        xs=pl.BlockSpec((CHUNK, SSM_GROUP_WIDTH), lambda g, i: (cidx(i), g)),
        b=pl.BlockSpec((CHUNK, D_STATE), lambda g, i: (cidx(i), boff + g)),
        c=pl.BlockSpec((CHUNK, D_STATE), lambda g, i: (cidx(i), boff + ng + g)),
        dtp=pl.BlockSpec((None, CHUNK, LANES), lambda g, i: (g, cidx(i), 0)),
        vec=pl.BlockSpec((None, 1, LANES), lambda g, i: (g, 0, 0)),
        wide=pl.BlockSpec((None, 1, SSM_GROUP_WIDTH), lambda g, i: (g, 0, 0)),
        e=pl.BlockSpec((LANES, SSM_GROUP_WIDTH), lambda g, i: (0, 0)),
        st=pl.BlockSpec((None, None, D_STATE, SSM_GROUP_WIDTH), lambda g, i: (g, cidx(i), 0, 0)),
        tok=pl.BlockSpec((CHUNK, SSM_GROUP_WIDTH), lambda g, i: (cidx(i), g)),
        bc_out=pl.BlockSpec((CHUNK, D_STATE), lambda g, i: (cidx(i), g)),
    )


def _ssd_fwd(xbc, dtp, a_pad, dtb_pad, dsk_e, e, d_inner):
    s = xbc.shape[0]
    ng = d_inner // SSM_GROUP_WIDTH
    nc = s // CHUNK

    def body(x_ref, b_ref, c_ref, dtp_ref, a_ref, dtb_ref, dsk_ref, e_ref, y_ref, st_ref, state):
        lane = lax.broadcasted_iota(jnp.int32, (CHUNK, LANES), 1)
        @pl.when(pl.program_id(1) == 0)
        def _():
            state[...] = jnp.zeros_like(state)

        ev = e_ref[...]
        q = _ssd_common(dtp_ref, a_ref, dtb_ref, x_ref, ev)
        bm, cm = b_ref[...], c_ref[...]
        cb = _dot_nt(cm, bm)
        s0 = state[...]
        st_ref[...] = s0
        y = _dot(cm, s0.astype(BF16)) * q["e_a_e"] + dsk_ref[...] * q["x"]
        xdt = q["xdt"]
        left = lane[:, :] < SSM_HEAD_DIM
        for j in range(HEADS_PER_SSM_GROUP // 2):
            sl = slice(j * LANES, (j + 1) * LANES)
            x2 = xdt[:, sl]
            m0 = (cb * _decay(q, 2 * j)).astype(BF16)
            m1 = (cb * _decay(q, 2 * j + 1)).astype(BF16)
            mcat = jnp.concatenate([m0, m1], axis=1)
            xbd = jnp.concatenate([jnp.where(left, x2, 0.0), jnp.where(left, 0.0, x2)], axis=0).astype(BF16)
            y_ref[:, sl] = (y[:, sl] + _dot(mcat, xbd)).astype(BF16)
        state[...] = s0 * q["e_a_e"][CHUNK - 1:CHUNK, :] + _dot_tn(bm, (q["to_end_e"] * xdt).astype(BF16))

    sp = _ssd_specs(ng, d_inner, False, nc)
    return pl.pallas_call(
        body, name="ssd_fwd", grid=(ng, nc),
        in_specs=[sp["xs"], sp["b"], sp["c"], sp["dtp"], sp["vec"], sp["vec"], sp["wide"], sp["e"]],
        out_specs=[sp["tok"], sp["st"]],
        out_shape=[jax.ShapeDtypeStruct((s, d_inner), BF16),
                   jax.ShapeDtypeStruct((ng, nc, D_STATE, SSM_GROUP_WIDTH), F32)],
        scratch_shapes=[pltpu.VMEM((D_STATE, SSM_GROUP_WIDTH), F32)],
        compiler_params=_cparams(("parallel", "arbitrary")),
    )(xbc, xbc, xbc, dtp, a_pad, dtb_pad, dsk_e, e)


def _ssd_bwd(xbc, dtp, a_pad, dtb_pad, dsk_e, e, states, dy, d_inner):
    s = xbc.shape[0]
    ng = d_inner // SSM_GROUP_WIDTH
    nc = s // CHUNK

    def body(x_ref, b_ref, c_ref, dtp_ref, a_ref, dtb_ref, dsk_ref, e_ref, st_ref, dy_ref,
             dx_ref, db_ref, dc_ref, ddt_ref, da_ref, ddtb_ref, dd_ref, dstate):
        lane = lax.broadcasted_iota(jnp.int32, (CHUNK, LANES), 1)
        sub = lax.broadcasted_iota(jnp.int32, (CHUNK, LANES), 0)
        @pl.when(pl.program_id(1) == 0)
        def _():
            dstate[...] = jnp.zeros_like(dstate)
            da_ref[...] = jnp.zeros_like(da_ref)
            ddtb_ref[...] = jnp.zeros_like(ddtb_ref)
            dd_ref[...] = jnp.zeros_like(dd_ref)

        ev = e_ref[...]
        q = _ssd_common(dtp_ref, a_ref, dtb_ref, x_ref, ev)
        bm, cm = b_ref[...], c_ref[...]
        cb = _dot_nt(cm, bm)
        x, xdt, e_a_e, to_end_e = q["x"], q["xdt"], q["e_a_e"], q["to_end_e"]
        s0 = st_ref[...]
        s0b = s0.astype(BF16)
        ds1 = dstate[...]
        ds1b = ds1.astype(BF16)
        dy = dy_ref[...].astype(F32)
        e_last_e = e_a_e[CHUNK - 1:CHUNK, :]

        dye = dy * e_a_e
        dyeb = dye.astype(BF16)
        cs0 = _dot(cm, s0b)
        dc = _dot_nt(dyeb, s0b)
        dstate[...] = e_last_e * ds1 + _dot_tn(cm, dyeb)
        da_col = _segsum(dye * cs0, ev)

        gmat = _dot(bm, ds1b)
        dxdt = to_end_e * gmat
        dte = _segsum(xdt * gmat, ev) * q["to_end"]
        db = _dot_nt((to_end_e * xdt).astype(BF16), ds1b)
        da_col = da_col - dte
        last_row = (jnp.sum(dte, axis=0, keepdims=True)
                    + q["e_a"][CHUNK - 1:CHUNK, :] * jnp.sum(_segsum(s0 * ds1, ev), axis=0, keepdims=True))

        left = lane < SSM_HEAD_DIM
        dcb = jnp.zeros((CHUNK, CHUNK), F32)
        row_acc = jnp.zeros((CHUNK, LANES), F32)
        for j in range(HEADS_PER_SSM_GROUP // 2):
            sl = slice(j * LANES, (j + 1) * LANES)
            x2 = xdt[:, sl].astype(BF16)
            dy2 = dy[:, sl]
            dyl = jnp.where(left, dy2, 0.0).astype(BF16)
            dyr = jnp.where(left, 0.0, dy2).astype(BF16)
            ms = []
            for hh, dyh in ((0, dyl), (1, dyr)):
                h = 2 * j + hh
                dec = _decay(q, h)
                m = cb * dec
                dm = _dot_nt(dyh, x2)
                dcb = dcb + dm * dec
                dseg = dm * m
                da_col = da_col + jnp.where(lane == h, jnp.sum(dseg, axis=1, keepdims=True), 0.0)
                row_acc = row_acc + jnp.where(sub == h, jnp.sum(dseg, axis=0, keepdims=True), 0.0)
                ms.append(m.astype(BF16))
            mst = jnp.concatenate(ms, axis=0)
            dyst = jnp.concatenate([dyl, dyr], axis=0)
            d2 = dxdt[:, sl] + _dot_tn(mst, dyst)
            dx_ref[:, sl] = (d2 * q["dt_e"][:, sl] + dsk_ref[:, sl] * dy2).astype(BF16)
            dxdt_x = d2 * x[:, sl]
            if j == 0:
                parts = [dxdt_x]
            else:
                parts.append(dxdt_x)
        dcbb = dcb.astype(BF16)
        dc_ref[...] = (dc + _dot(dcbb, bm)).astype(BF16)
        db_ref[...] = (db + _dot_tn(dcbb, cm)).astype(BF16)

        d_a = da_col - row_acc.T + jnp.where(sub == CHUNK - 1, last_row, 0.0)
        triu = (lax.broadcasted_iota(jnp.int32, (CHUNK, CHUNK), 1)
                >= lax.broadcasted_iota(jnp.int32, (CHUNK, CHUNK), 0)).astype(BF16)
        d_dta = _tri_dot(triu, d_a)
        ddt = d_dta * q["a"] + _segsum(jnp.concatenate(parts, axis=1), ev)
        ddt_raw = ddt * _sigmoid(q["raw"])
        ddt_ref[...] = ddt_raw
        da_ref[...] += jnp.sum(d_dta * q["dt"], axis=0, keepdims=True) * q["a"]
        ddtb_ref[...] += jnp.sum(ddt_raw, axis=0, keepdims=True)
        dd_ref[...] += jnp.sum(dy * x, axis=0, keepdims=True)

    sp = _ssd_specs(ng, d_inner, True, nc)
    return pl.pallas_call(
        body, name="ssd_bwd", grid=(ng, nc),
        in_specs=[sp["xs"], sp["b"], sp["c"], sp["dtp"], sp["vec"], sp["vec"], sp["wide"], sp["e"],
                  sp["st"], sp["tok"]],
        out_specs=[sp["tok"], sp["bc_out"], sp["bc_out"], sp["dtp"], sp["vec"], sp["vec"], sp["wide"]],
        out_shape=[jax.ShapeDtypeStruct((s, d_inner), BF16),
                   jax.ShapeDtypeStruct((s, ng * D_STATE), BF16),
                   jax.ShapeDtypeStruct((s, ng * D_STATE), BF16),
                   jax.ShapeDtypeStruct((ng, s, LANES), F32),
                   jax.ShapeDtypeStruct((ng, 1, LANES), F32),
                   jax.ShapeDtypeStruct((ng, 1, LANES), F32),
                   jax.ShapeDtypeStruct((ng, 1, SSM_GROUP_WIDTH), F32)],
        scratch_shapes=[pltpu.VMEM((D_STATE, SSM_GROUP_WIDTH), F32)],
        compiler_params=_cparams(("parallel", "arbitrary")),
    )(xbc, xbc, xbc, dtp, a_pad, dtb_pad, dsk_e, e, states, dy)


def _gate_norm_fwd(y, pzx, norm_w):
    s, di = y.shape
    ng = di // SSM_GROUP_WIDTH
    tm = _pick(s, (512, 256, 128))

    def body(y_ref, z_ref, w_ref, o_ref):
        z = z_ref[...].astype(F32)
        y2 = y_ref[...].astype(F32) * (z * _sigmoid(z))
        r = lax.rsqrt(jnp.mean(y2 * y2, axis=1, keepdims=True) + RMS_EPS)
        o_ref[...] = (y2 * r * w_ref[...]).astype(BF16)

    blk = pl.BlockSpec((tm, SSM_GROUP_WIDTH), lambda i, g: (i, g))
    return pl.pallas_call(
        body, name="gate_norm_fwd", grid=(s // tm, ng),
        in_specs=[blk, blk, pl.BlockSpec((1, SSM_GROUP_WIDTH), lambda i, g: (0, g))],
        out_specs=blk, out_shape=jax.ShapeDtypeStruct((s, di), BF16),
        compiler_params=_cparams(("parallel", "parallel")),
    )(y, pzx, norm_w)


def _gate_norm_bwd(dy3, y, pzx, norm_w):
    s, di = y.shape
    ng = di // SSM_GROUP_WIDTH
    tm = _pick(s, (512, 256, 128))

    def body(d_ref, y_ref, z_ref, w_ref, dy_ref, dz_ref, dw_ref):
        @pl.when(pl.program_id(1) == 0)
        def _():
            dw_ref[...] = jnp.zeros_like(dw_ref)

        z = z_ref[...].astype(F32)
        yv = y_ref[...].astype(F32)
        sg = _sigmoid(z)
        sz = z * sg
        y2 = yv * sz
        r = lax.rsqrt(jnp.mean(y2 * y2, axis=1, keepdims=True) + RMS_EPS)
        nrm = y2 * r
        d3 = d_ref[...].astype(F32)
        dw_ref[...] += jnp.sum(d3 * nrm, axis=0, keepdims=True)
        dn = d3 * w_ref[...]
        dy2 = r * (dn - nrm * jnp.mean(dn * nrm, axis=1, keepdims=True))
        dy_ref[...] = (dy2 * sz).astype(BF16)
        dz_ref[...] = (dy2 * yv * (sg * (1.0 + z * (1.0 - sg)))).astype(BF16)

    blk = pl.BlockSpec((tm, SSM_GROUP_WIDTH), lambda g, i: (i, g))
    vec = pl.BlockSpec((1, SSM_GROUP_WIDTH), lambda g, i: (0, g))
    return pl.pallas_call(
        body, name="gate_norm_bwd", grid=(ng, s // tm),
        in_specs=[blk, blk, blk, vec], out_specs=[blk, blk, vec],
        out_shape=[jax.ShapeDtypeStruct((s, di), BF16), jax.ShapeDtypeStruct(pzx.shape, BF16),
                   jax.ShapeDtypeStruct((1, di), F32)],
        compiler_params=_cparams(("parallel", "arbitrary")),
    )(dy3, y, pzx, norm_w)


_ANY = pl.BlockSpec(memory_space=pl.ANY)


def _place():
    x, y, c = lax.axis_index("x"), lax.axis_index("y"), lax.axis_index("c")
    chips = [(1 - x, y), (x, 1 - y), (1 - x, 1 - y)]
    return x, y, c, chips


def _cast_to_slot(x, kvec, name, after=None):
    r, cn = x.shape
    tr = _rows_per_block(r, cn)
    extra = [] if after is None else [after]

    def body(k_ref, x_ref, *rest):
        rest[-1][...] = x_ref[...].astype(BF16)

    grid_spec = pltpu.PrefetchScalarGridSpec(
        num_scalar_prefetch=1, grid=(r // tr,),
        in_specs=[pl.BlockSpec((tr, cn), lambda i, k: (i, 0))] + [_ANY] * len(extra),
        out_specs=pl.BlockSpec((None, tr, cn), lambda i, k: (k[0], i, 0)))
    return pl.pallas_call(
        body, name=name, grid_spec=grid_spec, out_shape=jax.ShapeDtypeStruct((N_CHIPS, r, cn), BF16),
        compiler_params=_cparams(("parallel",)),
    )(kvec, x, *extra)


def _swap_halves(gs, name, after=None):
    n = len(gs)
    extra = [] if after is None else [after]

    def body(*refs):
        ins, outs = refs[:n], refs[n + len(extra):2 * n + len(extra)]
        send_sems, recv_sems = refs[2 * n + len(extra):]
        x, y, c, _ = _place()
        cps = []
        for w in range(n):
            hr = gs[w].shape[1] // 2
            cp = pltpu.make_async_remote_copy(
                src_ref=ins[w].at[:, pl.ds((1 - c) * hr, hr)], dst_ref=outs[w],
                send_sem=send_sems.at[w], recv_sem=recv_sems.at[w],
                device_id=(x, y, 1 - c), device_id_type=MESH)
            cp.start()
            cps.append(cp)
        for cp in cps:
            cp.wait()

    return pl.pallas_call(
        body, name=name,
        in_specs=[_ANY] * (n + len(extra)), out_specs=[_ANY] * n,
        out_shape=[jax.ShapeDtypeStruct((g.shape[0], g.shape[1] // 2, g.shape[2]), g.dtype) for g in gs],
        scratch_shapes=[pltpu.SemaphoreType.DMA((n,)), pltpu.SemaphoreType.DMA((n,))],
    )(*gs, *extra)


def _join_halves(fs, name, after=None):
    n = len(fs)
    extra = [] if after is None else [after]

    def body(*refs):
        outs = refs[n + len(extra):2 * n + len(extra)]
        send_sems, recv_sems = refs[2 * n + len(extra):]
        x, y, c, _ = _place()

        def copy(w, hc):
            hr = fs[w].shape[0] // 2
            rows = outs[w].at[pl.ds(hc * hr, hr)]
            return pltpu.make_async_remote_copy(
                src_ref=rows, dst_ref=rows, send_sem=send_sems.at[w], recv_sem=recv_sems.at[w],
                device_id=(x, y, 1 - c), device_id_type=MESH)

        cps = [copy(w, c) for w in range(n)]
        for cp in cps:
            cp.start()
        for w in range(n):
            copy(w, 1 - c).wait_recv()
        for cp in cps:
            cp.wait_send()

    return pl.pallas_call(
        body, name=name,
        in_specs=[_ANY] * (n + len(extra)), out_specs=[_ANY] * n,
        out_shape=[jax.ShapeDtypeStruct(f.shape, f.dtype) for f in fs],
        input_output_aliases={w: w for w in range(n)},
        scratch_shapes=[pltpu.SemaphoreType.DMA((n,)), pltpu.SemaphoreType.DMA((n,))],
    )(*fs, *extra)


_HBM_SPEC = pl.BlockSpec(memory_space=pltpu.HBM)
_SEM_SPEC = pl.BlockSpec(memory_space=pltpu.SEMAPHORE)
_VMEM_SPEC = pl.BlockSpec(memory_space=pltpu.VMEM)
_EFFECT = pltpu.SideEffectType.DATAFLOW_SIDE_EFFECTING
_TOKEN = jax.ShapeDtypeStruct((8, LANES), F32)


def _hbm(a):
    return pltpu.with_memory_space_constraint(a, pltpu.HBM)


def _gather_copies(bufs, refs, send_sems, recv_sems, forward, arrivals=True):
    x, y, c, chips = _place()
    k = 2 * x + y
    out, arrive = [], []
    for w, ref in enumerate(refs):
        hr = bufs[w].shape[1] // 2
        for j, (cx, cy) in enumerate(chips):
            kj = 2 * cx + cy
            slot_out, slot_in, half_in = (kj, kj, 1 - c) if forward else (k, kj, c)
            to = (x, y, 1 - c) if forward else (cx, cy, c)
            src = ref.at[slot_out, pl.ds(c * hr, hr)]
            land = ref.at[slot_in, pl.ds(half_in * hr, hr)]
            out.append(pltpu.make_async_remote_copy(
                src_ref=src, dst_ref=src, send_sem=send_sems.at[3 * w + j], recv_sem=recv_sems.at[3 * w + j],
                device_id=to, device_id_type=MESH))
            if arrivals:
                arrive.append(pltpu.make_async_remote_copy(
                    src_ref=land, dst_ref=land, send_sem=send_sems.at[3 * w + j], recv_sem=recv_sems.at[3 * w + j],
                    device_id=to, device_id_type=MESH))
    return out, arrive


def _gather_start(bufs, forward, name, after=None):
    n = len(bufs)
    extra = [] if after is None else [after]

    def body(*refs):
        ins = refs[:n]
        send_sems, recv_sems = refs[n + len(extra)], refs[n + len(extra) + 1]
        token = refs[-1]
        out, _ = _gather_copies(bufs, ins, send_sems, recv_sems, forward, arrivals=False)
        for cp in out:
            cp.start()
        token[...] = jnp.zeros_like(token)

    res = pl.pallas_call(
        body, name=name,
        out_shape=(pltpu.SemaphoreType.DMA((3 * n,)), pltpu.SemaphoreType.DMA((3 * n,)))
        + tuple(pltpu.HBM(b.shape, b.dtype) for b in bufs) + (_TOKEN,),
        in_specs=(_HBM_SPEC,) * n + (_ANY,) * len(extra),
        out_specs=(_SEM_SPEC, _SEM_SPEC) + (_HBM_SPEC,) * n + (_VMEM_SPEC,),
        input_output_aliases={w: 2 + w for w in range(n)},
        compiler_params=pltpu.CompilerParams(has_side_effects=_EFFECT),
    )(*[_hbm(b) for b in bufs], *extra)
    return res[0], res[1], list(res[2:2 + n]), res[-1]


def _gather_wait(bufs, send_sems, recv_sems, after, forward, name):
    n = len(bufs)

    def body(*refs):
        ins = refs[:n]
        send_sems, recv_sems = refs[n], refs[n + 1]
        out, arrive = _gather_copies(bufs, ins, send_sems, recv_sems, forward)
        for cp in out:
            cp.wait_send()
        for cp in arrive:
            cp.wait_recv()

    res = pl.pallas_call(
        body, name=name,
        out_shape=tuple(pltpu.HBM(b.shape, b.dtype) for b in bufs),
        in_specs=(_HBM_SPEC,) * n + (_SEM_SPEC, _SEM_SPEC, _ANY), out_specs=(_HBM_SPEC,) * n,
        input_output_aliases={w: w for w in range(n)},
        compiler_params=pltpu.CompilerParams(has_side_effects=_EFFECT),
    )(*bufs, send_sems, recv_sems, after)
    return list(res)


def _scatter_copies(t_ref, land_ref, send_sems, recv_sems, arrivals=True):
    x, y, c, chips = _place()
    k = 2 * x + y
    out, arrive = [], []
    for j, (cx, cy) in enumerate(chips):
        kj = 2 * cx + cy
        out.append(pltpu.make_async_remote_copy(
            src_ref=t_ref.at[kj], dst_ref=land_ref.at[k], send_sem=send_sems.at[j], recv_sem=recv_sems.at[j],
            device_id=(cx, cy, c), device_id_type=MESH))
        if arrivals:
            arrive.append(pltpu.make_async_remote_copy(
                src_ref=t_ref.at[kj], dst_ref=land_ref.at[kj], send_sem=send_sems.at[j], recv_sem=recv_sems.at[j],
                device_id=(cx, cy, c), device_id_type=MESH))
    return out, arrive


def _scatter_start(t, name):
    def body(t_ref, land_ref, send_sems, recv_sems, t_thru, land_thru, token):
        out, _ = _scatter_copies(t_ref, land_ref, send_sems, recv_sems, arrivals=False)
        for cp in out:
            cp.start()
        token[...] = jnp.zeros_like(token)

    return pl.pallas_call(
        body, name=name,
        out_shape=(pltpu.SemaphoreType.DMA((3,)), pltpu.SemaphoreType.DMA((3,)),
                   pltpu.HBM(t.shape, t.dtype), pltpu.HBM(t.shape, t.dtype), _TOKEN),
        in_specs=(_HBM_SPEC, _HBM_SPEC), out_specs=(_SEM_SPEC, _SEM_SPEC, _HBM_SPEC, _HBM_SPEC, _VMEM_SPEC),
        input_output_aliases={0: 2, 1: 3},
        compiler_params=pltpu.CompilerParams(has_side_effects=_EFFECT),
    )(_hbm(t), _hbm(lax.empty(t.shape, t.dtype)))


def _scatter_wait(send_sems, recv_sems, t_thru, land_thru, after, name):
    def body(t_ref, land_ref, send_sems, recv_sems, after_ref, t_out, land_out):
        out, arrive = _scatter_copies(t_ref, land_ref, send_sems, recv_sems)
        for cp in out:
            cp.wait_send()
        for cp in arrive:
            cp.wait_recv()

    return pl.pallas_call(
        body, name=name,
        out_shape=(pltpu.HBM(t_thru.shape, t_thru.dtype), pltpu.HBM(land_thru.shape, land_thru.dtype)),
        in_specs=(_HBM_SPEC, _HBM_SPEC, _SEM_SPEC, _SEM_SPEC, _ANY), out_specs=(_HBM_SPEC, _HBM_SPEC),
        input_output_aliases={0: 0, 1: 1},
        compiler_params=pltpu.CompilerParams(has_side_effects=_EFFECT),
    )(t_thru, land_thru, send_sems, recv_sems, after)


def _all_gather_small(v, reduce, name):
    r, l = v.shape

    def body(v_ref, o_ref, *rest):
        if reduce:
            buf, send_sems, recv_sems = rest
        else:
            buf = o_ref
            send_sems, recv_sems = rest
        x, y, c, _ = _place()
        me = 4 * x + 2 * y + c
        buf[me] = v_ref[...]
        cps = []
        for d in range(1, N_DEV):
            peer = (x if d & 4 == 0 else 1 - x, y if d & 2 == 0 else 1 - y, c if d & 1 == 0 else 1 - c)
            cp = pltpu.make_async_remote_copy(
                src_ref=v_ref, dst_ref=buf.at[me], send_sem=send_sems.at[d - 1], recv_sem=recv_sems.at[d - 1],
                device_id=peer, device_id_type=MESH)
            cp.start()
            cps.append((cp, peer))
        for d, (cp, (px, py, pc)) in enumerate(cps, start=1):
            pltpu.make_async_remote_copy(
                src_ref=v_ref, dst_ref=buf.at[4 * px + 2 * py + pc], send_sem=send_sems.at[d - 1],
                recv_sem=recv_sems.at[d - 1], device_id=(px, py, pc), device_id_type=MESH).wait_recv()
        for cp, _ in cps:
            cp.wait_send()
        if reduce:
            acc = buf[0]
            for i in range(1, N_DEV):
                acc = acc + buf[i]
            o_ref[...] = acc

    vm = pl.BlockSpec(memory_space=pltpu.VMEM)
    out_shape = jax.ShapeDtypeStruct((r, l) if reduce else (N_DEV, r, l), F32)
    scratch = ([pltpu.VMEM((N_DEV, r, l), F32)] if reduce else []) + [
        pltpu.SemaphoreType.DMA((N_DEV - 1,)), pltpu.SemaphoreType.DMA((N_DEV - 1,))]
    return pl.pallas_call(
        body, name=name, in_specs=[vm], out_specs=vm, out_shape=out_shape, scratch_shapes=scratch,
    )(v)


_BLOCK_BYTES = 3 * 512 * 1024


def _rows_per_block(r, cn, itemsize=4):
    best = 8
    for t in range(8, r + 1, 8):
        if r % t == 0 and t * cn * itemsize <= _BLOCK_BYTES:
            best = t
    return best


def _add_sibling_half(g4, recv, cvec, name):
    ns, r, cn = g4.shape
    hr = r // 2
    tr = _rows_per_block(hr, cn)
    nrb = hr // tr

    def body(c_ref, a_ref, b_ref, o_ref):
        o_ref[...] = (a_ref[...].astype(F32) + b_ref[...].astype(F32)).astype(o_ref.dtype)

    grid_spec = pltpu.PrefetchScalarGridSpec(
        num_scalar_prefetch=1, grid=(ns, nrb),
        in_specs=[pl.BlockSpec((None, tr, cn), lambda j, i, c: (j, c[0] * nrb + i, 0)),
                  pl.BlockSpec((None, tr, cn), lambda j, i, c: (j, i, 0))],
        out_specs=pl.BlockSpec((None, tr, cn), lambda j, i, c: (j, i, 0)))
    return pl.pallas_call(
        body, name=name, grid_spec=grid_spec, out_shape=jax.ShapeDtypeStruct((ns, hr, cn), BF16),
        compiler_params=_cparams(("parallel", "parallel")),
    )(cvec, g4, recv)


def _sum_chips(r4, t4, kvec, cvec, name):
    ns, hr, cn = r4.shape
    tr = _rows_per_block(hr, cn)
    nrb = hr // tr

    def body(k_ref, c_ref, r_ref, t_ref, o_ref):
        acc = t_ref[...].astype(F32)
        for dlt in range(1, ns):
            acc = acc + r_ref[(k_ref[0] + dlt) % ns].astype(F32)
        o_ref[...] = acc

    grid_spec = pltpu.PrefetchScalarGridSpec(
        num_scalar_prefetch=2, grid=(nrb,),
        in_specs=[pl.BlockSpec((ns, tr, cn), lambda i, k, c: (0, i, 0)),
                  pl.BlockSpec((None, tr, cn), lambda i, k, c: (k[0], i, 0))],
        out_specs=pl.BlockSpec((tr, cn), lambda i, k, c: (c[0] * nrb + i, 0)))
    return pl.pallas_call(
        body, name=name, grid_spec=grid_spec, out_shape=jax.ShapeDtypeStruct((2 * hr, cn), F32),
        compiler_params=_cparams(("parallel",)),
    )(kvec, cvec, r4, t4)


def _adamw(w, g, m, v, name):
    r, cn = w.shape
    tr = _rows_per_block(r, cn)
    c1 = 1.0 - ADAM_B1 ** ADAM_STEP
    c2 = 1.0 - ADAM_B2 ** ADAM_STEP

    def body(w_ref, g_ref, m_ref, v_ref, go_ref, d_ref, mo_ref, vo_ref):
        gv = g_ref[...]
        mn = ADAM_B1 * m_ref[...] + (1.0 - ADAM_B1) * gv
        vn = ADAM_B2 * v_ref[...] + (1.0 - ADAM_B2) * (gv * gv)
        go_ref[...] = gv
        mo_ref[...] = mn
        vo_ref[...] = vn
        d_ref[...] = -ADAM_LR * ((mn / c1) / (jnp.sqrt(vn / c2) + ADAM_EPS) + ADAM_WD * w_ref[...])

    spec = pl.BlockSpec((tr, cn), lambda i: (i, 0))
    return pl.pallas_call(
        body, name=name, grid=(r // tr,), in_specs=[spec] * 4, out_specs=[spec] * 4,
        out_shape=[jax.ShapeDtypeStruct((r, cn), F32)] * 4,
        compiler_params=_cparams(("parallel",)),
    )(w, g, m, v)


def _pack(arrs):
    flat = jnp.concatenate([a.reshape(-1).astype(F32) for a in arrs])
    n = flat.shape[0]
    tot = -(-n // (8 * LANES)) * (8 * LANES)
    return jnp.pad(flat, (0, tot - n)).reshape(tot // LANES, LANES)


def _unpack(packed, shapes):
    flat = packed.reshape(-1)
    out, off = [], 0
    for shp in shapes:
        sz = int(np.prod(shp))
        out.append(flat[off:off + sz].reshape(shp))
        off += sz
    return out


class _LocalExchange:
    def __init__(self, ws4, wos4):
        self.ssm = [ws4, wos4]
        self.grads = {}

    def ssm_gather_start(self):
        return None

    def ssm_gather_mid(self, after):
        return None

    def ssm_gather_end(self, after):
        return self.ssm

    def grad_ready(self, name, g4, after=None):
        self.grads[name] = g4
        return None

    def grad_sync(self, name, after):
        pass

    def small_grads(self, small_full):
        self.small = small_full
        return None


class _Exchange:
    def __init__(self, kvec, cvec, ssm_bufs, after):
        self.kvec, self.cvec, self.bufs, self.after = kvec, cvec, ssm_bufs, after
        self.pending, self.summed, self.last_token = {}, {}, None

    def ssm_gather_start(self):
        self.sems = _gather_start(self.bufs, False, "ssm_gather_ici_start", self.after)
        self.bufs = self.sems[2]
        return self.sems[3]

    def ssm_gather_mid(self, after):
        bufs = _gather_wait(self.bufs, self.sems[0], self.sems[1], after, False, "ssm_gather_ici_wait")
        self.sems = _gather_start(bufs, True, "ssm_gather_fwd_start")
        self.bufs = self.sems[2]
        return self.sems[3]

    def ssm_gather_end(self, after):
        return _gather_wait(self.bufs, self.sems[0], self.sems[1], after, True, "ssm_gather_fwd_wait")

    def small_grads(self, small_full):
        packed = _all_gather_small(_pack(small_full), True, "reduce_small_grads")
        self.small = _unpack(packed, [t.shape for t in small_full])
        return packed

    def grad_ready(self, name, g4, after=None):
        recv = _swap_halves([g4], "grads_to_sibling_" + name, after)[0]
        t = _add_sibling_half(g4, recv, self.cvec, "add_sibling_" + name)
        send_sems, recv_sems, t_thru, land, token = _scatter_start(t, "scatter_start_" + name)
        self.pending[name] = (send_sems, recv_sems, t_thru, land)
        self.last_token = token
        return token

    def grad_sync(self, name, after):
        t, land = _scatter_wait(*self.pending.pop(name), after, "scatter_wait_" + name)
        self.summed[name] = _sum_chips(land, t, self.kvec, self.cvec, "sum_chips_" + name)


def _tie(vec, token):
    return vec if token is None else vec + token[0:1, 0:1].reshape((1,) * vec.ndim).astype(vec.dtype)


def _local_step(x2, xb, tgt, wa4, woa4, ex, conv_w_f, conv_b_f, norm_w_f, rel_bias, dt_bias, a_log, d_skip,
                ln_g, ln_b):
    s, d = x2.shape
    d_attn = woa4.shape[1]
    hpg = d_attn // HEAD_DIM
    d_inner = norm_w_f.shape[1]
    ng = d_inner // SSM_GROUP_WIDTH
    n_heads = dt_bias.shape[1]
    conv_dim = conv_w_f.shape[1]
    assert n_heads == ng * HEADS_PER_SSM_GROUP and conv_dim == d_inner + 2 * ng * D_STATE
    assert wa4.shape[2] * N_CHIPS == 10 * d_attn

    tok = ex.ssm_gather_start()
    buckets = _bucket_tiles()
    bias = _bias_expand(rel_bias, buckets, hpg)
    pgs, og, lg = [], [], []
    for g, (_, dil) in enumerate(ATTN_PATTERNS):
        pg = _mm_nn_sharded(xb, wa4, BF16, f"mm_in_attn_g{g}", after=tok,
                            col_off=3 * g * d_attn, n=3 * d_attn, classes=dil)
        o_, l_ = _attn_fwd(pg, bias, g, dil, hpg)
        pgs.append(pg)
        og.append(o_)
        lg.append(l_)
    gate = _mm_nn_sharded(xb, wa4, BF16, "mm_in_attn_gate", col_off=9 * d_attn, n=d_attn)
    o, lse, yat = _attn_combine(og, lg, gate, hpg)
    h0 = _mm_nn_sharded(yat, woa4, F32, "mm_out_attn", after=ex.ssm_gather_mid(yat))
    g0, b0, g1, b1 = ln_g[0:1], ln_b[0:1], ln_g[1:2], ln_b[1:2]
    xhat0, rstd0, x1b = _ln_fwd(x2, h0, g0, b0, "ln0_fwd")

    wst4, wos4 = ex.ssm_gather_end(x1b)
    wst = wst4.reshape(N_CHIPS * wst4.shape[1], d)
    nzx = d_inner + conv_dim
    wos = wos4.reshape(d_inner, d)
    pzx = _mm_nt(x1b, wst, BF16, "mm_in_ssm", n=nzx)
    dt_raw = _mm_nt(x1b, wst, F32, "mm_in_dt", n=n_heads, b_row_off=nzx)

    def pad_heads(t):
        t = t.reshape(t.shape[0], ng, HEADS_PER_SSM_GROUP).transpose(1, 0, 2)
        return jnp.pad(t, ((0, 0), (0, 0), (0, LANES - HEADS_PER_SSM_GROUP)))

    def unpad_heads(t):
        return t[:, :, :HEADS_PER_SSM_GROUP].transpose(1, 0, 2).reshape(t.shape[1], n_heads)

    dtp = pad_heads(dt_raw)
    alog_p, dtb_p = pad_heads(a_log), pad_heads(dt_bias)
    dsk_e = jnp.repeat(d_skip.reshape(ng, 1, HEADS_PER_SSM_GROUP), SSM_HEAD_DIM, axis=2)
    e = _expand_matrix()
    xbc = _conv_fwd(pzx, conv_w_f, conv_b_f, d_inner)
    y_ssd, states = _ssd_fwd(xbc, dtp, alog_p, dtb_p, dsk_e, e, d_inner)
    y3 = _gate_norm_fwd(y_ssd, pzx, norm_w_f)
    h1 = _mm_nn(y3, wos, F32, "mm_out_ssm")
    xhat1, rstd1, dy2, row_sq = _ln_fwd(xhat0, h1, g1, b1, "ln1_fwd_loss", affine_in=(g0, b0), target=tgt)
    loss_local = 0.5 * jnp.sum(row_sq) / d

    du1, du1b, dg1, db1 = _ln_bwd(dy2, xhat1, rstd1, g1, "ln1_bwd")
    dy3 = _mm_nt(du1b, wos, BF16, "mm_d_y3")
    g_wos = _mm_tn(y3, du1b, BF16, "mm_g_w_out_ssm").reshape(N_CHIPS, d_inner // N_CHIPS, d)
    norm_w_t = _tie(norm_w_f, ex.grad_ready("w_out_ssm", g_wos))
    dy_ssd, dz, d_nw = _gate_norm_bwd(dy3, y_ssd, pzx, norm_w_t)
    dxs, dbm, dcm, ddtp, d_alog, d_dtb, d_dsk = _ssd_bwd(xbc, dtp, alog_p, dtb_p, dsk_e, e, states, dy_ssd, d_inner)
    dpre, d_cw, d_cb = _conv_bwd_a(pzx, jnp.concatenate([dxs, dbm, dcm], axis=1), conv_w_f, conv_b_f, d_inner)
    dpzx = _conv_bwd_b(dpre, conv_w_f, dz, d_inner)
    ddt_raw = unpad_heads(ddtp)
    t1 = _mm_nn(ddt_raw, wst, F32, "mm_d_x1_dt", b_row_off=nzx, add=du1, add_scale=DEEPNORM_ALPHA)
    dx1 = _mm_nn(dpzx, wst, F32, "mm_d_x1", add=t1)
    ex.grad_sync("w_out_ssm", dx1)
    g_wst = _mm_tn(dpzx, x1b, BF16, "mm_g_w_in_ssm", out_rows=wst.shape[0])
    g_wst = _mm_tn(ddt_raw, x1b, BF16, "mm_g_w_dt", out_rows=wst.shape[0], out_row_off=nzx, into=g_wst)
    g0_t = _tie(g0, ex.grad_ready("w_in_ssm", g_wst.reshape(wst4.shape)))

    du0, du0b, dg0, db0 = _ln_bwd(dx1, xhat0, rstd0, g0_t, "ln0_bwd")
    dyat = _mm_nt_sharded_k(du0b, woa4, BF16, "mm_d_yat")
    g_woa = _mm_tn(yat, du0b, BF16, "mm_g_w_out_attn", shard_cols=d // N_CHIPS)
    tok_woa = ex.grad_ready("w_out_attn", g_woa)
    do, delta, dgate = _attn_pre_bwd(dyat, o, gate, hpg)
    pieces, dbt = [], []
    for g, (_, dil) in enumerate(ATTN_PATTERNS):
        dq, dk, dv, db_ = _attn_bwd(pgs[g], bias, do, lse, delta, g, dil, hpg)
        pieces += [dq, dk, dv]
        dbt.append(db_)
    dpa = jnp.concatenate(pieces + [dgate], axis=1)
    g_wa = _mm_tn(xb, dpa, BF16, "mm_g_w_in_attn", shard_cols=wa4.shape[2], after=tok_woa)
    ex.grad_sync("w_in_ssm", g_wa)
    ex.grad_sync("w_out_attn", g_wa)
    d_rel = _bias_reduce(jnp.stack(dbt), buckets, hpg)[:, :, 0].T
    d_dsk_h = d_dsk.reshape(n_heads, SSM_HEAD_DIM).sum(axis=1)
    small_full = [d_rel, d_cw, d_cb, unpad_heads(d_dtb), unpad_heads(d_alog), d_dsk_h[None], d_nw,
                  jnp.concatenate([dg0, dg1], axis=0), jnp.concatenate([db0, db1], axis=0)]
    tok_wa = ex.grad_ready("w_in_attn", g_wa, after=ex.small_grads(small_full))
    grad_x = _mm_nt_sharded_k(dpa, wa4, F32, "mm_d_x0", add=du0, add_scale=DEEPNORM_ALPHA, after=tok_wa)
    return loss_local, grad_x[None]


def kernel(x, w_in_attn, w_out_attn, rel_bias, w_in_ssm, conv_w, conv_b, dt_bias, a_log, d_skip, ssm_norm_w, w_out_ssm, ln_g, ln_b, loss_target, m_w_in_attn, m_w_out_attn, m_rel_bias, m_w_in_ssm, m_conv_w, m_conv_b, m_dt_bias, m_a_log, m_d_skip, m_ssm_norm_w, m_w_out_ssm, m_ln_g, m_ln_b, v_w_in_attn, v_w_out_attn, v_rel_bias, v_w_in_ssm, v_conv_w, v_conv_b, v_dt_bias, v_a_log, v_d_skip, v_ssm_norm_w, v_w_out_ssm, v_ln_g, v_ln_b):
    xi, yi, ci = lax.axis_index("x"), lax.axis_index("y"), lax.axis_index("c")
    chip = 2 * xi + yi
    cvec = jnp.reshape(ci, (1,)).astype(jnp.int32)
    kvec = jnp.reshape(chip, (1,)).astype(jnp.int32)

    cw_l, cb_l, nw_l = conv_w[0], conv_b[0], ssm_norm_w[0]
    vec_shapes = [cw_l.shape, cb_l.shape, nw_l.shape]
    vec_all = _all_gather_small(_pack([cw_l, cb_l, nw_l]), False, "gather_vectors")
    l0 = [_cast_to_slot(w_in_attn[0], kvec, "cast_w_in_attn"), _cast_to_slot(w_out_attn[0], kvec, "cast_w_out_attn")]
    send0, recv0, l0, tok0 = _gather_start(l0, False, "l0_gather_ici_start", vec_all)
    xb = _cast_bf16(x[0], "cast_x", tok0)
    ssm_bufs = [_cast_to_slot(w_in_ssm[0].T, kvec, "cast_w_in_ssm", tok0),
                _cast_to_slot(w_out_ssm[0], kvec, "cast_w_out_ssm", tok0)]
    l0 = _gather_wait(l0, send0, recv0, xb, False, "l0_gather_ici_wait")
    send0, recv0, l0, _ = _gather_start(l0, True, "l0_gather_fwd_start")
    wa4, woa4 = _gather_wait(l0, send0, recv0, ssm_bufs[0], True, "l0_gather_fwd_wait")
    ex = _Exchange(kvec, cvec, ssm_bufs, after=woa4)
    parts = [_unpack(vec_all[2 * j], vec_shapes) for j in range(N_CHIPS)]
    conv_w_f = jnp.concatenate([p[0] for p in parts], axis=1)
    conv_b_f = jnp.concatenate([p[1] for p in parts], axis=0)[None]
    norm_w_f = jnp.concatenate([p[2] for p in parts], axis=0)[None]

    loss_local, grad_x = _local_step(
        x[0], xb, loss_target[0], wa4, woa4, ex, conv_w_f, conv_b_f, norm_w_f, rel_bias, dt_bias, a_log,
        d_skip, ln_g, ln_b)
    loss = lax.psum(loss_local, ("x", "y", "c"))

    big_w = dict(w_in_attn=(w_in_attn, m_w_in_attn, v_w_in_attn), w_out_attn=(w_out_attn, m_w_out_attn, v_w_out_attn),
                 w_in_ssm=(w_in_ssm, m_w_in_ssm, v_w_in_ssm), w_out_ssm=(w_out_ssm, m_w_out_ssm, v_w_out_ssm))
    big = {}

    def finish(names, join_name, after):
        last = None
        for nm, gf in zip(names, _join_halves([ex.summed[nm] for nm in names], join_name, after)):
            flip = (lambda t: t.T) if nm == "w_in_ssm" else (lambda t: t)
            w_, m_, v_ = (flip(t[0]) for t in big_w[nm])
            res = _adamw(w_, gf, m_, v_, "adamw_" + nm)
            big[nm] = [flip(t)[None] for t in res]
            last = res[3]
        return last

    last = finish(["w_out_ssm", "w_in_ssm", "w_out_attn"], "grads_join_halves_a", ex.last_token)
    ex.grad_sync("w_in_attn", last)
    finish(["w_in_attn"], "grads_join_halves_b", None)

    s_rel, s_cw, s_cb, s_dtb, s_alog, s_dsk, s_nw, s_lng, s_lnb = ex.small
    cwc, nwc = conv_w.shape[2], ssm_norm_w.shape[1]
    s_cw = lax.dynamic_slice_in_dim(s_cw, chip * cwc, cwc, axis=1)[None]
    s_cb = lax.dynamic_slice_in_dim(s_cb, chip * cwc, cwc, axis=1)
    s_nw = lax.dynamic_slice_in_dim(s_nw, chip * nwc, nwc, axis=1)
    small_names = ["rel_bias", "conv_w", "conv_b", "dt_bias", "a_log", "d_skip", "ssm_norm_w", "ln_g", "ln_b"]
    small_g = [s_rel, s_cw, s_cb, s_dtb, s_alog, s_dsk, s_nw, s_lng, s_lnb]
    small_w = [rel_bias, conv_w, conv_b, dt_bias, a_log, d_skip, ssm_norm_w, ln_g, ln_b]
    small_m = [m_rel_bias, m_conv_w, m_conv_b, m_dt_bias, m_a_log, m_d_skip, m_ssm_norm_w, m_ln_g, m_ln_b]
    small_v = [v_rel_bias, v_conv_w, v_conv_b, v_dt_bias, v_a_log, v_d_skip, v_ssm_norm_w, v_ln_g, v_ln_b]
    shapes = [t.shape for t in small_w]
    res = _adamw(_pack(small_w), _pack(small_g), _pack(small_m), _pack(small_v), "adamw_small")
    small = {nm: [] for nm in small_names}
    for packed in res:
        for nm, t in zip(small_names, _unpack(packed, shapes)):
            small[nm].append(t)

    order = ["w_in_attn", "w_out_attn", "rel_bias", "w_in_ssm", "conv_w", "conv_b", "dt_bias", "a_log",
             "d_skip", "ssm_norm_w", "w_out_ssm", "ln_g", "ln_b"]
    table = {**big, **small}
    outs = [loss, grad_x]
    for kind in range(4):
        outs += [table[nm][kind] for nm in order]
    return tuple(outs)
```

```python
import functools
import math

import numpy as np
import jax
import jax.numpy as jnp
from jax import lax
from jax.experimental import pallas as pl
from jax.experimental.pallas import tpu as pltpu

F32 = jnp.float32
BF16 = jnp.bfloat16
MESH = pl.DeviceIdType.MESH

ATTN_PATTERNS = ((128, 1), (512, 4), (2048, 16))
N_GROUPS_ATTN = 3
HEAD_DIM = 128
ATTN_BLOCK = 128
NUM_BUCKETS = 32
MAX_DISTANCE = 2048
SSM_HEAD_DIM = 64
HEADS_PER_SSM_GROUP = 16
SSM_GROUP_WIDTH = HEADS_PER_SSM_GROUP * SSM_HEAD_DIM
D_STATE = 128
CONV_WIDTH = 4
CHUNK = 128
DEPTH = 2
DEEPNORM_ALPHA = (2 * DEPTH) ** 0.25
LN_EPS = 1e-5
RMS_EPS = 1e-5
NEG_INF = -1e30
ADAM_LR = 0.001
ADAM_B1 = 0.9
ADAM_B2 = 0.999
ADAM_EPS = 1e-08
ADAM_WD = 0.01
ADAM_STEP = 10

N_CHIPS = 4
N_DEV = 8

VMEM_LIMIT_V7X = 56 * 1024 * 1024
LANES = 128


def _cparams(sem=None):
    return pltpu.CompilerParams(dimension_semantics=sem, vmem_limit_bytes=VMEM_LIMIT_V7X)


def _sigmoid(x):
    return 0.5 * jnp.tanh(0.5 * x) + 0.5


def _dot(a, b):
    return jnp.dot(a, b, preferred_element_type=F32)


def _dot_nt(a, b):
    return lax.dot_general(a, b, (((1,), (1,)), ((), ())), preferred_element_type=F32)


def _dot_tn(a, b):
    return lax.dot_general(a, b, (((0,), (0,)), ((), ())), preferred_element_type=F32)


def _split2(x):
    hi = x.astype(BF16)
    lo = (x - hi.astype(F32)).astype(BF16)
    return hi, lo


def _split3(x):
    hi = x.astype(BF16)
    r = x - hi.astype(F32)
    mid = r.astype(BF16)
    lo = (r - mid.astype(F32)).astype(BF16)
    return hi, mid, lo


def _matmul(a, b, *, mode, grid, a_spec, b_spec, out_shape, out_spec, tile, name,
            add=None, add_spec=None, add_scale=1.0, after=None, into=None):
    nk = grid[2]
    tm, tn = tile
    dot = {"nn": _dot, "nt": _dot_nt, "tn": _dot_tn}[mode]
    has_add = add is not None
    has_after = after is not None
    has_into = into is not None

    def finish(r, add_ref, o_ref):
        if has_add:
            r = r + add_scale * add_ref[...].astype(F32)
        o_ref[...] = r.astype(o_ref.dtype)

    def body_one(*refs):
        a_ref, b_ref = refs[:2]
        finish(dot(a_ref[...].astype(BF16), b_ref[...].astype(BF16)), refs[2] if has_add else None, refs[-1])

    def body_acc(*refs):
        a_ref, b_ref = refs[:2]
        add_ref = refs[2] if has_add else None
        o_ref, acc_ref = refs[-2:]
        k = pl.program_id(2)

        @pl.when(k == 0)
        def _():
            acc_ref[...] = jnp.zeros_like(acc_ref)

        acc_ref[...] += dot(a_ref[...].astype(BF16), b_ref[...].astype(BF16))

        @pl.when(k == nk - 1)
        def _():
            finish(acc_ref[...], add_ref, o_ref)

    in_specs = ([a_spec, b_spec] + ([add_spec] if has_add else []) + ([_ANY] if has_after else [])
                + ([_ANY] if has_into else []))
    args = (a, b) + ((add,) if has_add else ()) + ((after,) if has_after else ()) + ((into,) if has_into else ())
    return pl.pallas_call(
        body_one if nk == 1 else body_acc, name=name, grid=grid, in_specs=in_specs, out_specs=out_spec,
        out_shape=out_shape,
        input_output_aliases={len(args) - 1: 0} if has_into else {},
        scratch_shapes=[] if nk == 1 else [pltpu.VMEM((tm, tn), F32)],
        compiler_params=_cparams(("parallel", "parallel", "arbitrary")),
    )(*args)


def _pick(n, pref):
    for t in pref:
        if n % t == 0:
            return t
    return n


_TILE_PREF = (1024, 512, 256, 128)
_K_TILE_PREF = (2048,) + _TILE_PREF


def _k_tile(k, out_dtype, has_add):
    return _pick(k, _K_TILE_PREF if (has_add or out_dtype != BF16) else (4096,) + _K_TILE_PREF)


def _mm_nn_sharded(a, w4, out_dtype, name, after=None, col_off=0, n=None, classes=1):
    m, k = a.shape
    _, _, nn = w4.shape
    n = N_CHIPS * nn if n is None else n
    tm, tk = _pick(m // classes, _TILE_PREF), _k_tile(k, out_dtype, False)
    tn = _pick(math.gcd(math.gcd(nn, n), col_off) if col_off else math.gcd(nn, n), _TILE_PREF)
    npb = nn // tn
    co = col_off // tn
    bpc, kb = m // classes // tm, k // tk
    av = a.reshape(m // classes, classes * k)
    out_shape = jax.ShapeDtypeStruct((m, n), out_dtype)
    if tm < _TILE_PREF[0]:
        return _matmul(
            av, w4, mode="nn", grid=(n // tn, m // tm, 1), tile=(tm, tn), name=name,
            a_spec=pl.BlockSpec((tm, k), lambda j, i, kk: (i % bpc, i // bpc)),
            b_spec=pl.BlockSpec((None, k, tn), lambda j, i, kk: ((j + co) // npb, 0, (j + co) % npb)),
            out_shape=out_shape, out_spec=pl.BlockSpec((tm, tn), lambda j, i, kk: (i, j)), after=after)
    return _matmul(
        av, w4, mode="nn", grid=(m // tm, n // tn, kb), tile=(tm, tn), name=name,
        a_spec=pl.BlockSpec((tm, tk), lambda i, j, kk: (i % bpc, (i // bpc) * kb + kk)),
        b_spec=pl.BlockSpec((None, tk, tn), lambda i, j, kk: ((j + co) // npb, kk, (j + co) % npb)),
        out_shape=out_shape, out_spec=pl.BlockSpec((tm, tn), lambda i, j, kk: (i, j)), after=after)


def _mm_nn(a, b, out_dtype, name, b_row_off=0, add=None, add_scale=1.0):
    m, k = a.shape
    _, n = b.shape
    tm, tk, tn = _pick(m, _TILE_PREF), _k_tile(k, out_dtype, add is not None), _pick(n, _TILE_PREF)
    assert b_row_off % tk == 0
    ko = b_row_off // tk
    return _matmul(
        a, b, mode="nn", grid=(m // tm, n // tn, k // tk), tile=(tm, tn), name=name,
        a_spec=pl.BlockSpec((tm, tk), lambda i, j, kk: (i, kk)),
        b_spec=pl.BlockSpec((tk, tn), lambda i, j, kk: (kk + ko, j)),
        out_shape=jax.ShapeDtypeStruct((m, n), out_dtype),
        out_spec=pl.BlockSpec((tm, tn), lambda i, j, kk: (i, j)),
        add=add, add_spec=pl.BlockSpec((tm, tn), lambda i, j, kk: (i, j)), add_scale=add_scale)


def _mm_nt(a, b, out_dtype, name, add=None, add_scale=1.0, n=None, b_row_off=0):
    m, k = a.shape
    n = b.shape[0] if n is None else n
    tm, tk, tn = _pick(m, _TILE_PREF), _k_tile(k, out_dtype, add is not None), _pick(n, _TILE_PREF)
    assert b_row_off % tn == 0
    no = b_row_off // tn
    return _matmul(
        a, b, mode="nt", grid=(m // tm, n // tn, k // tk), tile=(tm, tn), name=name,
        a_spec=pl.BlockSpec((tm, tk), lambda i, j, kk: (i, kk)),
        b_spec=pl.BlockSpec((tn, tk), lambda i, j, kk: (j + no, kk)),
        out_shape=jax.ShapeDtypeStruct((m, n), out_dtype),
        out_spec=pl.BlockSpec((tm, tn), lambda i, j, kk: (i, j)),
        add=add, add_spec=pl.BlockSpec((tm, tn), lambda i, j, kk: (i, j)), add_scale=add_scale)


def _mm_nt_sharded_k(a, w4, out_dtype, name, add=None, add_scale=1.0, after=None):
    m, _ = a.shape
    _, n, kn = w4.shape
    tm, tk, tn = _pick(m, _TILE_PREF), _pick(kn, (2560,) + _TILE_PREF), _pick(n, _TILE_PREF)
    kpb = kn // tk
    return _matmul(
        a, w4, mode="nt", grid=(m // tm, n // tn, N_CHIPS * kpb), tile=(tm, tn), name=name,
        a_spec=pl.BlockSpec((tm, tk), lambda i, j, kk: (i, kk)),
        b_spec=pl.BlockSpec((None, tn, tk), lambda i, j, kk: (kk // kpb, j, kk % kpb)),
        out_shape=jax.ShapeDtypeStruct((m, n), out_dtype),
        out_spec=pl.BlockSpec((tm, tn), lambda i, j, kk: (i, j)),
        add=add, add_spec=pl.BlockSpec((tm, tn), lambda i, j, kk: (i, j)), add_scale=add_scale, after=after)


def _mm_tn(a, b, out_dtype, name, shard_cols=None, out_rows=None, out_row_off=0, into=None, after=None):
    k, m = a.shape
    _, n = b.shape
    nn = n if shard_cols is None else shard_cols
    tm, tk, tn = _pick(m, _TILE_PREF), _k_tile(k, out_dtype, False), _pick(nn, _TILE_PREF)
    if shard_cols is None:
        assert out_row_off % tm == 0
        ro = out_row_off // tm
        out_shape = jax.ShapeDtypeStruct((m if out_rows is None else out_rows, n), out_dtype)
        out_spec = pl.BlockSpec((tm, tn), lambda i, j, kk: (i + ro, j))
    else:
        npb = nn // tn
        out_shape = jax.ShapeDtypeStruct((n // nn, m, nn), out_dtype)
        out_spec = pl.BlockSpec((None, tm, tn), lambda i, j, kk: (j // npb, i, j % npb))
    return _matmul(
        a, b, mode="tn", grid=(m // tm, n // tn, k // tk), tile=(tm, tn), name=name,
        a_spec=pl.BlockSpec((tk, tm), lambda i, j, kk: (kk, i)),
        b_spec=pl.BlockSpec((tk, tn), lambda i, j, kk: (kk, j)),
        out_shape=out_shape, out_spec=out_spec, into=into, after=after)


def _cast_bf16(x, name, after=None):
    r, c = x.shape
    tr = _pick(r, (512, 256, 128, 8))
    extra = [] if after is None else [after]

    def body(x_ref, *rest):
        rest[-1][...] = x_ref[...].astype(BF16)

    return pl.pallas_call(
        body, name=name, grid=(r // tr,),
        in_specs=[pl.BlockSpec((tr, c), lambda i: (i, 0))] + [_ANY] * len(extra),
        out_specs=pl.BlockSpec((tr, c), lambda i: (i, 0)),
        out_shape=jax.ShapeDtypeStruct((r, c), BF16),
        compiler_params=_cparams(("parallel",)),
    )(x, *extra)


def _bucket_tiles():
    qi = np.arange(ATTN_BLOCK)[:, None]
    ki = np.arange(2 * ATTN_BLOCK)[None, :]
    delta = np.clip(ATTN_BLOCK + qi - ki, 0, None)
    tiles = []
    max_exact = NUM_BUCKETS // 2
    for _, dil in ATTN_PATTERNS:
        dist = (delta * dil).astype(np.int32)
        d_f = np.maximum(dist, 1).astype(np.float32)
        large = max_exact + (np.log(d_f / np.float32(max_exact)) / np.float32(math.log(MAX_DISTANCE / max_exact))
                             * np.float32(NUM_BUCKETS - max_exact)).astype(np.int32)
        large = np.minimum(large, NUM_BUCKETS - 1)
        tiles.append(np.where(dist < max_exact, dist, large).astype(np.int32))
    return jnp.asarray(np.stack(tiles))


def _bias_expand(rel_bias, buckets, hpg):
    def body(tab_ref, bk_ref, o_ref):
        g, h = pl.program_id(0), pl.program_id(1)
        bk = bk_ref[...]
        acc = jnp.zeros((ATTN_BLOCK, 2 * ATTN_BLOCK), F32)
        for b in range(NUM_BUCKETS):
            acc = jnp.where(bk == b, tab_ref[b, g * hpg + h], acc)
        o_ref[...] = acc

    return pl.pallas_call(
        body, name="bias_expand", grid=(N_GROUPS_ATTN, hpg),
        in_specs=[pl.BlockSpec(memory_space=pltpu.SMEM),
                  pl.BlockSpec((None, ATTN_BLOCK, 2 * ATTN_BLOCK), lambda g, h: (g, 0, 0))],
        out_specs=pl.BlockSpec((None, None, ATTN_BLOCK, 2 * ATTN_BLOCK), lambda g, h: (g, h, 0, 0)),
        out_shape=jax.ShapeDtypeStruct((N_GROUPS_ATTN, hpg, ATTN_BLOCK, 2 * ATTN_BLOCK), F32),
        compiler_params=_cparams(("parallel", "parallel")),
    )(rel_bias, buckets)


def _bias_reduce(dtiles, buckets, hpg):
    def body(t_ref, bk_ref, o_ref):
        bk = bk_ref[...]
        t = t_ref[...]
        rows = lax.broadcasted_iota(jnp.int32, (NUM_BUCKETS, LANES), 0)
        acc = jnp.zeros((NUM_BUCKETS, LANES), F32)
        for b in range(NUM_BUCKETS):
            s = jnp.sum(jnp.sum(jnp.where(bk == b, t, 0.0), axis=1, keepdims=True), axis=0, keepdims=True)
            acc = jnp.where(rows == b, s, acc)
        o_ref[...] = acc

    return pl.pallas_call(
        body, name="bias_reduce", grid=(N_GROUPS_ATTN, hpg),
        in_specs=[pl.BlockSpec((None, None, ATTN_BLOCK, 2 * ATTN_BLOCK), lambda g, h: (g, h, 0, 0)),
                  pl.BlockSpec((None, ATTN_BLOCK, 2 * ATTN_BLOCK), lambda g, h: (g, 0, 0))],
        out_specs=pl.BlockSpec((None, NUM_BUCKETS, LANES), lambda g, h: (g * hpg + h, 0, 0)),
        out_shape=jax.ShapeDtypeStruct((N_GROUPS_ATTN * hpg, NUM_BUCKETS, LANES), F32),
        compiler_params=_cparams(("parallel", "parallel")),
    )(dtiles, buckets)


def _cast_x3(x, name, after=None):
    r, c = x.shape
    tr = _pick(r, (512, 256, 128, 8))
    extra = [] if after is None else [after]

    def body(x_ref, *rest):
        rest[-1][...] = x_ref[...].astype(BF16)

    return pl.pallas_call(
        body, name=name, grid=(r // tr,),
        in_specs=[pl.BlockSpec((tr, c), lambda i: (i, 0))] + [_ANY] * len(extra),
        out_specs=pl.BlockSpec((None, tr, c), lambda i: (0, i, 0)),
        out_shape=jax.ShapeDtypeStruct((N_GROUPS_ATTN, r, c), BF16),
        compiler_params=_cparams(("parallel",)),
    )(x, *extra)


def _class_copy(x3, slot, dil, name):
    _, s, d = x3.shape
    rows = s // dil
    tm = _pick(rows, (512, 256, 128))
    nbk = rows // tm

    def body(v_ref, x3_ref, o_ref):
        o_ref[...] = v_ref[...]

    return pl.pallas_call(
        body, name=name, grid=(dil, nbk),
        in_specs=[pl.BlockSpec((tm, d), lambda r, i: (i, r)), _ANY],
        out_specs=pl.BlockSpec((None, tm, d), lambda r, i: (slot, r * nbk + i, 0)),
        out_shape=jax.ShapeDtypeStruct(x3.shape, x3.dtype), input_output_aliases={1: 0},
        compiler_params=_cparams(("parallel", "parallel")),
    )(x3[0].reshape(rows, dil * d), x3)


def _in_proj_shard(x3, wa4, kvec, p, d_attn, name, into=None, after=None):
    _, s, d = x3.shape
    _, _, nn = wa4.shape
    tm = _pick(s, _TILE_PREF)
    tn = _pick(math.gcd(nn, 3 * d_attn), _TILE_PREF)
    npb, bpg = nn // tn, 3 * d_attn // tn
    extra = ([] if after is None else [after]) + ([] if into is None else [into])

    def block(k, j):
        return jnp.bitwise_xor(k[0], p) * npb + j

    def slot(k, j):
        jb = block(k, j)
        return jnp.where(jb < N_GROUPS_ATTN * bpg, jb // bpg, 0)

    def body(k_ref, a_ref, b_ref, *rest):
        rest[-1][...] = _dot(a_ref[...], b_ref[...]).astype(BF16)

    grid_spec = pltpu.PrefetchScalarGridSpec(
        num_scalar_prefetch=1, grid=(s // tm, npb),
        in_specs=[pl.BlockSpec((None, tm, d), lambda i, j, k: (slot(k, j), i, 0)),
                  pl.BlockSpec((None, d, tn), lambda i, j, k: (jnp.bitwise_xor(k[0], p), 0, j))]
        + [_ANY] * len(extra),
        out_specs=pl.BlockSpec((tm, tn), lambda i, j, k: (i, block(k, j))))
    return pl.pallas_call(
        body, name=name, grid_spec=grid_spec, out_shape=jax.ShapeDtypeStruct((s, N_CHIPS * nn), BF16),
        input_output_aliases={} if into is None else {2 + len(extra): 0},
        compiler_params=_cparams(("parallel", "parallel")),
    )(kvec, x3, wa4, *extra)


def _attn_valid(n_is_first):
    qi = lax.broadcasted_iota(jnp.int32, (ATTN_BLOCK, 2 * ATTN_BLOCK), 0)
    ki = lax.broadcasted_iota(jnp.int32, (ATTN_BLOCK, 2 * ATTN_BLOCK), 1)
    delta = ATTN_BLOCK + qi - ki
    band = (delta >= 0) & (delta <= ATTN_BLOCK)
    return band & (jnp.logical_not(n_is_first) | (ki >= ATTN_BLOCK))


def _attn_fwd(pg, bias, g, dil, hpg):
    s = pg.shape[0]
    w = hpg * HEAD_DIM
    rows = s // dil
    nb = rows // ATTN_BLOCK
    scale = HEAD_DIM ** -0.5

    def body(q_ref, kc_ref, kp_ref, vc_ref, vp_ref, bias_ref, o_ref, lse_ref):
        valid = _attn_valid(pl.program_id(1) == 0)
        lane = lax.broadcasted_iota(jnp.int32, (ATTN_BLOCK, LANES), 1)
        lse = jnp.zeros((ATTN_BLOCK, LANES), F32)
        for h in range(hpg):
            sl = slice(h * HEAD_DIM, (h + 1) * HEAD_DIM)
            k2 = jnp.concatenate([kp_ref[:, sl], kc_ref[:, sl]], axis=0)
            v2 = jnp.concatenate([vp_ref[:, sl], vc_ref[:, sl]], axis=0)
            sc = _dot_nt(q_ref[:, sl], k2) * scale + bias_ref[h]
            sc = jnp.where(valid, sc, NEG_INF)
            m = jnp.max(sc, axis=1, keepdims=True)
            p = jnp.exp(sc - m)
            l = jnp.sum(p, axis=1, keepdims=True)
            o_ref[:, sl] = _dot(p.astype(BF16), v2) * (1.0 / l)
            lse = jnp.where(lane == h, m + jnp.log(l), lse)
        lse_ref[...] = lse

    def col(off):
        return lambda r, n: (r * nb + n, 3 * g + off)

    def colp(off):
        return lambda r, n: (r * nb + jnp.maximum(n - 1, 0), 3 * g + off)

    blk = (ATTN_BLOCK, w)
    tok = pl.BlockSpec(blk, lambda r, n: (n, r))
    tok1 = pl.BlockSpec((ATTN_BLOCK, LANES), lambda r, n: (n, r))
    o, lse = pl.pallas_call(
        body, name=f"attn_fwd_g{g}", grid=(dil, nb),
        in_specs=[pl.BlockSpec(blk, col(0)), pl.BlockSpec(blk, col(1)), pl.BlockSpec(blk, colp(1)),
                  pl.BlockSpec(blk, col(2)), pl.BlockSpec(blk, colp(2)),
                  pl.BlockSpec((None, hpg, ATTN_BLOCK, 2 * ATTN_BLOCK), lambda r, n: (g, 0, 0, 0))],
        out_specs=[tok, tok1],
        out_shape=[jax.ShapeDtypeStruct((rows, dil * w), F32), jax.ShapeDtypeStruct((rows, dil * LANES), F32)],
        compiler_params=_cparams(("parallel", "parallel")),
    )(pg, pg, pg, pg, pg, bias)
    return o.reshape(s, w), lse.reshape(s, LANES)


def _attn_combine(os_, lses, pa, hpg):
    s, w = os_[0].shape
    gate_blk = pa.shape[1] // w - 1
    tm = _pick(s, (256, 128))

    def body(o0, o1, o2, l0, l1, l2, gate_ref, o_ref, lse_ref, y_ref):
        a0, a1, a2 = l0[...], l1[...], l2[...]
        m = jnp.maximum(jnp.maximum(a0, a1), a2)
        e0, e1, e2 = jnp.exp(a0 - m), jnp.exp(a1 - m), jnp.exp(a2 - m)
        den = e0 + e1 + e2
        inv = 1.0 / den
        w0, w1, w2 = e0 * inv, e1 * inv, e2 * inv
        lse_ref[...] = m + jnp.log(den)
        for h in range(hpg):
            sl = slice(h * HEAD_DIM, (h + 1) * HEAD_DIM)
            o = w0[:, h:h + 1] * o0[:, sl] + w1[:, h:h + 1] * o1[:, sl] + w2[:, h:h + 1] * o2[:, sl]
            gate = gate_ref[:, sl].astype(F32)
            o_ref[:, sl] = o.astype(BF16)
            y_ref[:, sl] = (o * (gate * _sigmoid(gate))).astype(BF16)

    spec = pl.BlockSpec((tm, w), lambda i: (i, 0))
    spec1 = pl.BlockSpec((tm, LANES), lambda i: (i, 0))
    return pl.pallas_call(
        body, name="attn_combine", grid=(s // tm,),
        in_specs=[spec] * 3 + [spec1] * 3 + [pl.BlockSpec((tm, w), lambda i: (i, gate_blk))],
        out_specs=[spec, spec1, spec],
        out_shape=[jax.ShapeDtypeStruct((s, w), BF16), jax.ShapeDtypeStruct((s, LANES), F32),
                   jax.ShapeDtypeStruct((s, w), BF16)],
        compiler_params=_cparams(("parallel",)),
    )(*os_, *lses, pa)


def _attn_pre_bwd(dy, o, pa, hpg):
    s, w = dy.shape
    gate_blk = pa.shape[1] // w - 1
    tm = _pick(s, (256, 128))

    def body(dy_ref, o_ref, gate_ref, do_ref, dl_ref, dg_ref):
        gate = gate_ref[...].astype(F32)
        sg = _sigmoid(gate)
        dyv = dy_ref[...].astype(F32)
        ov = o_ref[...].astype(F32)
        do = dyv * (gate * sg)
        do_ref[...] = do.astype(BF16)
        dg_ref[...] = (dyv * ov * (sg * (1.0 + gate * (1.0 - sg)))).astype(BF16)
        prod = do * ov
        lane = lax.broadcasted_iota(jnp.int32, (tm, LANES), 1)
        dl = jnp.zeros((tm, LANES), F32)
        for h in range(hpg):
            sl = slice(h * HEAD_DIM, (h + 1) * HEAD_DIM)
            dl = jnp.where(lane == h, jnp.sum(prod[:, sl], axis=1, keepdims=True), dl)
        dl_ref[...] = dl

    spec = pl.BlockSpec((tm, w), lambda i: (i, 0))
    return pl.pallas_call(
        body, name="attn_pre_bwd", grid=(s // tm,),
        in_specs=[spec, spec, pl.BlockSpec((tm, w), lambda i: (i, gate_blk))],
        out_specs=[spec, pl.BlockSpec((tm, LANES), lambda i: (i, 0)), spec],
        out_shape=[jax.ShapeDtypeStruct((s, w), BF16), jax.ShapeDtypeStruct((s, LANES), F32),
                   jax.ShapeDtypeStruct((s, w), BF16)],
        compiler_params=_cparams(("parallel",)),
    )(dy, o, pa)


def _attn_bwd(pg, bias, do, lse, delta, g, dil, hpg):
    s = pg.shape[0]
    w = hpg * HEAD_DIM
    rows = s // dil
    nb = rows // ATTN_BLOCK
    dov = do.reshape(rows, dil * w)
    lsev, dlv = (t.reshape(rows, dil * LANES) for t in (lse, delta))
    scale = HEAD_DIM ** -0.5

    def body(q_ref, kc_ref, kp_ref, vc_ref, vp_ref, bias_ref, do_ref, lse_ref, dl_ref,
             dq_ref, dk_ref, dv_ref, db_ref, dkc_ref, dvc_ref):
        r, i = pl.program_id(0), pl.program_id(1)
        n = nb - 1 - i
        valid = _attn_valid(n == 0)

        @pl.when((r == 0) & (i == 0))
        def _():
            db_ref[...] = jnp.zeros_like(db_ref)

        @pl.when(i == 0)
        def _():
            dkc_ref[...] = jnp.zeros_like(dkc_ref)
            dvc_ref[...] = jnp.zeros_like(dvc_ref)

        for h in range(hpg):
            sl = slice(h * HEAD_DIM, (h + 1) * HEAD_DIM)
            q = q_ref[:, sl]
            dov_ = do_ref[:, sl]
            k2 = jnp.concatenate([kp_ref[:, sl], kc_ref[:, sl]], axis=0)
            v2 = jnp.concatenate([vp_ref[:, sl], vc_ref[:, sl]], axis=0)
            sc = _dot_nt(q, k2) * scale + bias_ref[h]
            p = jnp.exp(jnp.where(valid, sc - lse_ref[:, h:h + 1], NEG_INF))
            dp = _dot_nt(dov_, v2)
            ds = p * (dp - dl_ref[:, h:h + 1])
            db_ref[h] += ds
            dsb = ds.astype(BF16)
            dq_ref[:, sl] = (_dot(dsb, k2) * scale).astype(BF16)
            dk2 = _dot_tn(dsb, q) * scale
            dv2 = _dot_tn(p.astype(BF16), dov_)
            dk_ref[:, sl] = (dk2[ATTN_BLOCK:] + dkc_ref[:, sl]).astype(BF16)
            dv_ref[:, sl] = (dv2[ATTN_BLOCK:] + dvc_ref[:, sl]).astype(BF16)
            dkc_ref[:, sl] = dk2[:ATTN_BLOCK]
            dvc_ref[:, sl] = dv2[:ATTN_BLOCK]

    def col(off):
        return lambda r, i: (r * nb + nb - 1 - i, 3 * g + off)

    def colp(off):
        return lambda r, i: (r * nb + jnp.maximum(nb - 2 - i, 0), 3 * g + off)

    blk = (ATTN_BLOCK, w)
    tok = pl.BlockSpec(blk, lambda r, i: (nb - 1 - i, r))
    tok1 = pl.BlockSpec((ATTN_BLOCK, LANES), lambda r, i: (nb - 1 - i, r))
    dq, dk, dv, db = pl.pallas_call(
        body, name=f"attn_bwd_g{g}", grid=(dil, nb),
        in_specs=[pl.BlockSpec(blk, col(0)), pl.BlockSpec(blk, col(1)), pl.BlockSpec(blk, colp(1)),
                  pl.BlockSpec(blk, col(2)), pl.BlockSpec(blk, colp(2)),
                  pl.BlockSpec((None, hpg, ATTN_BLOCK, 2 * ATTN_BLOCK), lambda r, i: (g, 0, 0, 0)),
                  tok, tok1, tok1],
        out_specs=[tok, tok, tok,
                   pl.BlockSpec((hpg, ATTN_BLOCK, 2 * ATTN_BLOCK), lambda r, i: (0, 0, 0))],
        out_shape=[jax.ShapeDtypeStruct((rows, dil * w), BF16)] * 3
        + [jax.ShapeDtypeStruct((hpg, ATTN_BLOCK, 2 * ATTN_BLOCK), F32)],
        scratch_shapes=[pltpu.VMEM(blk, F32), pltpu.VMEM(blk, F32)],
        compiler_params=_cparams(("arbitrary", "arbitrary")),
    )(pg, pg, pg, pg, pg, bias, dov, lsev, dlv)
    return dq.reshape(s, w), dk.reshape(s, w), dv.reshape(s, w), db


def _ln_fwd(xin, h, gamma, beta, name, affine_in=None, target=None):
    s, d = xin.shape
    tm = _pick(s, (128,))
    has_aff = affine_in is not None
    has_tgt = target is not None

    def body(*refs):
        it = iter(refs)
        x_ref, h_ref, g_ref, b_ref = next(it), next(it), next(it), next(it)
        if has_aff:
            gi_ref, bi_ref = next(it), next(it)
        if has_tgt:
            t_ref = next(it)
        xh_ref, rs_ref = next(it), next(it)
        x = x_ref[...]
        if has_aff:
            x = x * gi_ref[...] + bi_ref[...]
        u = DEEPNORM_ALPHA * x + h_ref[...]
        mu = jnp.mean(u, axis=1, keepdims=True)
        uc = u - mu
        var = jnp.mean(uc * uc, axis=1, keepdims=True)
        rstd = lax.rsqrt(var + LN_EPS)
        xhat = uc * rstd
        xh_ref[...] = xhat
        rs_ref[...] = rstd
        y = xhat * g_ref[...] + b_ref[...]
        if has_tgt:
            dy_ref, l_ref = next(it), next(it)
            e = y - t_ref[...]
            dy_ref[...] = e * (1.0 / d)
            l_ref[...] = jnp.sum(e * e, axis=1, keepdims=True)
        else:
            y_ref = next(it)
            y_ref[...] = y.astype(BF16)

    row = pl.BlockSpec((tm, d), lambda i: (i, 0))
    vec = pl.BlockSpec((1, d), lambda i: (0, 0))
    one = pl.BlockSpec((tm, 1), lambda i: (i, 0))
    in_specs = [row, row, vec, vec] + ([vec, vec] if has_aff else []) + ([row] if has_tgt else [])
    args = [xin, h, gamma, beta] + (list(affine_in) if has_aff else []) + ([target] if has_tgt else [])
    out_specs = [row, one] + ([row, one] if has_tgt else [row])
    out_shape = [jax.ShapeDtypeStruct((s, d), F32), jax.ShapeDtypeStruct((s, 1), F32)]
    out_shape += ([jax.ShapeDtypeStruct((s, d), F32), jax.ShapeDtypeStruct((s, 1), F32)] if has_tgt
                  else [jax.ShapeDtypeStruct((s, d), BF16)])
    return pl.pallas_call(
        body, name=name, grid=(s // tm,), in_specs=in_specs, out_specs=out_specs, out_shape=out_shape,
        compiler_params=_cparams(("parallel",)),
    )(*args)


def _ln_bwd(dy, xhat, rstd, gamma, name):
    s, d = dy.shape
    tm = _pick(s, (128,))

    def body(dy_ref, xh_ref, rs_ref, g_ref, du_ref, dub_ref, dg_ref, db_ref):
        @pl.when(pl.program_id(0) == 0)
        def _():
            dg_ref[...] = jnp.zeros_like(dg_ref)
            db_ref[...] = jnp.zeros_like(db_ref)

        dyv = dy_ref[...]
        xh = xh_ref[...]
        dg_ref[...] += jnp.sum(dyv * xh, axis=0, keepdims=True)
        db_ref[...] += jnp.sum(dyv, axis=0, keepdims=True)
        dxh = dyv * g_ref[...]
        m1 = jnp.mean(dxh, axis=1, keepdims=True)
        m2 = jnp.mean(dxh * xh, axis=1, keepdims=True)
        du = rs_ref[...] * (dxh - m1 - xh * m2)
        du_ref[...] = du
        dub_ref[...] = du.astype(BF16)

    row = pl.BlockSpec((tm, d), lambda i: (i, 0))
    vec = pl.BlockSpec((1, d), lambda i: (0, 0))
    one = pl.BlockSpec((tm, 1), lambda i: (i, 0))
    return pl.pallas_call(
        body, name=name, grid=(s // tm,), in_specs=[row, row, one, vec],
        out_specs=[row, row, vec, vec],
        out_shape=[jax.ShapeDtypeStruct((s, d), F32), jax.ShapeDtypeStruct((s, d), BF16),
                   jax.ShapeDtypeStruct((1, d), F32), jax.ShapeDtypeStruct((1, d), F32)],
        compiler_params=_cparams(("arbitrary",)),
    )(dy, xhat, rstd, gamma)


_HALO = 16
_STRIP = 16


def _strips(tm, fn, init, reverse=False):
    n = tm // _STRIP

    def step(i, carry):
        s_ = n - 1 - i if reverse else i
        return fn(pl.ds(pl.multiple_of(s_ * _STRIP, _STRIP), _STRIP), carry)

    return lax.fori_loop(0, n, step, init)


def _fold8(t):
    return t[0:8] + t[8:16]


def _conv_taps(ext, tm, w_ref):
    acc = None
    for k in range(CONV_WIDTH):
        lo = _HALO - (CONV_WIDTH - 1) + k
        term = w_ref[k:k + 1, :] * ext[lo:lo + tm, :]
        acc = term if acc is None else acc + term
    return acc


def _conv_strip(prev, cur, w_ref):
    ext = jnp.concatenate([prev, cur], axis=0)
    acc, taps = None, []
    for k in range(CONV_WIDTH):
        lo = _STRIP - (CONV_WIDTH - 1) + k
        taps.append(ext[lo:lo + _STRIP, :])
        term = w_ref[k:k + 1, :] * taps[k]
        acc = term if acc is None else acc + term
    return acc, taps


def _conv_fwd(pzx, conv_w, conv_b, d_inner):
    s, _ = pzx.shape
    cd = conv_w.shape[1]
    tm = _pick(s, (512, 256, 128))
    tc = _pick(cd, (1024, 512, 256, 128))
    off = d_inner // tc
    hb = tm // _HALO

    def body(x_ref, p_ref, w_ref, b_ref, o_ref):
        prev = jnp.where(pl.program_id(0) > 0, p_ref[...].astype(F32), 0.0)
        ext = jnp.concatenate([prev, x_ref[...].astype(F32)], axis=0)
        pre = _conv_taps(ext, tm, w_ref) + b_ref[...]
        o_ref[...] = (pre * _sigmoid(pre)).astype(BF16)

    return pl.pallas_call(
        body, name="conv_fwd", grid=(s // tm, cd // tc),
        in_specs=[pl.BlockSpec((tm, tc), lambda i, j: (i, off + j)),
                  pl.BlockSpec((_HALO, tc), lambda i, j: (jnp.maximum(i * hb - 1, 0), off + j)),
                  pl.BlockSpec((CONV_WIDTH, tc), lambda i, j: (0, j)),
                  pl.BlockSpec((1, tc), lambda i, j: (0, j))],
        out_specs=pl.BlockSpec((tm, tc), lambda i, j: (i, j)),
        out_shape=jax.ShapeDtypeStruct((s, cd), BF16),
        compiler_params=_cparams(("parallel", "parallel")),
    )(pzx, pzx, conv_w, conv_b)


def _conv_bwd_a(pzx, dxbc, conv_w, conv_b, d_inner):
    s, _ = pzx.shape
    cd = conv_w.shape[1]
    tm = _pick(s, (512, 256, 128))
    tc = _pick(cd, (1024, 512, 256, 128))
    off = d_inner // tc
    hb = tm // _HALO

    def body(x_ref, p_ref, d_ref, w_ref, b_ref, o_ref, dw_ref, db_ref, acc_ref):
        @pl.when(pl.program_id(1) == 0)
        def _():
            dw_ref[...] = jnp.zeros_like(dw_ref)
            db_ref[...] = jnp.zeros_like(db_ref)

        acc_ref[...] = jnp.zeros_like(acc_ref)

        def strip(rows, prev):
            cur = x_ref[rows, :].astype(F32)
            pre, taps = _conv_strip(prev, cur, w_ref)
            pre = pre + b_ref[...]
            sg = _sigmoid(pre)
            dpre = d_ref[rows, :].astype(F32) * (sg * (1.0 + pre * (1.0 - sg)))
            o_ref[rows, :] = dpre
            for k in range(CONV_WIDTH):
                acc_ref[k] += _fold8(dpre * taps[k])
            acc_ref[CONV_WIDTH] += _fold8(dpre)
            return cur

        _strips(tm, strip, jnp.where(pl.program_id(1) > 0, p_ref[...].astype(F32), 0.0))
        for k in range(CONV_WIDTH):
            dw_ref[k:k + 1, :] += jnp.sum(acc_ref[k], axis=0, keepdims=True)
        db_ref[...] += jnp.sum(acc_ref[CONV_WIDTH], axis=0, keepdims=True)

    return pl.pallas_call(
        body, name="conv_bwd_a", grid=(cd // tc, s // tm),
        in_specs=[pl.BlockSpec((tm, tc), lambda j, i: (i, off + j)),
                  pl.BlockSpec((_HALO, tc), lambda j, i: (jnp.maximum(i * hb - 1, 0), off + j)),
                  pl.BlockSpec((tm, tc), lambda j, i: (i, j)),
                  pl.BlockSpec((CONV_WIDTH, tc), lambda j, i: (0, j)),
                  pl.BlockSpec((1, tc), lambda j, i: (0, j))],
        out_specs=[pl.BlockSpec((tm, tc), lambda j, i: (i, j)),
                   pl.BlockSpec((CONV_WIDTH, tc), lambda j, i: (0, j)),
                   pl.BlockSpec((1, tc), lambda j, i: (0, j))],
        out_shape=[jax.ShapeDtypeStruct((s, cd), F32), jax.ShapeDtypeStruct((CONV_WIDTH, cd), F32),
                   jax.ShapeDtypeStruct((1, cd), F32)],
        scratch_shapes=[pltpu.VMEM((CONV_WIDTH + 1, 8, tc), F32)],
        compiler_params=_cparams(("parallel", "arbitrary")),
    )(pzx, pzx, dxbc, conv_w, conv_b)


def _conv_bwd_b(dpre, conv_w, into, col_off):
    s, cd = dpre.shape
    tm = _pick(s, (512, 256, 128))
    tc = _pick(cd, (1024, 512, 256, 128))
    hb = tm // 8
    nrb = s // tm
    assert col_off % tc == 0
    co = col_off // tc

    def body(x_ref, nx_ref, w_ref, into_ref, o_ref):
        nxt = jnp.where(pl.program_id(0) < nrb - 1, nx_ref[...], 0.0)
        ext = jnp.concatenate([x_ref[...], nxt], axis=0)
        acc = None
        for k in range(CONV_WIDTH):
            lo = CONV_WIDTH - 1 - k
            term = w_ref[k:k + 1, :] * ext[lo:lo + tm, :]
            acc = term if acc is None else acc + term
        o_ref[...] = acc.astype(BF16)

    return pl.pallas_call(
        body, name="conv_bwd_b", grid=(nrb, cd // tc),
        in_specs=[pl.BlockSpec((tm, tc), lambda i, j: (i, j)),
                  pl.BlockSpec((8, tc), lambda i, j: (jnp.minimum((i + 1) * hb, s // 8 - 1), j)),
                  pl.BlockSpec((CONV_WIDTH, tc), lambda i, j: (0, j)), _ANY],
        out_specs=pl.BlockSpec((tm, tc), lambda i, j: (i, j + co)),
        out_shape=jax.ShapeDtypeStruct(into.shape, BF16),
        input_output_aliases={3: 0},
        compiler_params=_cparams(("parallel", "parallel")),
    )(dpre, dpre, conv_w, into)


def _expand_matrix():
    e = np.zeros((LANES, SSM_GROUP_WIDTH), np.float32)
    for h in range(HEADS_PER_SSM_GROUP):
        e[h, h * SSM_HEAD_DIM:(h + 1) * SSM_HEAD_DIM] = 1.0
    return jnp.asarray(e, BF16)


def _expand(t, e):
    hi, lo = _split2(t)
    return _dot(hi, e) + _dot(lo, e)


def _segsum(v, e):
    hi, lo = _split2(v)
    return _dot_nt(hi, e) + _dot_nt(lo, e)


def _tri_dot(tri, x):
    hi, mid, lo = _split3(x)
    return _dot(tri, hi) + _dot(tri, mid) + _dot(tri, lo)


def _ssd_common(dtp_ref, a_ref, dtb_ref, x_ref, e):
    li = lax.broadcasted_iota(jnp.int32, (CHUNK, CHUNK), 0)
    si = lax.broadcasted_iota(jnp.int32, (CHUNK, CHUNK), 1)
    causal = li >= si
    tril = causal.astype(BF16)
    raw = dtp_ref[...] + dtb_ref[...]
    dt = jnp.maximum(raw, 0.0) + jnp.log(1.0 + jnp.exp(-jnp.abs(raw)))
    head_lane = lax.broadcasted_iota(jnp.int32, (1, LANES), 1) < HEADS_PER_SSM_GROUP
    a = jnp.where(head_lane, -jnp.exp(a_ref[...]), 0.0)
    a_cum = _tri_dot(tril, dt * a)
    a_cum_t = a_cum.T
    e_a = jnp.exp(a_cum)
    to_end = jnp.exp(a_cum[CHUNK - 1:CHUNK, :] - a_cum)
    x = x_ref[...].astype(F32)
    dt_e = _expand(dt, e)
    return dict(causal=causal, raw=raw, dt=dt, a=a, a_cum=a_cum, a_cum_t=a_cum_t, e_a=e_a,
                to_end=to_end, x=x, dt_e=dt_e, xdt=x * dt_e, e_a_e=_expand(e_a, e),
                to_end_e=_expand(to_end, e))


def _decay(q, h):
    seg = q["a_cum"][:, h:h + 1] - q["a_cum_t"][h:h + 1, :]
    return jnp.exp(jnp.where(q["causal"], seg, -jnp.inf))


def _ssd_specs(ng, d_inner, rev, nc):
    cidx = (lambda i: nc - 1 - i) if rev else (lambda i: i)
    boff = d_inner // D_STATE
    return dict(
        xs=pl.BlockSpec((CHUNK, SSM_GROUP_WIDTH), lambda g, i: (cidx(i), g)),
        b=pl.BlockSpec((CHUNK, D_STATE), lambda g, i: (cidx(i), boff + g)),
        c=pl.BlockSpec((CHUNK, D_STATE), lambda g, i: (cidx(i), boff + ng + g)),
        dtp=pl.BlockSpec((None, CHUNK, LANES), lambda g, i: (g, cidx(i), 0)),
        vec=pl.BlockSpec((None, 1, LANES), lambda g, i: (g, 0, 0)),
        wide=pl.BlockSpec((None, 1, SSM_GROUP_WIDTH), lambda g, i: (g, 0, 0)),
        e=pl.BlockSpec((LANES, SSM_GROUP_WIDTH), lambda g, i: (0, 0)),
        st=pl.BlockSpec((None, None, D_STATE, SSM_GROUP_WIDTH), lambda g, i: (g, cidx(i), 0, 0)),
        tok=pl.BlockSpec((CHUNK, SSM_GROUP_WIDTH), lambda g, i: (cidx(i), g)),
        bc_out=pl.BlockSpec((CHUNK, D_STATE), lambda g, i: (cidx(i), g)),
    )


def _ssd_fwd(xbc, dtp, a_pad, dtb_pad, dsk_e, e, d_inner):
    s = xbc.shape[0]
    ng = d_inner // SSM_GROUP_WIDTH
    nc = s // CHUNK

    def body(x_ref, b_ref, c_ref, dtp_ref, a_ref, dtb_ref, dsk_ref, e_ref, y_ref, st_ref, state):
        lane = lax.broadcasted_iota(jnp.int32, (CHUNK, LANES), 1)
        @pl.when(pl.program_id(1) == 0)
        def _():
            state[...] = jnp.zeros_like(state)

        ev = e_ref[...]
        q = _ssd_common(dtp_ref, a_ref, dtb_ref, x_ref, ev)
        bm, cm = b_ref[...], c_ref[...]
        cb = _dot_nt(cm, bm)
        s0 = state[...]
        st_ref[...] = s0
        y = _dot(cm, s0.astype(BF16)) * q["e_a_e"] + dsk_ref[...] * q["x"]
        xdt = q["xdt"]
        left = lane[:, :] < SSM_HEAD_DIM
        for j in range(HEADS_PER_SSM_GROUP // 2):
            sl = slice(j * LANES, (j + 1) * LANES)
            x2 = xdt[:, sl]
            m0 = (cb * _decay(q, 2 * j)).astype(BF16)
            m1 = (cb * _decay(q, 2 * j + 1)).astype(BF16)
            mcat = jnp.concatenate([m0, m1], axis=1)
            xbd = jnp.concatenate([jnp.where(left, x2, 0.0), jnp.where(left, 0.0, x2)], axis=0).astype(BF16)
            y_ref[:, sl] = (y[:, sl] + _dot(mcat, xbd)).astype(BF16)
        state[...] = s0 * q["e_a_e"][CHUNK - 1:CHUNK, :] + _dot_tn(bm, (q["to_end_e"] * xdt).astype(BF16))

    sp = _ssd_specs(ng, d_inner, False, nc)
    return pl.pallas_call(
        body, name="ssd_fwd", grid=(ng, nc),
        in_specs=[sp["xs"], sp["b"], sp["c"], sp["dtp"], sp["vec"], sp["vec"], sp["wide"], sp["e"]],
        out_specs=[sp["tok"], sp["st"]],
        out_shape=[jax.ShapeDtypeStruct((s, d_inner), BF16),
                   jax.ShapeDtypeStruct((ng, nc, D_STATE, SSM_GROUP_WIDTH), F32)],
        scratch_shapes=[pltpu.VMEM((D_STATE, SSM_GROUP_WIDTH), F32)],
        compiler_params=_cparams(("parallel", "arbitrary")),
    )(xbc, xbc, xbc, dtp, a_pad, dtb_pad, dsk_e, e)


def _ssd_bwd(xbc, dtp, a_pad, dtb_pad, dsk_e, e, states, dy, d_inner):
    s = xbc.shape[0]
    ng = d_inner // SSM_GROUP_WIDTH
    nc = s // CHUNK

    def body(x_ref, b_ref, c_ref, dtp_ref, a_ref, dtb_ref, dsk_ref, e_ref, st_ref, dy_ref,
             dx_ref, db_ref, dc_ref, ddt_ref, da_ref, ddtb_ref, dd_ref, dstate):
        lane = lax.broadcasted_iota(jnp.int32, (CHUNK, LANES), 1)
        sub = lax.broadcasted_iota(jnp.int32, (CHUNK, LANES), 0)
        @pl.when(pl.program_id(1) == 0)
        def _():
            dstate[...] = jnp.zeros_like(dstate)
            da_ref[...] = jnp.zeros_like(da_ref)
            ddtb_ref[...] = jnp.zeros_like(ddtb_ref)
            dd_ref[...] = jnp.zeros_like(dd_ref)

        ev = e_ref[...]
        q = _ssd_common(dtp_ref, a_ref, dtb_ref, x_ref, ev)
        bm, cm = b_ref[...], c_ref[...]
        cb = _dot_nt(cm, bm)
        x, xdt, e_a_e, to_end_e = q["x"], q["xdt"], q["e_a_e"], q["to_end_e"]
        s0 = st_ref[...]
        s0b = s0.astype(BF16)
        ds1 = dstate[...]
        ds1b = ds1.astype(BF16)
        dy = dy_ref[...].astype(F32)
        e_last_e = e_a_e[CHUNK - 1:CHUNK, :]

        dye = dy * e_a_e
        dyeb = dye.astype(BF16)
        cs0 = _dot(cm, s0b)
        dc = _dot_nt(dyeb, s0b)
        dstate[...] = e_last_e * ds1 + _dot_tn(cm, dyeb)
        da_col = _segsum(dye * cs0, ev)

        gmat = _dot(bm, ds1b)
        dxdt = to_end_e * gmat
        dte = _segsum(xdt * gmat, ev) * q["to_end"]
        db = _dot_nt((to_end_e * xdt).astype(BF16), ds1b)
        da_col = da_col - dte
        last_row = (jnp.sum(dte, axis=0, keepdims=True)
                    + q["e_a"][CHUNK - 1:CHUNK, :] * jnp.sum(_segsum(s0 * ds1, ev), axis=0, keepdims=True))

        left = lane < SSM_HEAD_DIM
        dcb = jnp.zeros((CHUNK, CHUNK), F32)
        row_acc = jnp.zeros((CHUNK, LANES), F32)
        for j in range(HEADS_PER_SSM_GROUP // 2):
            sl = slice(j * LANES, (j + 1) * LANES)
            x2 = xdt[:, sl].astype(BF16)
            dy2 = dy[:, sl]
            dyl = jnp.where(left, dy2, 0.0).astype(BF16)
            dyr = jnp.where(left, 0.0, dy2).astype(BF16)
            ms = []
            for hh, dyh in ((0, dyl), (1, dyr)):
                h = 2 * j + hh
                dec = _decay(q, h)
                m = cb * dec
                dm = _dot_nt(dyh, x2)
                dcb = dcb + dm * dec
                dseg = dm * m
                da_col = da_col + jnp.where(lane == h, jnp.sum(dseg, axis=1, keepdims=True), 0.0)
                row_acc = row_acc + jnp.where(sub == h, jnp.sum(dseg, axis=0, keepdims=True), 0.0)
                ms.append(m.astype(BF16))
            mst = jnp.concatenate(ms, axis=0)
            dyst = jnp.concatenate([dyl, dyr], axis=0)
            d2 = dxdt[:, sl] + _dot_tn(mst, dyst)
            dx_ref[:, sl] = (d2 * q["dt_e"][:, sl] + dsk_ref[:, sl] * dy2).astype(BF16)
            dxdt_x = d2 * x[:, sl]
            if j == 0:
                parts = [dxdt_x]
            else:
                parts.append(dxdt_x)
        dcbb = dcb.astype(BF16)
        dc_ref[...] = (dc + _dot(dcbb, bm)).astype(BF16)
        db_ref[...] = (db + _dot_tn(dcbb, cm)).astype(BF16)

        d_a = da_col - row_acc.T + jnp.where(sub == CHUNK - 1, last_row, 0.0)
        triu = (lax.broadcasted_iota(jnp.int32, (CHUNK, CHUNK), 1)
                >= lax.broadcasted_iota(jnp.int32, (CHUNK, CHUNK), 0)).astype(BF16)
        d_dta = _tri_dot(triu, d_a)
        ddt = d_dta * q["a"] + _segsum(jnp.concatenate(parts, axis=1), ev)
        ddt_raw = ddt * _sigmoid(q["raw"])
        ddt_ref[...] = ddt_raw
        da_ref[...] += jnp.sum(d_dta * q["dt"], axis=0, keepdims=True) * q["a"]
        ddtb_ref[...] += jnp.sum(ddt_raw, axis=0, keepdims=True)
        dd_ref[...] += jnp.sum(dy * x, axis=0, keepdims=True)

    sp = _ssd_specs(ng, d_inner, True, nc)
    return pl.pallas_call(
        body, name="ssd_bwd", grid=(ng, nc),
        in_specs=[sp["xs"], sp["b"], sp["c"], sp["dtp"], sp["vec"], sp["vec"], sp["wide"], sp["e"],
                  sp["st"], sp["tok"]],
        out_specs=[sp["tok"], sp["bc_out"], sp["bc_out"], sp["dtp"], sp["vec"], sp["vec"], sp["wide"]],
        out_shape=[jax.ShapeDtypeStruct((s, d_inner), BF16),
                   jax.ShapeDtypeStruct((s, ng * D_STATE), BF16),
                   jax.ShapeDtypeStruct((s, ng * D_STATE), BF16),
                   jax.ShapeDtypeStruct((ng, s, LANES), F32),
                   jax.ShapeDtypeStruct((ng, 1, LANES), F32),
                   jax.ShapeDtypeStruct((ng, 1, LANES), F32),
                   jax.ShapeDtypeStruct((ng, 1, SSM_GROUP_WIDTH), F32)],
        scratch_shapes=[pltpu.VMEM((D_STATE, SSM_GROUP_WIDTH), F32)],
        compiler_params=_cparams(("parallel", "arbitrary")),
    )(xbc, xbc, xbc, dtp, a_pad, dtb_pad, dsk_e, e, states, dy)


def _gate_norm_fwd(y, pzx, norm_w):
    s, di = y.shape
    ng = di // SSM_GROUP_WIDTH
    tm = _pick(s, (512, 256, 128))

    def body(y_ref, z_ref, w_ref, o_ref):
        z = z_ref[...].astype(F32)
        y2 = y_ref[...].astype(F32) * (z * _sigmoid(z))
        r = lax.rsqrt(jnp.mean(y2 * y2, axis=1, keepdims=True) + RMS_EPS)
        o_ref[...] = (y2 * r * w_ref[...]).astype(BF16)

    blk = pl.BlockSpec((tm, SSM_GROUP_WIDTH), lambda i, g: (i, g))
    return pl.pallas_call(
        body, name="gate_norm_fwd", grid=(s // tm, ng),
        in_specs=[blk, blk, pl.BlockSpec((1, SSM_GROUP_WIDTH), lambda i, g: (0, g))],
        out_specs=blk, out_shape=jax.ShapeDtypeStruct((s, di), BF16),
        compiler_params=_cparams(("parallel", "parallel")),
    )(y, pzx, norm_w)


def _gate_norm_bwd(dy3, y, pzx, norm_w):
    s, di = y.shape
    ng = di // SSM_GROUP_WIDTH
    tm = _pick(s, (512, 256, 128))

    def body(d_ref, y_ref, z_ref, w_ref, dy_ref, dz_ref, dw_ref):
        @pl.when(pl.program_id(1) == 0)
        def _():
            dw_ref[...] = jnp.zeros_like(dw_ref)

        z = z_ref[...].astype(F32)
        yv = y_ref[...].astype(F32)
        sg = _sigmoid(z)
        sz = z * sg
        y2 = yv * sz
        r = lax.rsqrt(jnp.mean(y2 * y2, axis=1, keepdims=True) + RMS_EPS)
        nrm = y2 * r
        d3 = d_ref[...].astype(F32)
        dw_ref[...] += jnp.sum(d3 * nrm, axis=0, keepdims=True)
        dn = d3 * w_ref[...]
        dy2 = r * (dn - nrm * jnp.mean(dn * nrm, axis=1, keepdims=True))
        dy_ref[...] = (dy2 * sz).astype(BF16)
        dz_ref[...] = (dy2 * yv * (sg * (1.0 + z * (1.0 - sg)))).astype(BF16)

    blk = pl.BlockSpec((tm, SSM_GROUP_WIDTH), lambda g, i: (i, g))
    vec = pl.BlockSpec((1, SSM_GROUP_WIDTH), lambda g, i: (0, g))
    return pl.pallas_call(
        body, name="gate_norm_bwd", grid=(ng, s // tm),
        in_specs=[blk, blk, blk, vec], out_specs=[blk, blk, vec],
        out_shape=[jax.ShapeDtypeStruct((s, di), BF16), jax.ShapeDtypeStruct(pzx.shape, BF16),
                   jax.ShapeDtypeStruct((1, di), F32)],
        compiler_params=_cparams(("parallel", "arbitrary")),
    )(dy3, y, pzx, norm_w)


_ANY = pl.BlockSpec(memory_space=pl.ANY)


def _place():
    x, y, c = lax.axis_index("x"), lax.axis_index("y"), lax.axis_index("c")
    chips = [(1 - x, y), (x, 1 - y), (1 - x, 1 - y)]
    return x, y, c, chips


def _cast_to_slot(x, kvec, name, after=None):
    r, cn = x.shape
    tr = _rows_per_block(r, cn)
    extra = [] if after is None else [after]

    def body(k_ref, x_ref, *rest):
        rest[-1][...] = x_ref[...].astype(BF16)

    grid_spec = pltpu.PrefetchScalarGridSpec(
        num_scalar_prefetch=1, grid=(r // tr,),
        in_specs=[pl.BlockSpec((tr, cn), lambda i, k: (i, 0))] + [_ANY] * len(extra),
        out_specs=pl.BlockSpec((None, tr, cn), lambda i, k: (k[0], i, 0)))
    return pl.pallas_call(
        body, name=name, grid_spec=grid_spec, out_shape=jax.ShapeDtypeStruct((N_CHIPS, r, cn), BF16),
        compiler_params=_cparams(("parallel",)),
    )(kvec, x, *extra)


def _swap_halves(gs, name, after=None):
    n = len(gs)
    extra = [] if after is None else [after]

    def body(*refs):
        ins, outs = refs[:n], refs[n + len(extra):2 * n + len(extra)]
        send_sems, recv_sems = refs[2 * n + len(extra):]
        x, y, c, _ = _place()
        cps = []
        for w in range(n):
            hr = gs[w].shape[1] // 2
            cp = pltpu.make_async_remote_copy(
                src_ref=ins[w].at[:, pl.ds((1 - c) * hr, hr)], dst_ref=outs[w],
                send_sem=send_sems.at[w], recv_sem=recv_sems.at[w],
                device_id=(x, y, 1 - c), device_id_type=MESH)
            cp.start()
            cps.append(cp)
        for cp in cps:
            cp.wait()

    return pl.pallas_call(
        body, name=name,
        in_specs=[_ANY] * (n + len(extra)), out_specs=[_ANY] * n,
        out_shape=[jax.ShapeDtypeStruct((g.shape[0], g.shape[1] // 2, g.shape[2]), g.dtype) for g in gs],
        scratch_shapes=[pltpu.SemaphoreType.DMA((n,)), pltpu.SemaphoreType.DMA((n,))],
    )(*gs, *extra)


def _join_halves(fs, name, after=None):
    n = len(fs)
    extra = [] if after is None else [after]

    def body(*refs):
        outs = refs[n + len(extra):2 * n + len(extra)]
        send_sems, recv_sems = refs[2 * n + len(extra):]
        x, y, c, _ = _place()

        def copy(w, hc):
            hr = fs[w].shape[0] // 2
            rows = outs[w].at[pl.ds(hc * hr, hr)]
            return pltpu.make_async_remote_copy(
                src_ref=rows, dst_ref=rows, send_sem=send_sems.at[w], recv_sem=recv_sems.at[w],
                device_id=(x, y, 1 - c), device_id_type=MESH)

        cps = [copy(w, c) for w in range(n)]
        for cp in cps:
            cp.start()
        for w in range(n):
            copy(w, 1 - c).wait_recv()
        for cp in cps:
            cp.wait_send()

    return pl.pallas_call(
        body, name=name,
        in_specs=[_ANY] * (n + len(extra)), out_specs=[_ANY] * n,
        out_shape=[jax.ShapeDtypeStruct(f.shape, f.dtype) for f in fs],
        input_output_aliases={w: w for w in range(n)},
        scratch_shapes=[pltpu.SemaphoreType.DMA((n,)), pltpu.SemaphoreType.DMA((n,))],
    )(*fs, *extra)


_HBM_SPEC = pl.BlockSpec(memory_space=pltpu.HBM)
_SEM_SPEC = pl.BlockSpec(memory_space=pltpu.SEMAPHORE)
_VMEM_SPEC = pl.BlockSpec(memory_space=pltpu.VMEM)
_EFFECT = pltpu.SideEffectType.DATAFLOW_SIDE_EFFECTING
_TOKEN = jax.ShapeDtypeStruct((8, LANES), F32)


def _hbm(a):
    return pltpu.with_memory_space_constraint(a, pltpu.HBM)


_NEIGHBOURS, _DIAGONAL, _ALL_CHIPS = (0, 1), (2,), (0, 1, 2)


def _gather_copies(bufs, refs, send_sems, recv_sems, forward, arrivals=True, which=_ALL_CHIPS):
    x, y, c, chips = _place()
    k = 2 * x + y
    out, arrive = [], []
    for w, ref in enumerate(refs):
        hr = bufs[w].shape[1] // 2
        for j, (cx, cy) in enumerate(chips):
            if j not in which:
                continue
            kj = 2 * cx + cy
            slot_out, slot_in, half_in = (kj, kj, 1 - c) if forward else (k, kj, c)
            to = (x, y, 1 - c) if forward else (cx, cy, c)
            src = ref.at[slot_out, pl.ds(c * hr, hr)]
            land = ref.at[slot_in, pl.ds(half_in * hr, hr)]
            out.append(pltpu.make_async_remote_copy(
                src_ref=src, dst_ref=src, send_sem=send_sems.at[3 * w + j], recv_sem=recv_sems.at[3 * w + j],
                device_id=to, device_id_type=MESH))
            if arrivals:
                arrive.append(pltpu.make_async_remote_copy(
                    src_ref=land, dst_ref=land, send_sem=send_sems.at[3 * w + j], recv_sem=recv_sems.at[3 * w + j],
                    device_id=to, device_id_type=MESH))
    return out, arrive


def _gather_start(bufs, forward, name, after=None, which=_ALL_CHIPS):
    n = len(bufs)
    extra = [] if after is None else [after]

    def body(*refs):
        ins = refs[:n]
        send_sems, recv_sems = refs[n + len(extra)], refs[n + len(extra) + 1]
        token = refs[-1]
        out, _ = _gather_copies(bufs, ins, send_sems, recv_sems, forward, arrivals=False, which=which)
        for cp in out:
            cp.start()
        token[...] = jnp.zeros_like(token)

    res = pl.pallas_call(
        body, name=name,
        out_shape=(pltpu.SemaphoreType.DMA((3 * n,)), pltpu.SemaphoreType.DMA((3 * n,)))
        + tuple(pltpu.HBM(b.shape, b.dtype) for b in bufs) + (_TOKEN,),
        in_specs=(_HBM_SPEC,) * n + (_ANY,) * len(extra),
        out_specs=(_SEM_SPEC, _SEM_SPEC) + (_HBM_SPEC,) * n + (_VMEM_SPEC,),
        input_output_aliases={w: 2 + w for w in range(n)},
        compiler_params=pltpu.CompilerParams(has_side_effects=_EFFECT),
    )(*[_hbm(b) for b in bufs], *extra)
    return res[0], res[1], list(res[2:2 + n]), res[-1]


def _gather_wait(bufs, send_sems, recv_sems, after, forward, name, which=_ALL_CHIPS):
    n = len(bufs)

    def body(*refs):
        ins = refs[:n]
        send_sems, recv_sems = refs[n], refs[n + 1]
        out, arrive = _gather_copies(bufs, ins, send_sems, recv_sems, forward, which=which)
        for cp in out:
            cp.wait_send()
        for cp in arrive:
            cp.wait_recv()

    res = pl.pallas_call(
        body, name=name,
        out_shape=tuple(pltpu.HBM(b.shape, b.dtype) for b in bufs),
        in_specs=(_HBM_SPEC,) * n + (_SEM_SPEC, _SEM_SPEC, _ANY), out_specs=(_HBM_SPEC,) * n,
        input_output_aliases={w: w for w in range(n)},
        compiler_params=pltpu.CompilerParams(has_side_effects=_EFFECT),
    )(*bufs, send_sems, recv_sems, after)
    return list(res)


def _scatter_copies(t_ref, land_ref, send_sems, recv_sems, arrivals=True):
    x, y, c, chips = _place()
    k = 2 * x + y
    out, arrive = [], []
    for j, (cx, cy) in enumerate(chips):
        kj = 2 * cx + cy
        out.append(pltpu.make_async_remote_copy(
            src_ref=t_ref.at[kj], dst_ref=land_ref.at[k], send_sem=send_sems.at[j], recv_sem=recv_sems.at[j],
            device_id=(cx, cy, c), device_id_type=MESH))
        if arrivals:
            arrive.append(pltpu.make_async_remote_copy(
                src_ref=t_ref.at[kj], dst_ref=land_ref.at[kj], send_sem=send_sems.at[j], recv_sem=recv_sems.at[j],
                device_id=(cx, cy, c), device_id_type=MESH))
    return out, arrive


def _scatter_start(t, name):
    def body(t_ref, land_ref, send_sems, recv_sems, t_thru, land_thru, token):
        out, _ = _scatter_copies(t_ref, land_ref, send_sems, recv_sems, arrivals=False)
        for cp in out:
            cp.start()
        token[...] = jnp.zeros_like(token)

    return pl.pallas_call(
        body, name=name,
        out_shape=(pltpu.SemaphoreType.DMA((3,)), pltpu.SemaphoreType.DMA((3,)),
                   pltpu.HBM(t.shape, t.dtype), pltpu.HBM(t.shape, t.dtype), _TOKEN),
        in_specs=(_HBM_SPEC, _HBM_SPEC), out_specs=(_SEM_SPEC, _SEM_SPEC, _HBM_SPEC, _HBM_SPEC, _VMEM_SPEC),
        input_output_aliases={0: 2, 1: 3},
        compiler_params=pltpu.CompilerParams(has_side_effects=_EFFECT),
    )(_hbm(t), _hbm(lax.empty(t.shape, t.dtype)))


def _scatter_wait(send_sems, recv_sems, t_thru, land_thru, after, name):
    def body(t_ref, land_ref, send_sems, recv_sems, after_ref, t_out, land_out):
        out, arrive = _scatter_copies(t_ref, land_ref, send_sems, recv_sems)
        for cp in out:
            cp.wait_send()
        for cp in arrive:
            cp.wait_recv()

    return pl.pallas_call(
        body, name=name,
        out_shape=(pltpu.HBM(t_thru.shape, t_thru.dtype), pltpu.HBM(land_thru.shape, land_thru.dtype)),
        in_specs=(_HBM_SPEC, _HBM_SPEC, _SEM_SPEC, _SEM_SPEC, _ANY), out_specs=(_HBM_SPEC, _HBM_SPEC),
        input_output_aliases={0: 0, 1: 1},
        compiler_params=pltpu.CompilerParams(has_side_effects=_EFFECT),
    )(t_thru, land_thru, send_sems, recv_sems, after)


def _all_gather_small(v, reduce, name):
    r, l = v.shape

    def body(v_ref, o_ref, *rest):
        if reduce:
            buf, send_sems, recv_sems = rest
        else:
            buf = o_ref
            send_sems, recv_sems = rest
        x, y, c, _ = _place()
        me = 4 * x + 2 * y + c
        buf[me] = v_ref[...]
        cps = []
        for d in range(1, N_DEV):
            peer = (x if d & 4 == 0 else 1 - x, y if d & 2 == 0 else 1 - y, c if d & 1 == 0 else 1 - c)
            cp = pltpu.make_async_remote_copy(
                src_ref=v_ref, dst_ref=buf.at[me], send_sem=send_sems.at[d - 1], recv_sem=recv_sems.at[d - 1],
                device_id=peer, device_id_type=MESH)
            cp.start()
            cps.append((cp, peer))
        for d, (cp, (px, py, pc)) in enumerate(cps, start=1):
            pltpu.make_async_remote_copy(
                src_ref=v_ref, dst_ref=buf.at[4 * px + 2 * py + pc], send_sem=send_sems.at[d - 1],
                recv_sem=recv_sems.at[d - 1], device_id=(px, py, pc), device_id_type=MESH).wait_recv()
        for cp, _ in cps:
            cp.wait_send()
        if reduce:
            acc = buf[0]
            for i in range(1, N_DEV):
                acc = acc + buf[i]
            o_ref[...] = acc

    vm = pl.BlockSpec(memory_space=pltpu.VMEM)
    out_shape = jax.ShapeDtypeStruct((r, l) if reduce else (N_DEV, r, l), F32)
    scratch = ([pltpu.VMEM((N_DEV, r, l), F32)] if reduce else []) + [
        pltpu.SemaphoreType.DMA((N_DEV - 1,)), pltpu.SemaphoreType.DMA((N_DEV - 1,))]
    return pl.pallas_call(
        body, name=name, in_specs=[vm], out_specs=vm, out_shape=out_shape, scratch_shapes=scratch,
    )(v)


_BLOCK_BYTES = 3 * 512 * 1024


def _rows_per_block(r, cn, itemsize=4):
    best = 8
    for t in range(8, r + 1, 8):
        if r % t == 0 and t * cn * itemsize <= _BLOCK_BYTES:
            best = t
    return best


def _add_sibling_half(g4, recv, cvec, name):
    ns, r, cn = g4.shape
    hr = r // 2
    tr = _rows_per_block(hr, cn)
    nrb = hr // tr

    def body(c_ref, a_ref, b_ref, o_ref):
        o_ref[...] = (a_ref[...].astype(F32) + b_ref[...].astype(F32)).astype(o_ref.dtype)

    grid_spec = pltpu.PrefetchScalarGridSpec(
        num_scalar_prefetch=1, grid=(ns, nrb),
        in_specs=[pl.BlockSpec((None, tr, cn), lambda j, i, c: (j, c[0] * nrb + i, 0)),
                  pl.BlockSpec((None, tr, cn), lambda j, i, c: (j, i, 0))],
        out_specs=pl.BlockSpec((None, tr, cn), lambda j, i, c: (j, i, 0)))
    return pl.pallas_call(
        body, name=name, grid_spec=grid_spec, out_shape=jax.ShapeDtypeStruct((ns, hr, cn), BF16),
        compiler_params=_cparams(("parallel", "parallel")),
    )(cvec, g4, recv)


def _sum_chips(r4, t4, kvec, cvec, name):
    ns, hr, cn = r4.shape
    tr = _rows_per_block(hr, cn)
    nrb = hr // tr

    def body(k_ref, c_ref, r_ref, t_ref, o_ref):
        acc = t_ref[...].astype(F32)
        for dlt in range(1, ns):
            acc = acc + r_ref[(k_ref[0] + dlt) % ns].astype(F32)
        o_ref[...] = acc

    grid_spec = pltpu.PrefetchScalarGridSpec(
        num_scalar_prefetch=2, grid=(nrb,),
        in_specs=[pl.BlockSpec((ns, tr, cn), lambda i, k, c: (0, i, 0)),
                  pl.BlockSpec((None, tr, cn), lambda i, k, c: (k[0], i, 0))],
        out_specs=pl.BlockSpec((tr, cn), lambda i, k, c: (c[0] * nrb + i, 0)))
    return pl.pallas_call(
        body, name=name, grid_spec=grid_spec, out_shape=jax.ShapeDtypeStruct((2 * hr, cn), F32),
        compiler_params=_cparams(("parallel",)),
    )(kvec, cvec, r4, t4)


def _adamw(w, g, m, v, name):
    r, cn = w.shape
    tr = _rows_per_block(r, cn)
    c1 = 1.0 - ADAM_B1 ** ADAM_STEP
    c2 = 1.0 - ADAM_B2 ** ADAM_STEP

    def body(w_ref, g_ref, m_ref, v_ref, go_ref, d_ref, mo_ref, vo_ref):
        gv = g_ref[...]
        mn = ADAM_B1 * m_ref[...] + (1.0 - ADAM_B1) * gv
        vn = ADAM_B2 * v_ref[...] + (1.0 - ADAM_B2) * (gv * gv)
        go_ref[...] = gv
        mo_ref[...] = mn
        vo_ref[...] = vn
        d_ref[...] = -ADAM_LR * ((mn / c1) / (jnp.sqrt(vn / c2) + ADAM_EPS) + ADAM_WD * w_ref[...])

    spec = pl.BlockSpec((tr, cn), lambda i: (i, 0))
    return pl.pallas_call(
        body, name=name, grid=(r // tr,), in_specs=[spec] * 4, out_specs=[spec] * 4,
        out_shape=[jax.ShapeDtypeStruct((r, cn), F32)] * 4,
        compiler_params=_cparams(("parallel",)),
    )(w, g, m, v)


def _pack(arrs):
    flat = jnp.concatenate([a.reshape(-1).astype(F32) for a in arrs])
    n = flat.shape[0]
    tot = -(-n // (8 * LANES)) * (8 * LANES)
    return jnp.pad(flat, (0, tot - n)).reshape(tot // LANES, LANES)


def _unpack(packed, shapes):
    flat = packed.reshape(-1)
    out, off = [], 0
    for shp in shapes:
        sz = int(np.prod(shp))
        out.append(flat[off:off + sz].reshape(shp))
        off += sz
    return out


class _LocalExchange:
    def __init__(self, wa4, woa4, ws4, wos4):
        self.l0, self.ssm = [wa4, woa4], [ws4, wos4]
        self.grads = {}

    def l0_begin(self):
        return self.l0[0], None

    def l0_neighbours(self, after):
        return self.l0[0]

    def l0_diagonal(self, after):
        return self.l0

    def ssm_gather_start(self):
        return None

    def ssm_gather_mid(self, after):
        return None

    def ssm_gather_end(self, after):
        return self.ssm

    def grad_ready(self, name, g4, after=None):
        self.grads[name] = g4
        return None

    def grad_sync(self, name, after):
        pass

    def small_grads(self, small_full):
        self.small = small_full
        return None


class _Exchange:
    def __init__(self, kvec, cvec, l0_bufs, ssm_bufs, after):
        self.kvec, self.cvec, self.l0_bufs, self.bufs, self.after = kvec, cvec, l0_bufs, ssm_bufs, after
        self.pending, self.summed, self.last_token = {}, {}, None

    def l0_begin(self):
        self.l0_send, self.l0_recv, self.l0_bufs, token = _gather_start(
            self.l0_bufs, False, "l0_gather_ici_start", self.after)
        return self.l0_bufs[0], token

    def _l0_step(self, which, tag, after):
        bufs = _gather_wait(self.l0_bufs, self.l0_send, self.l0_recv, after, False, "l0_gather_ici_wait_" + tag, which)
        send, recv, bufs, token = _gather_start(bufs, True, "l0_gather_fwd_start_" + tag, None, which)
        self.l0_bufs = _gather_wait(bufs, send, recv, token, True, "l0_gather_fwd_wait_" + tag, which)
        return self.l0_bufs

    def l0_neighbours(self, after):
        return self._l0_step(_NEIGHBOURS, "nb", after)[0]

    def l0_diagonal(self, after):
        return self._l0_step(_DIAGONAL, "diag", after)

    def ssm_gather_start(self):
        self.sems = _gather_start(self.bufs, False, "ssm_gather_ici_start", self.l0_bufs[1])
        self.bufs = self.sems[2]
        return self.sems[3]

    def ssm_gather_mid(self, after):
        bufs = _gather_wait(self.bufs, self.sems[0], self.sems[1], after, False, "ssm_gather_ici_wait")
        self.sems = _gather_start(bufs, True, "ssm_gather_fwd_start")
        self.bufs = self.sems[2]
        return self.sems[3]

    def ssm_gather_end(self, after):
        return _gather_wait(self.bufs, self.sems[0], self.sems[1], after, True, "ssm_gather_fwd_wait")

    def small_grads(self, small_full):
        packed = _all_gather_small(_pack(small_full), True, "reduce_small_grads")
        self.small = _unpack(packed, [t.shape for t in small_full])
        return packed

    def grad_ready(self, name, g4, after=None):
        recv = _swap_halves([g4], "grads_to_sibling_" + name, after)[0]
        t = _add_sibling_half(g4, recv, self.cvec, "add_sibling_" + name)
        send_sems, recv_sems, t_thru, land, token = _scatter_start(t, "scatter_start_" + name)
        self.pending[name] = (send_sems, recv_sems, t_thru, land)
        self.last_token = token
        return token

    def grad_sync(self, name, after):
        t, land = _scatter_wait(*self.pending.pop(name), after, "scatter_wait_" + name)
        self.summed[name] = _sum_chips(land, t, self.kvec, self.cvec, "sum_chips_" + name)


def _tie(vec, token):
    return vec if token is None else vec + token[0:1, 0:1].reshape((1,) * vec.ndim).astype(vec.dtype)


def _local_step(x2, tgt, ex, kvec, conv_w_f, conv_b_f, norm_w_f, rel_bias, dt_bias, a_log, d_skip,
                ln_g, ln_b):
    s, d = x2.shape
    wa4, tok = ex.l0_begin()
    d_attn = wa4.shape[2] * N_CHIPS // 10
    hpg = d_attn // HEAD_DIM
    d_inner = norm_w_f.shape[1]
    ng = d_inner // SSM_GROUP_WIDTH
    n_heads = dt_bias.shape[1]
    conv_dim = conv_w_f.shape[1]
    assert n_heads == ng * HEADS_PER_SSM_GROUP and conv_dim == d_inner + 2 * ng * D_STATE

    x3 = _cast_x3(x2, "cast_x", tok)
    for g, (_, dil) in enumerate(ATTN_PATTERNS):
        if dil > 1:
            x3 = _class_copy(x3, g, dil, f"class_order_x_g{g}")
    xb = x3[0]
    buckets = _bucket_tiles()
    bias = _bias_expand(rel_bias, buckets, hpg)
    pa = _in_proj_shard(x3, wa4, kvec, 0, d_attn, "mm_in_attn_own", after=tok)
    wa4 = ex.l0_neighbours(pa)
    pa = _in_proj_shard(x3, wa4, kvec, 1, d_attn, "mm_in_attn_nb1", into=pa)
    pa = _in_proj_shard(x3, wa4, kvec, 2, d_attn, "mm_in_attn_nb2", into=pa)
    wa4, woa4 = ex.l0_diagonal(pa)
    pa = _in_proj_shard(x3, wa4, kvec, 3, d_attn, "mm_in_attn_diag", into=pa, after=ex.ssm_gather_start())
    og, lg = [], []
    for g, (_, dil) in enumerate(ATTN_PATTERNS):
        o_, l_ = _attn_fwd(pa, bias, g, dil, hpg)
        og.append(o_)
        lg.append(l_)
    o, lse, yat = _attn_combine(og, lg, pa, hpg)
    h0 = _mm_nn_sharded(yat, woa4, F32, "mm_out_attn", after=ex.ssm_gather_mid(yat))
    g0, b0, g1, b1 = ln_g[0:1], ln_b[0:1], ln_g[1:2], ln_b[1:2]
    xhat0, rstd0, x1b = _ln_fwd(x2, h0, g0, b0, "ln0_fwd")

    wst4, wos4 = ex.ssm_gather_end(x1b)
    wst = wst4.reshape(N_CHIPS * wst4.shape[1], d)
    nzx = d_inner + conv_dim
    wos = wos4.reshape(d_inner, d)
    pzx = _mm_nt(x1b, wst, BF16, "mm_in_ssm", n=nzx)
    dt_raw = _mm_nt(x1b, wst, F32, "mm_in_dt", n=n_heads, b_row_off=nzx)

    def pad_heads(t):
        t = t.reshape(t.shape[0], ng, HEADS_PER_SSM_GROUP).transpose(1, 0, 2)
        return jnp.pad(t, ((0, 0), (0, 0), (0, LANES - HEADS_PER_SSM_GROUP)))

    def unpad_heads(t):
        return t[:, :, :HEADS_PER_SSM_GROUP].transpose(1, 0, 2).reshape(t.shape[1], n_heads)

    dtp = pad_heads(dt_raw)
    alog_p, dtb_p = pad_heads(a_log), pad_heads(dt_bias)
    dsk_e = jnp.repeat(d_skip.reshape(ng, 1, HEADS_PER_SSM_GROUP), SSM_HEAD_DIM, axis=2)
    e = _expand_matrix()
    xbc = _conv_fwd(pzx, conv_w_f, conv_b_f, d_inner)
    y_ssd, states = _ssd_fwd(xbc, dtp, alog_p, dtb_p, dsk_e, e, d_inner)
    y3 = _gate_norm_fwd(y_ssd, pzx, norm_w_f)
    h1 = _mm_nn(y3, wos, F32, "mm_out_ssm")
    xhat1, rstd1, dy2, row_sq = _ln_fwd(xhat0, h1, g1, b1, "ln1_fwd_loss", affine_in=(g0, b0), target=tgt)
    loss_local = 0.5 * jnp.sum(row_sq) / d

    du1, du1b, dg1, db1 = _ln_bwd(dy2, xhat1, rstd1, g1, "ln1_bwd")
    dy3 = _mm_nt(du1b, wos, BF16, "mm_d_y3")
    g_wos = _mm_tn(y3, du1b, BF16, "mm_g_w_out_ssm").reshape(N_CHIPS, d_inner // N_CHIPS, d)
    norm_w_t = _tie(norm_w_f, ex.grad_ready("w_out_ssm", g_wos))
    dy_ssd, dz, d_nw = _gate_norm_bwd(dy3, y_ssd, pzx, norm_w_t)
    dxs, dbm, dcm, ddtp, d_alog, d_dtb, d_dsk = _ssd_bwd(xbc, dtp, alog_p, dtb_p, dsk_e, e, states, dy_ssd, d_inner)
    dpre, d_cw, d_cb = _conv_bwd_a(pzx, jnp.concatenate([dxs, dbm, dcm], axis=1), conv_w_f, conv_b_f, d_inner)
    dpzx = _conv_bwd_b(dpre, conv_w_f, dz, d_inner)
    ddt_raw = unpad_heads(ddtp)
    t1 = _mm_nn(ddt_raw, wst, F32, "mm_d_x1_dt", b_row_off=nzx, add=du1, add_scale=DEEPNORM_ALPHA)
    dx1 = _mm_nn(dpzx, wst, F32, "mm_d_x1", add=t1)
    ex.grad_sync("w_out_ssm", dx1)
    g_wst = _mm_tn(dpzx, x1b, BF16, "mm_g_w_in_ssm", out_rows=wst.shape[0])
    g_wst = _mm_tn(ddt_raw, x1b, BF16, "mm_g_w_dt", out_rows=wst.shape[0], out_row_off=nzx, into=g_wst)
    g0_t = _tie(g0, ex.grad_ready("w_in_ssm", g_wst.reshape(wst4.shape)))

    du0, du0b, dg0, db0 = _ln_bwd(dx1, xhat0, rstd0, g0_t, "ln0_bwd")
    dyat = _mm_nt_sharded_k(du0b, woa4, BF16, "mm_d_yat")
    g_woa = _mm_tn(yat, du0b, BF16, "mm_g_w_out_attn", shard_cols=d // N_CHIPS)
    tok_woa = ex.grad_ready("w_out_attn", g_woa)
    do, delta, dgate = _attn_pre_bwd(dyat, o, pa, hpg)
    pieces, dbt = [], []
    for g, (_, dil) in enumerate(ATTN_PATTERNS):
        dq, dk, dv, db_ = _attn_bwd(pa, bias, do, lse, delta, g, dil, hpg)
        pieces += [dq, dk, dv]
        dbt.append(db_)
    dpa = jnp.concatenate(pieces + [dgate], axis=1)
    g_wa = _mm_tn(xb, dpa, BF16, "mm_g_w_in_attn", shard_cols=wa4.shape[2], after=tok_woa)
    ex.grad_sync("w_in_ssm", g_wa)
    ex.grad_sync("w_out_attn", g_wa)
    d_rel = _bias_reduce(jnp.stack(dbt), buckets, hpg)[:, :, 0].T
    d_dsk_h = d_dsk.reshape(n_heads, SSM_HEAD_DIM).sum(axis=1)
    small_full = [d_rel, d_cw, d_cb, unpad_heads(d_dtb), unpad_heads(d_alog), d_dsk_h[None], d_nw,
                  jnp.concatenate([dg0, dg1], axis=0), jnp.concatenate([db0, db1], axis=0)]
    tok_wa = ex.grad_ready("w_in_attn", g_wa, after=ex.small_grads(small_full))
    grad_x = _mm_nt_sharded_k(dpa, wa4, F32, "mm_d_x0", add=du0, add_scale=DEEPNORM_ALPHA, after=tok_wa)
    return loss_local, grad_x[None]


def kernel(x, w_in_attn, w_out_attn, rel_bias, w_in_ssm, conv_w, conv_b, dt_bias, a_log, d_skip, ssm_norm_w, w_out_ssm, ln_g, ln_b, loss_target, m_w_in_attn, m_w_out_attn, m_rel_bias, m_w_in_ssm, m_conv_w, m_conv_b, m_dt_bias, m_a_log, m_d_skip, m_ssm_norm_w, m_w_out_ssm, m_ln_g, m_ln_b, v_w_in_attn, v_w_out_attn, v_rel_bias, v_w_in_ssm, v_conv_w, v_conv_b, v_dt_bias, v_a_log, v_d_skip, v_ssm_norm_w, v_w_out_ssm, v_ln_g, v_ln_b):
    xi, yi, ci = lax.axis_index("x"), lax.axis_index("y"), lax.axis_index("c")
    chip = 2 * xi + yi
    cvec = jnp.reshape(ci, (1,)).astype(jnp.int32)
    kvec = jnp.reshape(chip, (1,)).astype(jnp.int32)

    cw_l, cb_l, nw_l = conv_w[0], conv_b[0], ssm_norm_w[0]
    vec_shapes = [cw_l.shape, cb_l.shape, nw_l.shape]
    vec_all = _all_gather_small(_pack([cw_l, cb_l, nw_l]), False, "gather_vectors")
    l0 = [_cast_to_slot(w_in_attn[0], kvec, "cast_w_in_attn"), _cast_to_slot(w_out_attn[0], kvec, "cast_w_out_attn")]
    ssm_bufs = [_cast_to_slot(w_in_ssm[0].T, kvec, "cast_w_in_ssm"), _cast_to_slot(w_out_ssm[0], kvec, "cast_w_out_ssm")]
    ex = _Exchange(kvec, cvec, l0, ssm_bufs, after=vec_all)
    parts = [_unpack(vec_all[2 * j], vec_shapes) for j in range(N_CHIPS)]
    conv_w_f = jnp.concatenate([p[0] for p in parts], axis=1)
    conv_b_f = jnp.concatenate([p[1] for p in parts], axis=0)[None]
    norm_w_f = jnp.concatenate([p[2] for p in parts], axis=0)[None]

    loss_local, grad_x = _local_step(
        x[0], loss_target[0], ex, kvec, conv_w_f, conv_b_f, norm_w_f, rel_bias, dt_bias, a_log,
        d_skip, ln_g, ln_b)
    loss = lax.psum(loss_local, ("x", "y", "c"))

    big_w = dict(w_in_attn=(w_in_attn, m_w_in_attn, v_w_in_attn), w_out_attn=(w_out_attn, m_w_out_attn, v_w_out_attn),
                 w_in_ssm=(w_in_ssm, m_w_in_ssm, v_w_in_ssm), w_out_ssm=(w_out_ssm, m_w_out_ssm, v_w_out_ssm))
    big = {}

    def finish(names, join_name, after):
        last = None
        for nm, gf in zip(names, _join_halves([ex.summed[nm] for nm in names], join_name, after)):
            flip = (lambda t: t.T) if nm == "w_in_ssm" else (lambda t: t)
            w_, m_, v_ = (flip(t[0]) for t in big_w[nm])
            res = _adamw(w_, gf, m_, v_, "adamw_" + nm)
            big[nm] = [flip(t)[None] for t in res]
            last = res[3]
        return last

    last = finish(["w_out_ssm", "w_in_ssm", "w_out_attn"], "grads_join_halves_a", ex.last_token)
    ex.grad_sync("w_in_attn", last)
    finish(["w_in_attn"], "grads_join_halves_b", None)

    s_rel, s_cw, s_cb, s_dtb, s_alog, s_dsk, s_nw, s_lng, s_lnb = ex.small
    cwc, nwc = conv_w.shape[2], ssm_norm_w.shape[1]
    s_cw = lax.dynamic_slice_in_dim(s_cw, chip * cwc, cwc, axis=1)[None]
    s_cb = lax.dynamic_slice_in_dim(s_cb, chip * cwc, cwc, axis=1)
    s_nw = lax.dynamic_slice_in_dim(s_nw, chip * nwc, nwc, axis=1)
    small_names = ["rel_bias", "conv_w", "conv_b", "dt_bias", "a_log", "d_skip", "ssm_norm_w", "ln_g", "ln_b"]
    small_g = [s_rel, s_cw, s_cb, s_dtb, s_alog, s_dsk, s_nw, s_lng, s_lnb]
    small_w = [rel_bias, conv_w, conv_b, dt_bias, a_log, d_skip, ssm_norm_w, ln_g, ln_b]
    small_m = [m_rel_bias, m_conv_w, m_conv_b, m_dt_bias, m_a_log, m_d_skip, m_ssm_norm_w, m_ln_g, m_ln_b]
    small_v = [v_rel_bias, v_conv_w, v_conv_b, v_dt_bias, v_a_log, v_d_skip, v_ssm_norm_w, v_ln_g, v_ln_b]
    shapes = [t.shape for t in small_w]
    res = _adamw(_pack(small_w), _pack(small_g), _pack(small_m), _pack(small_v), "adamw_small")
    small = {nm: [] for nm in small_names}
    for packed in res:
        for nm, t in zip(small_names, _unpack(packed, shapes)):
            small[nm].append(t)

    order = ["w_in_attn", "w_out_attn", "rel_bias", "w_in_ssm", "conv_w", "conv_b", "dt_bias", "a_log",
             "d_skip", "ssm_norm_w", "w_out_ssm", "ln_g", "ln_b"]
    table = {**big, **small}
    outs = [loss, grad_x]
    for kind in range(4):
        outs += [table[nm][kind] for nm in order]
    return tuple(outs)
```

```python
import functools
import math

import numpy as np
import jax
import jax.numpy as jnp
from jax import lax
from jax.experimental import pallas as pl
from jax.experimental.pallas import tpu as pltpu

F32 = jnp.float32
BF16 = jnp.bfloat16
MESH = pl.DeviceIdType.MESH

ATTN_PATTERNS = ((128, 1), (512, 4), (2048, 16))
N_GROUPS_ATTN = 3
HEAD_DIM = 128
ATTN_BLOCK = 128
NUM_BUCKETS = 32
MAX_DISTANCE = 2048
SSM_HEAD_DIM = 64
HEADS_PER_SSM_GROUP = 16
SSM_GROUP_WIDTH = HEADS_PER_SSM_GROUP * SSM_HEAD_DIM
D_STATE = 128
CONV_WIDTH = 4
CHUNK = 128
DEPTH = 2
DEEPNORM_ALPHA = (2 * DEPTH) ** 0.25
LN_EPS = 1e-5
RMS_EPS = 1e-5
NEG_INF = -1e30
ADAM_LR = 0.001
ADAM_B1 = 0.9
ADAM_B2 = 0.999
ADAM_EPS = 1e-08
ADAM_WD = 0.01
ADAM_STEP = 10

N_CHIPS = 4
N_DEV = 8

VMEM_LIMIT_V7X = 56 * 1024 * 1024
LANES = 128


def _cparams(sem=None):
    return pltpu.CompilerParams(dimension_semantics=sem, vmem_limit_bytes=VMEM_LIMIT_V7X)


def _sigmoid(x):
    return 0.5 * jnp.tanh(0.5 * x) + 0.5


def _dot(a, b):
    return jnp.dot(a, b, preferred_element_type=F32)


def _dot_nt(a, b):
    return lax.dot_general(a, b, (((1,), (1,)), ((), ())), preferred_element_type=F32)


def _dot_tn(a, b):
    return lax.dot_general(a, b, (((0,), (0,)), ((), ())), preferred_element_type=F32)


def _split2(x):
    hi = x.astype(BF16)
    lo = (x - hi.astype(F32)).astype(BF16)
    return hi, lo


def _split3(x):
    hi = x.astype(BF16)
    r = x - hi.astype(F32)
    mid = r.astype(BF16)
    lo = (r - mid.astype(F32)).astype(BF16)
    return hi, mid, lo


def _matmul(a, b, *, mode, grid, a_spec, b_spec, out_shape, out_spec, tile, name,
            add=None, add_spec=None, add_scale=1.0, after=None, into=None):
    nk = grid[2]
    tm, tn = tile
    dot = {"nn": _dot, "nt": _dot_nt, "tn": _dot_tn}[mode]
    has_add = add is not None
    has_after = after is not None
    has_into = into is not None

    def finish(r, add_ref, o_ref):
        if has_add:
            r = r + add_scale * add_ref[...].astype(F32)
        o_ref[...] = r.astype(o_ref.dtype)

    def body_one(*refs):
        a_ref, b_ref = refs[:2]
        finish(dot(a_ref[...].astype(BF16), b_ref[...].astype(BF16)), refs[2] if has_add else None, refs[-1])

    def body_acc(*refs):
        a_ref, b_ref = refs[:2]
        add_ref = refs[2] if has_add else None
        o_ref, acc_ref = refs[-2:]
        k = pl.program_id(2)

        @pl.when(k == 0)
        def _():
            acc_ref[...] = jnp.zeros_like(acc_ref)

        acc_ref[...] += dot(a_ref[...].astype(BF16), b_ref[...].astype(BF16))

        @pl.when(k == nk - 1)
        def _():
            finish(acc_ref[...], add_ref, o_ref)

    in_specs = ([a_spec, b_spec] + ([add_spec] if has_add else []) + ([_ANY] if has_after else [])
                + ([_ANY] if has_into else []))
    args = (a, b) + ((add,) if has_add else ()) + ((after,) if has_after else ()) + ((into,) if has_into else ())
    return pl.pallas_call(
        body_one if nk == 1 else body_acc, name=name, grid=grid, in_specs=in_specs, out_specs=out_spec,
        out_shape=out_shape,
        input_output_aliases={len(args) - 1: 0} if has_into else {},
        scratch_shapes=[] if nk == 1 else [pltpu.VMEM((tm, tn), F32)],
        compiler_params=_cparams(("parallel", "parallel", "arbitrary")),
    )(*args)


def _pick(n, pref):
    for t in pref:
        if n % t == 0:
            return t
    return n


_TILE_PREF = (1024, 512, 256, 128)
_K_TILE_PREF = (2048,) + _TILE_PREF


def _k_tile(k, out_dtype, has_add):
    return _pick(k, _K_TILE_PREF if (has_add or out_dtype != BF16) else (4096,) + _K_TILE_PREF)


def _mm_nn_sharded(a, w4, out_dtype, name, after=None, col_off=0, n=None, classes=1):
    m, k = a.shape
    _, _, nn = w4.shape
    n = N_CHIPS * nn if n is None else n
    tm, tk = _pick(m // classes, _TILE_PREF), _k_tile(k, out_dtype, False)
    tn = _pick(math.gcd(math.gcd(nn, n), col_off) if col_off else math.gcd(nn, n), _TILE_PREF)
    npb = nn // tn
    co = col_off // tn
    bpc, kb = m // classes // tm, k // tk
    av = a.reshape(m // classes, classes * k)
    out_shape = jax.ShapeDtypeStruct((m, n), out_dtype)
    if tm < _TILE_PREF[0]:
        return _matmul(
            av, w4, mode="nn", grid=(n // tn, m // tm, 1), tile=(tm, tn), name=name,
            a_spec=pl.BlockSpec((tm, k), lambda j, i, kk: (i % bpc, i // bpc)),
            b_spec=pl.BlockSpec((None, k, tn), lambda j, i, kk: ((j + co) // npb, 0, (j + co) % npb)),
            out_shape=out_shape, out_spec=pl.BlockSpec((tm, tn), lambda j, i, kk: (i, j)), after=after)
    return _matmul(
        av, w4, mode="nn", grid=(m // tm, n // tn, kb), tile=(tm, tn), name=name,
        a_spec=pl.BlockSpec((tm, tk), lambda i, j, kk: (i % bpc, (i // bpc) * kb + kk)),
        b_spec=pl.BlockSpec((None, tk, tn), lambda i, j, kk: ((j + co) // npb, kk, (j + co) % npb)),
        out_shape=out_shape, out_spec=pl.BlockSpec((tm, tn), lambda i, j, kk: (i, j)), after=after)


def _mm_nn(a, b, out_dtype, name, b_row_off=0, add=None, add_scale=1.0):
    m, k = a.shape
    _, n = b.shape
    tm, tk, tn = _pick(m, _TILE_PREF), _k_tile(k, out_dtype, add is not None), _pick(n, _TILE_PREF)
    assert b_row_off % tk == 0
    ko = b_row_off // tk
    return _matmul(
        a, b, mode="nn", grid=(m // tm, n // tn, k // tk), tile=(tm, tn), name=name,
        a_spec=pl.BlockSpec((tm, tk), lambda i, j, kk: (i, kk)),
        b_spec=pl.BlockSpec((tk, tn), lambda i, j, kk: (kk + ko, j)),
        out_shape=jax.ShapeDtypeStruct((m, n), out_dtype),
        out_spec=pl.BlockSpec((tm, tn), lambda i, j, kk: (i, j)),
        add=add, add_spec=pl.BlockSpec((tm, tn), lambda i, j, kk: (i, j)), add_scale=add_scale)


def _mm_nt(a, b, out_dtype, name, add=None, add_scale=1.0, n=None, b_row_off=0):
    m, k = a.shape
    n = b.shape[0] if n is None else n
    tm, tk, tn = _pick(m, _TILE_PREF), _k_tile(k, out_dtype, add is not None), _pick(n, _TILE_PREF)
    assert b_row_off % tn == 0
    no = b_row_off // tn
    return _matmul(
        a, b, mode="nt", grid=(m // tm, n // tn, k // tk), tile=(tm, tn), name=name,
        a_spec=pl.BlockSpec((tm, tk), lambda i, j, kk: (i, kk)),
        b_spec=pl.BlockSpec((tn, tk), lambda i, j, kk: (j + no, kk)),
        out_shape=jax.ShapeDtypeStruct((m, n), out_dtype),
        out_spec=pl.BlockSpec((tm, tn), lambda i, j, kk: (i, j)),
        add=add, add_spec=pl.BlockSpec((tm, tn), lambda i, j, kk: (i, j)), add_scale=add_scale)


def _mm_nt_sharded_k(a, w4, out_dtype, name, add=None, add_scale=1.0, after=None):
    m, _ = a.shape
    _, n, kn = w4.shape
    tm, tk, tn = _pick(m, _TILE_PREF), _pick(kn, (2560,) + _TILE_PREF), _pick(n, _TILE_PREF)
    kpb = kn // tk
    return _matmul(
        a, w4, mode="nt", grid=(m // tm, n // tn, N_CHIPS * kpb), tile=(tm, tn), name=name,
        a_spec=pl.BlockSpec((tm, tk), lambda i, j, kk: (i, kk)),
        b_spec=pl.BlockSpec((None, tn, tk), lambda i, j, kk: (kk // kpb, j, kk % kpb)),
        out_shape=jax.ShapeDtypeStruct((m, n), out_dtype),
        out_spec=pl.BlockSpec((tm, tn), lambda i, j, kk: (i, j)),
        add=add, add_spec=pl.BlockSpec((tm, tn), lambda i, j, kk: (i, j)), add_scale=add_scale, after=after)


def _mm_tn(a, b, out_dtype, name, shard_cols=None, out_rows=None, out_row_off=0, into=None, after=None):
    k, m = a.shape
    _, n = b.shape
    nn = n if shard_cols is None else shard_cols
    tm, tk, tn = _pick(m, _TILE_PREF), _k_tile(k, out_dtype, False), _pick(nn, _TILE_PREF)
    if shard_cols is None:
        assert out_row_off % tm == 0
        ro = out_row_off // tm
        out_shape = jax.ShapeDtypeStruct((m if out_rows is None else out_rows, n), out_dtype)
        out_spec = pl.BlockSpec((tm, tn), lambda i, j, kk: (i + ro, j))
    else:
        npb = nn // tn
        out_shape = jax.ShapeDtypeStruct((n // nn, m, nn), out_dtype)
        out_spec = pl.BlockSpec((None, tm, tn), lambda i, j, kk: (j // npb, i, j % npb))
    return _matmul(
        a, b, mode="tn", grid=(m // tm, n // tn, k // tk), tile=(tm, tn), name=name,
        a_spec=pl.BlockSpec((tk, tm), lambda i, j, kk: (kk, i)),
        b_spec=pl.BlockSpec((tk, tn), lambda i, j, kk: (kk, j)),
        out_shape=out_shape, out_spec=out_spec, into=into, after=after)


def _cast_bf16(x, name, after=None):
    r, c = x.shape
    tr = _pick(r, (512, 256, 128, 8))
    extra = [] if after is None else [after]

    def body(x_ref, *rest):
        rest[-1][...] = x_ref[...].astype(BF16)

    return pl.pallas_call(
        body, name=name, grid=(r // tr,),
        in_specs=[pl.BlockSpec((tr, c), lambda i: (i, 0))] + [_ANY] * len(extra),
        out_specs=pl.BlockSpec((tr, c), lambda i: (i, 0)),
        out_shape=jax.ShapeDtypeStruct((r, c), BF16),
        compiler_params=_cparams(("parallel",)),
    )(x, *extra)


def _bucket_tiles():
    qi = np.arange(ATTN_BLOCK)[:, None]
    ki = np.arange(2 * ATTN_BLOCK)[None, :]
    delta = np.clip(ATTN_BLOCK + qi - ki, 0, None)
    tiles = []
    max_exact = NUM_BUCKETS // 2
    for _, dil in ATTN_PATTERNS:
        dist = (delta * dil).astype(np.int32)
        d_f = np.maximum(dist, 1).astype(np.float32)
        large = max_exact + (np.log(d_f / np.float32(max_exact)) / np.float32(math.log(MAX_DISTANCE / max_exact))
                             * np.float32(NUM_BUCKETS - max_exact)).astype(np.int32)
        large = np.minimum(large, NUM_BUCKETS - 1)
        tiles.append(np.where(dist < max_exact, dist, large).astype(np.int32))
    return jnp.asarray(np.stack(tiles))


def _bias_expand(rel_bias, buckets, hpg):
    def body(tab_ref, bk_ref, o_ref):
        g, h = pl.program_id(0), pl.program_id(1)
        bk = bk_ref[...]
        acc = jnp.zeros((ATTN_BLOCK, 2 * ATTN_BLOCK), F32)
        for b in range(NUM_BUCKETS):
            acc = jnp.where(bk == b, tab_ref[b, g * hpg + h], acc)
        o_ref[...] = acc

    return pl.pallas_call(
        body, name="bias_expand", grid=(N_GROUPS_ATTN, hpg),
        in_specs=[pl.BlockSpec(memory_space=pltpu.SMEM),
                  pl.BlockSpec((None, ATTN_BLOCK, 2 * ATTN_BLOCK), lambda g, h: (g, 0, 0))],
        out_specs=pl.BlockSpec((None, None, ATTN_BLOCK, 2 * ATTN_BLOCK), lambda g, h: (g, h, 0, 0)),
        out_shape=jax.ShapeDtypeStruct((N_GROUPS_ATTN, hpg, ATTN_BLOCK, 2 * ATTN_BLOCK), F32),
        compiler_params=_cparams(("parallel", "parallel")),
    )(rel_bias, buckets)


def _bias_reduce(dtiles, buckets, hpg):
    def body(t_ref, bk_ref, o_ref):
        bk = bk_ref[...]
        t = t_ref[...]
        rows = lax.broadcasted_iota(jnp.int32, (NUM_BUCKETS, LANES), 0)
        acc = jnp.zeros((NUM_BUCKETS, LANES), F32)
        for b in range(NUM_BUCKETS):
            s = jnp.sum(jnp.sum(jnp.where(bk == b, t, 0.0), axis=1, keepdims=True), axis=0, keepdims=True)
            acc = jnp.where(rows == b, s, acc)
        o_ref[...] = acc

    return pl.pallas_call(
        body, name="bias_reduce", grid=(N_GROUPS_ATTN, hpg),
        in_specs=[pl.BlockSpec((None, None, ATTN_BLOCK, 2 * ATTN_BLOCK), lambda g, h: (g, h, 0, 0)),
                  pl.BlockSpec((None, ATTN_BLOCK, 2 * ATTN_BLOCK), lambda g, h: (g, 0, 0))],
        out_specs=pl.BlockSpec((None, NUM_BUCKETS, LANES), lambda g, h: (g * hpg + h, 0, 0)),
        out_shape=jax.ShapeDtypeStruct((N_GROUPS_ATTN * hpg, NUM_BUCKETS, LANES), F32),
        compiler_params=_cparams(("parallel", "parallel")),
    )(dtiles, buckets)


def _cast_x3(x, name, after=None):
    r, c = x.shape
    tr = _pick(r, (512, 256, 128, 8))
    extra = [] if after is None else [after]

    def body(x_ref, *rest):
        rest[-1][...] = x_ref[...].astype(BF16)

    return pl.pallas_call(
        body, name=name, grid=(r // tr,),
        in_specs=[pl.BlockSpec((tr, c), lambda i: (i, 0))] + [_ANY] * len(extra),
        out_specs=pl.BlockSpec((None, tr, c), lambda i: (0, i, 0)),
        out_shape=jax.ShapeDtypeStruct((N_GROUPS_ATTN, r, c), BF16),
        compiler_params=_cparams(("parallel",)),
    )(x, *extra)


def _class_copy(x3, slot, dil, name):
    _, s, d = x3.shape
    rows = s // dil
    tm = _pick(rows, (512, 256, 128))
    nbk = rows // tm

    def body(v_ref, x3_ref, o_ref):
        o_ref[...] = v_ref[...]

    return pl.pallas_call(
        body, name=name, grid=(dil, nbk),
        in_specs=[pl.BlockSpec((tm, d), lambda r, i: (i, r)), _ANY],
        out_specs=pl.BlockSpec((None, tm, d), lambda r, i: (slot, r * nbk + i, 0)),
        out_shape=jax.ShapeDtypeStruct(x3.shape, x3.dtype), input_output_aliases={1: 0},
        compiler_params=_cparams(("parallel", "parallel")),
    )(x3[0].reshape(rows, dil * d), x3)


def _in_proj_shard(x3, wa4, kvec, p, d_attn, name, into=None, after=None):
    _, s, d = x3.shape
    _, _, nn = wa4.shape
    tm = _pick(s, _TILE_PREF)
    tn = _pick(math.gcd(nn, 3 * d_attn), _TILE_PREF)
    npb, bpg = nn // tn, 3 * d_attn // tn
    extra = ([] if after is None else [after]) + ([] if into is None else [into])

    def block(k, j):
        return jnp.bitwise_xor(k[0], p) * npb + j

    def slot(k, j):
        jb = block(k, j)
        return jnp.where(jb < N_GROUPS_ATTN * bpg, jb // bpg, 0)

    def body(k_ref, a_ref, b_ref, *rest):
        rest[-1][...] = _dot(a_ref[...], b_ref[...]).astype(BF16)

    grid_spec = pltpu.PrefetchScalarGridSpec(
        num_scalar_prefetch=1, grid=(s // tm, npb),
        in_specs=[pl.BlockSpec((None, tm, d), lambda i, j, k: (slot(k, j), i, 0)),
                  pl.BlockSpec((None, d, tn), lambda i, j, k: (jnp.bitwise_xor(k[0], p), 0, j))]
        + [_ANY] * len(extra),
        out_specs=pl.BlockSpec((tm, tn), lambda i, j, k: (i, block(k, j))))
    return pl.pallas_call(
        body, name=name, grid_spec=grid_spec, out_shape=jax.ShapeDtypeStruct((s, N_CHIPS * nn), BF16),
        input_output_aliases={} if into is None else {2 + len(extra): 0},
        compiler_params=_cparams(("parallel", "parallel")),
    )(kvec, x3, wa4, *extra)


def _attn_valid(n_is_first):
    qi = lax.broadcasted_iota(jnp.int32, (ATTN_BLOCK, 2 * ATTN_BLOCK), 0)
    ki = lax.broadcasted_iota(jnp.int32, (ATTN_BLOCK, 2 * ATTN_BLOCK), 1)
    delta = ATTN_BLOCK + qi - ki
    band = (delta >= 0) & (delta <= ATTN_BLOCK)
    return band & (jnp.logical_not(n_is_first) | (ki >= ATTN_BLOCK))


def _attn_fwd(pg, bias, g, dil, hpg):
    s = pg.shape[0]
    w = hpg * HEAD_DIM
    rows = s // dil
    nb = rows // ATTN_BLOCK
    scale = HEAD_DIM ** -0.5

    def body(q_ref, kc_ref, kp_ref, vc_ref, vp_ref, bias_ref, o_ref, lse_ref):
        valid = _attn_valid(pl.program_id(1) == 0)
        lane = lax.broadcasted_iota(jnp.int32, (ATTN_BLOCK, LANES), 1)
        lse = jnp.zeros((ATTN_BLOCK, LANES), F32)
        for h in range(hpg):
            sl = slice(h * HEAD_DIM, (h + 1) * HEAD_DIM)
            k2 = jnp.concatenate([kp_ref[:, sl], kc_ref[:, sl]], axis=0)
            v2 = jnp.concatenate([vp_ref[:, sl], vc_ref[:, sl]], axis=0)
            sc = _dot_nt(q_ref[:, sl], k2) * scale + bias_ref[h]
            sc = jnp.where(valid, sc, NEG_INF)
            m = jnp.max(sc, axis=1, keepdims=True)
            p = jnp.exp(sc - m)
            l = jnp.sum(p, axis=1, keepdims=True)
            o_ref[:, sl] = _dot(p.astype(BF16), v2) * (1.0 / l)
            lse = jnp.where(lane == h, m + jnp.log(l), lse)
        lse_ref[...] = lse

    def col(off):
        return lambda r, n: (r * nb + n, 3 * g + off)

    def colp(off):
        return lambda r, n: (r * nb + jnp.maximum(n - 1, 0), 3 * g + off)

    blk = (ATTN_BLOCK, w)
    tok = pl.BlockSpec(blk, lambda r, n: (n, r))
    tok1 = pl.BlockSpec((ATTN_BLOCK, LANES), lambda r, n: (n, r))
    o, lse = pl.pallas_call(
        body, name=f"attn_fwd_g{g}", grid=(dil, nb),
        in_specs=[pl.BlockSpec(blk, col(0)), pl.BlockSpec(blk, col(1)), pl.BlockSpec(blk, colp(1)),
                  pl.BlockSpec(blk, col(2)), pl.BlockSpec(blk, colp(2)),
                  pl.BlockSpec((None, hpg, ATTN_BLOCK, 2 * ATTN_BLOCK), lambda r, n: (g, 0, 0, 0))],
        out_specs=[tok, tok1],
        out_shape=[jax.ShapeDtypeStruct((rows, dil * w), F32), jax.ShapeDtypeStruct((rows, dil * LANES), F32)],
        compiler_params=_cparams(("parallel", "parallel")),
    )(pg, pg, pg, pg, pg, bias)
    return o.reshape(s, w), lse.reshape(s, LANES)


def _attn_combine(os_, lses, pa, hpg):
    s, w = os_[0].shape
    gate_blk = pa.shape[1] // w - 1
    tm = _pick(s, (256, 128))

    def body(o0, o1, o2, l0, l1, l2, gate_ref, o_ref, lse_ref, y_ref):
        a0, a1, a2 = l0[...], l1[...], l2[...]
        m = jnp.maximum(jnp.maximum(a0, a1), a2)
        e0, e1, e2 = jnp.exp(a0 - m), jnp.exp(a1 - m), jnp.exp(a2 - m)
        den = e0 + e1 + e2
        inv = 1.0 / den
        w0, w1, w2 = e0 * inv, e1 * inv, e2 * inv
        lse_ref[...] = m + jnp.log(den)
        for h in range(hpg):
            sl = slice(h * HEAD_DIM, (h + 1) * HEAD_DIM)
            o = w0[:, h:h + 1] * o0[:, sl] + w1[:, h:h + 1] * o1[:, sl] + w2[:, h:h + 1] * o2[:, sl]
            gate = gate_ref[:, sl].astype(F32)
            o_ref[:, sl] = o.astype(BF16)
            y_ref[:, sl] = (o * (gate * _sigmoid(gate))).astype(BF16)

    spec = pl.BlockSpec((tm, w), lambda i: (i, 0))
    spec1 = pl.BlockSpec((tm, LANES), lambda i: (i, 0))
    return pl.pallas_call(
        body, name="attn_combine", grid=(s // tm,),
        in_specs=[spec] * 3 + [spec1] * 3 + [pl.BlockSpec((tm, w), lambda i: (i, gate_blk))],
        out_specs=[spec, spec1, spec],
        out_shape=[jax.ShapeDtypeStruct((s, w), BF16), jax.ShapeDtypeStruct((s, LANES), F32),
                   jax.ShapeDtypeStruct((s, w), BF16)],
        compiler_params=_cparams(("parallel",)),
    )(*os_, *lses, pa)


def _attn_pre_bwd(dy, o, pa, hpg):
    s, w = dy.shape
    gate_blk = pa.shape[1] // w - 1
    tm = _pick(s, (256, 128))

    def body(dy_ref, o_ref, gate_ref, do_ref, dl_ref, dg_ref):
        gate = gate_ref[...].astype(F32)
        sg = _sigmoid(gate)
        dyv = dy_ref[...].astype(F32)
        ov = o_ref[...].astype(F32)
        do = dyv * (gate * sg)
        do_ref[...] = do.astype(BF16)
        dg_ref[...] = (dyv * ov * (sg * (1.0 + gate * (1.0 - sg)))).astype(BF16)
        prod = do * ov
        lane = lax.broadcasted_iota(jnp.int32, (tm, LANES), 1)
        dl = jnp.zeros((tm, LANES), F32)
        for h in range(hpg):
            sl = slice(h * HEAD_DIM, (h + 1) * HEAD_DIM)
            dl = jnp.where(lane == h, jnp.sum(prod[:, sl], axis=1, keepdims=True), dl)
        dl_ref[...] = dl

    spec = pl.BlockSpec((tm, w), lambda i: (i, 0))
    return pl.pallas_call(
        body, name="attn_pre_bwd", grid=(s // tm,),
        in_specs=[spec, spec, pl.BlockSpec((tm, w), lambda i: (i, gate_blk))],
        out_specs=[spec, pl.BlockSpec((tm, LANES), lambda i: (i, 0)), spec],
        out_shape=[jax.ShapeDtypeStruct((s, w), BF16), jax.ShapeDtypeStruct((s, LANES), F32),
                   jax.ShapeDtypeStruct((s, w), BF16)],
        compiler_params=_cparams(("parallel",)),
    )(dy, o, pa)


def _attn_bwd(pg, bias, do, lse, delta, g, dil, hpg):
    s = pg.shape[0]
    w = hpg * HEAD_DIM
    rows = s // dil
    nb = rows // ATTN_BLOCK
    dov = do.reshape(rows, dil * w)
    lsev, dlv = (t.reshape(rows, dil * LANES) for t in (lse, delta))
    scale = HEAD_DIM ** -0.5

    def body(q_ref, kc_ref, kp_ref, vc_ref, vp_ref, bias_ref, do_ref, lse_ref, dl_ref,
             dq_ref, dk_ref, dv_ref, db_ref, dkc_ref, dvc_ref):
        r, i = pl.program_id(0), pl.program_id(1)
        n = nb - 1 - i
        valid = _attn_valid(n == 0)

        @pl.when((r == 0) & (i == 0))
        def _():
            db_ref[...] = jnp.zeros_like(db_ref)

        @pl.when(i == 0)
        def _():
            dkc_ref[...] = jnp.zeros_like(dkc_ref)
            dvc_ref[...] = jnp.zeros_like(dvc_ref)

        for h in range(hpg):
            sl = slice(h * HEAD_DIM, (h + 1) * HEAD_DIM)
            q = q_ref[:, sl]
            dov_ = do_ref[:, sl]
            k2 = jnp.concatenate([kp_ref[:, sl], kc_ref[:, sl]], axis=0)
            v2 = jnp.concatenate([vp_ref[:, sl], vc_ref[:, sl]], axis=0)
            sc = _dot_nt(q, k2) * scale + bias_ref[h]
            p = jnp.exp(jnp.where(valid, sc - lse_ref[:, h:h + 1], NEG_INF))
            dp = _dot_nt(dov_, v2)
            ds = p * (dp - dl_ref[:, h:h + 1])
            db_ref[h] += ds
            dsb = ds.astype(BF16)
            dq_ref[:, sl] = (_dot(dsb, k2) * scale).astype(BF16)
            dk2 = _dot_tn(dsb, q) * scale
            dv2 = _dot_tn(p.astype(BF16), dov_)
            dk_ref[:, sl] = (dk2[ATTN_BLOCK:] + dkc_ref[:, sl]).astype(BF16)
            dv_ref[:, sl] = (dv2[ATTN_BLOCK:] + dvc_ref[:, sl]).astype(BF16)
            dkc_ref[:, sl] = dk2[:ATTN_BLOCK]
            dvc_ref[:, sl] = dv2[:ATTN_BLOCK]

    def col(off):
        return lambda r, i: (r * nb + nb - 1 - i, 3 * g + off)

    def colp(off):
        return lambda r, i: (r * nb + jnp.maximum(nb - 2 - i, 0), 3 * g + off)

    blk = (ATTN_BLOCK, w)
    tok = pl.BlockSpec(blk, lambda r, i: (nb - 1 - i, r))
    tok1 = pl.BlockSpec((ATTN_BLOCK, LANES), lambda r, i: (nb - 1 - i, r))
    dq, dk, dv, db = pl.pallas_call(
        body, name=f"attn_bwd_g{g}", grid=(dil, nb),
        in_specs=[pl.BlockSpec(blk, col(0)), pl.BlockSpec(blk, col(1)), pl.BlockSpec(blk, colp(1)),
                  pl.BlockSpec(blk, col(2)), pl.BlockSpec(blk, colp(2)),
                  pl.BlockSpec((None, hpg, ATTN_BLOCK, 2 * ATTN_BLOCK), lambda r, i: (g, 0, 0, 0)),
                  tok, tok1, tok1],
        out_specs=[tok, tok, tok,
                   pl.BlockSpec((hpg, ATTN_BLOCK, 2 * ATTN_BLOCK), lambda r, i: (0, 0, 0))],
        out_shape=[jax.ShapeDtypeStruct((rows, dil * w), BF16)] * 3
        + [jax.ShapeDtypeStruct((hpg, ATTN_BLOCK, 2 * ATTN_BLOCK), F32)],
        scratch_shapes=[pltpu.VMEM(blk, F32), pltpu.VMEM(blk, F32)],
        compiler_params=_cparams(("arbitrary", "arbitrary")),
    )(pg, pg, pg, pg, pg, bias, dov, lsev, dlv)
    return dq.reshape(s, w), dk.reshape(s, w), dv.reshape(s, w), db


def _ln_fwd(xin, h, gamma, beta, name, affine_in=None, target=None):
    s, d = xin.shape
    tm = _pick(s, (128,))
    has_aff = affine_in is not None
    has_tgt = target is not None

    def body(*refs):
        it = iter(refs)
        x_ref, h_ref, g_ref, b_ref = next(it), next(it), next(it), next(it)
        if has_aff:
            gi_ref, bi_ref = next(it), next(it)
        if has_tgt:
            t_ref = next(it)
        xh_ref, rs_ref = next(it), next(it)
        x = x_ref[...]
        if has_aff:
            x = x * gi_ref[...] + bi_ref[...]
        u = DEEPNORM_ALPHA * x + h_ref[...]
        mu = jnp.mean(u, axis=1, keepdims=True)
        uc = u - mu
        var = jnp.mean(uc * uc, axis=1, keepdims=True)
        rstd = lax.rsqrt(var + LN_EPS)
        xhat = uc * rstd
        xh_ref[...] = xhat
        rs_ref[...] = rstd
        y = xhat * g_ref[...] + b_ref[...]
        if has_tgt:
            dy_ref, l_ref = next(it), next(it)
            e = y - t_ref[...]
            dy_ref[...] = e * (1.0 / d)
            l_ref[...] = jnp.sum(e * e, axis=1, keepdims=True)
        else:
            y_ref = next(it)
            y_ref[...] = y.astype(BF16)

    row = pl.BlockSpec((tm, d), lambda i: (i, 0))
    vec = pl.BlockSpec((1, d), lambda i: (0, 0))
    one = pl.BlockSpec((tm, 1), lambda i: (i, 0))
    in_specs = [row, row, vec, vec] + ([vec, vec] if has_aff else []) + ([row] if has_tgt else [])
    args = [xin, h, gamma, beta] + (list(affine_in) if has_aff else []) + ([target] if has_tgt else [])
    out_specs = [row, one] + ([row, one] if has_tgt else [row])
    out_shape = [jax.ShapeDtypeStruct((s, d), F32), jax.ShapeDtypeStruct((s, 1), F32)]
    out_shape += ([jax.ShapeDtypeStruct((s, d), F32), jax.ShapeDtypeStruct((s, 1), F32)] if has_tgt
                  else [jax.ShapeDtypeStruct((s, d), BF16)])
    return pl.pallas_call(
        body, name=name, grid=(s // tm,), in_specs=in_specs, out_specs=out_specs, out_shape=out_shape,
        compiler_params=_cparams(("parallel",)),
    )(*args)


def _ln_bwd(dy, xhat, rstd, gamma, name):
    s, d = dy.shape
    tm = _pick(s, (128,))

    def body(dy_ref, xh_ref, rs_ref, g_ref, du_ref, dub_ref, dg_ref, db_ref):
        @pl.when(pl.program_id(0) == 0)
        def _():
            dg_ref[...] = jnp.zeros_like(dg_ref)
            db_ref[...] = jnp.zeros_like(db_ref)

        dyv = dy_ref[...]
        xh = xh_ref[...]
        dg_ref[...] += jnp.sum(dyv * xh, axis=0, keepdims=True)
        db_ref[...] += jnp.sum(dyv, axis=0, keepdims=True)
        dxh = dyv * g_ref[...]
        m1 = jnp.mean(dxh, axis=1, keepdims=True)
        m2 = jnp.mean(dxh * xh, axis=1, keepdims=True)
        du = rs_ref[...] * (dxh - m1 - xh * m2)
        du_ref[...] = du
        dub_ref[...] = du.astype(BF16)

    row = pl.BlockSpec((tm, d), lambda i: (i, 0))
    vec = pl.BlockSpec((1, d), lambda i: (0, 0))
    one = pl.BlockSpec((tm, 1), lambda i: (i, 0))
    return pl.pallas_call(
        body, name=name, grid=(s // tm,), in_specs=[row, row, one, vec],
        out_specs=[row, row, vec, vec],
        out_shape=[jax.ShapeDtypeStruct((s, d), F32), jax.ShapeDtypeStruct((s, d), BF16),
                   jax.ShapeDtypeStruct((1, d), F32), jax.ShapeDtypeStruct((1, d), F32)],
        compiler_params=_cparams(("arbitrary",)),
    )(dy, xhat, rstd, gamma)


_HALO = 16
_STRIP = 16


def _strips(tm, fn, init, reverse=False):
    n = tm // _STRIP

    def step(i, carry):
        s_ = n - 1 - i if reverse else i
        return fn(pl.ds(pl.multiple_of(s_ * _STRIP, _STRIP), _STRIP), carry)

    return lax.fori_loop(0, n, step, init)


def _fold8(t):
    return t[0:8] + t[8:16]


def _conv_taps(ext, tm, w_ref):
    acc = None
    for k in range(CONV_WIDTH):
        lo = _HALO - (CONV_WIDTH - 1) + k
        term = w_ref[k:k + 1, :] * ext[lo:lo + tm, :]
        acc = term if acc is None else acc + term
    return acc


def _conv_strip(prev, cur, w_ref):
    ext = jnp.concatenate([prev, cur], axis=0)
    acc, taps = None, []
    for k in range(CONV_WIDTH):
        lo = _STRIP - (CONV_WIDTH - 1) + k
        taps.append(ext[lo:lo + _STRIP, :])
        term = w_ref[k:k + 1, :] * taps[k]
        acc = term if acc is None else acc + term
    return acc, taps


def _conv_fwd(pzx, conv_w, conv_b, d_inner):
    s, _ = pzx.shape
    cd = conv_w.shape[1]
    tm = _pick(s, (512, 256, 128))
    tc = _pick(cd, (1024, 512, 256, 128))
    off = d_inner // tc
    hb = tm // _HALO

    def body(x_ref, p_ref, w_ref, b_ref, o_ref):
        prev = jnp.where(pl.program_id(0) > 0, p_ref[...].astype(F32), 0.0)
        ext = jnp.concatenate([prev, x_ref[...].astype(F32)], axis=0)
        pre = _conv_taps(ext, tm, w_ref) + b_ref[...]
        o_ref[...] = (pre * _sigmoid(pre)).astype(BF16)

    return pl.pallas_call(
        body, name="conv_fwd", grid=(s // tm, cd // tc),
        in_specs=[pl.BlockSpec((tm, tc), lambda i, j: (i, off + j)),
                  pl.BlockSpec((_HALO, tc), lambda i, j: (jnp.maximum(i * hb - 1, 0), off + j)),
                  pl.BlockSpec((CONV_WIDTH, tc), lambda i, j: (0, j)),
                  pl.BlockSpec((1, tc), lambda i, j: (0, j))],
        out_specs=pl.BlockSpec((tm, tc), lambda i, j: (i, j)),
        out_shape=jax.ShapeDtypeStruct((s, cd), BF16),
        compiler_params=_cparams(("parallel", "parallel")),
    )(pzx, pzx, conv_w, conv_b)


def _conv_bwd_a(pzx, dxbc, conv_w, conv_b, d_inner):
    s, _ = pzx.shape
    cd = conv_w.shape[1]
    tm = _pick(s, (512, 256, 128))
    tc = _pick(cd, (1024, 512, 256, 128))
    off = d_inner // tc
    hb = tm // _HALO

    def body(x_ref, p_ref, d_ref, w_ref, b_ref, o_ref, dw_ref, db_ref, acc_ref):
        @pl.when(pl.program_id(1) == 0)
        def _():
            dw_ref[...] = jnp.zeros_like(dw_ref)
            db_ref[...] = jnp.zeros_like(db_ref)

        acc_ref[...] = jnp.zeros_like(acc_ref)

        def strip(rows, prev):
            cur = x_ref[rows, :].astype(F32)
            pre, taps = _conv_strip(prev, cur, w_ref)
            pre = pre + b_ref[...]
            sg = _sigmoid(pre)
            dpre = d_ref[rows, :].astype(F32) * (sg * (1.0 + pre * (1.0 - sg)))
            o_ref[rows, :] = dpre
            for k in range(CONV_WIDTH):
                acc_ref[k] += _fold8(dpre * taps[k])
            acc_ref[CONV_WIDTH] += _fold8(dpre)
            return cur

        _strips(tm, strip, jnp.where(pl.program_id(1) > 0, p_ref[...].astype(F32), 0.0))
        for k in range(CONV_WIDTH):
            dw_ref[k:k + 1, :] += jnp.sum(acc_ref[k], axis=0, keepdims=True)
        db_ref[...] += jnp.sum(acc_ref[CONV_WIDTH], axis=0, keepdims=True)

    return pl.pallas_call(
        body, name="conv_bwd_a", grid=(cd // tc, s // tm),
        in_specs=[pl.BlockSpec((tm, tc), lambda j, i: (i, off + j)),
                  pl.BlockSpec((_HALO, tc), lambda j, i: (jnp.maximum(i * hb - 1, 0), off + j)),
                  pl.BlockSpec((tm, tc), lambda j, i: (i, j)),
                  pl.BlockSpec((CONV_WIDTH, tc), lambda j, i: (0, j)),
                  pl.BlockSpec((1, tc), lambda j, i: (0, j))],
        out_specs=[pl.BlockSpec((tm, tc), lambda j, i: (i, j)),
                   pl.BlockSpec((CONV_WIDTH, tc), lambda j, i: (0, j)),
                   pl.BlockSpec((1, tc), lambda j, i: (0, j))],
        out_shape=[jax.ShapeDtypeStruct((s, cd), F32), jax.ShapeDtypeStruct((CONV_WIDTH, cd), F32),
                   jax.ShapeDtypeStruct((1, cd), F32)],
        scratch_shapes=[pltpu.VMEM((CONV_WIDTH + 1, 8, tc), F32)],
        compiler_params=_cparams(("parallel", "arbitrary")),
    )(pzx, pzx, dxbc, conv_w, conv_b)


def _conv_bwd_b(dpre, conv_w, into, col_off):
    s, cd = dpre.shape
    tm = _pick(s, (512, 256, 128))
    tc = _pick(cd, (1024, 512, 256, 128))
    hb = tm // 8
    nrb = s // tm
    assert col_off % tc == 0
    co = col_off // tc

    def body(x_ref, nx_ref, w_ref, into_ref, o_ref):
        nxt = jnp.where(pl.program_id(0) < nrb - 1, nx_ref[...], 0.0)
        ext = jnp.concatenate([x_ref[...], nxt], axis=0)
        acc = None
        for k in range(CONV_WIDTH):
            lo = CONV_WIDTH - 1 - k
            term = w_ref[k:k + 1, :] * ext[lo:lo + tm, :]
            acc = term if acc is None else acc + term
        o_ref[...] = acc.astype(BF16)

    return pl.pallas_call(
        body, name="conv_bwd_b", grid=(nrb, cd // tc),
        in_specs=[pl.BlockSpec((tm, tc), lambda i, j: (i, j)),
                  pl.BlockSpec((8, tc), lambda i, j: (jnp.minimum((i + 1) * hb, s // 8 - 1), j)),
                  pl.BlockSpec((CONV_WIDTH, tc), lambda i, j: (0, j)), _ANY],
        out_specs=pl.BlockSpec((tm, tc), lambda i, j: (i, j + co)),
        out_shape=jax.ShapeDtypeStruct(into.shape, BF16),
        input_output_aliases={3: 0},
        compiler_params=_cparams(("parallel", "parallel")),
    )(dpre, dpre, conv_w, into)


def _expand_matrix():
    e = np.zeros((LANES, SSM_GROUP_WIDTH), np.float32)
    for h in range(HEADS_PER_SSM_GROUP):
        e[h, h * SSM_HEAD_DIM:(h + 1) * SSM_HEAD_DIM] = 1.0
    return jnp.asarray(e, BF16)


def _expand(t, e):
    hi, lo = _split2(t)
    return _dot(hi, e) + _dot(lo, e)


def _segsum(v, e):
    hi, lo = _split2(v)
    return _dot_nt(hi, e) + _dot_nt(lo, e)


def _tri_dot(tri, x):
    hi, mid, lo = _split3(x)
    return _dot(tri, hi) + _dot(tri, mid) + _dot(tri, lo)


def _ssd_common(dtp_ref, a_ref, dtb_ref, x_ref, e):
    li = lax.broadcasted_iota(jnp.int32, (CHUNK, CHUNK), 0)
    si = lax.broadcasted_iota(jnp.int32, (CHUNK, CHUNK), 1)
    causal = li >= si
    tril = causal.astype(BF16)
    raw = dtp_ref[...] + dtb_ref[...]
    dt = jnp.maximum(raw, 0.0) + jnp.log(1.0 + jnp.exp(-jnp.abs(raw)))
    head_lane = lax.broadcasted_iota(jnp.int32, (1, LANES), 1) < HEADS_PER_SSM_GROUP
    a = jnp.where(head_lane, -jnp.exp(a_ref[...]), 0.0)
    a_cum = _tri_dot(tril, dt * a)
    a_cum_t = a_cum.T
    e_a = jnp.exp(a_cum)
    to_end = jnp.exp(a_cum[CHUNK - 1:CHUNK, :] - a_cum)
    x = x_ref[...].astype(F32)
    dt_e = _expand(dt, e)
    return dict(causal=causal, raw=raw, dt=dt, a=a, a_cum=a_cum, a_cum_t=a_cum_t, e_a=e_a,
                to_end=to_end, x=x, dt_e=dt_e, xdt=x * dt_e, e_a_e=_expand(e_a, e),
                to_end_e=_expand(to_end, e))


def _decay(q, h):
    seg = q["a_cum"][:, h:h + 1] - q["a_cum_t"][h:h + 1, :]
    return jnp.exp(jnp.where(q["causal"], seg, -jnp.inf))


def _ssd_specs(ng, d_inner, rev, nc):
    cidx = (lambda i: nc - 1 - i) if rev else (lambda i: i)
    boff = d_inner // D_STATE
    return dict(
        xs=pl.BlockSpec((CHUNK, SSM_GROUP_WIDTH), lambda g, i: (cidx(i), g)),
        b=pl.BlockSpec((CHUNK, D_STATE), lambda g, i: (cidx(i), boff + g)),
        c=pl.BlockSpec((CHUNK, D_STATE), lambda g, i: (cidx(i), boff + ng + g)),
        dtp=pl.BlockSpec((None, CHUNK, LANES), lambda g, i: (g, cidx(i), 0)),
        vec=pl.BlockSpec((None, 1, LANES), lambda g, i: (g, 0, 0)),
        wide=pl.BlockSpec((None, 1, SSM_GROUP_WIDTH), lambda g, i: (g, 0, 0)),
        e=pl.BlockSpec((LANES, SSM_GROUP_WIDTH), lambda g, i: (0, 0)),
        st=pl.BlockSpec((None, None, D_STATE, SSM_GROUP_WIDTH), lambda g, i: (g, cidx(i), 0, 0)),
        tok=pl.BlockSpec((CHUNK, SSM_GROUP_WIDTH), lambda g, i: (cidx(i), g)),
        bc_out=pl.BlockSpec((CHUNK, D_STATE), lambda g, i: (cidx(i), g)),
    )


def _ssd_fwd(xbc, dtp, a_pad, dtb_pad, dsk_e, e, d_inner):
    s = xbc.shape[0]
    ng = d_inner // SSM_GROUP_WIDTH
    nc = s // CHUNK

    def body(x_ref, b_ref, c_ref, dtp_ref, a_ref, dtb_ref, dsk_ref, e_ref, y_ref, st_ref, state):
        lane = lax.broadcasted_iota(jnp.int32, (CHUNK, LANES), 1)
        @pl.when(pl.program_id(1) == 0)
        def _():
            state[...] = jnp.zeros_like(state)

        ev = e_ref[...]
        q = _ssd_common(dtp_ref, a_ref, dtb_ref, x_ref, ev)
        bm, cm = b_ref[...], c_ref[...]
        cb = _dot_nt(cm, bm)
        s0 = state[...]
        st_ref[...] = s0
        y = _dot(cm, s0.astype(BF16)) * q["e_a_e"] + dsk_ref[...] * q["x"]
        xdt = q["xdt"]
        left = lane[:, :] < SSM_HEAD_DIM
        for j in range(HEADS_PER_SSM_GROUP // 2):
            sl = slice(j * LANES, (j + 1) * LANES)
            x2 = xdt[:, sl]
            m0 = (cb * _decay(q, 2 * j)).astype(BF16)
            m1 = (cb * _decay(q, 2 * j + 1)).astype(BF16)
            mcat = jnp.concatenate([m0, m1], axis=1)
            xbd = jnp.concatenate([jnp.where(left, x2, 0.0), jnp.where(left, 0.0, x2)], axis=0).astype(BF16)
            y_ref[:, sl] = (y[:, sl] + _dot(mcat, xbd)).astype(BF16)
        state[...] = s0 * q["e_a_e"][CHUNK - 1:CHUNK, :] + _dot_tn(bm, (q["to_end_e"] * xdt).astype(BF16))

    sp = _ssd_specs(ng, d_inner, False, nc)
    return pl.pallas_call(
        body, name="ssd_fwd", grid=(ng, nc),
        in_specs=[sp["xs"], sp["b"], sp["c"], sp["dtp"], sp["vec"], sp["vec"], sp["wide"], sp["e"]],
        out_specs=[sp["tok"], sp["st"]],
        out_shape=[jax.ShapeDtypeStruct((s, d_inner), BF16),
                   jax.ShapeDtypeStruct((ng, nc, D_STATE, SSM_GROUP_WIDTH), F32)],
        scratch_shapes=[pltpu.VMEM((D_STATE, SSM_GROUP_WIDTH), F32)],
        compiler_params=_cparams(("parallel", "arbitrary")),
    )(xbc, xbc, xbc, dtp, a_pad, dtb_pad, dsk_e, e)


def _ssd_bwd(xbc, dtp, a_pad, dtb_pad, dsk_e, e, states, dy, d_inner):
    s = xbc.shape[0]
    ng = d_inner // SSM_GROUP_WIDTH
    nc = s // CHUNK

    def body(x_ref, b_ref, c_ref, dtp_ref, a_ref, dtb_ref, dsk_ref, e_ref, st_ref, dy_ref,
             dx_ref, db_ref, dc_ref, ddt_ref, da_ref, ddtb_ref, dd_ref, dstate):
        lane = lax.broadcasted_iota(jnp.int32, (CHUNK, LANES), 1)
        sub = lax.broadcasted_iota(jnp.int32, (CHUNK, LANES), 0)
        @pl.when(pl.program_id(1) == 0)
        def _():
            dstate[...] = jnp.zeros_like(dstate)
            da_ref[...] = jnp.zeros_like(da_ref)
            ddtb_ref[...] = jnp.zeros_like(ddtb_ref)
            dd_ref[...] = jnp.zeros_like(dd_ref)

        ev = e_ref[...]
        q = _ssd_common(dtp_ref, a_ref, dtb_ref, x_ref, ev)
        bm, cm = b_ref[...], c_ref[...]
        cb = _dot_nt(cm, bm)
        x, xdt, e_a_e, to_end_e = q["x"], q["xdt"], q["e_a_e"], q["to_end_e"]
        s0 = st_ref[...]
        s0b = s0.astype(BF16)
        ds1 = dstate[...]
        ds1b = ds1.astype(BF16)
        dy = dy_ref[...].astype(F32)
        e_last_e = e_a_e[CHUNK - 1:CHUNK, :]

        dye = dy * e_a_e
        dyeb = dye.astype(BF16)
        cs0 = _dot(cm, s0b)
        dc = _dot_nt(dyeb, s0b)
        dstate[...] = e_last_e * ds1 + _dot_tn(cm, dyeb)
        da_col = _segsum(dye * cs0, ev)

        gmat = _dot(bm, ds1b)
        dxdt = to_end_e * gmat
        dte = _segsum(xdt * gmat, ev) * q["to_end"]
        db = _dot_nt((to_end_e * xdt).astype(BF16), ds1b)
        da_col = da_col - dte
        last_row = (jnp.sum(dte, axis=0, keepdims=True)
                    + q["e_a"][CHUNK - 1:CHUNK, :] * jnp.sum(_segsum(s0 * ds1, ev), axis=0, keepdims=True))

        left = lane < SSM_HEAD_DIM
        dcb = jnp.zeros((CHUNK, CHUNK), F32)
        row_acc = jnp.zeros((CHUNK, LANES), F32)
        for j in range(HEADS_PER_SSM_GROUP // 2):
            sl = slice(j * LANES, (j + 1) * LANES)
            x2 = xdt[:, sl].astype(BF16)
            dy2 = dy[:, sl]
            dyl = jnp.where(left, dy2, 0.0).astype(BF16)
            dyr = jnp.where(left, 0.0, dy2).astype(BF16)
            ms = []
            for hh, dyh in ((0, dyl), (1, dyr)):
                h = 2 * j + hh
                dec = _decay(q, h)
                m = cb * dec
                dm = _dot_nt(dyh, x2)
                dcb = dcb + dm * dec
                dseg = dm * m
                da_col = da_col + jnp.where(lane == h, jnp.sum(dseg, axis=1, keepdims=True), 0.0)
                row_acc = row_acc + jnp.where(sub == h, jnp.sum(dseg, axis=0, keepdims=True), 0.0)
                ms.append(m.astype(BF16))
            mst = jnp.concatenate(ms, axis=0)
            dyst = jnp.concatenate([dyl, dyr], axis=0)
            d2 = dxdt[:, sl] + _dot_tn(mst, dyst)
            dx_ref[:, sl] = (d2 * q["dt_e"][:, sl] + dsk_ref[:, sl] * dy2).astype(BF16)
            dxdt_x = d2 * x[:, sl]
            if j == 0:
                parts = [dxdt_x]
            else:
                parts.append(dxdt_x)
        dcbb = dcb.astype(BF16)
        dc_ref[...] = (dc + _dot(dcbb, bm)).astype(BF16)
        db_ref[...] = (db + _dot_tn(dcbb, cm)).astype(BF16)

        d_a = da_col - row_acc.T + jnp.where(sub == CHUNK - 1, last_row, 0.0)
        triu = (lax.broadcasted_iota(jnp.int32, (CHUNK, CHUNK), 1)
                >= lax.broadcasted_iota(jnp.int32, (CHUNK, CHUNK), 0)).astype(BF16)
        d_dta = _tri_dot(triu, d_a)
        ddt = d_dta * q["a"] + _segsum(jnp.concatenate(parts, axis=1), ev)
        ddt_raw = ddt * _sigmoid(q["raw"])
        ddt_ref[...] = ddt_raw
        da_ref[...] += jnp.sum(d_dta * q["dt"], axis=0, keepdims=True) * q["a"]
        ddtb_ref[...] += jnp.sum(ddt_raw, axis=0, keepdims=True)
        dd_ref[...] += jnp.sum(dy * x, axis=0, keepdims=True)

    sp = _ssd_specs(ng, d_inner, True, nc)
    return pl.pallas_call(
        body, name="ssd_bwd", grid=(ng, nc),
        in_specs=[sp["xs"], sp["b"], sp["c"], sp["dtp"], sp["vec"], sp["vec"], sp["wide"], sp["e"],
                  sp["st"], sp["tok"]],
        out_specs=[sp["tok"], sp["bc_out"], sp["bc_out"], sp["dtp"], sp["vec"], sp["vec"], sp["wide"]],
        out_shape=[jax.ShapeDtypeStruct((s, d_inner), BF16),
                   jax.ShapeDtypeStruct((s, ng * D_STATE), BF16),
                   jax.ShapeDtypeStruct((s, ng * D_STATE), BF16),
                   jax.ShapeDtypeStruct((ng, s, LANES), F32),
                   jax.ShapeDtypeStruct((ng, 1, LANES), F32),
                   jax.ShapeDtypeStruct((ng, 1, LANES), F32),
                   jax.ShapeDtypeStruct((ng, 1, SSM_GROUP_WIDTH), F32)],
        scratch_shapes=[pltpu.VMEM((D_STATE, SSM_GROUP_WIDTH), F32)],
        compiler_params=_cparams(("parallel", "arbitrary")),
    )(xbc, xbc, xbc, dtp, a_pad, dtb_pad, dsk_e, e, states, dy)


def _gate_norm_fwd(y, pzx, norm_w):
    s, di = y.shape
    ng = di // SSM_GROUP_WIDTH
    tm = _pick(s, (512, 256, 128))

    def body(y_ref, z_ref, w_ref, o_ref):
        z = z_ref[...].astype(F32)
        y2 = y_ref[...].astype(F32) * (z * _sigmoid(z))
        r = lax.rsqrt(jnp.mean(y2 * y2, axis=1, keepdims=True) + RMS_EPS)
        o_ref[...] = (y2 * r * w_ref[...]).astype(BF16)

    blk = pl.BlockSpec((tm, SSM_GROUP_WIDTH), lambda i, g: (i, g))
    return pl.pallas_call(
        body, name="gate_norm_fwd", grid=(s // tm, ng),
        in_specs=[blk, blk, pl.BlockSpec((1, SSM_GROUP_WIDTH), lambda i, g: (0, g))],
        out_specs=blk, out_shape=jax.ShapeDtypeStruct((s, di), BF16),
        compiler_params=_cparams(("parallel", "parallel")),
    )(y, pzx, norm_w)


def _gate_norm_bwd(dy3, y, pzx, norm_w):
    s, di = y.shape
    ng = di // SSM_GROUP_WIDTH
    tm = _pick(s, (512, 256, 128))

    def body(d_ref, y_ref, z_ref, w_ref, dy_ref, dz_ref, dw_ref):
        @pl.when(pl.program_id(1) == 0)
        def _():
            dw_ref[...] = jnp.zeros_like(dw_ref)

        z = z_ref[...].astype(F32)
        yv = y_ref[...].astype(F32)
        sg = _sigmoid(z)
        sz = z * sg
        y2 = yv * sz
        r = lax.rsqrt(jnp.mean(y2 * y2, axis=1, keepdims=True) + RMS_EPS)
        nrm = y2 * r
        d3 = d_ref[...].astype(F32)
        dw_ref[...] += jnp.sum(d3 * nrm, axis=0, keepdims=True)
        dn = d3 * w_ref[...]
        dy2 = r * (dn - nrm * jnp.mean(dn * nrm, axis=1, keepdims=True))
        dy_ref[...] = (dy2 * sz).astype(BF16)
        dz_ref[...] = (dy2 * yv * (sg * (1.0 + z * (1.0 - sg)))).astype(BF16)

    blk = pl.BlockSpec((tm, SSM_GROUP_WIDTH), lambda g, i: (i, g))
    vec = pl.BlockSpec((1, SSM_GROUP_WIDTH), lambda g, i: (0, g))
    return pl.pallas_call(
        body, name="gate_norm_bwd", grid=(ng, s // tm),
        in_specs=[blk, blk, blk, vec], out_specs=[blk, blk, vec],
        out_shape=[jax.ShapeDtypeStruct((s, di), BF16), jax.ShapeDtypeStruct(pzx.shape, BF16),
                   jax.ShapeDtypeStruct((1, di), F32)],
        compiler_params=_cparams(("parallel", "arbitrary")),
    )(dy3, y, pzx, norm_w)


_ANY = pl.BlockSpec(memory_space=pl.ANY)


def _place():
    x, y, c = lax.axis_index("x"), lax.axis_index("y"), lax.axis_index("c")
    chips = [(1 - x, y), (x, 1 - y), (1 - x, 1 - y)]
    return x, y, c, chips


def _cast_to_slot(x, kvec, name, after=None):
    r, cn = x.shape
    tr = _rows_per_block(r, cn)
    extra = [] if after is None else [after]

    def body(k_ref, x_ref, *rest):
        rest[-1][...] = x_ref[...].astype(BF16)

    grid_spec = pltpu.PrefetchScalarGridSpec(
        num_scalar_prefetch=1, grid=(r // tr,),
        in_specs=[pl.BlockSpec((tr, cn), lambda i, k: (i, 0))] + [_ANY] * len(extra),
        out_specs=pl.BlockSpec((None, tr, cn), lambda i, k: (k[0], i, 0)))
    return pl.pallas_call(
        body, name=name, grid_spec=grid_spec, out_shape=jax.ShapeDtypeStruct((N_CHIPS, r, cn), BF16),
        compiler_params=_cparams(("parallel",)),
    )(kvec, x, *extra)


def _swap_halves(gs, name, after=None):
    n = len(gs)
    extra = [] if after is None else [after]

    def body(*refs):
        ins, outs = refs[:n], refs[n + len(extra):2 * n + len(extra)]
        send_sems, recv_sems = refs[2 * n + len(extra):]
        x, y, c, _ = _place()
        cps = []
        for w in range(n):
            hr = gs[w].shape[1] // 2
            cp = pltpu.make_async_remote_copy(
                src_ref=ins[w].at[:, pl.ds((1 - c) * hr, hr)], dst_ref=outs[w],
                send_sem=send_sems.at[w], recv_sem=recv_sems.at[w],
                device_id=(x, y, 1 - c), device_id_type=MESH)
            cp.start()
            cps.append(cp)
        for cp in cps:
            cp.wait()

    return pl.pallas_call(
        body, name=name,
        in_specs=[_ANY] * (n + len(extra)), out_specs=[_ANY] * n,
        out_shape=[jax.ShapeDtypeStruct((g.shape[0], g.shape[1] // 2, g.shape[2]), g.dtype) for g in gs],
        scratch_shapes=[pltpu.SemaphoreType.DMA((n,)), pltpu.SemaphoreType.DMA((n,))],
    )(*gs, *extra)


def _join_halves(fs, name, after=None):
    n = len(fs)
    extra = [] if after is None else [after]

    def body(*refs):
        outs = refs[n + len(extra):2 * n + len(extra)]
        send_sems, recv_sems = refs[2 * n + len(extra):]
        x, y, c, _ = _place()

        def copy(w, hc):
            hr = fs[w].shape[0] // 2
            rows = outs[w].at[pl.ds(hc * hr, hr)]
            return pltpu.make_async_remote_copy(
                src_ref=rows, dst_ref=rows, send_sem=send_sems.at[w], recv_sem=recv_sems.at[w],
                device_id=(x, y, 1 - c), device_id_type=MESH)

        cps = [copy(w, c) for w in range(n)]
        for cp in cps:
            cp.start()
        for w in range(n):
            copy(w, 1 - c).wait_recv()
        for cp in cps:
            cp.wait_send()

    return pl.pallas_call(
        body, name=name,
        in_specs=[_ANY] * (n + len(extra)), out_specs=[_ANY] * n,
        out_shape=[jax.ShapeDtypeStruct(f.shape, f.dtype) for f in fs],
        input_output_aliases={w: w for w in range(n)},
        scratch_shapes=[pltpu.SemaphoreType.DMA((n,)), pltpu.SemaphoreType.DMA((n,))],
    )(*fs, *extra)


_HBM_SPEC = pl.BlockSpec(memory_space=pltpu.HBM)
_SEM_SPEC = pl.BlockSpec(memory_space=pltpu.SEMAPHORE)
_VMEM_SPEC = pl.BlockSpec(memory_space=pltpu.VMEM)
_EFFECT = pltpu.SideEffectType.DATAFLOW_SIDE_EFFECTING
_TOKEN = jax.ShapeDtypeStruct((8, LANES), F32)


def _hbm(a):
    return pltpu.with_memory_space_constraint(a, pltpu.HBM)


_NEIGHBOURS, _DIAGONAL, _ALL_CHIPS = (0, 1), (2,), (0, 1, 2)


def _gather_copies(bufs, refs, send_sems, recv_sems, forward, arrivals=True, which=_ALL_CHIPS):
    x, y, c, chips = _place()
    k = 2 * x + y
    out, arrive = [], []
    for w, ref in enumerate(refs):
        hr = bufs[w].shape[1] // 2
        for j, (cx, cy) in enumerate(chips):
            if j not in which:
                continue
            kj = 2 * cx + cy
            slot_out, slot_in, half_in = (kj, kj, 1 - c) if forward else (k, kj, c)
            to = (x, y, 1 - c) if forward else (cx, cy, c)
            src = ref.at[slot_out, pl.ds(c * hr, hr)]
            land = ref.at[slot_in, pl.ds(half_in * hr, hr)]
            out.append(pltpu.make_async_remote_copy(
                src_ref=src, dst_ref=src, send_sem=send_sems.at[3 * w + j], recv_sem=recv_sems.at[3 * w + j],
                device_id=to, device_id_type=MESH))
            if arrivals:
                arrive.append(pltpu.make_async_remote_copy(
                    src_ref=land, dst_ref=land, send_sem=send_sems.at[3 * w + j], recv_sem=recv_sems.at[3 * w + j],
                    device_id=to, device_id_type=MESH))
    return out, arrive


def _gather_start(bufs, forward, name, after=None, which=_ALL_CHIPS):
    n = len(bufs)
    extra = [] if after is None else [after]

    def body(*refs):
        ins = refs[:n]
        send_sems, recv_sems = refs[n + len(extra)], refs[n + len(extra) + 1]
        token = refs[-1]
        out, _ = _gather_copies(bufs, ins, send_sems, recv_sems, forward, arrivals=False, which=which)
        for cp in out:
            cp.start()
        token[...] = jnp.zeros_like(token)

    res = pl.pallas_call(
        body, name=name,
        out_shape=(pltpu.SemaphoreType.DMA((3 * n,)), pltpu.SemaphoreType.DMA((3 * n,)))
        + tuple(pltpu.HBM(b.shape, b.dtype) for b in bufs) + (_TOKEN,),
        in_specs=(_HBM_SPEC,) * n + (_ANY,) * len(extra),
        out_specs=(_SEM_SPEC, _SEM_SPEC) + (_HBM_SPEC,) * n + (_VMEM_SPEC,),
        input_output_aliases={w: 2 + w for w in range(n)},
        compiler_params=pltpu.CompilerParams(has_side_effects=_EFFECT),
    )(*[_hbm(b) for b in bufs], *extra)
    return res[0], res[1], list(res[2:2 + n]), res[-1]


def _gather_wait(bufs, send_sems, recv_sems, after, forward, name, which=_ALL_CHIPS):
    n = len(bufs)

    def body(*refs):
        ins = refs[:n]
        send_sems, recv_sems = refs[n], refs[n + 1]
        out, arrive = _gather_copies(bufs, ins, send_sems, recv_sems, forward, which=which)
        for cp in out:
            cp.wait_send()
        for cp in arrive:
            cp.wait_recv()

    res = pl.pallas_call(
        body, name=name,
        out_shape=tuple(pltpu.HBM(b.shape, b.dtype) for b in bufs),
        in_specs=(_HBM_SPEC,) * n + (_SEM_SPEC, _SEM_SPEC, _ANY), out_specs=(_HBM_SPEC,) * n,
        input_output_aliases={w: w for w in range(n)},
        compiler_params=pltpu.CompilerParams(has_side_effects=_EFFECT),
    )(*bufs, send_sems, recv_sems, after)
    return list(res)


def _scatter_copies(t_ref, land_ref, send_sems, recv_sems, arrivals=True):
    x, y, c, chips = _place()
    k = 2 * x + y
    out, arrive = [], []
    for j, (cx, cy) in enumerate(chips):
        kj = 2 * cx + cy
        out.append(pltpu.make_async_remote_copy(
            src_ref=t_ref.at[kj], dst_ref=land_ref.at[k], send_sem=send_sems.at[j], recv_sem=recv_sems.at[j],
            device_id=(cx, cy, c), device_id_type=MESH))
        if arrivals:
            arrive.append(pltpu.make_async_remote_copy(
                src_ref=t_ref.at[kj], dst_ref=land_ref.at[kj], send_sem=send_sems.at[j], recv_sem=recv_sems.at[j],
                device_id=(cx, cy, c), device_id_type=MESH))
    return out, arrive


def _scatter_start(t, name):
    def body(t_ref, land_ref, send_sems, recv_sems, t_thru, land_thru, token):
        out, _ = _scatter_copies(t_ref, land_ref, send_sems, recv_sems, arrivals=False)
        for cp in out:
            cp.start()
        token[...] = jnp.zeros_like(token)

    return pl.pallas_call(
        body, name=name,
        out_shape=(pltpu.SemaphoreType.DMA((3,)), pltpu.SemaphoreType.DMA((3,)),
                   pltpu.HBM(t.shape, t.dtype), pltpu.HBM(t.shape, t.dtype), _TOKEN),
        in_specs=(_HBM_SPEC, _HBM_SPEC), out_specs=(_SEM_SPEC, _SEM_SPEC, _HBM_SPEC, _HBM_SPEC, _VMEM_SPEC),
        input_output_aliases={0: 2, 1: 3},
        compiler_params=pltpu.CompilerParams(has_side_effects=_EFFECT),
    )(_hbm(t), _hbm(lax.empty(t.shape, t.dtype)))


def _scatter_wait(send_sems, recv_sems, t_thru, land_thru, after, name):
    def body(t_ref, land_ref, send_sems, recv_sems, after_ref, t_out, land_out):
        out, arrive = _scatter_copies(t_ref, land_ref, send_sems, recv_sems)
        for cp in out:
            cp.wait_send()
        for cp in arrive:
            cp.wait_recv()

    return pl.pallas_call(
        body, name=name,
        out_shape=(pltpu.HBM(t_thru.shape, t_thru.dtype), pltpu.HBM(land_thru.shape, land_thru.dtype)),
        in_specs=(_HBM_SPEC, _HBM_SPEC, _SEM_SPEC, _SEM_SPEC, _ANY), out_specs=(_HBM_SPEC, _HBM_SPEC),
        input_output_aliases={0: 0, 1: 1},
        compiler_params=pltpu.CompilerParams(has_side_effects=_EFFECT),
    )(t_thru, land_thru, send_sems, recv_sems, after)


def _all_gather_small(v, reduce, name):
    r, l = v.shape

    def body(v_ref, o_ref, *rest):
        if reduce:
            buf, send_sems, recv_sems = rest
        else:
            buf = o_ref
            send_sems, recv_sems = rest
        x, y, c, _ = _place()
        me = 4 * x + 2 * y + c
        buf[me] = v_ref[...]
        cps = []
        for d in range(1, N_DEV):
            peer = (x if d & 4 == 0 else 1 - x, y if d & 2 == 0 else 1 - y, c if d & 1 == 0 else 1 - c)
            cp = pltpu.make_async_remote_copy(
                src_ref=v_ref, dst_ref=buf.at[me], send_sem=send_sems.at[d - 1], recv_sem=recv_sems.at[d - 1],
                device_id=peer, device_id_type=MESH)
            cp.start()
            cps.append((cp, peer))
        for d, (cp, (px, py, pc)) in enumerate(cps, start=1):
            pltpu.make_async_remote_copy(
                src_ref=v_ref, dst_ref=buf.at[4 * px + 2 * py + pc], send_sem=send_sems.at[d - 1],
                recv_sem=recv_sems.at[d - 1], device_id=(px, py, pc), device_id_type=MESH).wait_recv()
        for cp, _ in cps:
            cp.wait_send()
        if reduce:
            acc = buf[0]
            for i in range(1, N_DEV):
                acc = acc + buf[i]
            o_ref[...] = acc

    vm = pl.BlockSpec(memory_space=pltpu.VMEM)
    out_shape = jax.ShapeDtypeStruct((r, l) if reduce else (N_DEV, r, l), F32)
    scratch = ([pltpu.VMEM((N_DEV, r, l), F32)] if reduce else []) + [
        pltpu.SemaphoreType.DMA((N_DEV - 1,)), pltpu.SemaphoreType.DMA((N_DEV - 1,))]
    return pl.pallas_call(
        body, name=name, in_specs=[vm], out_specs=vm, out_shape=out_shape, scratch_shapes=scratch,
    )(v)


_BLOCK_BYTES = 3 * 512 * 1024


def _rows_per_block(r, cn, itemsize=4):
    best = 8
    for t in range(8, r + 1, 8):
        if r % t == 0 and t * cn * itemsize <= _BLOCK_BYTES:
            best = t
    return best


def _add_sibling_half(g4, recv, cvec, name):
    ns, r, cn = g4.shape
    hr = r // 2
    tr = _rows_per_block(hr, cn)
    nrb = hr // tr

    def body(c_ref, a_ref, b_ref, o_ref):
        o_ref[...] = (a_ref[...].astype(F32) + b_ref[...].astype(F32)).astype(o_ref.dtype)

    grid_spec = pltpu.PrefetchScalarGridSpec(
        num_scalar_prefetch=1, grid=(ns, nrb),
        in_specs=[pl.BlockSpec((None, tr, cn), lambda j, i, c: (j, c[0] * nrb + i, 0)),
                  pl.BlockSpec((None, tr, cn), lambda j, i, c: (j, i, 0))],
        out_specs=pl.BlockSpec((None, tr, cn), lambda j, i, c: (j, i, 0)))
    return pl.pallas_call(
        body, name=name, grid_spec=grid_spec, out_shape=jax.ShapeDtypeStruct((ns, hr, cn), BF16),
        compiler_params=_cparams(("parallel", "parallel")),
    )(cvec, g4, recv)


def _sum_chips(r4, t4, kvec, cvec, name):
    ns, hr, cn = r4.shape
    tr = _rows_per_block(hr, cn)
    nrb = hr // tr

    def body(k_ref, c_ref, r_ref, t_ref, o_ref):
        acc = t_ref[...].astype(F32)
        for dlt in range(1, ns):
            acc = acc + r_ref[(k_ref[0] + dlt) % ns].astype(F32)
        o_ref[...] = acc

    grid_spec = pltpu.PrefetchScalarGridSpec(
        num_scalar_prefetch=2, grid=(nrb,),
        in_specs=[pl.BlockSpec((ns, tr, cn), lambda i, k, c: (0, i, 0)),
                  pl.BlockSpec((None, tr, cn), lambda i, k, c: (k[0], i, 0))],
        out_specs=pl.BlockSpec((tr, cn), lambda i, k, c: (c[0] * nrb + i, 0)))
    return pl.pallas_call(
        body, name=name, grid_spec=grid_spec, out_shape=jax.ShapeDtypeStruct((2 * hr, cn), F32),
        compiler_params=_cparams(("parallel",)),
    )(kvec, cvec, r4, t4)


def _adamw(w, g, m, v, name):
    r, cn = w.shape
    tr = _rows_per_block(r, cn)
    c1 = 1.0 - ADAM_B1 ** ADAM_STEP
    c2 = 1.0 - ADAM_B2 ** ADAM_STEP

    def body(w_ref, g_ref, m_ref, v_ref, go_ref, d_ref, mo_ref, vo_ref):
        gv = g_ref[...]
        mn = ADAM_B1 * m_ref[...] + (1.0 - ADAM_B1) * gv
        vn = ADAM_B2 * v_ref[...] + (1.0 - ADAM_B2) * (gv * gv)
        go_ref[...] = gv
        mo_ref[...] = mn
        vo_ref[...] = vn
        d_ref[...] = -ADAM_LR * ((mn / c1) / (jnp.sqrt(vn / c2) + ADAM_EPS) + ADAM_WD * w_ref[...])

    spec = pl.BlockSpec((tr, cn), lambda i: (i, 0))
    return pl.pallas_call(
        body, name=name, grid=(r // tr,), in_specs=[spec] * 4, out_specs=[spec] * 4,
        out_shape=[jax.ShapeDtypeStruct((r, cn), F32)] * 4,
        compiler_params=_cparams(("parallel",)),
    )(w, g, m, v)


def _pack(arrs):
    flat = jnp.concatenate([a.reshape(-1).astype(F32) for a in arrs])
    n = flat.shape[0]
    tot = -(-n // (8 * LANES)) * (8 * LANES)
    return jnp.pad(flat, (0, tot - n)).reshape(tot // LANES, LANES)


def _unpack(packed, shapes):
    flat = packed.reshape(-1)
    out, off = [], 0
    for shp in shapes:
        sz = int(np.prod(shp))
        out.append(flat[off:off + sz].reshape(shp))
        off += sz
    return out


class _LocalExchange:
    def __init__(self, wa4, woa4, ws4, wos4):
        self.l0, self.ssm = [wa4, woa4], [ws4, wos4]
        self.grads = {}

    def l0_begin(self):
        return self.l0[0], None

    def l0_neighbours(self, after):
        return self.l0[0]

    def l0_diagonal(self, after):
        return self.l0

    def ssm_gather_mid(self, after):
        return None

    def ssm_gather_end(self, after):
        return self.ssm

    def grad_ready(self, name, g4, after=None):
        self.grads[name] = g4
        return None

    def grad_sync(self, name, after):
        pass

    def small_grads(self, small_full):
        self.small = small_full
        return None


class _Exchange:
    def __init__(self, kvec, cvec, l0_bufs, after):
        self.kvec, self.cvec, self.l0_bufs, self.bufs, self.after = kvec, cvec, l0_bufs, None, after
        self.pending, self.summed, self.last_token, self.l0_token = {}, {}, None, None

    def l0_begin(self):
        if self.l0_token is None:
            self.l0_send, self.l0_recv, self.l0_bufs, self.l0_token = _gather_start(
                self.l0_bufs, False, "l0_gather_ici_start", self.after)
        return self.l0_bufs[0], (self.l0_token if self.bufs is None else self.sems[3])

    def _l0_step(self, which, tag, after):
        bufs = _gather_wait(self.l0_bufs, self.l0_send, self.l0_recv, after, False, "l0_gather_ici_wait_" + tag, which)
        send, recv, bufs, token = _gather_start(bufs, True, "l0_gather_fwd_start_" + tag, None, which)
        self.l0_bufs = _gather_wait(bufs, send, recv, token, True, "l0_gather_fwd_wait_" + tag, which)
        return self.l0_bufs

    def l0_neighbours(self, after):
        return self._l0_step(_NEIGHBOURS, "nb", after)[0]

    def l0_diagonal(self, after):
        return self._l0_step(_DIAGONAL, "diag", after)

    def ssm_gather_start(self, ssm_bufs):
        self.sems = _gather_start(ssm_bufs, False, "ssm_gather_ici_start", self.l0_token)
        self.bufs = self.sems[2]
        return self.sems[3]

    def ssm_gather_mid(self, after):
        bufs = _gather_wait(self.bufs, self.sems[0], self.sems[1], after, False, "ssm_gather_ici_wait")
        self.sems = _gather_start(bufs, True, "ssm_gather_fwd_start")
        self.bufs = self.sems[2]
        return self.sems[3]

    def ssm_gather_end(self, after):
        return _gather_wait(self.bufs, self.sems[0], self.sems[1], after, True, "ssm_gather_fwd_wait")

    def small_grads(self, small_full):
        packed = _all_gather_small(_pack(small_full), True, "reduce_small_grads")
        self.small = _unpack(packed, [t.shape for t in small_full])
        return packed

    def grad_ready(self, name, g4, after=None):
        recv = _swap_halves([g4], "grads_to_sibling_" + name, after)[0]
        t = _add_sibling_half(g4, recv, self.cvec, "add_sibling_" + name)
        send_sems, recv_sems, t_thru, land, token = _scatter_start(t, "scatter_start_" + name)
        self.pending[name] = (send_sems, recv_sems, t_thru, land)
        self.last_token = token
        return token

    def grad_sync(self, name, after):
        t, land = _scatter_wait(*self.pending.pop(name), after, "scatter_wait_" + name)
        self.summed[name] = _sum_chips(land, t, self.kvec, self.cvec, "sum_chips_" + name)


def _tie(vec, token):
    return vec if token is None else vec + token[0:1, 0:1].reshape((1,) * vec.ndim).astype(vec.dtype)


def _local_step(x2, tgt, ex, kvec, conv_w_f, conv_b_f, norm_w_f, rel_bias, dt_bias, a_log, d_skip,
                ln_g, ln_b):
    s, d = x2.shape
    wa4, tok = ex.l0_begin()
    d_attn = wa4.shape[2] * N_CHIPS // 10
    hpg = d_attn // HEAD_DIM
    d_inner = norm_w_f.shape[1]
    ng = d_inner // SSM_GROUP_WIDTH
    n_heads = dt_bias.shape[1]
    conv_dim = conv_w_f.shape[1]
    assert n_heads == ng * HEADS_PER_SSM_GROUP and conv_dim == d_inner + 2 * ng * D_STATE

    x3 = _cast_x3(x2, "cast_x", tok)
    for g, (_, dil) in enumerate(ATTN_PATTERNS):
        if dil > 1:
            x3 = _class_copy(x3, g, dil, f"class_order_x_g{g}")
    xb = x3[0]
    buckets = _bucket_tiles()
    bias = _bias_expand(rel_bias, buckets, hpg)
    pa = _in_proj_shard(x3, wa4, kvec, 0, d_attn, "mm_in_attn_own", after=tok)
    wa4 = ex.l0_neighbours(pa)
    pa = _in_proj_shard(x3, wa4, kvec, 1, d_attn, "mm_in_attn_nb1", into=pa)
    pa = _in_proj_shard(x3, wa4, kvec, 2, d_attn, "mm_in_attn_nb2", into=pa)
    wa4, woa4 = ex.l0_diagonal(pa)
    pa = _in_proj_shard(x3, wa4, kvec, 3, d_attn, "mm_in_attn_diag", into=pa)
    og, lg = [], []
    for g, (_, dil) in enumerate(ATTN_PATTERNS):
        o_, l_ = _attn_fwd(pa, bias, g, dil, hpg)
        og.append(o_)
        lg.append(l_)
    o, lse, yat = _attn_combine(og, lg, pa, hpg)
    h0 = _mm_nn_sharded(yat, woa4, F32, "mm_out_attn", after=ex.ssm_gather_mid(yat))
    g0, b0, g1, b1 = ln_g[0:1], ln_b[0:1], ln_g[1:2], ln_b[1:2]
    xhat0, rstd0, x1b = _ln_fwd(x2, h0, g0, b0, "ln0_fwd")

    wst4, wos4 = ex.ssm_gather_end(x1b)
    wst = wst4.reshape(N_CHIPS * wst4.shape[1], d)
    nzx = d_inner + conv_dim
    wos = wos4.reshape(d_inner, d)
    pzx = _mm_nt(x1b, wst, BF16, "mm_in_ssm", n=nzx)
    dt_raw = _mm_nt(x1b, wst, F32, "mm_in_dt", n=n_heads, b_row_off=nzx)

    def pad_heads(t):
        t = t.reshape(t.shape[0], ng, HEADS_PER_SSM_GROUP).transpose(1, 0, 2)
        return jnp.pad(t, ((0, 0), (0, 0), (0, LANES - HEADS_PER_SSM_GROUP)))

    def unpad_heads(t):
        return t[:, :, :HEADS_PER_SSM_GROUP].transpose(1, 0, 2).reshape(t.shape[1], n_heads)

    dtp = pad_heads(dt_raw)
    alog_p, dtb_p = pad_heads(a_log), pad_heads(dt_bias)
    dsk_e = jnp.repeat(d_skip.reshape(ng, 1, HEADS_PER_SSM_GROUP), SSM_HEAD_DIM, axis=2)
    e = _expand_matrix()
    xbc = _conv_fwd(pzx, conv_w_f, conv_b_f, d_inner)
    y_ssd, states = _ssd_fwd(xbc, dtp, alog_p, dtb_p, dsk_e, e, d_inner)
    y3 = _gate_norm_fwd(y_ssd, pzx, norm_w_f)
    h1 = _mm_nn(y3, wos, F32, "mm_out_ssm")
    xhat1, rstd1, dy2, row_sq = _ln_fwd(xhat0, h1, g1, b1, "ln1_fwd_loss", affine_in=(g0, b0), target=tgt)
    loss_local = 0.5 * jnp.sum(row_sq) / d

    du1, du1b, dg1, db1 = _ln_bwd(dy2, xhat1, rstd1, g1, "ln1_bwd")
    dy3 = _mm_nt(du1b, wos, BF16, "mm_d_y3")
    g_wos = _mm_tn(y3, du1b, BF16, "mm_g_w_out_ssm").reshape(N_CHIPS, d_inner // N_CHIPS, d)
    norm_w_t = _tie(norm_w_f, ex.grad_ready("w_out_ssm", g_wos))
    dy_ssd, dz, d_nw = _gate_norm_bwd(dy3, y_ssd, pzx, norm_w_t)
    dxs, dbm, dcm, ddtp, d_alog, d_dtb, d_dsk = _ssd_bwd(xbc, dtp, alog_p, dtb_p, dsk_e, e, states, dy_ssd, d_inner)
    dpre, d_cw, d_cb = _conv_bwd_a(pzx, jnp.concatenate([dxs, dbm, dcm], axis=1), conv_w_f, conv_b_f, d_inner)
    dpzx = _conv_bwd_b(dpre, conv_w_f, dz, d_inner)
    ddt_raw = unpad_heads(ddtp)
    t1 = _mm_nn(ddt_raw, wst, F32, "mm_d_x1_dt", b_row_off=nzx, add=du1, add_scale=DEEPNORM_ALPHA)
    dx1 = _mm_nn(dpzx, wst, F32, "mm_d_x1", add=t1)
    ex.grad_sync("w_out_ssm", dx1)
    g_wst = _mm_tn(dpzx, x1b, BF16, "mm_g_w_in_ssm", out_rows=wst.shape[0])
    g_wst = _mm_tn(ddt_raw, x1b, BF16, "mm_g_w_dt", out_rows=wst.shape[0], out_row_off=nzx, into=g_wst)
    g0_t = _tie(g0, ex.grad_ready("w_in_ssm", g_wst.reshape(wst4.shape)))

    du0, du0b, dg0, db0 = _ln_bwd(dx1, xhat0, rstd0, g0_t, "ln0_bwd")
    dyat = _mm_nt_sharded_k(du0b, woa4, BF16, "mm_d_yat")
    g_woa = _mm_tn(yat, du0b, BF16, "mm_g_w_out_attn", shard_cols=d // N_CHIPS)
    tok_woa = ex.grad_ready("w_out_attn", g_woa)
    do, delta, dgate = _attn_pre_bwd(dyat, o, pa, hpg)
    pieces, dbt = [], []
    for g, (_, dil) in enumerate(ATTN_PATTERNS):
        dq, dk, dv, db_ = _attn_bwd(pa, bias, do, lse, delta, g, dil, hpg)
        pieces += [dq, dk, dv]
        dbt.append(db_)
    dpa = jnp.concatenate(pieces + [dgate], axis=1)
    g_wa = _mm_tn(xb, dpa, BF16, "mm_g_w_in_attn", shard_cols=wa4.shape[2], after=tok_woa)
    ex.grad_sync("w_in_ssm", g_wa)
    ex.grad_sync("w_out_attn", g_wa)
    d_rel = _bias_reduce(jnp.stack(dbt), buckets, hpg)[:, :, 0].T
    d_dsk_h = d_dsk.reshape(n_heads, SSM_HEAD_DIM).sum(axis=1)
    small_full = [d_rel, d_cw, d_cb, unpad_heads(d_dtb), unpad_heads(d_alog), d_dsk_h[None], d_nw,
                  jnp.concatenate([dg0, dg1], axis=0), jnp.concatenate([db0, db1], axis=0)]
    tok_wa = ex.grad_ready("w_in_attn", g_wa, after=ex.small_grads(small_full))
    grad_x = _mm_nt_sharded_k(dpa, wa4, F32, "mm_d_x0", add=du0, add_scale=DEEPNORM_ALPHA, after=tok_wa)
    return loss_local, grad_x[None]


def kernel(x, w_in_attn, w_out_attn, rel_bias, w_in_ssm, conv_w, conv_b, dt_bias, a_log, d_skip, ssm_norm_w, w_out_ssm, ln_g, ln_b, loss_target, m_w_in_attn, m_w_out_attn, m_rel_bias, m_w_in_ssm, m_conv_w, m_conv_b, m_dt_bias, m_a_log, m_d_skip, m_ssm_norm_w, m_w_out_ssm, m_ln_g, m_ln_b, v_w_in_attn, v_w_out_attn, v_rel_bias, v_w_in_ssm, v_conv_w, v_conv_b, v_dt_bias, v_a_log, v_d_skip, v_ssm_norm_w, v_w_out_ssm, v_ln_g, v_ln_b):
    xi, yi, ci = lax.axis_index("x"), lax.axis_index("y"), lax.axis_index("c")
    chip = 2 * xi + yi
    cvec = jnp.reshape(ci, (1,)).astype(jnp.int32)
    kvec = jnp.reshape(chip, (1,)).astype(jnp.int32)

    cw_l, cb_l, nw_l = conv_w[0], conv_b[0], ssm_norm_w[0]
    vec_shapes = [cw_l.shape, cb_l.shape, nw_l.shape]
    vec_all = _all_gather_small(_pack([cw_l, cb_l, nw_l]), False, "gather_vectors")
    l0 = [_cast_to_slot(w_in_attn[0], kvec, "cast_w_in_attn"), _cast_to_slot(w_out_attn[0], kvec, "cast_w_out_attn")]
    ex = _Exchange(kvec, cvec, l0, after=vec_all)
    _, tok0 = ex.l0_begin()
    tok1 = ex.ssm_gather_start([_cast_to_slot(w_in_ssm[0].T, kvec, "cast_w_in_ssm", tok0),
                                _cast_to_slot(w_out_ssm[0], kvec, "cast_w_out_ssm", tok0)])
    parts = [_unpack(vec_all[2 * j], vec_shapes) for j in range(N_CHIPS)]
    conv_w_f = jnp.concatenate([p[0] for p in parts], axis=1)
    conv_b_f = jnp.concatenate([p[1] for p in parts], axis=0)[None]
    norm_w_f = jnp.concatenate([p[2] for p in parts], axis=0)[None]

    loss_local, grad_x = _local_step(
        x[0], loss_target[0], ex, kvec, conv_w_f, conv_b_f, norm_w_f, rel_bias, dt_bias, a_log,
        d_skip, ln_g, ln_b)
    loss = lax.psum(loss_local, ("x", "y", "c"))

    big_w = dict(w_in_attn=(w_in_attn, m_w_in_attn, v_w_in_attn), w_out_attn=(w_out_attn, m_w_out_attn, v_w_out_attn),
                 w_in_ssm=(w_in_ssm, m_w_in_ssm, v_w_in_ssm), w_out_ssm=(w_out_ssm, m_w_out_ssm, v_w_out_ssm))
    big = {}

    def finish(names, join_name, after):
        last = None
        for nm, gf in zip(names, _join_halves([ex.summed[nm] for nm in names], join_name, after)):
            flip = (lambda t: t.T) if nm == "w_in_ssm" else (lambda t: t)
            w_, m_, v_ = (flip(t[0]) for t in big_w[nm])
            res = _adamw(w_, gf, m_, v_, "adamw_" + nm)
            big[nm] = [flip(t)[None] for t in res]
            last = res[3]
        return last

    last = finish(["w_out_ssm", "w_in_ssm", "w_out_attn"], "grads_join_halves_a", ex.last_token)
    ex.grad_sync("w_in_attn", last)
    finish(["w_in_attn"], "grads_join_halves_b", None)

    s_rel, s_cw, s_cb, s_dtb, s_alog, s_dsk, s_nw, s_lng, s_lnb = ex.small
    cwc, nwc = conv_w.shape[2], ssm_norm_w.shape[1]
    s_cw = lax.dynamic_slice_in_dim(s_cw, chip * cwc, cwc, axis=1)[None]
    s_cb = lax.dynamic_slice_in_dim(s_cb, chip * cwc, cwc, axis=1)
    s_nw = lax.dynamic_slice_in_dim(s_nw, chip * nwc, nwc, axis=1)
    small_names = ["rel_bias", "conv_w", "conv_b", "dt_bias", "a_log", "d_skip", "ssm_norm_w", "ln_g", "ln_b"]
    small_g = [s_rel, s_cw, s_cb, s_dtb, s_alog, s_dsk, s_nw, s_lng, s_lnb]
    small_w = [rel_bias, conv_w, conv_b, dt_bias, a_log, d_skip, ssm_norm_w, ln_g, ln_b]
    small_m = [m_rel_bias, m_conv_w, m_conv_b, m_dt_bias, m_a_log, m_d_skip, m_ssm_norm_w, m_ln_g, m_ln_b]
    small_v = [v_rel_bias, v_conv_w, v_conv_b, v_dt_bias, v_a_log, v_d_skip, v_ssm_norm_w, v_ln_g, v_ln_b]
    shapes = [t.shape for t in small_w]
    res = _adamw(_pack(small_w), _pack(small_g), _pack(small_m), _pack(small_v), "adamw_small")
    small = {nm: [] for nm in small_names}
    for packed in res:
        for nm, t in zip(small_names, _unpack(packed, shapes)):
            small[nm].append(t)

    order = ["w_in_attn", "w_out_attn", "rel_bias", "w_in_ssm", "conv_w", "conv_b", "dt_bias", "a_log",
             "d_skip", "ssm_norm_w", "w_out_ssm", "ln_g", "ln_b"]
    table = {**big, **small}
    outs = [loss, grad_x]
    for kind in range(4):
        outs += [table[nm][kind] for nm in order]
    return tuple(outs)
```

```python
import functools
import math

import numpy as np
import jax
import jax.numpy as jnp
from jax import lax
from jax.experimental import pallas as pl
from jax.experimental.pallas import tpu as pltpu

F32 = jnp.float32
BF16 = jnp.bfloat16
MESH = pl.DeviceIdType.MESH

ATTN_PATTERNS = ((128, 1), (512, 4), (2048, 16))
N_GROUPS_ATTN = 3
HEAD_DIM = 128
ATTN_BLOCK = 128
NUM_BUCKETS = 32
MAX_DISTANCE = 2048
SSM_HEAD_DIM = 64
HEADS_PER_SSM_GROUP = 16
SSM_GROUP_WIDTH = HEADS_PER_SSM_GROUP * SSM_HEAD_DIM
D_STATE = 128
CONV_WIDTH = 4
CHUNK = 128
DEPTH = 2
DEEPNORM_ALPHA = (2 * DEPTH) ** 0.25
LN_EPS = 1e-5
RMS_EPS = 1e-5
NEG_INF = -1e30
ADAM_LR = 0.001
ADAM_B1 = 0.9
ADAM_B2 = 0.999
ADAM_EPS = 1e-08
ADAM_WD = 0.01
ADAM_STEP = 10

N_CHIPS = 4
N_DEV = 8

VMEM_LIMIT_V7X = 56 * 1024 * 1024
LANES = 128


def _cparams(sem=None):
    return pltpu.CompilerParams(dimension_semantics=sem, vmem_limit_bytes=VMEM_LIMIT_V7X)


def _sigmoid(x):
    return 0.5 * jnp.tanh(0.5 * x) + 0.5


def _dot(a, b):
    return jnp.dot(a, b, preferred_element_type=F32)


def _dot_nt(a, b):
    return lax.dot_general(a, b, (((1,), (1,)), ((), ())), preferred_element_type=F32)


def _dot_tn(a, b):
    return lax.dot_general(a, b, (((0,), (0,)), ((), ())), preferred_element_type=F32)


def _split2(x):
    hi = x.astype(BF16)
    lo = (x - hi.astype(F32)).astype(BF16)
    return hi, lo


def _split3(x):
    hi = x.astype(BF16)
    r = x - hi.astype(F32)
    mid = r.astype(BF16)
    lo = (r - mid.astype(F32)).astype(BF16)
    return hi, mid, lo


def _matmul(a, b, *, mode, grid, a_spec, b_spec, out_shape, out_spec, tile, name,
            add=None, add_spec=None, add_scale=1.0, after=None, into=None):
    nk = grid[2]
    tm, tn = tile
    dot = {"nn": _dot, "nt": _dot_nt, "tn": _dot_tn}[mode]
    has_add = add is not None
    has_after = after is not None
    has_into = into is not None

    def finish(r, add_ref, o_ref):
        if has_add:
            r = r + add_scale * add_ref[...].astype(F32)
        o_ref[...] = r.astype(o_ref.dtype)

    def body_one(*refs):
        a_ref, b_ref = refs[:2]
        finish(dot(a_ref[...].astype(BF16), b_ref[...].astype(BF16)), refs[2] if has_add else None, refs[-1])

    def body_acc(*refs):
        a_ref, b_ref = refs[:2]
        add_ref = refs[2] if has_add else None
        o_ref, acc_ref = refs[-2:]
        k = pl.program_id(2)

        @pl.when(k == 0)
        def _():
            acc_ref[...] = jnp.zeros_like(acc_ref)

        acc_ref[...] += dot(a_ref[...].astype(BF16), b_ref[...].astype(BF16))

        @pl.when(k == nk - 1)
        def _():
            finish(acc_ref[...], add_ref, o_ref)

    in_specs = ([a_spec, b_spec] + ([add_spec] if has_add else []) + ([_ANY] if has_after else [])
                + ([_ANY] if has_into else []))
    args = (a, b) + ((add,) if has_add else ()) + ((after,) if has_after else ()) + ((into,) if has_into else ())
    return pl.pallas_call(
        body_one if nk == 1 else body_acc, name=name, grid=grid, in_specs=in_specs, out_specs=out_spec,
        out_shape=out_shape,
        input_output_aliases={len(args) - 1: 0} if has_into else {},
        scratch_shapes=[] if nk == 1 else [pltpu.VMEM((tm, tn), F32)],
        compiler_params=_cparams(("parallel", "parallel", "arbitrary")),
    )(*args)


def _pick(n, pref):
    for t in pref:
        if n % t == 0:
            return t
    return n


_TILE_PREF = (1024, 512, 256, 128)
_K_TILE_PREF = (2048,) + _TILE_PREF


def _k_tile(k, out_dtype, has_add):
    return _pick(k, _K_TILE_PREF if (has_add or out_dtype != BF16) else (4096,) + _K_TILE_PREF)


def _mm_nn_sharded(a, w4, out_dtype, name, after=None, col_off=0, n=None, classes=1):
    m, k = a.shape
    _, _, nn = w4.shape
    n = N_CHIPS * nn if n is None else n
    tm, tk = _pick(m // classes, _TILE_PREF), _k_tile(k, out_dtype, False)
    tn = _pick(math.gcd(math.gcd(nn, n), col_off) if col_off else math.gcd(nn, n), _TILE_PREF)
    npb = nn // tn
    co = col_off // tn
    bpc, kb = m // classes // tm, k // tk
    av = a.reshape(m // classes, classes * k)
    out_shape = jax.ShapeDtypeStruct((m, n), out_dtype)
    if tm < _TILE_PREF[0]:
        return _matmul(
            av, w4, mode="nn", grid=(n // tn, m // tm, 1), tile=(tm, tn), name=name,
            a_spec=pl.BlockSpec((tm, k), lambda j, i, kk: (i % bpc, i // bpc)),
            b_spec=pl.BlockSpec((None, k, tn), lambda j, i, kk: ((j + co) // npb, 0, (j + co) % npb)),
            out_shape=out_shape, out_spec=pl.BlockSpec((tm, tn), lambda j, i, kk: (i, j)), after=after)
    return _matmul(
        av, w4, mode="nn", grid=(m // tm, n // tn, kb), tile=(tm, tn), name=name,
        a_spec=pl.BlockSpec((tm, tk), lambda i, j, kk: (i % bpc, (i // bpc) * kb + kk)),
        b_spec=pl.BlockSpec((None, tk, tn), lambda i, j, kk: ((j + co) // npb, kk, (j + co) % npb)),
        out_shape=out_shape, out_spec=pl.BlockSpec((tm, tn), lambda i, j, kk: (i, j)), after=after)


def _mm_nn(a, b, out_dtype, name, b_row_off=0, add=None, add_scale=1.0):
    m, k = a.shape
    _, n = b.shape
    tm, tk, tn = _pick(m, _TILE_PREF), _k_tile(k, out_dtype, add is not None), _pick(n, _TILE_PREF)
    assert b_row_off % tk == 0
    ko = b_row_off // tk
    return _matmul(
        a, b, mode="nn", grid=(m // tm, n // tn, k // tk), tile=(tm, tn), name=name,
        a_spec=pl.BlockSpec((tm, tk), lambda i, j, kk: (i, kk)),
        b_spec=pl.BlockSpec((tk, tn), lambda i, j, kk: (kk + ko, j)),
        out_shape=jax.ShapeDtypeStruct((m, n), out_dtype),
        out_spec=pl.BlockSpec((tm, tn), lambda i, j, kk: (i, j)),
        add=add, add_spec=pl.BlockSpec((tm, tn), lambda i, j, kk: (i, j)), add_scale=add_scale)


def _mm_nt(a, b, out_dtype, name, add=None, add_scale=1.0, n=None, b_row_off=0):
    m, k = a.shape
    n = b.shape[0] if n is None else n
    tm, tk, tn = _pick(m, _TILE_PREF), _k_tile(k, out_dtype, add is not None), _pick(n, _TILE_PREF)
    assert b_row_off % tn == 0
    no = b_row_off // tn
    return _matmul(
        a, b, mode="nt", grid=(m // tm, n // tn, k // tk), tile=(tm, tn), name=name,
        a_spec=pl.BlockSpec((tm, tk), lambda i, j, kk: (i, kk)),
        b_spec=pl.BlockSpec((tn, tk), lambda i, j, kk: (j + no, kk)),
        out_shape=jax.ShapeDtypeStruct((m, n), out_dtype),
        out_spec=pl.BlockSpec((tm, tn), lambda i, j, kk: (i, j)),
        add=add, add_spec=pl.BlockSpec((tm, tn), lambda i, j, kk: (i, j)), add_scale=add_scale)


def _mm_nt_sharded_k(a, w4, out_dtype, name, add=None, add_scale=1.0, after=None):
    m, _ = a.shape
    _, n, kn = w4.shape
    tm, tk, tn = _pick(m, _TILE_PREF), _pick(kn, (2560,) + _TILE_PREF), _pick(n, _TILE_PREF)
    kpb = kn // tk
    return _matmul(
        a, w4, mode="nt", grid=(m // tm, n // tn, N_CHIPS * kpb), tile=(tm, tn), name=name,
        a_spec=pl.BlockSpec((tm, tk), lambda i, j, kk: (i, kk)),
        b_spec=pl.BlockSpec((None, tn, tk), lambda i, j, kk: (kk // kpb, j, kk % kpb)),
        out_shape=jax.ShapeDtypeStruct((m, n), out_dtype),
        out_spec=pl.BlockSpec((tm, tn), lambda i, j, kk: (i, j)),
        add=add, add_spec=pl.BlockSpec((tm, tn), lambda i, j, kk: (i, j)), add_scale=add_scale, after=after)


def _mm_tn(a, b, out_dtype, name, shard_cols=None, out_rows=None, out_row_off=0, into=None, after=None):
    k, m = a.shape
    _, n = b.shape
    nn = n if shard_cols is None else shard_cols
    tm, tk, tn = _pick(m, _TILE_PREF), _k_tile(k, out_dtype, False), _pick(nn, _TILE_PREF)
    if shard_cols is None:
        assert out_row_off % tm == 0
        ro = out_row_off // tm
        out_shape = jax.ShapeDtypeStruct((m if out_rows is None else out_rows, n), out_dtype)
        out_spec = pl.BlockSpec((tm, tn), lambda i, j, kk: (i + ro, j))
    else:
        npb = nn // tn
        out_shape = jax.ShapeDtypeStruct((n // nn, m, nn), out_dtype)
        out_spec = pl.BlockSpec((None, tm, tn), lambda i, j, kk: (j // npb, i, j % npb))
    return _matmul(
        a, b, mode="tn", grid=(m // tm, n // tn, k // tk), tile=(tm, tn), name=name,
        a_spec=pl.BlockSpec((tk, tm), lambda i, j, kk: (kk, i)),
        b_spec=pl.BlockSpec((tk, tn), lambda i, j, kk: (kk, j)),
        out_shape=out_shape, out_spec=out_spec, into=into, after=after)


def _cast_bf16(x, name, after=None):
    r, c = x.shape
    tr = _pick(r, (512, 256, 128, 8))
    extra = [] if after is None else [after]

    def body(x_ref, *rest):
        rest[-1][...] = x_ref[...].astype(BF16)

    return pl.pallas_call(
        body, name=name, grid=(r // tr,),
        in_specs=[pl.BlockSpec((tr, c), lambda i: (i, 0))] + [_ANY] * len(extra),
        out_specs=pl.BlockSpec((tr, c), lambda i: (i, 0)),
        out_shape=jax.ShapeDtypeStruct((r, c), BF16),
        compiler_params=_cparams(("parallel",)),
    )(x, *extra)


def _bucket_tiles():
    qi = np.arange(ATTN_BLOCK)[:, None]
    ki = np.arange(2 * ATTN_BLOCK)[None, :]
    delta = np.clip(ATTN_BLOCK + qi - ki, 0, None)
    tiles = []
    max_exact = NUM_BUCKETS // 2
    for _, dil in ATTN_PATTERNS:
        dist = (delta * dil).astype(np.int32)
        d_f = np.maximum(dist, 1).astype(np.float32)
        large = max_exact + (np.log(d_f / np.float32(max_exact)) / np.float32(math.log(MAX_DISTANCE / max_exact))
                             * np.float32(NUM_BUCKETS - max_exact)).astype(np.int32)
        large = np.minimum(large, NUM_BUCKETS - 1)
        tiles.append(np.where(dist < max_exact, dist, large).astype(np.int32))
    return jnp.asarray(np.stack(tiles))


def _bias_expand(rel_bias, buckets, hpg):
    def body(tab_ref, bk_ref, o_ref):
        g, h = pl.program_id(0), pl.program_id(1)
        bk = bk_ref[...]
        acc = jnp.zeros((ATTN_BLOCK, 2 * ATTN_BLOCK), F32)
        for b in range(NUM_BUCKETS):
            acc = jnp.where(bk == b, tab_ref[b, g * hpg + h], acc)
        o_ref[...] = acc

    return pl.pallas_call(
        body, name="bias_expand", grid=(N_GROUPS_ATTN, hpg),
        in_specs=[pl.BlockSpec(memory_space=pltpu.SMEM),
                  pl.BlockSpec((None, ATTN_BLOCK, 2 * ATTN_BLOCK), lambda g, h: (g, 0, 0))],
        out_specs=pl.BlockSpec((None, None, ATTN_BLOCK, 2 * ATTN_BLOCK), lambda g, h: (g, h, 0, 0)),
        out_shape=jax.ShapeDtypeStruct((N_GROUPS_ATTN, hpg, ATTN_BLOCK, 2 * ATTN_BLOCK), F32),
        compiler_params=_cparams(("parallel", "parallel")),
    )(rel_bias, buckets)


def _bias_reduce(dtiles, buckets, hpg):
    def body(t_ref, bk_ref, o_ref):
        bk = bk_ref[...]
        t = t_ref[...]
        rows = lax.broadcasted_iota(jnp.int32, (NUM_BUCKETS, LANES), 0)
        acc = jnp.zeros((NUM_BUCKETS, LANES), F32)
        for b in range(NUM_BUCKETS):
            s = jnp.sum(jnp.sum(jnp.where(bk == b, t, 0.0), axis=1, keepdims=True), axis=0, keepdims=True)
            acc = jnp.where(rows == b, s, acc)
        o_ref[...] = acc

    return pl.pallas_call(
        body, name="bias_reduce", grid=(N_GROUPS_ATTN, hpg),
        in_specs=[pl.BlockSpec((None, None, ATTN_BLOCK, 2 * ATTN_BLOCK), lambda g, h: (g, h, 0, 0)),
                  pl.BlockSpec((None, ATTN_BLOCK, 2 * ATTN_BLOCK), lambda g, h: (g, 0, 0))],
        out_specs=pl.BlockSpec((None, NUM_BUCKETS, LANES), lambda g, h: (g * hpg + h, 0, 0)),
        out_shape=jax.ShapeDtypeStruct((N_GROUPS_ATTN * hpg, NUM_BUCKETS, LANES), F32),
        compiler_params=_cparams(("parallel", "parallel")),
    )(dtiles, buckets)


def _cast_x3(x, name, after=None):
    r, c = x.shape
    tr = _pick(r, (512, 256, 128, 8))
    extra = [] if after is None else [after]

    def body(x_ref, *rest):
        rest[-1][...] = x_ref[...].astype(BF16)

    return pl.pallas_call(
        body, name=name, grid=(r // tr,),
        in_specs=[pl.BlockSpec((tr, c), lambda i: (i, 0))] + [_ANY] * len(extra),
        out_specs=pl.BlockSpec((None, tr, c), lambda i: (0, i, 0)),
        out_shape=jax.ShapeDtypeStruct((N_GROUPS_ATTN, r, c), BF16),
        compiler_params=_cparams(("parallel",)),
    )(x, *extra)


def _class_copy(x3, slot, dil, name):
    _, s, d = x3.shape
    rows = s // dil
    tm = _pick(rows, (512, 256, 128))
    nbk = rows // tm

    def body(v_ref, x3_ref, o_ref):
        o_ref[...] = v_ref[...]

    return pl.pallas_call(
        body, name=name, grid=(dil, nbk),
        in_specs=[pl.BlockSpec((tm, d), lambda r, i: (i, r)), _ANY],
        out_specs=pl.BlockSpec((None, tm, d), lambda r, i: (slot, r * nbk + i, 0)),
        out_shape=jax.ShapeDtypeStruct(x3.shape, x3.dtype), input_output_aliases={1: 0},
        compiler_params=_cparams(("parallel", "parallel")),
    )(x3[0].reshape(rows, dil * d), x3)


def _in_proj_shard(x3, wa4, kvec, p, d_attn, name, into=None, after=None):
    _, s, d = x3.shape
    _, _, nn = wa4.shape
    tm = _pick(s, _TILE_PREF)
    tn = _pick(math.gcd(nn, 3 * d_attn), _TILE_PREF)
    npb, bpg = nn // tn, 3 * d_attn // tn
    extra = ([] if after is None else [after]) + ([] if into is None else [into])

    def block(k, j):
        return jnp.bitwise_xor(k[0], p) * npb + j

    def slot(k, j):
        jb = block(k, j)
        return jnp.where(jb < N_GROUPS_ATTN * bpg, jb // bpg, 0)

    def body(k_ref, a_ref, b_ref, *rest):
        rest[-1][...] = _dot(a_ref[...], b_ref[...]).astype(BF16)

    grid_spec = pltpu.PrefetchScalarGridSpec(
        num_scalar_prefetch=1, grid=(s // tm, npb),
        in_specs=[pl.BlockSpec((None, tm, d), lambda i, j, k: (slot(k, j), i, 0)),
                  pl.BlockSpec((None, d, tn), lambda i, j, k: (jnp.bitwise_xor(k[0], p), 0, j))]
        + [_ANY] * len(extra),
        out_specs=pl.BlockSpec((tm, tn), lambda i, j, k: (i, block(k, j))))
    return pl.pallas_call(
        body, name=name, grid_spec=grid_spec, out_shape=jax.ShapeDtypeStruct((s, N_CHIPS * nn), BF16),
        input_output_aliases={} if into is None else {2 + len(extra): 0},
        compiler_params=_cparams(("parallel", "parallel")),
    )(kvec, x3, wa4, *extra)


def _attn_valid(n_is_first):
    qi = lax.broadcasted_iota(jnp.int32, (ATTN_BLOCK, 2 * ATTN_BLOCK), 0)
    ki = lax.broadcasted_iota(jnp.int32, (ATTN_BLOCK, 2 * ATTN_BLOCK), 1)
    delta = ATTN_BLOCK + qi - ki
    band = (delta >= 0) & (delta <= ATTN_BLOCK)
    return band & (jnp.logical_not(n_is_first) | (ki >= ATTN_BLOCK))


def _attn_fwd(pg, bias, g, dil, hpg):
    s = pg.shape[0]
    w = hpg * HEAD_DIM
    rows = s // dil
    nb = rows // ATTN_BLOCK
    scale = HEAD_DIM ** -0.5

    def body(q_ref, kc_ref, kp_ref, vc_ref, vp_ref, bias_ref, o_ref, lse_ref):
        valid = _attn_valid(pl.program_id(1) == 0)
        lane = lax.broadcasted_iota(jnp.int32, (ATTN_BLOCK, LANES), 1)
        lse = jnp.zeros((ATTN_BLOCK, LANES), F32)
        for h in range(hpg):
            sl = slice(h * HEAD_DIM, (h + 1) * HEAD_DIM)
            k2 = jnp.concatenate([kp_ref[:, sl], kc_ref[:, sl]], axis=0)
            v2 = jnp.concatenate([vp_ref[:, sl], vc_ref[:, sl]], axis=0)
            sc = _dot_nt(q_ref[:, sl], k2) * scale + bias_ref[h]
            sc = jnp.where(valid, sc, NEG_INF)
            m = jnp.max(sc, axis=1, keepdims=True)
            p = jnp.exp(sc - m)
            l = jnp.sum(p, axis=1, keepdims=True)
            o_ref[:, sl] = (_dot(p.astype(BF16), v2) * (1.0 / l)).astype(BF16)
            lse = jnp.where(lane == h, m + jnp.log(l), lse)
        lse_ref[...] = lse

    def col(off):
        return lambda r, n: (r * nb + n, 3 * g + off)

    def colp(off):
        return lambda r, n: (r * nb + jnp.maximum(n - 1, 0), 3 * g + off)

    blk = (ATTN_BLOCK, w)
    tok = pl.BlockSpec(blk, lambda r, n: (n, r))
    tok1 = pl.BlockSpec((ATTN_BLOCK, LANES), lambda r, n: (n, r))
    o, lse = pl.pallas_call(
        body, name=f"attn_fwd_g{g}", grid=(dil, nb),
        in_specs=[pl.BlockSpec(blk, col(0)), pl.BlockSpec(blk, col(1)), pl.BlockSpec(blk, colp(1)),
                  pl.BlockSpec(blk, col(2)), pl.BlockSpec(blk, colp(2)),
                  pl.BlockSpec((None, hpg, ATTN_BLOCK, 2 * ATTN_BLOCK), lambda r, n: (g, 0, 0, 0))],
        out_specs=[tok, tok1],
        out_shape=[jax.ShapeDtypeStruct((rows, dil * w), BF16), jax.ShapeDtypeStruct((rows, dil * LANES), F32)],
        compiler_params=_cparams(("parallel", "parallel")),
    )(pg, pg, pg, pg, pg, bias)
    return o.reshape(s, w), lse.reshape(s, LANES)


def _attn_combine(os_, lses, pa, hpg):
    s, w = os_[0].shape
    gate_blk = pa.shape[1] // w - 1
    tm = _pick(s, (256, 128))

    def body(o0, o1, o2, l0, l1, l2, gate_ref, o_ref, lse_ref, y_ref):
        a0, a1, a2 = l0[...], l1[...], l2[...]
        m = jnp.maximum(jnp.maximum(a0, a1), a2)
        e0, e1, e2 = jnp.exp(a0 - m), jnp.exp(a1 - m), jnp.exp(a2 - m)
        den = e0 + e1 + e2
        inv = 1.0 / den
        w0, w1, w2 = e0 * inv, e1 * inv, e2 * inv
        lse_ref[...] = m + jnp.log(den)
        for h in range(hpg):
            sl = slice(h * HEAD_DIM, (h + 1) * HEAD_DIM)
            o = (w0[:, h:h + 1] * o0[:, sl].astype(F32) + w1[:, h:h + 1] * o1[:, sl].astype(F32)
                 + w2[:, h:h + 1] * o2[:, sl].astype(F32))
            gate = gate_ref[:, sl].astype(F32)
            o_ref[:, sl] = o.astype(BF16)
            y_ref[:, sl] = (o * (gate * _sigmoid(gate))).astype(BF16)

    spec = pl.BlockSpec((tm, w), lambda i: (i, 0))
    spec1 = pl.BlockSpec((tm, LANES), lambda i: (i, 0))
    return pl.pallas_call(
        body, name="attn_combine", grid=(s // tm,),
        in_specs=[spec] * 3 + [spec1] * 3 + [pl.BlockSpec((tm, w), lambda i: (i, gate_blk))],
        out_specs=[spec, spec1, spec],
        out_shape=[jax.ShapeDtypeStruct((s, w), BF16), jax.ShapeDtypeStruct((s, LANES), F32),
                   jax.ShapeDtypeStruct((s, w), BF16)],
        compiler_params=_cparams(("parallel",)),
    )(*os_, *lses, pa)


def _attn_pre_bwd(dy, o, pa, hpg):
    s, w = dy.shape
    gate_blk = pa.shape[1] // w - 1
    tm = _pick(s, (256, 128))

    def body(dy_ref, o_ref, gate_ref, do_ref, dl_ref, dg_ref):
        gate = gate_ref[...].astype(F32)
        sg = _sigmoid(gate)
        dyv = dy_ref[...].astype(F32)
        ov = o_ref[...].astype(F32)
        do = dyv * (gate * sg)
        do_ref[...] = do.astype(BF16)
        dg_ref[...] = (dyv * ov * (sg * (1.0 + gate * (1.0 - sg)))).astype(BF16)
        prod = do * ov
        lane = lax.broadcasted_iota(jnp.int32, (tm, LANES), 1)
        dl = jnp.zeros((tm, LANES), F32)
        for h in range(hpg):
            sl = slice(h * HEAD_DIM, (h + 1) * HEAD_DIM)
            dl = jnp.where(lane == h, jnp.sum(prod[:, sl], axis=1, keepdims=True), dl)
        dl_ref[...] = dl

    spec = pl.BlockSpec((tm, w), lambda i: (i, 0))
    return pl.pallas_call(
        body, name="attn_pre_bwd", grid=(s // tm,),
        in_specs=[spec, spec, pl.BlockSpec((tm, w), lambda i: (i, gate_blk))],
        out_specs=[spec, pl.BlockSpec((tm, LANES), lambda i: (i, 0)), spec],
        out_shape=[jax.ShapeDtypeStruct((s, w), BF16), jax.ShapeDtypeStruct((s, LANES), F32),
                   jax.ShapeDtypeStruct((s, w), BF16)],
        compiler_params=_cparams(("parallel",)),
    )(dy, o, pa)


def _attn_bwd(pg, bias, do, lse, delta, g, dil, hpg):
    s = pg.shape[0]
    w = hpg * HEAD_DIM
    rows = s // dil
    nb = rows // ATTN_BLOCK
    dov = do.reshape(rows, dil * w)
    lsev, dlv = (t.reshape(rows, dil * LANES) for t in (lse, delta))
    scale = HEAD_DIM ** -0.5

    def body(q_ref, kc_ref, kp_ref, vc_ref, vp_ref, bias_ref, do_ref, lse_ref, dl_ref,
             dq_ref, dk_ref, dv_ref, db_ref, dkc_ref, dvc_ref):
        r, i = pl.program_id(0), pl.program_id(1)
        n = nb - 1 - i
        valid = _attn_valid(n == 0)

        @pl.when((r == 0) & (i == 0))
        def _():
            db_ref[...] = jnp.zeros_like(db_ref)

        @pl.when(i == 0)
        def _():
            dkc_ref[...] = jnp.zeros_like(dkc_ref)
            dvc_ref[...] = jnp.zeros_like(dvc_ref)

        for h in range(hpg):
            sl = slice(h * HEAD_DIM, (h + 1) * HEAD_DIM)
            q = q_ref[:, sl]
            dov_ = do_ref[:, sl]
            k2 = jnp.concatenate([kp_ref[:, sl], kc_ref[:, sl]], axis=0)
            v2 = jnp.concatenate([vp_ref[:, sl], vc_ref[:, sl]], axis=0)
            sc = _dot_nt(q, k2) * scale + bias_ref[h]
            p = jnp.exp(jnp.where(valid, sc - lse_ref[:, h:h + 1], NEG_INF))
            dp = _dot_nt(dov_, v2)
            ds = p * (dp - dl_ref[:, h:h + 1])
            db_ref[h] += ds
            dsb = ds.astype(BF16)
            dq_ref[:, sl] = (_dot(dsb, k2) * scale).astype(BF16)
            dk2 = _dot_tn(dsb, q) * scale
            dv2 = _dot_tn(p.astype(BF16), dov_)
            dk_ref[:, sl] = (dk2[ATTN_BLOCK:] + dkc_ref[:, sl]).astype(BF16)
            dv_ref[:, sl] = (dv2[ATTN_BLOCK:] + dvc_ref[:, sl]).astype(BF16)
            dkc_ref[:, sl] = dk2[:ATTN_BLOCK]
            dvc_ref[:, sl] = dv2[:ATTN_BLOCK]

    def col(off):
        return lambda r, i: (r * nb + nb - 1 - i, 3 * g + off)

    def colp(off):
        return lambda r, i: (r * nb + jnp.maximum(nb - 2 - i, 0), 3 * g + off)

    blk = (ATTN_BLOCK, w)
    tok = pl.BlockSpec(blk, lambda r, i: (nb - 1 - i, r))
    tok1 = pl.BlockSpec((ATTN_BLOCK, LANES), lambda r, i: (nb - 1 - i, r))
    dq, dk, dv, db = pl.pallas_call(
        body, name=f"attn_bwd_g{g}", grid=(dil, nb),
        in_specs=[pl.BlockSpec(blk, col(0)), pl.BlockSpec(blk, col(1)), pl.BlockSpec(blk, colp(1)),
                  pl.BlockSpec(blk, col(2)), pl.BlockSpec(blk, colp(2)),
                  pl.BlockSpec((None, hpg, ATTN_BLOCK, 2 * ATTN_BLOCK), lambda r, i: (g, 0, 0, 0)),
                  tok, tok1, tok1],
        out_specs=[tok, tok, tok,
                   pl.BlockSpec((hpg, ATTN_BLOCK, 2 * ATTN_BLOCK), lambda r, i: (0, 0, 0))],
        out_shape=[jax.ShapeDtypeStruct((rows, dil * w), BF16)] * 3
        + [jax.ShapeDtypeStruct((hpg, ATTN_BLOCK, 2 * ATTN_BLOCK), F32)],
        scratch_shapes=[pltpu.VMEM(blk, F32), pltpu.VMEM(blk, F32)],
        compiler_params=_cparams(("arbitrary", "arbitrary")),
    )(pg, pg, pg, pg, pg, bias, dov, lsev, dlv)
    return dq.reshape(s, w), dk.reshape(s, w), dv.reshape(s, w), db


def _ln_fwd(xin, h, gamma, beta, name, affine_in=None, target=None):
    s, d = xin.shape
    tm = _pick(s, (128,))
    has_aff = affine_in is not None
    has_tgt = target is not None

    def body(*refs):
        it = iter(refs)
        x_ref, h_ref, g_ref, b_ref = next(it), next(it), next(it), next(it)
        if has_aff:
            gi_ref, bi_ref = next(it), next(it)
        if has_tgt:
            t_ref = next(it)
        xh_ref, rs_ref = next(it), next(it)
        x = x_ref[...]
        if has_aff:
            x = x * gi_ref[...] + bi_ref[...]
        u = DEEPNORM_ALPHA * x + h_ref[...]
        mu = jnp.mean(u, axis=1, keepdims=True)
        uc = u - mu
        var = jnp.mean(uc * uc, axis=1, keepdims=True)
        rstd = lax.rsqrt(var + LN_EPS)
        xhat = uc * rstd
        xh_ref[...] = xhat
        rs_ref[...] = rstd
        y = xhat * g_ref[...] + b_ref[...]
        if has_tgt:
            dy_ref, l_ref = next(it), next(it)
            e = y - t_ref[...]
            dy_ref[...] = e * (1.0 / d)
            l_ref[...] = jnp.sum(e * e, axis=1, keepdims=True)
        else:
            y_ref = next(it)
            y_ref[...] = y.astype(BF16)

    row = pl.BlockSpec((tm, d), lambda i: (i, 0))
    vec = pl.BlockSpec((1, d), lambda i: (0, 0))
    one = pl.BlockSpec((tm, 1), lambda i: (i, 0))
    in_specs = [row, row, vec, vec] + ([vec, vec] if has_aff else []) + ([row] if has_tgt else [])
    args = [xin, h, gamma, beta] + (list(affine_in) if has_aff else []) + ([target] if has_tgt else [])
    out_specs = [row, one] + ([row, one] if has_tgt else [row])
    out_shape = [jax.ShapeDtypeStruct((s, d), F32), jax.ShapeDtypeStruct((s, 1), F32)]
    out_shape += ([jax.ShapeDtypeStruct((s, d), F32), jax.ShapeDtypeStruct((s, 1), F32)] if has_tgt
                  else [jax.ShapeDtypeStruct((s, d), BF16)])
    return pl.pallas_call(
        body, name=name, grid=(s // tm,), in_specs=in_specs, out_specs=out_specs, out_shape=out_shape,
        compiler_params=_cparams(("parallel",)),
    )(*args)


def _ln_bwd(dy, xhat, rstd, gamma, name):
    s, d = dy.shape
    tm = _pick(s, (128,))

    def body(dy_ref, xh_ref, rs_ref, g_ref, du_ref, dub_ref, dg_ref, db_ref):
        @pl.when(pl.program_id(0) == 0)
        def _():
            dg_ref[...] = jnp.zeros_like(dg_ref)
            db_ref[...] = jnp.zeros_like(db_ref)

        dyv = dy_ref[...]
        xh = xh_ref[...]
        dg_ref[...] += jnp.sum(dyv * xh, axis=0, keepdims=True)
        db_ref[...] += jnp.sum(dyv, axis=0, keepdims=True)
        dxh = dyv * g_ref[...]
        m1 = jnp.mean(dxh, axis=1, keepdims=True)
        m2 = jnp.mean(dxh * xh, axis=1, keepdims=True)
        du = rs_ref[...] * (dxh - m1 - xh * m2)
        du_ref[...] = du
        dub_ref[...] = du.astype(BF16)

    row = pl.BlockSpec((tm, d), lambda i: (i, 0))
    vec = pl.BlockSpec((1, d), lambda i: (0, 0))
    one = pl.BlockSpec((tm, 1), lambda i: (i, 0))
    return pl.pallas_call(
        body, name=name, grid=(s // tm,), in_specs=[row, row, one, vec],
        out_specs=[row, row, vec, vec],
        out_shape=[jax.ShapeDtypeStruct((s, d), F32), jax.ShapeDtypeStruct((s, d), BF16),
                   jax.ShapeDtypeStruct((1, d), F32), jax.ShapeDtypeStruct((1, d), F32)],
        compiler_params=_cparams(("arbitrary",)),
    )(dy, xhat, rstd, gamma)


_HALO = 16
_STRIP = 16


def _strips(tm, fn, init, reverse=False):
    n = tm // _STRIP

    def step(i, carry):
        s_ = n - 1 - i if reverse else i
        return fn(pl.ds(pl.multiple_of(s_ * _STRIP, _STRIP), _STRIP), carry)

    return lax.fori_loop(0, n, step, init)


def _fold8(t):
    return t[0:8] + t[8:16]


def _conv_taps(ext, tm, w_ref):
    acc = None
    for k in range(CONV_WIDTH):
        lo = _HALO - (CONV_WIDTH - 1) + k
        term = w_ref[k:k + 1, :] * ext[lo:lo + tm, :]
        acc = term if acc is None else acc + term
    return acc


def _conv_strip(prev, cur, w_ref):
    ext = jnp.concatenate([prev, cur], axis=0)
    acc, taps = None, []
    for k in range(CONV_WIDTH):
        lo = _STRIP - (CONV_WIDTH - 1) + k
        taps.append(ext[lo:lo + _STRIP, :])
        term = w_ref[k:k + 1, :] * taps[k]
        acc = term if acc is None else acc + term
    return acc, taps


def _conv_fwd(pzx, conv_w, conv_b, d_inner):
    s, _ = pzx.shape
    cd = conv_w.shape[1]
    tm = _pick(s, (512, 256, 128))
    tc = _pick(cd, (1024, 512, 256, 128))
    off = d_inner // tc
    hb = tm // _HALO

    def body(x_ref, p_ref, w_ref, b_ref, o_ref):
        prev = jnp.where(pl.program_id(0) > 0, p_ref[...].astype(F32), 0.0)
        ext = jnp.concatenate([prev, x_ref[...].astype(F32)], axis=0)
        pre = _conv_taps(ext, tm, w_ref) + b_ref[...]
        o_ref[...] = (pre * _sigmoid(pre)).astype(BF16)

    return pl.pallas_call(
        body, name="conv_fwd", grid=(s // tm, cd // tc),
        in_specs=[pl.BlockSpec((tm, tc), lambda i, j: (i, off + j)),
                  pl.BlockSpec((_HALO, tc), lambda i, j: (jnp.maximum(i * hb - 1, 0), off + j)),
                  pl.BlockSpec((CONV_WIDTH, tc), lambda i, j: (0, j)),
                  pl.BlockSpec((1, tc), lambda i, j: (0, j))],
        out_specs=pl.BlockSpec((tm, tc), lambda i, j: (i, j)),
        out_shape=jax.ShapeDtypeStruct((s, cd), BF16),
        compiler_params=_cparams(("parallel", "parallel")),
    )(pzx, pzx, conv_w, conv_b)


def _conv_bwd_a(pzx, dxbc, conv_w, conv_b, d_inner):
    s, _ = pzx.shape
    cd = conv_w.shape[1]
    tm = _pick(s, (512, 256, 128))
    tc = _pick(cd, (1024, 512, 256, 128))
    off = d_inner // tc
    hb = tm // _HALO

    def body(x_ref, p_ref, d_ref, w_ref, b_ref, o_ref, dw_ref, db_ref, acc_ref):
        @pl.when(pl.program_id(1) == 0)
        def _():
            dw_ref[...] = jnp.zeros_like(dw_ref)
            db_ref[...] = jnp.zeros_like(db_ref)

        acc_ref[...] = jnp.zeros_like(acc_ref)

        def strip(rows, prev):
            cur = x_ref[rows, :].astype(F32)
            pre, taps = _conv_strip(prev, cur, w_ref)
            pre = pre + b_ref[...]
            sg = _sigmoid(pre)
            dpre = d_ref[rows, :].astype(F32) * (sg * (1.0 + pre * (1.0 - sg)))
            o_ref[rows, :] = dpre
            for k in range(CONV_WIDTH):
                acc_ref[k] += _fold8(dpre * taps[k])
            acc_ref[CONV_WIDTH] += _fold8(dpre)
            return cur

        _strips(tm, strip, jnp.where(pl.program_id(1) > 0, p_ref[...].astype(F32), 0.0))
        for k in range(CONV_WIDTH):
            dw_ref[k:k + 1, :] += jnp.sum(acc_ref[k], axis=0, keepdims=True)
        db_ref[...] += jnp.sum(acc_ref[CONV_WIDTH], axis=0, keepdims=True)

    return pl.pallas_call(
        body, name="conv_bwd_a", grid=(cd // tc, s // tm),
        in_specs=[pl.BlockSpec((tm, tc), lambda j, i: (i, off + j)),
                  pl.BlockSpec((_HALO, tc), lambda j, i: (jnp.maximum(i * hb - 1, 0), off + j)),
                  pl.BlockSpec((tm, tc), lambda j, i: (i, j)),
                  pl.BlockSpec((CONV_WIDTH, tc), lambda j, i: (0, j)),
                  pl.BlockSpec((1, tc), lambda j, i: (0, j))],
        out_specs=[pl.BlockSpec((tm, tc), lambda j, i: (i, j)),
                   pl.BlockSpec((CONV_WIDTH, tc), lambda j, i: (0, j)),
                   pl.BlockSpec((1, tc), lambda j, i: (0, j))],
        out_shape=[jax.ShapeDtypeStruct((s, cd), F32), jax.ShapeDtypeStruct((CONV_WIDTH, cd), F32),
                   jax.ShapeDtypeStruct((1, cd), F32)],
        scratch_shapes=[pltpu.VMEM((CONV_WIDTH + 1, 8, tc), F32)],
        compiler_params=_cparams(("parallel", "arbitrary")),
    )(pzx, pzx, dxbc, conv_w, conv_b)


def _conv_bwd_b(dpre, conv_w, into, col_off):
    s, cd = dpre.shape
    tm = _pick(s, (512, 256, 128))
    tc = _pick(cd, (1024, 512, 256, 128))
    hb = tm // 8
    nrb = s // tm
    assert col_off % tc == 0
    co = col_off // tc

    def body(x_ref, nx_ref, w_ref, into_ref, o_ref):
        nxt = jnp.where(pl.program_id(0) < nrb - 1, nx_ref[...], 0.0)
        ext = jnp.concatenate([x_ref[...], nxt], axis=0)
        acc = None
        for k in range(CONV_WIDTH):
            lo = CONV_WIDTH - 1 - k
            term = w_ref[k:k + 1, :] * ext[lo:lo + tm, :]
            acc = term if acc is None else acc + term
        o_ref[...] = acc.astype(BF16)

    return pl.pallas_call(
        body, name="conv_bwd_b", grid=(nrb, cd // tc),
        in_specs=[pl.BlockSpec((tm, tc), lambda i, j: (i, j)),
                  pl.BlockSpec((8, tc), lambda i, j: (jnp.minimum((i + 1) * hb, s // 8 - 1), j)),
                  pl.BlockSpec((CONV_WIDTH, tc), lambda i, j: (0, j)), _ANY],
        out_specs=pl.BlockSpec((tm, tc), lambda i, j: (i, j + co)),
        out_shape=jax.ShapeDtypeStruct(into.shape, BF16),
        input_output_aliases={3: 0},
        compiler_params=_cparams(("parallel", "parallel")),
    )(dpre, dpre, conv_w, into)


def _expand_matrix():
    e = np.zeros((LANES, SSM_GROUP_WIDTH), np.float32)
    for h in range(HEADS_PER_SSM_GROUP):
        e[h, h * SSM_HEAD_DIM:(h + 1) * SSM_HEAD_DIM] = 1.0
    return jnp.asarray(e, BF16)


def _expand(t, e):
    return _dot(t.astype(BF16), e)


def _segsum(v, e):
    hi, lo = _split2(v)
    return _dot_nt(hi, e) + _dot_nt(lo, e)


def _tri_dot(tri, x):
    hi, mid, lo = _split3(x)
    return _dot(tri, hi) + _dot(tri, mid) + _dot(tri, lo)


def _ssd_common(dtp_ref, a_ref, dtb_ref, x_ref, e):
    li = lax.broadcasted_iota(jnp.int32, (CHUNK, CHUNK), 0)
    si = lax.broadcasted_iota(jnp.int32, (CHUNK, CHUNK), 1)
    causal = li >= si
    tril = causal.astype(BF16)
    raw = dtp_ref[...] + dtb_ref[...]
    dt = jnp.maximum(raw, 0.0) + jnp.log(1.0 + jnp.exp(-jnp.abs(raw)))
    head_lane = lax.broadcasted_iota(jnp.int32, (1, LANES), 1) < HEADS_PER_SSM_GROUP
    a = jnp.where(head_lane, -jnp.exp(a_ref[...]), 0.0)
    a_cum = _tri_dot(tril, dt * a)
    a_cum_t = a_cum.T
    e_a = jnp.exp(a_cum)
    to_end = jnp.exp(a_cum[CHUNK - 1:CHUNK, :] - a_cum)
    x = x_ref[...].astype(F32)
    dt_e = _expand(dt, e)
    return dict(causal=causal, raw=raw, dt=dt, a=a, a_cum=a_cum, a_cum_t=a_cum_t, e_a=e_a,
                to_end=to_end, x=x, dt_e=dt_e, xdt=x * dt_e, e_a_e=_expand(e_a, e),
                to_end_e=_expand(to_end, e))


def _decay(q, h):
    seg = q["a_cum"][:, h:h + 1] - q["a_cum_t"][h:h + 1, :]
    return jnp.exp(jnp.where(q["causal"], seg, -jnp.inf))


def _ssd_specs(ng, d_inner, rev, nc):
    cidx = (lambda i: nc - 1 - i) if rev else (lambda i: i)
    boff = d_inner // D_STATE
    return dict(
        xs=pl.BlockSpec((CHUNK, SSM_GROUP_WIDTH), lambda g, i: (cidx(i), g)),
        b=pl.BlockSpec((CHUNK, D_STATE), lambda g, i: (cidx(i), boff + g)),
        c=pl.BlockSpec((CHUNK, D_STATE), lambda g, i: (cidx(i), boff + ng + g)),
        dtp=pl.BlockSpec((None, CHUNK, LANES), lambda g, i: (g, cidx(i), 0)),
        vec=pl.BlockSpec((None, 1, LANES), lambda g, i: (g, 0, 0)),
        wide=pl.BlockSpec((None, 1, SSM_GROUP_WIDTH), lambda g, i: (g, 0, 0)),
        e=pl.BlockSpec((LANES, SSM_GROUP_WIDTH), lambda g, i: (0, 0)),
        st=pl.BlockSpec((None, None, D_STATE, SSM_GROUP_WIDTH), lambda g, i: (g, cidx(i), 0, 0)),
        tok=pl.BlockSpec((CHUNK, SSM_GROUP_WIDTH), lambda g, i: (cidx(i), g)),
        bc_out=pl.BlockSpec((CHUNK, D_STATE), lambda g, i: (cidx(i), g)),
    )


def _ssd_fwd(xbc, dtp, a_pad, dtb_pad, dsk_e, e, d_inner):
    s = xbc.shape[0]
    ng = d_inner // SSM_GROUP_WIDTH
    nc = s // CHUNK

    def body(x_ref, b_ref, c_ref, dtp_ref, a_ref, dtb_ref, dsk_ref, e_ref, y_ref, st_ref, state):
        lane = lax.broadcasted_iota(jnp.int32, (CHUNK, LANES), 1)
        @pl.when(pl.program_id(1) == 0)
        def _():
            state[...] = jnp.zeros_like(state)

        ev = e_ref[...]
        q = _ssd_common(dtp_ref, a_ref, dtb_ref, x_ref, ev)
        bm, cm = b_ref[...], c_ref[...]
        cb = _dot_nt(cm, bm)
        s0 = state[...]
        st_ref[...] = s0
        y = _dot(cm, s0.astype(BF16)) * q["e_a_e"] + dsk_ref[...] * q["x"]
        xdt = q["xdt"]
        left = lane[:, :] < SSM_HEAD_DIM
        for j in range(HEADS_PER_SSM_GROUP // 2):
            sl = slice(j * LANES, (j + 1) * LANES)
            x2 = xdt[:, sl]
            m0 = (cb * _decay(q, 2 * j)).astype(BF16)
            m1 = (cb * _decay(q, 2 * j + 1)).astype(BF16)
            mcat = jnp.concatenate([m0, m1], axis=1)
            xbd = jnp.concatenate([jnp.where(left, x2, 0.0), jnp.where(left, 0.0, x2)], axis=0).astype(BF16)
            y_ref[:, sl] = (y[:, sl] + _dot(mcat, xbd)).astype(BF16)
        state[...] = s0 * q["e_a_e"][CHUNK - 1:CHUNK, :] + _dot_tn(bm, (q["to_end_e"] * xdt).astype(BF16))

    sp = _ssd_specs(ng, d_inner, False, nc)
    return pl.pallas_call(
        body, name="ssd_fwd", grid=(ng, nc),
        in_specs=[sp["xs"], sp["b"], sp["c"], sp["dtp"], sp["vec"], sp["vec"], sp["wide"], sp["e"]],
        out_specs=[sp["tok"], sp["st"]],
        out_shape=[jax.ShapeDtypeStruct((s, d_inner), BF16),
                   jax.ShapeDtypeStruct((ng, nc, D_STATE, SSM_GROUP_WIDTH), F32)],
        scratch_shapes=[pltpu.VMEM((D_STATE, SSM_GROUP_WIDTH), F32)],
        compiler_params=_cparams(("parallel", "arbitrary")),
    )(xbc, xbc, xbc, dtp, a_pad, dtb_pad, dsk_e, e)


def _ssd_bwd(xbc, dtp, a_pad, dtb_pad, dsk_e, e, states, dy, d_inner):
    s = xbc.shape[0]
    ng = d_inner // SSM_GROUP_WIDTH
    nc = s // CHUNK

    def body(x_ref, b_ref, c_ref, dtp_ref, a_ref, dtb_ref, dsk_ref, e_ref, st_ref, dy_ref,
             dx_ref, db_ref, dc_ref, ddt_ref, da_ref, ddtb_ref, dd_ref, dstate):
        lane = lax.broadcasted_iota(jnp.int32, (CHUNK, LANES), 1)
        sub = lax.broadcasted_iota(jnp.int32, (CHUNK, LANES), 0)
        @pl.when(pl.program_id(1) == 0)
        def _():
            dstate[...] = jnp.zeros_like(dstate)
            da_ref[...] = jnp.zeros_like(da_ref)
            ddtb_ref[...] = jnp.zeros_like(ddtb_ref)
            dd_ref[...] = jnp.zeros_like(dd_ref)

        ev = e_ref[...]
        q = _ssd_common(dtp_ref, a_ref, dtb_ref, x_ref, ev)
        bm, cm = b_ref[...], c_ref[...]
        cb = _dot_nt(cm, bm)
        x, xdt, e_a_e, to_end_e = q["x"], q["xdt"], q["e_a_e"], q["to_end_e"]
        s0 = st_ref[...]
        s0b = s0.astype(BF16)
        ds1 = dstate[...]
        ds1b = ds1.astype(BF16)
        dy = dy_ref[...].astype(F32)
        e_last_e = e_a_e[CHUNK - 1:CHUNK, :]

        dye = dy * e_a_e
        dyeb = dye.astype(BF16)
        cs0 = _dot(cm, s0b)
        dc = _dot_nt(dyeb, s0b)
        dstate[...] = e_last_e * ds1 + _dot_tn(cm, dyeb)
        da_col = _segsum(dye * cs0, ev)

        gmat = _dot(bm, ds1b)
        dxdt = to_end_e * gmat
        dte = _segsum(xdt * gmat, ev) * q["to_end"]
        db = _dot_nt((to_end_e * xdt).astype(BF16), ds1b)
        da_col = da_col - dte
        last_row = (jnp.sum(dte, axis=0, keepdims=True)
                    + q["e_a"][CHUNK - 1:CHUNK, :] * jnp.sum(_segsum(s0 * ds1, ev), axis=0, keepdims=True))

        left = lane < SSM_HEAD_DIM
        dcb = jnp.zeros((CHUNK, CHUNK), F32)
        row_acc = jnp.zeros((CHUNK, LANES), F32)
        for j in range(HEADS_PER_SSM_GROUP // 2):
            sl = slice(j * LANES, (j + 1) * LANES)
            x2 = xdt[:, sl].astype(BF16)
            dy2 = dy[:, sl]
            dyl = jnp.where(left, dy2, 0.0).astype(BF16)
            dyr = jnp.where(left, 0.0, dy2).astype(BF16)
            ms = []
            for hh, dyh in ((0, dyl), (1, dyr)):
                h = 2 * j + hh
                dec = _decay(q, h)
                m = cb * dec
                dm = _dot_nt(dyh, x2)
                dcb = dcb + dm * dec
                dseg = dm * m
                da_col = da_col + jnp.where(lane == h, jnp.sum(dseg, axis=1, keepdims=True), 0.0)
                row_acc = row_acc + jnp.where(sub == h, jnp.sum(dseg, axis=0, keepdims=True), 0.0)
                ms.append(m.astype(BF16))
            mst = jnp.concatenate(ms, axis=0)
            dyst = jnp.concatenate([dyl, dyr], axis=0)
            d2 = dxdt[:, sl] + _dot_tn(mst, dyst)
            dx_ref[:, sl] = (d2 * q["dt_e"][:, sl] + dsk_ref[:, sl] * dy2).astype(BF16)
            dxdt_x = d2 * x[:, sl]
            if j == 0:
                parts = [dxdt_x]
            else:
                parts.append(dxdt_x)
        dcbb = dcb.astype(BF16)
        dc_ref[...] = (dc + _dot(dcbb, bm)).astype(BF16)
        db_ref[...] = (db + _dot_tn(dcbb, cm)).astype(BF16)

        d_a = da_col - row_acc.T + jnp.where(sub == CHUNK - 1, last_row, 0.0)
        triu = (lax.broadcasted_iota(jnp.int32, (CHUNK, CHUNK), 1)
                >= lax.broadcasted_iota(jnp.int32, (CHUNK, CHUNK), 0)).astype(BF16)
        d_dta = _tri_dot(triu, d_a)
        ddt = d_dta * q["a"] + _segsum(jnp.concatenate(parts, axis=1), ev)
        ddt_raw = ddt * _sigmoid(q["raw"])
        ddt_ref[...] = ddt_raw
        da_ref[...] += jnp.sum(d_dta * q["dt"], axis=0, keepdims=True) * q["a"]
        ddtb_ref[...] += jnp.sum(ddt_raw, axis=0, keepdims=True)
        dd_ref[...] += jnp.sum(dy * x, axis=0, keepdims=True)

    sp = _ssd_specs(ng, d_inner, True, nc)
    return pl.pallas_call(
        body, name="ssd_bwd", grid=(ng, nc),
        in_specs=[sp["xs"], sp["b"], sp["c"], sp["dtp"], sp["vec"], sp["vec"], sp["wide"], sp["e"],
                  sp["st"], sp["tok"]],
        out_specs=[sp["tok"], sp["bc_out"], sp["bc_out"], sp["dtp"], sp["vec"], sp["vec"], sp["wide"]],
        out_shape=[jax.ShapeDtypeStruct((s, d_inner), BF16),
                   jax.ShapeDtypeStruct((s, ng * D_STATE), BF16),
                   jax.ShapeDtypeStruct((s, ng * D_STATE), BF16),
                   jax.ShapeDtypeStruct((ng, s, LANES), F32),
                   jax.ShapeDtypeStruct((ng, 1, LANES), F32),
                   jax.ShapeDtypeStruct((ng, 1, LANES), F32),
                   jax.ShapeDtypeStruct((ng, 1, SSM_GROUP_WIDTH), F32)],
        scratch_shapes=[pltpu.VMEM((D_STATE, SSM_GROUP_WIDTH), F32)],
        compiler_params=_cparams(("parallel", "arbitrary")),
    )(xbc, xbc, xbc, dtp, a_pad, dtb_pad, dsk_e, e, states, dy)


def _gate_norm_fwd(y, pzx, norm_w):
    s, di = y.shape
    ng = di // SSM_GROUP_WIDTH
    tm = _pick(s, (512, 256, 128))

    def body(y_ref, z_ref, w_ref, o_ref):
        z = z_ref[...].astype(F32)
        y2 = y_ref[...].astype(F32) * (z * _sigmoid(z))
        r = lax.rsqrt(jnp.mean(y2 * y2, axis=1, keepdims=True) + RMS_EPS)
        o_ref[...] = (y2 * r * w_ref[...]).astype(BF16)

    blk = pl.BlockSpec((tm, SSM_GROUP_WIDTH), lambda i, g: (i, g))
    return pl.pallas_call(
        body, name="gate_norm_fwd", grid=(s // tm, ng),
        in_specs=[blk, blk, pl.BlockSpec((1, SSM_GROUP_WIDTH), lambda i, g: (0, g))],
        out_specs=blk, out_shape=jax.ShapeDtypeStruct((s, di), BF16),
        compiler_params=_cparams(("parallel", "parallel")),
    )(y, pzx, norm_w)


def _gate_norm_bwd(dy3, y, pzx, norm_w):
    s, di = y.shape
    ng = di // SSM_GROUP_WIDTH
    tm = _pick(s, (512, 256, 128))

    def body(d_ref, y_ref, z_ref, w_ref, dy_ref, dz_ref, dw_ref):
        @pl.when(pl.program_id(1) == 0)
        def _():
            dw_ref[...] = jnp.zeros_like(dw_ref)

        z = z_ref[...].astype(F32)
        yv = y_ref[...].astype(F32)
        sg = _sigmoid(z)
        sz = z * sg
        y2 = yv * sz
        r = lax.rsqrt(jnp.mean(y2 * y2, axis=1, keepdims=True) + RMS_EPS)
        nrm = y2 * r
        d3 = d_ref[...].astype(F32)
        dw_ref[...] += jnp.sum(d3 * nrm, axis=0, keepdims=True)
        dn = d3 * w_ref[...]
        dy2 = r * (dn - nrm * jnp.mean(dn * nrm, axis=1, keepdims=True))
        dy_ref[...] = (dy2 * sz).astype(BF16)
        dz_ref[...] = (dy2 * yv * (sg * (1.0 + z * (1.0 - sg)))).astype(BF16)

    blk = pl.BlockSpec((tm, SSM_GROUP_WIDTH), lambda g, i: (i, g))
    vec = pl.BlockSpec((1, SSM_GROUP_WIDTH), lambda g, i: (0, g))
    return pl.pallas_call(
        body, name="gate_norm_bwd", grid=(ng, s // tm),
        in_specs=[blk, blk, blk, vec], out_specs=[blk, blk, vec],
        out_shape=[jax.ShapeDtypeStruct((s, di), BF16), jax.ShapeDtypeStruct(pzx.shape, BF16),
                   jax.ShapeDtypeStruct((1, di), F32)],
        compiler_params=_cparams(("parallel", "arbitrary")),
    )(dy3, y, pzx, norm_w)


_ANY = pl.BlockSpec(memory_space=pl.ANY)


def _place():
    x, y, c = lax.axis_index("x"), lax.axis_index("y"), lax.axis_index("c")
    chips = [(1 - x, y), (x, 1 - y), (1 - x, 1 - y)]
    return x, y, c, chips


def _cast_to_slot(x, kvec, name, after=None):
    r, cn = x.shape
    tr = _rows_per_block(r, cn)
    extra = [] if after is None else [after]

    def body(k_ref, x_ref, *rest):
        rest[-1][...] = x_ref[...].astype(BF16)

    grid_spec = pltpu.PrefetchScalarGridSpec(
        num_scalar_prefetch=1, grid=(r // tr,),
        in_specs=[pl.BlockSpec((tr, cn), lambda i, k: (i, 0))] + [_ANY] * len(extra),
        out_specs=pl.BlockSpec((None, tr, cn), lambda i, k: (k[0], i, 0)))
    return pl.pallas_call(
        body, name=name, grid_spec=grid_spec, out_shape=jax.ShapeDtypeStruct((N_CHIPS, r, cn), BF16),
        compiler_params=_cparams(("parallel",)),
    )(kvec, x, *extra)


def _swap_halves(gs, name, after=None):
    n = len(gs)
    extra = [] if after is None else [after]

    def body(*refs):
        ins, outs = refs[:n], refs[n + len(extra):2 * n + len(extra)]
        send_sems, recv_sems = refs[2 * n + len(extra):]
        x, y, c, _ = _place()
        cps = []
        for w in range(n):
            hr = gs[w].shape[1] // 2
            cp = pltpu.make_async_remote_copy(
                src_ref=ins[w].at[:, pl.ds((1 - c) * hr, hr)], dst_ref=outs[w],
                send_sem=send_sems.at[w], recv_sem=recv_sems.at[w],
                device_id=(x, y, 1 - c), device_id_type=MESH)
            cp.start()
            cps.append(cp)
        for cp in cps:
            cp.wait()

    return pl.pallas_call(
        body, name=name,
        in_specs=[_ANY] * (n + len(extra)), out_specs=[_ANY] * n,
        out_shape=[jax.ShapeDtypeStruct((g.shape[0], g.shape[1] // 2, g.shape[2]), g.dtype) for g in gs],
        scratch_shapes=[pltpu.SemaphoreType.DMA((n,)), pltpu.SemaphoreType.DMA((n,))],
    )(*gs, *extra)


def _join_halves(fs, name, after=None):
    n = len(fs)
    extra = [] if after is None else [after]

    def body(*refs):
        outs = refs[n + len(extra):2 * n + len(extra)]
        send_sems, recv_sems = refs[2 * n + len(extra):]
        x, y, c, _ = _place()

        def copy(w, hc):
            hr = fs[w].shape[0] // 2
            rows = outs[w].at[pl.ds(hc * hr, hr)]
            return pltpu.make_async_remote_copy(
                src_ref=rows, dst_ref=rows, send_sem=send_sems.at[w], recv_sem=recv_sems.at[w],
                device_id=(x, y, 1 - c), device_id_type=MESH)

        cps = [copy(w, c) for w in range(n)]
        for cp in cps:
            cp.start()
        for w in range(n):
            copy(w, 1 - c).wait_recv()
        for cp in cps:
            cp.wait_send()

    return pl.pallas_call(
        body, name=name,
        in_specs=[_ANY] * (n + len(extra)), out_specs=[_ANY] * n,
        out_shape=[jax.ShapeDtypeStruct(f.shape, f.dtype) for f in fs],
        input_output_aliases={w: w for w in range(n)},
        scratch_shapes=[pltpu.SemaphoreType.DMA((n,)), pltpu.SemaphoreType.DMA((n,))],
    )(*fs, *extra)


_HBM_SPEC = pl.BlockSpec(memory_space=pltpu.HBM)
_SEM_SPEC = pl.BlockSpec(memory_space=pltpu.SEMAPHORE)
_VMEM_SPEC = pl.BlockSpec(memory_space=pltpu.VMEM)
_EFFECT = pltpu.SideEffectType.DATAFLOW_SIDE_EFFECTING
_TOKEN = jax.ShapeDtypeStruct((8, LANES), F32)


def _hbm(a):
    return pltpu.with_memory_space_constraint(a, pltpu.HBM)


_NEIGHBOURS, _DIAGONAL, _ALL_CHIPS = (0, 1), (2,), (0, 1, 2)


def _gather_copies(bufs, refs, send_sems, recv_sems, forward, arrivals=True, which=_ALL_CHIPS):
    x, y, c, chips = _place()
    k = 2 * x + y
    out, arrive = [], []
    for w, ref in enumerate(refs):
        hr = bufs[w].shape[1] // 2
        for j, (cx, cy) in enumerate(chips):
            if j not in which:
                continue
            kj = 2 * cx + cy
            slot_out, slot_in, half_in = (kj, kj, 1 - c) if forward else (k, kj, c)
            to = (x, y, 1 - c) if forward else (cx, cy, c)
            src = ref.at[slot_out, pl.ds(c * hr, hr)]
            land = ref.at[slot_in, pl.ds(half_in * hr, hr)]
            out.append(pltpu.make_async_remote_copy(
                src_ref=src, dst_ref=src, send_sem=send_sems.at[3 * w + j], recv_sem=recv_sems.at[3 * w + j],
                device_id=to, device_id_type=MESH))
            if arrivals:
                arrive.append(pltpu.make_async_remote_copy(
                    src_ref=land, dst_ref=land, send_sem=send_sems.at[3 * w + j], recv_sem=recv_sems.at[3 * w + j],
                    device_id=to, device_id_type=MESH))
    return out, arrive


def _gather_start(bufs, forward, name, after=None, which=_ALL_CHIPS):
    n = len(bufs)
    extra = [] if after is None else [after]

    def body(*refs):
        ins = refs[:n]
        send_sems, recv_sems = refs[n + len(extra)], refs[n + len(extra) + 1]
        token = refs[-1]
        out, _ = _gather_copies(bufs, ins, send_sems, recv_sems, forward, arrivals=False, which=which)
        for cp in out:
            cp.start()
        token[...] = jnp.zeros_like(token)

    res = pl.pallas_call(
        body, name=name,
        out_shape=(pltpu.SemaphoreType.DMA((3 * n,)), pltpu.SemaphoreType.DMA((3 * n,)))
        + tuple(pltpu.HBM(b.shape, b.dtype) for b in bufs) + (_TOKEN,),
        in_specs=(_HBM_SPEC,) * n + (_ANY,) * len(extra),
        out_specs=(_SEM_SPEC, _SEM_SPEC) + (_HBM_SPEC,) * n + (_VMEM_SPEC,),
        input_output_aliases={w: 2 + w for w in range(n)},
        compiler_params=pltpu.CompilerParams(has_side_effects=_EFFECT),
    )(*[_hbm(b) for b in bufs], *extra)
    return res[0], res[1], list(res[2:2 + n]), res[-1]


def _gather_wait(bufs, send_sems, recv_sems, after, forward, name, which=_ALL_CHIPS):
    n = len(bufs)

    def body(*refs):
        ins = refs[:n]
        send_sems, recv_sems = refs[n], refs[n + 1]
        out, arrive = _gather_copies(bufs, ins, send_sems, recv_sems, forward, which=which)
        for cp in out:
            cp.wait_send()
        for cp in arrive:
            cp.wait_recv()

    res = pl.pallas_call(
        body, name=name,
        out_shape=tuple(pltpu.HBM(b.shape, b.dtype) for b in bufs),
        in_specs=(_HBM_SPEC,) * n + (_SEM_SPEC, _SEM_SPEC, _ANY), out_specs=(_HBM_SPEC,) * n,
        input_output_aliases={w: w for w in range(n)},
        compiler_params=pltpu.CompilerParams(has_side_effects=_EFFECT),
    )(*bufs, send_sems, recv_sems, after)
    return list(res)


def _swap_copy(g_ref, land_ref, send_sems, recv_sems):
    x, y, c, _ = _place()
    hr = g_ref.shape[1] // 2
    return pltpu.make_async_remote_copy(
        src_ref=g_ref.at[:, pl.ds((1 - c) * hr, hr)], dst_ref=land_ref, send_sem=send_sems.at[0],
        recv_sem=recv_sems.at[0], device_id=(x, y, 1 - c), device_id_type=MESH)


def _swap_start(g4, name):
    ns, r, cn = g4.shape

    def body(g_ref, land_ref, send_sems, recv_sems, g_thru, land_thru, token):
        _swap_copy(g_ref, land_ref, send_sems, recv_sems).start()
        token[...] = jnp.zeros_like(token)

    return pl.pallas_call(
        body, name=name,
        out_shape=(pltpu.SemaphoreType.DMA((1,)), pltpu.SemaphoreType.DMA((1,)),
                   pltpu.HBM(g4.shape, g4.dtype), pltpu.HBM((ns, r // 2, cn), g4.dtype), _TOKEN),
        in_specs=(_HBM_SPEC, _HBM_SPEC), out_specs=(_SEM_SPEC, _SEM_SPEC, _HBM_SPEC, _HBM_SPEC, _VMEM_SPEC),
        input_output_aliases={0: 2, 1: 3},
        compiler_params=pltpu.CompilerParams(has_side_effects=_EFFECT),
    )(_hbm(g4), _hbm(lax.empty((ns, r // 2, cn), g4.dtype)))


def _swap_wait(send_sems, recv_sems, g_thru, land_thru, after, name):
    def body(g_ref, land_ref, send_sems, recv_sems, after_ref, g_out, land_out):
        cp = _swap_copy(g_ref, land_ref, send_sems, recv_sems)
        cp.wait_send()
        cp.wait_recv()

    return pl.pallas_call(
        body, name=name,
        out_shape=(pltpu.HBM(g_thru.shape, g_thru.dtype), pltpu.HBM(land_thru.shape, land_thru.dtype)),
        in_specs=(_HBM_SPEC, _HBM_SPEC, _SEM_SPEC, _SEM_SPEC, _ANY), out_specs=(_HBM_SPEC, _HBM_SPEC),
        input_output_aliases={0: 0, 1: 1},
        compiler_params=pltpu.CompilerParams(has_side_effects=_EFFECT),
    )(g_thru, land_thru, send_sems, recv_sems, after)


def _scatter_copies(t_ref, land_ref, send_sems, recv_sems, arrivals=True):
    x, y, c, chips = _place()
    k = 2 * x + y
    out, arrive = [], []
    for j, (cx, cy) in enumerate(chips):
        kj = 2 * cx + cy
        out.append(pltpu.make_async_remote_copy(
            src_ref=t_ref.at[kj], dst_ref=land_ref.at[k], send_sem=send_sems.at[j], recv_sem=recv_sems.at[j],
            device_id=(cx, cy, c), device_id_type=MESH))
        if arrivals:
            arrive.append(pltpu.make_async_remote_copy(
                src_ref=t_ref.at[kj], dst_ref=land_ref.at[kj], send_sem=send_sems.at[j], recv_sem=recv_sems.at[j],
                device_id=(cx, cy, c), device_id_type=MESH))
    return out, arrive


def _scatter_start(t, name):
    def body(t_ref, land_ref, send_sems, recv_sems, t_thru, land_thru, token):
        out, _ = _scatter_copies(t_ref, land_ref, send_sems, recv_sems, arrivals=False)
        for cp in out:
            cp.start()
        token[...] = jnp.zeros_like(token)

    return pl.pallas_call(
        body, name=name,
        out_shape=(pltpu.SemaphoreType.DMA((3,)), pltpu.SemaphoreType.DMA((3,)),
                   pltpu.HBM(t.shape, t.dtype), pltpu.HBM(t.shape, t.dtype), _TOKEN),
        in_specs=(_HBM_SPEC, _HBM_SPEC), out_specs=(_SEM_SPEC, _SEM_SPEC, _HBM_SPEC, _HBM_SPEC, _VMEM_SPEC),
        input_output_aliases={0: 2, 1: 3},
        compiler_params=pltpu.CompilerParams(has_side_effects=_EFFECT),
    )(_hbm(t), _hbm(lax.empty(t.shape, t.dtype)))


def _scatter_wait(send_sems, recv_sems, t_thru, land_thru, after, name):
    def body(t_ref, land_ref, send_sems, recv_sems, after_ref, t_out, land_out):
        out, arrive = _scatter_copies(t_ref, land_ref, send_sems, recv_sems)
        for cp in out:
            cp.wait_send()
        for cp in arrive:
            cp.wait_recv()

    return pl.pallas_call(
        body, name=name,
        out_shape=(pltpu.HBM(t_thru.shape, t_thru.dtype), pltpu.HBM(land_thru.shape, land_thru.dtype)),
        in_specs=(_HBM_SPEC, _HBM_SPEC, _SEM_SPEC, _SEM_SPEC, _ANY), out_specs=(_HBM_SPEC, _HBM_SPEC),
        input_output_aliases={0: 0, 1: 1},
        compiler_params=pltpu.CompilerParams(has_side_effects=_EFFECT),
    )(t_thru, land_thru, send_sems, recv_sems, after)


def _all_gather_small(v, reduce, name):
    r, l = v.shape

    def body(v_ref, o_ref, *rest):
        if reduce:
            buf, send_sems, recv_sems = rest
        else:
            buf = o_ref
            send_sems, recv_sems = rest
        x, y, c, _ = _place()
        me = 4 * x + 2 * y + c
        buf[me] = v_ref[...]
        cps = []
        for d in range(1, N_DEV):
            peer = (x if d & 4 == 0 else 1 - x, y if d & 2 == 0 else 1 - y, c if d & 1 == 0 else 1 - c)
            cp = pltpu.make_async_remote_copy(
                src_ref=v_ref, dst_ref=buf.at[me], send_sem=send_sems.at[d - 1], recv_sem=recv_sems.at[d - 1],
                device_id=peer, device_id_type=MESH)
            cp.start()
            cps.append((cp, peer))
        for d, (cp, (px, py, pc)) in enumerate(cps, start=1):
            pltpu.make_async_remote_copy(
                src_ref=v_ref, dst_ref=buf.at[4 * px + 2 * py + pc], send_sem=send_sems.at[d - 1],
                recv_sem=recv_sems.at[d - 1], device_id=(px, py, pc), device_id_type=MESH).wait_recv()
        for cp, _ in cps:
            cp.wait_send()
        if reduce:
            acc = buf[0]
            for i in range(1, N_DEV):
                acc = acc + buf[i]
            o_ref[...] = acc

    vm = pl.BlockSpec(memory_space=pltpu.VMEM)
    out_shape = jax.ShapeDtypeStruct((r, l) if reduce else (N_DEV, r, l), F32)
    scratch = ([pltpu.VMEM((N_DEV, r, l), F32)] if reduce else []) + [
        pltpu.SemaphoreType.DMA((N_DEV - 1,)), pltpu.SemaphoreType.DMA((N_DEV - 1,))]
    return pl.pallas_call(
        body, name=name, in_specs=[vm], out_specs=vm, out_shape=out_shape, scratch_shapes=scratch,
    )(v)


_BLOCK_BYTES = 3 * 512 * 1024


def _rows_per_block(r, cn, itemsize=4):
    best = 8
    for t in range(8, r + 1, 8):
        if r % t == 0 and t * cn * itemsize <= _BLOCK_BYTES:
            best = t
    return best


def _add_sibling_half(g4, recv, cvec, name):
    ns, r, cn = g4.shape
    hr = r // 2
    tr = _rows_per_block(hr, cn)
    nrb = hr // tr

    def body(c_ref, a_ref, b_ref, o_ref):
        o_ref[...] = (a_ref[...].astype(F32) + b_ref[...].astype(F32)).astype(o_ref.dtype)

    grid_spec = pltpu.PrefetchScalarGridSpec(
        num_scalar_prefetch=1, grid=(ns, nrb),
        in_specs=[pl.BlockSpec((None, tr, cn), lambda j, i, c: (j, c[0] * nrb + i, 0)),
                  pl.BlockSpec((None, tr, cn), lambda j, i, c: (j, i, 0))],
        out_specs=pl.BlockSpec((None, tr, cn), lambda j, i, c: (j, i, 0)))
    return pl.pallas_call(
        body, name=name, grid_spec=grid_spec, out_shape=jax.ShapeDtypeStruct((ns, hr, cn), BF16),
        compiler_params=_cparams(("parallel", "parallel")),
    )(cvec, g4, recv)


def _sum_chips(r4, t4, kvec, cvec, name):
    ns, hr, cn = r4.shape
    tr = _rows_per_block(hr, cn)
    nrb = hr // tr

    def body(k_ref, c_ref, r_ref, t_ref, o_ref):
        acc = t_ref[...].astype(F32)
        for dlt in range(1, ns):
            acc = acc + r_ref[(k_ref[0] + dlt) % ns].astype(F32)
        o_ref[...] = acc

    grid_spec = pltpu.PrefetchScalarGridSpec(
        num_scalar_prefetch=2, grid=(nrb,),
        in_specs=[pl.BlockSpec((ns, tr, cn), lambda i, k, c: (0, i, 0)),
                  pl.BlockSpec((None, tr, cn), lambda i, k, c: (k[0], i, 0))],
        out_specs=pl.BlockSpec((tr, cn), lambda i, k, c: (c[0] * nrb + i, 0)))
    return pl.pallas_call(
        body, name=name, grid_spec=grid_spec, out_shape=jax.ShapeDtypeStruct((2 * hr, cn), F32),
        compiler_params=_cparams(("parallel",)),
    )(kvec, cvec, r4, t4)


def _adamw(w, g, m, v, name):
    r, cn = w.shape
    tr = _rows_per_block(r, cn)
    c1 = 1.0 - ADAM_B1 ** ADAM_STEP
    c2 = 1.0 - ADAM_B2 ** ADAM_STEP

    def body(w_ref, g_ref, m_ref, v_ref, go_ref, d_ref, mo_ref, vo_ref):
        gv = g_ref[...]
        mn = ADAM_B1 * m_ref[...] + (1.0 - ADAM_B1) * gv
        vn = ADAM_B2 * v_ref[...] + (1.0 - ADAM_B2) * (gv * gv)
        go_ref[...] = gv
        mo_ref[...] = mn
        vo_ref[...] = vn
        d_ref[...] = -ADAM_LR * ((mn / c1) / (jnp.sqrt(vn / c2) + ADAM_EPS) + ADAM_WD * w_ref[...])

    spec = pl.BlockSpec((tr, cn), lambda i: (i, 0))
    return pl.pallas_call(
        body, name=name, grid=(r // tr,), in_specs=[spec] * 4, out_specs=[spec] * 4,
        out_shape=[jax.ShapeDtypeStruct((r, cn), F32)] * 4,
        compiler_params=_cparams(("parallel",)),
    )(w, g, m, v)


def _pack(arrs):
    flat = jnp.concatenate([a.reshape(-1).astype(F32) for a in arrs])
    n = flat.shape[0]
    tot = -(-n // (8 * LANES)) * (8 * LANES)
    return jnp.pad(flat, (0, tot - n)).reshape(tot // LANES, LANES)


def _unpack(packed, shapes):
    flat = packed.reshape(-1)
    out, off = [], 0
    for shp in shapes:
        sz = int(np.prod(shp))
        out.append(flat[off:off + sz].reshape(shp))
        off += sz
    return out


class _LocalExchange:
    def __init__(self, wa4, woa4, ws4, wos4):
        self.l0, self.ssm = [wa4, woa4], [ws4, wos4]
        self.grads = {}

    def l0_begin(self):
        return self.l0[0], None

    def l0_neighbours(self, after):
        return self.l0[0]

    def l0_diagonal(self, after):
        return self.l0

    def ssm_gather_mid(self, after):
        return None

    def ssm_gather_end(self, after):
        return self.ssm

    def grad_ready(self, name, g4, after=None):
        self.grads[name] = g4
        return None

    def grad_begin(self, name, g4):
        self.grads[name] = g4
        return None

    def grad_mid(self, name, after):
        return None

    def grad_sync(self, name, after):
        pass

    def small_grads(self, small_full):
        self.small = small_full
        return None


class _Exchange:
    def __init__(self, kvec, cvec, l0_bufs, after):
        self.kvec, self.cvec, self.l0_bufs, self.bufs, self.after = kvec, cvec, l0_bufs, None, after
        self.pending, self.summed, self.last_token, self.l0_token, self.swaps = {}, {}, None, None, {}

    def l0_begin(self):
        if self.l0_token is None:
            self.l0_send, self.l0_recv, self.l0_bufs, self.l0_token = _gather_start(
                self.l0_bufs, False, "l0_gather_ici_start", self.after)
        return self.l0_bufs[0], (self.l0_token if self.bufs is None else self.sems[3])

    def _l0_step(self, which, tag, after):
        bufs = _gather_wait(self.l0_bufs, self.l0_send, self.l0_recv, after, False, "l0_gather_ici_wait_" + tag, which)
        send, recv, bufs, token = _gather_start(bufs, True, "l0_gather_fwd_start_" + tag, None, which)
        self.l0_bufs = _gather_wait(bufs, send, recv, token, True, "l0_gather_fwd_wait_" + tag, which)
        return self.l0_bufs

    def l0_neighbours(self, after):
        return self._l0_step(_NEIGHBOURS, "nb", after)[0]

    def l0_diagonal(self, after):
        return self._l0_step(_DIAGONAL, "diag", after)

    def ssm_gather_start(self, ssm_bufs):
        self.sems = _gather_start(ssm_bufs, False, "ssm_gather_ici_start", self.l0_token)
        self.bufs = self.sems[2]
        return self.sems[3]

    def ssm_gather_mid(self, after):
        bufs = _gather_wait(self.bufs, self.sems[0], self.sems[1], after, False, "ssm_gather_ici_wait")
        self.sems = _gather_start(bufs, True, "ssm_gather_fwd_start")
        self.bufs = self.sems[2]
        return self.sems[3]

    def ssm_gather_end(self, after):
        return _gather_wait(self.bufs, self.sems[0], self.sems[1], after, True, "ssm_gather_fwd_wait")

    def small_grads(self, small_full):
        packed = _all_gather_small(_pack(small_full), True, "reduce_small_grads")
        self.small = _unpack(packed, [t.shape for t in small_full])
        return packed

    def grad_begin(self, name, g4):
        self.swaps[name] = _swap_start(g4, "swap_start_" + name)
        return self.swaps[name][4]

    def grad_mid(self, name, after):
        send_sem, recv_sem, g_thru, land, _ = self.swaps.pop(name)
        g4, recv = _swap_wait(send_sem, recv_sem, g_thru, land, after, "swap_wait_" + name)
        return self._reduce(name, g4, recv)

    def grad_ready(self, name, g4, after=None):
        return self._reduce(name, g4, _swap_halves([g4], "grads_to_sibling_" + name, after)[0])

    def _reduce(self, name, g4, recv):
        t = _add_sibling_half(g4, recv, self.cvec, "add_sibling_" + name)
        send_sems, recv_sems, t_thru, land, token = _scatter_start(t, "scatter_start_" + name)
        self.pending[name] = (send_sems, recv_sems, t_thru, land)
        self.last_token = token
        return token

    def grad_sync(self, name, after):
        t, land = _scatter_wait(*self.pending.pop(name), after, "scatter_wait_" + name)
        self.summed[name] = _sum_chips(land, t, self.kvec, self.cvec, "sum_chips_" + name)


def _tie(vec, token):
    return vec if token is None else vec + token[0:1, 0:1].reshape((1,) * vec.ndim).astype(vec.dtype)


def _local_step(x2, tgt, ex, kvec, conv_w_f, conv_b_f, norm_w_f, rel_bias, dt_bias, a_log, d_skip,
                ln_g, ln_b):
    s, d = x2.shape
    wa4, tok = ex.l0_begin()
    d_attn = wa4.shape[2] * N_CHIPS // 10
    hpg = d_attn // HEAD_DIM
    d_inner = norm_w_f.shape[1]
    ng = d_inner // SSM_GROUP_WIDTH
    n_heads = dt_bias.shape[1]
    conv_dim = conv_w_f.shape[1]
    assert n_heads == ng * HEADS_PER_SSM_GROUP and conv_dim == d_inner + 2 * ng * D_STATE

    x3 = _cast_x3(x2, "cast_x", tok)
    for g, (_, dil) in enumerate(ATTN_PATTERNS):
        if dil > 1:
            x3 = _class_copy(x3, g, dil, f"class_order_x_g{g}")
    xb = x3[0]
    buckets = _bucket_tiles()
    bias = _bias_expand(rel_bias, buckets, hpg)
    pa = _in_proj_shard(x3, wa4, kvec, 0, d_attn, "mm_in_attn_own", after=tok)
    wa4 = ex.l0_neighbours(pa)
    pa = _in_proj_shard(x3, wa4, kvec, 1, d_attn, "mm_in_attn_nb1", into=pa)
    pa = _in_proj_shard(x3, wa4, kvec, 2, d_attn, "mm_in_attn_nb2", into=pa)
    wa4, woa4 = ex.l0_diagonal(pa)
    pa = _in_proj_shard(x3, wa4, kvec, 3, d_attn, "mm_in_attn_diag", into=pa)
    og, lg = [], []
    for g, (_, dil) in enumerate(ATTN_PATTERNS):
        o_, l_ = _attn_fwd(pa, bias, g, dil, hpg)
        og.append(o_)
        lg.append(l_)
    o, lse, yat = _attn_combine(og, lg, pa, hpg)
    h0 = _mm_nn_sharded(yat, woa4, F32, "mm_out_attn", after=ex.ssm_gather_mid(yat))
    g0, b0, g1, b1 = ln_g[0:1], ln_b[0:1], ln_g[1:2], ln_b[1:2]
    xhat0, rstd0, x1b = _ln_fwd(x2, h0, g0, b0, "ln0_fwd")

    wst4, wos4 = ex.ssm_gather_end(x1b)
    wst = wst4.reshape(N_CHIPS * wst4.shape[1], d)
    nzx = d_inner + conv_dim
    wos = wos4.reshape(d_inner, d)
    pzx = _mm_nt(x1b, wst, BF16, "mm_in_ssm", n=nzx)
    dt_raw = _mm_nt(x1b, wst, F32, "mm_in_dt", n=n_heads, b_row_off=nzx)

    def pad_heads(t):
        t = t.reshape(t.shape[0], ng, HEADS_PER_SSM_GROUP).transpose(1, 0, 2)
        return jnp.pad(t, ((0, 0), (0, 0), (0, LANES - HEADS_PER_SSM_GROUP)))

    def unpad_heads(t):
        return t[:, :, :HEADS_PER_SSM_GROUP].transpose(1, 0, 2).reshape(t.shape[1], n_heads)

    dtp = pad_heads(dt_raw)
    alog_p, dtb_p = pad_heads(a_log), pad_heads(dt_bias)
    dsk_e = jnp.repeat(d_skip.reshape(ng, 1, HEADS_PER_SSM_GROUP), SSM_HEAD_DIM, axis=2)
    e = _expand_matrix()
    xbc = _conv_fwd(pzx, conv_w_f, conv_b_f, d_inner)
    y_ssd, states = _ssd_fwd(xbc, dtp, alog_p, dtb_p, dsk_e, e, d_inner)
    y3 = _gate_norm_fwd(y_ssd, pzx, norm_w_f)
    h1 = _mm_nn(y3, wos, F32, "mm_out_ssm")
    xhat1, rstd1, dy2, row_sq = _ln_fwd(xhat0, h1, g1, b1, "ln1_fwd_loss", affine_in=(g0, b0), target=tgt)
    loss_local = 0.5 * jnp.sum(row_sq) / d

    du1, du1b, dg1, db1 = _ln_bwd(dy2, xhat1, rstd1, g1, "ln1_bwd")
    dy3 = _mm_nt(du1b, wos, BF16, "mm_d_y3")
    g_wos = _mm_tn(y3, du1b, BF16, "mm_g_w_out_ssm").reshape(N_CHIPS, d_inner // N_CHIPS, d)
    norm_w_t = _tie(norm_w_f, ex.grad_begin("w_out_ssm", g_wos))
    dy_ssd, dz, d_nw = _gate_norm_bwd(dy3, y_ssd, pzx, norm_w_t)
    dsk_t = _tie(dsk_e, ex.grad_mid("w_out_ssm", dy_ssd))
    dxs, dbm, dcm, ddtp, d_alog, d_dtb, d_dsk = _ssd_bwd(xbc, dtp, alog_p, dtb_p, dsk_t, e, states, dy_ssd, d_inner)
    dpre, d_cw, d_cb = _conv_bwd_a(pzx, jnp.concatenate([dxs, dbm, dcm], axis=1), conv_w_f, conv_b_f, d_inner)
    dpzx = _conv_bwd_b(dpre, conv_w_f, dz, d_inner)
    ddt_raw = unpad_heads(ddtp)
    t1 = _mm_nn(ddt_raw, wst, F32, "mm_d_x1_dt", b_row_off=nzx, add=du1, add_scale=DEEPNORM_ALPHA)
    dx1 = _mm_nn(dpzx, wst, F32, "mm_d_x1", add=t1)
    ex.grad_sync("w_out_ssm", dx1)
    g_wst = _mm_tn(dpzx, x1b, BF16, "mm_g_w_in_ssm", out_rows=wst.shape[0])
    g_wst = _mm_tn(ddt_raw, x1b, BF16, "mm_g_w_dt", out_rows=wst.shape[0], out_row_off=nzx, into=g_wst)
    g0_t = _tie(g0, ex.grad_begin("w_in_ssm", g_wst.reshape(wst4.shape)))

    du0, du0b, dg0, db0 = _ln_bwd(dx1, xhat0, rstd0, g0_t, "ln0_bwd")
    dyat = _mm_nt_sharded_k(du0b, woa4, BF16, "mm_d_yat")
    g_woa = _mm_tn(yat, du0b, BF16, "mm_g_w_out_attn", shard_cols=d // N_CHIPS)
    lse_t = _tie(lse, ex.grad_mid("w_in_ssm", g_woa))
    ex.grad_begin("w_out_attn", g_woa)
    do, delta, dgate = _attn_pre_bwd(dyat, o, pa, hpg)
    pieces, dbt = [], []
    for g, (_, dil) in enumerate(ATTN_PATTERNS):
        dq, dk, dv, db_ = _attn_bwd(pa, bias, do, lse_t, delta, g, dil, hpg)
        pieces += [dq, dk, dv]
        dbt.append(db_)
    dpa = jnp.concatenate(pieces + [dgate], axis=1)
    tok_woa = ex.grad_mid("w_out_attn", dpa)
    g_wa = _mm_tn(xb, dpa, BF16, "mm_g_w_in_attn", shard_cols=wa4.shape[2], after=tok_woa)
    ex.grad_sync("w_in_ssm", g_wa)
    ex.grad_sync("w_out_attn", g_wa)
    d_rel = _bias_reduce(jnp.stack(dbt), buckets, hpg)[:, :, 0].T
    d_dsk_h = d_dsk.reshape(n_heads, SSM_HEAD_DIM).sum(axis=1)
    small_full = [d_rel, d_cw, d_cb, unpad_heads(d_dtb), unpad_heads(d_alog), d_dsk_h[None], d_nw,
                  jnp.concatenate([dg0, dg1], axis=0), jnp.concatenate([db0, db1], axis=0)]
    tok_wa = ex.grad_ready("w_in_attn", g_wa, after=ex.small_grads(small_full))
    grad_x = _mm_nt_sharded_k(dpa, wa4, F32, "mm_d_x0", add=du0, add_scale=DEEPNORM_ALPHA, after=tok_wa)
    return loss_local, grad_x[None]


def kernel(x, w_in_attn, w_out_attn, rel_bias, w_in_ssm, conv_w, conv_b, dt_bias, a_log, d_skip, ssm_norm_w, w_out_ssm, ln_g, ln_b, loss_target, m_w_in_attn, m_w_out_attn, m_rel_bias, m_w_in_ssm, m_conv_w, m_conv_b, m_dt_bias, m_a_log, m_d_skip, m_ssm_norm_w, m_w_out_ssm, m_ln_g, m_ln_b, v_w_in_attn, v_w_out_attn, v_rel_bias, v_w_in_ssm, v_conv_w, v_conv_b, v_dt_bias, v_a_log, v_d_skip, v_ssm_norm_w, v_w_out_ssm, v_ln_g, v_ln_b):
    xi, yi, ci = lax.axis_index("x"), lax.axis_index("y"), lax.axis_index("c")
    chip = 2 * xi + yi
    cvec = jnp.reshape(ci, (1,)).astype(jnp.int32)
    kvec = jnp.reshape(chip, (1,)).astype(jnp.int32)

    cw_l, cb_l, nw_l = conv_w[0], conv_b[0], ssm_norm_w[0]
    vec_shapes = [cw_l.shape, cb_l.shape, nw_l.shape]
    vec_all = _all_gather_small(_pack([cw_l, cb_l, nw_l]), False, "gather_vectors")
    l0 = [_cast_to_slot(w_in_attn[0], kvec, "cast_w_in_attn"), _cast_to_slot(w_out_attn[0], kvec, "cast_w_out_attn")]
    ex = _Exchange(kvec, cvec, l0, after=vec_all)
    _, tok0 = ex.l0_begin()
    tok1 = ex.ssm_gather_start([_cast_to_slot(w_in_ssm[0].T, kvec, "cast_w_in_ssm", tok0),
                                _cast_to_slot(w_out_ssm[0], kvec, "cast_w_out_ssm", tok0)])
    parts = [_unpack(vec_all[2 * j], vec_shapes) for j in range(N_CHIPS)]
    conv_w_f = jnp.concatenate([p[0] for p in parts], axis=1)
    conv_b_f = jnp.concatenate([p[1] for p in parts], axis=0)[None]
    norm_w_f = jnp.concatenate([p[2] for p in parts], axis=0)[None]

    loss_local, grad_x = _local_step(
        x[0], loss_target[0], ex, kvec, conv_w_f, conv_b_f, norm_w_f, rel_bias, dt_bias, a_log,
        d_skip, ln_g, ln_b)
    loss = lax.psum(loss_local, ("x", "y", "c"))

    big_w = dict(w_in_attn=(w_in_attn, m_w_in_attn, v_w_in_attn), w_out_attn=(w_out_attn, m_w_out_attn, v_w_out_attn),
                 w_in_ssm=(w_in_ssm, m_w_in_ssm, v_w_in_ssm), w_out_ssm=(w_out_ssm, m_w_out_ssm, v_w_out_ssm))
    big = {}

    def finish(names, join_name, after):
        last = None
        for nm, gf in zip(names, _join_halves([ex.summed[nm] for nm in names], join_name, after)):
            flip = (lambda t: t.T) if nm == "w_in_ssm" else (lambda t: t)
            w_, m_, v_ = (flip(t[0]) for t in big_w[nm])
            res = _adamw(w_, gf, m_, v_, "adamw_" + nm)
            big[nm] = [flip(t)[None] for t in res]
            last = res[3]
        return last

    last = finish(["w_out_ssm", "w_in_ssm", "w_out_attn"], "grads_join_halves_a", ex.last_token)
    ex.grad_sync("w_in_attn", last)
    finish(["w_in_attn"], "grads_join_halves_b", None)

    s_rel, s_cw, s_cb, s_dtb, s_alog, s_dsk, s_nw, s_lng, s_lnb = ex.small
    cwc, nwc = conv_w.shape[2], ssm_norm_w.shape[1]
    s_cw = lax.dynamic_slice_in_dim(s_cw, chip * cwc, cwc, axis=1)[None]
    s_cb = lax.dynamic_slice_in_dim(s_cb, chip * cwc, cwc, axis=1)
    s_nw = lax.dynamic_slice_in_dim(s_nw, chip * nwc, nwc, axis=1)
    small_names = ["rel_bias", "conv_w", "conv_b", "dt_bias", "a_log", "d_skip", "ssm_norm_w", "ln_g", "ln_b"]
    small_g = [s_rel, s_cw, s_cb, s_dtb, s_alog, s_dsk, s_nw, s_lng, s_lnb]
    small_w = [rel_bias, conv_w, conv_b, dt_bias, a_log, d_skip, ssm_norm_w, ln_g, ln_b]
    small_m = [m_rel_bias, m_conv_w, m_conv_b, m_dt_bias, m_a_log, m_d_skip, m_ssm_norm_w, m_ln_g, m_ln_b]
    small_v = [v_rel_bias, v_conv_w, v_conv_b, v_dt_bias, v_a_log, v_d_skip, v_ssm_norm_w, v_ln_g, v_ln_b]
    shapes = [t.shape for t in small_w]
    res = _adamw(_pack(small_w), _pack(small_g), _pack(small_m), _pack(small_v), "adamw_small")
    small = {nm: [] for nm in small_names}
    for packed in res:
        for nm, t in zip(small_names, _unpack(packed, shapes)):
            small[nm].append(t)

    order = ["w_in_attn", "w_out_attn", "rel_bias", "w_in_ssm", "conv_w", "conv_b", "dt_bias", "a_log",
             "d_skip", "ssm_norm_w", "w_out_ssm", "ln_g", "ln_b"]
    table = {**big, **small}
    outs = [loss, grad_x]
    for kind in range(4):
        outs += [table[nm][kind] for nm in order]
    return tuple(outs)
```

```python
import functools
import math

import numpy as np
import jax
import jax.numpy as jnp
from jax import lax
from jax.experimental import pallas as pl
from jax.experimental.pallas import tpu as pltpu

F32 = jnp.float32
BF16 = jnp.bfloat16
MESH = pl.DeviceIdType.MESH

ATTN_PATTERNS = ((128, 1), (512, 4), (2048, 16))
N_GROUPS_ATTN = 3
HEAD_DIM = 128
ATTN_BLOCK = 128
NUM_BUCKETS = 32
MAX_DISTANCE = 2048
SSM_HEAD_DIM = 64
HEADS_PER_SSM_GROUP = 16
SSM_GROUP_WIDTH = HEADS_PER_SSM_GROUP * SSM_HEAD_DIM
D_STATE = 128
CONV_WIDTH = 4
CHUNK = 128
DEPTH = 2
DEEPNORM_ALPHA = (2 * DEPTH) ** 0.25
LN_EPS = 1e-5
RMS_EPS = 1e-5
NEG_INF = -1e30
ADAM_LR = 0.001
ADAM_B1 = 0.9
ADAM_B2 = 0.999
ADAM_EPS = 1e-08
ADAM_WD = 0.01
ADAM_STEP = 10

N_CHIPS = 4
N_DEV = 8

VMEM_LIMIT_V7X = 56 * 1024 * 1024
LANES = 128


def _cparams(sem=None):
    return pltpu.CompilerParams(dimension_semantics=sem, vmem_limit_bytes=VMEM_LIMIT_V7X)


def _sigmoid(x):
    return 0.5 * jnp.tanh(0.5 * x) + 0.5


def _dot(a, b):
    return jnp.dot(a, b, preferred_element_type=F32)


def _dot_nt(a, b):
    return lax.dot_general(a, b, (((1,), (1,)), ((), ())), preferred_element_type=F32)


def _dot_tn(a, b):
    return lax.dot_general(a, b, (((0,), (0,)), ((), ())), preferred_element_type=F32)


def _split2(x):
    hi = x.astype(BF16)
    lo = (x - hi.astype(F32)).astype(BF16)
    return hi, lo


def _split3(x):
    hi = x.astype(BF16)
    r = x - hi.astype(F32)
    mid = r.astype(BF16)
    lo = (r - mid.astype(F32)).astype(BF16)
    return hi, mid, lo


def _matmul(a, b, *, mode, grid, a_spec, b_spec, out_shape, out_spec, tile, name,
            add=None, add_spec=None, add_scale=1.0, after=None, into=None):
    nk = grid[2]
    tm, tn = tile
    dot = {"nn": _dot, "nt": _dot_nt, "tn": _dot_tn}[mode]
    has_add = add is not None
    has_after = after is not None
    has_into = into is not None

    def finish(r, add_ref, o_ref):
        if has_add:
            r = r + add_scale * add_ref[...].astype(F32)
        o_ref[...] = r.astype(o_ref.dtype)

    def body_one(*refs):
        a_ref, b_ref = refs[:2]
        finish(dot(a_ref[...].astype(BF16), b_ref[...].astype(BF16)), refs[2] if has_add else None, refs[-1])

    def body_acc(*refs):
        a_ref, b_ref = refs[:2]
        add_ref = refs[2] if has_add else None
        o_ref, acc_ref = refs[-2:]
        k = pl.program_id(2)

        @pl.when(k == 0)
        def _():
            acc_ref[...] = jnp.zeros_like(acc_ref)

        acc_ref[...] += dot(a_ref[...].astype(BF16), b_ref[...].astype(BF16))

        @pl.when(k == nk - 1)
        def _():
            finish(acc_ref[...], add_ref, o_ref)

    in_specs = ([a_spec, b_spec] + ([add_spec] if has_add else []) + ([_ANY] if has_after else [])
                + ([_ANY] if has_into else []))
    args = (a, b) + ((add,) if has_add else ()) + ((after,) if has_after else ()) + ((into,) if has_into else ())
    return pl.pallas_call(
        body_one if nk == 1 else body_acc, name=name, grid=grid, in_specs=in_specs, out_specs=out_spec,
        out_shape=out_shape,
        input_output_aliases={len(args) - 1: 0} if has_into else {},
        scratch_shapes=[] if nk == 1 else [pltpu.VMEM((tm, tn), F32)],
        compiler_params=_cparams(("parallel", "parallel", "arbitrary")),
    )(*args)


def _pick(n, pref):
    for t in pref:
        if n % t == 0:
            return t
    return n


_TILE_PREF = (1024, 512, 256, 128)
_K_TILE_PREF = (2048,) + _TILE_PREF


def _k_tile(k, out_dtype, has_add):
    return _pick(k, _K_TILE_PREF if (has_add or out_dtype != BF16) else (4096,) + _K_TILE_PREF)


def _mm_nn_sharded(a, w4, out_dtype, name, after=None, col_off=0, n=None, classes=1):
    m, k = a.shape
    _, _, nn = w4.shape
    n = N_CHIPS * nn if n is None else n
    tm, tk = _pick(m // classes, _TILE_PREF), _k_tile(k, out_dtype, False)
    tn = _pick(math.gcd(math.gcd(nn, n), col_off) if col_off else math.gcd(nn, n), _TILE_PREF)
    npb = nn // tn
    co = col_off // tn
    bpc, kb = m // classes // tm, k // tk
    av = a.reshape(m // classes, classes * k)
    out_shape = jax.ShapeDtypeStruct((m, n), out_dtype)
    if tm < _TILE_PREF[0]:
        return _matmul(
            av, w4, mode="nn", grid=(n // tn, m // tm, 1), tile=(tm, tn), name=name,
            a_spec=pl.BlockSpec((tm, k), lambda j, i, kk: (i % bpc, i // bpc)),
            b_spec=pl.BlockSpec((None, k, tn), lambda j, i, kk: ((j + co) // npb, 0, (j + co) % npb)),
            out_shape=out_shape, out_spec=pl.BlockSpec((tm, tn), lambda j, i, kk: (i, j)), after=after)
    return _matmul(
        av, w4, mode="nn", grid=(m // tm, n // tn, kb), tile=(tm, tn), name=name,
        a_spec=pl.BlockSpec((tm, tk), lambda i, j, kk: (i % bpc, (i // bpc) * kb + kk)),
        b_spec=pl.BlockSpec((None, tk, tn), lambda i, j, kk: ((j + co) // npb, kk, (j + co) % npb)),
        out_shape=out_shape, out_spec=pl.BlockSpec((tm, tn), lambda i, j, kk: (i, j)), after=after)


def _mm_nn(a, b, out_dtype, name, b_row_off=0, add=None, add_scale=1.0):
    m, k = a.shape
    _, n = b.shape
    tm, tk, tn = _pick(m, _TILE_PREF), _k_tile(k, out_dtype, add is not None), _pick(n, _TILE_PREF)
    assert b_row_off % tk == 0
    ko = b_row_off // tk
    return _matmul(
        a, b, mode="nn", grid=(m // tm, n // tn, k // tk), tile=(tm, tn), name=name,
        a_spec=pl.BlockSpec((tm, tk), lambda i, j, kk: (i, kk)),
        b_spec=pl.BlockSpec((tk, tn), lambda i, j, kk: (kk + ko, j)),
        out_shape=jax.ShapeDtypeStruct((m, n), out_dtype),
        out_spec=pl.BlockSpec((tm, tn), lambda i, j, kk: (i, j)),
        add=add, add_spec=pl.BlockSpec((tm, tn), lambda i, j, kk: (i, j)), add_scale=add_scale)


def _mm_nt(a, b, out_dtype, name, add=None, add_scale=1.0, n=None, b_row_off=0):
    m, k = a.shape
    n = b.shape[0] if n is None else n
    tm, tk, tn = _pick(m, _TILE_PREF), _k_tile(k, out_dtype, add is not None), _pick(n, _TILE_PREF)
    assert b_row_off % tn == 0
    no = b_row_off // tn
    return _matmul(
        a, b, mode="nt", grid=(m // tm, n // tn, k // tk), tile=(tm, tn), name=name,
        a_spec=pl.BlockSpec((tm, tk), lambda i, j, kk: (i, kk)),
        b_spec=pl.BlockSpec((tn, tk), lambda i, j, kk: (j + no, kk)),
        out_shape=jax.ShapeDtypeStruct((m, n), out_dtype),
        out_spec=pl.BlockSpec((tm, tn), lambda i, j, kk: (i, j)),
        add=add, add_spec=pl.BlockSpec((tm, tn), lambda i, j, kk: (i, j)), add_scale=add_scale)


def _mm_nt_sharded_k(a, w4, out_dtype, name, add=None, add_scale=1.0, after=None):
    m, _ = a.shape
    _, n, kn = w4.shape
    tm, tk, tn = _pick(m, _TILE_PREF), _pick(kn, (2560,) + _TILE_PREF), _pick(n, _TILE_PREF)
    kpb = kn // tk
    return _matmul(
        a, w4, mode="nt", grid=(m // tm, n // tn, N_CHIPS * kpb), tile=(tm, tn), name=name,
        a_spec=pl.BlockSpec((tm, tk), lambda i, j, kk: (i, kk)),
        b_spec=pl.BlockSpec((None, tn, tk), lambda i, j, kk: (kk // kpb, j, kk % kpb)),
        out_shape=jax.ShapeDtypeStruct((m, n), out_dtype),
        out_spec=pl.BlockSpec((tm, tn), lambda i, j, kk: (i, j)),
        add=add, add_spec=pl.BlockSpec((tm, tn), lambda i, j, kk: (i, j)), add_scale=add_scale, after=after)


def _mm_tn(a, b, out_dtype, name, shard_cols=None, out_rows=None, out_row_off=0, into=None, after=None):
    k, m = a.shape
    _, n = b.shape
    nn = n if shard_cols is None else shard_cols
    tm, tk, tn = _pick(m, _TILE_PREF), _k_tile(k, out_dtype, False), _pick(nn, _TILE_PREF)
    if shard_cols is None:
        assert out_row_off % tm == 0
        ro = out_row_off // tm
        out_shape = jax.ShapeDtypeStruct((m if out_rows is None else out_rows, n), out_dtype)
        out_spec = pl.BlockSpec((tm, tn), lambda i, j, kk: (i + ro, j))
    else:
        npb = nn // tn
        out_shape = jax.ShapeDtypeStruct((n // nn, m, nn), out_dtype)
        out_spec = pl.BlockSpec((None, tm, tn), lambda i, j, kk: (j // npb, i, j % npb))
    return _matmul(
        a, b, mode="tn", grid=(m // tm, n // tn, k // tk), tile=(tm, tn), name=name,
        a_spec=pl.BlockSpec((tk, tm), lambda i, j, kk: (kk, i)),
        b_spec=pl.BlockSpec((tk, tn), lambda i, j, kk: (kk, j)),
        out_shape=out_shape, out_spec=out_spec, into=into, after=after)


def _cast_bf16(x, name, after=None):
    r, c = x.shape
    tr = _pick(r, (512, 256, 128, 8))
    extra = [] if after is None else [after]

    def body(x_ref, *rest):
        rest[-1][...] = x_ref[...].astype(BF16)

    return pl.pallas_call(
        body, name=name, grid=(r // tr,),
        in_specs=[pl.BlockSpec((tr, c), lambda i: (i, 0))] + [_ANY] * len(extra),
        out_specs=pl.BlockSpec((tr, c), lambda i: (i, 0)),
        out_shape=jax.ShapeDtypeStruct((r, c), BF16),
        compiler_params=_cparams(("parallel",)),
    )(x, *extra)


def _bucket_tiles():
    qi = np.arange(ATTN_BLOCK)[:, None]
    ki = np.arange(2 * ATTN_BLOCK)[None, :]
    delta = np.clip(ATTN_BLOCK + qi - ki, 0, None)
    tiles = []
    max_exact = NUM_BUCKETS // 2
    for _, dil in ATTN_PATTERNS:
        dist = (delta * dil).astype(np.int32)
        d_f = np.maximum(dist, 1).astype(np.float32)
        large = max_exact + (np.log(d_f / np.float32(max_exact)) / np.float32(math.log(MAX_DISTANCE / max_exact))
                             * np.float32(NUM_BUCKETS - max_exact)).astype(np.int32)
        large = np.minimum(large, NUM_BUCKETS - 1)
        tiles.append(np.where(dist < max_exact, dist, large).astype(np.int32))
    return jnp.asarray(np.stack(tiles))


def _bias_expand(rel_bias, buckets, hpg):
    def body(tab_ref, bk_ref, o_ref):
        g, h = pl.program_id(0), pl.program_id(1)
        bk = bk_ref[...]
        acc = jnp.zeros((ATTN_BLOCK, 2 * ATTN_BLOCK), F32)
        for b in range(NUM_BUCKETS):
            acc = jnp.where(bk == b, tab_ref[b, g * hpg + h], acc)
        o_ref[...] = acc

    return pl.pallas_call(
        body, name="bias_expand", grid=(N_GROUPS_ATTN, hpg),
        in_specs=[pl.BlockSpec(memory_space=pltpu.SMEM),
                  pl.BlockSpec((None, ATTN_BLOCK, 2 * ATTN_BLOCK), lambda g, h: (g, 0, 0))],
        out_specs=pl.BlockSpec((None, None, ATTN_BLOCK, 2 * ATTN_BLOCK), lambda g, h: (g, h, 0, 0)),
        out_shape=jax.ShapeDtypeStruct((N_GROUPS_ATTN, hpg, ATTN_BLOCK, 2 * ATTN_BLOCK), F32),
        compiler_params=_cparams(("parallel", "parallel")),
    )(rel_bias, buckets)


def _bias_reduce(dtiles, buckets, hpg):
    def body(t_ref, bk_ref, o_ref):
        bk = bk_ref[...]
        t = t_ref[...]
        rows = lax.broadcasted_iota(jnp.int32, (NUM_BUCKETS, LANES), 0)
        acc = jnp.zeros((NUM_BUCKETS, LANES), F32)
        for b in range(NUM_BUCKETS):
            s = jnp.sum(jnp.sum(jnp.where(bk == b, t, 0.0), axis=1, keepdims=True), axis=0, keepdims=True)
            acc = jnp.where(rows == b, s, acc)
        o_ref[...] = acc

    return pl.pallas_call(
        body, name="bias_reduce", grid=(N_GROUPS_ATTN, hpg),
        in_specs=[pl.BlockSpec((None, None, ATTN_BLOCK, 2 * ATTN_BLOCK), lambda g, h: (g, h, 0, 0)),
                  pl.BlockSpec((None, ATTN_BLOCK, 2 * ATTN_BLOCK), lambda g, h: (g, 0, 0))],
        out_specs=pl.BlockSpec((None, NUM_BUCKETS, LANES), lambda g, h: (g * hpg + h, 0, 0)),
        out_shape=jax.ShapeDtypeStruct((N_GROUPS_ATTN * hpg, NUM_BUCKETS, LANES), F32),
        compiler_params=_cparams(("parallel", "parallel")),
    )(dtiles, buckets)


def _cast_x3(x, name, after=None):
    r, c = x.shape
    tr = _pick(r, (512, 256, 128, 8))
    extra = [] if after is None else [after]

    def body(x_ref, *rest):
        rest[-1][...] = x_ref[...].astype(BF16)

    return pl.pallas_call(
        body, name=name, grid=(r // tr,),
        in_specs=[pl.BlockSpec((tr, c), lambda i: (i, 0))] + [_ANY] * len(extra),
        out_specs=pl.BlockSpec((None, tr, c), lambda i: (0, i, 0)),
        out_shape=jax.ShapeDtypeStruct((N_GROUPS_ATTN, r, c), BF16),
        compiler_params=_cparams(("parallel",)),
    )(x, *extra)


def _class_copy(x3, slot, dil, name):
    _, s, d = x3.shape
    rows = s // dil
    tm = _pick(rows, (512, 256, 128))
    nbk = rows // tm

    def body(v_ref, x3_ref, o_ref):
        o_ref[...] = v_ref[...]

    return pl.pallas_call(
        body, name=name, grid=(dil, nbk),
        in_specs=[pl.BlockSpec((tm, d), lambda r, i: (i, r)), _ANY],
        out_specs=pl.BlockSpec((None, tm, d), lambda r, i: (slot, r * nbk + i, 0)),
        out_shape=jax.ShapeDtypeStruct(x3.shape, x3.dtype), input_output_aliases={1: 0},
        compiler_params=_cparams(("parallel", "parallel")),
    )(x3[0].reshape(rows, dil * d), x3)


def _in_proj_shard(x3, wa4, kvec, p, d_attn, name, into=None, after=None):
    _, s, d = x3.shape
    _, _, nn = wa4.shape
    tm = _pick(s, _TILE_PREF)
    tn = _pick(math.gcd(nn, 3 * d_attn), _TILE_PREF)
    npb, bpg = nn // tn, 3 * d_attn // tn
    extra = ([] if after is None else [after]) + ([] if into is None else [into])

    def block(k, j):
        return jnp.bitwise_xor(k[0], p) * npb + j

    def slot(k, j):
        jb = block(k, j)
        return jnp.where(jb < N_GROUPS_ATTN * bpg, jb // bpg, 0)

    def body(k_ref, a_ref, b_ref, *rest):
        rest[-1][...] = _dot(a_ref[...], b_ref[...]).astype(BF16)

    grid_spec = pltpu.PrefetchScalarGridSpec(
        num_scalar_prefetch=1, grid=(s // tm, npb),
        in_specs=[pl.BlockSpec((None, tm, d), lambda i, j, k: (slot(k, j), i, 0)),
                  pl.BlockSpec((None, d, tn), lambda i, j, k: (jnp.bitwise_xor(k[0], p), 0, j))]
        + [_ANY] * len(extra),
        out_specs=pl.BlockSpec((tm, tn), lambda i, j, k: (i, block(k, j))))
    return pl.pallas_call(
        body, name=name, grid_spec=grid_spec, out_shape=jax.ShapeDtypeStruct((s, N_CHIPS * nn), BF16),
        input_output_aliases={} if into is None else {2 + len(extra): 0},
        compiler_params=_cparams(("parallel", "parallel")),
    )(kvec, x3, wa4, *extra)


def _attn_valid(n_is_first):
    qi = lax.broadcasted_iota(jnp.int32, (ATTN_BLOCK, 2 * ATTN_BLOCK), 0)
    ki = lax.broadcasted_iota(jnp.int32, (ATTN_BLOCK, 2 * ATTN_BLOCK), 1)
    delta = ATTN_BLOCK + qi - ki
    band = (delta >= 0) & (delta <= ATTN_BLOCK)
    return band & (jnp.logical_not(n_is_first) | (ki >= ATTN_BLOCK))


def _attn_fwd(pg, bias, g, dil, hpg):
    s = pg.shape[0]
    w = hpg * HEAD_DIM
    rows = s // dil
    nb = rows // ATTN_BLOCK
    scale = HEAD_DIM ** -0.5

    def body(q_ref, kc_ref, kp_ref, vc_ref, vp_ref, bias_ref, o_ref, lse_ref):
        valid = _attn_valid(pl.program_id(1) == 0)
        lane = lax.broadcasted_iota(jnp.int32, (ATTN_BLOCK, LANES), 1)
        lse = jnp.zeros((ATTN_BLOCK, LANES), F32)
        for h in range(hpg):
            sl = slice(h * HEAD_DIM, (h + 1) * HEAD_DIM)
            k2 = jnp.concatenate([kp_ref[:, sl], kc_ref[:, sl]], axis=0)
            v2 = jnp.concatenate([vp_ref[:, sl], vc_ref[:, sl]], axis=0)
            sc = _dot_nt(q_ref[:, sl], k2) * scale + bias_ref[h]
            sc = jnp.where(valid, sc, NEG_INF)
            m = jnp.max(sc, axis=1, keepdims=True)
            p = jnp.exp(sc - m)
            l = jnp.sum(p, axis=1, keepdims=True)
            o_ref[:, sl] = (_dot(p.astype(BF16), v2) * (1.0 / l)).astype(BF16)
            lse = jnp.where(lane == h, m + jnp.log(l), lse)
        lse_ref[...] = lse

    def col(off):
        return lambda r, n: (r * nb + n, 3 * g + off)

    def colp(off):
        return lambda r, n: (r * nb + jnp.maximum(n - 1, 0), 3 * g + off)

    blk = (ATTN_BLOCK, w)
    tok = pl.BlockSpec(blk, lambda r, n: (n, r))
    tok1 = pl.BlockSpec((ATTN_BLOCK, LANES), lambda r, n: (n, r))
    o, lse = pl.pallas_call(
        body, name=f"attn_fwd_g{g}", grid=(dil, nb),
        in_specs=[pl.BlockSpec(blk, col(0)), pl.BlockSpec(blk, col(1)), pl.BlockSpec(blk, colp(1)),
                  pl.BlockSpec(blk, col(2)), pl.BlockSpec(blk, colp(2)),
                  pl.BlockSpec((None, hpg, ATTN_BLOCK, 2 * ATTN_BLOCK), lambda r, n: (g, 0, 0, 0))],
        out_specs=[tok, tok1],
        out_shape=[jax.ShapeDtypeStruct((rows, dil * w), BF16), jax.ShapeDtypeStruct((rows, dil * LANES), F32)],
        compiler_params=_cparams(("parallel", "parallel")),
    )(pg, pg, pg, pg, pg, bias)
    return o.reshape(s, w), lse.reshape(s, LANES)


def _attn_combine(os_, lses, pa, hpg):
    s, w = os_[0].shape
    gate_blk = pa.shape[1] // w - 1
    tm = _pick(s, (256, 128))

    def body(o0, o1, o2, l0, l1, l2, gate_ref, o_ref, lse_ref, y_ref):
        a0, a1, a2 = l0[...], l1[...], l2[...]
        m = jnp.maximum(jnp.maximum(a0, a1), a2)
        e0, e1, e2 = jnp.exp(a0 - m), jnp.exp(a1 - m), jnp.exp(a2 - m)
        den = e0 + e1 + e2
        inv = 1.0 / den
        w0, w1, w2 = e0 * inv, e1 * inv, e2 * inv
        lse_ref[...] = m + jnp.log(den)
        for h in range(hpg):
            sl = slice(h * HEAD_DIM, (h + 1) * HEAD_DIM)
            o = (w0[:, h:h + 1] * o0[:, sl].astype(F32) + w1[:, h:h + 1] * o1[:, sl].astype(F32)
                 + w2[:, h:h + 1] * o2[:, sl].astype(F32))
            gate = gate_ref[:, sl].astype(F32)
            o_ref[:, sl] = o.astype(BF16)
            y_ref[:, sl] = (o * (gate * _sigmoid(gate))).astype(BF16)

    spec = pl.BlockSpec((tm, w), lambda i: (i, 0))
    spec1 = pl.BlockSpec((tm, LANES), lambda i: (i, 0))
    return pl.pallas_call(
        body, name="attn_combine", grid=(s // tm,),
        in_specs=[spec] * 3 + [spec1] * 3 + [pl.BlockSpec((tm, w), lambda i: (i, gate_blk))],
        out_specs=[spec, spec1, spec],
        out_shape=[jax.ShapeDtypeStruct((s, w), BF16), jax.ShapeDtypeStruct((s, LANES), F32),
                   jax.ShapeDtypeStruct((s, w), BF16)],
        compiler_params=_cparams(("parallel",)),
    )(*os_, *lses, pa)


def _attn_pre_bwd(dy, o, pa, hpg):
    s, w = dy.shape
    gate_blk = pa.shape[1] // w - 1
    tm = _pick(s, (256, 128))

    def body(dy_ref, o_ref, gate_ref, do_ref, dl_ref, dg_ref):
        gate = gate_ref[...].astype(F32)
        sg = _sigmoid(gate)
        dyv = dy_ref[...].astype(F32)
        ov = o_ref[...].astype(F32)
        do = dyv * (gate * sg)
        do_ref[...] = do.astype(BF16)
        dg_ref[...] = (dyv * ov * (sg * (1.0 + gate * (1.0 - sg)))).astype(BF16)
        prod = do * ov
        lane = lax.broadcasted_iota(jnp.int32, (tm, LANES), 1)
        dl = jnp.zeros((tm, LANES), F32)
        for h in range(hpg):
            sl = slice(h * HEAD_DIM, (h + 1) * HEAD_DIM)
            dl = jnp.where(lane == h, jnp.sum(prod[:, sl], axis=1, keepdims=True), dl)
        dl_ref[...] = dl

    spec = pl.BlockSpec((tm, w), lambda i: (i, 0))
    return pl.pallas_call(
        body, name="attn_pre_bwd", grid=(s // tm,),
        in_specs=[spec, spec, pl.BlockSpec((tm, w), lambda i: (i, gate_blk))],
        out_specs=[spec, pl.BlockSpec((tm, LANES), lambda i: (i, 0)), spec],
        out_shape=[jax.ShapeDtypeStruct((s, w), BF16), jax.ShapeDtypeStruct((s, LANES), F32),
                   jax.ShapeDtypeStruct((s, w), BF16)],
        compiler_params=_cparams(("parallel",)),
    )(dy, o, pa)


def _attn_bwd(pg, bias, do, lse, delta, g, dil, hpg):
    s = pg.shape[0]
    w = hpg * HEAD_DIM
    rows = s // dil
    nb = rows // ATTN_BLOCK
    dov = do.reshape(rows, dil * w)
    lsev, dlv = (t.reshape(rows, dil * LANES) for t in (lse, delta))
    scale = HEAD_DIM ** -0.5

    def body(q_ref, kc_ref, kp_ref, vc_ref, vp_ref, bias_ref, do_ref, lse_ref, dl_ref,
             dq_ref, dk_ref, dv_ref, db_ref, dkc_ref, dvc_ref):
        r, i = pl.program_id(0), pl.program_id(1)
        n = nb - 1 - i
        valid = _attn_valid(n == 0)

        @pl.when((r == 0) & (i == 0))
        def _():
            db_ref[...] = jnp.zeros_like(db_ref)

        @pl.when(i == 0)
        def _():
            dkc_ref[...] = jnp.zeros_like(dkc_ref)
            dvc_ref[...] = jnp.zeros_like(dvc_ref)

        for h in range(hpg):
            sl = slice(h * HEAD_DIM, (h + 1) * HEAD_DIM)
            q = q_ref[:, sl]
            dov_ = do_ref[:, sl]
            k2 = jnp.concatenate([kp_ref[:, sl], kc_ref[:, sl]], axis=0)
            v2 = jnp.concatenate([vp_ref[:, sl], vc_ref[:, sl]], axis=0)
            sc = _dot_nt(q, k2) * scale + bias_ref[h]
            p = jnp.exp(jnp.where(valid, sc - lse_ref[:, h:h + 1], NEG_INF))
            dp = _dot_nt(dov_, v2)
            ds = p * (dp - dl_ref[:, h:h + 1])
            db_ref[h] += ds
            dsb = ds.astype(BF16)
            dq_ref[:, sl] = (_dot(dsb, k2) * scale).astype(BF16)
            dk2 = _dot_tn(dsb, q) * scale
            dv2 = _dot_tn(p.astype(BF16), dov_)
            dk_ref[:, sl] = (dk2[ATTN_BLOCK:] + dkc_ref[:, sl]).astype(BF16)
            dv_ref[:, sl] = (dv2[ATTN_BLOCK:] + dvc_ref[:, sl]).astype(BF16)
            dkc_ref[:, sl] = dk2[:ATTN_BLOCK]
            dvc_ref[:, sl] = dv2[:ATTN_BLOCK]

    def col(off):
        return lambda r, i: (r * nb + nb - 1 - i, 3 * g + off)

    def colp(off):
        return lambda r, i: (r * nb + jnp.maximum(nb - 2 - i, 0), 3 * g + off)

    blk = (ATTN_BLOCK, w)
    tok = pl.BlockSpec(blk, lambda r, i: (nb - 1 - i, r))
    tok1 = pl.BlockSpec((ATTN_BLOCK, LANES), lambda r, i: (nb - 1 - i, r))
    dq, dk, dv, db = pl.pallas_call(
        body, name=f"attn_bwd_g{g}", grid=(dil, nb),
        in_specs=[pl.BlockSpec(blk, col(0)), pl.BlockSpec(blk, col(1)), pl.BlockSpec(blk, colp(1)),
                  pl.BlockSpec(blk, col(2)), pl.BlockSpec(blk, colp(2)),
                  pl.BlockSpec((None, hpg, ATTN_BLOCK, 2 * ATTN_BLOCK), lambda r, i: (g, 0, 0, 0)),
                  tok, tok1, tok1],
        out_specs=[tok, tok, tok,
                   pl.BlockSpec((hpg, ATTN_BLOCK, 2 * ATTN_BLOCK), lambda r, i: (0, 0, 0))],
        out_shape=[jax.ShapeDtypeStruct((rows, dil * w), BF16)] * 3
        + [jax.ShapeDtypeStruct((hpg, ATTN_BLOCK, 2 * ATTN_BLOCK), F32)],
        scratch_shapes=[pltpu.VMEM(blk, F32), pltpu.VMEM(blk, F32)],
        compiler_params=_cparams(("arbitrary", "arbitrary")),
    )(pg, pg, pg, pg, pg, bias, dov, lsev, dlv)
    return dq.reshape(s, w), dk.reshape(s, w), dv.reshape(s, w), db


def _ln_fwd(xin, h, gamma, beta, name, affine_in=None, target=None):
    s, d = xin.shape
    tm = _pick(s, (128,))
    has_aff = affine_in is not None
    has_tgt = target is not None

    def body(*refs):
        it = iter(refs)
        x_ref, h_ref, g_ref, b_ref = next(it), next(it), next(it), next(it)
        if has_aff:
            gi_ref, bi_ref = next(it), next(it)
        if has_tgt:
            t_ref = next(it)
        xh_ref, rs_ref = next(it), next(it)
        x = x_ref[...]
        if has_aff:
            x = x * gi_ref[...] + bi_ref[...]
        u = DEEPNORM_ALPHA * x + h_ref[...]
        mu = jnp.mean(u, axis=1, keepdims=True)
        uc = u - mu
        var = jnp.mean(uc * uc, axis=1, keepdims=True)
        rstd = lax.rsqrt(var + LN_EPS)
        xhat = uc * rstd
        xh_ref[...] = xhat
        rs_ref[...] = rstd
        y = xhat * g_ref[...] + b_ref[...]
        if has_tgt:
            dy_ref, l_ref = next(it), next(it)
            e = y - t_ref[...]
            dy_ref[...] = e * (1.0 / d)
            l_ref[...] = jnp.sum(e * e, axis=1, keepdims=True)
        else:
            y_ref = next(it)
            y_ref[...] = y.astype(BF16)

    row = pl.BlockSpec((tm, d), lambda i: (i, 0))
    vec = pl.BlockSpec((1, d), lambda i: (0, 0))
    one = pl.BlockSpec((tm, 1), lambda i: (i, 0))
    in_specs = [row, row, vec, vec] + ([vec, vec] if has_aff else []) + ([row] if has_tgt else [])
    args = [xin, h, gamma, beta] + (list(affine_in) if has_aff else []) + ([target] if has_tgt else [])
    out_specs = [row, one] + ([row, one] if has_tgt else [row])
    out_shape = [jax.ShapeDtypeStruct((s, d), F32), jax.ShapeDtypeStruct((s, 1), F32)]
    out_shape += ([jax.ShapeDtypeStruct((s, d), F32), jax.ShapeDtypeStruct((s, 1), F32)] if has_tgt
                  else [jax.ShapeDtypeStruct((s, d), BF16)])
    return pl.pallas_call(
        body, name=name, grid=(s // tm,), in_specs=in_specs, out_specs=out_specs, out_shape=out_shape,
        compiler_params=_cparams(("parallel",)),
    )(*args)


def _ln_bwd(dy, xhat, rstd, gamma, name):
    s, d = dy.shape
    tm = _pick(s, (128,))

    def body(dy_ref, xh_ref, rs_ref, g_ref, du_ref, dub_ref, dg_ref, db_ref):
        @pl.when(pl.program_id(0) == 0)
        def _():
            dg_ref[...] = jnp.zeros_like(dg_ref)
            db_ref[...] = jnp.zeros_like(db_ref)

        dyv = dy_ref[...]
        xh = xh_ref[...]
        dg_ref[...] += jnp.sum(dyv * xh, axis=0, keepdims=True)
        db_ref[...] += jnp.sum(dyv, axis=0, keepdims=True)
        dxh = dyv * g_ref[...]
        m1 = jnp.mean(dxh, axis=1, keepdims=True)
        m2 = jnp.mean(dxh * xh, axis=1, keepdims=True)
        du = rs_ref[...] * (dxh - m1 - xh * m2)
        du_ref[...] = du
        dub_ref[...] = du.astype(BF16)

    row = pl.BlockSpec((tm, d), lambda i: (i, 0))
    vec = pl.BlockSpec((1, d), lambda i: (0, 0))
    one = pl.BlockSpec((tm, 1), lambda i: (i, 0))
    return pl.pallas_call(
        body, name=name, grid=(s // tm,), in_specs=[row, row, one, vec],
        out_specs=[row, row, vec, vec],
        out_shape=[jax.ShapeDtypeStruct((s, d), F32), jax.ShapeDtypeStruct((s, d), BF16),
                   jax.ShapeDtypeStruct((1, d), F32), jax.ShapeDtypeStruct((1, d), F32)],
        compiler_params=_cparams(("arbitrary",)),
    )(dy, xhat, rstd, gamma)


_HALO = 16
_STRIP = 16


def _strips(tm, fn, init, reverse=False):
    n = tm // _STRIP

    def step(i, carry):
        s_ = n - 1 - i if reverse else i
        return fn(pl.ds(pl.multiple_of(s_ * _STRIP, _STRIP), _STRIP), carry)

    return lax.fori_loop(0, n, step, init)


def _fold8(t):
    return t[0:8] + t[8:16]


def _conv_taps(ext, tm, w_ref):
    acc = None
    for k in range(CONV_WIDTH):
        lo = _HALO - (CONV_WIDTH - 1) + k
        term = w_ref[k:k + 1, :] * ext[lo:lo + tm, :]
        acc = term if acc is None else acc + term
    return acc


def _conv_strip(prev, cur, w_ref):
    ext = jnp.concatenate([prev, cur], axis=0)
    acc, taps = None, []
    for k in range(CONV_WIDTH):
        lo = _STRIP - (CONV_WIDTH - 1) + k
        taps.append(ext[lo:lo + _STRIP, :])
        term = w_ref[k:k + 1, :] * taps[k]
        acc = term if acc is None else acc + term
    return acc, taps


def _conv_fwd(pzx, conv_w, conv_b, d_inner):
    s, _ = pzx.shape
    cd = conv_w.shape[1]
    tm = _pick(s, (512, 256, 128))
    tc = _pick(cd, (1024, 512, 256, 128))
    off = d_inner // tc
    hb = tm // _HALO

    def body(x_ref, p_ref, w_ref, b_ref, o_ref):
        prev = jnp.where(pl.program_id(0) > 0, p_ref[...].astype(F32), 0.0)
        ext = jnp.concatenate([prev, x_ref[...].astype(F32)], axis=0)
        pre = _conv_taps(ext, tm, w_ref) + b_ref[...]
        o_ref[...] = (pre * _sigmoid(pre)).astype(BF16)

    return pl.pallas_call(
        body, name="conv_fwd", grid=(s // tm, cd // tc),
        in_specs=[pl.BlockSpec((tm, tc), lambda i, j: (i, off + j)),
                  pl.BlockSpec((_HALO, tc), lambda i, j: (jnp.maximum(i * hb - 1, 0), off + j)),
                  pl.BlockSpec((CONV_WIDTH, tc), lambda i, j: (0, j)),
                  pl.BlockSpec((1, tc), lambda i, j: (0, j))],
        out_specs=pl.BlockSpec((tm, tc), lambda i, j: (i, j)),
        out_shape=jax.ShapeDtypeStruct((s, cd), BF16),
        compiler_params=_cparams(("parallel", "parallel")),
    )(pzx, pzx, conv_w, conv_b)


def _conv_bwd_a(pzx, dxbc, conv_w, conv_b, d_inner):
    s, _ = pzx.shape
    cd = conv_w.shape[1]
    tm = _pick(s, (512, 256, 128))
    tc = _pick(cd, (1024, 512, 256, 128))
    off = d_inner // tc
    hb = tm // _HALO

    def body(x_ref, p_ref, d_ref, w_ref, b_ref, o_ref, dw_ref, db_ref, acc_ref):
        @pl.when(pl.program_id(1) == 0)
        def _():
            dw_ref[...] = jnp.zeros_like(dw_ref)
            db_ref[...] = jnp.zeros_like(db_ref)

        acc_ref[...] = jnp.zeros_like(acc_ref)

        def strip(rows, prev):
            cur = x_ref[rows, :].astype(F32)
            pre, taps = _conv_strip(prev, cur, w_ref)
            pre = pre + b_ref[...]
            sg = _sigmoid(pre)
            dpre = d_ref[rows, :].astype(F32) * (sg * (1.0 + pre * (1.0 - sg)))
            o_ref[rows, :] = dpre
            for k in range(CONV_WIDTH):
                acc_ref[k] += _fold8(dpre * taps[k])
            acc_ref[CONV_WIDTH] += _fold8(dpre)
            return cur

        _strips(tm, strip, jnp.where(pl.program_id(1) > 0, p_ref[...].astype(F32), 0.0))
        for k in range(CONV_WIDTH):
            dw_ref[k:k + 1, :] += jnp.sum(acc_ref[k], axis=0, keepdims=True)
        db_ref[...] += jnp.sum(acc_ref[CONV_WIDTH], axis=0, keepdims=True)

    return pl.pallas_call(
        body, name="conv_bwd_a", grid=(cd // tc, s // tm),
        in_specs=[pl.BlockSpec((tm, tc), lambda j, i: (i, off + j)),
                  pl.BlockSpec((_HALO, tc), lambda j, i: (jnp.maximum(i * hb - 1, 0), off + j)),
                  pl.BlockSpec((tm, tc), lambda j, i: (i, j)),
                  pl.BlockSpec((CONV_WIDTH, tc), lambda j, i: (0, j)),
                  pl.BlockSpec((1, tc), lambda j, i: (0, j))],
        out_specs=[pl.BlockSpec((tm, tc), lambda j, i: (i, j)),
                   pl.BlockSpec((CONV_WIDTH, tc), lambda j, i: (0, j)),
                   pl.BlockSpec((1, tc), lambda j, i: (0, j))],
        out_shape=[jax.ShapeDtypeStruct((s, cd), F32), jax.ShapeDtypeStruct((CONV_WIDTH, cd), F32),
                   jax.ShapeDtypeStruct((1, cd), F32)],
        scratch_shapes=[pltpu.VMEM((CONV_WIDTH + 1, 8, tc), F32)],
        compiler_params=_cparams(("parallel", "arbitrary")),
    )(pzx, pzx, dxbc, conv_w, conv_b)


def _conv_bwd_b(dpre, conv_w, into, col_off):
    s, cd = dpre.shape
    tm = _pick(s, (512, 256, 128))
    tc = _pick(cd, (1024, 512, 256, 128))
    hb = tm // 8
    nrb = s // tm
    assert col_off % tc == 0
    co = col_off // tc

    def body(x_ref, nx_ref, w_ref, into_ref, o_ref):
        nxt = jnp.where(pl.program_id(0) < nrb - 1, nx_ref[...], 0.0)
        ext = jnp.concatenate([x_ref[...], nxt], axis=0)
        acc = None
        for k in range(CONV_WIDTH):
            lo = CONV_WIDTH - 1 - k
            term = w_ref[k:k + 1, :] * ext[lo:lo + tm, :]
            acc = term if acc is None else acc + term
        o_ref[...] = acc.astype(BF16)

    return pl.pallas_call(
        body, name="conv_bwd_b", grid=(nrb, cd // tc),
        in_specs=[pl.BlockSpec((tm, tc), lambda i, j: (i, j)),
                  pl.BlockSpec((8, tc), lambda i, j: (jnp.minimum((i + 1) * hb, s // 8 - 1), j)),
                  pl.BlockSpec((CONV_WIDTH, tc), lambda i, j: (0, j)), _ANY],
        out_specs=pl.BlockSpec((tm, tc), lambda i, j: (i, j + co)),
        out_shape=jax.ShapeDtypeStruct(into.shape, BF16),
        input_output_aliases={3: 0},
        compiler_params=_cparams(("parallel", "parallel")),
    )(dpre, dpre, conv_w, into)


def _expand_matrix():
    e = np.zeros((LANES, SSM_GROUP_WIDTH), np.float32)
    for h in range(HEADS_PER_SSM_GROUP):
        e[h, h * SSM_HEAD_DIM:(h + 1) * SSM_HEAD_DIM] = 1.0
    return jnp.asarray(e, BF16)


def _expand(t, e):
    return _dot(t.astype(BF16), e)


def _segsum(v, e):
    hi, lo = _split2(v)
    return _dot_nt(hi, e) + _dot_nt(lo, e)


def _tri_dot(tri, x):
    hi, mid, lo = _split3(x)
    return _dot(tri, hi) + _dot(tri, mid) + _dot(tri, lo)


def _ssd_common(dtp_ref, a_ref, dtb_ref, x_ref, e):
    li = lax.broadcasted_iota(jnp.int32, (CHUNK, CHUNK), 0)
    si = lax.broadcasted_iota(jnp.int32, (CHUNK, CHUNK), 1)
    causal = li >= si
    tril = causal.astype(BF16)
    raw = dtp_ref[...] + dtb_ref[...]
    dt = jnp.maximum(raw, 0.0) + jnp.log(1.0 + jnp.exp(-jnp.abs(raw)))
    head_lane = lax.broadcasted_iota(jnp.int32, (1, LANES), 1) < HEADS_PER_SSM_GROUP
    a = jnp.where(head_lane, -jnp.exp(a_ref[...]), 0.0)
    a_cum = _tri_dot(tril, dt * a)
    a_cum_t = a_cum.T
    e_a = jnp.exp(a_cum)
    to_end = jnp.exp(a_cum[CHUNK - 1:CHUNK, :] - a_cum)
    x = x_ref[...].astype(F32)
    dt_e = _expand(dt, e)
    return dict(causal=causal, raw=raw, dt=dt, a=a, a_cum=a_cum, a_cum_t=a_cum_t, e_a=e_a,
                to_end=to_end, x=x, dt_e=dt_e, xdt=x * dt_e, e_a_e=_expand(e_a, e),
                to_end_e=_expand(to_end, e))


def _decay(q, h):
    seg = q["a_cum"][:, h:h + 1] - q["a_cum_t"][h:h + 1, :]
    return jnp.exp(jnp.where(q["causal"], seg, -jnp.inf))


def _ssd_specs(ng, d_inner, rev, nc):
    cidx = (lambda i: nc - 1 - i) if rev else (lambda i: i)
    boff = d_inner // D_STATE
    return dict(
        xs=pl.BlockSpec((CHUNK, SSM_GROUP_WIDTH), lambda g, i: (cidx(i), g)),
        b=pl.BlockSpec((CHUNK, D_STATE), lambda g, i: (cidx(i), boff + g)),
        c=pl.BlockSpec((CHUNK, D_STATE), lambda g, i: (cidx(i), boff + ng + g)),
        dtp=pl.BlockSpec((None, CHUNK, LANES), lambda g, i: (g, cidx(i), 0)),
        vec=pl.BlockSpec((None, 1, LANES), lambda g, i: (g, 0, 0)),
        wide=pl.BlockSpec((None, 1, SSM_GROUP_WIDTH), lambda g, i: (g, 0, 0)),
        e=pl.BlockSpec((LANES, SSM_GROUP_WIDTH), lambda g, i: (0, 0)),
        st=pl.BlockSpec((None, None, D_STATE, SSM_GROUP_WIDTH), lambda g, i: (g, cidx(i), 0, 0)),
        tok=pl.BlockSpec((CHUNK, SSM_GROUP_WIDTH), lambda g, i: (cidx(i), g)),
        bc_out=pl.BlockSpec((CHUNK, D_STATE), lambda g, i: (cidx(i), g)),
    )


def _ssd_fwd(xbc, dtp, a_pad, dtb_pad, dsk_e, e, d_inner):
    s = xbc.shape[0]
    ng = d_inner // SSM_GROUP_WIDTH
    nc = s // CHUNK

    def body(x_ref, b_ref, c_ref, dtp_ref, a_ref, dtb_ref, dsk_ref, e_ref, y_ref, st_ref, state):
        lane = lax.broadcasted_iota(jnp.int32, (CHUNK, LANES), 1)
        @pl.when(pl.program_id(1) == 0)
        def _():
            state[...] = jnp.zeros_like(state)

        ev = e_ref[...]
        q = _ssd_common(dtp_ref, a_ref, dtb_ref, x_ref, ev)
        bm, cm = b_ref[...], c_ref[...]
        cb = _dot_nt(cm, bm)
        s0 = state[...]
        st_ref[...] = s0
        y = _dot(cm, s0.astype(BF16)) * q["e_a_e"] + dsk_ref[...] * q["x"]
        xdt = q["xdt"]
        left = lane[:, :] < SSM_HEAD_DIM
        for j in range(HEADS_PER_SSM_GROUP // 2):
            sl = slice(j * LANES, (j + 1) * LANES)
            x2 = xdt[:, sl]
            m0 = (cb * _decay(q, 2 * j)).astype(BF16)
            m1 = (cb * _decay(q, 2 * j + 1)).astype(BF16)
            mcat = jnp.concatenate([m0, m1], axis=1)
            xbd = jnp.concatenate([jnp.where(left, x2, 0.0), jnp.where(left, 0.0, x2)], axis=0).astype(BF16)
            y_ref[:, sl] = (y[:, sl] + _dot(mcat, xbd)).astype(BF16)
        state[...] = s0 * q["e_a_e"][CHUNK - 1:CHUNK, :] + _dot_tn(bm, (q["to_end_e"] * xdt).astype(BF16))

    sp = _ssd_specs(ng, d_inner, False, nc)
    return pl.pallas_call(
        body, name="ssd_fwd", grid=(ng, nc),
        in_specs=[sp["xs"], sp["b"], sp["c"], sp["dtp"], sp["vec"], sp["vec"], sp["wide"], sp["e"]],
        out_specs=[sp["tok"], sp["st"]],
        out_shape=[jax.ShapeDtypeStruct((s, d_inner), BF16),
                   jax.ShapeDtypeStruct((ng, nc, D_STATE, SSM_GROUP_WIDTH), F32)],
        scratch_shapes=[pltpu.VMEM((D_STATE, SSM_GROUP_WIDTH), F32)],
        compiler_params=_cparams(("parallel", "arbitrary")),
    )(xbc, xbc, xbc, dtp, a_pad, dtb_pad, dsk_e, e)


def _ssd_bwd(xbc, dtp, a_pad, dtb_pad, dsk_e, e, states, dy, d_inner):
    s = xbc.shape[0]
    ng = d_inner // SSM_GROUP_WIDTH
    nc = s // CHUNK

    def body(x_ref, b_ref, c_ref, dtp_ref, a_ref, dtb_ref, dsk_ref, e_ref, st_ref, dy_ref,
             dx_ref, db_ref, dc_ref, ddt_ref, da_ref, ddtb_ref, dd_ref, dstate):
        lane = lax.broadcasted_iota(jnp.int32, (CHUNK, LANES), 1)
        sub = lax.broadcasted_iota(jnp.int32, (CHUNK, LANES), 0)
        @pl.when(pl.program_id(1) == 0)
        def _():
            dstate[...] = jnp.zeros_like(dstate)
            da_ref[...] = jnp.zeros_like(da_ref)
            ddtb_ref[...] = jnp.zeros_like(ddtb_ref)
            dd_ref[...] = jnp.zeros_like(dd_ref)

        ev = e_ref[...]
        q = _ssd_common(dtp_ref, a_ref, dtb_ref, x_ref, ev)
        bm, cm = b_ref[...], c_ref[...]
        cb = _dot_nt(cm, bm)
        x, xdt, e_a_e, to_end_e = q["x"], q["xdt"], q["e_a_e"], q["to_end_e"]
        s0 = st_ref[...]
        s0b = s0.astype(BF16)
        ds1 = dstate[...]
        ds1b = ds1.astype(BF16)
        dy = dy_ref[...].astype(F32)
        e_last_e = e_a_e[CHUNK - 1:CHUNK, :]

        dye = dy * e_a_e
        dyeb = dye.astype(BF16)
        cs0 = _dot(cm, s0b)
        dc = _dot_nt(dyeb, s0b)
        dstate[...] = e_last_e * ds1 + _dot_tn(cm, dyeb)
        da_col = _segsum(dye * cs0, ev)

        gmat = _dot(bm, ds1b)
        dxdt = to_end_e * gmat
        dte = _segsum(xdt * gmat, ev) * q["to_end"]
        db = _dot_nt((to_end_e * xdt).astype(BF16), ds1b)
        da_col = da_col - dte
        last_row = (jnp.sum(dte, axis=0, keepdims=True)
                    + q["e_a"][CHUNK - 1:CHUNK, :] * jnp.sum(_segsum(s0 * ds1, ev), axis=0, keepdims=True))

        left = lane < SSM_HEAD_DIM
        dcb = jnp.zeros((CHUNK, CHUNK), F32)
        row_acc = jnp.zeros((CHUNK, LANES), F32)
        for j in range(HEADS_PER_SSM_GROUP // 2):
            sl = slice(j * LANES, (j + 1) * LANES)
            x2 = xdt[:, sl].astype(BF16)
            dy2 = dy[:, sl]
            dyl = jnp.where(left, dy2, 0.0).astype(BF16)
            dyr = jnp.where(left, 0.0, dy2).astype(BF16)
            ms = []
            for hh, dyh in ((0, dyl), (1, dyr)):
                h = 2 * j + hh
                dec = _decay(q, h)
                m = cb * dec
                dm = _dot_nt(dyh, x2)
                dcb = dcb + dm * dec
                dseg = dm * m
                da_col = da_col + jnp.where(lane == h, jnp.sum(dseg, axis=1, keepdims=True), 0.0)
                row_acc = row_acc + jnp.where(sub == h, jnp.sum(dseg, axis=0, keepdims=True), 0.0)
                ms.append(m.astype(BF16))
            mst = jnp.concatenate(ms, axis=0)
            dyst = jnp.concatenate([dyl, dyr], axis=0)
            d2 = dxdt[:, sl] + _dot_tn(mst, dyst)
            dx_ref[:, sl] = (d2 * q["dt_e"][:, sl] + dsk_ref[:, sl] * dy2).astype(BF16)
            dxdt_x = d2 * x[:, sl]
            if j == 0:
                parts = [dxdt_x]
            else:
                parts.append(dxdt_x)
        dcbb = dcb.astype(BF16)
        dc_ref[...] = (dc + _dot(dcbb, bm)).astype(BF16)
        db_ref[...] = (db + _dot_tn(dcbb, cm)).astype(BF16)

        d_a = da_col - row_acc.T + jnp.where(sub == CHUNK - 1, last_row, 0.0)
        triu = (lax.broadcasted_iota(jnp.int32, (CHUNK, CHUNK), 1)
                >= lax.broadcasted_iota(jnp.int32, (CHUNK, CHUNK), 0)).astype(BF16)
        d_dta = _tri_dot(triu, d_a)
        ddt = d_dta * q["a"] + _segsum(jnp.concatenate(parts, axis=1), ev)
        ddt_raw = ddt * _sigmoid(q["raw"])
        ddt_ref[...] = ddt_raw
        da_ref[...] += jnp.sum(d_dta * q["dt"], axis=0, keepdims=True) * q["a"]
        ddtb_ref[...] += jnp.sum(ddt_raw, axis=0, keepdims=True)
        dd_ref[...] += jnp.sum(dy * x, axis=0, keepdims=True)

    sp = _ssd_specs(ng, d_inner, True, nc)
    return pl.pallas_call(
        body, name="ssd_bwd", grid=(ng, nc),
        in_specs=[sp["xs"], sp["b"], sp["c"], sp["dtp"], sp["vec"], sp["vec"], sp["wide"], sp["e"],
                  sp["st"], sp["tok"]],
        out_specs=[sp["tok"], sp["bc_out"], sp["bc_out"], sp["dtp"], sp["vec"], sp["vec"], sp["wide"]],
        out_shape=[jax.ShapeDtypeStruct(xbc.shape, BF16),
                   jax.ShapeDtypeStruct((s, ng * D_STATE), BF16),
                   jax.ShapeDtypeStruct((s, ng * D_STATE), BF16),
                   jax.ShapeDtypeStruct((ng, s, LANES), F32),
                   jax.ShapeDtypeStruct((ng, 1, LANES), F32),
                   jax.ShapeDtypeStruct((ng, 1, LANES), F32),
                   jax.ShapeDtypeStruct((ng, 1, SSM_GROUP_WIDTH), F32)],
        scratch_shapes=[pltpu.VMEM((D_STATE, SSM_GROUP_WIDTH), F32)],
        compiler_params=_cparams(("parallel", "arbitrary")),
    )(xbc, xbc, xbc, dtp, a_pad, dtb_pad, dsk_e, e, states, dy)


def _gate_norm_fwd(y, pzx, norm_w):
    s, di = y.shape
    ng = di // SSM_GROUP_WIDTH
    tm = _pick(s, (512, 256, 128))

    def body(y_ref, z_ref, w_ref, o_ref):
        z = z_ref[...].astype(F32)
        y2 = y_ref[...].astype(F32) * (z * _sigmoid(z))
        r = lax.rsqrt(jnp.mean(y2 * y2, axis=1, keepdims=True) + RMS_EPS)
        o_ref[...] = (y2 * r * w_ref[...]).astype(BF16)

    blk = pl.BlockSpec((tm, SSM_GROUP_WIDTH), lambda i, g: (i, g))
    return pl.pallas_call(
        body, name="gate_norm_fwd", grid=(s // tm, ng),
        in_specs=[blk, blk, pl.BlockSpec((1, SSM_GROUP_WIDTH), lambda i, g: (0, g))],
        out_specs=blk, out_shape=jax.ShapeDtypeStruct((s, di), BF16),
        compiler_params=_cparams(("parallel", "parallel")),
    )(y, pzx, norm_w)


def _gate_norm_bwd(dy3, y, pzx, norm_w):
    s, di = y.shape
    ng = di // SSM_GROUP_WIDTH
    tm = _pick(s, (512, 256, 128))

    def body(d_ref, y_ref, z_ref, w_ref, dy_ref, dz_ref, dw_ref):
        @pl.when(pl.program_id(1) == 0)
        def _():
            dw_ref[...] = jnp.zeros_like(dw_ref)

        z = z_ref[...].astype(F32)
        yv = y_ref[...].astype(F32)
        sg = _sigmoid(z)
        sz = z * sg
        y2 = yv * sz
        r = lax.rsqrt(jnp.mean(y2 * y2, axis=1, keepdims=True) + RMS_EPS)
        nrm = y2 * r
        d3 = d_ref[...].astype(F32)
        dw_ref[...] += jnp.sum(d3 * nrm, axis=0, keepdims=True)
        dn = d3 * w_ref[...]
        dy2 = r * (dn - nrm * jnp.mean(dn * nrm, axis=1, keepdims=True))
        dy_ref[...] = (dy2 * sz).astype(BF16)
        dz_ref[...] = (dy2 * yv * (sg * (1.0 + z * (1.0 - sg)))).astype(BF16)

    blk = pl.BlockSpec((tm, SSM_GROUP_WIDTH), lambda g, i: (i, g))
    vec = pl.BlockSpec((1, SSM_GROUP_WIDTH), lambda g, i: (0, g))
    return pl.pallas_call(
        body, name="gate_norm_bwd", grid=(ng, s // tm),
        in_specs=[blk, blk, blk, vec], out_specs=[blk, blk, vec],
        out_shape=[jax.ShapeDtypeStruct((s, di), BF16), jax.ShapeDtypeStruct(pzx.shape, BF16),
                   jax.ShapeDtypeStruct((1, di), F32)],
        compiler_params=_cparams(("parallel", "arbitrary")),
    )(dy3, y, pzx, norm_w)


_ANY = pl.BlockSpec(memory_space=pl.ANY)


def _place():
    x, y, c = lax.axis_index("x"), lax.axis_index("y"), lax.axis_index("c")
    chips = [(1 - x, y), (x, 1 - y), (1 - x, 1 - y)]
    return x, y, c, chips


def _cast_to_slot(x, kvec, name, after=None):
    r, cn = x.shape
    tr = _rows_per_block(r, cn)
    extra = [] if after is None else [after]

    def body(k_ref, x_ref, *rest):
        rest[-1][...] = x_ref[...].astype(BF16)

    grid_spec = pltpu.PrefetchScalarGridSpec(
        num_scalar_prefetch=1, grid=(r // tr,),
        in_specs=[pl.BlockSpec((tr, cn), lambda i, k: (i, 0))] + [_ANY] * len(extra),
        out_specs=pl.BlockSpec((None, tr, cn), lambda i, k: (k[0], i, 0)))
    return pl.pallas_call(
        body, name=name, grid_spec=grid_spec, out_shape=jax.ShapeDtypeStruct((N_CHIPS, r, cn), BF16),
        compiler_params=_cparams(("parallel",)),
    )(kvec, x, *extra)


def _swap_halves(gs, name, after=None):
    n = len(gs)
    extra = [] if after is None else [after]

    def body(*refs):
        ins, outs = refs[:n], refs[n + len(extra):2 * n + len(extra)]
        send_sems, recv_sems = refs[2 * n + len(extra):]
        x, y, c, _ = _place()
        cps = []
        for w in range(n):
            hr = gs[w].shape[1] // 2
            cp = pltpu.make_async_remote_copy(
                src_ref=ins[w].at[:, pl.ds((1 - c) * hr, hr)], dst_ref=outs[w],
                send_sem=send_sems.at[w], recv_sem=recv_sems.at[w],
                device_id=(x, y, 1 - c), device_id_type=MESH)
            cp.start()
            cps.append(cp)
        for cp in cps:
            cp.wait()

    return pl.pallas_call(
        body, name=name,
        in_specs=[_ANY] * (n + len(extra)), out_specs=[_ANY] * n,
        out_shape=[jax.ShapeDtypeStruct((g.shape[0], g.shape[1] // 2, g.shape[2]), g.dtype) for g in gs],
        scratch_shapes=[pltpu.SemaphoreType.DMA((n,)), pltpu.SemaphoreType.DMA((n,))],
    )(*gs, *extra)


def _join_halves(fs, name, after=None):
    n = len(fs)
    extra = [] if after is None else [after]

    def body(*refs):
        outs = refs[n + len(extra):2 * n + len(extra)]
        send_sems, recv_sems = refs[2 * n + len(extra):]
        x, y, c, _ = _place()

        def copy(w, hc):
            hr = fs[w].shape[0] // 2
            rows = outs[w].at[pl.ds(hc * hr, hr)]
            return pltpu.make_async_remote_copy(
                src_ref=rows, dst_ref=rows, send_sem=send_sems.at[w], recv_sem=recv_sems.at[w],
                device_id=(x, y, 1 - c), device_id_type=MESH)

        cps = [copy(w, c) for w in range(n)]
        for cp in cps:
            cp.start()
        for w in range(n):
            copy(w, 1 - c).wait_recv()
        for cp in cps:
            cp.wait_send()

    return pl.pallas_call(
        body, name=name,
        in_specs=[_ANY] * (n + len(extra)), out_specs=[_ANY] * n,
        out_shape=[jax.ShapeDtypeStruct(f.shape, f.dtype) for f in fs],
        input_output_aliases={w: w for w in range(n)},
        scratch_shapes=[pltpu.SemaphoreType.DMA((n,)), pltpu.SemaphoreType.DMA((n,))],
    )(*fs, *extra)


_HBM_SPEC = pl.BlockSpec(memory_space=pltpu.HBM)
_SEM_SPEC = pl.BlockSpec(memory_space=pltpu.SEMAPHORE)
_VMEM_SPEC = pl.BlockSpec(memory_space=pltpu.VMEM)
_EFFECT = pltpu.SideEffectType.DATAFLOW_SIDE_EFFECTING
_TOKEN = jax.ShapeDtypeStruct((8, LANES), F32)


def _hbm(a):
    return pltpu.with_memory_space_constraint(a, pltpu.HBM)


_NEIGHBOURS, _DIAGONAL, _ALL_CHIPS = (0, 1), (2,), (0, 1, 2)


def _gather_copies(bufs, refs, send_sems, recv_sems, forward, arrivals=True, which=_ALL_CHIPS):
    x, y, c, chips = _place()
    k = 2 * x + y
    out, arrive = [], []
    for w, ref in enumerate(refs):
        hr = bufs[w].shape[1] // 2
        for j, (cx, cy) in enumerate(chips):
            if j not in which:
                continue
            kj = 2 * cx + cy
            slot_out, slot_in, half_in = (kj, kj, 1 - c) if forward else (k, kj, c)
            to = (x, y, 1 - c) if forward else (cx, cy, c)
            src = ref.at[slot_out, pl.ds(c * hr, hr)]
            land = ref.at[slot_in, pl.ds(half_in * hr, hr)]
            out.append(pltpu.make_async_remote_copy(
                src_ref=src, dst_ref=src, send_sem=send_sems.at[3 * w + j], recv_sem=recv_sems.at[3 * w + j],
                device_id=to, device_id_type=MESH))
            if arrivals:
                arrive.append(pltpu.make_async_remote_copy(
                    src_ref=land, dst_ref=land, send_sem=send_sems.at[3 * w + j], recv_sem=recv_sems.at[3 * w + j],
                    device_id=to, device_id_type=MESH))
    return out, arrive


def _gather_start(bufs, forward, name, after=None, which=_ALL_CHIPS):
    n = len(bufs)
    extra = [] if after is None else [after]

    def body(*refs):
        ins = refs[:n]
        send_sems, recv_sems = refs[n + len(extra)], refs[n + len(extra) + 1]
        token = refs[-1]
        out, _ = _gather_copies(bufs, ins, send_sems, recv_sems, forward, arrivals=False, which=which)
        for cp in out:
            cp.start()
        token[...] = jnp.zeros_like(token)

    res = pl.pallas_call(
        body, name=name,
        out_shape=(pltpu.SemaphoreType.DMA((3 * n,)), pltpu.SemaphoreType.DMA((3 * n,)))
        + tuple(pltpu.HBM(b.shape, b.dtype) for b in bufs) + (_TOKEN,),
        in_specs=(_HBM_SPEC,) * n + (_ANY,) * len(extra),
        out_specs=(_SEM_SPEC, _SEM_SPEC) + (_HBM_SPEC,) * n + (_VMEM_SPEC,),
        input_output_aliases={w: 2 + w for w in range(n)},
        compiler_params=pltpu.CompilerParams(has_side_effects=_EFFECT),
    )(*[_hbm(b) for b in bufs], *extra)
    return res[0], res[1], list(res[2:2 + n]), res[-1]


def _gather_wait(bufs, send_sems, recv_sems, after, forward, name, which=_ALL_CHIPS):
    n = len(bufs)

    def body(*refs):
        ins = refs[:n]
        send_sems, recv_sems = refs[n], refs[n + 1]
        out, arrive = _gather_copies(bufs, ins, send_sems, recv_sems, forward, which=which)
        for cp in out:
            cp.wait_send()
        for cp in arrive:
            cp.wait_recv()

    res = pl.pallas_call(
        body, name=name,
        out_shape=tuple(pltpu.HBM(b.shape, b.dtype) for b in bufs),
        in_specs=(_HBM_SPEC,) * n + (_SEM_SPEC, _SEM_SPEC, _ANY), out_specs=(_HBM_SPEC,) * n,
        input_output_aliases={w: w for w in range(n)},
        compiler_params=pltpu.CompilerParams(has_side_effects=_EFFECT),
    )(*bufs, send_sems, recv_sems, after)
    return list(res)


def _swap_copy(g_ref, land_ref, send_sems, recv_sems):
    x, y, c, _ = _place()
    hr = g_ref.shape[1] // 2
    return pltpu.make_async_remote_copy(
        src_ref=g_ref.at[:, pl.ds((1 - c) * hr, hr)], dst_ref=land_ref, send_sem=send_sems.at[0],
        recv_sem=recv_sems.at[0], device_id=(x, y, 1 - c), device_id_type=MESH)


def _swap_start(g4, name):
    ns, r, cn = g4.shape

    def body(g_ref, land_ref, send_sems, recv_sems, g_thru, land_thru, token):
        _swap_copy(g_ref, land_ref, send_sems, recv_sems).start()
        token[...] = jnp.zeros_like(token)

    return pl.pallas_call(
        body, name=name,
        out_shape=(pltpu.SemaphoreType.DMA((1,)), pltpu.SemaphoreType.DMA((1,)),
                   pltpu.HBM(g4.shape, g4.dtype), pltpu.HBM((ns, r // 2, cn), g4.dtype), _TOKEN),
        in_specs=(_HBM_SPEC, _HBM_SPEC), out_specs=(_SEM_SPEC, _SEM_SPEC, _HBM_SPEC, _HBM_SPEC, _VMEM_SPEC),
        input_output_aliases={0: 2, 1: 3},
        compiler_params=pltpu.CompilerParams(has_side_effects=_EFFECT),
    )(_hbm(g4), _hbm(lax.empty((ns, r // 2, cn), g4.dtype)))


def _swap_wait(send_sems, recv_sems, g_thru, land_thru, after, name):
    def body(g_ref, land_ref, send_sems, recv_sems, after_ref, g_out, land_out):
        cp = _swap_copy(g_ref, land_ref, send_sems, recv_sems)
        cp.wait_send()
        cp.wait_recv()

    return pl.pallas_call(
        body, name=name,
        out_shape=(pltpu.HBM(g_thru.shape, g_thru.dtype), pltpu.HBM(land_thru.shape, land_thru.dtype)),
        in_specs=(_HBM_SPEC, _HBM_SPEC, _SEM_SPEC, _SEM_SPEC, _ANY), out_specs=(_HBM_SPEC, _HBM_SPEC),
        input_output_aliases={0: 0, 1: 1},
        compiler_params=pltpu.CompilerParams(has_side_effects=_EFFECT),
    )(g_thru, land_thru, send_sems, recv_sems, after)


def _join_copy(f_ref, send_sems, recv_sems, half):
    x, y, c, _ = _place()
    hr = f_ref.shape[0] // 2
    rows = f_ref.at[pl.ds(half * hr, hr)]
    return pltpu.make_async_remote_copy(
        src_ref=rows, dst_ref=rows, send_sem=send_sems.at[0], recv_sem=recv_sems.at[0],
        device_id=(x, y, 1 - c), device_id_type=MESH)


def _join_start(f, name):
    def body(f_ref, send_sems, recv_sems, f_thru, token):
        _join_copy(f_ref, send_sems, recv_sems, lax.axis_index("c")).start()
        token[...] = jnp.zeros_like(token)

    return pl.pallas_call(
        body, name=name,
        out_shape=(pltpu.SemaphoreType.DMA((1,)), pltpu.SemaphoreType.DMA((1,)), pltpu.HBM(f.shape, f.dtype), _TOKEN),
        in_specs=(_HBM_SPEC,), out_specs=(_SEM_SPEC, _SEM_SPEC, _HBM_SPEC, _VMEM_SPEC),
        input_output_aliases={0: 2},
        compiler_params=pltpu.CompilerParams(has_side_effects=_EFFECT),
    )(_hbm(f))


def _join_wait(send_sems, recv_sems, f_thru, after, name):
    def body(f_ref, send_sems, recv_sems, after_ref, f_out):
        c = lax.axis_index("c")
        _join_copy(f_ref, send_sems, recv_sems, c).wait_send()
        _join_copy(f_ref, send_sems, recv_sems, 1 - c).wait_recv()

    return pl.pallas_call(
        body, name=name, out_shape=pltpu.HBM(f_thru.shape, f_thru.dtype),
        in_specs=(_HBM_SPEC, _SEM_SPEC, _SEM_SPEC, _ANY), out_specs=_HBM_SPEC,
        input_output_aliases={0: 0},
        compiler_params=pltpu.CompilerParams(has_side_effects=_EFFECT),
    )(f_thru, send_sems, recv_sems, after)


def _place_bc(dxw, dbm, dcm, col_off):
    s, w = dbm.shape
    tm = _pick(s, (512, 256, 128))
    cb = col_off // w

    def body(b_ref, c_ref, x_ref, o_ref):
        o_ref[...] = jnp.where(pl.program_id(1) == 0, b_ref[...], c_ref[...])

    blk = pl.BlockSpec((tm, w), lambda i, j: (i, 0))
    return pl.pallas_call(
        body, name="place_db_dc", grid=(s // tm, 2), in_specs=[blk, blk, _ANY],
        out_specs=pl.BlockSpec((tm, w), lambda i, j: (i, cb + j)),
        out_shape=jax.ShapeDtypeStruct(dxw.shape, dxw.dtype), input_output_aliases={2: 0},
        compiler_params=_cparams(("parallel", "parallel")),
    )(dbm, dcm, dxw)


def _scatter_copies(t_ref, land_ref, send_sems, recv_sems, arrivals=True):
    x, y, c, chips = _place()
    k = 2 * x + y
    out, arrive = [], []
    for j, (cx, cy) in enumerate(chips):
        kj = 2 * cx + cy
        out.append(pltpu.make_async_remote_copy(
            src_ref=t_ref.at[kj], dst_ref=land_ref.at[k], send_sem=send_sems.at[j], recv_sem=recv_sems.at[j],
            device_id=(cx, cy, c), device_id_type=MESH))
        if arrivals:
            arrive.append(pltpu.make_async_remote_copy(
                src_ref=t_ref.at[kj], dst_ref=land_ref.at[kj], send_sem=send_sems.at[j], recv_sem=recv_sems.at[j],
                device_id=(cx, cy, c), device_id_type=MESH))
    return out, arrive


def _scatter_start(t, name):
    def body(t_ref, land_ref, send_sems, recv_sems, t_thru, land_thru, token):
        out, _ = _scatter_copies(t_ref, land_ref, send_sems, recv_sems, arrivals=False)
        for cp in out:
            cp.start()
        token[...] = jnp.zeros_like(token)

    return pl.pallas_call(
        body, name=name,
        out_shape=(pltpu.SemaphoreType.DMA((3,)), pltpu.SemaphoreType.DMA((3,)),
                   pltpu.HBM(t.shape, t.dtype), pltpu.HBM(t.shape, t.dtype), _TOKEN),
        in_specs=(_HBM_SPEC, _HBM_SPEC), out_specs=(_SEM_SPEC, _SEM_SPEC, _HBM_SPEC, _HBM_SPEC, _VMEM_SPEC),
        input_output_aliases={0: 2, 1: 3},
        compiler_params=pltpu.CompilerParams(has_side_effects=_EFFECT),
    )(_hbm(t), _hbm(lax.empty(t.shape, t.dtype)))


def _scatter_wait(send_sems, recv_sems, t_thru, land_thru, after, name):
    def body(t_ref, land_ref, send_sems, recv_sems, after_ref, t_out, land_out):
        out, arrive = _scatter_copies(t_ref, land_ref, send_sems, recv_sems)
        for cp in out:
            cp.wait_send()
        for cp in arrive:
            cp.wait_recv()

    return pl.pallas_call(
        body, name=name,
        out_shape=(pltpu.HBM(t_thru.shape, t_thru.dtype), pltpu.HBM(land_thru.shape, land_thru.dtype)),
        in_specs=(_HBM_SPEC, _HBM_SPEC, _SEM_SPEC, _SEM_SPEC, _ANY), out_specs=(_HBM_SPEC, _HBM_SPEC),
        input_output_aliases={0: 0, 1: 1},
        compiler_params=pltpu.CompilerParams(has_side_effects=_EFFECT),
    )(t_thru, land_thru, send_sems, recv_sems, after)


def _all_gather_small(v, reduce, name):
    r, l = v.shape

    def body(v_ref, o_ref, *rest):
        if reduce:
            buf, send_sems, recv_sems = rest
        else:
            buf = o_ref
            send_sems, recv_sems = rest
        x, y, c, _ = _place()
        me = 4 * x + 2 * y + c
        buf[me] = v_ref[...]
        cps = []
        for d in range(1, N_DEV):
            peer = (x if d & 4 == 0 else 1 - x, y if d & 2 == 0 else 1 - y, c if d & 1 == 0 else 1 - c)
            cp = pltpu.make_async_remote_copy(
                src_ref=v_ref, dst_ref=buf.at[me], send_sem=send_sems.at[d - 1], recv_sem=recv_sems.at[d - 1],
                device_id=peer, device_id_type=MESH)
            cp.start()
            cps.append((cp, peer))
        for d, (cp, (px, py, pc)) in enumerate(cps, start=1):
            pltpu.make_async_remote_copy(
                src_ref=v_ref, dst_ref=buf.at[4 * px + 2 * py + pc], send_sem=send_sems.at[d - 1],
                recv_sem=recv_sems.at[d - 1], device_id=(px, py, pc), device_id_type=MESH).wait_recv()
        for cp, _ in cps:
            cp.wait_send()
        if reduce:
            acc = buf[0]
            for i in range(1, N_DEV):
                acc = acc + buf[i]
            o_ref[...] = acc

    vm = pl.BlockSpec(memory_space=pltpu.VMEM)
    out_shape = jax.ShapeDtypeStruct((r, l) if reduce else (N_DEV, r, l), F32)
    scratch = ([pltpu.VMEM((N_DEV, r, l), F32)] if reduce else []) + [
        pltpu.SemaphoreType.DMA((N_DEV - 1,)), pltpu.SemaphoreType.DMA((N_DEV - 1,))]
    return pl.pallas_call(
        body, name=name, in_specs=[vm], out_specs=vm, out_shape=out_shape, scratch_shapes=scratch,
    )(v)


_BLOCK_BYTES = 3 * 512 * 1024


def _rows_per_block(r, cn, itemsize=4):
    best = 8
    for t in range(8, r + 1, 8):
        if r % t == 0 and t * cn * itemsize <= _BLOCK_BYTES:
            best = t
    return best


def _add_sibling_half(g4, recv, cvec, name):
    ns, r, cn = g4.shape
    hr = r // 2
    tr = _rows_per_block(hr, cn)
    nrb = hr // tr

    def body(c_ref, a_ref, b_ref, o_ref):
        o_ref[...] = (a_ref[...].astype(F32) + b_ref[...].astype(F32)).astype(o_ref.dtype)

    grid_spec = pltpu.PrefetchScalarGridSpec(
        num_scalar_prefetch=1, grid=(ns, nrb),
        in_specs=[pl.BlockSpec((None, tr, cn), lambda j, i, c: (j, c[0] * nrb + i, 0)),
                  pl.BlockSpec((None, tr, cn), lambda j, i, c: (j, i, 0))],
        out_specs=pl.BlockSpec((None, tr, cn), lambda j, i, c: (j, i, 0)))
    return pl.pallas_call(
        body, name=name, grid_spec=grid_spec, out_shape=jax.ShapeDtypeStruct((ns, hr, cn), BF16),
        compiler_params=_cparams(("parallel", "parallel")),
    )(cvec, g4, recv)


def _sum_chips(r4, t4, kvec, cvec, name):
    ns, hr, cn = r4.shape
    tr = _rows_per_block(hr, cn)
    nrb = hr // tr

    def body(k_ref, c_ref, r_ref, t_ref, o_ref):
        acc = t_ref[...].astype(F32)
        for dlt in range(1, ns):
            acc = acc + r_ref[(k_ref[0] + dlt) % ns].astype(F32)
        o_ref[...] = acc

    grid_spec = pltpu.PrefetchScalarGridSpec(
        num_scalar_prefetch=2, grid=(nrb,),
        in_specs=[pl.BlockSpec((ns, tr, cn), lambda i, k, c: (0, i, 0)),
                  pl.BlockSpec((None, tr, cn), lambda i, k, c: (k[0], i, 0))],
        out_specs=pl.BlockSpec((tr, cn), lambda i, k, c: (c[0] * nrb + i, 0)))
    return pl.pallas_call(
        body, name=name, grid_spec=grid_spec, out_shape=jax.ShapeDtypeStruct((2 * hr, cn), F32),
        compiler_params=_cparams(("parallel",)),
    )(kvec, cvec, r4, t4)


def _adamw(w, g, m, v, name):
    r, cn = w.shape
    tr = _rows_per_block(r, cn)
    c1 = 1.0 - ADAM_B1 ** ADAM_STEP
    c2 = 1.0 - ADAM_B2 ** ADAM_STEP

    def body(w_ref, g_ref, m_ref, v_ref, go_ref, d_ref, mo_ref, vo_ref):
        gv = g_ref[...]
        mn = ADAM_B1 * m_ref[...] + (1.0 - ADAM_B1) * gv
        vn = ADAM_B2 * v_ref[...] + (1.0 - ADAM_B2) * (gv * gv)
        go_ref[...] = gv
        mo_ref[...] = mn
        vo_ref[...] = vn
        d_ref[...] = -ADAM_LR * ((mn / c1) / (jnp.sqrt(vn / c2) + ADAM_EPS) + ADAM_WD * w_ref[...])

    spec = pl.BlockSpec((tr, cn), lambda i: (i, 0))
    return pl.pallas_call(
        body, name=name, grid=(r // tr,), in_specs=[spec] * 4, out_specs=[spec] * 4,
        out_shape=[jax.ShapeDtypeStruct((r, cn), F32)] * 4,
        compiler_params=_cparams(("parallel",)),
    )(w, g, m, v)


def _pack(arrs):
    flat = jnp.concatenate([a.reshape(-1).astype(F32) for a in arrs])
    n = flat.shape[0]
    tot = -(-n // (8 * LANES)) * (8 * LANES)
    return jnp.pad(flat, (0, tot - n)).reshape(tot // LANES, LANES)


def _unpack(packed, shapes):
    flat = packed.reshape(-1)
    out, off = [], 0
    for shp in shapes:
        sz = int(np.prod(shp))
        out.append(flat[off:off + sz].reshape(shp))
        off += sz
    return out


class _LocalExchange:
    def __init__(self, wa4, woa4, ws4, wos4):
        self.l0, self.ssm = [wa4, woa4], [ws4, wos4]
        self.grads = {}

    def l0_begin(self):
        return self.l0[0], None

    def l0_neighbours(self, after):
        return self.l0[0]

    def l0_diagonal(self, after):
        return self.l0

    def ssm_gather_mid(self, after):
        return None

    def ssm_gather_end(self, after):
        return self.ssm

    def grad_begin(self, name, g4):
        self.grads[name] = g4
        return None

    def grad_mid(self, name, after):
        return None

    def grad_sync(self, name, after):
        pass

    def small_grads(self, small_full):
        self.small = small_full
        return None


class _Exchange:
    def __init__(self, kvec, cvec, l0_bufs, after):
        self.kvec, self.cvec, self.l0_bufs, self.bufs, self.after = kvec, cvec, l0_bufs, None, after
        self.pending, self.summed, self.last_token, self.l0_token, self.swaps, self.joins = {}, {}, None, None, {}, {}

    def l0_begin(self):
        if self.l0_token is None:
            self.l0_send, self.l0_recv, self.l0_bufs, self.l0_token = _gather_start(
                self.l0_bufs, False, "l0_gather_ici_start", self.after)
        return self.l0_bufs[0], (self.l0_token if self.bufs is None else self.sems[3])

    def _l0_step(self, which, tag, after):
        bufs = _gather_wait(self.l0_bufs, self.l0_send, self.l0_recv, after, False, "l0_gather_ici_wait_" + tag, which)
        send, recv, bufs, token = _gather_start(bufs, True, "l0_gather_fwd_start_" + tag, None, which)
        self.l0_bufs = _gather_wait(bufs, send, recv, token, True, "l0_gather_fwd_wait_" + tag, which)
        return self.l0_bufs

    def l0_neighbours(self, after):
        return self._l0_step(_NEIGHBOURS, "nb", after)[0]

    def l0_diagonal(self, after):
        return self._l0_step(_DIAGONAL, "diag", after)

    def ssm_gather_start(self, ssm_bufs):
        self.sems = _gather_start(ssm_bufs, False, "ssm_gather_ici_start", self.l0_token)
        self.bufs = self.sems[2]
        return self.sems[3]

    def ssm_gather_mid(self, after):
        bufs = _gather_wait(self.bufs, self.sems[0], self.sems[1], after, False, "ssm_gather_ici_wait")
        self.sems = _gather_start(bufs, True, "ssm_gather_fwd_start")
        self.bufs = self.sems[2]
        return self.sems[3]

    def ssm_gather_end(self, after):
        return _gather_wait(self.bufs, self.sems[0], self.sems[1], after, True, "ssm_gather_fwd_wait")

    def small_grads(self, small_full):
        packed = _all_gather_small(_pack(small_full), True, "reduce_small_grads")
        self.small = _unpack(packed, [t.shape for t in small_full])
        return packed

    def grad_begin(self, name, g4):
        self.swaps[name] = _swap_start(g4, "swap_start_" + name)
        return self.swaps[name][4]

    def grad_mid(self, name, after):
        send_sem, recv_sem, g_thru, land, _ = self.swaps.pop(name)
        g4, recv = _swap_wait(send_sem, recv_sem, g_thru, land, after, "swap_wait_" + name)
        return self._reduce(name, g4, recv)

    def _reduce(self, name, g4, recv):
        t = _add_sibling_half(g4, recv, self.cvec, "add_sibling_" + name)
        send_sems, recv_sems, t_thru, land, token = _scatter_start(t, "scatter_start_" + name)
        self.pending[name] = (send_sems, recv_sems, t_thru, land)
        self.last_token = token
        return token

    def grad_sync(self, name, after):
        t, land = _scatter_wait(*self.pending.pop(name), after, "scatter_wait_" + name)
        summed = _sum_chips(land, t, self.kvec, self.cvec, "sum_chips_" + name)
        self.joins[name] = _join_start(summed, "join_start_" + name)

    def grad_end(self, name, after):
        send_sem, recv_sem, f_thru, _ = self.joins.pop(name)
        return _join_wait(send_sem, recv_sem, f_thru, after, "join_wait_" + name)


def _tie(vec, token):
    return vec if token is None else vec + token[0:1, 0:1].reshape((1,) * vec.ndim).astype(vec.dtype)


def _local_step(x2, tgt, ex, kvec, conv_w_f, conv_b_f, norm_w_f, rel_bias, dt_bias, a_log, d_skip,
                ln_g, ln_b):
    s, d = x2.shape
    wa4, tok = ex.l0_begin()
    d_attn = wa4.shape[2] * N_CHIPS // 10
    hpg = d_attn // HEAD_DIM
    d_inner = norm_w_f.shape[1]
    ng = d_inner // SSM_GROUP_WIDTH
    n_heads = dt_bias.shape[1]
    conv_dim = conv_w_f.shape[1]
    assert n_heads == ng * HEADS_PER_SSM_GROUP and conv_dim == d_inner + 2 * ng * D_STATE

    x3 = _cast_x3(x2, "cast_x", tok)
    for g, (_, dil) in enumerate(ATTN_PATTERNS):
        if dil > 1:
            x3 = _class_copy(x3, g, dil, f"class_order_x_g{g}")
    xb = x3[0]
    buckets = _bucket_tiles()
    bias = _bias_expand(rel_bias, buckets, hpg)
    pa = _in_proj_shard(x3, wa4, kvec, 0, d_attn, "mm_in_attn_own", after=tok)
    wa4 = ex.l0_neighbours(pa)
    pa = _in_proj_shard(x3, wa4, kvec, 1, d_attn, "mm_in_attn_nb1", into=pa)
    pa = _in_proj_shard(x3, wa4, kvec, 2, d_attn, "mm_in_attn_nb2", into=pa)
    wa4, woa4 = ex.l0_diagonal(pa)
    pa = _in_proj_shard(x3, wa4, kvec, 3, d_attn, "mm_in_attn_diag", into=pa)
    og, lg = [], []
    for g, (_, dil) in enumerate(ATTN_PATTERNS):
        o_, l_ = _attn_fwd(pa, bias, g, dil, hpg)
        og.append(o_)
        lg.append(l_)
    o, lse, yat = _attn_combine(og, lg, pa, hpg)
    h0 = _mm_nn_sharded(yat, woa4, F32, "mm_out_attn", after=ex.ssm_gather_mid(yat))
    g0, b0, g1, b1 = ln_g[0:1], ln_b[0:1], ln_g[1:2], ln_b[1:2]
    xhat0, rstd0, x1b = _ln_fwd(x2, h0, g0, b0, "ln0_fwd")

    wst4, wos4 = ex.ssm_gather_end(x1b)
    wst = wst4.reshape(N_CHIPS * wst4.shape[1], d)
    nzx = d_inner + conv_dim
    wos = wos4.reshape(d_inner, d)
    pzx = _mm_nt(x1b, wst, BF16, "mm_in_ssm", n=nzx)
    dt_raw = _mm_nt(x1b, wst, F32, "mm_in_dt", n=n_heads, b_row_off=nzx)

    def pad_heads(t):
        t = t.reshape(t.shape[0], ng, HEADS_PER_SSM_GROUP).transpose(1, 0, 2)
        return jnp.pad(t, ((0, 0), (0, 0), (0, LANES - HEADS_PER_SSM_GROUP)))

    def unpad_heads(t):
        return t[:, :, :HEADS_PER_SSM_GROUP].transpose(1, 0, 2).reshape(t.shape[1], n_heads)

    dtp = pad_heads(dt_raw)
    alog_p, dtb_p = pad_heads(a_log), pad_heads(dt_bias)
    dsk_e = jnp.repeat(d_skip.reshape(ng, 1, HEADS_PER_SSM_GROUP), SSM_HEAD_DIM, axis=2)
    e = _expand_matrix()
    xbc = _conv_fwd(pzx, conv_w_f, conv_b_f, d_inner)
    y_ssd, states = _ssd_fwd(xbc, dtp, alog_p, dtb_p, dsk_e, e, d_inner)
    y3 = _gate_norm_fwd(y_ssd, pzx, norm_w_f)
    h1 = _mm_nn(y3, wos, F32, "mm_out_ssm")
    xhat1, rstd1, dy2, row_sq = _ln_fwd(xhat0, h1, g1, b1, "ln1_fwd_loss", affine_in=(g0, b0), target=tgt)
    loss_local = 0.5 * jnp.sum(row_sq) / d

    du1, du1b, dg1, db1 = _ln_bwd(dy2, xhat1, rstd1, g1, "ln1_bwd")
    dy3 = _mm_nt(du1b, wos, BF16, "mm_d_y3")
    g_wos = _mm_tn(y3, du1b, BF16, "mm_g_w_out_ssm").reshape(N_CHIPS, d_inner // N_CHIPS, d)
    norm_w_t = _tie(norm_w_f, ex.grad_begin("w_out_ssm", g_wos))
    dy_ssd, dz, d_nw = _gate_norm_bwd(dy3, y_ssd, pzx, norm_w_t)
    dsk_t = _tie(dsk_e, ex.grad_mid("w_out_ssm", dy_ssd))
    dxs, dbm, dcm, ddtp, d_alog, d_dtb, d_dsk = _ssd_bwd(xbc, dtp, alog_p, dtb_p, dsk_t, e, states, dy_ssd, d_inner)
    dpre, d_cw, d_cb = _conv_bwd_a(pzx, _place_bc(dxs, dbm, dcm, d_inner), conv_w_f, conv_b_f, d_inner)
    dpzx = _conv_bwd_b(dpre, conv_w_f, dz, d_inner)
    ddt_raw = unpad_heads(ddtp)
    t1 = _mm_nn(ddt_raw, wst, F32, "mm_d_x1_dt", b_row_off=nzx, add=du1, add_scale=DEEPNORM_ALPHA)
    dx1 = _mm_nn(dpzx, wst, F32, "mm_d_x1", add=t1)
    ex.grad_sync("w_out_ssm", dx1)
    g_wst = _mm_tn(dpzx, x1b, BF16, "mm_g_w_in_ssm", out_rows=wst.shape[0])
    g_wst = _mm_tn(ddt_raw, x1b, BF16, "mm_g_w_dt", out_rows=wst.shape[0], out_row_off=nzx, into=g_wst)
    g0_t = _tie(g0, ex.grad_begin("w_in_ssm", g_wst.reshape(wst4.shape)))

    du0, du0b, dg0, db0 = _ln_bwd(dx1, xhat0, rstd0, g0_t, "ln0_bwd")
    dyat = _mm_nt_sharded_k(du0b, woa4, BF16, "mm_d_yat")
    g_woa = _mm_tn(yat, du0b, BF16, "mm_g_w_out_attn", shard_cols=d // N_CHIPS)
    lse_t = _tie(lse, ex.grad_mid("w_in_ssm", g_woa))
    ex.grad_begin("w_out_attn", g_woa)
    do, delta, dgate = _attn_pre_bwd(dyat, o, pa, hpg)
    pieces, dbt = [], []
    for g, (_, dil) in enumerate(ATTN_PATTERNS):
        dq, dk, dv, db_ = _attn_bwd(pa, bias, do, lse_t, delta, g, dil, hpg)
        pieces += [dq, dk, dv]
        dbt.append(db_)
    dpa = jnp.concatenate(pieces + [dgate], axis=1)
    tok_woa = ex.grad_mid("w_out_attn", dpa)
    g_wa = _mm_tn(xb, dpa, BF16, "mm_g_w_in_attn", shard_cols=wa4.shape[2], after=tok_woa)
    ex.grad_begin("w_in_attn", g_wa)
    ex.grad_sync("w_in_ssm", g_wa)
    ex.grad_sync("w_out_attn", g_wa)
    d_rel = _bias_reduce(jnp.stack(dbt), buckets, hpg)[:, :, 0].T
    d_dsk_h = d_dsk.reshape(n_heads, SSM_HEAD_DIM).sum(axis=1)
    small_full = [d_rel, d_cw, d_cb, unpad_heads(d_dtb), unpad_heads(d_alog), d_dsk_h[None], d_nw,
                  jnp.concatenate([dg0, dg1], axis=0), jnp.concatenate([db0, db1], axis=0)]
    tok_wa = ex.grad_mid("w_in_attn", ex.small_grads(small_full))
    grad_x = _mm_nt_sharded_k(dpa, wa4, F32, "mm_d_x0", add=du0, add_scale=DEEPNORM_ALPHA, after=tok_wa)
    return loss_local, grad_x[None]


def kernel(x, w_in_attn, w_out_attn, rel_bias, w_in_ssm, conv_w, conv_b, dt_bias, a_log, d_skip, ssm_norm_w, w_out_ssm, ln_g, ln_b, loss_target, m_w_in_attn, m_w_out_attn, m_rel_bias, m_w_in_ssm, m_conv_w, m_conv_b, m_dt_bias, m_a_log, m_d_skip, m_ssm_norm_w, m_w_out_ssm, m_ln_g, m_ln_b, v_w_in_attn, v_w_out_attn, v_rel_bias, v_w_in_ssm, v_conv_w, v_conv_b, v_dt_bias, v_a_log, v_d_skip, v_ssm_norm_w, v_w_out_ssm, v_ln_g, v_ln_b):
    xi, yi, ci = lax.axis_index("x"), lax.axis_index("y"), lax.axis_index("c")
    chip = 2 * xi + yi
    cvec = jnp.reshape(ci, (1,)).astype(jnp.int32)
    kvec = jnp.reshape(chip, (1,)).astype(jnp.int32)

    cw_l, cb_l, nw_l = conv_w[0], conv_b[0], ssm_norm_w[0]
    vec_shapes = [cw_l.shape, cb_l.shape, nw_l.shape]
    vec_all = _all_gather_small(_pack([cw_l, cb_l, nw_l]), False, "gather_vectors")
    l0 = [_cast_to_slot(w_in_attn[0], kvec, "cast_w_in_attn"), _cast_to_slot(w_out_attn[0], kvec, "cast_w_out_attn")]
    ex = _Exchange(kvec, cvec, l0, after=vec_all)
    _, tok0 = ex.l0_begin()
    tok1 = ex.ssm_gather_start([_cast_to_slot(w_in_ssm[0].T, kvec, "cast_w_in_ssm", tok0),
                                _cast_to_slot(w_out_ssm[0], kvec, "cast_w_out_ssm", tok0)])
    parts = [_unpack(vec_all[2 * j], vec_shapes) for j in range(N_CHIPS)]
    conv_w_f = jnp.concatenate([p[0] for p in parts], axis=1)
    conv_b_f = jnp.concatenate([p[1] for p in parts], axis=0)[None]
    norm_w_f = jnp.concatenate([p[2] for p in parts], axis=0)[None]

    loss_local, grad_x = _local_step(
        x[0], loss_target[0], ex, kvec, conv_w_f, conv_b_f, norm_w_f, rel_bias, dt_bias, a_log,
        d_skip, ln_g, ln_b)
    loss = lax.psum(loss_local, ("x", "y", "c"))

    big_w = dict(w_in_attn=(w_in_attn, m_w_in_attn, v_w_in_attn), w_out_attn=(w_out_attn, m_w_out_attn, v_w_out_attn),
                 w_in_ssm=(w_in_ssm, m_w_in_ssm, v_w_in_ssm), w_out_ssm=(w_out_ssm, m_w_out_ssm, v_w_out_ssm))
    big = {}

    def finish(names, after):
        last = after
        for nm in names:
            gf = ex.grad_end(nm, last)
            flip = (lambda t: t.T) if nm == "w_in_ssm" else (lambda t: t)
            w_, m_, v_ = (flip(t[0]) for t in big_w[nm])
            res = _adamw(w_, gf, m_, v_, "adamw_" + nm)
            big[nm] = [flip(t)[None] for t in res]
            last = res[3]
        return last

    last = finish(["w_out_ssm", "w_in_ssm", "w_out_attn"], grad_x)
    ex.grad_sync("w_in_attn", last)
    finish(["w_in_attn"], last)

    s_rel, s_cw, s_cb, s_dtb, s_alog, s_dsk, s_nw, s_lng, s_lnb = ex.small
    cwc, nwc = conv_w.shape[2], ssm_norm_w.shape[1]
    s_cw = lax.dynamic_slice_in_dim(s_cw, chip * cwc, cwc, axis=1)[None]
    s_cb = lax.dynamic_slice_in_dim(s_cb, chip * cwc, cwc, axis=1)
    s_nw = lax.dynamic_slice_in_dim(s_nw, chip * nwc, nwc, axis=1)
    small_names = ["rel_bias", "conv_w", "conv_b", "dt_bias", "a_log", "d_skip", "ssm_norm_w", "ln_g", "ln_b"]
    small_g = [s_rel, s_cw, s_cb, s_dtb, s_alog, s_dsk, s_nw, s_lng, s_lnb]
    small_w = [rel_bias, conv_w, conv_b, dt_bias, a_log, d_skip, ssm_norm_w, ln_g, ln_b]
    small_m = [m_rel_bias, m_conv_w, m_conv_b, m_dt_bias, m_a_log, m_d_skip, m_ssm_norm_w, m_ln_g, m_ln_b]
    small_v = [v_rel_bias, v_conv_w, v_conv_b, v_dt_bias, v_a_log, v_d_skip, v_ssm_norm_w, v_ln_g, v_ln_b]
    shapes = [t.shape for t in small_w]
    res = _adamw(_pack(small_w), _pack(small_g), _pack(small_m), _pack(small_v), "adamw_small")
    small = {nm: [] for nm in small_names}
    for packed in res:
        for nm, t in zip(small_names, _unpack(packed, shapes)):
            small[nm].append(t)

    order = ["w_in_attn", "w_out_attn", "rel_bias", "w_in_ssm", "conv_w", "conv_b", "dt_bias", "a_log",
             "d_skip", "ssm_norm_w", "w_out_ssm", "ln_g", "ln_b"]
    table = {**big, **small}
    outs = [loss, grad_x]
    for kind in range(4):
        outs += [table[nm][kind] for nm in order]
    return tuple(outs)
```

```python
import math

import numpy as np
import jax
import jax.numpy as jnp
from jax import lax
from jax.experimental import pallas as pl
from jax.experimental.pallas import tpu as pltpu

F32 = jnp.float32
BF16 = jnp.bfloat16
MESH = pl.DeviceIdType.MESH

ATTN_PATTERNS = ((128, 1), (512, 4), (2048, 16))
N_GROUPS_ATTN = 3
HEAD_DIM = 128
ATTN_BLOCK = 128
NUM_BUCKETS = 32
MAX_DISTANCE = 2048
SSM_HEAD_DIM = 64
HEADS_PER_SSM_GROUP = 16
SSM_GROUP_WIDTH = HEADS_PER_SSM_GROUP * SSM_HEAD_DIM
D_STATE = 128
CONV_WIDTH = 4
CHUNK = 128
DEPTH = 2
DEEPNORM_ALPHA = (2 * DEPTH) ** 0.25
LN_EPS = 1e-5
RMS_EPS = 1e-5
NEG_INF = -1e30
ADAM_LR = 0.001
ADAM_B1 = 0.9
ADAM_B2 = 0.999
ADAM_EPS = 1e-08
ADAM_WD = 0.01
ADAM_STEP = 10

N_CHIPS = 4
N_DEV = 8

VMEM_LIMIT_V7X = 56 * 1024 * 1024
LANES = 128


def _cparams(sem=None):
    return pltpu.CompilerParams(dimension_semantics=sem, vmem_limit_bytes=VMEM_LIMIT_V7X)


def _sigmoid(x):
    return 0.5 * jnp.tanh(0.5 * x) + 0.5


def _dot(a, b):
    return jnp.dot(a, b, preferred_element_type=F32)


def _dot_nt(a, b):
    return lax.dot_general(a, b, (((1,), (1,)), ((), ())), preferred_element_type=F32)


def _dot_tn(a, b):
    return lax.dot_general(a, b, (((0,), (0,)), ((), ())), preferred_element_type=F32)


def _split2(x):
    hi = x.astype(BF16)
    lo = (x - hi.astype(F32)).astype(BF16)
    return hi, lo


def _split3(x):
    hi = x.astype(BF16)
    r = x - hi.astype(F32)
    mid = r.astype(BF16)
    lo = (r - mid.astype(F32)).astype(BF16)
    return hi, mid, lo


def _matmul(a, b, *, mode, grid, a_spec, b_spec, out_shape, out_spec, tile, name,
            add=None, add_spec=None, add_scale=1.0, after=None, into=None):
    nk = grid[2]
    tm, tn = tile
    dot = {"nn": _dot, "nt": _dot_nt, "tn": _dot_tn}[mode]
    has_add = add is not None
    has_after = after is not None
    has_into = into is not None

    def finish(r, add_ref, o_ref):
        if has_add:
            r = r + add_scale * add_ref[...].astype(F32)
        o_ref[...] = r.astype(o_ref.dtype)

    def body_one(*refs):
        a_ref, b_ref = refs[:2]
        finish(dot(a_ref[...].astype(BF16), b_ref[...].astype(BF16)), refs[2] if has_add else None, refs[-1])

    def body_acc(*refs):
        a_ref, b_ref = refs[:2]
        add_ref = refs[2] if has_add else None
        o_ref, acc_ref = refs[-2:]
        k = pl.program_id(2)

        @pl.when(k == 0)
        def _():
            acc_ref[...] = jnp.zeros_like(acc_ref)

        acc_ref[...] += dot(a_ref[...].astype(BF16), b_ref[...].astype(BF16))

        @pl.when(k == nk - 1)
        def _():
            finish(acc_ref[...], add_ref, o_ref)

    in_specs = ([a_spec, b_spec] + ([add_spec] if has_add else []) + ([_ANY] if has_after else [])
                + ([_ANY] if has_into else []))
    args = (a, b) + ((add,) if has_add else ()) + ((after,) if has_after else ()) + ((into,) if has_into else ())
    return pl.pallas_call(
        body_one if nk == 1 else body_acc, name=name, grid=grid, in_specs=in_specs, out_specs=out_spec,
        out_shape=out_shape,
        input_output_aliases={len(args) - 1: 0} if has_into else {},
        scratch_shapes=[] if nk == 1 else [pltpu.VMEM((tm, tn), F32)],
        compiler_params=_cparams(("parallel", "parallel", "arbitrary")),
    )(*args)


def _pick(n, pref):
    for t in pref:
        if n % t == 0:
            return t
    return n


_TILE_PREF = (1024, 512, 256, 128)
_K_TILE_PREF = (2048,) + _TILE_PREF


def _k_tile(k, out_dtype, has_add):
    return _pick(k, _K_TILE_PREF if (has_add or out_dtype != BF16) else (4096,) + _K_TILE_PREF)


def _mm_nn_sharded(a, w4, out_dtype, name, after=None):
    m, k = a.shape
    _, _, nn = w4.shape
    tm, tk, tn = _pick(m, _TILE_PREF), _k_tile(k, out_dtype, False), _pick(nn, _TILE_PREF)
    npb = nn // tn
    return _matmul(
        a, w4, mode="nn", grid=(m // tm, N_CHIPS * npb, k // tk), tile=(tm, tn), name=name,
        a_spec=pl.BlockSpec((tm, tk), lambda i, j, kk: (i, kk)),
        b_spec=pl.BlockSpec((None, tk, tn), lambda i, j, kk: (j // npb, kk, j % npb)),
        out_shape=jax.ShapeDtypeStruct((m, N_CHIPS * nn), out_dtype),
        out_spec=pl.BlockSpec((tm, tn), lambda i, j, kk: (i, j)), after=after)


def _mm_nn(a, b, out_dtype, name, b_row_off=0, add=None, add_scale=1.0):
    m, k = a.shape
    _, n = b.shape
    tm, tk, tn = _pick(m, _TILE_PREF), _k_tile(k, out_dtype, add is not None), _pick(n, _TILE_PREF)
    assert b_row_off % tk == 0
    ko = b_row_off // tk
    return _matmul(
        a, b, mode="nn", grid=(m // tm, n // tn, k // tk), tile=(tm, tn), name=name,
        a_spec=pl.BlockSpec((tm, tk), lambda i, j, kk: (i, kk)),
        b_spec=pl.BlockSpec((tk, tn), lambda i, j, kk: (kk + ko, j)),
        out_shape=jax.ShapeDtypeStruct((m, n), out_dtype),
        out_spec=pl.BlockSpec((tm, tn), lambda i, j, kk: (i, j)),
        add=add, add_spec=pl.BlockSpec((tm, tn), lambda i, j, kk: (i, j)), add_scale=add_scale)


def _mm_nt(a, b, out_dtype, name, add=None, add_scale=1.0, n=None, b_row_off=0):
    m, k = a.shape
    n = b.shape[0] if n is None else n
    tm, tk, tn = _pick(m, _TILE_PREF), _k_tile(k, out_dtype, add is not None), _pick(n, _TILE_PREF)
    assert b_row_off % tn == 0
    no = b_row_off // tn
    return _matmul(
        a, b, mode="nt", grid=(m // tm, n // tn, k // tk), tile=(tm, tn), name=name,
        a_spec=pl.BlockSpec((tm, tk), lambda i, j, kk: (i, kk)),
        b_spec=pl.BlockSpec((tn, tk), lambda i, j, kk: (j + no, kk)),
        out_shape=jax.ShapeDtypeStruct((m, n), out_dtype),
        out_spec=pl.BlockSpec((tm, tn), lambda i, j, kk: (i, j)),
        add=add, add_spec=pl.BlockSpec((tm, tn), lambda i, j, kk: (i, j)), add_scale=add_scale)


def _mm_nt_sharded_k(a, w4, out_dtype, name, add=None, add_scale=1.0, after=None):
    m, _ = a.shape
    _, n, kn = w4.shape
    tm, tk, tn = _pick(m, _TILE_PREF), _pick(kn, (2560,) + _TILE_PREF), _pick(n, _TILE_PREF)
    kpb = kn // tk
    return _matmul(
        a, w4, mode="nt", grid=(m // tm, n // tn, N_CHIPS * kpb), tile=(tm, tn), name=name,
        a_spec=pl.BlockSpec((tm, tk), lambda i, j, kk: (i, kk)),
        b_spec=pl.BlockSpec((None, tn, tk), lambda i, j, kk: (kk // kpb, j, kk % kpb)),
        out_shape=jax.ShapeDtypeStruct((m, n), out_dtype),
        out_spec=pl.BlockSpec((tm, tn), lambda i, j, kk: (i, j)),
        add=add, add_spec=pl.BlockSpec((tm, tn), lambda i, j, kk: (i, j)), add_scale=add_scale, after=after)


def _mm_tn(a, b, out_dtype, name, shard_cols=None, out_rows=None, out_row_off=0, into=None, after=None):
    k, m = a.shape
    _, n = b.shape
    nn = n if shard_cols is None else shard_cols
    tm, tk, tn = _pick(m, _TILE_PREF), _k_tile(k, out_dtype, False), _pick(nn, _TILE_PREF)
    if shard_cols is None:
        assert out_row_off % tm == 0
        ro = out_row_off // tm
        out_shape = jax.ShapeDtypeStruct((m if out_rows is None else out_rows, n), out_dtype)
        out_spec = pl.BlockSpec((tm, tn), lambda i, j, kk: (i + ro, j))
    else:
        npb = nn // tn
        out_shape = jax.ShapeDtypeStruct((n // nn, m, nn), out_dtype)
        out_spec = pl.BlockSpec((None, tm, tn), lambda i, j, kk: (j // npb, i, j % npb))
    return _matmul(
        a, b, mode="tn", grid=(m // tm, n // tn, k // tk), tile=(tm, tn), name=name,
        a_spec=pl.BlockSpec((tk, tm), lambda i, j, kk: (kk, i)),
        b_spec=pl.BlockSpec((tk, tn), lambda i, j, kk: (kk, j)),
        out_shape=out_shape, out_spec=out_spec, into=into, after=after)


def _bucket_tiles():
    qi = np.arange(ATTN_BLOCK)[:, None]
    ki = np.arange(2 * ATTN_BLOCK)[None, :]
    delta = np.clip(ATTN_BLOCK + qi - ki, 0, None)
    tiles = []
    max_exact = NUM_BUCKETS // 2
    for _, dil in ATTN_PATTERNS:
        dist = (delta * dil).astype(np.int32)
        d_f = np.maximum(dist, 1).astype(np.float32)
        large = max_exact + (np.log(d_f / np.float32(max_exact)) / np.float32(math.log(MAX_DISTANCE / max_exact))
                             * np.float32(NUM_BUCKETS - max_exact)).astype(np.int32)
        large = np.minimum(large, NUM_BUCKETS - 1)
        tiles.append(np.where(dist < max_exact, dist, large).astype(np.int32))
    return jnp.asarray(np.stack(tiles))


def _bias_expand(rel_bias, buckets, hpg):
    def body(tab_ref, bk_ref, o_ref):
        g, h = pl.program_id(0), pl.program_id(1)
        bk = bk_ref[...]
        acc = jnp.zeros((ATTN_BLOCK, 2 * ATTN_BLOCK), F32)
        for b in range(NUM_BUCKETS):
            acc = jnp.where(bk == b, tab_ref[b, g * hpg + h], acc)
        o_ref[...] = acc

    return pl.pallas_call(
        body, name="bias_expand", grid=(N_GROUPS_ATTN, hpg),
        in_specs=[pl.BlockSpec(memory_space=pltpu.SMEM),
                  pl.BlockSpec((None, ATTN_BLOCK, 2 * ATTN_BLOCK), lambda g, h: (g, 0, 0))],
        out_specs=pl.BlockSpec((None, None, ATTN_BLOCK, 2 * ATTN_BLOCK), lambda g, h: (g, h, 0, 0)),
        out_shape=jax.ShapeDtypeStruct((N_GROUPS_ATTN, hpg, ATTN_BLOCK, 2 * ATTN_BLOCK), F32),
        compiler_params=_cparams(("parallel", "parallel")),
    )(rel_bias, buckets)


def _bias_reduce(dtiles, buckets, hpg):
    def body(t_ref, bk_ref, o_ref):
        bk = bk_ref[...]
        t = t_ref[...]
        rows = lax.broadcasted_iota(jnp.int32, (NUM_BUCKETS, LANES), 0)
        acc = jnp.zeros((NUM_BUCKETS, LANES), F32)
        for b in range(NUM_BUCKETS):
            s = jnp.sum(jnp.sum(jnp.where(bk == b, t, 0.0), axis=1, keepdims=True), axis=0, keepdims=True)
            acc = jnp.where(rows == b, s, acc)
        o_ref[...] = acc

    return pl.pallas_call(
        body, name="bias_reduce", grid=(N_GROUPS_ATTN, hpg),
        in_specs=[pl.BlockSpec((None, None, ATTN_BLOCK, 2 * ATTN_BLOCK), lambda g, h: (g, h, 0, 0)),
                  pl.BlockSpec((None, ATTN_BLOCK, 2 * ATTN_BLOCK), lambda g, h: (g, 0, 0))],
        out_specs=pl.BlockSpec((None, NUM_BUCKETS, LANES), lambda g, h: (g * hpg + h, 0, 0)),
        out_shape=jax.ShapeDtypeStruct((N_GROUPS_ATTN * hpg, NUM_BUCKETS, LANES), F32),
        compiler_params=_cparams(("parallel", "parallel")),
    )(dtiles, buckets)


def _cast_x3(x, name, after=None):
    r, c = x.shape
    tr = _pick(r, (512, 256, 128, 8))
    extra = [] if after is None else [after]

    def body(x_ref, *rest):
        rest[-1][...] = x_ref[...].astype(BF16)

    return pl.pallas_call(
        body, name=name, grid=(r // tr,),
        in_specs=[pl.BlockSpec((tr, c), lambda i: (i, 0))] + [_ANY] * len(extra),
        out_specs=pl.BlockSpec((None, tr, c), lambda i: (0, i, 0)),
        out_shape=jax.ShapeDtypeStruct((N_GROUPS_ATTN, r, c), BF16),
        compiler_params=_cparams(("parallel",)),
    )(x, *extra)


def _class_copy(x3, slot, dil, name):
    _, s, d = x3.shape
    rows = s // dil
    tm = _pick(rows, (512, 256, 128))
    nbk = rows // tm

    def body(v_ref, x3_ref, o_ref):
        o_ref[...] = v_ref[...]

    return pl.pallas_call(
        body, name=name, grid=(dil, nbk),
        in_specs=[pl.BlockSpec((tm, d), lambda r, i: (i, r)), _ANY],
        out_specs=pl.BlockSpec((None, tm, d), lambda r, i: (slot, r * nbk + i, 0)),
        out_shape=jax.ShapeDtypeStruct(x3.shape, x3.dtype), input_output_aliases={1: 0},
        compiler_params=_cparams(("parallel", "parallel")),
    )(x3[0].reshape(rows, dil * d), x3)


def _in_proj_shard(x3, wa4, kvec, p, d_attn, name, into=None, after=None):
    _, s, d = x3.shape
    _, _, nn = wa4.shape
    tm = _pick(s, _TILE_PREF)
    tn = _pick(math.gcd(nn, 3 * d_attn), _TILE_PREF)
    npb, bpg = nn // tn, 3 * d_attn // tn
    extra = ([] if after is None else [after]) + ([] if into is None else [into])

    def block(k, j):
        return jnp.bitwise_xor(k[0], p) * npb + j

    def slot(k, j):
        jb = block(k, j)
        return jnp.where(jb < N_GROUPS_ATTN * bpg, jb // bpg, 0)

    def body(k_ref, a_ref, b_ref, *rest):
        rest[-1][...] = _dot(a_ref[...], b_ref[...]).astype(BF16)

    grid_spec = pltpu.PrefetchScalarGridSpec(
        num_scalar_prefetch=1, grid=(s // tm, npb),
        in_specs=[pl.BlockSpec((None, tm, d), lambda i, j, k: (slot(k, j), i, 0)),
                  pl.BlockSpec((None, d, tn), lambda i, j, k: (jnp.bitwise_xor(k[0], p), 0, j))]
        + [_ANY] * len(extra),
        out_specs=pl.BlockSpec((tm, tn), lambda i, j, k: (i, block(k, j))))
    return pl.pallas_call(
        body, name=name, grid_spec=grid_spec, out_shape=jax.ShapeDtypeStruct((s, N_CHIPS * nn), BF16),
        input_output_aliases={} if into is None else {2 + len(extra): 0},
        compiler_params=_cparams(("parallel", "parallel")),
    )(kvec, x3, wa4, *extra)


def _attn_valid(n_is_first):
    qi = lax.broadcasted_iota(jnp.int32, (ATTN_BLOCK, 2 * ATTN_BLOCK), 0)
    ki = lax.broadcasted_iota(jnp.int32, (ATTN_BLOCK, 2 * ATTN_BLOCK), 1)
    delta = ATTN_BLOCK + qi - ki
    band = (delta >= 0) & (delta <= ATTN_BLOCK)
    return band & (jnp.logical_not(n_is_first) | (ki >= ATTN_BLOCK))


def _attn_fwd(pg, bias, g, dil, hpg):
    s = pg.shape[0]
    w = hpg * HEAD_DIM
    rows = s // dil
    nb = rows // ATTN_BLOCK
    scale = HEAD_DIM ** -0.5

    def body(q_ref, kc_ref, kp_ref, vc_ref, vp_ref, bias_ref, o_ref, lse_ref):
        valid = _attn_valid(pl.program_id(1) == 0)
        lane = lax.broadcasted_iota(jnp.int32, (ATTN_BLOCK, LANES), 1)
        lse = jnp.zeros((ATTN_BLOCK, LANES), F32)
        for h in range(hpg):
            sl = slice(h * HEAD_DIM, (h + 1) * HEAD_DIM)
            k2 = jnp.concatenate([kp_ref[:, sl], kc_ref[:, sl]], axis=0)
            v2 = jnp.concatenate([vp_ref[:, sl], vc_ref[:, sl]], axis=0)
            sc = _dot_nt(q_ref[:, sl], k2) * scale + bias_ref[h]
            sc = jnp.where(valid, sc, NEG_INF)
            m = jnp.max(sc, axis=1, keepdims=True)
            p = jnp.exp(sc - m)
            l = jnp.sum(p, axis=1, keepdims=True)
            o_ref[:, sl] = (_dot(p.astype(BF16), v2) * (1.0 / l)).astype(BF16)
            lse = jnp.where(lane == h, m + jnp.log(l), lse)
        lse_ref[...] = lse

    def col(off):
        return lambda r, n: (r * nb + n, 3 * g + off)

    def colp(off):
        return lambda r, n: (r * nb + jnp.maximum(n - 1, 0), 3 * g + off)

    blk = (ATTN_BLOCK, w)
    tok = pl.BlockSpec(blk, lambda r, n: (n, r))
    tok1 = pl.BlockSpec((ATTN_BLOCK, LANES), lambda r, n: (n, r))
    o, lse = pl.pallas_call(
        body, name=f"attn_fwd_g{g}", grid=(dil, nb),
        in_specs=[pl.BlockSpec(blk, col(0)), pl.BlockSpec(blk, col(1)), pl.BlockSpec(blk, colp(1)),
                  pl.BlockSpec(blk, col(2)), pl.BlockSpec(blk, colp(2)),
                  pl.BlockSpec((None, hpg, ATTN_BLOCK, 2 * ATTN_BLOCK), lambda r, n: (g, 0, 0, 0))],
        out_specs=[tok, tok1],
        out_shape=[jax.ShapeDtypeStruct((rows, dil * w), BF16), jax.ShapeDtypeStruct((rows, dil * LANES), F32)],
        compiler_params=_cparams(("parallel", "parallel")),
    )(pg, pg, pg, pg, pg, bias)
    return o.reshape(s, w), lse.reshape(s, LANES)


def _attn_combine(os_, lses, pa, hpg):
    s, w = os_[0].shape
    gate_blk = pa.shape[1] // w - 1
    tm = _pick(s, (256, 128))

    def body(o0, o1, o2, l0, l1, l2, gate_ref, o_ref, lse_ref, y_ref):
        a0, a1, a2 = l0[...], l1[...], l2[...]
        m = jnp.maximum(jnp.maximum(a0, a1), a2)
        e0, e1, e2 = jnp.exp(a0 - m), jnp.exp(a1 - m), jnp.exp(a2 - m)
        den = e0 + e1 + e2
        inv = 1.0 / den
        w0, w1, w2 = e0 * inv, e1 * inv, e2 * inv
        lse_ref[...] = m + jnp.log(den)
        for h in range(hpg):
            sl = slice(h * HEAD_DIM, (h + 1) * HEAD_DIM)
            o = (w0[:, h:h + 1] * o0[:, sl].astype(F32) + w1[:, h:h + 1] * o1[:, sl].astype(F32)
                 + w2[:, h:h + 1] * o2[:, sl].astype(F32))
            gate = gate_ref[:, sl].astype(F32)
            o_ref[:, sl] = o.astype(BF16)
            y_ref[:, sl] = (o * (gate * _sigmoid(gate))).astype(BF16)

    spec = pl.BlockSpec((tm, w), lambda i: (i, 0))
    spec1 = pl.BlockSpec((tm, LANES), lambda i: (i, 0))
    return pl.pallas_call(
        body, name="attn_combine", grid=(s // tm,),
        in_specs=[spec] * 3 + [spec1] * 3 + [pl.BlockSpec((tm, w), lambda i: (i, gate_blk))],
        out_specs=[spec, spec1, spec],
        out_shape=[jax.ShapeDtypeStruct((s, w), BF16), jax.ShapeDtypeStruct((s, LANES), F32),
                   jax.ShapeDtypeStruct((s, w), BF16)],
        compiler_params=_cparams(("parallel",)),
    )(*os_, *lses, pa)


def _attn_pre_bwd(dy, o, pa, hpg):
    s, w = dy.shape
    gate_blk = pa.shape[1] // w - 1
    tm = _pick(s, (256, 128))

    def body(dy_ref, o_ref, gate_ref, do_ref, dl_ref, dg_ref):
        gate = gate_ref[...].astype(F32)
        sg = _sigmoid(gate)
        dyv = dy_ref[...].astype(F32)
        ov = o_ref[...].astype(F32)
        do = dyv * (gate * sg)
        do_ref[...] = do.astype(BF16)
        dg_ref[...] = (dyv * ov * (sg * (1.0 + gate * (1.0 - sg)))).astype(BF16)
        prod = do * ov
        lane = lax.broadcasted_iota(jnp.int32, (tm, LANES), 1)
        dl = jnp.zeros((tm, LANES), F32)
        for h in range(hpg):
            sl = slice(h * HEAD_DIM, (h + 1) * HEAD_DIM)
            dl = jnp.where(lane == h, jnp.sum(prod[:, sl], axis=1, keepdims=True), dl)
        dl_ref[...] = dl

    spec = pl.BlockSpec((tm, w), lambda i: (i, 0))
    return pl.pallas_call(
        body, name="attn_pre_bwd", grid=(s // tm,),
        in_specs=[spec, spec, pl.BlockSpec((tm, w), lambda i: (i, gate_blk))],
        out_specs=[spec, pl.BlockSpec((tm, LANES), lambda i: (i, 0)), spec],
        out_shape=[jax.ShapeDtypeStruct((s, w), BF16), jax.ShapeDtypeStruct((s, LANES), F32),
                   jax.ShapeDtypeStruct((s, w), BF16)],
        compiler_params=_cparams(("parallel",)),
    )(dy, o, pa)


def _attn_bwd(pg, bias, do, lse, delta, g, dil, hpg):
    s = pg.shape[0]
    w = hpg * HEAD_DIM
    rows = s // dil
    nb = rows // ATTN_BLOCK
    dov = do.reshape(rows, dil * w)
    lsev, dlv = (t.reshape(rows, dil * LANES) for t in (lse, delta))
    scale = HEAD_DIM ** -0.5

    def body(q_ref, kc_ref, kp_ref, vc_ref, vp_ref, bias_ref, do_ref, lse_ref, dl_ref,
             dq_ref, dk_ref, dv_ref, db_ref, dkc_ref, dvc_ref):
        r, i = pl.program_id(0), pl.program_id(1)
        n = nb - 1 - i
        valid = _attn_valid(n == 0)

        @pl.when((r == 0) & (i == 0))
        def _():
            db_ref[...] = jnp.zeros_like(db_ref)

        @pl.when(i == 0)
        def _():
            dkc_ref[...] = jnp.zeros_like(dkc_ref)
            dvc_ref[...] = jnp.zeros_like(dvc_ref)

        for h in range(hpg):
            sl = slice(h * HEAD_DIM, (h + 1) * HEAD_DIM)
            q = q_ref[:, sl]
            dov_ = do_ref[:, sl]
            k2 = jnp.concatenate([kp_ref[:, sl], kc_ref[:, sl]], axis=0)
            v2 = jnp.concatenate([vp_ref[:, sl], vc_ref[:, sl]], axis=0)
            sc = _dot_nt(q, k2) * scale + bias_ref[h]
            p = jnp.exp(jnp.where(valid, sc - lse_ref[:, h:h + 1], NEG_INF))
            dp = _dot_nt(dov_, v2)
            ds = p * (dp - dl_ref[:, h:h + 1])
            db_ref[h] += ds
            dsb = ds.astype(BF16)
            dq_ref[:, sl] = (_dot(dsb, k2) * scale).astype(BF16)
            dk2 = _dot_tn(dsb, q) * scale
            dv2 = _dot_tn(p.astype(BF16), dov_)
            dk_ref[:, sl] = (dk2[ATTN_BLOCK:] + dkc_ref[:, sl]).astype(BF16)
            dv_ref[:, sl] = (dv2[ATTN_BLOCK:] + dvc_ref[:, sl]).astype(BF16)
            dkc_ref[:, sl] = dk2[:ATTN_BLOCK]
            dvc_ref[:, sl] = dv2[:ATTN_BLOCK]

    def col(off):
        return lambda r, i: (r * nb + nb - 1 - i, 3 * g + off)

    def colp(off):
        return lambda r, i: (r * nb + jnp.maximum(nb - 2 - i, 0), 3 * g + off)

    blk = (ATTN_BLOCK, w)
    tok = pl.BlockSpec(blk, lambda r, i: (nb - 1 - i, r))
    tok1 = pl.BlockSpec((ATTN_BLOCK, LANES), lambda r, i: (nb - 1 - i, r))
    dq, dk, dv, db = pl.pallas_call(
        body, name=f"attn_bwd_g{g}", grid=(dil, nb),
        in_specs=[pl.BlockSpec(blk, col(0)), pl.BlockSpec(blk, col(1)), pl.BlockSpec(blk, colp(1)),
                  pl.BlockSpec(blk, col(2)), pl.BlockSpec(blk, colp(2)),
                  pl.BlockSpec((None, hpg, ATTN_BLOCK, 2 * ATTN_BLOCK), lambda r, i: (g, 0, 0, 0)),
                  tok, tok1, tok1],
        out_specs=[tok, tok, tok,
                   pl.BlockSpec((hpg, ATTN_BLOCK, 2 * ATTN_BLOCK), lambda r, i: (0, 0, 0))],
        out_shape=[jax.ShapeDtypeStruct((rows, dil * w), BF16)] * 3
        + [jax.ShapeDtypeStruct((hpg, ATTN_BLOCK, 2 * ATTN_BLOCK), F32)],
        scratch_shapes=[pltpu.VMEM(blk, F32), pltpu.VMEM(blk, F32)],
        compiler_params=_cparams(("arbitrary", "arbitrary")),
    )(pg, pg, pg, pg, pg, bias, dov, lsev, dlv)
    return dq.reshape(s, w), dk.reshape(s, w), dv.reshape(s, w), db


def _ln_fwd(xin, h, gamma, beta, name, affine_in=None, target=None):
    s, d = xin.shape
    tm = _pick(s, (128,))
    has_aff = affine_in is not None
    has_tgt = target is not None

    def body(*refs):
        it = iter(refs)
        x_ref, h_ref, g_ref, b_ref = next(it), next(it), next(it), next(it)
        if has_aff:
            gi_ref, bi_ref = next(it), next(it)
        if has_tgt:
            t_ref = next(it)
        xh_ref, rs_ref = next(it), next(it)
        x = x_ref[...]
        if has_aff:
            x = x * gi_ref[...] + bi_ref[...]
        u = DEEPNORM_ALPHA * x + h_ref[...]
        mu = jnp.mean(u, axis=1, keepdims=True)
        uc = u - mu
        var = jnp.mean(uc * uc, axis=1, keepdims=True)
        rstd = lax.rsqrt(var + LN_EPS)
        xhat = uc * rstd
        xh_ref[...] = xhat
        rs_ref[...] = rstd
        y = xhat * g_ref[...] + b_ref[...]
        if has_tgt:
            dy_ref, l_ref = next(it), next(it)
            e = y - t_ref[...]
            dy_ref[...] = e * (1.0 / d)
            l_ref[...] = jnp.sum(e * e, axis=1, keepdims=True)
        else:
            y_ref = next(it)
            y_ref[...] = y.astype(BF16)

    row = pl.BlockSpec((tm, d), lambda i: (i, 0))
    vec = pl.BlockSpec((1, d), lambda i: (0, 0))
    one = pl.BlockSpec((tm, 1), lambda i: (i, 0))
    in_specs = [row, row, vec, vec] + ([vec, vec] if has_aff else []) + ([row] if has_tgt else [])
    args = [xin, h, gamma, beta] + (list(affine_in) if has_aff else []) + ([target] if has_tgt else [])
    out_specs = [row, one] + ([row, one] if has_tgt else [row])
    out_shape = [jax.ShapeDtypeStruct((s, d), F32), jax.ShapeDtypeStruct((s, 1), F32)]
    out_shape += ([jax.ShapeDtypeStruct((s, d), F32), jax.ShapeDtypeStruct((s, 1), F32)] if has_tgt
                  else [jax.ShapeDtypeStruct((s, d), BF16)])
    return pl.pallas_call(
        body, name=name, grid=(s // tm,), in_specs=in_specs, out_specs=out_specs, out_shape=out_shape,
        compiler_params=_cparams(("parallel",)),
    )(*args)


def _ln_bwd(dy, xhat, rstd, gamma, name):
    s, d = dy.shape
    tm = _pick(s, (128,))

    def body(dy_ref, xh_ref, rs_ref, g_ref, du_ref, dub_ref, dg_ref, db_ref):
        @pl.when(pl.program_id(0) == 0)
        def _():
            dg_ref[...] = jnp.zeros_like(dg_ref)
            db_ref[...] = jnp.zeros_like(db_ref)

        dyv = dy_ref[...]
        xh = xh_ref[...]
        dg_ref[...] += jnp.sum(dyv * xh, axis=0, keepdims=True)
        db_ref[...] += jnp.sum(dyv, axis=0, keepdims=True)
        dxh = dyv * g_ref[...]
        m1 = jnp.mean(dxh, axis=1, keepdims=True)
        m2 = jnp.mean(dxh * xh, axis=1, keepdims=True)
        du = rs_ref[...] * (dxh - m1 - xh * m2)
        du_ref[...] = du
        dub_ref[...] = du.astype(BF16)

    row = pl.BlockSpec((tm, d), lambda i: (i, 0))
    vec = pl.BlockSpec((1, d), lambda i: (0, 0))
    one = pl.BlockSpec((tm, 1), lambda i: (i, 0))
    return pl.pallas_call(
        body, name=name, grid=(s // tm,), in_specs=[row, row, one, vec],
        out_specs=[row, row, vec, vec],
        out_shape=[jax.ShapeDtypeStruct((s, d), F32), jax.ShapeDtypeStruct((s, d), BF16),
                   jax.ShapeDtypeStruct((1, d), F32), jax.ShapeDtypeStruct((1, d), F32)],
        compiler_params=_cparams(("arbitrary",)),
    )(dy, xhat, rstd, gamma)


_HALO = 16
_STRIP = 16


def _strips(tm, fn, init, reverse=False):
    n = tm // _STRIP

    def step(i, carry):
        s_ = n - 1 - i if reverse else i
        return fn(pl.ds(pl.multiple_of(s_ * _STRIP, _STRIP), _STRIP), carry)

    return lax.fori_loop(0, n, step, init)


def _fold8(t):
    return t[0:8] + t[8:16]


def _conv_taps(ext, tm, w_ref):
    acc = None
    for k in range(CONV_WIDTH):
        lo = _HALO - (CONV_WIDTH - 1) + k
        term = w_ref[k:k + 1, :] * ext[lo:lo + tm, :]
        acc = term if acc is None else acc + term
    return acc


def _conv_strip(prev, cur, w_ref):
    ext = jnp.concatenate([prev, cur], axis=0)
    acc, taps = None, []
    for k in range(CONV_WIDTH):
        lo = _STRIP - (CONV_WIDTH - 1) + k
        taps.append(ext[lo:lo + _STRIP, :])
        term = w_ref[k:k + 1, :] * taps[k]
        acc = term if acc is None else acc + term
    return acc, taps


def _conv_fwd(pzx, conv_w, conv_b, d_inner):
    s, _ = pzx.shape
    cd = conv_w.shape[1]
    tm = _pick(s, (512, 256, 128))
    tc = _pick(cd, (1024, 512, 256, 128))
    off = d_inner // tc
    hb = tm // _HALO

    def body(x_ref, p_ref, w_ref, b_ref, o_ref):
        prev = jnp.where(pl.program_id(0) > 0, p_ref[...].astype(F32), 0.0)
        ext = jnp.concatenate([prev, x_ref[...].astype(F32)], axis=0)
        pre = _conv_taps(ext, tm, w_ref) + b_ref[...]
        o_ref[...] = (pre * _sigmoid(pre)).astype(BF16)

    return pl.pallas_call(
        body, name="conv_fwd", grid=(s // tm, cd // tc),
        in_specs=[pl.BlockSpec((tm, tc), lambda i, j: (i, off + j)),
                  pl.BlockSpec((_HALO, tc), lambda i, j: (jnp.maximum(i * hb - 1, 0), off + j)),
                  pl.BlockSpec((CONV_WIDTH, tc), lambda i, j: (0, j)),
                  pl.BlockSpec((1, tc), lambda i, j: (0, j))],
        out_specs=pl.BlockSpec((tm, tc), lambda i, j: (i, j)),
        out_shape=jax.ShapeDtypeStruct((s, cd), BF16),
        compiler_params=_cparams(("parallel", "parallel")),
    )(pzx, pzx, conv_w, conv_b)


def _conv_bwd_a(pzx, dxbc, conv_w, conv_b, d_inner):
    s, _ = pzx.shape
    cd = conv_w.shape[1]
    tm = _pick(s, (512, 256, 128))
    tc = _pick(cd, (1024, 512, 256, 128))
    off = d_inner // tc
    hb = tm // _HALO

    def body(x_ref, p_ref, d_ref, w_ref, b_ref, o_ref, dw_ref, db_ref, acc_ref):
        @pl.when(pl.program_id(1) == 0)
        def _():
            dw_ref[...] = jnp.zeros_like(dw_ref)
            db_ref[...] = jnp.zeros_like(db_ref)

        acc_ref[...] = jnp.zeros_like(acc_ref)

        def strip(rows, prev):
            cur = x_ref[rows, :].astype(F32)
            pre, taps = _conv_strip(prev, cur, w_ref)
            pre = pre + b_ref[...]
            sg = _sigmoid(pre)
            dpre = d_ref[rows, :].astype(F32) * (sg * (1.0 + pre * (1.0 - sg)))
            o_ref[rows, :] = dpre
            for k in range(CONV_WIDTH):
                acc_ref[k] += _fold8(dpre * taps[k])
            acc_ref[CONV_WIDTH] += _fold8(dpre)
            return cur

        _strips(tm, strip, jnp.where(pl.program_id(1) > 0, p_ref[...].astype(F32), 0.0))
        for k in range(CONV_WIDTH):
            dw_ref[k:k + 1, :] += jnp.sum(acc_ref[k], axis=0, keepdims=True)
        db_ref[...] += jnp.sum(acc_ref[CONV_WIDTH], axis=0, keepdims=True)

    return pl.pallas_call(
        body, name="conv_bwd_a", grid=(cd // tc, s // tm),
        in_specs=[pl.BlockSpec((tm, tc), lambda j, i: (i, off + j)),
                  pl.BlockSpec((_HALO, tc), lambda j, i: (jnp.maximum(i * hb - 1, 0), off + j)),
                  pl.BlockSpec((tm, tc), lambda j, i: (i, j)),
                  pl.BlockSpec((CONV_WIDTH, tc), lambda j, i: (0, j)),
                  pl.BlockSpec((1, tc), lambda j, i: (0, j))],
        out_specs=[pl.BlockSpec((tm, tc), lambda j, i: (i, j)),
                   pl.BlockSpec((CONV_WIDTH, tc), lambda j, i: (0, j)),
                   pl.BlockSpec((1, tc), lambda j, i: (0, j))],
        out_shape=[jax.ShapeDtypeStruct((s, cd), F32), jax.ShapeDtypeStruct((CONV_WIDTH, cd), F32),
                   jax.ShapeDtypeStruct((1, cd), F32)],
        scratch_shapes=[pltpu.VMEM((CONV_WIDTH + 1, 8, tc), F32)],
        compiler_params=_cparams(("parallel", "arbitrary")),
    )(pzx, pzx, dxbc, conv_w, conv_b)


def _conv_bwd_b(dpre, conv_w, into, col_off):
    s, cd = dpre.shape
    tm = _pick(s, (512, 256, 128))
    tc = _pick(cd, (1024, 512, 256, 128))
    hb = tm // 8
    nrb = s // tm
    assert col_off % tc == 0
    co = col_off // tc

    def body(x_ref, nx_ref, w_ref, into_ref, o_ref):
        nxt = jnp.where(pl.program_id(0) < nrb - 1, nx_ref[...], 0.0)
        ext = jnp.concatenate([x_ref[...], nxt], axis=0)
        acc = None
        for k in range(CONV_WIDTH):
            lo = CONV_WIDTH - 1 - k
            term = w_ref[k:k + 1, :] * ext[lo:lo + tm, :]
            acc = term if acc is None else acc + term
        o_ref[...] = acc.astype(BF16)

    return pl.pallas_call(
        body, name="conv_bwd_b", grid=(nrb, cd // tc),
        in_specs=[pl.BlockSpec((tm, tc), lambda i, j: (i, j)),
                  pl.BlockSpec((8, tc), lambda i, j: (jnp.minimum((i + 1) * hb, s // 8 - 1), j)),
                  pl.BlockSpec((CONV_WIDTH, tc), lambda i, j: (0, j)), _ANY],
        out_specs=pl.BlockSpec((tm, tc), lambda i, j: (i, j + co)),
        out_shape=jax.ShapeDtypeStruct(into.shape, BF16),
        input_output_aliases={3: 0},
        compiler_params=_cparams(("parallel", "parallel")),
    )(dpre, dpre, conv_w, into)


def _expand_matrix():
    e = np.zeros((LANES, SSM_GROUP_WIDTH), np.float32)
    for h in range(HEADS_PER_SSM_GROUP):
        e[h, h * SSM_HEAD_DIM:(h + 1) * SSM_HEAD_DIM] = 1.0
    return jnp.asarray(e, BF16)


def _expand(t, e):
    return _dot(t.astype(BF16), e)


def _segsum(v, e):
    hi, lo = _split2(v)
    return _dot_nt(hi, e) + _dot_nt(lo, e)


def _tri_dot(tri, x):
    hi, mid, lo = _split3(x)
    return _dot(tri, hi) + _dot(tri, mid) + _dot(tri, lo)


def _ssd_common(dtp_ref, a_ref, dtb_ref, x_ref, e):
    li = lax.broadcasted_iota(jnp.int32, (CHUNK, CHUNK), 0)
    si = lax.broadcasted_iota(jnp.int32, (CHUNK, CHUNK), 1)
    causal = li >= si
    tril = causal.astype(BF16)
    raw = dtp_ref[...] + dtb_ref[...]
    dt = jnp.maximum(raw, 0.0) + jnp.log(1.0 + jnp.exp(-jnp.abs(raw)))
    head_lane = lax.broadcasted_iota(jnp.int32, (1, LANES), 1) < HEADS_PER_SSM_GROUP
    a = jnp.where(head_lane, -jnp.exp(a_ref[...]), 0.0)
    a_cum = _tri_dot(tril, dt * a)
    a_cum_t = a_cum.T
    e_a = jnp.exp(a_cum)
    to_end = jnp.exp(a_cum[CHUNK - 1:CHUNK, :] - a_cum)
    x = x_ref[...].astype(F32)
    dt_e = _expand(dt, e)
    return dict(causal=causal, raw=raw, dt=dt, a=a, a_cum=a_cum, a_cum_t=a_cum_t, e_a=e_a,
                to_end=to_end, x=x, dt_e=dt_e, xdt=x * dt_e, e_a_e=_expand(e_a, e),
                to_end_e=_expand(to_end, e))


def _decay(q, h):
    seg = q["a_cum"][:, h:h + 1] - q["a_cum_t"][h:h + 1, :]
    return jnp.exp(jnp.where(q["causal"], seg, -jnp.inf))


def _ssd_specs(ng, d_inner, rev, nc):
    cidx = (lambda i: nc - 1 - i) if rev else (lambda i: i)
    boff = d_inner // D_STATE
    return dict(
        xs=pl.BlockSpec((CHUNK, SSM_GROUP_WIDTH), lambda g, i: (cidx(i), g)),
        b=pl.BlockSpec((CHUNK, D_STATE), lambda g, i: (cidx(i), boff + g)),
        c=pl.BlockSpec((CHUNK, D_STATE), lambda g, i: (cidx(i), boff + ng + g)),
        dtp=pl.BlockSpec((None, CHUNK, LANES), lambda g, i: (g, cidx(i), 0)),
        vec=pl.BlockSpec((None, 1, LANES), lambda g, i: (g, 0, 0)),
        wide=pl.BlockSpec((None, 1, SSM_GROUP_WIDTH), lambda g, i: (g, 0, 0)),
        e=pl.BlockSpec((LANES, SSM_GROUP_WIDTH), lambda g, i: (0, 0)),
        st=pl.BlockSpec((None, None, D_STATE, SSM_GROUP_WIDTH), lambda g, i: (g, cidx(i), 0, 0)),
        tok=pl.BlockSpec((CHUNK, SSM_GROUP_WIDTH), lambda g, i: (cidx(i), g)),
        bc_out=pl.BlockSpec((CHUNK, D_STATE), lambda g, i: (cidx(i), g)),
    )


def _ssd_fwd(xbc, dtp, a_pad, dtb_pad, dsk_e, e, d_inner):
    s = xbc.shape[0]
    ng = d_inner // SSM_GROUP_WIDTH
    nc = s // CHUNK

    def body(x_ref, b_ref, c_ref, dtp_ref, a_ref, dtb_ref, dsk_ref, e_ref, y_ref, st_ref, state):
        lane = lax.broadcasted_iota(jnp.int32, (CHUNK, LANES), 1)
        @pl.when(pl.program_id(1) == 0)
        def _():
            state[...] = jnp.zeros_like(state)

        ev = e_ref[...]
        q = _ssd_common(dtp_ref, a_ref, dtb_ref, x_ref, ev)
        bm, cm = b_ref[...], c_ref[...]
        cb = _dot_nt(cm, bm)
        s0 = state[...]
        st_ref[...] = s0
        y = _dot(cm, s0.astype(BF16)) * q["e_a_e"] + dsk_ref[...] * q["x"]
        xdt = q["xdt"]
        left = lane[:, :] < SSM_HEAD_DIM
        for j in range(HEADS_PER_SSM_GROUP // 2):
            sl = slice(j * LANES, (j + 1) * LANES)
            x2 = xdt[:, sl]
            m0 = (cb * _decay(q, 2 * j)).astype(BF16)
            m1 = (cb * _decay(q, 2 * j + 1)).astype(BF16)
            mcat = jnp.concatenate([m0, m1], axis=1)
            xbd = jnp.concatenate([jnp.where(left, x2, 0.0), jnp.where(left, 0.0, x2)], axis=0).astype(BF16)
            y_ref[:, sl] = (y[:, sl] + _dot(mcat, xbd)).astype(BF16)
        state[...] = s0 * q["e_a_e"][CHUNK - 1:CHUNK, :] + _dot_tn(bm, (q["to_end_e"] * xdt).astype(BF16))

    sp = _ssd_specs(ng, d_inner, False, nc)
    return pl.pallas_call(
        body, name="ssd_fwd", grid=(ng, nc),
        in_specs=[sp["xs"], sp["b"], sp["c"], sp["dtp"], sp["vec"], sp["vec"], sp["wide"], sp["e"]],
        out_specs=[sp["tok"], sp["st"]],
        out_shape=[jax.ShapeDtypeStruct((s, d_inner), BF16),
                   jax.ShapeDtypeStruct((ng, nc, D_STATE, SSM_GROUP_WIDTH), F32)],
        scratch_shapes=[pltpu.VMEM((D_STATE, SSM_GROUP_WIDTH), F32)],
        compiler_params=_cparams(("parallel", "arbitrary")),
    )(xbc, xbc, xbc, dtp, a_pad, dtb_pad, dsk_e, e)


def _ssd_bwd(xbc, dtp, a_pad, dtb_pad, dsk_e, e, states, dy, d_inner):
    s = xbc.shape[0]
    ng = d_inner // SSM_GROUP_WIDTH
    nc = s // CHUNK

    def body(x_ref, b_ref, c_ref, dtp_ref, a_ref, dtb_ref, dsk_ref, e_ref, st_ref, dy_ref,
             dx_ref, db_ref, dc_ref, ddt_ref, da_ref, ddtb_ref, dd_ref, dstate):
        lane = lax.broadcasted_iota(jnp.int32, (CHUNK, LANES), 1)
        sub = lax.broadcasted_iota(jnp.int32, (CHUNK, LANES), 0)
        @pl.when(pl.program_id(1) == 0)
        def _():
            dstate[...] = jnp.zeros_like(dstate)
            da_ref[...] = jnp.zeros_like(da_ref)
            ddtb_ref[...] = jnp.zeros_like(ddtb_ref)
            dd_ref[...] = jnp.zeros_like(dd_ref)

        ev = e_ref[...]
        q = _ssd_common(dtp_ref, a_ref, dtb_ref, x_ref, ev)
        bm, cm = b_ref[...], c_ref[...]
        cb = _dot_nt(cm, bm)
        x, xdt, e_a_e, to_end_e = q["x"], q["xdt"], q["e_a_e"], q["to_end_e"]
        s0 = st_ref[...]
        s0b = s0.astype(BF16)
        ds1 = dstate[...]
        ds1b = ds1.astype(BF16)
        dy = dy_ref[...].astype(F32)
        e_last_e = e_a_e[CHUNK - 1:CHUNK, :]

        dye = dy * e_a_e
        dyeb = dye.astype(BF16)
        cs0 = _dot(cm, s0b)
        dc = _dot_nt(dyeb, s0b)
        dstate[...] = e_last_e * ds1 + _dot_tn(cm, dyeb)
        da_col = _segsum(dye * cs0, ev)

        gmat = _dot(bm, ds1b)
        dxdt = to_end_e * gmat
        dte = _segsum(xdt * gmat, ev) * q["to_end"]
        db = _dot_nt((to_end_e * xdt).astype(BF16), ds1b)
        da_col = da_col - dte
        last_row = (jnp.sum(dte, axis=0, keepdims=True)
                    + q["e_a"][CHUNK - 1:CHUNK, :] * jnp.sum(_segsum(s0 * ds1, ev), axis=0, keepdims=True))

        left = lane < SSM_HEAD_DIM
        dcb = jnp.zeros((CHUNK, CHUNK), F32)
        row_acc = jnp.zeros((CHUNK, LANES), F32)
        for j in range(HEADS_PER_SSM_GROUP // 2):
            sl = slice(j * LANES, (j + 1) * LANES)
            x2 = xdt[:, sl].astype(BF16)
            dy2 = dy[:, sl]
            dyl = jnp.where(left, dy2, 0.0).astype(BF16)
            dyr = jnp.where(left, 0.0, dy2).astype(BF16)
            ms = []
            for hh, dyh in ((0, dyl), (1, dyr)):
                h = 2 * j + hh
                dec = _decay(q, h)
                m = cb * dec
                dm = _dot_nt(dyh, x2)
                dcb = dcb + dm * dec
                dseg = dm * m
                da_col = da_col + jnp.where(lane == h, jnp.sum(dseg, axis=1, keepdims=True), 0.0)
                row_acc = row_acc + jnp.where(sub == h, jnp.sum(dseg, axis=0, keepdims=True), 0.0)
                ms.append(m.astype(BF16))
            mst = jnp.concatenate(ms, axis=0)
            dyst = jnp.concatenate([dyl, dyr], axis=0)
            d2 = dxdt[:, sl] + _dot_tn(mst, dyst)
            dx_ref[:, sl] = (d2 * q["dt_e"][:, sl] + dsk_ref[:, sl] * dy2).astype(BF16)
            dxdt_x = d2 * x[:, sl]
            if j == 0:
                parts = [dxdt_x]
            else:
                parts.append(dxdt_x)
        dcbb = dcb.astype(BF16)
        dc_ref[...] = (dc + _dot(dcbb, bm)).astype(BF16)
        db_ref[...] = (db + _dot_tn(dcbb, cm)).astype(BF16)

        d_a = da_col - row_acc.T + jnp.where(sub == CHUNK - 1, last_row, 0.0)
        triu = (lax.broadcasted_iota(jnp.int32, (CHUNK, CHUNK), 1)
                >= lax.broadcasted_iota(jnp.int32, (CHUNK, CHUNK), 0)).astype(BF16)
        d_dta = _tri_dot(triu, d_a)
        ddt = d_dta * q["a"] + _segsum(jnp.concatenate(parts, axis=1), ev)
        ddt_raw = ddt * _sigmoid(q["raw"])
        ddt_ref[...] = ddt_raw
        da_ref[...] += jnp.sum(d_dta * q["dt"], axis=0, keepdims=True) * q["a"]
        ddtb_ref[...] += jnp.sum(ddt_raw, axis=0, keepdims=True)
        dd_ref[...] += jnp.sum(dy * x, axis=0, keepdims=True)

    sp = _ssd_specs(ng, d_inner, True, nc)
    return pl.pallas_call(
        body, name="ssd_bwd", grid=(ng, nc),
        in_specs=[sp["xs"], sp["b"], sp["c"], sp["dtp"], sp["vec"], sp["vec"], sp["wide"], sp["e"],
                  sp["st"], sp["tok"]],
        out_specs=[sp["tok"], sp["bc_out"], sp["bc_out"], sp["dtp"], sp["vec"], sp["vec"], sp["wide"]],
        out_shape=[jax.ShapeDtypeStruct(xbc.shape, BF16),
                   jax.ShapeDtypeStruct((s, ng * D_STATE), BF16),
                   jax.ShapeDtypeStruct((s, ng * D_STATE), BF16),
                   jax.ShapeDtypeStruct((ng, s, LANES), F32),
                   jax.ShapeDtypeStruct((ng, 1, LANES), F32),
                   jax.ShapeDtypeStruct((ng, 1, LANES), F32),
                   jax.ShapeDtypeStruct((ng, 1, SSM_GROUP_WIDTH), F32)],
        scratch_shapes=[pltpu.VMEM((D_STATE, SSM_GROUP_WIDTH), F32)],
        compiler_params=_cparams(("parallel", "arbitrary")),
    )(xbc, xbc, xbc, dtp, a_pad, dtb_pad, dsk_e, e, states, dy)


def _gate_norm_fwd(y, pzx, norm_w):
    s, di = y.shape
    ng = di // SSM_GROUP_WIDTH
    tm = _pick(s, (512, 256, 128))

    def body(y_ref, z_ref, w_ref, o_ref):
        z = z_ref[...].astype(F32)
        y2 = y_ref[...].astype(F32) * (z * _sigmoid(z))
        r = lax.rsqrt(jnp.mean(y2 * y2, axis=1, keepdims=True) + RMS_EPS)
        o_ref[...] = (y2 * r * w_ref[...]).astype(BF16)

    blk = pl.BlockSpec((tm, SSM_GROUP_WIDTH), lambda i, g: (i, g))
    return pl.pallas_call(
        body, name="gate_norm_fwd", grid=(s // tm, ng),
        in_specs=[blk, blk, pl.BlockSpec((1, SSM_GROUP_WIDTH), lambda i, g: (0, g))],
        out_specs=blk, out_shape=jax.ShapeDtypeStruct((s, di), BF16),
        compiler_params=_cparams(("parallel", "parallel")),
    )(y, pzx, norm_w)


def _gate_norm_bwd(dy3, y, pzx, norm_w):
    s, di = y.shape
    ng = di // SSM_GROUP_WIDTH
    tm = _pick(s, (512, 256, 128))

    def body(d_ref, y_ref, z_ref, w_ref, dy_ref, dz_ref, dw_ref):
        @pl.when(pl.program_id(1) == 0)
        def _():
            dw_ref[...] = jnp.zeros_like(dw_ref)

        z = z_ref[...].astype(F32)
        yv = y_ref[...].astype(F32)
        sg = _sigmoid(z)
        sz = z * sg
        y2 = yv * sz
        r = lax.rsqrt(jnp.mean(y2 * y2, axis=1, keepdims=True) + RMS_EPS)
        nrm = y2 * r
        d3 = d_ref[...].astype(F32)
        dw_ref[...] += jnp.sum(d3 * nrm, axis=0, keepdims=True)
        dn = d3 * w_ref[...]
        dy2 = r * (dn - nrm * jnp.mean(dn * nrm, axis=1, keepdims=True))
        dy_ref[...] = (dy2 * sz).astype(BF16)
        dz_ref[...] = (dy2 * yv * (sg * (1.0 + z * (1.0 - sg)))).astype(BF16)

    blk = pl.BlockSpec((tm, SSM_GROUP_WIDTH), lambda g, i: (i, g))
    vec = pl.BlockSpec((1, SSM_GROUP_WIDTH), lambda g, i: (0, g))
    return pl.pallas_call(
        body, name="gate_norm_bwd", grid=(ng, s // tm),
        in_specs=[blk, blk, blk, vec], out_specs=[blk, blk, vec],
        out_shape=[jax.ShapeDtypeStruct((s, di), BF16), jax.ShapeDtypeStruct(pzx.shape, BF16),
                   jax.ShapeDtypeStruct((1, di), F32)],
        compiler_params=_cparams(("parallel", "arbitrary")),
    )(dy3, y, pzx, norm_w)


_ANY = pl.BlockSpec(memory_space=pl.ANY)


def _place():
    x, y, c = lax.axis_index("x"), lax.axis_index("y"), lax.axis_index("c")
    chips = [(1 - x, y), (x, 1 - y), (1 - x, 1 - y)]
    return x, y, c, chips


def _cast_to_slot(x, kvec, name, after=None):
    r, cn = x.shape
    tr = _rows_per_block(r, cn)
    extra = [] if after is None else [after]

    def body(k_ref, x_ref, *rest):
        rest[-1][...] = x_ref[...].astype(BF16)

    grid_spec = pltpu.PrefetchScalarGridSpec(
        num_scalar_prefetch=1, grid=(r // tr,),
        in_specs=[pl.BlockSpec((tr, cn), lambda i, k: (i, 0))] + [_ANY] * len(extra),
        out_specs=pl.BlockSpec((None, tr, cn), lambda i, k: (k[0], i, 0)))
    return pl.pallas_call(
        body, name=name, grid_spec=grid_spec, out_shape=jax.ShapeDtypeStruct((N_CHIPS, r, cn), BF16),
        compiler_params=_cparams(("parallel",)),
    )(kvec, x, *extra)


_HBM_SPEC = pl.BlockSpec(memory_space=pltpu.HBM)
_SEM_SPEC = pl.BlockSpec(memory_space=pltpu.SEMAPHORE)
_VMEM_SPEC = pl.BlockSpec(memory_space=pltpu.VMEM)
_EFFECT = pltpu.SideEffectType.DATAFLOW_SIDE_EFFECTING
_TOKEN = jax.ShapeDtypeStruct((8, LANES), F32)


def _hbm(a):
    return pltpu.with_memory_space_constraint(a, pltpu.HBM)


_NEIGHBOURS, _DIAGONAL, _ALL_CHIPS = (0, 1), (2,), (0, 1, 2)


def _gather_copies(bufs, refs, send_sems, recv_sems, forward, arrivals=True, which=_ALL_CHIPS):
    x, y, c, chips = _place()
    k = 2 * x + y
    out, arrive = [], []
    for w, ref in enumerate(refs):
        hr = bufs[w].shape[1] // 2
        for j, (cx, cy) in enumerate(chips):
            if j not in which:
                continue
            kj = 2 * cx + cy
            slot_out, slot_in, half_in = (kj, kj, 1 - c) if forward else (k, kj, c)
            to = (x, y, 1 - c) if forward else (cx, cy, c)
            src = ref.at[slot_out, pl.ds(c * hr, hr)]
            land = ref.at[slot_in, pl.ds(half_in * hr, hr)]
            out.append(pltpu.make_async_remote_copy(
                src_ref=src, dst_ref=src, send_sem=send_sems.at[3 * w + j], recv_sem=recv_sems.at[3 * w + j],
                device_id=to, device_id_type=MESH))
            if arrivals:
                arrive.append(pltpu.make_async_remote_copy(
                    src_ref=land, dst_ref=land, send_sem=send_sems.at[3 * w + j], recv_sem=recv_sems.at[3 * w + j],
                    device_id=to, device_id_type=MESH))
    return out, arrive


def _gather_start(bufs, forward, name, after=None, which=_ALL_CHIPS):
    n = len(bufs)
    extra = [] if after is None else [after]

    def body(*refs):
        ins = refs[:n]
        send_sems, recv_sems = refs[n + len(extra)], refs[n + len(extra) + 1]
        token = refs[-1]
        out, _ = _gather_copies(bufs, ins, send_sems, recv_sems, forward, arrivals=False, which=which)
        for cp in out:
            cp.start()
        token[...] = jnp.zeros_like(token)

    res = pl.pallas_call(
        body, name=name,
        out_shape=(pltpu.SemaphoreType.DMA((3 * n,)), pltpu.SemaphoreType.DMA((3 * n,)))
        + tuple(pltpu.HBM(b.shape, b.dtype) for b in bufs) + (_TOKEN,),
        in_specs=(_HBM_SPEC,) * n + (_ANY,) * len(extra),
        out_specs=(_SEM_SPEC, _SEM_SPEC) + (_HBM_SPEC,) * n + (_VMEM_SPEC,),
        input_output_aliases={w: 2 + w for w in range(n)},
        compiler_params=pltpu.CompilerParams(has_side_effects=_EFFECT),
    )(*[_hbm(b) for b in bufs], *extra)
    return res[0], res[1], list(res[2:2 + n]), res[-1]


def _gather_wait(bufs, send_sems, recv_sems, after, forward, name, which=_ALL_CHIPS):
    n = len(bufs)

    def body(*refs):
        ins = refs[:n]
        send_sems, recv_sems = refs[n], refs[n + 1]
        out, arrive = _gather_copies(bufs, ins, send_sems, recv_sems, forward, which=which)
        for cp in out:
            cp.wait_send()
        for cp in arrive:
            cp.wait_recv()

    res = pl.pallas_call(
        body, name=name,
        out_shape=tuple(pltpu.HBM(b.shape, b.dtype) for b in bufs),
        in_specs=(_HBM_SPEC,) * n + (_SEM_SPEC, _SEM_SPEC, _ANY), out_specs=(_HBM_SPEC,) * n,
        input_output_aliases={w: w for w in range(n)},
        compiler_params=pltpu.CompilerParams(has_side_effects=_EFFECT),
    )(*bufs, send_sems, recv_sems, after)
    return list(res)


def _swap_copy(g_ref, land_ref, send_sems, recv_sems):
    x, y, c, _ = _place()
    hr = g_ref.shape[1] // 2
    return pltpu.make_async_remote_copy(
        src_ref=g_ref.at[:, pl.ds((1 - c) * hr, hr)], dst_ref=land_ref, send_sem=send_sems.at[0],
        recv_sem=recv_sems.at[0], device_id=(x, y, 1 - c), device_id_type=MESH)


def _swap_start(g4, name):
    ns, r, cn = g4.shape

    def body(g_ref, land_ref, send_sems, recv_sems, g_thru, land_thru, token):
        _swap_copy(g_ref, land_ref, send_sems, recv_sems).start()
        token[...] = jnp.zeros_like(token)

    return pl.pallas_call(
        body, name=name,
        out_shape=(pltpu.SemaphoreType.DMA((1,)), pltpu.SemaphoreType.DMA((1,)),
                   pltpu.HBM(g4.shape, g4.dtype), pltpu.HBM((ns, r // 2, cn), g4.dtype), _TOKEN),
        in_specs=(_HBM_SPEC, _HBM_SPEC), out_specs=(_SEM_SPEC, _SEM_SPEC, _HBM_SPEC, _HBM_SPEC, _VMEM_SPEC),
        input_output_aliases={0: 2, 1: 3},
        compiler_params=pltpu.CompilerParams(has_side_effects=_EFFECT),
    )(_hbm(g4), _hbm(lax.empty((ns, r // 2, cn), g4.dtype)))


def _swap_wait(send_sems, recv_sems, g_thru, land_thru, after, name):
    def body(g_ref, land_ref, send_sems, recv_sems, after_ref, g_out, land_out):
        cp = _swap_copy(g_ref, land_ref, send_sems, recv_sems)
        cp.wait_send()
        cp.wait_recv()

    return pl.pallas_call(
        body, name=name,
        out_shape=(pltpu.HBM(g_thru.shape, g_thru.dtype), pltpu.HBM(land_thru.shape, land_thru.dtype)),
        in_specs=(_HBM_SPEC, _HBM_SPEC, _SEM_SPEC, _SEM_SPEC, _ANY), out_specs=(_HBM_SPEC, _HBM_SPEC),
        input_output_aliases={0: 0, 1: 1},
        compiler_params=pltpu.CompilerParams(has_side_effects=_EFFECT),
    )(g_thru, land_thru, send_sems, recv_sems, after)


def _join_copy(f_ref, send_sems, recv_sems, half):
    x, y, c, _ = _place()
    hr = f_ref.shape[0] // 2
    rows = f_ref.at[pl.ds(half * hr, hr)]
    return pltpu.make_async_remote_copy(
        src_ref=rows, dst_ref=rows, send_sem=send_sems.at[0], recv_sem=recv_sems.at[0],
        device_id=(x, y, 1 - c), device_id_type=MESH)


def _join_start(f, name):
    def body(f_ref, send_sems, recv_sems, f_thru, token):
        _join_copy(f_ref, send_sems, recv_sems, lax.axis_index("c")).start()
        token[...] = jnp.zeros_like(token)

    return pl.pallas_call(
        body, name=name,
        out_shape=(pltpu.SemaphoreType.DMA((1,)), pltpu.SemaphoreType.DMA((1,)), pltpu.HBM(f.shape, f.dtype), _TOKEN),
        in_specs=(_HBM_SPEC,), out_specs=(_SEM_SPEC, _SEM_SPEC, _HBM_SPEC, _VMEM_SPEC),
        input_output_aliases={0: 2},
        compiler_params=pltpu.CompilerParams(has_side_effects=_EFFECT),
    )(_hbm(f))


def _join_wait(send_sems, recv_sems, f_thru, after, name):
    def body(f_ref, send_sems, recv_sems, after_ref, f_out):
        c = lax.axis_index("c")
        _join_copy(f_ref, send_sems, recv_sems, c).wait_send()
        _join_copy(f_ref, send_sems, recv_sems, 1 - c).wait_recv()

    return pl.pallas_call(
        body, name=name, out_shape=pltpu.HBM(f_thru.shape, f_thru.dtype),
        in_specs=(_HBM_SPEC, _SEM_SPEC, _SEM_SPEC, _ANY), out_specs=_HBM_SPEC,
        input_output_aliases={0: 0},
        compiler_params=pltpu.CompilerParams(has_side_effects=_EFFECT),
    )(f_thru, send_sems, recv_sems, after)


def _place_bc(dxw, dbm, dcm, col_off):
    s, w = dbm.shape
    tm = _pick(s, (512, 256, 128))
    cb = col_off // w

    def body(b_ref, c_ref, x_ref, o_ref):
        o_ref[...] = jnp.where(pl.program_id(1) == 0, b_ref[...], c_ref[...])

    blk = pl.BlockSpec((tm, w), lambda i, j: (i, 0))
    return pl.pallas_call(
        body, name="place_db_dc", grid=(s // tm, 2), in_specs=[blk, blk, _ANY],
        out_specs=pl.BlockSpec((tm, w), lambda i, j: (i, cb + j)),
        out_shape=jax.ShapeDtypeStruct(dxw.shape, dxw.dtype), input_output_aliases={2: 0},
        compiler_params=_cparams(("parallel", "parallel")),
    )(dbm, dcm, dxw)


def _scatter_copies(t_ref, land_ref, send_sems, recv_sems, arrivals=True):
    x, y, c, chips = _place()
    k = 2 * x + y
    out, arrive = [], []
    for j, (cx, cy) in enumerate(chips):
        kj = 2 * cx + cy
        out.append(pltpu.make_async_remote_copy(
            src_ref=t_ref.at[kj], dst_ref=land_ref.at[k], send_sem=send_sems.at[j], recv_sem=recv_sems.at[j],
            device_id=(cx, cy, c), device_id_type=MESH))
        if arrivals:
            arrive.append(pltpu.make_async_remote_copy(
                src_ref=t_ref.at[kj], dst_ref=land_ref.at[kj], send_sem=send_sems.at[j], recv_sem=recv_sems.at[j],
                device_id=(cx, cy, c), device_id_type=MESH))
    return out, arrive


def _scatter_start(t, name):
    def body(t_ref, land_ref, send_sems, recv_sems, t_thru, land_thru, token):
        out, _ = _scatter_copies(t_ref, land_ref, send_sems, recv_sems, arrivals=False)
        for cp in out:
            cp.start()
        token[...] = jnp.zeros_like(token)

    return pl.pallas_call(
        body, name=name,
        out_shape=(pltpu.SemaphoreType.DMA((3,)), pltpu.SemaphoreType.DMA((3,)),
                   pltpu.HBM(t.shape, t.dtype), pltpu.HBM(t.shape, t.dtype), _TOKEN),
        in_specs=(_HBM_SPEC, _HBM_SPEC), out_specs=(_SEM_SPEC, _SEM_SPEC, _HBM_SPEC, _HBM_SPEC, _VMEM_SPEC),
        input_output_aliases={0: 2, 1: 3},
        compiler_params=pltpu.CompilerParams(has_side_effects=_EFFECT),
    )(_hbm(t), _hbm(lax.empty(t.shape, t.dtype)))


def _scatter_wait(send_sems, recv_sems, t_thru, land_thru, after, name):
    def body(t_ref, land_ref, send_sems, recv_sems, after_ref, t_out, land_out):
        out, arrive = _scatter_copies(t_ref, land_ref, send_sems, recv_sems)
        for cp in out:
            cp.wait_send()
        for cp in arrive:
            cp.wait_recv()

    return pl.pallas_call(
        body, name=name,
        out_shape=(pltpu.HBM(t_thru.shape, t_thru.dtype), pltpu.HBM(land_thru.shape, land_thru.dtype)),
        in_specs=(_HBM_SPEC, _HBM_SPEC, _SEM_SPEC, _SEM_SPEC, _ANY), out_specs=(_HBM_SPEC, _HBM_SPEC),
        input_output_aliases={0: 0, 1: 1},
        compiler_params=pltpu.CompilerParams(has_side_effects=_EFFECT),
    )(t_thru, land_thru, send_sems, recv_sems, after)


def _all_gather_small(v, reduce, name):
    r, l = v.shape

    def body(v_ref, o_ref, *rest):
        if reduce:
            buf, send_sems, recv_sems = rest
        else:
            buf = o_ref
            send_sems, recv_sems = rest
        x, y, c, _ = _place()
        me = 4 * x + 2 * y + c
        buf[me] = v_ref[...]
        cps = []
        for d in range(1, N_DEV):
            peer = (x if d & 4 == 0 else 1 - x, y if d & 2 == 0 else 1 - y, c if d & 1 == 0 else 1 - c)
            cp = pltpu.make_async_remote_copy(
                src_ref=v_ref, dst_ref=buf.at[me], send_sem=send_sems.at[d - 1], recv_sem=recv_sems.at[d - 1],
                device_id=peer, device_id_type=MESH)
            cp.start()
            cps.append((cp, peer))
        for d, (cp, (px, py, pc)) in enumerate(cps, start=1):
            pltpu.make_async_remote_copy(
                src_ref=v_ref, dst_ref=buf.at[4 * px + 2 * py + pc], send_sem=send_sems.at[d - 1],
                recv_sem=recv_sems.at[d - 1], device_id=(px, py, pc), device_id_type=MESH).wait_recv()
        for cp, _ in cps:
            cp.wait_send()
        if reduce:
            acc = buf[0]
            for i in range(1, N_DEV):
                acc = acc + buf[i]
            o_ref[...] = acc

    vm = pl.BlockSpec(memory_space=pltpu.VMEM)
    out_shape = jax.ShapeDtypeStruct((r, l) if reduce else (N_DEV, r, l), F32)
    scratch = ([pltpu.VMEM((N_DEV, r, l), F32)] if reduce else []) + [
        pltpu.SemaphoreType.DMA((N_DEV - 1,)), pltpu.SemaphoreType.DMA((N_DEV - 1,))]
    return pl.pallas_call(
        body, name=name, in_specs=[vm], out_specs=vm, out_shape=out_shape, scratch_shapes=scratch,
    )(v)


_BLOCK_BYTES = 3 * 512 * 1024


def _rows_per_block(r, cn, itemsize=4):
    best = 8
    for t in range(8, r + 1, 8):
        if r % t == 0 and t * cn * itemsize <= _BLOCK_BYTES:
            best = t
    return best


def _add_sibling_half(g4, recv, cvec, name):
    ns, r, cn = g4.shape
    hr = r // 2
    tr = _rows_per_block(hr, cn, itemsize=2)
    nrb = hr // tr

    def body(c_ref, a_ref, b_ref, o_ref):
        o_ref[...] = (a_ref[...].astype(F32) + b_ref[...].astype(F32)).astype(o_ref.dtype)

    grid_spec = pltpu.PrefetchScalarGridSpec(
        num_scalar_prefetch=1, grid=(ns, nrb),
        in_specs=[pl.BlockSpec((None, tr, cn), lambda j, i, c: (j, c[0] * nrb + i, 0)),
                  pl.BlockSpec((None, tr, cn), lambda j, i, c: (j, i, 0))],
        out_specs=pl.BlockSpec((None, tr, cn), lambda j, i, c: (j, i, 0)))
    return pl.pallas_call(
        body, name=name, grid_spec=grid_spec, out_shape=jax.ShapeDtypeStruct((ns, hr, cn), BF16),
        compiler_params=_cparams(("parallel", "parallel")),
    )(cvec, g4, recv)


def _sum_chips(r4, t4, kvec, cvec, name):
    ns, hr, cn = r4.shape
    tr = _rows_per_block(hr, cn, itemsize=2)
    nrb = hr // tr

    def body(k_ref, c_ref, r_ref, t_ref, o_ref):
        acc = t_ref[...].astype(F32)
        for dlt in range(1, ns):
            acc = acc + r_ref[(k_ref[0] + dlt) % ns].astype(F32)
        o_ref[...] = acc

    grid_spec = pltpu.PrefetchScalarGridSpec(
        num_scalar_prefetch=2, grid=(nrb,),
        in_specs=[pl.BlockSpec((ns, tr, cn), lambda i, k, c: (0, i, 0)),
                  pl.BlockSpec((None, tr, cn), lambda i, k, c: (k[0], i, 0))],
        out_specs=pl.BlockSpec((tr, cn), lambda i, k, c: (c[0] * nrb + i, 0)))
    return pl.pallas_call(
        body, name=name, grid_spec=grid_spec, out_shape=jax.ShapeDtypeStruct((2 * hr, cn), F32),
        compiler_params=_cparams(("parallel",)),
    )(kvec, cvec, r4, t4)


def _adamw(w, g, m, v, name):
    r, cn = w.shape
    tr = _rows_per_block(r, cn)
    c1 = 1.0 - ADAM_B1 ** ADAM_STEP
    c2 = 1.0 - ADAM_B2 ** ADAM_STEP

    def body(w_ref, g_ref, m_ref, v_ref, go_ref, d_ref, mo_ref, vo_ref):
        gv = g_ref[...]
        mn = ADAM_B1 * m_ref[...] + (1.0 - ADAM_B1) * gv
        vn = ADAM_B2 * v_ref[...] + (1.0 - ADAM_B2) * (gv * gv)
        go_ref[...] = gv
        mo_ref[...] = mn
        vo_ref[...] = vn
        d_ref[...] = -ADAM_LR * ((mn / c1) / (jnp.sqrt(vn / c2) + ADAM_EPS) + ADAM_WD * w_ref[...])

    spec = pl.BlockSpec((tr, cn), lambda i: (i, 0))
    return pl.pallas_call(
        body, name=name, grid=(r // tr,), in_specs=[spec] * 4, out_specs=[spec] * 4,
        out_shape=[jax.ShapeDtypeStruct((r, cn), F32)] * 4,
        compiler_params=_cparams(("parallel",)),
    )(w, g, m, v)


def _pack(arrs):
    flat = jnp.concatenate([a.reshape(-1).astype(F32) for a in arrs])
    n = flat.shape[0]
    tot = -(-n // (8 * LANES)) * (8 * LANES)
    return jnp.pad(flat, (0, tot - n)).reshape(tot // LANES, LANES)


def _unpack(packed, shapes):
    flat = packed.reshape(-1)
    out, off = [], 0
    for shp in shapes:
        sz = int(np.prod(shp))
        out.append(flat[off:off + sz].reshape(shp))
        off += sz
    return out


class _LocalExchange:
    def __init__(self, wa4, woa4, ws4, wos4):
        self.l0, self.ssm = [wa4, woa4], [ws4, wos4]
        self.grads = {}

    def l0_begin(self):
        return self.l0[0], None

    def l0_neighbours(self, after):
        return self.l0[0]

    def l0_diagonal(self, after):
        return self.l0

    def ssm_gather_mid(self, after):
        return None

    def ssm_gather_end(self, after):
        return self.ssm

    def grad_begin(self, name, g4):
        self.grads[name] = g4
        return None

    def grad_mid(self, name, after):
        return None

    def grad_sync(self, name, after):
        pass

    def small_grads(self, small_full):
        self.small = small_full
        return None


class _Exchange:
    def __init__(self, kvec, cvec, l0_bufs, after):
        self.kvec, self.cvec, self.l0_bufs, self.bufs, self.after = kvec, cvec, l0_bufs, None, after
        self.pending, self.summed, self.last_token, self.l0_token, self.swaps, self.joins = {}, {}, None, None, {}, {}

    def l0_begin(self):
        if self.l0_token is None:
            self.l0_send, self.l0_recv, self.l0_bufs, self.l0_token = _gather_start(
                self.l0_bufs, False, "l0_gather_ici_start", self.after)
        return self.l0_bufs[0], (self.l0_token if self.bufs is None else self.sems[3])

    def _l0_step(self, which, tag, after):
        bufs = _gather_wait(self.l0_bufs, self.l0_send, self.l0_recv, after, False, "l0_gather_ici_wait_" + tag, which)
        send, recv, bufs, token = _gather_start(bufs, True, "l0_gather_fwd_start_" + tag, None, which)
        self.l0_bufs = _gather_wait(bufs, send, recv, token, True, "l0_gather_fwd_wait_" + tag, which)
        return self.l0_bufs

    def l0_neighbours(self, after):
        return self._l0_step(_NEIGHBOURS, "nb", after)[0]

    def l0_diagonal(self, after):
        return self._l0_step(_DIAGONAL, "diag", after)

    def ssm_gather_start(self, ssm_bufs):
        self.sems = _gather_start(ssm_bufs, False, "ssm_gather_ici_start", self.l0_token)
        self.bufs = self.sems[2]
        return self.sems[3]

    def ssm_gather_mid(self, after):
        bufs = _gather_wait(self.bufs, self.sems[0], self.sems[1], after, False, "ssm_gather_ici_wait")
        self.sems = _gather_start(bufs, True, "ssm_gather_fwd_start")
        self.bufs = self.sems[2]
        return self.sems[3]

    def ssm_gather_end(self, after):
        return _gather_wait(self.bufs, self.sems[0], self.sems[1], after, True, "ssm_gather_fwd_wait")

    def small_grads(self, small_full):
        packed = _all_gather_small(_pack(small_full), True, "reduce_small_grads")
        self.small = _unpack(packed, [t.shape for t in small_full])
        return packed

    def grad_begin(self, name, g4):
        self.swaps[name] = _swap_start(g4, "swap_start_" + name)
        return self.swaps[name][4]

    def grad_mid(self, name, after):
        send_sem, recv_sem, g_thru, land, _ = self.swaps.pop(name)
        g4, recv = _swap_wait(send_sem, recv_sem, g_thru, land, after, "swap_wait_" + name)
        return self._reduce(name, g4, recv)

    def _reduce(self, name, g4, recv):
        t = _add_sibling_half(g4, recv, self.cvec, "add_sibling_" + name)
        send_sems, recv_sems, t_thru, land, token = _scatter_start(t, "scatter_start_" + name)
        self.pending[name] = (send_sems, recv_sems, t_thru, land)
        self.last_token = token
        return token

    def grad_sync(self, name, after):
        t, land = _scatter_wait(*self.pending.pop(name), after, "scatter_wait_" + name)
        summed = _sum_chips(land, t, self.kvec, self.cvec, "sum_chips_" + name)
        self.joins[name] = _join_start(summed, "join_start_" + name)

    def grad_end(self, name, after):
        send_sem, recv_sem, f_thru, _ = self.joins.pop(name)
        return _join_wait(send_sem, recv_sem, f_thru, after, "join_wait_" + name)


def _tie(vec, token):
    return vec if token is None else vec + token[0:1, 0:1].reshape((1,) * vec.ndim).astype(vec.dtype)


def _local_step(x2, tgt, ex, kvec, conv_w_f, conv_b_f, norm_w_f, rel_bias, dt_bias, a_log, d_skip,
                ln_g, ln_b):
    s, d = x2.shape
    wa4, tok = ex.l0_begin()
    d_attn = wa4.shape[2] * N_CHIPS // 10
    hpg = d_attn // HEAD_DIM
    d_inner = norm_w_f.shape[1]
    ng = d_inner // SSM_GROUP_WIDTH
    n_heads = dt_bias.shape[1]
    conv_dim = conv_w_f.shape[1]
    assert n_heads == ng * HEADS_PER_SSM_GROUP and conv_dim == d_inner + 2 * ng * D_STATE

    x3 = _cast_x3(x2, "cast_x", tok)
    for g, (_, dil) in enumerate(ATTN_PATTERNS):
        if dil > 1:
            x3 = _class_copy(x3, g, dil, f"class_order_x_g{g}")
    xb = x3[0]
    buckets = _bucket_tiles()
    bias = _bias_expand(rel_bias, buckets, hpg)
    pa = _in_proj_shard(x3, wa4, kvec, 0, d_attn, "mm_in_attn_own", after=tok)
    wa4 = ex.l0_neighbours(pa)
    pa = _in_proj_shard(x3, wa4, kvec, 1, d_attn, "mm_in_attn_nb1", into=pa)
    pa = _in_proj_shard(x3, wa4, kvec, 2, d_attn, "mm_in_attn_nb2", into=pa)
    wa4, woa4 = ex.l0_diagonal(pa)
    pa = _in_proj_shard(x3, wa4, kvec, 3, d_attn, "mm_in_attn_diag", into=pa)
    og, lg = [], []
    for g, (_, dil) in enumerate(ATTN_PATTERNS):
        o_, l_ = _attn_fwd(pa, bias, g, dil, hpg)
        og.append(o_)
        lg.append(l_)
    o, lse, yat = _attn_combine(og, lg, pa, hpg)
    h0 = _mm_nn_sharded(yat, woa4, F32, "mm_out_attn", after=ex.ssm_gather_mid(yat))
    g0, b0, g1, b1 = ln_g[0:1], ln_b[0:1], ln_g[1:2], ln_b[1:2]
    xhat0, rstd0, x1b = _ln_fwd(x2, h0, g0, b0, "ln0_fwd")

    wst4, wos4 = ex.ssm_gather_end(x1b)
    wst = wst4.reshape(N_CHIPS * wst4.shape[1], d)
    nzx = d_inner + conv_dim
    wos = wos4.reshape(d_inner, d)
    pzx = _mm_nt(x1b, wst, BF16, "mm_in_ssm", n=nzx)
    dt_raw = _mm_nt(x1b, wst, F32, "mm_in_dt", n=n_heads, b_row_off=nzx)

    def pad_heads(t):
        t = t.reshape(t.shape[0], ng, HEADS_PER_SSM_GROUP).transpose(1, 0, 2)
        return jnp.pad(t, ((0, 0), (0, 0), (0, LANES - HEADS_PER_SSM_GROUP)))

    def unpad_heads(t):
        return t[:, :, :HEADS_PER_SSM_GROUP].transpose(1, 0, 2).reshape(t.shape[1], n_heads)

    dtp = pad_heads(dt_raw)
    alog_p, dtb_p = pad_heads(a_log), pad_heads(dt_bias)
    dsk_e = jnp.repeat(d_skip.reshape(ng, 1, HEADS_PER_SSM_GROUP), SSM_HEAD_DIM, axis=2)
    e = _expand_matrix()
    xbc = _conv_fwd(pzx, conv_w_f, conv_b_f, d_inner)
    y_ssd, states = _ssd_fwd(xbc, dtp, alog_p, dtb_p, dsk_e, e, d_inner)
    y3 = _gate_norm_fwd(y_ssd, pzx, norm_w_f)
    h1 = _mm_nn(y3, wos, F32, "mm_out_ssm")
    xhat1, rstd1, dy2, row_sq = _ln_fwd(xhat0, h1, g1, b1, "ln1_fwd_loss", affine_in=(g0, b0), target=tgt)
    loss_local = 0.5 * jnp.sum(row_sq) / d

    du1, du1b, dg1, db1 = _ln_bwd(dy2, xhat1, rstd1, g1, "ln1_bwd")
    dy3 = _mm_nt(du1b, wos, BF16, "mm_d_y3")
    g_wos = _mm_tn(y3, du1b, BF16, "mm_g_w_out_ssm").reshape(N_CHIPS, d_inner // N_CHIPS, d)
    norm_w_t = _tie(norm_w_f, ex.grad_begin("w_out_ssm", g_wos))
    dy_ssd, dz, d_nw = _gate_norm_bwd(dy3, y_ssd, pzx, norm_w_t)
    dsk_t = _tie(dsk_e, ex.grad_mid("w_out_ssm", dy_ssd))
    dxs, dbm, dcm, ddtp, d_alog, d_dtb, d_dsk = _ssd_bwd(xbc, dtp, alog_p, dtb_p, dsk_t, e, states, dy_ssd, d_inner)
    dpre, d_cw, d_cb = _conv_bwd_a(pzx, _place_bc(dxs, dbm, dcm, d_inner), conv_w_f, conv_b_f, d_inner)
    dpzx = _conv_bwd_b(dpre, conv_w_f, dz, d_inner)
    ddt_raw = unpad_heads(ddtp)
    t1 = _mm_nn(ddt_raw, wst, F32, "mm_d_x1_dt", b_row_off=nzx, add=du1, add_scale=DEEPNORM_ALPHA)
    dx1 = _mm_nn(dpzx, wst, F32, "mm_d_x1", add=t1)
    ex.grad_sync("w_out_ssm", dx1)
    g_wst = _mm_tn(dpzx, x1b, BF16, "mm_g_w_in_ssm", out_rows=wst.shape[0])
    g_wst = _mm_tn(ddt_raw, x1b, BF16, "mm_g_w_dt", out_rows=wst.shape[0], out_row_off=nzx, into=g_wst)
    g0_t = _tie(g0, ex.grad_begin("w_in_ssm", g_wst.reshape(wst4.shape)))

    du0, du0b, dg0, db0 = _ln_bwd(dx1, xhat0, rstd0, g0_t, "ln0_bwd")
    dyat = _mm_nt_sharded_k(du0b, woa4, BF16, "mm_d_yat")
    g_woa = _mm_tn(yat, du0b, BF16, "mm_g_w_out_attn", shard_cols=d // N_CHIPS)
    lse_t = _tie(lse, ex.grad_mid("w_in_ssm", g_woa))
    ex.grad_begin("w_out_attn", g_woa)
    do, delta, dgate = _attn_pre_bwd(dyat, o, pa, hpg)
    pieces, dbt = [], []
    for g, (_, dil) in enumerate(ATTN_PATTERNS):
        dq, dk, dv, db_ = _attn_bwd(pa, bias, do, lse_t, delta, g, dil, hpg)
        pieces += [dq, dk, dv]
        dbt.append(db_)
    dpa = jnp.concatenate(pieces + [dgate], axis=1)
    tok_woa = ex.grad_mid("w_out_attn", dpa)
    g_wa = _mm_tn(xb, dpa, BF16, "mm_g_w_in_attn", shard_cols=wa4.shape[2], after=tok_woa)
    ex.grad_begin("w_in_attn", g_wa)
    ex.grad_sync("w_in_ssm", g_wa)
    ex.grad_sync("w_out_attn", g_wa)
    d_rel = _bias_reduce(jnp.stack(dbt), buckets, hpg)[:, :, 0].T
    d_dsk_h = d_dsk.reshape(n_heads, SSM_HEAD_DIM).sum(axis=1)
    small_full = [d_rel, d_cw, d_cb, unpad_heads(d_dtb), unpad_heads(d_alog), d_dsk_h[None], d_nw,
                  jnp.concatenate([dg0, dg1], axis=0), jnp.concatenate([db0, db1], axis=0)]
    tok_wa = ex.grad_mid("w_in_attn", ex.small_grads(small_full))
    grad_x = _mm_nt_sharded_k(dpa, wa4, F32, "mm_d_x0", add=du0, add_scale=DEEPNORM_ALPHA, after=tok_wa)
    return loss_local, grad_x[None]


def kernel(x, w_in_attn, w_out_attn, rel_bias, w_in_ssm, conv_w, conv_b, dt_bias, a_log, d_skip, ssm_norm_w, w_out_ssm, ln_g, ln_b, loss_target, m_w_in_attn, m_w_out_attn, m_rel_bias, m_w_in_ssm, m_conv_w, m_conv_b, m_dt_bias, m_a_log, m_d_skip, m_ssm_norm_w, m_w_out_ssm, m_ln_g, m_ln_b, v_w_in_attn, v_w_out_attn, v_rel_bias, v_w_in_ssm, v_conv_w, v_conv_b, v_dt_bias, v_a_log, v_d_skip, v_ssm_norm_w, v_w_out_ssm, v_ln_g, v_ln_b):
    xi, yi, ci = lax.axis_index("x"), lax.axis_index("y"), lax.axis_index("c")
    chip = 2 * xi + yi
    cvec = jnp.reshape(ci, (1,)).astype(jnp.int32)
    kvec = jnp.reshape(chip, (1,)).astype(jnp.int32)

    cw_l, cb_l, nw_l = conv_w[0], conv_b[0], ssm_norm_w[0]
    vec_shapes = [cw_l.shape, cb_l.shape, nw_l.shape]
    vec_all = _all_gather_small(_pack([cw_l, cb_l, nw_l]), False, "gather_vectors")
    l0 = [_cast_to_slot(w_in_attn[0], kvec, "cast_w_in_attn"), _cast_to_slot(w_out_attn[0], kvec, "cast_w_out_attn")]
    ex = _Exchange(kvec, cvec, l0, after=vec_all)
    _, tok0 = ex.l0_begin()
    tok1 = ex.ssm_gather_start([_cast_to_slot(w_in_ssm[0].T, kvec, "cast_w_in_ssm", tok0),
                                _cast_to_slot(w_out_ssm[0], kvec, "cast_w_out_ssm", tok0)])
    parts = [_unpack(vec_all[2 * j], vec_shapes) for j in range(N_CHIPS)]
    conv_w_f = jnp.concatenate([p[0] for p in parts], axis=1)
    conv_b_f = jnp.concatenate([p[1] for p in parts], axis=0)[None]
    norm_w_f = jnp.concatenate([p[2] for p in parts], axis=0)[None]

    loss_local, grad_x = _local_step(
        x[0], loss_target[0], ex, kvec, conv_w_f, conv_b_f, norm_w_f, rel_bias, dt_bias, a_log,
        d_skip, ln_g, ln_b)
    loss = lax.psum(loss_local, ("x", "y", "c"))

    big_w = dict(w_in_attn=(w_in_attn, m_w_in_attn, v_w_in_attn), w_out_attn=(w_out_attn, m_w_out_attn, v_w_out_attn),
                 w_in_ssm=(w_in_ssm, m_w_in_ssm, v_w_in_ssm), w_out_ssm=(w_out_ssm, m_w_out_ssm, v_w_out_ssm))
    big = {}

    def finish(names, after):
        last = after
        for nm in names:
            gf = ex.grad_end(nm, last)
            flip = (lambda t: t.T) if nm == "w_in_ssm" else (lambda t: t)
            w_, m_, v_ = (flip(t[0]) for t in big_w[nm])
            res = _adamw(w_, gf, m_, v_, "adamw_" + nm)
            big[nm] = [flip(t)[None] for t in res]
            last = res[3]
        return last

    last = finish(["w_out_ssm", "w_in_ssm", "w_out_attn"], grad_x)
    ex.grad_sync("w_in_attn", last)
    finish(["w_in_attn"], last)

    s_rel, s_cw, s_cb, s_dtb, s_alog, s_dsk, s_nw, s_lng, s_lnb = ex.small
    cwc, nwc = conv_w.shape[2], ssm_norm_w.shape[1]
    s_cw = lax.dynamic_slice_in_dim(s_cw, chip * cwc, cwc, axis=1)[None]
    s_cb = lax.dynamic_slice_in_dim(s_cb, chip * cwc, cwc, axis=1)
    s_nw = lax.dynamic_slice_in_dim(s_nw, chip * nwc, nwc, axis=1)
    small_names = ["rel_bias", "conv_w", "conv_b", "dt_bias", "a_log", "d_skip", "ssm_norm_w", "ln_g", "ln_b"]
    small_g = [s_rel, s_cw, s_cb, s_dtb, s_alog, s_dsk, s_nw, s_lng, s_lnb]
    small_w = [rel_bias, conv_w, conv_b, dt_bias, a_log, d_skip, ssm_norm_w, ln_g, ln_b]
    small_m = [m_rel_bias, m_conv_w, m_conv_b, m_dt_bias, m_a_log, m_d_skip, m_ssm_norm_w, m_ln_g, m_ln_b]
    small_v = [v_rel_bias, v_conv_w, v_conv_b, v_dt_bias, v_a_log, v_d_skip, v_ssm_norm_w, v_ln_g, v_ln_b]
    shapes = [t.shape for t in small_w]
    res = _adamw(_pack(small_w), _pack(small_g), _pack(small_m), _pack(small_v), "adamw_small")
    small = {nm: [] for nm in small_names}
    for packed in res:
        for nm, t in zip(small_names, _unpack(packed, shapes)):
            small[nm].append(t)

    order = ["w_in_attn", "w_out_attn", "rel_bias", "w_in_ssm", "conv_w", "conv_b", "dt_bias", "a_log",
             "d_skip", "ssm_norm_w", "w_out_ssm", "ln_g", "ln_b"]
    table = {**big, **small}
    outs = [loss, grad_x]
    for kind in range(4):
        outs += [table[nm][kind] for nm in order]
    return tuple(outs)
```

```python
import math

import numpy as np
import jax
import jax.numpy as jnp
from jax import lax
from jax.experimental import pallas as pl
from jax.experimental.pallas import tpu as pltpu

F32 = jnp.float32
BF16 = jnp.bfloat16
MESH = pl.DeviceIdType.MESH

ATTN_PATTERNS = ((128, 1), (512, 4), (2048, 16))
N_GROUPS_ATTN = 3
HEAD_DIM = 128
ATTN_BLOCK = 128
NUM_BUCKETS = 32
MAX_DISTANCE = 2048
SSM_HEAD_DIM = 64
HEADS_PER_SSM_GROUP = 16
SSM_GROUP_WIDTH = HEADS_PER_SSM_GROUP * SSM_HEAD_DIM
D_STATE = 128
CONV_WIDTH = 4
CHUNK = 128
DEPTH = 2
DEEPNORM_ALPHA = (2 * DEPTH) ** 0.25
LN_EPS = 1e-5
RMS_EPS = 1e-5
NEG_INF = -1e30
ADAM_LR = 0.001
ADAM_B1 = 0.9
ADAM_B2 = 0.999
ADAM_EPS = 1e-08
ADAM_WD = 0.01
ADAM_STEP = 10

N_CHIPS = 4
N_DEV = 8

VMEM_LIMIT_V7X = 56 * 1024 * 1024
LANES = 128


def _cparams(sem=None):
    return pltpu.CompilerParams(dimension_semantics=sem, vmem_limit_bytes=VMEM_LIMIT_V7X)


def _sigmoid(x):
    return 0.5 * jnp.tanh(0.5 * x) + 0.5


def _dot(a, b):
    return jnp.dot(a, b, preferred_element_type=F32)


def _dot_nt(a, b):
    return lax.dot_general(a, b, (((1,), (1,)), ((), ())), preferred_element_type=F32)


def _dot_tn(a, b):
    return lax.dot_general(a, b, (((0,), (0,)), ((), ())), preferred_element_type=F32)


def _split2(x):
    hi = x.astype(BF16)
    lo = (x - hi.astype(F32)).astype(BF16)
    return hi, lo


def _split3(x):
    hi = x.astype(BF16)
    r = x - hi.astype(F32)
    mid = r.astype(BF16)
    lo = (r - mid.astype(F32)).astype(BF16)
    return hi, mid, lo


def _matmul(a, b, *, mode, grid, a_spec, b_spec, out_shape, out_spec, tile, name,
            add=None, add_spec=None, add_scale=1.0, after=None, into=None):
    nk = grid[2]
    tm, tn = tile
    dot = {"nn": _dot, "nt": _dot_nt, "tn": _dot_tn}[mode]
    has_add = add is not None
    has_after = after is not None
    has_into = into is not None

    def finish(r, add_ref, o_ref):
        if has_add:
            r = r + add_scale * add_ref[...].astype(F32)
        o_ref[...] = r.astype(o_ref.dtype)

    def body_one(*refs):
        a_ref, b_ref = refs[:2]
        finish(dot(a_ref[...].astype(BF16), b_ref[...].astype(BF16)), refs[2] if has_add else None, refs[-1])

    def body_acc(*refs):
        a_ref, b_ref = refs[:2]
        add_ref = refs[2] if has_add else None
        o_ref, acc_ref = refs[-2:]
        k = pl.program_id(2)

        @pl.when(k == 0)
        def _():
            acc_ref[...] = jnp.zeros_like(acc_ref)

        acc_ref[...] += dot(a_ref[...].astype(BF16), b_ref[...].astype(BF16))

        @pl.when(k == nk - 1)
        def _():
            finish(acc_ref[...], add_ref, o_ref)

    in_specs = ([a_spec, b_spec] + ([add_spec] if has_add else []) + ([_ANY] if has_after else [])
                + ([_ANY] if has_into else []))
    args = (a, b) + ((add,) if has_add else ()) + ((after,) if has_after else ()) + ((into,) if has_into else ())
    return pl.pallas_call(
        body_one if nk == 1 else body_acc, name=name, grid=grid, in_specs=in_specs, out_specs=out_spec,
        out_shape=out_shape,
        input_output_aliases={len(args) - 1: 0} if has_into else {},
        scratch_shapes=[] if nk == 1 else [pltpu.VMEM((tm, tn), F32)],
        compiler_params=_cparams(("parallel", "parallel", "arbitrary")),
    )(*args)


def _pick(n, pref):
    for t in pref:
        if n % t == 0:
            return t
    return n


_TILE_PREF = (1024, 512, 256, 128)
_K_TILE_PREF = (2048,) + _TILE_PREF


def _k_tile(k, out_dtype, has_add):
    return _pick(k, _K_TILE_PREF if (has_add or out_dtype != BF16) else (4096,) + _K_TILE_PREF)


def _mm_nn_sharded(a, w4, out_dtype, name, after=None):
    m, k = a.shape
    _, _, nn = w4.shape
    tm, tk, tn = _pick(m, _TILE_PREF), _k_tile(k, out_dtype, False), _pick(nn, _TILE_PREF)
    npb = nn // tn
    return _matmul(
        a, w4, mode="nn", grid=(m // tm, N_CHIPS * npb, k // tk), tile=(tm, tn), name=name,
        a_spec=pl.BlockSpec((tm, tk), lambda i, j, kk: (i, kk)),
        b_spec=pl.BlockSpec((None, tk, tn), lambda i, j, kk: (j // npb, kk, j % npb)),
        out_shape=jax.ShapeDtypeStruct((m, N_CHIPS * nn), out_dtype),
        out_spec=pl.BlockSpec((tm, tn), lambda i, j, kk: (i, j)), after=after)


def _mm_nn(a, b, out_dtype, name, b_row_off=0, add=None, add_scale=1.0):
    m, k = a.shape
    _, n = b.shape
    tm, tk, tn = _pick(m, _TILE_PREF), _k_tile(k, out_dtype, add is not None), _pick(n, _TILE_PREF)
    assert b_row_off % tk == 0
    ko = b_row_off // tk
    return _matmul(
        a, b, mode="nn", grid=(m // tm, n // tn, k // tk), tile=(tm, tn), name=name,
        a_spec=pl.BlockSpec((tm, tk), lambda i, j, kk: (i, kk)),
        b_spec=pl.BlockSpec((tk, tn), lambda i, j, kk: (kk + ko, j)),
        out_shape=jax.ShapeDtypeStruct((m, n), out_dtype),
        out_spec=pl.BlockSpec((tm, tn), lambda i, j, kk: (i, j)),
        add=add, add_spec=pl.BlockSpec((tm, tn), lambda i, j, kk: (i, j)), add_scale=add_scale)


def _mm_nt(a, b, out_dtype, name, add=None, add_scale=1.0, n=None, b_row_off=0):
    m, k = a.shape
    n = b.shape[0] if n is None else n
    tm, tk, tn = _pick(m, _TILE_PREF), _k_tile(k, out_dtype, add is not None), _pick(n, _TILE_PREF)
    assert b_row_off % tn == 0
    no = b_row_off // tn
    return _matmul(
        a, b, mode="nt", grid=(m // tm, n // tn, k // tk), tile=(tm, tn), name=name,
        a_spec=pl.BlockSpec((tm, tk), lambda i, j, kk: (i, kk)),
        b_spec=pl.BlockSpec((tn, tk), lambda i, j, kk: (j + no, kk)),
        out_shape=jax.ShapeDtypeStruct((m, n), out_dtype),
        out_spec=pl.BlockSpec((tm, tn), lambda i, j, kk: (i, j)),
        add=add, add_spec=pl.BlockSpec((tm, tn), lambda i, j, kk: (i, j)), add_scale=add_scale)


def _mm_nt_sharded_k(a, w4, out_dtype, name, add=None, add_scale=1.0, after=None):
    m, _ = a.shape
    _, n, kn = w4.shape
    tm, tk, tn = _pick(m, _TILE_PREF), _pick(kn, (2560,) + _TILE_PREF), _pick(n, _TILE_PREF)
    kpb = kn // tk
    return _matmul(
        a, w4, mode="nt", grid=(m // tm, n // tn, N_CHIPS * kpb), tile=(tm, tn), name=name,
        a_spec=pl.BlockSpec((tm, tk), lambda i, j, kk: (i, kk)),
        b_spec=pl.BlockSpec((None, tn, tk), lambda i, j, kk: (kk // kpb, j, kk % kpb)),
        out_shape=jax.ShapeDtypeStruct((m, n), out_dtype),
        out_spec=pl.BlockSpec((tm, tn), lambda i, j, kk: (i, j)),
        add=add, add_spec=pl.BlockSpec((tm, tn), lambda i, j, kk: (i, j)), add_scale=add_scale, after=after)


def _mm_tn(a, b, out_dtype, name, shard_cols=None, out_rows=None, out_row_off=0, into=None, after=None):
    k, m = a.shape
    _, n = b.shape
    nn = n if shard_cols is None else shard_cols
    tm, tk, tn = _pick(m, _TILE_PREF), _k_tile(k, out_dtype, False), _pick(nn, _TILE_PREF)
    if shard_cols is None:
        assert out_row_off % tm == 0
        ro = out_row_off // tm
        out_shape = jax.ShapeDtypeStruct((m if out_rows is None else out_rows, n), out_dtype)
        out_spec = pl.BlockSpec((tm, tn), lambda i, j, kk: (i + ro, j))
    else:
        npb = nn // tn
        out_shape = jax.ShapeDtypeStruct((n // nn, m, nn), out_dtype)
        out_spec = pl.BlockSpec((None, tm, tn), lambda i, j, kk: (j // npb, i, j % npb))
    return _matmul(
        a, b, mode="tn", grid=(m // tm, n // tn, k // tk), tile=(tm, tn), name=name,
        a_spec=pl.BlockSpec((tk, tm), lambda i, j, kk: (kk, i)),
        b_spec=pl.BlockSpec((tk, tn), lambda i, j, kk: (kk, j)),
        out_shape=out_shape, out_spec=out_spec, into=into, after=after)


def _bucket_tiles():
    qi = np.arange(ATTN_BLOCK)[:, None]
    ki = np.arange(2 * ATTN_BLOCK)[None, :]
    delta = np.clip(ATTN_BLOCK + qi - ki, 0, None)
    tiles = []
    max_exact = NUM_BUCKETS // 2
    for _, dil in ATTN_PATTERNS:
        dist = (delta * dil).astype(np.int32)
        d_f = np.maximum(dist, 1).astype(np.float32)
        large = max_exact + (np.log(d_f / np.float32(max_exact)) / np.float32(math.log(MAX_DISTANCE / max_exact))
                             * np.float32(NUM_BUCKETS - max_exact)).astype(np.int32)
        large = np.minimum(large, NUM_BUCKETS - 1)
        tiles.append(np.where(dist < max_exact, dist, large).astype(np.int32))
    return jnp.asarray(np.stack(tiles))


def _bias_expand(rel_bias, buckets, hpg):
    def body(tab_ref, bk_ref, o_ref):
        g, h = pl.program_id(0), pl.program_id(1)
        bk = bk_ref[...]
        acc = jnp.zeros((ATTN_BLOCK, 2 * ATTN_BLOCK), F32)
        for b in range(NUM_BUCKETS):
            acc = jnp.where(bk == b, tab_ref[b, g * hpg + h], acc)
        o_ref[...] = acc

    return pl.pallas_call(
        body, name="bias_expand", grid=(N_GROUPS_ATTN, hpg),
        in_specs=[pl.BlockSpec(memory_space=pltpu.SMEM),
                  pl.BlockSpec((None, ATTN_BLOCK, 2 * ATTN_BLOCK), lambda g, h: (g, 0, 0))],
        out_specs=pl.BlockSpec((None, None, ATTN_BLOCK, 2 * ATTN_BLOCK), lambda g, h: (g, h, 0, 0)),
        out_shape=jax.ShapeDtypeStruct((N_GROUPS_ATTN, hpg, ATTN_BLOCK, 2 * ATTN_BLOCK), F32),
        compiler_params=_cparams(("parallel", "parallel")),
    )(rel_bias, buckets)


def _bias_reduce(dtiles, buckets, hpg):
    def body(t_ref, bk_ref, o_ref):
        bk = bk_ref[...]
        t = t_ref[...]
        rows = lax.broadcasted_iota(jnp.int32, (NUM_BUCKETS, LANES), 0)
        acc = jnp.zeros((NUM_BUCKETS, LANES), F32)
        for b in range(NUM_BUCKETS):
            s = jnp.sum(jnp.sum(jnp.where(bk == b, t, 0.0), axis=1, keepdims=True), axis=0, keepdims=True)
            acc = jnp.where(rows == b, s, acc)
        o_ref[...] = acc

    return pl.pallas_call(
        body, name="bias_reduce", grid=(N_GROUPS_ATTN, hpg),
        in_specs=[pl.BlockSpec((None, None, ATTN_BLOCK, 2 * ATTN_BLOCK), lambda g, h: (g, h, 0, 0)),
                  pl.BlockSpec((None, ATTN_BLOCK, 2 * ATTN_BLOCK), lambda g, h: (g, 0, 0))],
        out_specs=pl.BlockSpec((None, NUM_BUCKETS, LANES), lambda g, h: (g * hpg + h, 0, 0)),
        out_shape=jax.ShapeDtypeStruct((N_GROUPS_ATTN * hpg, NUM_BUCKETS, LANES), F32),
        compiler_params=_cparams(("parallel", "parallel")),
    )(dtiles, buckets)


def _cast_x3(x, name, after=None):
    r, c = x.shape
    tr = _pick(r, (512, 256, 128, 8))
    extra = [] if after is None else [after]

    def body(x_ref, *rest):
        rest[-1][...] = x_ref[...].astype(BF16)

    return pl.pallas_call(
        body, name=name, grid=(r // tr,),
        in_specs=[pl.BlockSpec((tr, c), lambda i: (i, 0))] + [_ANY] * len(extra),
        out_specs=pl.BlockSpec((None, tr, c), lambda i: (0, i, 0)),
        out_shape=jax.ShapeDtypeStruct((N_GROUPS_ATTN, r, c), BF16),
        compiler_params=_cparams(("parallel",)),
    )(x, *extra)


def _class_copy(x3, slot, dil, name):
    _, s, d = x3.shape
    rows = s // dil
    tm = _pick(rows, (512, 256, 128))
    nbk = rows // tm

    def body(v_ref, x3_ref, o_ref):
        o_ref[...] = v_ref[...]

    return pl.pallas_call(
        body, name=name, grid=(dil, nbk),
        in_specs=[pl.BlockSpec((tm, d), lambda r, i: (i, r)), _ANY],
        out_specs=pl.BlockSpec((None, tm, d), lambda r, i: (slot, r * nbk + i, 0)),
        out_shape=jax.ShapeDtypeStruct(x3.shape, x3.dtype), input_output_aliases={1: 0},
        compiler_params=_cparams(("parallel", "parallel")),
    )(x3[0].reshape(rows, dil * d), x3)


def _in_proj_shard(x3, wa4, kvec, p, d_attn, name, into=None, after=None):
    _, s, d = x3.shape
    _, _, nn = wa4.shape
    tm = _pick(s, _TILE_PREF)
    tn = _pick(math.gcd(nn, 3 * d_attn), _TILE_PREF)
    npb, bpg = nn // tn, 3 * d_attn // tn
    extra = ([] if after is None else [after]) + ([] if into is None else [into])

    def block(k, j):
        return jnp.bitwise_xor(k[0], p) * npb + j

    def slot(k, j):
        jb = block(k, j)
        return jnp.where(jb < N_GROUPS_ATTN * bpg, jb // bpg, 0)

    def body(k_ref, a_ref, b_ref, *rest):
        rest[-1][...] = _dot(a_ref[...], b_ref[...]).astype(BF16)

    grid_spec = pltpu.PrefetchScalarGridSpec(
        num_scalar_prefetch=1, grid=(s // tm, npb),
        in_specs=[pl.BlockSpec((None, tm, d), lambda i, j, k: (slot(k, j), i, 0)),
                  pl.BlockSpec((None, d, tn), lambda i, j, k: (jnp.bitwise_xor(k[0], p), 0, j))]
        + [_ANY] * len(extra),
        out_specs=pl.BlockSpec((tm, tn), lambda i, j, k: (i, block(k, j))))
    return pl.pallas_call(
        body, name=name, grid_spec=grid_spec, out_shape=jax.ShapeDtypeStruct((s, N_CHIPS * nn), BF16),
        input_output_aliases={} if into is None else {2 + len(extra): 0},
        compiler_params=_cparams(("parallel", "parallel")),
    )(kvec, x3, wa4, *extra)


def _attn_valid(n_is_first):
    qi = lax.broadcasted_iota(jnp.int32, (ATTN_BLOCK, 2 * ATTN_BLOCK), 0)
    ki = lax.broadcasted_iota(jnp.int32, (ATTN_BLOCK, 2 * ATTN_BLOCK), 1)
    delta = ATTN_BLOCK + qi - ki
    band = (delta >= 0) & (delta <= ATTN_BLOCK)
    return band & (jnp.logical_not(n_is_first) | (ki >= ATTN_BLOCK))


def _attn_fwd(pg, bias, g, dil, hpg):
    s = pg.shape[0]
    w = hpg * HEAD_DIM
    rows = s // dil
    nb = rows // ATTN_BLOCK
    scale = HEAD_DIM ** -0.5

    def body(q_ref, kc_ref, kp_ref, vc_ref, vp_ref, bias_ref, o_ref, lse_ref):
        valid = _attn_valid(pl.program_id(1) == 0)
        lane = lax.broadcasted_iota(jnp.int32, (ATTN_BLOCK, LANES), 1)
        lse = jnp.zeros((ATTN_BLOCK, LANES), F32)
        for h in range(hpg):
            sl = slice(h * HEAD_DIM, (h + 1) * HEAD_DIM)
            k2 = jnp.concatenate([kp_ref[:, sl], kc_ref[:, sl]], axis=0)
            v2 = jnp.concatenate([vp_ref[:, sl], vc_ref[:, sl]], axis=0)
            sc = _dot_nt(q_ref[:, sl], k2) * scale + bias_ref[h]
            sc = jnp.where(valid, sc, NEG_INF)
            m = jnp.max(sc, axis=1, keepdims=True)
            p = jnp.exp(sc - m)
            l = jnp.sum(p, axis=1, keepdims=True)
            o_ref[:, sl] = (_dot(p.astype(BF16), v2) * (1.0 / l)).astype(BF16)
            lse = jnp.where(lane == h, m + jnp.log(l), lse)
        lse_ref[...] = lse

    def col(off):
        return lambda r, n: (r * nb + n, 3 * g + off)

    def colp(off):
        return lambda r, n: (r * nb + jnp.maximum(n - 1, 0), 3 * g + off)

    blk = (ATTN_BLOCK, w)
    tok = pl.BlockSpec(blk, lambda r, n: (n, r))
    tok1 = pl.BlockSpec((ATTN_BLOCK, LANES), lambda r, n: (n, r))
    o, lse = pl.pallas_call(
        body, name=f"attn_fwd_g{g}", grid=(dil, nb),
        in_specs=[pl.BlockSpec(blk, col(0)), pl.BlockSpec(blk, col(1)), pl.BlockSpec(blk, colp(1)),
                  pl.BlockSpec(blk, col(2)), pl.BlockSpec(blk, colp(2)),
                  pl.BlockSpec((None, hpg, ATTN_BLOCK, 2 * ATTN_BLOCK), lambda r, n: (g, 0, 0, 0))],
        out_specs=[tok, tok1],
        out_shape=[jax.ShapeDtypeStruct((rows, dil * w), BF16), jax.ShapeDtypeStruct((rows, dil * LANES), F32)],
        compiler_params=_cparams(("parallel", "parallel")),
    )(pg, pg, pg, pg, pg, bias)
    return o.reshape(s, w), lse.reshape(s, LANES)


def _attn_combine(os_, lses, pa, hpg):
    s, w = os_[0].shape
    gate_blk = pa.shape[1] // w - 1
    tm = _pick(s, (256, 128))

    def body(o0, o1, o2, l0, l1, l2, gate_ref, o_ref, lse_ref, y_ref):
        a0, a1, a2 = l0[...], l1[...], l2[...]
        m = jnp.maximum(jnp.maximum(a0, a1), a2)
        e0, e1, e2 = jnp.exp(a0 - m), jnp.exp(a1 - m), jnp.exp(a2 - m)
        den = e0 + e1 + e2
        inv = 1.0 / den
        w0, w1, w2 = e0 * inv, e1 * inv, e2 * inv
        lse_ref[...] = m + jnp.log(den)
        for h in range(hpg):
            sl = slice(h * HEAD_DIM, (h + 1) * HEAD_DIM)
            o = (w0[:, h:h + 1] * o0[:, sl].astype(F32) + w1[:, h:h + 1] * o1[:, sl].astype(F32)
                 + w2[:, h:h + 1] * o2[:, sl].astype(F32))
            gate = gate_ref[:, sl].astype(F32)
            o_ref[:, sl] = o.astype(BF16)
            y_ref[:, sl] = (o * (gate * _sigmoid(gate))).astype(BF16)

    spec = pl.BlockSpec((tm, w), lambda i: (i, 0))
    spec1 = pl.BlockSpec((tm, LANES), lambda i: (i, 0))
    return pl.pallas_call(
        body, name="attn_combine", grid=(s // tm,),
        in_specs=[spec] * 3 + [spec1] * 3 + [pl.BlockSpec((tm, w), lambda i: (i, gate_blk))],
        out_specs=[spec, spec1, spec],
        out_shape=[jax.ShapeDtypeStruct((s, w), BF16), jax.ShapeDtypeStruct((s, LANES), F32),
                   jax.ShapeDtypeStruct((s, w), BF16)],
        compiler_params=_cparams(("parallel",)),
    )(*os_, *lses, pa)


def _attn_pre_bwd(dy, o, pa, hpg):
    s, w = dy.shape
    gate_blk = pa.shape[1] // w - 1
    tm = _pick(s, (256, 128))

    def body(dy_ref, o_ref, gate_ref, do_ref, dl_ref, dg_ref):
        gate = gate_ref[...].astype(F32)
        sg = _sigmoid(gate)
        dyv = dy_ref[...].astype(F32)
        ov = o_ref[...].astype(F32)
        do = dyv * (gate * sg)
        do_ref[...] = do.astype(BF16)
        dg_ref[...] = (dyv * ov * (sg * (1.0 + gate * (1.0 - sg)))).astype(BF16)
        prod = do * ov
        lane = lax.broadcasted_iota(jnp.int32, (tm, LANES), 1)
        dl = jnp.zeros((tm, LANES), F32)
        for h in range(hpg):
            sl = slice(h * HEAD_DIM, (h + 1) * HEAD_DIM)
            dl = jnp.where(lane == h, jnp.sum(prod[:, sl], axis=1, keepdims=True), dl)
        dl_ref[...] = dl

    spec = pl.BlockSpec((tm, w), lambda i: (i, 0))
    return pl.pallas_call(
        body, name="attn_pre_bwd", grid=(s // tm,),
        in_specs=[spec, spec, pl.BlockSpec((tm, w), lambda i: (i, gate_blk))],
        out_specs=[spec, pl.BlockSpec((tm, LANES), lambda i: (i, 0)), spec],
        out_shape=[jax.ShapeDtypeStruct((s, w), BF16), jax.ShapeDtypeStruct((s, LANES), F32),
                   jax.ShapeDtypeStruct((s, w), BF16)],
        compiler_params=_cparams(("parallel",)),
    )(dy, o, pa)


def _attn_bwd(pg, bias, do, lse, delta, g, dil, hpg):
    s = pg.shape[0]
    w = hpg * HEAD_DIM
    rows = s // dil
    nb = rows // ATTN_BLOCK
    dov = do.reshape(rows, dil * w)
    lsev, dlv = (t.reshape(rows, dil * LANES) for t in (lse, delta))
    scale = HEAD_DIM ** -0.5

    def body(q_ref, kc_ref, kp_ref, vc_ref, vp_ref, bias_ref, do_ref, lse_ref, dl_ref,
             dq_ref, dk_ref, dv_ref, db_ref, dkc_ref, dvc_ref):
        r, i = pl.program_id(0), pl.program_id(1)
        n = nb - 1 - i
        valid = _attn_valid(n == 0)

        @pl.when((r == 0) & (i == 0))
        def _():
            db_ref[...] = jnp.zeros_like(db_ref)

        @pl.when(i == 0)
        def _():
            dkc_ref[...] = jnp.zeros_like(dkc_ref)
            dvc_ref[...] = jnp.zeros_like(dvc_ref)

        for h in range(hpg):
            sl = slice(h * HEAD_DIM, (h + 1) * HEAD_DIM)
            q = q_ref[:, sl]
            dov_ = do_ref[:, sl]
            k2 = jnp.concatenate([kp_ref[:, sl], kc_ref[:, sl]], axis=0)
            v2 = jnp.concatenate([vp_ref[:, sl], vc_ref[:, sl]], axis=0)
            sc = _dot_nt(q, k2) * scale + bias_ref[h]
            p = jnp.exp(jnp.where(valid, sc - lse_ref[:, h:h + 1], NEG_INF))
            dp = _dot_nt(dov_, v2)
            ds = p * (dp - dl_ref[:, h:h + 1])
            db_ref[h] += ds
            dsb = ds.astype(BF16)
            dq_ref[:, sl] = (_dot(dsb, k2) * scale).astype(BF16)
            dk2 = _dot_tn(dsb, q) * scale
            dv2 = _dot_tn(p.astype(BF16), dov_)
            dk_ref[:, sl] = (dk2[ATTN_BLOCK:] + dkc_ref[:, sl]).astype(BF16)
            dv_ref[:, sl] = (dv2[ATTN_BLOCK:] + dvc_ref[:, sl]).astype(BF16)
            dkc_ref[:, sl] = dk2[:ATTN_BLOCK]
            dvc_ref[:, sl] = dv2[:ATTN_BLOCK]

    def col(off):
        return lambda r, i: (r * nb + nb - 1 - i, 3 * g + off)

    def colp(off):
        return lambda r, i: (r * nb + jnp.maximum(nb - 2 - i, 0), 3 * g + off)

    blk = (ATTN_BLOCK, w)
    tok = pl.BlockSpec(blk, lambda r, i: (nb - 1 - i, r))
    tok1 = pl.BlockSpec((ATTN_BLOCK, LANES), lambda r, i: (nb - 1 - i, r))
    dq, dk, dv, db = pl.pallas_call(
        body, name=f"attn_bwd_g{g}", grid=(dil, nb),
        in_specs=[pl.BlockSpec(blk, col(0)), pl.BlockSpec(blk, col(1)), pl.BlockSpec(blk, colp(1)),
                  pl.BlockSpec(blk, col(2)), pl.BlockSpec(blk, colp(2)),
                  pl.BlockSpec((None, hpg, ATTN_BLOCK, 2 * ATTN_BLOCK), lambda r, i: (g, 0, 0, 0)),
                  tok, tok1, tok1],
        out_specs=[tok, tok, tok,
                   pl.BlockSpec((hpg, ATTN_BLOCK, 2 * ATTN_BLOCK), lambda r, i: (0, 0, 0))],
        out_shape=[jax.ShapeDtypeStruct((rows, dil * w), BF16)] * 3
        + [jax.ShapeDtypeStruct((hpg, ATTN_BLOCK, 2 * ATTN_BLOCK), F32)],
        scratch_shapes=[pltpu.VMEM(blk, F32), pltpu.VMEM(blk, F32)],
        compiler_params=_cparams(("arbitrary", "arbitrary")),
    )(pg, pg, pg, pg, pg, bias, dov, lsev, dlv)
    return dq.reshape(s, w), dk.reshape(s, w), dv.reshape(s, w), db


def _ln_fwd(xin, h, gamma, beta, name, affine_in=None, target=None):
    s, d = xin.shape
    tm = _pick(s, (128,))
    has_aff = affine_in is not None
    has_tgt = target is not None

    def body(*refs):
        it = iter(refs)
        x_ref, h_ref, g_ref, b_ref = next(it), next(it), next(it), next(it)
        if has_aff:
            gi_ref, bi_ref = next(it), next(it)
        if has_tgt:
            t_ref = next(it)
        xh_ref, rs_ref = next(it), next(it)
        x = x_ref[...]
        if has_aff:
            x = x * gi_ref[...] + bi_ref[...]
        u = DEEPNORM_ALPHA * x + h_ref[...]
        mu = jnp.mean(u, axis=1, keepdims=True)
        uc = u - mu
        var = jnp.mean(uc * uc, axis=1, keepdims=True)
        rstd = lax.rsqrt(var + LN_EPS)
        xhat = uc * rstd
        xh_ref[...] = xhat
        rs_ref[...] = rstd
        y = xhat * g_ref[...] + b_ref[...]
        if has_tgt:
            dy_ref, l_ref = next(it), next(it)
            e = y - t_ref[...]
            dy_ref[...] = e * (1.0 / d)
            l_ref[...] = jnp.sum(e * e, axis=1, keepdims=True)
        else:
            y_ref = next(it)
            y_ref[...] = y.astype(BF16)

    row = pl.BlockSpec((tm, d), lambda i: (i, 0))
    vec = pl.BlockSpec((1, d), lambda i: (0, 0))
    one = pl.BlockSpec((tm, 1), lambda i: (i, 0))
    in_specs = [row, row, vec, vec] + ([vec, vec] if has_aff else []) + ([row] if has_tgt else [])
    args = [xin, h, gamma, beta] + (list(affine_in) if has_aff else []) + ([target] if has_tgt else [])
    out_specs = [row, one] + ([row, one] if has_tgt else [row])
    out_shape = [jax.ShapeDtypeStruct((s, d), F32), jax.ShapeDtypeStruct((s, 1), F32)]
    out_shape += ([jax.ShapeDtypeStruct((s, d), F32), jax.ShapeDtypeStruct((s, 1), F32)] if has_tgt
                  else [jax.ShapeDtypeStruct((s, d), BF16)])
    return pl.pallas_call(
        body, name=name, grid=(s // tm,), in_specs=in_specs, out_specs=out_specs, out_shape=out_shape,
        compiler_params=_cparams(("parallel",)),
    )(*args)


def _ln_bwd(dy, xhat, rstd, gamma, name):
    s, d = dy.shape
    tm = _pick(s, (128,))

    def body(dy_ref, xh_ref, rs_ref, g_ref, du_ref, dub_ref, dg_ref, db_ref):
        @pl.when(pl.program_id(0) == 0)
        def _():
            dg_ref[...] = jnp.zeros_like(dg_ref)
            db_ref[...] = jnp.zeros_like(db_ref)

        dyv = dy_ref[...]
        xh = xh_ref[...]
        dg_ref[...] += jnp.sum(dyv * xh, axis=0, keepdims=True)
        db_ref[...] += jnp.sum(dyv, axis=0, keepdims=True)
        dxh = dyv * g_ref[...]
        m1 = jnp.mean(dxh, axis=1, keepdims=True)
        m2 = jnp.mean(dxh * xh, axis=1, keepdims=True)
        du = rs_ref[...] * (dxh - m1 - xh * m2)
        du_ref[...] = du
        dub_ref[...] = du.astype(BF16)

    row = pl.BlockSpec((tm, d), lambda i: (i, 0))
    vec = pl.BlockSpec((1, d), lambda i: (0, 0))
    one = pl.BlockSpec((tm, 1), lambda i: (i, 0))
    return pl.pallas_call(
        body, name=name, grid=(s // tm,), in_specs=[row, row, one, vec],
        out_specs=[row, row, vec, vec],
        out_shape=[jax.ShapeDtypeStruct((s, d), F32), jax.ShapeDtypeStruct((s, d), BF16),
                   jax.ShapeDtypeStruct((1, d), F32), jax.ShapeDtypeStruct((1, d), F32)],
        compiler_params=_cparams(("arbitrary",)),
    )(dy, xhat, rstd, gamma)


_HALO = 16
_STRIP = 16


def _strips(tm, fn, init, reverse=False):
    n = tm // _STRIP

    def step(i, carry):
        s_ = n - 1 - i if reverse else i
        return fn(pl.ds(pl.multiple_of(s_ * _STRIP, _STRIP), _STRIP), carry)

    return lax.fori_loop(0, n, step, init)


def _fold8(t):
    return t[0:8] + t[8:16]


def _conv_taps(ext, tm, w_ref):
    acc = None
    for k in range(CONV_WIDTH):
        lo = _HALO - (CONV_WIDTH - 1) + k
        term = w_ref[k:k + 1, :] * ext[lo:lo + tm, :]
        acc = term if acc is None else acc + term
    return acc


def _conv_strip(prev, cur, w_ref):
    ext = jnp.concatenate([prev, cur], axis=0)
    acc, taps = None, []
    for k in range(CONV_WIDTH):
        lo = _STRIP - (CONV_WIDTH - 1) + k
        taps.append(ext[lo:lo + _STRIP, :])
        term = w_ref[k:k + 1, :] * taps[k]
        acc = term if acc is None else acc + term
    return acc, taps


def _conv_fwd(pzx, conv_w, conv_b, d_inner):
    s, _ = pzx.shape
    cd = conv_w.shape[1]
    tm = _pick(s, (512, 256, 128))
    tc = _pick(cd, (1024, 512, 256, 128))
    off = d_inner // tc
    hb = tm // _HALO

    def body(x_ref, p_ref, w_ref, b_ref, o_ref):
        prev = jnp.where(pl.program_id(0) > 0, p_ref[...].astype(F32), 0.0)
        ext = jnp.concatenate([prev, x_ref[...].astype(F32)], axis=0)
        pre = _conv_taps(ext, tm, w_ref) + b_ref[...]
        o_ref[...] = (pre * _sigmoid(pre)).astype(BF16)

    return pl.pallas_call(
        body, name="conv_fwd", grid=(s // tm, cd // tc),
        in_specs=[pl.BlockSpec((tm, tc), lambda i, j: (i, off + j)),
                  pl.BlockSpec((_HALO, tc), lambda i, j: (jnp.maximum(i * hb - 1, 0), off + j)),
                  pl.BlockSpec((CONV_WIDTH, tc), lambda i, j: (0, j)),
                  pl.BlockSpec((1, tc), lambda i, j: (0, j))],
        out_specs=pl.BlockSpec((tm, tc), lambda i, j: (i, j)),
        out_shape=jax.ShapeDtypeStruct((s, cd), BF16),
        compiler_params=_cparams(("parallel", "parallel")),
    )(pzx, pzx, conv_w, conv_b)


def _conv_bwd_a(pzx, dxbc, conv_w, conv_b, d_inner):
    s, _ = pzx.shape
    cd = conv_w.shape[1]
    tm = _pick(s, (512, 256, 128))
    tc = _pick(cd, (1024, 512, 256, 128))
    off = d_inner // tc
    hb = tm // _HALO

    def body(x_ref, p_ref, d_ref, w_ref, b_ref, o_ref, dw_ref, db_ref, acc_ref):
        @pl.when(pl.program_id(1) == 0)
        def _():
            dw_ref[...] = jnp.zeros_like(dw_ref)
            db_ref[...] = jnp.zeros_like(db_ref)

        acc_ref[...] = jnp.zeros_like(acc_ref)

        def strip(rows, prev):
            cur = x_ref[rows, :].astype(F32)
            pre, taps = _conv_strip(prev, cur, w_ref)
            pre = pre + b_ref[...]
            sg = _sigmoid(pre)
            dpre = d_ref[rows, :].astype(F32) * (sg * (1.0 + pre * (1.0 - sg)))
            o_ref[rows, :] = dpre
            for k in range(CONV_WIDTH):
                acc_ref[k] += _fold8(dpre * taps[k])
            acc_ref[CONV_WIDTH] += _fold8(dpre)
            return cur

        _strips(tm, strip, jnp.where(pl.program_id(1) > 0, p_ref[...].astype(F32), 0.0))
        for k in range(CONV_WIDTH):
            dw_ref[k:k + 1, :] += jnp.sum(acc_ref[k], axis=0, keepdims=True)
        db_ref[...] += jnp.sum(acc_ref[CONV_WIDTH], axis=0, keepdims=True)

    return pl.pallas_call(
        body, name="conv_bwd_a", grid=(cd // tc, s // tm),
        in_specs=[pl.BlockSpec((tm, tc), lambda j, i: (i, off + j)),
                  pl.BlockSpec((_HALO, tc), lambda j, i: (jnp.maximum(i * hb - 1, 0), off + j)),
                  pl.BlockSpec((tm, tc), lambda j, i: (i, j)),
                  pl.BlockSpec((CONV_WIDTH, tc), lambda j, i: (0, j)),
                  pl.BlockSpec((1, tc), lambda j, i: (0, j))],
        out_specs=[pl.BlockSpec((tm, tc), lambda j, i: (i, j)),
                   pl.BlockSpec((CONV_WIDTH, tc), lambda j, i: (0, j)),
                   pl.BlockSpec((1, tc), lambda j, i: (0, j))],
        out_shape=[jax.ShapeDtypeStruct((s, cd), F32), jax.ShapeDtypeStruct((CONV_WIDTH, cd), F32),
                   jax.ShapeDtypeStruct((1, cd), F32)],
        scratch_shapes=[pltpu.VMEM((CONV_WIDTH + 1, 8, tc), F32)],
        compiler_params=_cparams(("parallel", "arbitrary")),
    )(pzx, pzx, dxbc, conv_w, conv_b)


def _conv_bwd_b(dpre, conv_w, into, col_off):
    s, cd = dpre.shape
    tm = _pick(s, (512, 256, 128))
    tc = _pick(cd, (1024, 512, 256, 128))
    hb = tm // 8
    nrb = s // tm
    assert col_off % tc == 0
    co = col_off // tc

    def body(x_ref, nx_ref, w_ref, into_ref, o_ref):
        nxt = jnp.where(pl.program_id(0) < nrb - 1, nx_ref[...], 0.0)
        ext = jnp.concatenate([x_ref[...], nxt], axis=0)
        acc = None
        for k in range(CONV_WIDTH):
            lo = CONV_WIDTH - 1 - k
            term = w_ref[k:k + 1, :] * ext[lo:lo + tm, :]
            acc = term if acc is None else acc + term
        o_ref[...] = acc.astype(BF16)

    return pl.pallas_call(
        body, name="conv_bwd_b", grid=(nrb, cd // tc),
        in_specs=[pl.BlockSpec((tm, tc), lambda i, j: (i, j)),
                  pl.BlockSpec((8, tc), lambda i, j: (jnp.minimum((i + 1) * hb, s // 8 - 1), j)),
                  pl.BlockSpec((CONV_WIDTH, tc), lambda i, j: (0, j)), _ANY],
        out_specs=pl.BlockSpec((tm, tc), lambda i, j: (i, j + co)),
        out_shape=jax.ShapeDtypeStruct(into.shape, BF16),
        input_output_aliases={3: 0},
        compiler_params=_cparams(("parallel", "parallel")),
    )(dpre, dpre, conv_w, into)


def _expand_matrix():
    e = np.zeros((LANES, SSM_GROUP_WIDTH), np.float32)
    for h in range(HEADS_PER_SSM_GROUP):
        e[h, h * SSM_HEAD_DIM:(h + 1) * SSM_HEAD_DIM] = 1.0
    return jnp.asarray(e, BF16)


def _expand(t, e):
    return _dot(t.astype(BF16), e)


def _segsum(v, e):
    hi, lo = _split2(v)
    return _dot_nt(hi, e) + _dot_nt(lo, e)


def _tri_dot(tri, x):
    hi, mid, lo = _split3(x)
    return _dot(tri, hi) + _dot(tri, mid) + _dot(tri, lo)


def _ssd_common(dtp_ref, a_ref, dtb_ref, x_ref, e):
    li = lax.broadcasted_iota(jnp.int32, (CHUNK, CHUNK), 0)
    si = lax.broadcasted_iota(jnp.int32, (CHUNK, CHUNK), 1)
    causal = li >= si
    tril = causal.astype(BF16)
    raw = dtp_ref[...] + dtb_ref[...]
    dt = jnp.maximum(raw, 0.0) + jnp.log(1.0 + jnp.exp(-jnp.abs(raw)))
    head_lane = lax.broadcasted_iota(jnp.int32, (1, LANES), 1) < HEADS_PER_SSM_GROUP
    a = jnp.where(head_lane, -jnp.exp(a_ref[...]), 0.0)
    a_cum = _tri_dot(tril, dt * a)
    a_cum_t = a_cum.T
    e_a = jnp.exp(a_cum)
    to_end = jnp.exp(a_cum[CHUNK - 1:CHUNK, :] - a_cum)
    x = x_ref[...].astype(F32)
    dt_e = _expand(dt, e)
    return dict(causal=causal, raw=raw, dt=dt, a=a, a_cum=a_cum, a_cum_t=a_cum_t, e_a=e_a,
                to_end=to_end, x=x, dt_e=dt_e, xdt=x * dt_e, e_a_e=_expand(e_a, e),
                to_end_e=_expand(to_end, e))


def _decay(q, h):
    seg = q["a_cum"][:, h:h + 1] - q["a_cum_t"][h:h + 1, :]
    return jnp.exp(jnp.where(q["causal"], seg, -jnp.inf))


def _ssd_specs(ng, d_inner, rev, nc):
    cidx = (lambda i: nc - 1 - i) if rev else (lambda i: i)
    boff = d_inner // D_STATE
    return dict(
        xs=pl.BlockSpec((CHUNK, SSM_GROUP_WIDTH), lambda g, i: (cidx(i), g)),
        b=pl.BlockSpec((CHUNK, D_STATE), lambda g, i: (cidx(i), boff + g)),
        c=pl.BlockSpec((CHUNK, D_STATE), lambda g, i: (cidx(i), boff + ng + g)),
        dtp=pl.BlockSpec((None, CHUNK, LANES), lambda g, i: (g, cidx(i), 0)),
        vec=pl.BlockSpec((None, 1, LANES), lambda g, i: (g, 0, 0)),
        wide=pl.BlockSpec((None, 1, SSM_GROUP_WIDTH), lambda g, i: (g, 0, 0)),
        e=pl.BlockSpec((LANES, SSM_GROUP_WIDTH), lambda g, i: (0, 0)),
        st=pl.BlockSpec((None, None, D_STATE, SSM_GROUP_WIDTH), lambda g, i: (g, cidx(i), 0, 0)),
        tok=pl.BlockSpec((CHUNK, SSM_GROUP_WIDTH), lambda g, i: (cidx(i), g)),
        bc_out=pl.BlockSpec((CHUNK, D_STATE), lambda g, i: (cidx(i), g)),
    )


def _ssd_fwd(xbc, dtp, a_pad, dtb_pad, dsk_e, e, d_inner):
    s = xbc.shape[0]
    ng = d_inner // SSM_GROUP_WIDTH
    nc = s // CHUNK

    def body(x_ref, b_ref, c_ref, dtp_ref, a_ref, dtb_ref, dsk_ref, e_ref, y_ref, st_ref, state):
        lane = lax.broadcasted_iota(jnp.int32, (CHUNK, LANES), 1)
        @pl.when(pl.program_id(1) == 0)
        def _():
            state[...] = jnp.zeros_like(state)

        ev = e_ref[...]
        q = _ssd_common(dtp_ref, a_ref, dtb_ref, x_ref, ev)
        bm, cm = b_ref[...], c_ref[...]
        cb = _dot_nt(cm, bm)
        s0 = state[...]
        st_ref[...] = s0
        y = _dot(cm, s0.astype(BF16)) * q["e_a_e"] + dsk_ref[...] * q["x"]
        xdt = q["xdt"]
        left = lane[:, :] < SSM_HEAD_DIM
        for j in range(HEADS_PER_SSM_GROUP // 2):
            sl = slice(j * LANES, (j + 1) * LANES)
            x2 = xdt[:, sl]
            m0 = (cb * _decay(q, 2 * j)).astype(BF16)
            m1 = (cb * _decay(q, 2 * j + 1)).astype(BF16)
            mcat = jnp.concatenate([m0, m1], axis=1)
            xbd = jnp.concatenate([jnp.where(left, x2, 0.0), jnp.where(left, 0.0, x2)], axis=0).astype(BF16)
            y_ref[:, sl] = (y[:, sl] + _dot(mcat, xbd)).astype(BF16)
        state[...] = s0 * q["e_a_e"][CHUNK - 1:CHUNK, :] + _dot_tn(bm, (q["to_end_e"] * xdt).astype(BF16))

    sp = _ssd_specs(ng, d_inner, False, nc)
    return pl.pallas_call(
        body, name="ssd_fwd", grid=(ng, nc),
        in_specs=[sp["xs"], sp["b"], sp["c"], sp["dtp"], sp["vec"], sp["vec"], sp["wide"], sp["e"]],
        out_specs=[sp["tok"], sp["st"]],
        out_shape=[jax.ShapeDtypeStruct((s, d_inner), BF16),
                   jax.ShapeDtypeStruct((ng, nc, D_STATE, SSM_GROUP_WIDTH), F32)],
        scratch_shapes=[pltpu.VMEM((D_STATE, SSM_GROUP_WIDTH), F32)],
        compiler_params=_cparams(("parallel", "arbitrary")),
    )(xbc, xbc, xbc, dtp, a_pad, dtb_pad, dsk_e, e)


def _ssd_bwd(xbc, dtp, a_pad, dtb_pad, dsk_e, e, states, dy, d_inner):
    s = xbc.shape[0]
    ng = d_inner // SSM_GROUP_WIDTH
    nc = s // CHUNK

    def body(x_ref, b_ref, c_ref, dtp_ref, a_ref, dtb_ref, dsk_ref, e_ref, st_ref, dy_ref,
             dx_ref, db_ref, dc_ref, ddt_ref, da_ref, ddtb_ref, dd_ref, dstate):
        lane = lax.broadcasted_iota(jnp.int32, (CHUNK, LANES), 1)
        sub = lax.broadcasted_iota(jnp.int32, (CHUNK, LANES), 0)
        @pl.when(pl.program_id(1) == 0)
        def _():
            dstate[...] = jnp.zeros_like(dstate)
            da_ref[...] = jnp.zeros_like(da_ref)
            ddtb_ref[...] = jnp.zeros_like(ddtb_ref)
            dd_ref[...] = jnp.zeros_like(dd_ref)

        ev = e_ref[...]
        q = _ssd_common(dtp_ref, a_ref, dtb_ref, x_ref, ev)
        bm, cm = b_ref[...], c_ref[...]
        cb = _dot_nt(cm, bm)
        x, xdt, e_a_e, to_end_e = q["x"], q["xdt"], q["e_a_e"], q["to_end_e"]
        s0 = st_ref[...]
        s0b = s0.astype(BF16)
        ds1 = dstate[...]
        ds1b = ds1.astype(BF16)
        dy = dy_ref[...].astype(F32)
        e_last_e = e_a_e[CHUNK - 1:CHUNK, :]

        dye = dy * e_a_e
        dyeb = dye.astype(BF16)
        cs0 = _dot(cm, s0b)
        dc = _dot_nt(dyeb, s0b)
        dstate[...] = e_last_e * ds1 + _dot_tn(cm, dyeb)
        da_col = _segsum(dye * cs0, ev)

        gmat = _dot(bm, ds1b)
        dxdt = to_end_e * gmat
        dte = _segsum(xdt * gmat, ev) * q["to_end"]
        db = _dot_nt((to_end_e * xdt).astype(BF16), ds1b)
        da_col = da_col - dte
        last_row = (jnp.sum(dte, axis=0, keepdims=True)
                    + q["e_a"][CHUNK - 1:CHUNK, :] * jnp.sum(_segsum(s0 * ds1, ev), axis=0, keepdims=True))

        left = lane < SSM_HEAD_DIM
        dcb = jnp.zeros((CHUNK, CHUNK), F32)
        row_acc = jnp.zeros((CHUNK, LANES), F32)
        for j in range(HEADS_PER_SSM_GROUP // 2):
            sl = slice(j * LANES, (j + 1) * LANES)
            x2 = xdt[:, sl].astype(BF16)
            dy2 = dy[:, sl]
            dyl = jnp.where(left, dy2, 0.0).astype(BF16)
            dyr = jnp.where(left, 0.0, dy2).astype(BF16)
            ms = []
            for hh, dyh in ((0, dyl), (1, dyr)):
                h = 2 * j + hh
                dec = _decay(q, h)
                m = cb * dec
                dm = _dot_nt(dyh, x2)
                dcb = dcb + dm * dec
                dseg = dm * m
                da_col = da_col + jnp.where(lane == h, jnp.sum(dseg, axis=1, keepdims=True), 0.0)
                row_acc = row_acc + jnp.where(sub == h, jnp.sum(dseg, axis=0, keepdims=True), 0.0)
                ms.append(m.astype(BF16))
            mst = jnp.concatenate(ms, axis=0)
            dyst = jnp.concatenate([dyl, dyr], axis=0)
            d2 = dxdt[:, sl] + _dot_tn(mst, dyst)
            dx_ref[:, sl] = (d2 * q["dt_e"][:, sl] + dsk_ref[:, sl] * dy2).astype(BF16)
            dxdt_x = d2 * x[:, sl]
            if j == 0:
                parts = [dxdt_x]
            else:
                parts.append(dxdt_x)
        dcbb = dcb.astype(BF16)
        dc_ref[...] = (dc + _dot(dcbb, bm)).astype(BF16)
        db_ref[...] = (db + _dot_tn(dcbb, cm)).astype(BF16)

        d_a = da_col - row_acc.T + jnp.where(sub == CHUNK - 1, last_row, 0.0)
        triu = (lax.broadcasted_iota(jnp.int32, (CHUNK, CHUNK), 1)
                >= lax.broadcasted_iota(jnp.int32, (CHUNK, CHUNK), 0)).astype(BF16)
        d_dta = _tri_dot(triu, d_a)
        ddt = d_dta * q["a"] + _segsum(jnp.concatenate(parts, axis=1), ev)
        ddt_raw = ddt * _sigmoid(q["raw"])
        ddt_ref[...] = ddt_raw
        da_ref[...] += jnp.sum(d_dta * q["dt"], axis=0, keepdims=True) * q["a"]
        ddtb_ref[...] += jnp.sum(ddt_raw, axis=0, keepdims=True)
        dd_ref[...] += jnp.sum(dy * x, axis=0, keepdims=True)

    sp = _ssd_specs(ng, d_inner, True, nc)
    return pl.pallas_call(
        body, name="ssd_bwd", grid=(ng, nc),
        in_specs=[sp["xs"], sp["b"], sp["c"], sp["dtp"], sp["vec"], sp["vec"], sp["wide"], sp["e"],
                  sp["st"], sp["tok"]],
        out_specs=[sp["tok"], sp["bc_out"], sp["bc_out"], sp["dtp"], sp["vec"], sp["vec"], sp["wide"]],
        out_shape=[jax.ShapeDtypeStruct(xbc.shape, BF16),
                   jax.ShapeDtypeStruct((s, ng * D_STATE), BF16),
                   jax.ShapeDtypeStruct((s, ng * D_STATE), BF16),
                   jax.ShapeDtypeStruct((ng, s, LANES), F32),
                   jax.ShapeDtypeStruct((ng, 1, LANES), F32),
                   jax.ShapeDtypeStruct((ng, 1, LANES), F32),
                   jax.ShapeDtypeStruct((ng, 1, SSM_GROUP_WIDTH), F32)],
        scratch_shapes=[pltpu.VMEM((D_STATE, SSM_GROUP_WIDTH), F32)],
        compiler_params=_cparams(("parallel", "arbitrary")),
    )(xbc, xbc, xbc, dtp, a_pad, dtb_pad, dsk_e, e, states, dy)


def _gate_norm_fwd(y, pzx, norm_w):
    s, di = y.shape
    ng = di // SSM_GROUP_WIDTH
    tm = _pick(s, (512, 256, 128))

    def body(y_ref, z_ref, w_ref, o_ref):
        z = z_ref[...].astype(F32)
        y2 = y_ref[...].astype(F32) * (z * _sigmoid(z))
        r = lax.rsqrt(jnp.mean(y2 * y2, axis=1, keepdims=True) + RMS_EPS)
        o_ref[...] = (y2 * r * w_ref[...]).astype(BF16)

    blk = pl.BlockSpec((tm, SSM_GROUP_WIDTH), lambda i, g: (i, g))
    return pl.pallas_call(
        body, name="gate_norm_fwd", grid=(s // tm, ng),
        in_specs=[blk, blk, pl.BlockSpec((1, SSM_GROUP_WIDTH), lambda i, g: (0, g))],
        out_specs=blk, out_shape=jax.ShapeDtypeStruct((s, di), BF16),
        compiler_params=_cparams(("parallel", "parallel")),
    )(y, pzx, norm_w)


def _gate_norm_bwd(dy3, y, pzx, norm_w):
    s, di = y.shape
    ng = di // SSM_GROUP_WIDTH
    tm = _pick(s, (512, 256, 128))

    def body(d_ref, y_ref, z_ref, w_ref, dy_ref, dz_ref, dw_ref):
        @pl.when(pl.program_id(1) == 0)
        def _():
            dw_ref[...] = jnp.zeros_like(dw_ref)

        z = z_ref[...].astype(F32)
        yv = y_ref[...].astype(F32)
        sg = _sigmoid(z)
        sz = z * sg
        y2 = yv * sz
        r = lax.rsqrt(jnp.mean(y2 * y2, axis=1, keepdims=True) + RMS_EPS)
        nrm = y2 * r
        d3 = d_ref[...].astype(F32)
        dw_ref[...] += jnp.sum(d3 * nrm, axis=0, keepdims=True)
        dn = d3 * w_ref[...]
        dy2 = r * (dn - nrm * jnp.mean(dn * nrm, axis=1, keepdims=True))
        dy_ref[...] = (dy2 * sz).astype(BF16)
        dz_ref[...] = (dy2 * yv * (sg * (1.0 + z * (1.0 - sg)))).astype(BF16)

    blk = pl.BlockSpec((tm, SSM_GROUP_WIDTH), lambda g, i: (i, g))
    vec = pl.BlockSpec((1, SSM_GROUP_WIDTH), lambda g, i: (0, g))
    return pl.pallas_call(
        body, name="gate_norm_bwd", grid=(ng, s // tm),
        in_specs=[blk, blk, blk, vec], out_specs=[blk, blk, vec],
        out_shape=[jax.ShapeDtypeStruct((s, di), BF16), jax.ShapeDtypeStruct(pzx.shape, BF16),
                   jax.ShapeDtypeStruct((1, di), F32)],
        compiler_params=_cparams(("parallel", "arbitrary")),
    )(dy3, y, pzx, norm_w)


_ANY = pl.BlockSpec(memory_space=pl.ANY)


def _place():
    x, y, c = lax.axis_index("x"), lax.axis_index("y"), lax.axis_index("c")
    chips = [(1 - x, y), (x, 1 - y), (1 - x, 1 - y)]
    return x, y, c, chips


def _cast_to_slot(x, kvec, name, after=None):
    r, cn = x.shape
    tr = _rows_per_block(r, cn)
    extra = [] if after is None else [after]

    def body(k_ref, x_ref, *rest):
        rest[-1][...] = x_ref[...].astype(BF16)

    grid_spec = pltpu.PrefetchScalarGridSpec(
        num_scalar_prefetch=1, grid=(r // tr,),
        in_specs=[pl.BlockSpec((tr, cn), lambda i, k: (i, 0))] + [_ANY] * len(extra),
        out_specs=pl.BlockSpec((None, tr, cn), lambda i, k: (k[0], i, 0)))
    return pl.pallas_call(
        body, name=name, grid_spec=grid_spec, out_shape=jax.ShapeDtypeStruct((N_CHIPS, r, cn), BF16),
        compiler_params=_cparams(("parallel",)),
    )(kvec, x, *extra)


_HBM_SPEC = pl.BlockSpec(memory_space=pltpu.HBM)
_SEM_SPEC = pl.BlockSpec(memory_space=pltpu.SEMAPHORE)
_VMEM_SPEC = pl.BlockSpec(memory_space=pltpu.VMEM)
_EFFECT = pltpu.SideEffectType.DATAFLOW_SIDE_EFFECTING
_TOKEN = jax.ShapeDtypeStruct((8, LANES), F32)


def _hbm(a):
    return pltpu.with_memory_space_constraint(a, pltpu.HBM)


_NEIGHBOURS, _DIAGONAL, _ALL_CHIPS = (0, 1), (2,), (0, 1, 2)


def _gather_copies(bufs, refs, send_sems, recv_sems, forward, arrivals=True, which=_ALL_CHIPS):
    x, y, c, chips = _place()
    k = 2 * x + y
    out, arrive = [], []
    for w, ref in enumerate(refs):
        hr = bufs[w].shape[1] // 2
        for j, (cx, cy) in enumerate(chips):
            if j not in which:
                continue
            kj = 2 * cx + cy
            slot_out, slot_in, half_in = (kj, kj, 1 - c) if forward else (k, kj, c)
            to = (x, y, 1 - c) if forward else (cx, cy, c)
            src = ref.at[slot_out, pl.ds(c * hr, hr)]
            land = ref.at[slot_in, pl.ds(half_in * hr, hr)]
            out.append(pltpu.make_async_remote_copy(
                src_ref=src, dst_ref=src, send_sem=send_sems.at[3 * w + j], recv_sem=recv_sems.at[3 * w + j],
                device_id=to, device_id_type=MESH))
            if arrivals:
                arrive.append(pltpu.make_async_remote_copy(
                    src_ref=land, dst_ref=land, send_sem=send_sems.at[3 * w + j], recv_sem=recv_sems.at[3 * w + j],
                    device_id=to, device_id_type=MESH))
    return out, arrive


def _gather_start(bufs, forward, name, after=None, which=_ALL_CHIPS):
    n = len(bufs)
    extra = [] if after is None else [after]

    def body(*refs):
        ins = refs[:n]
        send_sems, recv_sems = refs[n + len(extra)], refs[n + len(extra) + 1]
        token = refs[-1]
        out, _ = _gather_copies(bufs, ins, send_sems, recv_sems, forward, arrivals=False, which=which)
        for cp in out:
            cp.start()
        token[...] = jnp.zeros_like(token)

    res = pl.pallas_call(
        body, name=name,
        out_shape=(pltpu.SemaphoreType.DMA((3 * n,)), pltpu.SemaphoreType.DMA((3 * n,)))
        + tuple(pltpu.HBM(b.shape, b.dtype) for b in bufs) + (_TOKEN,),
        in_specs=(_HBM_SPEC,) * n + (_ANY,) * len(extra),
        out_specs=(_SEM_SPEC, _SEM_SPEC) + (_HBM_SPEC,) * n + (_VMEM_SPEC,),
        input_output_aliases={w: 2 + w for w in range(n)},
        compiler_params=pltpu.CompilerParams(has_side_effects=_EFFECT),
    )(*[_hbm(b) for b in bufs], *extra)
    return res[0], res[1], list(res[2:2 + n]), res[-1]


def _gather_wait(bufs, send_sems, recv_sems, after, forward, name, which=_ALL_CHIPS):
    n = len(bufs)

    def body(*refs):
        ins = refs[:n]
        send_sems, recv_sems = refs[n], refs[n + 1]
        out, arrive = _gather_copies(bufs, ins, send_sems, recv_sems, forward, which=which)
        for cp in out:
            cp.wait_send()
        for cp in arrive:
            cp.wait_recv()

    res = pl.pallas_call(
        body, name=name,
        out_shape=tuple(pltpu.HBM(b.shape, b.dtype) for b in bufs),
        in_specs=(_HBM_SPEC,) * n + (_SEM_SPEC, _SEM_SPEC, _ANY), out_specs=(_HBM_SPEC,) * n,
        input_output_aliases={w: w for w in range(n)},
        compiler_params=pltpu.CompilerParams(has_side_effects=_EFFECT),
    )(*bufs, send_sems, recv_sems, after)
    return list(res)


def _swap_copy(g_ref, land_ref, send_sems, recv_sems):
    x, y, c, _ = _place()
    hr = g_ref.shape[1] // 2
    return pltpu.make_async_remote_copy(
        src_ref=g_ref.at[:, pl.ds((1 - c) * hr, hr)], dst_ref=land_ref, send_sem=send_sems.at[0],
        recv_sem=recv_sems.at[0], device_id=(x, y, 1 - c), device_id_type=MESH)


def _swap_start(g4, name):
    ns, r, cn = g4.shape

    def body(g_ref, land_ref, send_sems, recv_sems, g_thru, land_thru, token):
        _swap_copy(g_ref, land_ref, send_sems, recv_sems).start()
        token[...] = jnp.zeros_like(token)

    return pl.pallas_call(
        body, name=name,
        out_shape=(pltpu.SemaphoreType.DMA((1,)), pltpu.SemaphoreType.DMA((1,)),
                   pltpu.HBM(g4.shape, g4.dtype), pltpu.HBM((ns, r // 2, cn), g4.dtype), _TOKEN),
        in_specs=(_HBM_SPEC, _HBM_SPEC), out_specs=(_SEM_SPEC, _SEM_SPEC, _HBM_SPEC, _HBM_SPEC, _VMEM_SPEC),
        input_output_aliases={0: 2, 1: 3},
        compiler_params=pltpu.CompilerParams(has_side_effects=_EFFECT),
    )(_hbm(g4), _hbm(lax.empty((ns, r // 2, cn), g4.dtype)))


def _swap_wait(send_sems, recv_sems, g_thru, land_thru, after, name):
    def body(g_ref, land_ref, send_sems, recv_sems, after_ref, g_out, land_out):
        cp = _swap_copy(g_ref, land_ref, send_sems, recv_sems)
        cp.wait_send()
        cp.wait_recv()

    return pl.pallas_call(
        body, name=name,
        out_shape=(pltpu.HBM(g_thru.shape, g_thru.dtype), pltpu.HBM(land_thru.shape, land_thru.dtype)),
        in_specs=(_HBM_SPEC, _HBM_SPEC, _SEM_SPEC, _SEM_SPEC, _ANY), out_specs=(_HBM_SPEC, _HBM_SPEC),
        input_output_aliases={0: 0, 1: 1},
        compiler_params=pltpu.CompilerParams(has_side_effects=_EFFECT),
    )(g_thru, land_thru, send_sems, recv_sems, after)


def _join_copy(f_ref, send_sems, recv_sems, half):
    x, y, c, _ = _place()
    hr = f_ref.shape[0] // 2
    rows = f_ref.at[pl.ds(half * hr, hr)]
    return pltpu.make_async_remote_copy(
        src_ref=rows, dst_ref=rows, send_sem=send_sems.at[0], recv_sem=recv_sems.at[0],
        device_id=(x, y, 1 - c), device_id_type=MESH)


def _join_start(f, name):
    def body(f_ref, send_sems, recv_sems, f_thru, token):
        _join_copy(f_ref, send_sems, recv_sems, lax.axis_index("c")).start()
        token[...] = jnp.zeros_like(token)

    return pl.pallas_call(
        body, name=name,
        out_shape=(pltpu.SemaphoreType.DMA((1,)), pltpu.SemaphoreType.DMA((1,)), pltpu.HBM(f.shape, f.dtype), _TOKEN),
        in_specs=(_HBM_SPEC,), out_specs=(_SEM_SPEC, _SEM_SPEC, _HBM_SPEC, _VMEM_SPEC),
        input_output_aliases={0: 2},
        compiler_params=pltpu.CompilerParams(has_side_effects=_EFFECT),
    )(_hbm(f))


def _join_wait(send_sems, recv_sems, f_thru, after, name):
    def body(f_ref, send_sems, recv_sems, after_ref, f_out):
        c = lax.axis_index("c")
        _join_copy(f_ref, send_sems, recv_sems, c).wait_send()
        _join_copy(f_ref, send_sems, recv_sems, 1 - c).wait_recv()

    return pl.pallas_call(
        body, name=name, out_shape=pltpu.HBM(f_thru.shape, f_thru.dtype),
        in_specs=(_HBM_SPEC, _SEM_SPEC, _SEM_SPEC, _ANY), out_specs=_HBM_SPEC,
        input_output_aliases={0: 0},
        compiler_params=pltpu.CompilerParams(has_side_effects=_EFFECT),
    )(f_thru, send_sems, recv_sems, after)


def _place_bc(dxw, dbm, dcm, col_off):
    s, w = dbm.shape
    tm = _pick(s, (512, 256, 128))
    cb = col_off // w

    def body(b_ref, c_ref, x_ref, o_ref):
        o_ref[...] = jnp.where(pl.program_id(1) == 0, b_ref[...], c_ref[...])

    blk = pl.BlockSpec((tm, w), lambda i, j: (i, 0))
    return pl.pallas_call(
        body, name="place_db_dc", grid=(s // tm, 2), in_specs=[blk, blk, _ANY],
        out_specs=pl.BlockSpec((tm, w), lambda i, j: (i, cb + j)),
        out_shape=jax.ShapeDtypeStruct(dxw.shape, dxw.dtype), input_output_aliases={2: 0},
        compiler_params=_cparams(("parallel", "parallel")),
    )(dbm, dcm, dxw)


def _scatter_copies(t_ref, land_ref, send_sems, recv_sems, arrivals=True):
    x, y, c, chips = _place()
    k = 2 * x + y
    out, arrive = [], []
    for j, (cx, cy) in enumerate(chips):
        kj = 2 * cx + cy
        out.append(pltpu.make_async_remote_copy(
            src_ref=t_ref.at[kj], dst_ref=land_ref.at[k], send_sem=send_sems.at[j], recv_sem=recv_sems.at[j],
            device_id=(cx, cy, c), device_id_type=MESH))
        if arrivals:
            arrive.append(pltpu.make_async_remote_copy(
                src_ref=t_ref.at[kj], dst_ref=land_ref.at[kj], send_sem=send_sems.at[j], recv_sem=recv_sems.at[j],
                device_id=(cx, cy, c), device_id_type=MESH))
    return out, arrive


def _scatter_start(t, name):
    def body(t_ref, land_ref, send_sems, recv_sems, t_thru, land_thru, token):
        out, _ = _scatter_copies(t_ref, land_ref, send_sems, recv_sems, arrivals=False)
        for cp in out:
            cp.start()
        token[...] = jnp.zeros_like(token)

    return pl.pallas_call(
        body, name=name,
        out_shape=(pltpu.SemaphoreType.DMA((3,)), pltpu.SemaphoreType.DMA((3,)),
                   pltpu.HBM(t.shape, t.dtype), pltpu.HBM(t.shape, t.dtype), _TOKEN),
        in_specs=(_HBM_SPEC, _HBM_SPEC), out_specs=(_SEM_SPEC, _SEM_SPEC, _HBM_SPEC, _HBM_SPEC, _VMEM_SPEC),
        input_output_aliases={0: 2, 1: 3},
        compiler_params=pltpu.CompilerParams(has_side_effects=_EFFECT),
    )(_hbm(t), _hbm(lax.empty(t.shape, t.dtype)))


def _scatter_wait(send_sems, recv_sems, t_thru, land_thru, after, name):
    def body(t_ref, land_ref, send_sems, recv_sems, after_ref, t_out, land_out):
        out, arrive = _scatter_copies(t_ref, land_ref, send_sems, recv_sems)
        for cp in out:
            cp.wait_send()
        for cp in arrive:
            cp.wait_recv()

    return pl.pallas_call(
        body, name=name,
        out_shape=(pltpu.HBM(t_thru.shape, t_thru.dtype), pltpu.HBM(land_thru.shape, land_thru.dtype)),
        in_specs=(_HBM_SPEC, _HBM_SPEC, _SEM_SPEC, _SEM_SPEC, _ANY), out_specs=(_HBM_SPEC, _HBM_SPEC),
        input_output_aliases={0: 0, 1: 1},
        compiler_params=pltpu.CompilerParams(has_side_effects=_EFFECT),
    )(t_thru, land_thru, send_sems, recv_sems, after)


def _all_gather_small(v, reduce, name):
    r, l = v.shape

    def body(v_ref, o_ref, *rest):
        if reduce:
            buf, send_sems, recv_sems = rest
        else:
            buf = o_ref
            send_sems, recv_sems = rest
        x, y, c, _ = _place()
        me = 4 * x + 2 * y + c
        buf[me] = v_ref[...]
        cps = []
        for d in range(1, N_DEV):
            peer = (x if d & 4 == 0 else 1 - x, y if d & 2 == 0 else 1 - y, c if d & 1 == 0 else 1 - c)
            cp = pltpu.make_async_remote_copy(
                src_ref=v_ref, dst_ref=buf.at[me], send_sem=send_sems.at[d - 1], recv_sem=recv_sems.at[d - 1],
                device_id=peer, device_id_type=MESH)
            cp.start()
            cps.append((cp, peer))
        for d, (cp, (px, py, pc)) in enumerate(cps, start=1):
            pltpu.make_async_remote_copy(
                src_ref=v_ref, dst_ref=buf.at[4 * px + 2 * py + pc], send_sem=send_sems.at[d - 1],
                recv_sem=recv_sems.at[d - 1], device_id=(px, py, pc), device_id_type=MESH).wait_recv()
        for cp, _ in cps:
            cp.wait_send()
        if reduce:
            acc = buf[0]
            for i in range(1, N_DEV):
                acc = acc + buf[i]
            o_ref[...] = acc

    vm = pl.BlockSpec(memory_space=pltpu.VMEM)
    out_shape = jax.ShapeDtypeStruct((r, l) if reduce else (N_DEV, r, l), F32)
    scratch = ([pltpu.VMEM((N_DEV, r, l), F32)] if reduce else []) + [
        pltpu.SemaphoreType.DMA((N_DEV - 1,)), pltpu.SemaphoreType.DMA((N_DEV - 1,))]
    return pl.pallas_call(
        body, name=name, in_specs=[vm], out_specs=vm, out_shape=out_shape, scratch_shapes=scratch,
    )(v)


_BLOCK_BYTES = 3 * 512 * 1024


def _rows_per_block(r, cn, itemsize=4):
    best = 8
    for t in range(8, r + 1, 8):
        if r % t == 0 and t * cn * itemsize <= _BLOCK_BYTES:
            best = t
    return best


def _add_sibling_half(g4, recv, cvec, name):
    ns, r, cn = g4.shape
    hr = r // 2
    tr = _rows_per_block(hr, cn, itemsize=2)
    nrb = hr // tr

    def body(c_ref, a_ref, b_ref, o_ref):
        o_ref[...] = (a_ref[...].astype(F32) + b_ref[...].astype(F32)).astype(o_ref.dtype)

    grid_spec = pltpu.PrefetchScalarGridSpec(
        num_scalar_prefetch=1, grid=(ns, nrb),
        in_specs=[pl.BlockSpec((None, tr, cn), lambda j, i, c: (j, c[0] * nrb + i, 0)),
                  pl.BlockSpec((None, tr, cn), lambda j, i, c: (j, i, 0))],
        out_specs=pl.BlockSpec((None, tr, cn), lambda j, i, c: (j, i, 0)))
    return pl.pallas_call(
        body, name=name, grid_spec=grid_spec, out_shape=jax.ShapeDtypeStruct((ns, hr, cn), BF16),
        compiler_params=_cparams(("parallel", "parallel")),
    )(cvec, g4, recv)


def _sum_chips(r4, t4, kvec, cvec, name):
    ns, hr, cn = r4.shape
    tr = _rows_per_block(hr, cn, itemsize=2)
    nrb = hr // tr

    def body(k_ref, c_ref, r_ref, t_ref, o_ref):
        acc = t_ref[...].astype(F32)
        for dlt in range(1, ns):
            acc = acc + r_ref[(k_ref[0] + dlt) % ns].astype(F32)
        o_ref[...] = acc

    grid_spec = pltpu.PrefetchScalarGridSpec(
        num_scalar_prefetch=2, grid=(nrb,),
        in_specs=[pl.BlockSpec((ns, tr, cn), lambda i, k, c: (0, i, 0)),
                  pl.BlockSpec((None, tr, cn), lambda i, k, c: (k[0], i, 0))],
        out_specs=pl.BlockSpec((tr, cn), lambda i, k, c: (c[0] * nrb + i, 0)))
    return pl.pallas_call(
        body, name=name, grid_spec=grid_spec, out_shape=jax.ShapeDtypeStruct((2 * hr, cn), F32),
        compiler_params=_cparams(("parallel",)),
    )(kvec, cvec, r4, t4)


def _adamw(w, g, m, v, name):
    r, cn = w.shape
    tr = _rows_per_block(r, cn)
    c1 = 1.0 - ADAM_B1 ** ADAM_STEP
    c2 = 1.0 - ADAM_B2 ** ADAM_STEP

    def body(w_ref, g_ref, m_ref, v_ref, go_ref, d_ref, mo_ref, vo_ref):
        gv = g_ref[...]
        mn = ADAM_B1 * m_ref[...] + (1.0 - ADAM_B1) * gv
        vn = ADAM_B2 * v_ref[...] + (1.0 - ADAM_B2) * (gv * gv)
        go_ref[...] = gv
        mo_ref[...] = mn
        vo_ref[...] = vn
        d_ref[...] = -ADAM_LR * ((mn / c1) / (jnp.sqrt(vn / c2) + ADAM_EPS) + ADAM_WD * w_ref[...])

    spec = pl.BlockSpec((tr, cn), lambda i: (i, 0))
    return pl.pallas_call(
        body, name=name, grid=(r // tr,), in_specs=[spec] * 4, out_specs=[spec] * 4,
        out_shape=[jax.ShapeDtypeStruct((r, cn), F32)] * 4,
        compiler_params=_cparams(("parallel",)),
    )(w, g, m, v)


def _pack(arrs):
    flat = jnp.concatenate([a.reshape(-1).astype(F32) for a in arrs])
    n = flat.shape[0]
    tot = -(-n // (8 * LANES)) * (8 * LANES)
    return jnp.pad(flat, (0, tot - n)).reshape(tot // LANES, LANES)


def _unpack(packed, shapes):
    flat = packed.reshape(-1)
    out, off = [], 0
    for shp in shapes:
        sz = int(np.prod(shp))
        out.append(flat[off:off + sz].reshape(shp))
        off += sz
    return out


class _LocalExchange:
    def __init__(self, wa4, woa4, ws4, wos4):
        self.l0, self.ssm = [wa4, woa4], [ws4, wos4]
        self.grads = {}

    def l0_begin(self):
        return self.l0[0], None

    def l0_neighbours(self, after):
        return self.l0[0]

    def l0_diagonal(self, after):
        return self.l0

    def ssm_gather_mid(self, after):
        return None

    def ssm_gather_end(self, after):
        return self.ssm

    def grad_begin(self, name, g4):
        self.grads[name] = g4
        return None

    def grad_mid(self, name, after):
        return None

    def grad_sync(self, name, after):
        return None

    def small_grads(self, small_full):
        self.small = small_full
        return None


class _Exchange:
    def __init__(self, kvec, cvec, l0_bufs, after):
        self.kvec, self.cvec, self.l0_bufs, self.bufs, self.after = kvec, cvec, l0_bufs, None, after
        self.pending, self.summed, self.last_token, self.l0_token, self.swaps, self.joins = {}, {}, None, None, {}, {}

    def l0_begin(self):
        if self.l0_token is None:
            self.l0_send, self.l0_recv, self.l0_bufs, self.l0_token = _gather_start(
                self.l0_bufs, False, "l0_gather_ici_start", self.after)
        return self.l0_bufs[0], (self.l0_token if self.bufs is None else self.sems[3])

    def _l0_step(self, which, tag, after):
        bufs = _gather_wait(self.l0_bufs, self.l0_send, self.l0_recv, after, False, "l0_gather_ici_wait_" + tag, which)
        send, recv, bufs, token = _gather_start(bufs, True, "l0_gather_fwd_start_" + tag, None, which)
        self.l0_bufs = _gather_wait(bufs, send, recv, token, True, "l0_gather_fwd_wait_" + tag, which)
        return self.l0_bufs

    def l0_neighbours(self, after):
        return self._l0_step(_NEIGHBOURS, "nb", after)[0]

    def l0_diagonal(self, after):
        return self._l0_step(_DIAGONAL, "diag", after)

    def ssm_gather_start(self, ssm_bufs):
        self.sems = _gather_start(ssm_bufs, False, "ssm_gather_ici_start", self.l0_token)
        self.bufs = self.sems[2]
        return self.sems[3]

    def ssm_gather_mid(self, after):
        bufs = _gather_wait(self.bufs, self.sems[0], self.sems[1], after, False, "ssm_gather_ici_wait")
        self.sems = _gather_start(bufs, True, "ssm_gather_fwd_start")
        self.bufs = self.sems[2]
        return self.sems[3]

    def ssm_gather_end(self, after):
        return _gather_wait(self.bufs, self.sems[0], self.sems[1], after, True, "ssm_gather_fwd_wait")

    def small_grads(self, small_full):
        packed = _all_gather_small(_pack(small_full), True, "reduce_small_grads")
        self.small = _unpack(packed, [t.shape for t in small_full])
        return packed

    def grad_begin(self, name, g4):
        self.swaps[name] = _swap_start(g4, "swap_start_" + name)
        return self.swaps[name][4]

    def grad_mid(self, name, after):
        send_sem, recv_sem, g_thru, land, _ = self.swaps.pop(name)
        g4, recv = _swap_wait(send_sem, recv_sem, g_thru, land, after, "swap_wait_" + name)
        return self._reduce(name, g4, recv)

    def _reduce(self, name, g4, recv):
        t = _add_sibling_half(g4, recv, self.cvec, "add_sibling_" + name)
        send_sems, recv_sems, t_thru, land, token = _scatter_start(t, "scatter_start_" + name)
        self.pending[name] = (send_sems, recv_sems, t_thru, land)
        self.last_token = token
        return token

    def grad_sync(self, name, after):
        t, land = _scatter_wait(*self.pending.pop(name), after, "scatter_wait_" + name)
        summed = _sum_chips(land, t, self.kvec, self.cvec, "sum_chips_" + name)
        self.joins[name] = _join_start(summed, "join_start_" + name)
        return self.joins[name][3]

    def grad_end(self, name, after):
        send_sem, recv_sem, f_thru, _ = self.joins.pop(name)
        return _join_wait(send_sem, recv_sem, f_thru, after, "join_wait_" + name)


def _tie(vec, token):
    return vec if token is None else vec + token[0:1, 0:1].reshape((1,) * vec.ndim).astype(vec.dtype)


def _local_step(x2, tgt, ex, kvec, conv_w_f, conv_b_f, norm_w_f, rel_bias, dt_bias, a_log, d_skip,
                ln_g, ln_b):
    s, d = x2.shape
    wa4, tok = ex.l0_begin()
    d_attn = wa4.shape[2] * N_CHIPS // 10
    hpg = d_attn // HEAD_DIM
    d_inner = norm_w_f.shape[1]
    ng = d_inner // SSM_GROUP_WIDTH
    n_heads = dt_bias.shape[1]
    conv_dim = conv_w_f.shape[1]
    assert n_heads == ng * HEADS_PER_SSM_GROUP and conv_dim == d_inner + 2 * ng * D_STATE

    x3 = _cast_x3(x2, "cast_x", tok)
    for g, (_, dil) in enumerate(ATTN_PATTERNS):
        if dil > 1:
            x3 = _class_copy(x3, g, dil, f"class_order_x_g{g}")
    xb = x3[0]
    buckets = _bucket_tiles()
    bias = _bias_expand(rel_bias, buckets, hpg)
    pa = _in_proj_shard(x3, wa4, kvec, 0, d_attn, "mm_in_attn_own", after=tok)
    wa4 = ex.l0_neighbours(pa)
    pa = _in_proj_shard(x3, wa4, kvec, 1, d_attn, "mm_in_attn_nb1", into=pa)
    pa = _in_proj_shard(x3, wa4, kvec, 2, d_attn, "mm_in_attn_nb2", into=pa)
    wa4, woa4 = ex.l0_diagonal(pa)
    pa = _in_proj_shard(x3, wa4, kvec, 3, d_attn, "mm_in_attn_diag", into=pa)
    og, lg = [], []
    for g, (_, dil) in enumerate(ATTN_PATTERNS):
        o_, l_ = _attn_fwd(pa, bias, g, dil, hpg)
        og.append(o_)
        lg.append(l_)
    o, lse, yat = _attn_combine(og, lg, pa, hpg)
    h0 = _mm_nn_sharded(yat, woa4, F32, "mm_out_attn", after=ex.ssm_gather_mid(yat))
    g0, b0, g1, b1 = ln_g[0:1], ln_b[0:1], ln_g[1:2], ln_b[1:2]
    xhat0, rstd0, x1b = _ln_fwd(x2, h0, g0, b0, "ln0_fwd")

    wst4, wos4 = ex.ssm_gather_end(x1b)
    wst = wst4.reshape(N_CHIPS * wst4.shape[1], d)
    nzx = d_inner + conv_dim
    wos = wos4.reshape(d_inner, d)
    pzx = _mm_nt(x1b, wst, BF16, "mm_in_ssm", n=nzx)
    dt_raw = _mm_nt(x1b, wst, F32, "mm_in_dt", n=n_heads, b_row_off=nzx)

    def pad_heads(t):
        t = t.reshape(t.shape[0], ng, HEADS_PER_SSM_GROUP).transpose(1, 0, 2)
        return jnp.pad(t, ((0, 0), (0, 0), (0, LANES - HEADS_PER_SSM_GROUP)))

    def unpad_heads(t):
        return t[:, :, :HEADS_PER_SSM_GROUP].transpose(1, 0, 2).reshape(t.shape[1], n_heads)

    dtp = pad_heads(dt_raw)
    alog_p, dtb_p = pad_heads(a_log), pad_heads(dt_bias)
    dsk_e = jnp.repeat(d_skip.reshape(ng, 1, HEADS_PER_SSM_GROUP), SSM_HEAD_DIM, axis=2)
    e = _expand_matrix()
    xbc = _conv_fwd(pzx, conv_w_f, conv_b_f, d_inner)
    y_ssd, states = _ssd_fwd(xbc, dtp, alog_p, dtb_p, dsk_e, e, d_inner)
    y3 = _gate_norm_fwd(y_ssd, pzx, norm_w_f)
    h1 = _mm_nn(y3, wos, F32, "mm_out_ssm")
    xhat1, rstd1, dy2, row_sq = _ln_fwd(xhat0, h1, g1, b1, "ln1_fwd_loss", affine_in=(g0, b0), target=tgt)
    loss_local = 0.5 * jnp.sum(row_sq) / d

    du1, du1b, dg1, db1 = _ln_bwd(dy2, xhat1, rstd1, g1, "ln1_bwd")
    dy3 = _mm_nt(du1b, wos, BF16, "mm_d_y3")
    g_wos = _mm_tn(y3, du1b, BF16, "mm_g_w_out_ssm").reshape(N_CHIPS, d_inner // N_CHIPS, d)
    norm_w_t = _tie(norm_w_f, ex.grad_begin("w_out_ssm", g_wos))
    dy_ssd, dz, d_nw = _gate_norm_bwd(dy3, y_ssd, pzx, norm_w_t)
    dsk_t = _tie(dsk_e, ex.grad_mid("w_out_ssm", dy_ssd))
    dxs, dbm, dcm, ddtp, d_alog, d_dtb, d_dsk = _ssd_bwd(xbc, dtp, alog_p, dtb_p, dsk_t, e, states, dy_ssd, d_inner)
    dpre, d_cw, d_cb = _conv_bwd_a(pzx, _place_bc(dxs, dbm, dcm, d_inner), conv_w_f, conv_b_f, d_inner)
    dpzx = _conv_bwd_b(dpre, conv_w_f, dz, d_inner)
    ddt_raw = unpad_heads(ddtp)
    t1 = _mm_nn(ddt_raw, wst, F32, "mm_d_x1_dt", b_row_off=nzx, add=du1, add_scale=DEEPNORM_ALPHA)
    dx1 = _mm_nn(dpzx, wst, F32, "mm_d_x1", add=t1)
    g0_t = _tie(g0, ex.grad_sync("w_out_ssm", dx1))
    g_wst = _mm_tn(dpzx, x1b, BF16, "mm_g_w_in_ssm", out_rows=wst.shape[0])
    g_wst = _mm_tn(ddt_raw, x1b, BF16, "mm_g_w_dt", out_rows=wst.shape[0], out_row_off=nzx, into=g_wst)
    g0_t = _tie(g0_t, ex.grad_begin("w_in_ssm", g_wst.reshape(wst4.shape)))

    du0, du0b, dg0, db0 = _ln_bwd(dx1, xhat0, rstd0, g0_t, "ln0_bwd")
    dyat = _mm_nt_sharded_k(du0b, woa4, BF16, "mm_d_yat")
    g_woa = _mm_tn(yat, du0b, BF16, "mm_g_w_out_attn", shard_cols=d // N_CHIPS)
    lse_t = _tie(lse, ex.grad_mid("w_in_ssm", g_woa))
    ex.grad_begin("w_out_attn", g_woa)
    do, delta, dgate = _attn_pre_bwd(dyat, o, pa, hpg)
    pieces, dbt = [], []
    for g, (_, dil) in enumerate(ATTN_PATTERNS):
        dq, dk, dv, db_ = _attn_bwd(pa, bias, do, lse_t, delta, g, dil, hpg)
        pieces += [dq, dk, dv]
        dbt.append(db_)
    dpa = jnp.concatenate(pieces + [dgate], axis=1)
    tok_woa = ex.grad_mid("w_out_attn", dpa)
    g_wa = _mm_tn(xb, dpa, BF16, "mm_g_w_in_attn", shard_cols=wa4.shape[2], after=tok_woa)
    ex.grad_begin("w_in_attn", g_wa)
    join_toks = [ex.grad_sync("w_in_ssm", g_wa), ex.grad_sync("w_out_attn", g_wa)]
    d_rel = _bias_reduce(jnp.stack(dbt), buckets, hpg)[:, :, 0].T
    d_dsk_h = d_dsk.reshape(n_heads, SSM_HEAD_DIM).sum(axis=1)
    small_full = [d_rel, d_cw, d_cb, unpad_heads(d_dtb), unpad_heads(d_alog), d_dsk_h[None], d_nw,
                  jnp.concatenate([dg0, dg1], axis=0), jnp.concatenate([db0, db1], axis=0)]
    tok_wa = ex.grad_mid("w_in_attn", ex.small_grads(small_full))
    for t in join_toks:
        tok_wa = tok_wa if t is None or tok_wa is None else tok_wa + t
    grad_x = _mm_nt_sharded_k(dpa, wa4, F32, "mm_d_x0", add=du0, add_scale=DEEPNORM_ALPHA, after=tok_wa)
    return loss_local, grad_x[None]


def kernel(x, w_in_attn, w_out_attn, rel_bias, w_in_ssm, conv_w, conv_b, dt_bias, a_log, d_skip, ssm_norm_w, w_out_ssm, ln_g, ln_b, loss_target, m_w_in_attn, m_w_out_attn, m_rel_bias, m_w_in_ssm, m_conv_w, m_conv_b, m_dt_bias, m_a_log, m_d_skip, m_ssm_norm_w, m_w_out_ssm, m_ln_g, m_ln_b, v_w_in_attn, v_w_out_attn, v_rel_bias, v_w_in_ssm, v_conv_w, v_conv_b, v_dt_bias, v_a_log, v_d_skip, v_ssm_norm_w, v_w_out_ssm, v_ln_g, v_ln_b):
    xi, yi, ci = lax.axis_index("x"), lax.axis_index("y"), lax.axis_index("c")
    chip = 2 * xi + yi
    cvec = jnp.reshape(ci, (1,)).astype(jnp.int32)
    kvec = jnp.reshape(chip, (1,)).astype(jnp.int32)

    cw_l, cb_l, nw_l = conv_w[0], conv_b[0], ssm_norm_w[0]
    vec_shapes = [cw_l.shape, cb_l.shape, nw_l.shape]
    vec_all = _all_gather_small(_pack([cw_l, cb_l, nw_l]), False, "gather_vectors")
    l0 = [_cast_to_slot(w_in_attn[0], kvec, "cast_w_in_attn"), _cast_to_slot(w_out_attn[0], kvec, "cast_w_out_attn")]
    ex = _Exchange(kvec, cvec, l0, after=vec_all)
    _, tok0 = ex.l0_begin()
    tok1 = ex.ssm_gather_start([_cast_to_slot(w_in_ssm[0].T, kvec, "cast_w_in_ssm", tok0),
                                _cast_to_slot(w_out_ssm[0], kvec, "cast_w_out_ssm", tok0)])
    parts = [_unpack(vec_all[2 * j], vec_shapes) for j in range(N_CHIPS)]
    conv_w_f = jnp.concatenate([p[0] for p in parts], axis=1)
    conv_b_f = jnp.concatenate([p[1] for p in parts], axis=0)[None]
    norm_w_f = jnp.concatenate([p[2] for p in parts], axis=0)[None]

    loss_local, grad_x = _local_step(
        x[0], loss_target[0], ex, kvec, conv_w_f, conv_b_f, norm_w_f, rel_bias, dt_bias, a_log,
        d_skip, ln_g, ln_b)
    loss = lax.psum(loss_local, ("x", "y", "c"))

    big_w = dict(w_in_attn=(w_in_attn, m_w_in_attn, v_w_in_attn), w_out_attn=(w_out_attn, m_w_out_attn, v_w_out_attn),
                 w_in_ssm=(w_in_ssm, m_w_in_ssm, v_w_in_ssm), w_out_ssm=(w_out_ssm, m_w_out_ssm, v_w_out_ssm))
    big = {}

    def finish(names, after):
        last = after
        for nm in names:
            gf = ex.grad_end(nm, last)
            flip = (lambda t: t.T) if nm == "w_in_ssm" else (lambda t: t)
            w_, m_, v_ = (flip(t[0]) for t in big_w[nm])
            res = _adamw(w_, gf, m_, v_, "adamw_" + nm)
            big[nm] = [flip(t)[None] for t in res]
            last = res[3]
        return last

    last = finish(["w_out_ssm", "w_in_ssm", "w_out_attn"], grad_x)
    ex.grad_sync("w_in_attn", last)
    finish(["w_in_attn"], last)

    s_rel, s_cw, s_cb, s_dtb, s_alog, s_dsk, s_nw, s_lng, s_lnb = ex.small
    cwc, nwc = conv_w.shape[2], ssm_norm_w.shape[1]
    s_cw = lax.dynamic_slice_in_dim(s_cw, chip * cwc, cwc, axis=1)[None]
    s_cb = lax.dynamic_slice_in_dim(s_cb, chip * cwc, cwc, axis=1)
    s_nw = lax.dynamic_slice_in_dim(s_nw, chip * nwc, nwc, axis=1)
    small_names = ["rel_bias", "conv_w", "conv_b", "dt_bias", "a_log", "d_skip", "ssm_norm_w", "ln_g", "ln_b"]
    small_g = [s_rel, s_cw, s_cb, s_dtb, s_alog, s_dsk, s_nw, s_lng, s_lnb]
    small_w = [rel_bias, conv_w, conv_b, dt_bias, a_log, d_skip, ssm_norm_w, ln_g, ln_b]
    small_m = [m_rel_bias, m_conv_w, m_conv_b, m_dt_bias, m_a_log, m_d_skip, m_ssm_norm_w, m_ln_g, m_ln_b]
    small_v = [v_rel_bias, v_conv_w, v_conv_b, v_dt_bias, v_a_log, v_d_skip, v_ssm_norm_w, v_ln_g, v_ln_b]
    shapes = [t.shape for t in small_w]
    res = _adamw(_pack(small_w), _pack(small_g), _pack(small_m), _pack(small_v), "adamw_small")
    small = {nm: [] for nm in small_names}
    for packed in res:
        for nm, t in zip(small_names, _unpack(packed, shapes)):
            small[nm].append(t)

    order = ["w_in_attn", "w_out_attn", "rel_bias", "w_in_ssm", "conv_w", "conv_b", "dt_bias", "a_log",
             "d_skip", "ssm_norm_w", "w_out_ssm", "ln_g", "ln_b"]
    table = {**big, **small}
    outs = [loss, grad_x]
    for kind in range(4):
        outs += [table[nm][kind] for nm in order]
    return tuple(outs)
```

```python
import math

import numpy as np
import jax
import jax.numpy as jnp
from jax import lax
from jax.experimental import pallas as pl
from jax.experimental.pallas import tpu as pltpu

F32 = jnp.float32
BF16 = jnp.bfloat16
MESH = pl.DeviceIdType.MESH

ATTN_PATTERNS = ((128, 1), (512, 4), (2048, 16))
N_GROUPS_ATTN = 3
HEAD_DIM = 128
ATTN_BLOCK = 128
NUM_BUCKETS = 32
MAX_DISTANCE = 2048
SSM_HEAD_DIM = 64
HEADS_PER_SSM_GROUP = 16
SSM_GROUP_WIDTH = HEADS_PER_SSM_GROUP * SSM_HEAD_DIM
D_STATE = 128
CONV_WIDTH = 4
CHUNK = 128
DEPTH = 2
DEEPNORM_ALPHA = (2 * DEPTH) ** 0.25
LN_EPS = 1e-5
RMS_EPS = 1e-5
NEG_INF = -1e30
ADAM_LR = 0.001
ADAM_B1 = 0.9
ADAM_B2 = 0.999
ADAM_EPS = 1e-08
ADAM_WD = 0.01
ADAM_STEP = 10

N_CHIPS = 4
N_DEV = 8

VMEM_LIMIT_V7X = 56 * 1024 * 1024
LANES = 128


def _cparams(sem=None):
    return pltpu.CompilerParams(dimension_semantics=sem, vmem_limit_bytes=VMEM_LIMIT_V7X)


def _sigmoid(x):
    return 0.5 * jnp.tanh(0.5 * x) + 0.5


def _dot(a, b):
    return jnp.dot(a, b, preferred_element_type=F32)


def _dot_nt(a, b):
    return lax.dot_general(a, b, (((1,), (1,)), ((), ())), preferred_element_type=F32)


def _dot_tn(a, b):
    return lax.dot_general(a, b, (((0,), (0,)), ((), ())), preferred_element_type=F32)


def _split2(x):
    hi = x.astype(BF16)
    lo = (x - hi.astype(F32)).astype(BF16)
    return hi, lo


def _split3(x):
    hi = x.astype(BF16)
    r = x - hi.astype(F32)
    mid = r.astype(BF16)
    lo = (r - mid.astype(F32)).astype(BF16)
    return hi, mid, lo


def _matmul(a, b, *, mode, grid, a_spec, b_spec, out_shape, out_spec, tile, name,
            add=None, add_spec=None, add_scale=1.0, after=None, into=None):
    nk = grid[2]
    tm, tn = tile
    dot = {"nn": _dot, "nt": _dot_nt, "tn": _dot_tn}[mode]
    has_add = add is not None
    has_after = after is not None
    has_into = into is not None

    def finish(r, add_ref, o_ref):
        if has_add:
            r = r + add_scale * add_ref[...].astype(F32)
        o_ref[...] = r.astype(o_ref.dtype)

    def body_one(*refs):
        a_ref, b_ref = refs[:2]
        finish(dot(a_ref[...].astype(BF16), b_ref[...].astype(BF16)), refs[2] if has_add else None, refs[-1])

    def body_acc(*refs):
        a_ref, b_ref = refs[:2]
        add_ref = refs[2] if has_add else None
        o_ref, acc_ref = refs[-2:]
        k = pl.program_id(2)

        @pl.when(k == 0)
        def _():
            acc_ref[...] = jnp.zeros_like(acc_ref)

        acc_ref[...] += dot(a_ref[...].astype(BF16), b_ref[...].astype(BF16))

        @pl.when(k == nk - 1)
        def _():
            finish(acc_ref[...], add_ref, o_ref)

    in_specs = ([a_spec, b_spec] + ([add_spec] if has_add else []) + ([_ANY] if has_after else [])
                + ([_ANY] if has_into else []))
    args = (a, b) + ((add,) if has_add else ()) + ((after,) if has_after else ()) + ((into,) if has_into else ())
    return pl.pallas_call(
        body_one if nk == 1 else body_acc, name=name, grid=grid, in_specs=in_specs, out_specs=out_spec,
        out_shape=out_shape,
        input_output_aliases={len(args) - 1: 0} if has_into else {},
        scratch_shapes=[] if nk == 1 else [pltpu.VMEM((tm, tn), F32)],
        compiler_params=_cparams(("parallel", "parallel", "arbitrary")),
    )(*args)


def _pick(n, pref):
    for t in pref:
        if n % t == 0:
            return t
    return n


_TILE_PREF = (1024, 512, 256, 128)
_K_TILE_PREF = (2048,) + _TILE_PREF


def _k_tile(k, out_dtype, has_add):
    return _pick(k, _K_TILE_PREF if (has_add or out_dtype != BF16) else (4096,) + _K_TILE_PREF)


def _mm_nn_sharded(a, w4, out_dtype, name, after=None):
    m, k = a.shape
    _, _, nn = w4.shape
    tm, tk, tn = _pick(m, _TILE_PREF), _k_tile(k, out_dtype, False), _pick(nn, _TILE_PREF)
    npb = nn // tn
    return _matmul(
        a, w4, mode="nn", grid=(m // tm, N_CHIPS * npb, k // tk), tile=(tm, tn), name=name,
        a_spec=pl.BlockSpec((tm, tk), lambda i, j, kk: (i, kk)),
        b_spec=pl.BlockSpec((None, tk, tn), lambda i, j, kk: (j // npb, kk, j % npb)),
        out_shape=jax.ShapeDtypeStruct((m, N_CHIPS * nn), out_dtype),
        out_spec=pl.BlockSpec((tm, tn), lambda i, j, kk: (i, j)), after=after)


def _mm_nn(a, b, out_dtype, name, b_row_off=0, add=None, add_scale=1.0):
    m, k = a.shape
    _, n = b.shape
    tm, tk, tn = _pick(m, _TILE_PREF), _k_tile(k, out_dtype, add is not None), _pick(n, _TILE_PREF)
    assert b_row_off % tk == 0
    ko = b_row_off // tk
    return _matmul(
        a, b, mode="nn", grid=(m // tm, n // tn, k // tk), tile=(tm, tn), name=name,
        a_spec=pl.BlockSpec((tm, tk), lambda i, j, kk: (i, kk)),
        b_spec=pl.BlockSpec((tk, tn), lambda i, j, kk: (kk + ko, j)),
        out_shape=jax.ShapeDtypeStruct((m, n), out_dtype),
        out_spec=pl.BlockSpec((tm, tn), lambda i, j, kk: (i, j)),
        add=add, add_spec=pl.BlockSpec((tm, tn), lambda i, j, kk: (i, j)), add_scale=add_scale)


def _mm_nt(a, b, out_dtype, name, add=None, add_scale=1.0, n=None, b_row_off=0):
    m, k = a.shape
    n = b.shape[0] if n is None else n
    tm, tk, tn = _pick(m, _TILE_PREF), _k_tile(k, out_dtype, add is not None), _pick(n, _TILE_PREF)
    assert b_row_off % tn == 0
    no = b_row_off // tn
    return _matmul(
        a, b, mode="nt", grid=(m // tm, n // tn, k // tk), tile=(tm, tn), name=name,
        a_spec=pl.BlockSpec((tm, tk), lambda i, j, kk: (i, kk)),
        b_spec=pl.BlockSpec((tn, tk), lambda i, j, kk: (j + no, kk)),
        out_shape=jax.ShapeDtypeStruct((m, n), out_dtype),
        out_spec=pl.BlockSpec((tm, tn), lambda i, j, kk: (i, j)),
        add=add, add_spec=pl.BlockSpec((tm, tn), lambda i, j, kk: (i, j)), add_scale=add_scale)


def _mm_nt_sharded_k(a, w4, out_dtype, name, add=None, add_scale=1.0, after=None):
    m, _ = a.shape
    _, n, kn = w4.shape
    tm, tk, tn = _pick(m, _TILE_PREF), _pick(kn, (2560,) + _TILE_PREF), _pick(n, _TILE_PREF)
    kpb = kn // tk
    return _matmul(
        a, w4, mode="nt", grid=(m // tm, n // tn, N_CHIPS * kpb), tile=(tm, tn), name=name,
        a_spec=pl.BlockSpec((tm, tk), lambda i, j, kk: (i, kk)),
        b_spec=pl.BlockSpec((None, tn, tk), lambda i, j, kk: (kk // kpb, j, kk % kpb)),
        out_shape=jax.ShapeDtypeStruct((m, n), out_dtype),
        out_spec=pl.BlockSpec((tm, tn), lambda i, j, kk: (i, j)),
        add=add, add_spec=pl.BlockSpec((tm, tn), lambda i, j, kk: (i, j)), add_scale=add_scale, after=after)


def _mm_tn(a, b, out_dtype, name, shard_cols=None, out_rows=None, out_row_off=0, into=None, after=None):
    k, m = a.shape
    _, n = b.shape
    nn = n if shard_cols is None else shard_cols
    tm, tk, tn = _pick(m, _TILE_PREF), _k_tile(k, out_dtype, False), _pick(nn, _TILE_PREF)
    if shard_cols is None:
        assert out_row_off % tm == 0
        ro = out_row_off // tm
        out_shape = jax.ShapeDtypeStruct((m if out_rows is None else out_rows, n), out_dtype)
        out_spec = pl.BlockSpec((tm, tn), lambda i, j, kk: (i + ro, j))
    else:
        npb = nn // tn
        out_shape = jax.ShapeDtypeStruct((n // nn, m, nn), out_dtype)
        out_spec = pl.BlockSpec((None, tm, tn), lambda i, j, kk: (j // npb, i, j % npb))
    return _matmul(
        a, b, mode="tn", grid=(m // tm, n // tn, k // tk), tile=(tm, tn), name=name,
        a_spec=pl.BlockSpec((tk, tm), lambda i, j, kk: (kk, i)),
        b_spec=pl.BlockSpec((tk, tn), lambda i, j, kk: (kk, j)),
        out_shape=out_shape, out_spec=out_spec, into=into, after=after)


def _bucket_tiles():
    qi = np.arange(ATTN_BLOCK)[:, None]
    ki = np.arange(2 * ATTN_BLOCK)[None, :]
    delta = np.clip(ATTN_BLOCK + qi - ki, 0, None)
    tiles = []
    max_exact = NUM_BUCKETS // 2
    for _, dil in ATTN_PATTERNS:
        dist = (delta * dil).astype(np.int32)
        d_f = np.maximum(dist, 1).astype(np.float32)
        large = max_exact + (np.log(d_f / np.float32(max_exact)) / np.float32(math.log(MAX_DISTANCE / max_exact))
                             * np.float32(NUM_BUCKETS - max_exact)).astype(np.int32)
        large = np.minimum(large, NUM_BUCKETS - 1)
        tiles.append(np.where(dist < max_exact, dist, large).astype(np.int32))
    return jnp.asarray(np.stack(tiles))


def _bias_expand(rel_bias, buckets, hpg):
    def body(tab_ref, bk_ref, o_ref):
        g, h = pl.program_id(0), pl.program_id(1)
        bk = bk_ref[...]
        acc = jnp.zeros((ATTN_BLOCK, 2 * ATTN_BLOCK), F32)
        for b in range(NUM_BUCKETS):
            acc = jnp.where(bk == b, tab_ref[b, g * hpg + h], acc)
        o_ref[...] = acc

    return pl.pallas_call(
        body, name="bias_expand", grid=(N_GROUPS_ATTN, hpg),
        in_specs=[pl.BlockSpec(memory_space=pltpu.SMEM),
                  pl.BlockSpec((None, ATTN_BLOCK, 2 * ATTN_BLOCK), lambda g, h: (g, 0, 0))],
        out_specs=pl.BlockSpec((None, None, ATTN_BLOCK, 2 * ATTN_BLOCK), lambda g, h: (g, h, 0, 0)),
        out_shape=jax.ShapeDtypeStruct((N_GROUPS_ATTN, hpg, ATTN_BLOCK, 2 * ATTN_BLOCK), F32),
        compiler_params=_cparams(("parallel", "parallel")),
    )(rel_bias, buckets)


def _bias_reduce(dtiles, buckets, hpg):
    def body(t_ref, bk_ref, o_ref):
        bk = bk_ref[...]
        t = t_ref[...]
        rows = lax.broadcasted_iota(jnp.int32, (NUM_BUCKETS, LANES), 0)
        acc = jnp.zeros((NUM_BUCKETS, LANES), F32)
        for b in range(NUM_BUCKETS):
            s = jnp.sum(jnp.sum(jnp.where(bk == b, t, 0.0), axis=1, keepdims=True), axis=0, keepdims=True)
            acc = jnp.where(rows == b, s, acc)
        o_ref[...] = acc

    return pl.pallas_call(
        body, name="bias_reduce", grid=(N_GROUPS_ATTN, hpg),
        in_specs=[pl.BlockSpec((None, None, ATTN_BLOCK, 2 * ATTN_BLOCK), lambda g, h: (g, h, 0, 0)),
                  pl.BlockSpec((None, ATTN_BLOCK, 2 * ATTN_BLOCK), lambda g, h: (g, 0, 0))],
        out_specs=pl.BlockSpec((None, NUM_BUCKETS, LANES), lambda g, h: (g * hpg + h, 0, 0)),
        out_shape=jax.ShapeDtypeStruct((N_GROUPS_ATTN * hpg, NUM_BUCKETS, LANES), F32),
        compiler_params=_cparams(("parallel", "parallel")),
    )(dtiles, buckets)


def _cast_x3(x, name, after=None):
    r, c = x.shape
    tr = _pick(r, (512, 256, 128, 8))
    extra = [] if after is None else [after]

    def body(x_ref, *rest):
        rest[-1][...] = x_ref[...].astype(BF16)

    return pl.pallas_call(
        body, name=name, grid=(r // tr,),
        in_specs=[pl.BlockSpec((tr, c), lambda i: (i, 0))] + [_ANY] * len(extra),
        out_specs=pl.BlockSpec((None, tr, c), lambda i: (0, i, 0)),
        out_shape=jax.ShapeDtypeStruct((N_GROUPS_ATTN, r, c), BF16),
        compiler_params=_cparams(("parallel",)),
    )(x, *extra)


def _class_copy(x3, slot, dil, name):
    _, s, d = x3.shape
    rows = s // dil
    tm = _pick(rows, (512, 256, 128))
    nbk = rows // tm

    def body(v_ref, x3_ref, o_ref):
        o_ref[...] = v_ref[...]

    return pl.pallas_call(
        body, name=name, grid=(dil, nbk),
        in_specs=[pl.BlockSpec((tm, d), lambda r, i: (i, r)), _ANY],
        out_specs=pl.BlockSpec((None, tm, d), lambda r, i: (slot, r * nbk + i, 0)),
        out_shape=jax.ShapeDtypeStruct(x3.shape, x3.dtype), input_output_aliases={1: 0},
        compiler_params=_cparams(("parallel", "parallel")),
    )(x3[0].reshape(rows, dil * d), x3)


def _in_proj_shard(x3, wa4, kvec, p, d_attn, name, into=None, after=None):
    _, s, d = x3.shape
    _, _, nn = wa4.shape
    tm = _pick(s, _TILE_PREF)
    tn = _pick(math.gcd(nn, 3 * d_attn), _TILE_PREF)
    npb, bpg = nn // tn, 3 * d_attn // tn
    extra = ([] if after is None else [after]) + ([] if into is None else [into])

    def block(k, j):
        return jnp.bitwise_xor(k[0], p) * npb + j

    def slot(k, j):
        jb = block(k, j)
        return jnp.where(jb < N_GROUPS_ATTN * bpg, jb // bpg, 0)

    def body(k_ref, a_ref, b_ref, *rest):
        rest[-1][...] = _dot(a_ref[...], b_ref[...]).astype(BF16)

    grid_spec = pltpu.PrefetchScalarGridSpec(
        num_scalar_prefetch=1, grid=(s // tm, npb),
        in_specs=[pl.BlockSpec((None, tm, d), lambda i, j, k: (slot(k, j), i, 0)),
                  pl.BlockSpec((None, d, tn), lambda i, j, k: (jnp.bitwise_xor(k[0], p), 0, j))]
        + [_ANY] * len(extra),
        out_specs=pl.BlockSpec((tm, tn), lambda i, j, k: (i, block(k, j))))
    return pl.pallas_call(
        body, name=name, grid_spec=grid_spec, out_shape=jax.ShapeDtypeStruct((s, N_CHIPS * nn), BF16),
        input_output_aliases={} if into is None else {2 + len(extra): 0},
        compiler_params=_cparams(("parallel", "parallel")),
    )(kvec, x3, wa4, *extra)


def _attn_valid(n_is_first):
    qi = lax.broadcasted_iota(jnp.int32, (ATTN_BLOCK, 2 * ATTN_BLOCK), 0)
    ki = lax.broadcasted_iota(jnp.int32, (ATTN_BLOCK, 2 * ATTN_BLOCK), 1)
    delta = ATTN_BLOCK + qi - ki
    band = (delta >= 0) & (delta <= ATTN_BLOCK)
    return band & (jnp.logical_not(n_is_first) | (ki >= ATTN_BLOCK))


def _attn_fwd(pg, bias, g, dil, hpg):
    s = pg.shape[0]
    w = hpg * HEAD_DIM
    rows = s // dil
    nb = rows // ATTN_BLOCK
    scale = HEAD_DIM ** -0.5

    def body(q_ref, kc_ref, kp_ref, vc_ref, vp_ref, bias_ref, o_ref, lse_ref):
        valid = _attn_valid(pl.program_id(1) == 0)
        lane = lax.broadcasted_iota(jnp.int32, (ATTN_BLOCK, LANES), 1)
        lse = jnp.zeros((ATTN_BLOCK, LANES), F32)
        for h in range(hpg):
            sl = slice(h * HEAD_DIM, (h + 1) * HEAD_DIM)
            k2 = jnp.concatenate([kp_ref[:, sl], kc_ref[:, sl]], axis=0)
            v2 = jnp.concatenate([vp_ref[:, sl], vc_ref[:, sl]], axis=0)
            sc = _dot_nt(q_ref[:, sl], k2) * scale + bias_ref[h]
            sc = jnp.where(valid, sc, NEG_INF)
            m = jnp.max(sc, axis=1, keepdims=True)
            p = jnp.exp(sc - m)
            l = jnp.sum(p, axis=1, keepdims=True)
            o_ref[:, sl] = (_dot(p.astype(BF16), v2) * (1.0 / l)).astype(BF16)
            lse = jnp.where(lane == h, m + jnp.log(l), lse)
        lse_ref[...] = lse

    def col(off):
        return lambda r, n: (r * nb + n, 3 * g + off)

    def colp(off):
        return lambda r, n: (r * nb + jnp.maximum(n - 1, 0), 3 * g + off)

    blk = (ATTN_BLOCK, w)
    tok = pl.BlockSpec(blk, lambda r, n: (n, r))
    tok1 = pl.BlockSpec((ATTN_BLOCK, LANES), lambda r, n: (n, r))
    o, lse = pl.pallas_call(
        body, name=f"attn_fwd_g{g}", grid=(dil, nb),
        in_specs=[pl.BlockSpec(blk, col(0)), pl.BlockSpec(blk, col(1)), pl.BlockSpec(blk, colp(1)),
                  pl.BlockSpec(blk, col(2)), pl.BlockSpec(blk, colp(2)),
                  pl.BlockSpec((None, hpg, ATTN_BLOCK, 2 * ATTN_BLOCK), lambda r, n: (g, 0, 0, 0))],
        out_specs=[tok, tok1],
        out_shape=[jax.ShapeDtypeStruct((rows, dil * w), BF16), jax.ShapeDtypeStruct((rows, dil * LANES), F32)],
        compiler_params=_cparams(("parallel", "parallel")),
    )(pg, pg, pg, pg, pg, bias)
    return o.reshape(s, w), lse.reshape(s, LANES)


def _attn_combine(os_, lses, pa, hpg):
    s, w = os_[0].shape
    gate_blk = pa.shape[1] // w - 1
    tm = _pick(s, (256, 128))

    def body(o0, o1, o2, l0, l1, l2, gate_ref, o_ref, lse_ref, y_ref):
        a0, a1, a2 = l0[...], l1[...], l2[...]
        m = jnp.maximum(jnp.maximum(a0, a1), a2)
        e0, e1, e2 = jnp.exp(a0 - m), jnp.exp(a1 - m), jnp.exp(a2 - m)
        den = e0 + e1 + e2
        inv = 1.0 / den
        w0, w1, w2 = e0 * inv, e1 * inv, e2 * inv
        lse_ref[...] = m + jnp.log(den)
        for h in range(hpg):
            sl = slice(h * HEAD_DIM, (h + 1) * HEAD_DIM)
            o = (w0[:, h:h + 1] * o0[:, sl].astype(F32) + w1[:, h:h + 1] * o1[:, sl].astype(F32)
                 + w2[:, h:h + 1] * o2[:, sl].astype(F32))
            gate = gate_ref[:, sl].astype(F32)
            o_ref[:, sl] = o.astype(BF16)
            y_ref[:, sl] = (o * (gate * _sigmoid(gate))).astype(BF16)

    spec = pl.BlockSpec((tm, w), lambda i: (i, 0))
    spec1 = pl.BlockSpec((tm, LANES), lambda i: (i, 0))
    return pl.pallas_call(
        body, name="attn_combine", grid=(s // tm,),
        in_specs=[spec] * 3 + [spec1] * 3 + [pl.BlockSpec((tm, w), lambda i: (i, gate_blk))],
        out_specs=[spec, spec1, spec],
        out_shape=[jax.ShapeDtypeStruct((s, w), BF16), jax.ShapeDtypeStruct((s, LANES), F32),
                   jax.ShapeDtypeStruct((s, w), BF16)],
        compiler_params=_cparams(("parallel",)),
    )(*os_, *lses, pa)


def _attn_pre_bwd(dy, o, pa, hpg):
    s, w = dy.shape
    gate_blk = pa.shape[1] // w - 1
    tm = _pick(s, (256, 128))

    def body(dy_ref, o_ref, gate_ref, do_ref, dl_ref, dg_ref):
        gate = gate_ref[...].astype(F32)
        sg = _sigmoid(gate)
        dyv = dy_ref[...].astype(F32)
        ov = o_ref[...].astype(F32)
        do = dyv * (gate * sg)
        do_ref[...] = do.astype(BF16)
        dg_ref[...] = (dyv * ov * (sg * (1.0 + gate * (1.0 - sg)))).astype(BF16)
        prod = do * ov
        lane = lax.broadcasted_iota(jnp.int32, (tm, LANES), 1)
        dl = jnp.zeros((tm, LANES), F32)
        for h in range(hpg):
            sl = slice(h * HEAD_DIM, (h + 1) * HEAD_DIM)
            dl = jnp.where(lane == h, jnp.sum(prod[:, sl], axis=1, keepdims=True), dl)
        dl_ref[...] = dl

    spec = pl.BlockSpec((tm, w), lambda i: (i, 0))
    return pl.pallas_call(
        body, name="attn_pre_bwd", grid=(s // tm,),
        in_specs=[spec, spec, pl.BlockSpec((tm, w), lambda i: (i, gate_blk))],
        out_specs=[spec, pl.BlockSpec((tm, LANES), lambda i: (i, 0)), spec],
        out_shape=[jax.ShapeDtypeStruct((s, w), BF16), jax.ShapeDtypeStruct((s, LANES), F32),
                   jax.ShapeDtypeStruct((s, w), BF16)],
        compiler_params=_cparams(("parallel",)),
    )(dy, o, pa)


def _attn_bwd(pg, bias, do, lse, delta, g, dil, hpg):
    s = pg.shape[0]
    w = hpg * HEAD_DIM
    rows = s // dil
    nb = rows // ATTN_BLOCK
    dov = do.reshape(rows, dil * w)
    lsev, dlv = (t.reshape(rows, dil * LANES) for t in (lse, delta))
    scale = HEAD_DIM ** -0.5

    def body(q_ref, kc_ref, kp_ref, vc_ref, vp_ref, bias_ref, do_ref, lse_ref, dl_ref,
             dq_ref, dk_ref, dv_ref, db_ref, dkc_ref, dvc_ref):
        r, i = pl.program_id(0), pl.program_id(1)
        n = nb - 1 - i
        valid = _attn_valid(n == 0)

        @pl.when((r == 0) & (i == 0))
        def _():
            db_ref[...] = jnp.zeros_like(db_ref)

        @pl.when(i == 0)
        def _():
            dkc_ref[...] = jnp.zeros_like(dkc_ref)
            dvc_ref[...] = jnp.zeros_like(dvc_ref)

        for h in range(hpg):
            sl = slice(h * HEAD_DIM, (h + 1) * HEAD_DIM)
            q = q_ref[:, sl]
            dov_ = do_ref[:, sl]
            k2 = jnp.concatenate([kp_ref[:, sl], kc_ref[:, sl]], axis=0)
            v2 = jnp.concatenate([vp_ref[:, sl], vc_ref[:, sl]], axis=0)
            sc = _dot_nt(q, k2) * scale + bias_ref[h]
            p = jnp.exp(jnp.where(valid, sc - lse_ref[:, h:h + 1], NEG_INF))
            dp = _dot_nt(dov_, v2)
            ds = p * (dp - dl_ref[:, h:h + 1])
            db_ref[h] += ds
            dsb = ds.astype(BF16)
            dq_ref[:, sl] = (_dot(dsb, k2) * scale).astype(BF16)
            dk2 = _dot_tn(dsb, q) * scale
            dv2 = _dot_tn(p.astype(BF16), dov_)
            dk_ref[:, sl] = (dk2[ATTN_BLOCK:] + dkc_ref[:, sl]).astype(BF16)
            dv_ref[:, sl] = (dv2[ATTN_BLOCK:] + dvc_ref[:, sl]).astype(BF16)
            dkc_ref[:, sl] = dk2[:ATTN_BLOCK]
            dvc_ref[:, sl] = dv2[:ATTN_BLOCK]

    def col(off):
        return lambda r, i: (r * nb + nb - 1 - i, 3 * g + off)

    def colp(off):
        return lambda r, i: (r * nb + jnp.maximum(nb - 2 - i, 0), 3 * g + off)

    blk = (ATTN_BLOCK, w)
    tok = pl.BlockSpec(blk, lambda r, i: (nb - 1 - i, r))
    tok1 = pl.BlockSpec((ATTN_BLOCK, LANES), lambda r, i: (nb - 1 - i, r))
    dq, dk, dv, db = pl.pallas_call(
        body, name=f"attn_bwd_g{g}", grid=(dil, nb),
        in_specs=[pl.BlockSpec(blk, col(0)), pl.BlockSpec(blk, col(1)), pl.BlockSpec(blk, colp(1)),
                  pl.BlockSpec(blk, col(2)), pl.BlockSpec(blk, colp(2)),
                  pl.BlockSpec((None, hpg, ATTN_BLOCK, 2 * ATTN_BLOCK), lambda r, i: (g, 0, 0, 0)),
                  tok, tok1, tok1],
        out_specs=[tok, tok, tok,
                   pl.BlockSpec((hpg, ATTN_BLOCK, 2 * ATTN_BLOCK), lambda r, i: (0, 0, 0))],
        out_shape=[jax.ShapeDtypeStruct((rows, dil * w), BF16)] * 3
        + [jax.ShapeDtypeStruct((hpg, ATTN_BLOCK, 2 * ATTN_BLOCK), F32)],
        scratch_shapes=[pltpu.VMEM(blk, F32), pltpu.VMEM(blk, F32)],
        compiler_params=_cparams(("arbitrary", "arbitrary")),
    )(pg, pg, pg, pg, pg, bias, dov, lsev, dlv)
    return dq.reshape(s, w), dk.reshape(s, w), dv.reshape(s, w), db


def _ln_fwd(xin, h, gamma, beta, name, affine_in=None, target=None):
    s, d = xin.shape
    tm = _pick(s, (128,))
    has_aff = affine_in is not None
    has_tgt = target is not None

    def body(*refs):
        it = iter(refs)
        x_ref, h_ref, g_ref, b_ref = next(it), next(it), next(it), next(it)
        if has_aff:
            gi_ref, bi_ref = next(it), next(it)
        if has_tgt:
            t_ref = next(it)
        xh_ref, rs_ref = next(it), next(it)
        x = x_ref[...]
        if has_aff:
            x = x * gi_ref[...] + bi_ref[...]
        u = DEEPNORM_ALPHA * x + h_ref[...]
        mu = jnp.mean(u, axis=1, keepdims=True)
        uc = u - mu
        var = jnp.mean(uc * uc, axis=1, keepdims=True)
        rstd = lax.rsqrt(var + LN_EPS)
        xhat = uc * rstd
        xh_ref[...] = xhat
        rs_ref[...] = rstd
        y = xhat * g_ref[...] + b_ref[...]
        if has_tgt:
            dy_ref, l_ref = next(it), next(it)
            e = y - t_ref[...]
            dy_ref[...] = e * (1.0 / d)
            l_ref[...] = jnp.sum(e * e, axis=1, keepdims=True)
        else:
            y_ref = next(it)
            y_ref[...] = y.astype(BF16)

    row = pl.BlockSpec((tm, d), lambda i: (i, 0))
    vec = pl.BlockSpec((1, d), lambda i: (0, 0))
    one = pl.BlockSpec((tm, 1), lambda i: (i, 0))
    in_specs = [row, row, vec, vec] + ([vec, vec] if has_aff else []) + ([row] if has_tgt else [])
    args = [xin, h, gamma, beta] + (list(affine_in) if has_aff else []) + ([target] if has_tgt else [])
    out_specs = [row, one] + ([row, one] if has_tgt else [row])
    out_shape = [jax.ShapeDtypeStruct((s, d), F32), jax.ShapeDtypeStruct((s, 1), F32)]
    out_shape += ([jax.ShapeDtypeStruct((s, d), F32), jax.ShapeDtypeStruct((s, 1), F32)] if has_tgt
                  else [jax.ShapeDtypeStruct((s, d), BF16)])
    return pl.pallas_call(
        body, name=name, grid=(s // tm,), in_specs=in_specs, out_specs=out_specs, out_shape=out_shape,
        compiler_params=_cparams(("parallel",)),
    )(*args)


def _ln_bwd(dy, xhat, rstd, gamma, name):
    s, d = dy.shape
    tm = _pick(s, (128,))

    def body(dy_ref, xh_ref, rs_ref, g_ref, du_ref, dub_ref, dg_ref, db_ref):
        @pl.when(pl.program_id(0) == 0)
        def _():
            dg_ref[...] = jnp.zeros_like(dg_ref)
            db_ref[...] = jnp.zeros_like(db_ref)

        dyv = dy_ref[...]
        xh = xh_ref[...]
        dg_ref[...] += jnp.sum(dyv * xh, axis=0, keepdims=True)
        db_ref[...] += jnp.sum(dyv, axis=0, keepdims=True)
        dxh = dyv * g_ref[...]
        m1 = jnp.mean(dxh, axis=1, keepdims=True)
        m2 = jnp.mean(dxh * xh, axis=1, keepdims=True)
        du = rs_ref[...] * (dxh - m1 - xh * m2)
        du_ref[...] = du
        dub_ref[...] = du.astype(BF16)

    row = pl.BlockSpec((tm, d), lambda i: (i, 0))
    vec = pl.BlockSpec((1, d), lambda i: (0, 0))
    one = pl.BlockSpec((tm, 1), lambda i: (i, 0))
    return pl.pallas_call(
        body, name=name, grid=(s // tm,), in_specs=[row, row, one, vec],
        out_specs=[row, row, vec, vec],
        out_shape=[jax.ShapeDtypeStruct((s, d), F32), jax.ShapeDtypeStruct((s, d), BF16),
                   jax.ShapeDtypeStruct((1, d), F32), jax.ShapeDtypeStruct((1, d), F32)],
        compiler_params=_cparams(("arbitrary",)),
    )(dy, xhat, rstd, gamma)


_HALO = 16
_STRIP = 16


def _strips(tm, fn, init, reverse=False):
    n = tm // _STRIP

    def step(i, carry):
        s_ = n - 1 - i if reverse else i
        return fn(pl.ds(pl.multiple_of(s_ * _STRIP, _STRIP), _STRIP), carry)

    return lax.fori_loop(0, n, step, init)


def _fold8(t):
    return t[0:8] + t[8:16]


def _conv_taps(ext, tm, w_ref):
    acc = None
    for k in range(CONV_WIDTH):
        lo = _HALO - (CONV_WIDTH - 1) + k
        term = w_ref[k:k + 1, :] * ext[lo:lo + tm, :]
        acc = term if acc is None else acc + term
    return acc


def _conv_strip(prev, cur, w_ref):
    ext = jnp.concatenate([prev, cur], axis=0)
    acc, taps = None, []
    for k in range(CONV_WIDTH):
        lo = _STRIP - (CONV_WIDTH - 1) + k
        taps.append(ext[lo:lo + _STRIP, :])
        term = w_ref[k:k + 1, :] * taps[k]
        acc = term if acc is None else acc + term
    return acc, taps


def _conv_fwd(pzx, conv_w, conv_b, d_inner):
    s, _ = pzx.shape
    cd = conv_w.shape[1]
    tm = _pick(s, (512, 256, 128))
    tc = _pick(cd, (1024, 512, 256, 128))
    off = d_inner // tc
    hb = tm // _HALO

    def body(x_ref, p_ref, w_ref, b_ref, o_ref):
        prev = jnp.where(pl.program_id(0) > 0, p_ref[...].astype(F32), 0.0)
        ext = jnp.concatenate([prev, x_ref[...].astype(F32)], axis=0)
        pre = _conv_taps(ext, tm, w_ref) + b_ref[...]
        o_ref[...] = (pre * _sigmoid(pre)).astype(BF16)

    return pl.pallas_call(
        body, name="conv_fwd", grid=(s // tm, cd // tc),
        in_specs=[pl.BlockSpec((tm, tc), lambda i, j: (i, off + j)),
                  pl.BlockSpec((_HALO, tc), lambda i, j: (jnp.maximum(i * hb - 1, 0), off + j)),
                  pl.BlockSpec((CONV_WIDTH, tc), lambda i, j: (0, j)),
                  pl.BlockSpec((1, tc), lambda i, j: (0, j))],
        out_specs=pl.BlockSpec((tm, tc), lambda i, j: (i, j)),
        out_shape=jax.ShapeDtypeStruct((s, cd), BF16),
        compiler_params=_cparams(("parallel", "parallel")),
    )(pzx, pzx, conv_w, conv_b)


def _conv_bwd_a(pzx, dxbc, conv_w, conv_b, d_inner):
    s, _ = pzx.shape
    cd = conv_w.shape[1]
    tm = _pick(s, (512, 256, 128))
    tc = _pick(cd, (1024, 512, 256, 128))
    off = d_inner // tc
    hb = tm // _HALO

    def body(x_ref, p_ref, d_ref, w_ref, b_ref, o_ref, dw_ref, db_ref, acc_ref):
        @pl.when(pl.program_id(1) == 0)
        def _():
            dw_ref[...] = jnp.zeros_like(dw_ref)
            db_ref[...] = jnp.zeros_like(db_ref)

        acc_ref[...] = jnp.zeros_like(acc_ref)

        def strip(rows, prev):
            cur = x_ref[rows, :].astype(F32)
            pre, taps = _conv_strip(prev, cur, w_ref)
            pre = pre + b_ref[...]
            sg = _sigmoid(pre)
            dpre = d_ref[rows, :].astype(F32) * (sg * (1.0 + pre * (1.0 - sg)))
            o_ref[rows, :] = dpre
            for k in range(CONV_WIDTH):
                acc_ref[k] += _fold8(dpre * taps[k])
            acc_ref[CONV_WIDTH] += _fold8(dpre)
            return cur

        _strips(tm, strip, jnp.where(pl.program_id(1) > 0, p_ref[...].astype(F32), 0.0))
        for k in range(CONV_WIDTH):
            dw_ref[k:k + 1, :] += jnp.sum(acc_ref[k], axis=0, keepdims=True)
        db_ref[...] += jnp.sum(acc_ref[CONV_WIDTH], axis=0, keepdims=True)

    return pl.pallas_call(
        body, name="conv_bwd_a", grid=(cd // tc, s // tm),
        in_specs=[pl.BlockSpec((tm, tc), lambda j, i: (i, off + j)),
                  pl.BlockSpec((_HALO, tc), lambda j, i: (jnp.maximum(i * hb - 1, 0), off + j)),
                  pl.BlockSpec((tm, tc), lambda j, i: (i, j)),
                  pl.BlockSpec((CONV_WIDTH, tc), lambda j, i: (0, j)),
                  pl.BlockSpec((1, tc), lambda j, i: (0, j))],
        out_specs=[pl.BlockSpec((tm, tc), lambda j, i: (i, j)),
                   pl.BlockSpec((CONV_WIDTH, tc), lambda j, i: (0, j)),
                   pl.BlockSpec((1, tc), lambda j, i: (0, j))],
        out_shape=[jax.ShapeDtypeStruct((s, cd), F32), jax.ShapeDtypeStruct((CONV_WIDTH, cd), F32),
                   jax.ShapeDtypeStruct((1, cd), F32)],
        scratch_shapes=[pltpu.VMEM((CONV_WIDTH + 1, 8, tc), F32)],
        compiler_params=_cparams(("parallel", "arbitrary")),
    )(pzx, pzx, dxbc, conv_w, conv_b)


def _conv_bwd_b(dpre, conv_w, into, col_off):
    s, cd = dpre.shape
    tm = _pick(s, (512, 256, 128))
    tc = _pick(cd, (1024, 512, 256, 128))
    hb = tm // 8
    nrb = s // tm
    assert col_off % tc == 0
    co = col_off // tc

    def body(x_ref, nx_ref, w_ref, into_ref, o_ref):
        nxt = jnp.where(pl.program_id(0) < nrb - 1, nx_ref[...], 0.0)
        ext = jnp.concatenate([x_ref[...], nxt], axis=0)
        acc = None
        for k in range(CONV_WIDTH):
            lo = CONV_WIDTH - 1 - k
            term = w_ref[k:k + 1, :] * ext[lo:lo + tm, :]
            acc = term if acc is None else acc + term
        o_ref[...] = acc.astype(BF16)

    return pl.pallas_call(
        body, name="conv_bwd_b", grid=(nrb, cd // tc),
        in_specs=[pl.BlockSpec((tm, tc), lambda i, j: (i, j)),
                  pl.BlockSpec((8, tc), lambda i, j: (jnp.minimum((i + 1) * hb, s // 8 - 1), j)),
                  pl.BlockSpec((CONV_WIDTH, tc), lambda i, j: (0, j)), _ANY],
        out_specs=pl.BlockSpec((tm, tc), lambda i, j: (i, j + co)),
        out_shape=jax.ShapeDtypeStruct(into.shape, BF16),
        input_output_aliases={3: 0},
        compiler_params=_cparams(("parallel", "parallel")),
    )(dpre, dpre, conv_w, into)


def _expand_matrix():
    e = np.zeros((LANES, SSM_GROUP_WIDTH), np.float32)
    for h in range(HEADS_PER_SSM_GROUP):
        e[h, h * SSM_HEAD_DIM:(h + 1) * SSM_HEAD_DIM] = 1.0
    return jnp.asarray(e, BF16)


def _expand(t, e):
    return _dot(t.astype(BF16), e)


def _segsum(v, e):
    hi, lo = _split2(v)
    return _dot_nt(hi, e) + _dot_nt(lo, e)


def _tri_dot(tri, x):
    hi, mid, lo = _split3(x)
    return _dot(tri, hi) + _dot(tri, mid) + _dot(tri, lo)


def _ssd_common(dtp_ref, a_ref, dtb_ref, x_ref, e):
    li = lax.broadcasted_iota(jnp.int32, (CHUNK, CHUNK), 0)
    si = lax.broadcasted_iota(jnp.int32, (CHUNK, CHUNK), 1)
    causal = li >= si
    tril = causal.astype(BF16)
    raw = dtp_ref[...] + dtb_ref[...]
    dt = jnp.maximum(raw, 0.0) + jnp.log(1.0 + jnp.exp(-jnp.abs(raw)))
    head_lane = lax.broadcasted_iota(jnp.int32, (1, LANES), 1) < HEADS_PER_SSM_GROUP
    a = jnp.where(head_lane, -jnp.exp(a_ref[...]), 0.0)
    a_cum = _tri_dot(tril, dt * a)
    a_cum_t = a_cum.T
    e_a = jnp.exp(a_cum)
    to_end = jnp.exp(a_cum[CHUNK - 1:CHUNK, :] - a_cum)
    x = x_ref[...].astype(F32)
    dt_e = _expand(dt, e)
    return dict(causal=causal, raw=raw, dt=dt, a=a, a_cum=a_cum, a_cum_t=a_cum_t, e_a=e_a,
                to_end=to_end, x=x, dt_e=dt_e, xdt=x * dt_e, e_a_e=_expand(e_a, e),
                to_end_e=_expand(to_end, e))


def _decay(q, h):
    seg = q["a_cum"][:, h:h + 1] - q["a_cum_t"][h:h + 1, :]
    return jnp.exp(jnp.where(q["causal"], seg, -jnp.inf))


def _ssd_specs(ng, d_inner, rev, nc):
    cidx = (lambda i: nc - 1 - i) if rev else (lambda i: i)
    boff = d_inner // D_STATE
    return dict(
        xs=pl.BlockSpec((CHUNK, SSM_GROUP_WIDTH), lambda g, i: (cidx(i), g)),
        b=pl.BlockSpec((CHUNK, D_STATE), lambda g, i: (cidx(i), boff + g)),
        c=pl.BlockSpec((CHUNK, D_STATE), lambda g, i: (cidx(i), boff + ng + g)),
        dtp=pl.BlockSpec((None, CHUNK, LANES), lambda g, i: (g, cidx(i), 0)),
        vec=pl.BlockSpec((None, 1, LANES), lambda g, i: (g, 0, 0)),
        wide=pl.BlockSpec((None, 1, SSM_GROUP_WIDTH), lambda g, i: (g, 0, 0)),
        e=pl.BlockSpec((LANES, SSM_GROUP_WIDTH), lambda g, i: (0, 0)),
        st=pl.BlockSpec((None, None, D_STATE, SSM_GROUP_WIDTH), lambda g, i: (g, cidx(i), 0, 0)),
        tok=pl.BlockSpec((CHUNK, SSM_GROUP_WIDTH), lambda g, i: (cidx(i), g)),
        bc_out=pl.BlockSpec((CHUNK, D_STATE), lambda g, i: (cidx(i), g)),
    )


def _ssd_fwd(xbc, dtp, a_pad, dtb_pad, dsk_e, e, d_inner):
    s = xbc.shape[0]
    ng = d_inner // SSM_GROUP_WIDTH
    nc = s // CHUNK

    def body(x_ref, b_ref, c_ref, dtp_ref, a_ref, dtb_ref, dsk_ref, e_ref, y_ref, st_ref, state):
        lane = lax.broadcasted_iota(jnp.int32, (CHUNK, LANES), 1)
        @pl.when(pl.program_id(1) == 0)
        def _():
            state[...] = jnp.zeros_like(state)

        ev = e_ref[...]
        q = _ssd_common(dtp_ref, a_ref, dtb_ref, x_ref, ev)
        bm, cm = b_ref[...], c_ref[...]
        cb = _dot_nt(cm, bm)
        s0 = state[...]
        st_ref[...] = s0
        y = _dot(cm, s0.astype(BF16)) * q["e_a_e"] + dsk_ref[...] * q["x"]
        xdt = q["xdt"]
        left = lane[:, :] < SSM_HEAD_DIM
        for j in range(HEADS_PER_SSM_GROUP // 2):
            sl = slice(j * LANES, (j + 1) * LANES)
            x2 = xdt[:, sl]
            m0 = (cb * _decay(q, 2 * j)).astype(BF16)
            m1 = (cb * _decay(q, 2 * j + 1)).astype(BF16)
            mcat = jnp.concatenate([m0, m1], axis=1)
            xbd = jnp.concatenate([jnp.where(left, x2, 0.0), jnp.where(left, 0.0, x2)], axis=0).astype(BF16)
            y_ref[:, sl] = (y[:, sl] + _dot(mcat, xbd)).astype(BF16)
        state[...] = s0 * q["e_a_e"][CHUNK - 1:CHUNK, :] + _dot_tn(bm, (q["to_end_e"] * xdt).astype(BF16))

    sp = _ssd_specs(ng, d_inner, False, nc)
    return pl.pallas_call(
        body, name="ssd_fwd", grid=(ng, nc),
        in_specs=[sp["xs"], sp["b"], sp["c"], sp["dtp"], sp["vec"], sp["vec"], sp["wide"], sp["e"]],
        out_specs=[sp["tok"], sp["st"]],
        out_shape=[jax.ShapeDtypeStruct((s, d_inner), BF16),
                   jax.ShapeDtypeStruct((ng, nc, D_STATE, SSM_GROUP_WIDTH), F32)],
        scratch_shapes=[pltpu.VMEM((D_STATE, SSM_GROUP_WIDTH), F32)],
        compiler_params=_cparams(("parallel", "arbitrary")),
    )(xbc, xbc, xbc, dtp, a_pad, dtb_pad, dsk_e, e)


def _ssd_bwd(xbc, dtp, a_pad, dtb_pad, dsk_e, e, states, dy, d_inner):
    s = xbc.shape[0]
    ng = d_inner // SSM_GROUP_WIDTH
    nc = s // CHUNK

    def body(x_ref, b_ref, c_ref, dtp_ref, a_ref, dtb_ref, dsk_ref, e_ref, st_ref, dy_ref,
             dx_ref, db_ref, dc_ref, ddt_ref, da_ref, ddtb_ref, dd_ref, dstate):
        lane = lax.broadcasted_iota(jnp.int32, (CHUNK, LANES), 1)
        sub = lax.broadcasted_iota(jnp.int32, (CHUNK, LANES), 0)
        @pl.when(pl.program_id(1) == 0)
        def _():
            dstate[...] = jnp.zeros_like(dstate)
            da_ref[...] = jnp.zeros_like(da_ref)
            ddtb_ref[...] = jnp.zeros_like(ddtb_ref)
            dd_ref[...] = jnp.zeros_like(dd_ref)

        ev = e_ref[...]
        q = _ssd_common(dtp_ref, a_ref, dtb_ref, x_ref, ev)
        bm, cm = b_ref[...], c_ref[...]
        cb = _dot_nt(cm, bm)
        x, xdt, e_a_e, to_end_e = q["x"], q["xdt"], q["e_a_e"], q["to_end_e"]
        s0 = st_ref[...]
        s0b = s0.astype(BF16)
        ds1 = dstate[...]
        ds1b = ds1.astype(BF16)
        dy = dy_ref[...].astype(F32)
        e_last_e = e_a_e[CHUNK - 1:CHUNK, :]

        dye = dy * e_a_e
        dyeb = dye.astype(BF16)
        cs0 = _dot(cm, s0b)
        dc = _dot_nt(dyeb, s0b)
        dstate[...] = e_last_e * ds1 + _dot_tn(cm, dyeb)
        da_col = _segsum(dye * cs0, ev)

        gmat = _dot(bm, ds1b)
        dxdt = to_end_e * gmat
        dte = _segsum(xdt * gmat, ev) * q["to_end"]
        db = _dot_nt((to_end_e * xdt).astype(BF16), ds1b)
        da_col = da_col - dte
        last_row = (jnp.sum(dte, axis=0, keepdims=True)
                    + q["e_a"][CHUNK - 1:CHUNK, :] * jnp.sum(_segsum(s0 * ds1, ev), axis=0, keepdims=True))

        left = lane < SSM_HEAD_DIM
        dcb = jnp.zeros((CHUNK, CHUNK), F32)
        row_acc = jnp.zeros((CHUNK, LANES), F32)
        for j in range(HEADS_PER_SSM_GROUP // 2):
            sl = slice(j * LANES, (j + 1) * LANES)
            x2 = xdt[:, sl].astype(BF16)
            dy2 = dy[:, sl]
            dyl = jnp.where(left, dy2, 0.0).astype(BF16)
            dyr = jnp.where(left, 0.0, dy2).astype(BF16)
            ms = []
            for hh, dyh in ((0, dyl), (1, dyr)):
                h = 2 * j + hh
                dec = _decay(q, h)
                m = cb * dec
                dm = _dot_nt(dyh, x2)
                dcb = dcb + dm * dec
                dseg = dm * m
                da_col = da_col + jnp.where(lane == h, jnp.sum(dseg, axis=1, keepdims=True), 0.0)
                row_acc = row_acc + jnp.where(sub == h, jnp.sum(dseg, axis=0, keepdims=True), 0.0)
                ms.append(m.astype(BF16))
            mst = jnp.concatenate(ms, axis=0)
            dyst = jnp.concatenate([dyl, dyr], axis=0)
            d2 = dxdt[:, sl] + _dot_tn(mst, dyst)
            dx_ref[:, sl] = (d2 * q["dt_e"][:, sl] + dsk_ref[:, sl] * dy2).astype(BF16)
            dxdt_x = d2 * x[:, sl]
            if j == 0:
                parts = [dxdt_x]
            else:
                parts.append(dxdt_x)
        dcbb = dcb.astype(BF16)
        dc_ref[...] = (dc + _dot(dcbb, bm)).astype(BF16)
        db_ref[...] = (db + _dot_tn(dcbb, cm)).astype(BF16)

        d_a = da_col - row_acc.T + jnp.where(sub == CHUNK - 1, last_row, 0.0)
        triu = (lax.broadcasted_iota(jnp.int32, (CHUNK, CHUNK), 1)
                >= lax.broadcasted_iota(jnp.int32, (CHUNK, CHUNK), 0)).astype(BF16)
        d_dta = _tri_dot(triu, d_a)
        ddt = d_dta * q["a"] + _segsum(jnp.concatenate(parts, axis=1), ev)
        ddt_raw = ddt * _sigmoid(q["raw"])
        ddt_ref[...] = ddt_raw
        da_ref[...] += jnp.sum(d_dta * q["dt"], axis=0, keepdims=True) * q["a"]
        ddtb_ref[...] += jnp.sum(ddt_raw, axis=0, keepdims=True)
        dd_ref[...] += jnp.sum(dy * x, axis=0, keepdims=True)

    sp = _ssd_specs(ng, d_inner, True, nc)
    return pl.pallas_call(
        body, name="ssd_bwd", grid=(ng, nc),
        in_specs=[sp["xs"], sp["b"], sp["c"], sp["dtp"], sp["vec"], sp["vec"], sp["wide"], sp["e"],
                  sp["st"], sp["tok"]],
        out_specs=[sp["tok"], sp["bc_out"], sp["bc_out"], sp["dtp"], sp["vec"], sp["vec"], sp["wide"]],
        out_shape=[jax.ShapeDtypeStruct(xbc.shape, BF16),
                   jax.ShapeDtypeStruct((s, ng * D_STATE), BF16),
                   jax.ShapeDtypeStruct((s, ng * D_STATE), BF16),
                   jax.ShapeDtypeStruct((ng, s, LANES), F32),
                   jax.ShapeDtypeStruct((ng, 1, LANES), F32),
                   jax.ShapeDtypeStruct((ng, 1, LANES), F32),
                   jax.ShapeDtypeStruct((ng, 1, SSM_GROUP_WIDTH), F32)],
        scratch_shapes=[pltpu.VMEM((D_STATE, SSM_GROUP_WIDTH), F32)],
        compiler_params=_cparams(("parallel", "arbitrary")),
    )(xbc, xbc, xbc, dtp, a_pad, dtb_pad, dsk_e, e, states, dy)


def _gate_norm_fwd(y, pzx, norm_w):
    s, di = y.shape
    ng = di // SSM_GROUP_WIDTH
    tm = _pick(s, (512, 256, 128))

    def body(y_ref, z_ref, w_ref, o_ref):
        z = z_ref[...].astype(F32)
        y2 = y_ref[...].astype(F32) * (z * _sigmoid(z))
        r = lax.rsqrt(jnp.mean(y2 * y2, axis=1, keepdims=True) + RMS_EPS)
        o_ref[...] = (y2 * r * w_ref[...]).astype(BF16)

    blk = pl.BlockSpec((tm, SSM_GROUP_WIDTH), lambda i, g: (i, g))
    return pl.pallas_call(
        body, name="gate_norm_fwd", grid=(s // tm, ng),
        in_specs=[blk, blk, pl.BlockSpec((1, SSM_GROUP_WIDTH), lambda i, g: (0, g))],
        out_specs=blk, out_shape=jax.ShapeDtypeStruct((s, di), BF16),
        compiler_params=_cparams(("parallel", "parallel")),
    )(y, pzx, norm_w)


def _gate_norm_bwd(dy3, y, pzx, norm_w):
    s, di = y.shape
    ng = di // SSM_GROUP_WIDTH
    tm = _pick(s, (256, 128))

    def body(d_ref, y_ref, z_ref, w_ref, dy_ref, dz_ref, dw_ref):
        @pl.when(pl.program_id(1) == 0)
        def _():
            dw_ref[...] = jnp.zeros_like(dw_ref)

        z = z_ref[...].astype(F32)
        yv = y_ref[...].astype(F32)
        sg = _sigmoid(z)
        sz = z * sg
        y2 = yv * sz
        r = lax.rsqrt(jnp.mean(y2 * y2, axis=1, keepdims=True) + RMS_EPS)
        nrm = y2 * r
        d3 = d_ref[...].astype(F32)
        dw_ref[...] += jnp.sum(d3 * nrm, axis=0, keepdims=True)
        dn = d3 * w_ref[...]
        dy2 = r * (dn - nrm * jnp.mean(dn * nrm, axis=1, keepdims=True))
        dy_ref[...] = (dy2 * sz).astype(BF16)
        dz_ref[...] = (dy2 * yv * (sg * (1.0 + z * (1.0 - sg)))).astype(BF16)

    blk = pl.BlockSpec((tm, SSM_GROUP_WIDTH), lambda g, i: (i, g))
    vec = pl.BlockSpec((1, SSM_GROUP_WIDTH), lambda g, i: (0, g))
    return pl.pallas_call(
        body, name="gate_norm_bwd", grid=(ng, s // tm),
        in_specs=[blk, blk, blk, vec], out_specs=[blk, blk, vec],
        out_shape=[jax.ShapeDtypeStruct((s, di), BF16), jax.ShapeDtypeStruct(pzx.shape, BF16),
                   jax.ShapeDtypeStruct((1, di), F32)],
        compiler_params=_cparams(("parallel", "arbitrary")),
    )(dy3, y, pzx, norm_w)


_ANY = pl.BlockSpec(memory_space=pl.ANY)


def _place():
    x, y, c = lax.axis_index("x"), lax.axis_index("y"), lax.axis_index("c")
    chips = [(1 - x, y), (x, 1 - y), (1 - x, 1 - y)]
    return x, y, c, chips


def _cast_to_slot(x, kvec, name, after=None):
    r, cn = x.shape
    tr = _rows_per_block(r, cn)
    extra = [] if after is None else [after]

    def body(k_ref, x_ref, *rest):
        rest[-1][...] = x_ref[...].astype(BF16)

    grid_spec = pltpu.PrefetchScalarGridSpec(
        num_scalar_prefetch=1, grid=(r // tr,),
        in_specs=[pl.BlockSpec((tr, cn), lambda i, k: (i, 0))] + [_ANY] * len(extra),
        out_specs=pl.BlockSpec((None, tr, cn), lambda i, k: (k[0], i, 0)))
    return pl.pallas_call(
        body, name=name, grid_spec=grid_spec, out_shape=jax.ShapeDtypeStruct((N_CHIPS, r, cn), BF16),
        compiler_params=_cparams(("parallel",)),
    )(kvec, x, *extra)


_HBM_SPEC = pl.BlockSpec(memory_space=pltpu.HBM)
_SEM_SPEC = pl.BlockSpec(memory_space=pltpu.SEMAPHORE)
_VMEM_SPEC = pl.BlockSpec(memory_space=pltpu.VMEM)
_EFFECT = pltpu.SideEffectType.DATAFLOW_SIDE_EFFECTING
_TOKEN = jax.ShapeDtypeStruct((8, LANES), F32)


def _hbm(a):
    return pltpu.with_memory_space_constraint(a, pltpu.HBM)


_NEIGHBOURS, _DIAGONAL, _ALL_CHIPS = (0, 1), (2,), (0, 1, 2)


def _gather_copies(bufs, refs, send_sems, recv_sems, forward, arrivals=True, which=_ALL_CHIPS):
    x, y, c, chips = _place()
    k = 2 * x + y
    out, arrive = [], []
    for w, ref in enumerate(refs):
        hr = bufs[w].shape[1] // 2
        for j, (cx, cy) in enumerate(chips):
            if j not in which:
                continue
            kj = 2 * cx + cy
            slot_out, slot_in, half_in = (kj, kj, 1 - c) if forward else (k, kj, c)
            to = (x, y, 1 - c) if forward else (cx, cy, c)
            src = ref.at[slot_out, pl.ds(c * hr, hr)]
            land = ref.at[slot_in, pl.ds(half_in * hr, hr)]
            out.append(pltpu.make_async_remote_copy(
                src_ref=src, dst_ref=src, send_sem=send_sems.at[3 * w + j], recv_sem=recv_sems.at[3 * w + j],
                device_id=to, device_id_type=MESH))
            if arrivals:
                arrive.append(pltpu.make_async_remote_copy(
                    src_ref=land, dst_ref=land, send_sem=send_sems.at[3 * w + j], recv_sem=recv_sems.at[3 * w + j],
                    device_id=to, device_id_type=MESH))
    return out, arrive


def _gather_start(bufs, forward, name, after=None, which=_ALL_CHIPS):
    n = len(bufs)
    extra = [] if after is None else [after]

    def body(*refs):
        ins = refs[:n]
        send_sems, recv_sems = refs[n + len(extra)], refs[n + len(extra) + 1]
        token = refs[-1]
        out, _ = _gather_copies(bufs, ins, send_sems, recv_sems, forward, arrivals=False, which=which)
        for cp in out:
            cp.start()
        token[...] = jnp.zeros_like(token)

    res = pl.pallas_call(
        body, name=name,
        out_shape=(pltpu.SemaphoreType.DMA((3 * n,)), pltpu.SemaphoreType.DMA((3 * n,)))
        + tuple(pltpu.HBM(b.shape, b.dtype) for b in bufs) + (_TOKEN,),
        in_specs=(_HBM_SPEC,) * n + (_ANY,) * len(extra),
        out_specs=(_SEM_SPEC, _SEM_SPEC) + (_HBM_SPEC,) * n + (_VMEM_SPEC,),
        input_output_aliases={w: 2 + w for w in range(n)},
        compiler_params=pltpu.CompilerParams(has_side_effects=_EFFECT),
    )(*[_hbm(b) for b in bufs], *extra)
    return res[0], res[1], list(res[2:2 + n]), res[-1]


def _gather_wait(bufs, send_sems, recv_sems, after, forward, name, which=_ALL_CHIPS):
    n = len(bufs)

    def body(*refs):
        ins = refs[:n]
        send_sems, recv_sems = refs[n], refs[n + 1]
        out, arrive = _gather_copies(bufs, ins, send_sems, recv_sems, forward, which=which)
        for cp in out:
            cp.wait_send()
        for cp in arrive:
            cp.wait_recv()

    res = pl.pallas_call(
        body, name=name,
        out_shape=tuple(pltpu.HBM(b.shape, b.dtype) for b in bufs),
        in_specs=(_HBM_SPEC,) * n + (_SEM_SPEC, _SEM_SPEC, _ANY), out_specs=(_HBM_SPEC,) * n,
        input_output_aliases={w: w for w in range(n)},
        compiler_params=pltpu.CompilerParams(has_side_effects=_EFFECT),
    )(*bufs, send_sems, recv_sems, after)
    return list(res)


def _swap_copy(g_ref, land_ref, send_sems, recv_sems):
    x, y, c, _ = _place()
    hr = g_ref.shape[1] // 2
    return pltpu.make_async_remote_copy(
        src_ref=g_ref.at[:, pl.ds((1 - c) * hr, hr)], dst_ref=land_ref, send_sem=send_sems.at[0],
        recv_sem=recv_sems.at[0], device_id=(x, y, 1 - c), device_id_type=MESH)


def _swap_start(g4, name):
    ns, r, cn = g4.shape

    def body(g_ref, land_ref, send_sems, recv_sems, g_thru, land_thru, token):
        _swap_copy(g_ref, land_ref, send_sems, recv_sems).start()
        token[...] = jnp.zeros_like(token)

    return pl.pallas_call(
        body, name=name,
        out_shape=(pltpu.SemaphoreType.DMA((1,)), pltpu.SemaphoreType.DMA((1,)),
                   pltpu.HBM(g4.shape, g4.dtype), pltpu.HBM((ns, r // 2, cn), g4.dtype), _TOKEN),
        in_specs=(_HBM_SPEC, _HBM_SPEC), out_specs=(_SEM_SPEC, _SEM_SPEC, _HBM_SPEC, _HBM_SPEC, _VMEM_SPEC),
        input_output_aliases={0: 2, 1: 3},
        compiler_params=pltpu.CompilerParams(has_side_effects=_EFFECT),
    )(_hbm(g4), _hbm(lax.empty((ns, r // 2, cn), g4.dtype)))


def _swap_wait(send_sems, recv_sems, g_thru, land_thru, after, name):
    def body(g_ref, land_ref, send_sems, recv_sems, after_ref, g_out, land_out):
        cp = _swap_copy(g_ref, land_ref, send_sems, recv_sems)
        cp.wait_send()
        cp.wait_recv()

    return pl.pallas_call(
        body, name=name,
        out_shape=(pltpu.HBM(g_thru.shape, g_thru.dtype), pltpu.HBM(land_thru.shape, land_thru.dtype)),
        in_specs=(_HBM_SPEC, _HBM_SPEC, _SEM_SPEC, _SEM_SPEC, _ANY), out_specs=(_HBM_SPEC, _HBM_SPEC),
        input_output_aliases={0: 0, 1: 1},
        compiler_params=pltpu.CompilerParams(has_side_effects=_EFFECT),
    )(g_thru, land_thru, send_sems, recv_sems, after)


def _join_copy(f_ref, send_sems, recv_sems, half):
    x, y, c, _ = _place()
    hr = f_ref.shape[0] // 2
    rows = f_ref.at[pl.ds(half * hr, hr)]
    return pltpu.make_async_remote_copy(
        src_ref=rows, dst_ref=rows, send_sem=send_sems.at[0], recv_sem=recv_sems.at[0],
        device_id=(x, y, 1 - c), device_id_type=MESH)


def _join_start(f, name):
    def body(f_ref, send_sems, recv_sems, f_thru, token):
        _join_copy(f_ref, send_sems, recv_sems, lax.axis_index("c")).start()
        token[...] = jnp.zeros_like(token)

    return pl.pallas_call(
        body, name=name,
        out_shape=(pltpu.SemaphoreType.DMA((1,)), pltpu.SemaphoreType.DMA((1,)), pltpu.HBM(f.shape, f.dtype), _TOKEN),
        in_specs=(_HBM_SPEC,), out_specs=(_SEM_SPEC, _SEM_SPEC, _HBM_SPEC, _VMEM_SPEC),
        input_output_aliases={0: 2},
        compiler_params=pltpu.CompilerParams(has_side_effects=_EFFECT),
    )(_hbm(f))


def _join_wait(send_sems, recv_sems, f_thru, after, name):
    def body(f_ref, send_sems, recv_sems, after_ref, f_out):
        c = lax.axis_index("c")
        _join_copy(f_ref, send_sems, recv_sems, c).wait_send()
        _join_copy(f_ref, send_sems, recv_sems, 1 - c).wait_recv()

    return pl.pallas_call(
        body, name=name, out_shape=pltpu.HBM(f_thru.shape, f_thru.dtype),
        in_specs=(_HBM_SPEC, _SEM_SPEC, _SEM_SPEC, _ANY), out_specs=_HBM_SPEC,
        input_output_aliases={0: 0},
        compiler_params=pltpu.CompilerParams(has_side_effects=_EFFECT),
    )(f_thru, send_sems, recv_sems, after)


def _place_bc(dxw, dbm, dcm, col_off):
    s, w = dbm.shape
    tm = _pick(s, (512, 256, 128))
    cb = col_off // w

    def body(b_ref, c_ref, x_ref, o_ref):
        o_ref[...] = jnp.where(pl.program_id(1) == 0, b_ref[...], c_ref[...])

    blk = pl.BlockSpec((tm, w), lambda i, j: (i, 0))
    return pl.pallas_call(
        body, name="place_db_dc", grid=(s // tm, 2), in_specs=[blk, blk, _ANY],
        out_specs=pl.BlockSpec((tm, w), lambda i, j: (i, cb + j)),
        out_shape=jax.ShapeDtypeStruct(dxw.shape, dxw.dtype), input_output_aliases={2: 0},
        compiler_params=_cparams(("parallel", "parallel")),
    )(dbm, dcm, dxw)


def _scatter_copies(t_ref, land_ref, send_sems, recv_sems, arrivals=True):
    x, y, c, chips = _place()
    k = 2 * x + y
    out, arrive = [], []
    for j, (cx, cy) in enumerate(chips):
        kj = 2 * cx + cy
        out.append(pltpu.make_async_remote_copy(
            src_ref=t_ref.at[kj], dst_ref=land_ref.at[k], send_sem=send_sems.at[j], recv_sem=recv_sems.at[j],
            device_id=(cx, cy, c), device_id_type=MESH))
        if arrivals:
            arrive.append(pltpu.make_async_remote_copy(
                src_ref=t_ref.at[kj], dst_ref=land_ref.at[kj], send_sem=send_sems.at[j], recv_sem=recv_sems.at[j],
                device_id=(cx, cy, c), device_id_type=MESH))
    return out, arrive


def _scatter_start(t, name):
    def body(t_ref, land_ref, send_sems, recv_sems, t_thru, land_thru, token):
        out, _ = _scatter_copies(t_ref, land_ref, send_sems, recv_sems, arrivals=False)
        for cp in out:
            cp.start()
        token[...] = jnp.zeros_like(token)

    return pl.pallas_call(
        body, name=name,
        out_shape=(pltpu.SemaphoreType.DMA((3,)), pltpu.SemaphoreType.DMA((3,)),
                   pltpu.HBM(t.shape, t.dtype), pltpu.HBM(t.shape, t.dtype), _TOKEN),
        in_specs=(_HBM_SPEC, _HBM_SPEC), out_specs=(_SEM_SPEC, _SEM_SPEC, _HBM_SPEC, _HBM_SPEC, _VMEM_SPEC),
        input_output_aliases={0: 2, 1: 3},
        compiler_params=pltpu.CompilerParams(has_side_effects=_EFFECT),
    )(_hbm(t), _hbm(lax.empty(t.shape, t.dtype)))


def _scatter_wait(send_sems, recv_sems, t_thru, land_thru, after, name):
    def body(t_ref, land_ref, send_sems, recv_sems, after_ref, t_out, land_out):
        out, arrive = _scatter_copies(t_ref, land_ref, send_sems, recv_sems)
        for cp in out:
            cp.wait_send()
        for cp in arrive:
            cp.wait_recv()

    return pl.pallas_call(
        body, name=name,
        out_shape=(pltpu.HBM(t_thru.shape, t_thru.dtype), pltpu.HBM(land_thru.shape, land_thru.dtype)),
        in_specs=(_HBM_SPEC, _HBM_SPEC, _SEM_SPEC, _SEM_SPEC, _ANY), out_specs=(_HBM_SPEC, _HBM_SPEC),
        input_output_aliases={0: 0, 1: 1},
        compiler_params=pltpu.CompilerParams(has_side_effects=_EFFECT),
    )(t_thru, land_thru, send_sems, recv_sems, after)


def _all_gather_small(v, reduce, name):
    r, l = v.shape

    def body(v_ref, o_ref, *rest):
        if reduce:
            buf, send_sems, recv_sems = rest
        else:
            buf = o_ref
            send_sems, recv_sems = rest
        x, y, c, _ = _place()
        me = 4 * x + 2 * y + c
        buf[me] = v_ref[...]
        cps = []
        for d in range(1, N_DEV):
            peer = (x if d & 4 == 0 else 1 - x, y if d & 2 == 0 else 1 - y, c if d & 1 == 0 else 1 - c)
            cp = pltpu.make_async_remote_copy(
                src_ref=v_ref, dst_ref=buf.at[me], send_sem=send_sems.at[d - 1], recv_sem=recv_sems.at[d - 1],
                device_id=peer, device_id_type=MESH)
            cp.start()
            cps.append((cp, peer))
        for d, (cp, (px, py, pc)) in enumerate(cps, start=1):
            pltpu.make_async_remote_copy(
                src_ref=v_ref, dst_ref=buf.at[4 * px + 2 * py + pc], send_sem=send_sems.at[d - 1],
                recv_sem=recv_sems.at[d - 1], device_id=(px, py, pc), device_id_type=MESH).wait_recv()
        for cp, _ in cps:
            cp.wait_send()
        if reduce:
            acc = buf[0]
            for i in range(1, N_DEV):
                acc = acc + buf[i]
            o_ref[...] = acc

    vm = pl.BlockSpec(memory_space=pltpu.VMEM)
    out_shape = jax.ShapeDtypeStruct((r, l) if reduce else (N_DEV, r, l), F32)
    scratch = ([pltpu.VMEM((N_DEV, r, l), F32)] if reduce else []) + [
        pltpu.SemaphoreType.DMA((N_DEV - 1,)), pltpu.SemaphoreType.DMA((N_DEV - 1,))]
    return pl.pallas_call(
        body, name=name, in_specs=[vm], out_specs=vm, out_shape=out_shape, scratch_shapes=scratch,
    )(v)


_BLOCK_BYTES = 3 * 512 * 1024


def _rows_per_block(r, cn, itemsize=4):
    best = 8
    for t in range(8, r + 1, 8):
        if r % t == 0 and t * cn * itemsize <= _BLOCK_BYTES:
            best = t
    return best


def _add_sibling_half(g4, recv, cvec, name):
    ns, r, cn = g4.shape
    hr = r // 2
    tr = _rows_per_block(hr, cn, itemsize=2)
    nrb = hr // tr

    def body(c_ref, a_ref, b_ref, o_ref):
        o_ref[...] = (a_ref[...].astype(F32) + b_ref[...].astype(F32)).astype(o_ref.dtype)

    grid_spec = pltpu.PrefetchScalarGridSpec(
        num_scalar_prefetch=1, grid=(ns, nrb),
        in_specs=[pl.BlockSpec((None, tr, cn), lambda j, i, c: (j, c[0] * nrb + i, 0)),
                  pl.BlockSpec((None, tr, cn), lambda j, i, c: (j, i, 0))],
        out_specs=pl.BlockSpec((None, tr, cn), lambda j, i, c: (j, i, 0)))
    return pl.pallas_call(
        body, name=name, grid_spec=grid_spec, out_shape=jax.ShapeDtypeStruct((ns, hr, cn), BF16),
        compiler_params=_cparams(("parallel", "parallel")),
    )(cvec, g4, recv)


def _sum_chips(r4, t4, kvec, cvec, name):
    ns, hr, cn = r4.shape
    tr = _rows_per_block(hr, cn, itemsize=2)
    nrb = hr // tr

    def body(k_ref, c_ref, r_ref, t_ref, o_ref):
        acc = t_ref[...].astype(F32)
        for dlt in range(1, ns):
            acc = acc + r_ref[(k_ref[0] + dlt) % ns].astype(F32)
        o_ref[...] = acc

    grid_spec = pltpu.PrefetchScalarGridSpec(
        num_scalar_prefetch=2, grid=(nrb,),
        in_specs=[pl.BlockSpec((ns, tr, cn), lambda i, k, c: (0, i, 0)),
                  pl.BlockSpec((None, tr, cn), lambda i, k, c: (k[0], i, 0))],
        out_specs=pl.BlockSpec((tr, cn), lambda i, k, c: (c[0] * nrb + i, 0)))
    return pl.pallas_call(
        body, name=name, grid_spec=grid_spec, out_shape=jax.ShapeDtypeStruct((2 * hr, cn), F32),
        compiler_params=_cparams(("parallel",)),
    )(kvec, cvec, r4, t4)


def _adamw(w, g, m, v, name):
    r, cn = w.shape
    tr = _rows_per_block(r, cn)
    c1 = 1.0 - ADAM_B1 ** ADAM_STEP
    c2 = 1.0 - ADAM_B2 ** ADAM_STEP

    def body(w_ref, g_ref, m_ref, v_ref, go_ref, d_ref, mo_ref, vo_ref):
        gv = g_ref[...]
        mn = ADAM_B1 * m_ref[...] + (1.0 - ADAM_B1) * gv
        vn = ADAM_B2 * v_ref[...] + (1.0 - ADAM_B2) * (gv * gv)
        go_ref[...] = gv
        mo_ref[...] = mn
        vo_ref[...] = vn
        d_ref[...] = -ADAM_LR * ((mn / c1) / (jnp.sqrt(vn / c2) + ADAM_EPS) + ADAM_WD * w_ref[...])

    spec = pl.BlockSpec((tr, cn), lambda i: (i, 0))
    return pl.pallas_call(
        body, name=name, grid=(r // tr,), in_specs=[spec] * 4, out_specs=[spec] * 4,
        out_shape=[jax.ShapeDtypeStruct((r, cn), F32)] * 4,
        compiler_params=_cparams(("parallel",)),
    )(w, g, m, v)


def _pack(arrs):
    flat = jnp.concatenate([a.reshape(-1).astype(F32) for a in arrs])
    n = flat.shape[0]
    tot = -(-n // (8 * LANES)) * (8 * LANES)
    return jnp.pad(flat, (0, tot - n)).reshape(tot // LANES, LANES)


def _unpack(packed, shapes):
    flat = packed.reshape(-1)
    out, off = [], 0
    for shp in shapes:
        sz = int(np.prod(shp))
        out.append(flat[off:off + sz].reshape(shp))
        off += sz
    return out


class _LocalExchange:
    def __init__(self, wa4, woa4, ws4, wos4):
        self.l0, self.ssm = [wa4, woa4], [ws4, wos4]
        self.grads = {}

    def l0_begin(self):
        return self.l0[0], None

    def l0_neighbours(self, after):
        return self.l0[0]

    def l0_diagonal(self, after):
        return self.l0

    def ssm_gather_mid(self, after):
        return None

    def ssm_gather_end(self, after):
        return self.ssm

    def grad_begin(self, name, g4):
        self.grads[name] = g4
        return None

    def grad_mid(self, name, after):
        return None

    def grad_sync(self, name, after):
        return None

    def small_grads(self, small_full):
        self.small = small_full
        return None


class _Exchange:
    def __init__(self, kvec, cvec, l0_bufs, after):
        self.kvec, self.cvec, self.l0_bufs, self.bufs, self.after = kvec, cvec, l0_bufs, None, after
        self.pending, self.summed, self.last_token, self.l0_token, self.swaps, self.joins = {}, {}, None, None, {}, {}

    def l0_begin(self):
        if self.l0_token is None:
            self.l0_send, self.l0_recv, self.l0_bufs, self.l0_token = _gather_start(
                self.l0_bufs, False, "l0_gather_ici_start", self.after)
        return self.l0_bufs[0], (self.l0_token if self.bufs is None else self.sems[3])

    def _l0_step(self, which, tag, after):
        bufs = _gather_wait(self.l0_bufs, self.l0_send, self.l0_recv, after, False, "l0_gather_ici_wait_" + tag, which)
        send, recv, bufs, token = _gather_start(bufs, True, "l0_gather_fwd_start_" + tag, None, which)
        self.l0_bufs = _gather_wait(bufs, send, recv, token, True, "l0_gather_fwd_wait_" + tag, which)
        return self.l0_bufs

    def l0_neighbours(self, after):
        return self._l0_step(_NEIGHBOURS, "nb", after)[0]

    def l0_diagonal(self, after):
        return self._l0_step(_DIAGONAL, "diag", after)

    def ssm_gather_start(self, ssm_bufs):
        self.sems = _gather_start(ssm_bufs, False, "ssm_gather_ici_start", self.l0_token)
        self.bufs = self.sems[2]
        return self.sems[3]

    def ssm_gather_mid(self, after):
        bufs = _gather_wait(self.bufs, self.sems[0], self.sems[1], after, False, "ssm_gather_ici_wait")
        self.sems = _gather_start(bufs, True, "ssm_gather_fwd_start")
        self.bufs = self.sems[2]
        return self.sems[3]

    def ssm_gather_end(self, after):
        return _gather_wait(self.bufs, self.sems[0], self.sems[1], after, True, "ssm_gather_fwd_wait")

    def small_grads(self, small_full):
        packed = _all_gather_small(_pack(small_full), True, "reduce_small_grads")
        self.small = _unpack(packed, [t.shape for t in small_full])
        return packed

    def grad_begin(self, name, g4):
        self.swaps[name] = _swap_start(g4, "swap_start_" + name)
        return self.swaps[name][4]

    def grad_mid(self, name, after):
        send_sem, recv_sem, g_thru, land, _ = self.swaps.pop(name)
        g4, recv = _swap_wait(send_sem, recv_sem, g_thru, land, after, "swap_wait_" + name)
        return self._reduce(name, g4, recv)

    def _reduce(self, name, g4, recv):
        t = _add_sibling_half(g4, recv, self.cvec, "add_sibling_" + name)
        send_sems, recv_sems, t_thru, land, token = _scatter_start(t, "scatter_start_" + name)
        self.pending[name] = (send_sems, recv_sems, t_thru, land)
        self.last_token = token
        return token

    def grad_sync(self, name, after):
        t, land = _scatter_wait(*self.pending.pop(name), after, "scatter_wait_" + name)
        summed = _sum_chips(land, t, self.kvec, self.cvec, "sum_chips_" + name)
        self.joins[name] = _join_start(summed, "join_start_" + name)
        return self.joins[name][3]

    def grad_end(self, name, after):
        send_sem, recv_sem, f_thru, _ = self.joins.pop(name)
        return _join_wait(send_sem, recv_sem, f_thru, after, "join_wait_" + name)


def _tie(vec, token):
    return vec if token is None else vec + token[0:1, 0:1].reshape((1,) * vec.ndim).astype(vec.dtype)


def _local_step(x2, tgt, ex, kvec, conv_w_f, conv_b_f, norm_w_f, rel_bias, dt_bias, a_log, d_skip,
                ln_g, ln_b):
    s, d = x2.shape
    wa4, tok = ex.l0_begin()
    d_attn = wa4.shape[2] * N_CHIPS // 10
    hpg = d_attn // HEAD_DIM
    d_inner = norm_w_f.shape[1]
    ng = d_inner // SSM_GROUP_WIDTH
    n_heads = dt_bias.shape[1]
    conv_dim = conv_w_f.shape[1]
    assert n_heads == ng * HEADS_PER_SSM_GROUP and conv_dim == d_inner + 2 * ng * D_STATE

    x3 = _cast_x3(x2, "cast_x", tok)
    for g, (_, dil) in enumerate(ATTN_PATTERNS):
        if dil > 1:
            x3 = _class_copy(x3, g, dil, f"class_order_x_g{g}")
    xb = x3[0]
    buckets = _bucket_tiles()
    bias = _bias_expand(rel_bias, buckets, hpg)
    pa = _in_proj_shard(x3, wa4, kvec, 0, d_attn, "mm_in_attn_own", after=tok)
    wa4 = ex.l0_neighbours(pa)
    pa = _in_proj_shard(x3, wa4, kvec, 1, d_attn, "mm_in_attn_nb1", into=pa)
    pa = _in_proj_shard(x3, wa4, kvec, 2, d_attn, "mm_in_attn_nb2", into=pa)
    wa4, woa4 = ex.l0_diagonal(pa)
    pa = _in_proj_shard(x3, wa4, kvec, 3, d_attn, "mm_in_attn_diag", into=pa)
    og, lg = [], []
    for g, (_, dil) in enumerate(ATTN_PATTERNS):
        o_, l_ = _attn_fwd(pa, bias, g, dil, hpg)
        og.append(o_)
        lg.append(l_)
    o, lse, yat = _attn_combine(og, lg, pa, hpg)
    h0 = _mm_nn_sharded(yat, woa4, F32, "mm_out_attn", after=ex.ssm_gather_mid(yat))
    g0, b0, g1, b1 = ln_g[0:1], ln_b[0:1], ln_g[1:2], ln_b[1:2]
    xhat0, rstd0, x1b = _ln_fwd(x2, h0, g0, b0, "ln0_fwd")

    wst4, wos4 = ex.ssm_gather_end(x1b)
    wst = wst4.reshape(N_CHIPS * wst4.shape[1], d)
    nzx = d_inner + conv_dim
    wos = wos4.reshape(d_inner, d)
    pzx = _mm_nt(x1b, wst, BF16, "mm_in_ssm", n=nzx)
    dt_raw = _mm_nt(x1b, wst, F32, "mm_in_dt", n=n_heads, b_row_off=nzx)

    def pad_heads(t):
        t = t.reshape(t.shape[0], ng, HEADS_PER_SSM_GROUP).transpose(1, 0, 2)
        return jnp.pad(t, ((0, 0), (0, 0), (0, LANES - HEADS_PER_SSM_GROUP)))

    def unpad_heads(t):
        return t[:, :, :HEADS_PER_SSM_GROUP].transpose(1, 0, 2).reshape(t.shape[1], n_heads)

    dtp = pad_heads(dt_raw)
    alog_p, dtb_p = pad_heads(a_log), pad_heads(dt_bias)
    dsk_e = jnp.repeat(d_skip.reshape(ng, 1, HEADS_PER_SSM_GROUP), SSM_HEAD_DIM, axis=2)
    e = _expand_matrix()
    xbc = _conv_fwd(pzx, conv_w_f, conv_b_f, d_inner)
    y_ssd, states = _ssd_fwd(xbc, dtp, alog_p, dtb_p, dsk_e, e, d_inner)
    y3 = _gate_norm_fwd(y_ssd, pzx, norm_w_f)
    h1 = _mm_nn(y3, wos, F32, "mm_out_ssm")
    xhat1, rstd1, dy2, row_sq = _ln_fwd(xhat0, h1, g1, b1, "ln1_fwd_loss", affine_in=(g0, b0), target=tgt)
    loss_local = 0.5 * jnp.sum(row_sq) / d

    du1, du1b, dg1, db1 = _ln_bwd(dy2, xhat1, rstd1, g1, "ln1_bwd")
    dy3 = _mm_nt(du1b, wos, BF16, "mm_d_y3")
    g_wos = _mm_tn(y3, du1b, BF16, "mm_g_w_out_ssm").reshape(N_CHIPS, d_inner // N_CHIPS, d)
    norm_w_t = _tie(norm_w_f, ex.grad_begin("w_out_ssm", g_wos))
    dy_ssd, dz, d_nw = _gate_norm_bwd(dy3, y_ssd, pzx, norm_w_t)
    dsk_t = _tie(dsk_e, ex.grad_mid("w_out_ssm", dy_ssd))
    dxs, dbm, dcm, ddtp, d_alog, d_dtb, d_dsk = _ssd_bwd(xbc, dtp, alog_p, dtb_p, dsk_t, e, states, dy_ssd, d_inner)
    dpre, d_cw, d_cb = _conv_bwd_a(pzx, _place_bc(dxs, dbm, dcm, d_inner), conv_w_f, conv_b_f, d_inner)
    dpzx = _conv_bwd_b(dpre, conv_w_f, dz, d_inner)
    ddt_raw = unpad_heads(ddtp)
    t1 = _mm_nn(ddt_raw, wst, F32, "mm_d_x1_dt", b_row_off=nzx, add=du1, add_scale=DEEPNORM_ALPHA)
    dx1 = _mm_nn(dpzx, wst, F32, "mm_d_x1", add=t1)
    g0_t = _tie(g0, ex.grad_sync("w_out_ssm", dx1))
    g_wst = _mm_tn(dpzx, x1b, BF16, "mm_g_w_in_ssm", out_rows=wst.shape[0])
    g_wst = _mm_tn(ddt_raw, x1b, BF16, "mm_g_w_dt", out_rows=wst.shape[0], out_row_off=nzx, into=g_wst)
    g0_t = _tie(g0_t, ex.grad_begin("w_in_ssm", g_wst.reshape(wst4.shape)))

    du0, du0b, dg0, db0 = _ln_bwd(dx1, xhat0, rstd0, g0_t, "ln0_bwd")
    dyat = _mm_nt_sharded_k(du0b, woa4, BF16, "mm_d_yat")
    g_woa = _mm_tn(yat, du0b, BF16, "mm_g_w_out_attn", shard_cols=d // N_CHIPS)
    lse_t = _tie(lse, ex.grad_mid("w_in_ssm", g_woa))
    ex.grad_begin("w_out_attn", g_woa)
    do, delta, dgate = _attn_pre_bwd(dyat, o, pa, hpg)
    pieces, dbt = [], []
    for g, (_, dil) in enumerate(ATTN_PATTERNS):
        dq, dk, dv, db_ = _attn_bwd(pa, bias, do, lse_t, delta, g, dil, hpg)
        pieces += [dq, dk, dv]
        dbt.append(db_)
    dpa = jnp.concatenate(pieces + [dgate], axis=1)
    tok_woa = ex.grad_mid("w_out_attn", dpa)
    g_wa = _mm_tn(xb, dpa, BF16, "mm_g_w_in_attn", shard_cols=wa4.shape[2], after=tok_woa)
    ex.grad_begin("w_in_attn", g_wa)
    join_toks = [ex.grad_sync("w_in_ssm", g_wa), ex.grad_sync("w_out_attn", g_wa)]
    d_rel = _bias_reduce(jnp.stack(dbt), buckets, hpg)[:, :, 0].T
    d_dsk_h = d_dsk.reshape(n_heads, SSM_HEAD_DIM).sum(axis=1)
    small_full = [d_rel, d_cw, d_cb, unpad_heads(d_dtb), unpad_heads(d_alog), d_dsk_h[None], d_nw,
                  jnp.concatenate([dg0, dg1], axis=0), jnp.concatenate([db0, db1], axis=0)]
    tok_wa = ex.grad_mid("w_in_attn", ex.small_grads(small_full))
    for t in join_toks:
        tok_wa = tok_wa if t is None or tok_wa is None else tok_wa + t
    grad_x = _mm_nt_sharded_k(dpa, wa4, F32, "mm_d_x0", add=du0, add_scale=DEEPNORM_ALPHA, after=tok_wa)
    return loss_local, grad_x[None]


def kernel(x, w_in_attn, w_out_attn, rel_bias, w_in_ssm, conv_w, conv_b, dt_bias, a_log, d_skip, ssm_norm_w, w_out_ssm, ln_g, ln_b, loss_target, m_w_in_attn, m_w_out_attn, m_rel_bias, m_w_in_ssm, m_conv_w, m_conv_b, m_dt_bias, m_a_log, m_d_skip, m_ssm_norm_w, m_w_out_ssm, m_ln_g, m_ln_b, v_w_in_attn, v_w_out_attn, v_rel_bias, v_w_in_ssm, v_conv_w, v_conv_b, v_dt_bias, v_a_log, v_d_skip, v_ssm_norm_w, v_w_out_ssm, v_ln_g, v_ln_b):
    xi, yi, ci = lax.axis_index("x"), lax.axis_index("y"), lax.axis_index("c")
    chip = 2 * xi + yi
    cvec = jnp.reshape(ci, (1,)).astype(jnp.int32)
    kvec = jnp.reshape(chip, (1,)).astype(jnp.int32)

    cw_l, cb_l, nw_l = conv_w[0], conv_b[0], ssm_norm_w[0]
    vec_shapes = [cw_l.shape, cb_l.shape, nw_l.shape]
    vec_all = _all_gather_small(_pack([cw_l, cb_l, nw_l]), False, "gather_vectors")
    l0 = [_cast_to_slot(w_in_attn[0], kvec, "cast_w_in_attn"), _cast_to_slot(w_out_attn[0], kvec, "cast_w_out_attn")]
    ex = _Exchange(kvec, cvec, l0, after=vec_all)
    _, tok0 = ex.l0_begin()
    tok1 = ex.ssm_gather_start([_cast_to_slot(w_in_ssm[0].T, kvec, "cast_w_in_ssm", tok0),
                                _cast_to_slot(w_out_ssm[0], kvec, "cast_w_out_ssm", tok0)])
    parts = [_unpack(vec_all[2 * j], vec_shapes) for j in range(N_CHIPS)]
    conv_w_f = jnp.concatenate([p[0] for p in parts], axis=1)
    conv_b_f = jnp.concatenate([p[1] for p in parts], axis=0)[None]
    norm_w_f = jnp.concatenate([p[2] for p in parts], axis=0)[None]

    loss_local, grad_x = _local_step(
        x[0], loss_target[0], ex, kvec, conv_w_f, conv_b_f, norm_w_f, rel_bias, dt_bias, a_log,
        d_skip, ln_g, ln_b)
    loss = lax.psum(loss_local, ("x", "y", "c"))

    big_w = dict(w_in_attn=(w_in_attn, m_w_in_attn, v_w_in_attn), w_out_attn=(w_out_attn, m_w_out_attn, v_w_out_attn),
                 w_in_ssm=(w_in_ssm, m_w_in_ssm, v_w_in_ssm), w_out_ssm=(w_out_ssm, m_w_out_ssm, v_w_out_ssm))
    big = {}

    def finish(names, after):
        last = after
        for nm in names:
            gf = ex.grad_end(nm, last)
            flip = (lambda t: t.T) if nm == "w_in_ssm" else (lambda t: t)
            w_, m_, v_ = (flip(t[0]) for t in big_w[nm])
            res = _adamw(w_, gf, m_, v_, "adamw_" + nm)
            big[nm] = [flip(t)[None] for t in res]
            last = res[3]
        return last

    last = finish(["w_out_ssm", "w_in_ssm", "w_out_attn"], grad_x)
    ex.grad_sync("w_in_attn", last)
    finish(["w_in_attn"], last)

    s_rel, s_cw, s_cb, s_dtb, s_alog, s_dsk, s_nw, s_lng, s_lnb = ex.small
    cwc, nwc = conv_w.shape[2], ssm_norm_w.shape[1]
    s_cw = lax.dynamic_slice_in_dim(s_cw, chip * cwc, cwc, axis=1)[None]
    s_cb = lax.dynamic_slice_in_dim(s_cb, chip * cwc, cwc, axis=1)
    s_nw = lax.dynamic_slice_in_dim(s_nw, chip * nwc, nwc, axis=1)
    small_names = ["rel_bias", "conv_w", "conv_b", "dt_bias", "a_log", "d_skip", "ssm_norm_w", "ln_g", "ln_b"]
    small_g = [s_rel, s_cw, s_cb, s_dtb, s_alog, s_dsk, s_nw, s_lng, s_lnb]
    small_w = [rel_bias, conv_w, conv_b, dt_bias, a_log, d_skip, ssm_norm_w, ln_g, ln_b]
    small_m = [m_rel_bias, m_conv_w, m_conv_b, m_dt_bias, m_a_log, m_d_skip, m_ssm_norm_w, m_ln_g, m_ln_b]
    small_v = [v_rel_bias, v_conv_w, v_conv_b, v_dt_bias, v_a_log, v_d_skip, v_ssm_norm_w, v_ln_g, v_ln_b]
    shapes = [t.shape for t in small_w]
    res = _adamw(_pack(small_w), _pack(small_g), _pack(small_m), _pack(small_v), "adamw_small")
    small = {nm: [] for nm in small_names}
    for packed in res:
        for nm, t in zip(small_names, _unpack(packed, shapes)):
            small[nm].append(t)

    order = ["w_in_attn", "w_out_attn", "rel_bias", "w_in_ssm", "conv_w", "conv_b", "dt_bias", "a_log",
             "d_skip", "ssm_norm_w", "w_out_ssm", "ln_g", "ln_b"]
    table = {**big, **small}
    outs = [loss, grad_x]
    for kind in range(4):
        outs += [table[nm][kind] for nm in order]
    return tuple(outs)
```
